```python
import math
import jax
import jax.numpy as jnp
from jax import lax
import numpy as np

D_MODEL = 1024
BATCH = 32
SEQ = 256
DEPTH = 1
DEC_BATCH = 2
DEC_SEQ = 1024
PAST_LEN = 256

GRID_W = 64
DN_HEADS = 4
DN_DK = 128
DN_DV = 128
DN_WIDTH = DN_HEADS * DN_DV
DN_CHUNK = 64
SHORT_CONV = 3
HY_WIDTH = D_MODEL // 2
HY_SHORT = 3
HY_EMB = 33
HY_FFN = 64
HY_TARGET = 1e-2
HY_FAST_DECAY = 0.3
HY_SLOW_DECAY = 1.5
N_EXPERTS = 16
EC_FACTOR = 2
EXPERT_FF = D_MODEL
N_MOD = 6
ALPHA = (2 * DEPTH) ** 0.25
BETA = (8 * DEPTH) ** -0.25
LN_EPS = 1e-5
RMS_EPS = 1e-6
QKV_WIDTH = 2 * DN_HEADS * DN_DK + DN_WIDTH
IN_SPLITS = (QKV_WIDTH, DN_WIDTH, 2 * DN_HEADS, 2 * DN_HEADS, 3 * HY_WIDTH, 2 * D_MODEL)
N_IN = QKV_WIDTH + DN_WIDTH + 4 * DN_HEADS + 3 * HY_WIDTH + 2 * D_MODEL

kernel_name = 'hybrid_deltanet_hyena_ec_diffusion_step'


def _split_points(sizes):
    return [int(s) for s in np.cumsum(np.array(sizes))[:-1]]


def layer_norm(x, g=None, b=None):
    xf = x.astype(jnp.float32)
    mu = jnp.mean(xf, -1, keepdims=True)
    var = jnp.mean(jnp.square(xf - mu), -1, keepdims=True)
    y = (xf - mu) * lax.rsqrt(var + LN_EPS)
    if g is not None:
        y = y * g.astype(jnp.float32) + b.astype(jnp.float32)
    return y.astype(x.dtype)


def grid_pos_embed(rows, dim):
    r = jnp.repeat(jnp.arange(rows), GRID_W)
    col = jnp.tile(jnp.arange(GRID_W), rows)
    quarter = dim // 4
    omega = 1.0 / (10000.0 ** (jnp.arange(quarter, dtype=jnp.float32) / quarter))

    def enc(p):
        ang = p.astype(jnp.float32)[:, None] * omega[None, :]
        return jnp.concatenate([jnp.sin(ang), jnp.cos(ang)], -1)

    return jnp.concatenate([enc(r), enc(col)], -1)


def short_conv(x, w, b=None):
    k, ch = w.shape
    pad = k // 2
    y = lax.conv_general_dilated(x, w[:, None, :].astype(x.dtype), window_strides=(1,),
                                 padding=[(pad, pad)], dimension_numbers=('NWC', 'WIO', 'NWC'),
                                 feature_group_count=ch)
    if b is not None:
        y = y + b.astype(x.dtype)
    return y


def l2norm(x):
    return x * lax.rsqrt(jnp.sum(jnp.square(x), -1, keepdims=True) + 1e-6)


def chunk_gated_delta(q, k, v, g, beta, s0):
    b, h, l, dk = q.shape
    dv = v.shape[-1]
    c = DN_CHUNK
    n = l // c
    q = q.reshape(b, h, n, c, dk)
    k = k.reshape(b, h, n, c, dk)
    v = v.reshape(b, h, n, c, dv)
    g = g.reshape(b, h, n, c)
    beta = beta.reshape(b, h, n, c)
    gc = jnp.cumsum(g, -1)
    incl = jnp.tril(jnp.ones((c, c), dtype=bool))
    strict = jnp.tril(jnp.ones((c, c), dtype=bool), -1)
    decay = jnp.exp(jnp.where(incl, gc[..., :, None] - gc[..., None, :], -jnp.inf))
    kb = k * beta[..., None]
    lower = jnp.where(strict, jnp.einsum('bhncd,bhnsd->bhncs', kb, k) * decay, 0.0)
    eye = jnp.eye(c, dtype=jnp.float32)
    rhs = jnp.concatenate([v * beta[..., None], kb * jnp.exp(gc)[..., None]], -1)
    sol = lax.linalg.triangular_solve(eye + lower, rhs, left_side=True, lower=True, unit_diagonal=True)
    u, w = sol[..., :dv], sol[..., dv:]
    a_intra = jnp.einsum('bhncd,bhnsd->bhncs', q, k) * decay

    def step(s, xs):
        qi, ki, ui, wi, gi, ai = xs
        v_new = ui - jnp.einsum('bhck,bhkv->bhcv', wi, s)
        o = (jnp.einsum('bhck,bhkv->bhcv', qi * jnp.exp(gi)[..., None], s)
             + jnp.einsum('bhcs,bhsv->bhcv', ai, v_new))
        gl = gi[..., -1:]
        s = s * jnp.exp(gl)[..., None] + jnp.einsum('bhck,bhcv->bhkv', ki * jnp.exp(gl - gi)[..., None], v_new)
        return s, o

    xs = tuple(jnp.moveaxis(t, 2, 0) for t in (q, k, u, w, gc, a_intra))
    s_fin, o = lax.scan(step, s0, xs)
    o = jnp.moveaxis(o, 0, 2).reshape(b, h, l, dv)
    return o, s_fin


def deltanet_branch(qkv_raw, z, b_raw, a_raw, conv_w, a_log, dt_bias, gnorm_w, s0_f, s0_b):
    bsz, l, _ = qkv_raw.shape
    hk = DN_HEADS * DN_DK
    qkv = jax.nn.silu(short_conv(qkv_raw, conv_w)).astype(jnp.float32)
    q, k, v = jnp.split(qkv, [hk, 2 * hk], -1)

    def heads(t, d):
        return t.reshape(bsz, l, DN_HEADS, d).transpose(0, 2, 1, 3)

    q = l2norm(heads(q, DN_DK)) * (DN_DK ** -0.5)
    k = l2norm(heads(k, DN_DK))
    v = heads(v, DN_DV)

    def per_dir(t):
        return t.astype(jnp.float32).reshape(bsz, l, 2, DN_HEADS).transpose(2, 0, 3, 1)

    beta = jax.nn.sigmoid(per_dir(b_raw))
    g = -jnp.exp(a_log.astype(jnp.float32))[:, None, :, None] * jax.nn.softplus(
        per_dir(a_raw) + dt_bias.astype(jnp.float32)[:, None, :, None])
    o_f, s_f = chunk_gated_delta(q, k, v, g[0], beta[0], s0_f.astype(jnp.float32))

    def rev(t):
        return jnp.flip(t, 2)

    o_b, s_b = chunk_gated_delta(rev(q), rev(k), rev(v), rev(g[1]), rev(beta[1]), s0_b.astype(jnp.float32))
    o = o_f + rev(o_b)
    o = o * lax.rsqrt(jnp.mean(jnp.square(o), -1, keepdims=True) + RMS_EPS) * gnorm_w.astype(jnp.float32)
    o = o * jax.nn.silu(heads(z.astype(jnp.float32), DN_DV))
    return o.transpose(0, 2, 1, 3).reshape(bsz, l, DN_WIDTH).astype(qkv_raw.dtype), s_f, s_b


def hyena_filter(l, w1, b1, w2, b2, w3, freq):
    f32 = jnp.float32
    t = jnp.linspace(0.0, 1.0, l, dtype=f32)[:, None]
    bands = (HY_EMB - 1) // 2
    ang = (2.0 * math.pi * jnp.arange(l, dtype=f32) / l)[:, None] * jnp.linspace(1e-4, bands - 1, bands, dtype=f32)[None, :]
    feats = jnp.concatenate([t, jnp.cos(ang), -jnp.sin(ang)], -1)
    fr = freq.astype(f32)
    hid = jnp.sin(fr * (feats @ w1.astype(f32) + b1.astype(f32)))
    hid = jnp.sin(fr * (hid @ w2.astype(f32) + b2.astype(f32)))
    filt = hid @ w3.astype(f32)
    deltas = jnp.abs(jnp.linspace(math.log(HY_TARGET) / HY_SLOW_DECAY, math.log(HY_TARGET) / HY_FAST_DECAY,
                                  HY_WIDTH, dtype=f32))
    offset = jnp.abs(jnp.arange(l) - l // 2).astype(f32) / (l // 2)
    filt = filt * jnp.exp(-offset[:, None] * deltas[None, :])
    return filt / (jnp.sum(jnp.abs(filt), 0, keepdims=True) + 1e-6)


def hyena_branch(hy, conv_w, conv_b, w1, b1, w2, b2, w3, freq, skip):
    l = hy.shape[1]
    x0, x1, v = jnp.split(short_conv(hy, conv_w, conv_b).astype(jnp.float32), 3, -1)
    filt = hyena_filter(l, w1, b1, w2, b2, w3, freq)
    uu = x1 * v
    n_fft = 2 * l
    spec = jnp.fft.rfft(uu, n=n_fft, axis=1) * jnp.fft.rfft(filt, n=n_fft, axis=0)[None]
    conv = jnp.fft.irfft(spec, n=n_fft, axis=1)[:, l // 2: l // 2 + l]
    y = x0 * (conv + uu * skip.astype(jnp.float32))
    return y.astype(hy.dtype)


def expert_choice_ffn(u, w_router, w_gate, w_up, w_down):
    bsz, l, _ = u.shape
    cap = EC_FACTOR * l // N_EXPERTS
    probs = jax.nn.softmax((u @ w_router).astype(jnp.float32), axis=-1)
    aff, idx = lax.top_k(jnp.swapaxes(probs, 1, 2), cap)
    bidx = jnp.arange(bsz)[:, None, None]
    xs = u[bidx, idx]
    hid = jax.nn.silu(jnp.einsum('becd,edf->becf', xs, w_gate)) * jnp.einsum('becd,edf->becf', xs, w_up)
    ys = jnp.einsum('becf,efd->becd', hid, w_down) * aff[..., None].astype(u.dtype)
    return jnp.zeros_like(u).at[bidx, idx].add(ys)


def trunk_layer(x, cond, s0_f, s0_b, w_mod, b_mod, w_in, conv_a_w, a_log, dt_bias, gnorm_w,
                conv_b_w, conv_b_b, hy_w1, hy_b1, hy_w2, hy_b2, hy_w3, hy_freq, hy_bias,
                w_up_a, w_up_b, w_out, ln1_g, ln1_b, w_router, w_e_gate, w_e_up, w_e_down, ln2_g, ln2_b):
    mod = jax.nn.silu(cond) @ w_mod + b_mod
    sh1, sc1, g1, sh2, sc2, g2 = [m[:, None, :] for m in jnp.split(mod, N_MOD, -1)]
    u = layer_norm(x) * (1.0 + sc1) + sh1
    proj = u @ w_in
    qkv, z, b_raw, a_raw, hy, gates = jnp.split(proj, _split_points(IN_SPLITS), -1)
    o_a, s_f, s_b = deltanet_branch(qkv, z, b_raw, a_raw, conv_a_w, a_log, dt_bias, gnorm_w, s0_f, s0_b)
    o_b = hyena_branch(hy, conv_b_w, conv_b_b, hy_w1, hy_b1, hy_w2, hy_b2, hy_w3, hy_freq, hy_bias)
    gate_a, gate_b = jnp.split(jax.nn.sigmoid(gates), 2, -1)
    mixed = (gate_a * (o_a @ w_up_a) + gate_b * (o_b @ w_up_b)) @ w_out
    x = layer_norm(ALPHA * x + g1 * mixed, ln1_g, ln1_b)
    u = layer_norm(x) * (1.0 + sc2) + sh2
    x = layer_norm(ALPHA * x + g2 * expert_choice_ffn(u, w_router, w_e_gate, w_e_up, w_e_down), ln2_g, ln2_b)
    return x, s_f, s_b


def setup_inputs(seed: int = 0) -> dict:
    key = jax.random.key(seed)
    keys = jax.random.split(key, 40)
    f32 = jnp.float32
    L = DEPTH
    D = D_MODEL

    def nrm(i, shape, scale):
        return scale * jax.random.normal(keys[i], shape, f32)

    a_log = jnp.log(jax.random.uniform(keys[10], (L, 2, DN_HEADS), f32, 1.0, 16.0))
    dt = jnp.exp(jax.random.uniform(keys[11], (L, 2, DN_HEADS), f32, math.log(1e-3), math.log(1e-1)))
    dt_bias = dt + jnp.log(-jnp.expm1(-dt))
    return {
        'x_prompt': nrm(0, (BATCH, SEQ, D), 1.0),
        'x_sample': nrm(1, (DEC_BATCH, DEC_SEQ, D), 1.0),
        'state_delta_fwd': nrm(2, (DEC_BATCH, L, DN_HEADS, DN_DK, DN_DV), 0.3),
        'state_delta_bwd': nrm(3, (DEC_BATCH, L, DN_HEADS, DN_DK, DN_DV), 0.3),
        'c': nrm(4, (DEC_BATCH, D), 1.0),
        'c_ctx': nrm(5, (D,), 1.0),
        'w_mod': nrm(6, (L, D, N_MOD * D), D ** -0.5),
        'b_mod': nrm(7, (L, N_MOD * D), 0.1),
        'w_in': nrm(8, (L, D, N_IN), D ** -0.5),
        'conv_a_w': nrm(9, (L, SHORT_CONV, QKV_WIDTH), SHORT_CONV ** -0.5),
        'a_log': a_log,
        'dt_bias': dt_bias,
        'gnorm_w': 1.0 + nrm(12, (L, DN_DV), 0.02),
        'conv_b_w': nrm(13, (L, HY_SHORT, 3 * HY_WIDTH), HY_SHORT ** -0.5),
        'conv_b_b': nrm(14, (L, 3 * HY_WIDTH), 0.02),
        'hy_w1': nrm(15, (L, HY_EMB, HY_FFN), HY_EMB ** -0.5),
        'hy_b1': nrm(16, (L, HY_FFN), 0.02),
        'hy_w2': nrm(17, (L, HY_FFN, HY_FFN), HY_FFN ** -0.5),
        'hy_b2': nrm(18, (L, HY_FFN), 0.02),
        'hy_w3': nrm(19, (L, HY_FFN, HY_WIDTH), HY_FFN ** -0.5),
        'hy_freq': 1.0 + nrm(20, (L, HY_FFN), 0.1),
        'hy_bias': nrm(21, (L, HY_WIDTH), 1.0),
        'w_up_a': nrm(22, (L, DN_WIDTH, D), BETA * DN_WIDTH ** -0.5),
        'w_up_b': nrm(23, (L, HY_WIDTH, D), BETA * HY_WIDTH ** -0.5),
        'w_out': nrm(24, (L, D, D), BETA * D ** -0.5),
        'ln1_g': 1.0 + nrm(25, (L, D), 0.02),
        'ln1_b': nrm(26, (L, D), 0.02),
        'w_router': nrm(27, (L, D, N_EXPERTS), D ** -0.5),
        'w_e_gate': nrm(28, (L, N_EXPERTS, D, EXPERT_FF), D ** -0.5),
        'w_e_up': nrm(29, (L, N_EXPERTS, D, EXPERT_FF), D ** -0.5),
        'w_e_down': nrm(30, (L, N_EXPERTS, EXPERT_FF, D), BETA * EXPERT_FF ** -0.5),
        'ln2_g': 1.0 + nrm(31, (L, D), 0.02),
        'ln2_b': nrm(32, (L, D), 0.02),
    }


def reference(x_prompt, x_sample, state_delta_fwd, state_delta_bwd, c, c_ctx, w_mod, b_mod, w_in,
              conv_a_w, a_log, dt_bias, gnorm_w, conv_b_w, conv_b_b, hy_w1, hy_b1, hy_w2, hy_b2, hy_w3,
              hy_freq, hy_bias, w_up_a, w_up_b, w_out, ln1_g, ln1_b, w_router, w_e_gate, w_e_up,
              w_e_down, ln2_g, ln2_b):
    def layer_params(l):
        return (w_mod[l], b_mod[l], w_in[l], conv_a_w[l], a_log[l], dt_bias[l], gnorm_w[l],
                conv_b_w[l], conv_b_b[l], hy_w1[l], hy_b1[l], hy_w2[l], hy_b2[l], hy_w3[l],
                hy_freq[l], hy_bias[l], w_up_a[l], w_up_b[l], w_out[l], ln1_g[l], ln1_b[l],
                w_router[l], w_e_gate[l], w_e_up[l], w_e_down[l], ln2_g[l], ln2_b[l])

    n_req = x_prompt.shape[0]
    zero_state = jnp.zeros((n_req, DN_HEADS, DN_DK, DN_DV), jnp.float32)
    h = x_prompt
    fwd_states = []
    bwd_states = []
    for l in range(DEPTH):
        h, s_f, s_b = trunk_layer(h, c_ctx[None, :], zero_state, zero_state, *layer_params(l))
        fwd_states.append(s_f)
        bwd_states.append(s_b)
    y_prompt = h
    new_state_delta_fwd = jnp.stack(fwd_states, axis=1).astype(x_prompt.dtype)
    new_state_delta_bwd = jnp.stack(bwd_states, axis=1).astype(x_prompt.dtype)

    rows = x_sample.shape[1] // GRID_W
    h = x_sample + grid_pos_embed(rows, D_MODEL).astype(x_sample.dtype)[None]
    for l in range(DEPTH):
        h, _, _ = trunk_layer(h, c, state_delta_fwd[:, l], state_delta_bwd[:, l], *layer_params(l))
    y_sample = h
    return (y_prompt, y_sample, new_state_delta_fwd, new_state_delta_bwd)
```

```python
import functools
import math

import jax
import jax.numpy as jnp
import numpy as np
from jax import lax
from jax.experimental import pallas as pl
from jax.experimental.pallas import tpu as pltpu

F32 = jnp.float32
BF16 = jnp.bfloat16
HIGHEST = lax.Precision.HIGHEST

D_MODEL = 1024
DEPTH = 1
GRID_W = 64
DN_HEADS = 4
DN_DK = 128
DN_DV = 128
DN_WIDTH = DN_HEADS * DN_DV
HY_WIDTH = D_MODEL // 2
HY_EMB = 33
HY_FFN = 64
HY_TARGET = 1e-2
HY_FAST_DECAY = 0.3
HY_SLOW_DECAY = 1.5
N_EXPERTS = 16
EC_FACTOR = 2
N_MOD = 6
ALPHA = (2 * DEPTH) ** 0.25
LN_EPS = 1e-5
RMS_EPS = 1e-6
QKV_WIDTH = 2 * DN_HEADS * DN_DK + DN_WIDTH

LANES = 128
TOKEN_TILE = 256
DN_CHUNK = 256
INV_BASE = 16
HY_CH_BLOCK = 256
FF_BLOCK = 512
VMEM_LIMIT = 56 * 1024 * 1024
MIN_NORMAL_F32_BITS = 0x00800000


def _cparams(n_axes):
    return pltpu.CompilerParams(dimension_semantics=("arbitrary",) * n_axes,
                                vmem_limit_bytes=VMEM_LIMIT)


def _mm(a, b):
    return jnp.dot(a.astype(BF16), b.astype(BF16), preferred_element_type=F32)


def _mm_nt(a, b):
    return lax.dot_general(a.astype(BF16), b.astype(BF16), (((1,), (1,)), ((), ())),
                           preferred_element_type=F32)


def _mm_f32(a, b):
    return jnp.dot(a, b, precision=HIGHEST, preferred_element_type=F32)


def _sigmoid(x):
    return 1.0 / (1.0 + jnp.exp(-x))


def _silu(x):
    return x * _sigmoid(x)


def _softplus(x):
    return jnp.maximum(x, 0.0) + jnp.log1p(jnp.exp(-jnp.abs(x)))


def _ln(x):
    mu = jnp.mean(x, axis=-1, keepdims=True)
    xc = x - mu
    var = jnp.mean(xc * xc, axis=-1, keepdims=True)
    return xc * lax.rsqrt(var + LN_EPS)


def _iota(shape, dim):
    return lax.broadcasted_iota(jnp.int32, shape, dim)


def _mod_kernel(c_ref, w_ref, b_ref, o_ref):
    c = c_ref[...]
    o_ref[...] = _mm_f32(_silu(c), w_ref[...]) + b_ref[...]


def _mod_call(cond8, w_mod, b_mod):
    d = D_MODEL
    tn = 512
    return pl.pallas_call(
        _mod_kernel,
        grid=(N_MOD * d // tn,),
        in_specs=[pl.BlockSpec((8, d), lambda j: (0, 0)),
                  pl.BlockSpec((d, tn), lambda j: (0, j)),
                  pl.BlockSpec((1, tn), lambda j: (0, j))],
        out_specs=pl.BlockSpec((8, tn), lambda j: (0, j)),
        out_shape=jax.ShapeDtypeStruct((8, N_MOD * d), F32),
        compiler_params=_cparams(1),
        name="mod",
    )(cond8, w_mod, b_mod.reshape(1, -1))


def _mod_spec(row_of_step):
    return pl.BlockSpec((1, N_MOD, 1, D_MODEL), lambda *idx: (row_of_step(*idx), 0, 0, 0))


def _inproj_kernel(has_pos, *refs):
    if has_pos:
        x_ref, pos_ref, mod_ref, wqkv, wz, wba, why, wg, qkv_o, z_o, ba_o, hy_o, g_o = refs
        x = x_ref[...] + pos_ref[...]
    else:
        x_ref, mod_ref, wqkv, wz, wba, why, wg, qkv_o, z_o, ba_o, hy_o, g_o = refs
        x = x_ref[...]
    sh1 = mod_ref[0, 0]
    sc1 = mod_ref[0, 1]
    u = (_ln(x) * (1.0 + sc1) + sh1).astype(BF16)
    qkv_o[...] = jnp.dot(u, wqkv[...], preferred_element_type=F32).astype(qkv_o.dtype)
    z_o[...] = jnp.dot(u, wz[...], preferred_element_type=F32).astype(z_o.dtype)
    ba_o[...] = jnp.dot(u, wba[...], preferred_element_type=F32)
    hy_o[...] = jnp.dot(u, why[...], preferred_element_type=F32).astype(hy_o.dtype)
    g_o[...] = jnp.dot(u, wg[...], preferred_element_type=F32).astype(g_o.dtype)


def _inproj_call(x2d, pos, mod4, mod_row, weights, act_dt):
    n, d = x2d.shape
    tm = TOKEN_TILE
    wqkv, wz, wba, why, wg = weights
    row = lambda i: (i, 0)
    const = lambda i: (0, 0)
    in_specs = [pl.BlockSpec((tm, d), row)]
    args = [x2d]
    if pos is not None:
        tiles = pos.shape[0] // tm
        in_specs.append(pl.BlockSpec((tm, d), lambda i: (i % tiles, 0)))
        args.append(pos)
    in_specs.append(_mod_spec(mod_row))
    args.append(mod4)
    for w in weights:
        in_specs.append(pl.BlockSpec(w.shape, const, pipeline_mode=pl.Buffered(1)))
        args.append(w)
    widths = (QKV_WIDTH, DN_WIDTH, LANES, 3 * HY_WIDTH, 2 * D_MODEL)
    dts = (act_dt, act_dt, F32, act_dt, act_dt)
    return pl.pallas_call(
        functools.partial(_inproj_kernel, pos is not None),
        grid=(n // tm,),
        in_specs=in_specs,
        out_specs=[pl.BlockSpec((tm, w), row) for w in widths],
        out_shape=[jax.ShapeDtypeStruct((n, w), dt) for w, dt in zip(widths, dts)],
        compiler_params=_cparams(1),
        name="inproj",
    )(*args)


def _inv_unit_tri(lm, ri, ci):
    c = lm.shape[0]
    eye = jnp.where(ri == ci, 1.0, 0.0).astype(F32)
    base_shift = INV_BASE.bit_length() - 1
    nd = jnp.where((ri >> base_shift) == (ci >> base_shift), -lm, 0.0)
    t = eye + nd
    npow = nd
    span = 2
    while span < INV_BASE:
        npow = _mm(npow, npow)
        t = t + _mm(t, npow)
        span *= 2
    s = INV_BASE
    while s < c:
        sh = s.bit_length() - 1
        off = ((ri >> (sh + 1)) == (ci >> (sh + 1))) & ((ri >> sh) != (ci >> sh))
        lo = jnp.where(off, lm, 0.0)
        t = t - _mm(t, _mm(lo, t))
        s *= 2
    return t


def _dn_chunk(q, k, v, gram, qk, gc_col, gc_row, beta, s_prev, lower, ri, ci):
    c = DN_CHUNK
    diff = gc_col - gc_row
    incl = (ri >= ci) if lower else (ri <= ci)
    strict = (ri > ci) if lower else (ri < ci)
    decay = jnp.exp(jnp.where(incl, diff, -jnp.inf))
    lm = jnp.where(strict, beta * gram * decay, 0.0)
    a_intra = qk * decay
    t = _inv_unit_tri(lm, ri, ci)
    e_col = jnp.exp(gc_col)
    rhs = jnp.concatenate([v * beta, k * (beta * e_col)], axis=1)
    sol = _mm(t, rhs)
    u = sol[:, :DN_DV]
    w = sol[:, DN_DV:]
    last = c - 1 if lower else 0
    gl = gc_col[last:last + 1, :]
    ks_t = (k * jnp.exp(gl - gc_col)).T
    if s_prev is None:
        v_new = u
        o = _mm(a_intra, v_new)
        s_new = _mm(ks_t, v_new)
    else:
        v_new = u - _mm(w, s_prev)
        o = _mm(q * e_col, s_prev) + _mm(a_intra, v_new)
        s_new = s_prev * jnp.exp(gl) + _mm(ks_t, v_new)
    return o, s_new


def _deltanet_kernel(seq, zero_init, *refs):
    (q_ref, k_ref, v_ref, cq_ref, ck_ref, cv_ref, z_ref, ba_ref, prm_ref, gn_ref) = refs[:10]
    refs = refs[10:]
    if not zero_init:
        s0f_ref, s0b_ref = refs[:2]
        refs = refs[2:]
    o_ref, sf_ref, sb_ref, q_s, k_s, v_s, g2_s, of_s, ob_s, st_s = refs
    c = DN_CHUNK
    n = seq // c
    h = pl.program_id(1)

    row = _iota((seq, LANES), 0)

    def conv_silu(x_ref, w_ref):
        x = x_ref[...].astype(F32)
        w = w_ref[...]
        xm = jnp.where(row == 0, 0.0, pltpu.roll(x, 1, 0))
        xp = jnp.where(row == seq - 1, 0.0, pltpu.roll(x, seq - 1, 0))
        y = w[0:1, :] * xm + w[1:2, :] * x + w[2:3, :] * xp
        return _silu(y)

    q = conv_silu(q_ref, cq_ref)
    q_s[...] = q * lax.rsqrt(jnp.sum(q * q, axis=-1, keepdims=True) + 1e-6) * (DN_DK ** -0.5)
    k = conv_silu(k_ref, ck_ref)
    k_s[...] = k * lax.rsqrt(jnp.sum(k * k, axis=-1, keepdims=True) + 1e-6)
    v_s[...] = conv_silu(v_ref, cv_ref)

    ba = ba_ref[...]
    lane = _iota((seq, LANES), 1)
    beta_all = _sigmoid(ba)
    g_all = -jnp.exp(prm_ref[0:1, :]) * _softplus(ba + prm_ref[1:2, :])

    def col(x, idx):
        return jnp.sum(jnp.where(lane == idx, x, 0.0), axis=1, keepdims=True)

    beta_f = col(beta_all, h)
    beta_b = col(beta_all, DN_HEADS + h)
    g_f = col(g_all, 2 * DN_HEADS + h)
    g_b = col(g_all, 3 * DN_HEADS + h)
    g2_s[...] = jnp.where(lane == 0, g_f, jnp.where(lane == 1, g_b, jnp.where(
        lane == 2, beta_f, jnp.where(lane == 3, beta_b, 0.0))))

    ri = _iota((c, c), 0)
    ci = _iota((c, c), 1)
    tri_l = jnp.where(ri >= ci, 1.0, 0.0).astype(F32)
    tri_u = jnp.where(ri <= ci, 1.0, 0.0).astype(F32)

    def one_chunk(idx, lower, s_prev):
        start = idx * c
        if not isinstance(start, int):
            start = pl.multiple_of(start, c)
        sl = pl.ds(start, c)
        qc = q_s[sl, :]
        kc = k_s[sl, :]
        vc = v_s[sl, :]
        g2 = g2_s[sl, :]
        gram = _mm_nt(kc, kc)
        qk = _mm_nt(qc, kc)
        cs = _mm_f32(tri_l if lower else tri_u, g2)
        cs_t = cs.T
        j = 0 if lower else 1
        gc_col = cs[:, j:j + 1]
        gc_row = cs_t[j:j + 1, :]
        beta = g2[:, 2 + j:3 + j]
        return _dn_chunk(qc, kc, vc, gram, qk, gc_col, gc_row, beta, s_prev, lower, ri, ci)

    if n == 1:
        s0f = None if zero_init else s0f_ref[0, 0]
        s0b = None if zero_init else s0b_ref[0, 0]
        o_f, s_f = one_chunk(0, True, s0f)
        o_b, s_b = one_chunk(0, False, s0b)
        o = o_f + o_b
        sf_ref[0, 0] = s_f
        sb_ref[0, 0] = s_b
    else:
        if zero_init:
            st_s[0] = jnp.zeros((DN_DK, DN_DV), F32)
            st_s[1] = jnp.zeros((DN_DK, DN_DV), F32)
        else:
            st_s[0] = s0f_ref[0, 0]
            st_s[1] = s0b_ref[0, 0]

        def body(i, carry):
            o_f, s_f = one_chunk(i, True, st_s[0])
            of_s[pl.ds(pl.multiple_of(i * c, c), c), :] = o_f
            st_s[0] = s_f
            ib = n - 1 - i
            o_b, s_b = one_chunk(ib, False, st_s[1])
            ob_s[pl.ds(pl.multiple_of(ib * c, c), c), :] = o_b
            st_s[1] = s_b
            return carry

        lax.fori_loop(0, n, body, 0)
        o = of_s[...] + ob_s[...]
        sf_ref[0, 0] = st_s[0]
        sb_ref[0, 0] = st_s[1]

    o = o * lax.rsqrt(jnp.mean(o * o, axis=-1, keepdims=True) + RMS_EPS) * gn_ref[...]
    o_ref[...] = (o * _silu(z_ref[...].astype(F32))).astype(o_ref.dtype)


def _deltanet_call(qkv, z, ba, conv_w, prm, gn, s0f, s0b, bsz, seq, act_dt):
    h = DN_HEADS
    zero_init = s0f is None
    tok = lambda off: pl.BlockSpec((seq, LANES), lambda b, hh: (b, off + hh))
    cw = lambda off: pl.BlockSpec((3, LANES), lambda b, hh: (0, off + hh))
    st = pl.BlockSpec((1, 1, DN_DK, DN_DV), lambda b, hh: (b, hh, 0, 0))
    in_specs = [tok(0), tok(h), tok(2 * h), cw(0), cw(h), cw(2 * h), tok(0),
                pl.BlockSpec((seq, LANES), lambda b, hh: (b, 0)),
                pl.BlockSpec((8, LANES), lambda b, hh: (0, 0)),
                pl.BlockSpec((1, LANES), lambda b, hh: (0, 0))]
    args = [qkv, qkv, qkv, conv_w, conv_w, conv_w, z, ba, prm, gn]
    if not zero_init:
        in_specs += [st, st]
        args += [s0f, s0b]
    vm = lambda shape: pltpu.VMEM(shape, F32)
    return pl.pallas_call(
        functools.partial(_deltanet_kernel, seq, zero_init),
        grid=(bsz, h),
        in_specs=in_specs,
        out_specs=[tok(0), st, st],
        out_shape=[jax.ShapeDtypeStruct((bsz * seq, DN_WIDTH), act_dt),
                   jax.ShapeDtypeStruct((bsz, h, DN_DK, DN_DV), F32),
                   jax.ShapeDtypeStruct((bsz, h, DN_DK, DN_DV), F32)],
        scratch_shapes=[vm((seq, LANES))] * 6 + [vm((2, DN_DK, DN_DV))],
        compiler_params=_cparams(2),
        name="deltanet",
    )(*args)


def _dft_mats(seq):
    n = 2 * seq
    f = np.arange(seq)[:, None]
    s = np.arange(seq)[None, :]
    ang = 2.0 * np.pi * ((f * s) % n) / n
    fwd = np.concatenate([np.cos(ang), -np.sin(ang)], axis=0)
    fwd[seq, :] = np.cos(np.pi * np.arange(seq))
    t = (np.arange(seq) + seq // 2)[:, None]
    ff = np.arange(seq)[None, :]
    ang2 = 2.0 * np.pi * ((t * ff) % n) / n
    inv_r = 2.0 * np.cos(ang2) / n
    inv_i = -2.0 * np.sin(ang2) / n
    inv_r[:, 0] = 1.0 / n
    inv_i[:, 0] = np.cos(np.pi * t[:, 0]) / n
    inv = np.concatenate([inv_r, inv_i], axis=1)
    return fwd.astype(np.float32), inv.astype(np.float32)


def _filter_feats(seq):
    f32 = jnp.float32
    t = jnp.linspace(0.0, 1.0, seq, dtype=f32)[:, None]
    bands = (HY_EMB - 1) // 2
    ang = (2.0 * math.pi * jnp.arange(seq, dtype=f32) / seq)[:, None] * jnp.linspace(
        1e-4, bands - 1, bands, dtype=f32)[None, :]
    feats = jnp.concatenate([t, jnp.cos(ang), -jnp.sin(ang)], -1)
    deltas = jnp.abs(jnp.linspace(math.log(HY_TARGET) / HY_SLOW_DECAY, math.log(HY_TARGET) / HY_FAST_DECAY,
                                  HY_WIDTH, dtype=f32))
    offset = jnp.abs(jnp.arange(seq) - seq // 2).astype(f32) / (seq // 2)
    window = jnp.exp(-offset[:, None] * deltas[None, :])
    return jnp.pad(feats, ((0, 0), (0, LANES - HY_EMB))), window


def _hfilter_kernel(seq, feats_ref, win_ref, w1_ref, b1_ref, w2_ref, b2_ref, w3_ref, fr_ref, fwd_ref,
                    ha_ref, hb_ref, hd_ref):
    fr = fr_ref[...]
    hid = jnp.sin(fr * (_mm_f32(feats_ref[...], w1_ref[...]) + b1_ref[...]))
    hid = jnp.sin(fr * (_mm_f32(hid, w2_ref[...]) + b2_ref[...]))
    filt = _mm_f32(hid, w3_ref[...]) * win_ref[...]
    filt = filt / (jnp.sum(jnp.abs(filt), axis=0, keepdims=True) + 1e-6)
    spec = _mm_f32(fwd_ref[...], filt)
    h_re = spec[:seq, :]
    h_im = spec[seq:, :]
    first = _iota(h_re.shape, 0) == 0
    ha_ref[...] = h_re
    hb_ref[...] = jnp.where(first, 0.0, h_im)
    hd_ref[...] = jnp.where(first, h_im, h_re)


def _hfilter_call(seq, w1, b1, w2, b2, w3, freq, fwd_f32):
    feats, window = _filter_feats(seq)
    w1p = jnp.pad(w1, ((0, LANES - HY_EMB), (0, 0)))
    full = lambda a: pl.BlockSpec(a.shape, lambda i: (0,) * a.ndim)
    args = [feats, window, w1p, b1.reshape(1, -1), w2, b2.reshape(1, -1), w3, freq.reshape(1, -1), fwd_f32]
    out = jax.ShapeDtypeStruct((seq, HY_WIDTH), F32)
    return pl.pallas_call(
        functools.partial(_hfilter_kernel, seq),
        grid=(1,),
        in_specs=[full(a) for a in args],
        out_specs=[pl.BlockSpec((seq, HY_WIDTH), lambda i: (0, 0))] * 3,
        out_shape=[out, out, out],
        compiler_params=_cparams(1),
        name="hfilter",
    )(*args)


def _hyena_kernel(seq, x0_ref, x1_ref, v_ref, c0_ref, c1_ref, c2_ref, b0_ref, b1_ref, b2_ref,
                  ha_ref, hb_ref, hd_ref, skip_ref, fwd_ref, inv_ref, o_ref):
    row = _iota((seq, HY_CH_BLOCK), 0)

    def conv(x_ref, w_ref, b_ref):
        x = x_ref[...].astype(F32)
        w = w_ref[...]
        xm = jnp.where(row == 0, 0.0, pltpu.roll(x, 1, 0))
        xp = jnp.where(row == seq - 1, 0.0, pltpu.roll(x, seq - 1, 0))
        return w[0:1, :] * xm + w[1:2, :] * x + w[2:3, :] * xp + b_ref[...]

    x0 = conv(x0_ref, c0_ref, b0_ref)
    uu = conv(x1_ref, c1_ref, b1_ref) * conv(v_ref, c2_ref, b2_ref)
    spec = jnp.dot(fwd_ref[...], uu.astype(BF16), preferred_element_type=F32)
    u_re = spec[:seq, :]
    u_im = spec[seq:, :]
    hb = hb_ref[...]
    y_re = u_re * ha_ref[...] - u_im * hb
    y_im = u_re * hb + u_im * hd_ref[...]
    y = jnp.concatenate([y_re, y_im], axis=0).astype(BF16)
    cv = jnp.dot(inv_ref[...], y, preferred_element_type=F32)
    o_ref[...] = (x0 * (cv + uu * skip_ref[...])).astype(o_ref.dtype)


def _hyena_call(hy, conv_w, conv_b, ha, hb, hd, skip, fwd, inv, bsz, seq, act_dt):
    cb = HY_CH_BLOCK
    nblk = HY_WIDTH // cb
    tok = lambda off: pl.BlockSpec((seq, cb), lambda b, j: (b, off + j))
    cw = lambda off: pl.BlockSpec((3, cb), lambda b, j: (0, off + j))
    bias = lambda off: pl.BlockSpec((1, cb), lambda b, j: (0, off + j))
    ch = pl.BlockSpec((seq, cb), lambda b, j: (0, j))
    const = lambda a: pl.BlockSpec(a.shape, lambda b, j: (0, 0), pipeline_mode=pl.Buffered(1))
    conv_b2 = conv_b.reshape(1, -1)
    return pl.pallas_call(
        functools.partial(_hyena_kernel, seq),
        grid=(bsz, nblk),
        in_specs=[tok(0), tok(nblk), tok(2 * nblk), cw(0), cw(nblk), cw(2 * nblk),
                  bias(0), bias(nblk), bias(2 * nblk), ch, ch, ch,
                  pl.BlockSpec((1, cb), lambda b, j: (0, j)), const(fwd), const(inv)],
        out_specs=tok(0),
        out_shape=jax.ShapeDtypeStruct((bsz * seq, HY_WIDTH), act_dt),
        compiler_params=_cparams(2),
        name="hyena",
    )(hy, hy, hy, conv_w, conv_w, conv_w, conv_b2, conv_b2, conv_b2, ha, hb, hd,
      skip.reshape(1, -1), fwd, inv)


def _mix_kernel(has_pos, *refs):
    if has_pos:
        x_ref, pos_ref = refs[:2]
        refs = refs[2:]
        x = x_ref[...] + pos_ref[...]
    else:
        x_ref = refs[0]
        refs = refs[1:]
        x = x_ref[...]
    (oa_ref, ob_ref, g_ref, mod_ref, wua, wub, wout, l1g, l1b, wr, x1_o, u2_o, p_o) = refs
    d = D_MODEL
    gates = g_ref[...].astype(F32)
    ga = _sigmoid(gates[:, :d])
    gb = _sigmoid(gates[:, d:])
    up = (ga * jnp.dot(oa_ref[...].astype(BF16), wua[...], preferred_element_type=F32)
          + gb * jnp.dot(ob_ref[...].astype(BF16), wub[...], preferred_element_type=F32))
    mixed = jnp.dot(up.astype(BF16), wout[...], preferred_element_type=F32)
    g1 = mod_ref[0, 2]
    sh2 = mod_ref[0, 3]
    sc2 = mod_ref[0, 4]
    x1 = _ln(ALPHA * x + g1 * mixed) * l1g[...] + l1b[...]
    x1_o[...] = x1
    u2 = _ln(x1) * (1.0 + sc2) + sh2
    u2_o[...] = u2.astype(u2_o.dtype)
    logits = _mm_f32(u2, wr[...])
    lane = _iota(logits.shape, 1)
    logits = jnp.where(lane < N_EXPERTS, logits, -jnp.inf)
    m = jnp.max(logits, axis=-1, keepdims=True)
    e = jnp.exp(logits - m)
    p_o[...] = e / jnp.sum(e, axis=-1, keepdims=True)


def _mix_call(x2d, pos, o_a, o_b, gates, mod4, mod_row, wua, wub, wout, l1g, l1b, wr):
    n, d = x2d.shape
    tm = TOKEN_TILE
    row = lambda i: (i, 0)
    const = lambda i: (0, 0)
    in_specs = [pl.BlockSpec((tm, d), row)]
    args = [x2d]
    if pos is not None:
        tiles = pos.shape[0] // tm
        in_specs.append(pl.BlockSpec((tm, d), lambda i: (i % tiles, 0)))
        args.append(pos)
    in_specs += [pl.BlockSpec((tm, DN_WIDTH), row), pl.BlockSpec((tm, HY_WIDTH), row),
                 pl.BlockSpec((tm, 2 * d), row), _mod_spec(mod_row)]
    args += [o_a, o_b, gates, mod4]
    for w in (wua, wub, wout, l1g, l1b, wr):
        in_specs.append(pl.BlockSpec(w.shape, const, pipeline_mode=pl.Buffered(1)))
        args.append(w)
    return pl.pallas_call(
        functools.partial(_mix_kernel, pos is not None),
        grid=(n // tm,),
        in_specs=in_specs,
        out_specs=[pl.BlockSpec((tm, d), row), pl.BlockSpec((tm, d), row), pl.BlockSpec((tm, LANES), row)],
        out_shape=[jax.ShapeDtypeStruct((n, d), F32), jax.ShapeDtypeStruct((n, d), BF16),
                   jax.ShapeDtypeStruct((n, LANES), F32)],
        compiler_params=_cparams(1),
        name="mix",
    )(*args)


def _route_kernel(seq, cap, p_ref, u_ref, tri_ref, xs_ref, g_ref, pt_ref):
    e_n = N_EXPERTS
    pt = p_ref[...].T[:e_n, :]

    def search(i, cur):
        cand = cur | (1 << (30 - i))
        cand_f = pltpu.bitcast(cand, F32)
        cnt = jnp.sum(jnp.where(pt >= cand_f, 1.0, 0.0), axis=1, keepdims=True)
        return jnp.where((cnt >= cap) & (cand >= MIN_NORMAL_F32_BITS), cand, cur)

    thr = pltpu.bitcast(lax.fori_loop(0, 31, search, jnp.zeros((e_n, 1), jnp.int32)), F32)
    gt = pt > thr
    eq = pt == thr
    n_gt = jnp.sum(jnp.where(gt, 1.0, 0.0), axis=1, keepdims=True)
    tri = tri_ref[...]
    eq_rank = jnp.dot(jnp.where(eq, 1.0, 0.0).astype(BF16), tri, preferred_element_type=F32)
    sel = gt | (eq & (eq_rank < cap - n_gt))
    pos = jnp.dot(jnp.where(sel, 1.0, 0.0).astype(BF16), tri, preferred_element_type=F32)
    slot = jnp.where(sel, pos, -1.0)
    pt_ref[0] = pt
    u = u_ref[...]
    jrow = _iota((cap, seq), 0).astype(F32)
    for e in range(e_n):
        onehot = jnp.where(slot[e:e + 1, :] == jrow, 1.0, 0.0).astype(BF16)
        g_ref[0, e * cap:(e + 1) * cap, :] = onehot
        xs_ref[e] = jnp.dot(onehot, u, preferred_element_type=F32).astype(xs_ref.dtype)


def _route_call(probs, u2, bsz, seq):
    cap = EC_FACTOR * seq // N_EXPERTS
    tri = jnp.asarray(np.triu(np.ones((seq, seq), np.float32), 1), dtype=BF16)
    return pl.pallas_call(
        functools.partial(_route_kernel, seq, cap),
        grid=(bsz,),
        in_specs=[pl.BlockSpec((seq, LANES), lambda b: (b, 0)),
                  pl.BlockSpec((seq, D_MODEL), lambda b: (b, 0)),
                  pl.BlockSpec((seq, seq), lambda b: (0, 0), pipeline_mode=pl.Buffered(1))],
        out_specs=[pl.BlockSpec((N_EXPERTS, cap, D_MODEL), lambda b: (0, b, 0)),
                   pl.BlockSpec((1, N_EXPERTS * cap, seq), lambda b: (b, 0, 0)),
                   pl.BlockSpec((1, N_EXPERTS, seq), lambda b: (b, 0, 0))],
        out_shape=[jax.ShapeDtypeStruct((N_EXPERTS, bsz * cap, D_MODEL), BF16),
                   jax.ShapeDtypeStruct((bsz, N_EXPERTS * cap, seq), BF16),
                   jax.ShapeDtypeStruct((bsz, N_EXPERTS, seq), F32)],
        compiler_params=_cparams(1),
        name="route",
    )(probs, u2, tri)


def _expert_kernel(xc_ref, xd_ref, wg_ref, wu_ref, wd_ref, yc_ref, yd_ref, accc, accd):
    j = pl.program_id(1)
    wg = wg_ref[0].astype(BF16)
    wu = wu_ref[0].astype(BF16)
    wd = wd_ref[0].astype(BF16)
    for x_ref, y_ref, acc in ((xc_ref, yc_ref, accc), (xd_ref, yd_ref, accd)):
        x = x_ref[0]
        hg = jnp.dot(x, wg, preferred_element_type=F32)
        hu = jnp.dot(x, wu, preferred_element_type=F32)
        part = jnp.dot((_silu(hg) * hu).astype(BF16), wd, preferred_element_type=F32)

        @pl.when(j == 0)
        def _():
            acc[...] = part

        @pl.when(j > 0)
        def _():
            acc[...] += part

        @pl.when(j == pl.num_programs(1) - 1)
        def _():
            y_ref[0] = acc[...].astype(y_ref.dtype)


def _expert_call(xs_c, xs_d, w_gate, w_up, w_down):
    e_n, rc, d = xs_c.shape
    rd = xs_d.shape[1]
    ff = w_gate.shape[2]
    fb = FF_BLOCK
    return pl.pallas_call(
        _expert_kernel,
        grid=(e_n, ff // fb),
        in_specs=[pl.BlockSpec((1, rc, d), lambda e, j: (e, 0, 0)),
                  pl.BlockSpec((1, rd, d), lambda e, j: (e, 0, 0)),
                  pl.BlockSpec((1, d, fb), lambda e, j: (e, 0, j)),
                  pl.BlockSpec((1, d, fb), lambda e, j: (e, 0, j)),
                  pl.BlockSpec((1, fb, d), lambda e, j: (e, j, 0))],
        out_specs=[pl.BlockSpec((1, rc, d), lambda e, j: (e, 0, 0)),
                   pl.BlockSpec((1, rd, d), lambda e, j: (e, 0, 0))],
        out_shape=[jax.ShapeDtypeStruct((e_n, rc, d), BF16), jax.ShapeDtypeStruct((e_n, rd, d), BF16)],
        scratch_shapes=[pltpu.VMEM((rc, d), F32), pltpu.VMEM((rd, d), F32)],
        compiler_params=_cparams(2),
        name="experts",
    )(xs_c, xs_d, w_gate, w_up, w_down)


def _final_kernel(seq, cap, x1_ref, y_ref, g_ref, pt_ref, mod_ref, l2g, l2b, o_ref, yw_s):
    pt = pt_ref[0]
    for e in range(N_EXPERTS):
        onehot = g_ref[0, e * cap:(e + 1) * cap, :].astype(F32)
        aff = jnp.sum(onehot * pt[e:e + 1, :], axis=1, keepdims=True)
        yw_s[e * cap:(e + 1) * cap, :] = (y_ref[e].astype(F32) * aff).astype(BF16)
    ffn = lax.dot_general(g_ref[0], yw_s[...], (((0,), (0,)), ((), ())), preferred_element_type=F32)
    g2 = mod_ref[0, 5]
    o_ref[...] = _ln(ALPHA * x1_ref[...] + g2 * ffn) * l2g[...] + l2b[...]


def _final_call(x1, ys, gmat, pt, mod4, mod_row, l2g, l2b, bsz, seq):
    cap = EC_FACTOR * seq // N_EXPERTS
    d = D_MODEL
    return pl.pallas_call(
        functools.partial(_final_kernel, seq, cap),
        grid=(bsz,),
        in_specs=[pl.BlockSpec((seq, d), lambda b: (b, 0)),
                  pl.BlockSpec((N_EXPERTS, cap, d), lambda b: (0, b, 0)),
                  pl.BlockSpec((1, N_EXPERTS * cap, seq), lambda b: (b, 0, 0)),
                  pl.BlockSpec((1, N_EXPERTS, seq), lambda b: (b, 0, 0)),
                  _mod_spec(mod_row),
                  pl.BlockSpec((1, d), lambda b: (0, 0)),
                  pl.BlockSpec((1, d), lambda b: (0, 0))],
        out_specs=pl.BlockSpec((seq, d), lambda b: (b, 0)),
        out_shape=jax.ShapeDtypeStruct((bsz * seq, d), F32),
        scratch_shapes=[pltpu.VMEM((N_EXPERTS * cap, d), BF16)],
        compiler_params=_cparams(1),
        name="final",
    )(x1, ys, gmat, pt, mod4, l2g, l2b)


def _grid_pos_embed(rows, dim):
    r = jnp.repeat(jnp.arange(rows), GRID_W)
    col = jnp.tile(jnp.arange(GRID_W), rows)
    quarter = dim // 4
    omega = 1.0 / (10000.0 ** (jnp.arange(quarter, dtype=jnp.float32) / quarter))

    def enc(p):
        ang = p.astype(jnp.float32)[:, None] * omega[None, :]
        return jnp.concatenate([jnp.sin(ang), jnp.cos(ang)], -1)

    return jnp.concatenate([enc(r), enc(col)], -1)


ACT_DT = F32


def kernel(x_prompt, x_sample, state_delta_fwd, state_delta_bwd, c, c_ctx, w_mod, b_mod, w_in, conv_a_w, a_log, dt_bias, gnorm_w, conv_b_w, conv_b_b, hy_w1, hy_b1, hy_w2, hy_b2, hy_w3, hy_freq, hy_bias, w_up_a, w_up_b, w_out, ln1_g, ln1_b, w_router, w_e_gate, w_e_up, w_e_down, ln2_g, ln2_b):
    d = D_MODEL
    n_ctx, l_ctx, _ = x_prompt.shape
    n_dec, l_dec, _ = x_sample.shape
    lyr = 0

    cond8 = jnp.zeros((8, d), F32).at[0].set(c_ctx).at[1:1 + n_dec].set(c)
    mod4 = _mod_call(cond8, w_mod[lyr], b_mod[lyr]).reshape(8, N_MOD, 1, d)

    w = w_in[lyr].astype(BF16)
    o_z = QKV_WIDTH
    o_b = o_z + DN_WIDTH
    o_hy = o_b + 4 * DN_HEADS
    o_g = o_hy + 3 * HY_WIDTH
    w_ba = jnp.pad(w[:, o_b:o_hy], ((0, 0), (0, LANES - 4 * DN_HEADS)))
    in_w = (w[:, :o_z], w[:, o_z:o_b], w_ba, w[:, o_hy:o_g], w[:, o_g:])

    prm = jnp.zeros((8, LANES), F32)
    prm = prm.at[0, 2 * DN_HEADS:4 * DN_HEADS].set(a_log[lyr].reshape(-1))
    prm = prm.at[1, 2 * DN_HEADS:4 * DN_HEADS].set(dt_bias[lyr].reshape(-1))
    gn = gnorm_w[lyr].reshape(1, -1)

    wua = w_up_a[lyr].astype(BF16)
    wub = w_up_b[lyr].astype(BF16)
    wout = w_out[lyr].astype(BF16)
    wr = jnp.pad(w_router[lyr], ((0, 0), (0, LANES - N_EXPERTS)))
    l1g, l1b = ln1_g[lyr].reshape(1, -1), ln1_b[lyr].reshape(1, -1)
    l2g, l2b = ln2_g[lyr].reshape(1, -1), ln2_b[lyr].reshape(1, -1)

    def front(x2d, pos, bsz, seq, mod_row, s0f, s0b):
        qkv, z, ba, hy, gates = _inproj_call(x2d, pos, mod4, mod_row, in_w, ACT_DT)
        o_a, s_f, s_b = _deltanet_call(qkv, z, ba, conv_a_w[lyr], prm, gn, s0f, s0b, bsz, seq, ACT_DT)
        fwd, inv = (jnp.asarray(m) for m in _dft_mats(seq))
        ha, hb, hd = _hfilter_call(seq, hy_w1[lyr], hy_b1[lyr], hy_w2[lyr], hy_b2[lyr], hy_w3[lyr],
                                   hy_freq[lyr], fwd)
        o_h = _hyena_call(hy, conv_b_w[lyr], conv_b_b[lyr], ha, hb, hd, hy_bias[lyr],
                          fwd.astype(BF16), inv.astype(BF16), bsz, seq, ACT_DT)
        x1, u2, probs = _mix_call(x2d, pos, o_a, o_h, gates, mod4, mod_row, wua, wub, wout, l1g, l1b, wr)
        xs, gmat, pt = _route_call(probs, u2, bsz, seq)
        return x1, xs, gmat, pt, s_f, s_b

    tiles_dec = l_dec // TOKEN_TILE
    row_ctx = lambda *idx: 0
    row_dec_tile = lambda i: 1 + i // tiles_dec
    row_dec_req = lambda b: 1 + b

    xc = x_prompt.reshape(n_ctx * l_ctx, d)
    xd = x_sample.reshape(n_dec * l_dec, d)
    pos = _grid_pos_embed(l_dec // GRID_W, d).astype(F32)

    x1c, xsc, gc, ptc, s_f, s_b = front(xc, None, n_ctx, l_ctx, row_ctx, None, None)
    x1d, xsd, gd, ptd, _, _ = front(xd, pos, n_dec, l_dec, row_dec_tile,
                                    state_delta_fwd[:, lyr], state_delta_bwd[:, lyr])
    yc, yd = _expert_call(xsc, xsd, w_e_gate[lyr], w_e_up[lyr], w_e_down[lyr])
    y_prompt = _final_call(x1c, yc, gc, ptc, mod4, row_ctx, l2g, l2b, n_ctx, l_ctx)
    y_sample = _final_call(x1d, yd, gd, ptd, mod4, row_dec_req, l2g, l2b, n_dec, l_dec)

    return (y_prompt.reshape(n_ctx, l_ctx, d), y_sample.reshape(n_dec, l_dec, d),
            s_f[:, None], s_b[:, None])
```

```python
import functools
import math

import jax
import jax.numpy as jnp
import numpy as np
from jax import lax
from jax.experimental import pallas as pl
from jax.experimental.pallas import tpu as pltpu

F32 = jnp.float32
BF16 = jnp.bfloat16
HIGHEST = lax.Precision.HIGHEST

D_MODEL = 1024
DEPTH = 1
GRID_W = 64
DN_HEADS = 4
DN_DK = 128
DN_DV = 128
DN_WIDTH = DN_HEADS * DN_DV
HY_WIDTH = D_MODEL // 2
HY_EMB = 33
HY_FFN = 64
HY_TARGET = 1e-2
HY_FAST_DECAY = 0.3
HY_SLOW_DECAY = 1.5
N_EXPERTS = 16
EC_FACTOR = 2
N_MOD = 6
ALPHA = (2 * DEPTH) ** 0.25
LN_EPS = 1e-5
RMS_EPS = 1e-6
QKV_WIDTH = 2 * DN_HEADS * DN_DK + DN_WIDTH

LANES = 128
TOKEN_TILE = 256
DN_CHUNK = 256
INV_BASE = 16
HY_CH_BLOCK = 256
FF_BLOCK = 512
VMEM_LIMIT = 56 * 1024 * 1024
MIN_NORMAL_F32_BITS = 0x00800000
RANK_COUNT_MAX_SEQ = 256


def _cparams(n_axes):
    return pltpu.CompilerParams(dimension_semantics=("arbitrary",) * n_axes,
                                vmem_limit_bytes=VMEM_LIMIT)


def _mm(a, b):
    return jnp.dot(a.astype(BF16), b.astype(BF16), preferred_element_type=F32)


def _mm_nt(a, b):
    return lax.dot_general(a.astype(BF16), b.astype(BF16), (((1,), (1,)), ((), ())),
                           preferred_element_type=F32)


def _bmm(a, b):
    return jnp.einsum("bmk,bkn->bmn", a.astype(BF16), b.astype(BF16), preferred_element_type=F32)


def _bmm_nt(a, b):
    return jnp.einsum("bmk,bnk->bmn", a.astype(BF16), b.astype(BF16), preferred_element_type=F32)


def _bmm_tn(a, b):
    return jnp.einsum("bkm,bkn->bmn", a.astype(BF16), b.astype(BF16), preferred_element_type=F32)


def _mm_f32(a, b):
    return jnp.dot(a, b, precision=HIGHEST, preferred_element_type=F32)


def _sigmoid(x):
    return 1.0 / (1.0 + jnp.exp(-x))


def _silu(x):
    return x * _sigmoid(x)


def _softplus(x):
    return jnp.maximum(x, 0.0) + jnp.log1p(jnp.exp(-jnp.abs(x)))


def _ln(x):
    mu = jnp.mean(x, axis=-1, keepdims=True)
    xc = x - mu
    var = jnp.mean(xc * xc, axis=-1, keepdims=True)
    return xc * lax.rsqrt(var + LN_EPS)


def _iota(shape, dim):
    return lax.broadcasted_iota(jnp.int32, shape, dim)


def _mod_kernel(c_ref, w_ref, b_ref, o_ref):
    c = c_ref[...]
    o_ref[...] = _mm_f32(_silu(c), w_ref[...]) + b_ref[...]


def _mod_call(cond8, w_mod, b_mod):
    d = D_MODEL
    tn = 512
    return pl.pallas_call(
        _mod_kernel,
        grid=(N_MOD * d // tn,),
        in_specs=[pl.BlockSpec((8, d), lambda j: (0, 0)),
                  pl.BlockSpec((d, tn), lambda j: (0, j)),
                  pl.BlockSpec((1, tn), lambda j: (0, j))],
        out_specs=pl.BlockSpec((8, tn), lambda j: (0, j)),
        out_shape=jax.ShapeDtypeStruct((8, N_MOD * d), F32),
        compiler_params=_cparams(1),
        name="mod",
    )(cond8, w_mod, b_mod.reshape(1, -1))


def _mod_spec(row_of_step):
    return pl.BlockSpec((1, N_MOD, 1, D_MODEL), lambda *idx: (row_of_step(*idx), 0, 0, 0))


def _inproj_kernel(has_pos, *refs):
    if has_pos:
        x_ref, pos_ref, mod_ref, wqkv, wz, wba, why, wg, qkv_o, z_o, ba_o, hy_o, g_o = refs
        x = x_ref[...] + pos_ref[...]
    else:
        x_ref, mod_ref, wqkv, wz, wba, why, wg, qkv_o, z_o, ba_o, hy_o, g_o = refs
        x = x_ref[...]
    sh1 = mod_ref[0, 0]
    sc1 = mod_ref[0, 1]
    u = (_ln(x) * (1.0 + sc1) + sh1).astype(BF16)
    qkv_o[...] = jnp.dot(u, wqkv[...], preferred_element_type=F32).astype(qkv_o.dtype)
    z_o[...] = jnp.dot(u, wz[...], preferred_element_type=F32).astype(z_o.dtype)
    ba_o[...] = jnp.dot(u, wba[...], preferred_element_type=F32)
    hy_o[...] = jnp.dot(u, why[...], preferred_element_type=F32).astype(hy_o.dtype)
    g_o[...] = jnp.dot(u, wg[...], preferred_element_type=F32).astype(g_o.dtype)


def _inproj_call(x2d, pos, mod4, mod_row, weights, act_dt):
    n, d = x2d.shape
    tm = TOKEN_TILE
    wqkv, wz, wba, why, wg = weights
    row = lambda i: (i, 0)
    const = lambda i: (0, 0)
    in_specs = [pl.BlockSpec((tm, d), row)]
    args = [x2d]
    if pos is not None:
        tiles = pos.shape[0] // tm
        in_specs.append(pl.BlockSpec((tm, d), lambda i: (i % tiles, 0)))
        args.append(pos)
    in_specs.append(_mod_spec(mod_row))
    args.append(mod4)
    for w in weights:
        in_specs.append(pl.BlockSpec(w.shape, const, pipeline_mode=pl.Buffered(1)))
        args.append(w)
    widths = (QKV_WIDTH, DN_WIDTH, LANES, 3 * HY_WIDTH, 2 * D_MODEL)
    dts = (act_dt, act_dt, F32, act_dt, act_dt)
    return pl.pallas_call(
        functools.partial(_inproj_kernel, pos is not None),
        grid=(n // tm,),
        in_specs=in_specs,
        out_specs=[pl.BlockSpec((tm, w), row) for w in widths],
        out_shape=[jax.ShapeDtypeStruct((n, w), dt) for w, dt in zip(widths, dts)],
        compiler_params=_cparams(1),
        name="inproj",
    )(*args)


def _inv_unit_tri(lm, ri, ci):
    c = lm.shape[-1]
    eye = jnp.where(ri == ci, 1.0, 0.0).astype(F32)
    base_shift = INV_BASE.bit_length() - 1
    nd = jnp.where((ri >> base_shift) == (ci >> base_shift), -lm, 0.0)
    t = eye + nd
    npow = nd
    span = 2
    while span < INV_BASE:
        npow = _bmm(npow, npow)
        t = t + _bmm(t, npow)
        span *= 2
    s = INV_BASE
    while s < c:
        sh = s.bit_length() - 1
        off = ((ri >> (sh + 1)) == (ci >> (sh + 1))) & ((ri >> sh) != (ci >> sh))
        lo = jnp.where(off, lm, 0.0)
        t = t - _bmm(t, _bmm(lo, t))
        s *= 2
    return t


def _dn_chunk(q, k, v, gc_col, gc_row, beta, s_prev, ri, ci):
    c = DN_CHUNK
    nh = DN_HEADS
    gram = _bmm_nt(k, k)
    qk = _bmm_nt(q, k)
    diff = gc_col - gc_row
    incl = jnp.concatenate([jnp.broadcast_to(ri >= ci, (nh, c, c)), jnp.broadcast_to(ri <= ci, (nh, c, c))], 0)
    strict = incl & (ri != ci)
    decay = jnp.exp(jnp.where(incl, diff, -jnp.inf))
    lm = jnp.where(strict, beta * gram * decay, 0.0)
    a_intra = qk * decay
    t = _inv_unit_tri(lm, ri, ci)
    e_col = jnp.exp(gc_col)
    rhs = jnp.concatenate([v * beta, k * (beta * e_col)], axis=2)
    sol = _bmm(t, rhs)
    u = sol[:, :, :DN_DV]
    w = sol[:, :, DN_DV:]
    gl = jnp.concatenate([gc_col[:nh, c - 1:c, :], gc_col[nh:, 0:1, :]], 0)
    ks = k * jnp.exp(gl - gc_col)
    if s_prev is None:
        v_new = u
        o = _bmm(a_intra, v_new)
        s_new = _bmm_tn(ks, v_new)
    else:
        v_new = u - _bmm(w, s_prev)
        o = _bmm(q * e_col, s_prev) + _bmm(a_intra, v_new)
        s_new = s_prev * jnp.exp(gl) + _bmm_tn(ks, v_new)
    return o, s_new


def _deltanet_kernel(seq, zero_init, *refs):
    qkv_ref, cw_ref, z_ref, ba_ref, prm_ref, gn_ref = refs[:6]
    refs = refs[6:]
    if not zero_init:
        s0f_ref, s0b_ref = refs[:2]
        refs = refs[2:]
    o_ref, sf_ref, sb_ref, q_s, k_s, v_s, g_s, b_s, o_s, st_s = refs
    c = DN_CHUNK
    n = seq // c
    nh = DN_HEADS

    row = _iota((seq, LANES), 0)
    first_row = row == 0
    last_row = row == seq - 1

    def conv_silu(j):
        cols = slice(j * LANES, (j + 1) * LANES)
        x = qkv_ref[:, cols].astype(F32)
        w = cw_ref[:, cols]
        xm = jnp.where(first_row, 0.0, pltpu.roll(x, 1, 0))
        xp = jnp.where(last_row, 0.0, pltpu.roll(x, seq - 1, 0))
        return _silu(w[0:1, :] * xm + w[1:2, :] * x + w[2:3, :] * xp)

    for h in range(nh):
        q = conv_silu(h)
        q_s[h] = q * lax.rsqrt(jnp.sum(q * q, axis=-1, keepdims=True) + 1e-6) * (DN_DK ** -0.5)
        k = conv_silu(nh + h)
        k_s[h] = k * lax.rsqrt(jnp.sum(k * k, axis=-1, keepdims=True) + 1e-6)
        v_s[h] = conv_silu(2 * nh + h)

    ba = ba_ref[...]
    b_s[...] = _sigmoid(ba)
    g_s[...] = -jnp.exp(prm_ref[0:1, :]) * _softplus(ba + prm_ref[1:2, :])

    ri = _iota((c, c), 0)
    ci = _iota((c, c), 1)
    tri_l = jnp.where(ri >= ci, 1.0, 0.0).astype(F32)
    tri_u = jnp.where(ri <= ci, 1.0, 0.0).astype(F32)

    def chunk_pair(idx_f, idx_b, s_prev):
        def ds(idx):
            start = idx * c
            return pl.ds(start if isinstance(start, int) else pl.multiple_of(start, c), c)

        sl_f, sl_b = ds(idx_f), ds(idx_b)
        cs_f = _mm_f32(tri_l, g_s[sl_f, :])
        cs_b = _mm_f32(tri_u, g_s[sl_b, :])
        cst_f = cs_f.T
        cst_b = cs_b.T
        b_f = b_s[sl_f, :]
        b_b = b_s[sl_b, :]
        heads = range(nh)
        gc_col = jnp.stack([cs_f[:, 2 * nh + h:2 * nh + h + 1] for h in heads]
                           + [cs_b[:, 3 * nh + h:3 * nh + h + 1] for h in heads], 0)
        gc_row = jnp.stack([cst_f[2 * nh + h:2 * nh + h + 1, :] for h in heads]
                           + [cst_b[3 * nh + h:3 * nh + h + 1, :] for h in heads], 0)
        beta = jnp.stack([b_f[:, h:h + 1] for h in heads] + [b_b[:, nh + h:nh + h + 1] for h in heads], 0)
        both = lambda ref: jnp.concatenate([ref[:, sl_f, :], ref[:, sl_b, :]], 0)
        o, s_new = _dn_chunk(both(q_s), both(k_s), both(v_s), gc_col, gc_row, beta, s_prev, ri, ci)
        o_s[:nh, sl_f, :] = o[:nh]
        o_s[nh:, sl_b, :] = o[nh:]
        return s_new

    if n == 1:
        s0 = None if zero_init else jnp.concatenate([s0f_ref[0], s0b_ref[0]], 0)
        s_fin = chunk_pair(0, 0, s0)
    else:
        if zero_init:
            st_s[...] = jnp.zeros(st_s.shape, F32)
        else:
            st_s[:nh] = s0f_ref[0]
            st_s[nh:] = s0b_ref[0]

        def body(i, carry):
            st_s[...] = chunk_pair(i, n - 1 - i, st_s[...])
            return carry

        lax.fori_loop(0, n, body, 0)
        s_fin = st_s[...]
    sf_ref[0] = s_fin[:nh]
    sb_ref[0] = s_fin[nh:]

    gn = gn_ref[...]
    for h in range(nh):
        cols = slice(h * LANES, (h + 1) * LANES)
        o = o_s[h] + o_s[nh + h]
        o = o * lax.rsqrt(jnp.mean(o * o, axis=-1, keepdims=True) + RMS_EPS) * gn
        o_ref[:, cols] = (o * _silu(z_ref[:, cols].astype(F32))).astype(o_ref.dtype)


def _deltanet_call(qkv, z, ba, conv_w, prm, gn, s0f, s0b, bsz, seq, act_dt):
    nh = DN_HEADS
    zero_init = s0f is None
    st = pl.BlockSpec((1, nh, DN_DK, DN_DV), lambda b: (b, 0, 0, 0))
    in_specs = [pl.BlockSpec((seq, QKV_WIDTH), lambda b: (b, 0)),
                pl.BlockSpec((3, QKV_WIDTH), lambda b: (0, 0)),
                pl.BlockSpec((seq, DN_WIDTH), lambda b: (b, 0)),
                pl.BlockSpec((seq, LANES), lambda b: (b, 0)),
                pl.BlockSpec((8, LANES), lambda b: (0, 0)),
                pl.BlockSpec((1, LANES), lambda b: (0, 0))]
    args = [qkv, conv_w, z, ba, prm, gn]
    if not zero_init:
        in_specs += [st, st]
        args += [s0f, s0b]
    vm = lambda shape: pltpu.VMEM(shape, F32)
    per_head = vm((nh, seq, LANES))
    return pl.pallas_call(
        functools.partial(_deltanet_kernel, seq, zero_init),
        grid=(bsz,),
        in_specs=in_specs,
        out_specs=[pl.BlockSpec((seq, DN_WIDTH), lambda b: (b, 0)), st, st],
        out_shape=[jax.ShapeDtypeStruct((bsz * seq, DN_WIDTH), act_dt),
                   jax.ShapeDtypeStruct((bsz, nh, DN_DK, DN_DV), F32),
                   jax.ShapeDtypeStruct((bsz, nh, DN_DK, DN_DV), F32)],
        scratch_shapes=[per_head, per_head, per_head, vm((seq, LANES)), vm((seq, LANES)),
                        vm((2 * nh, seq, LANES)), vm((2 * nh, DN_DK, DN_DV))],
        compiler_params=_cparams(1),
        name="deltanet",
    )(*args)


def _dft_mats(seq):
    n = 2 * seq
    f = np.arange(seq)[:, None]
    s = np.arange(seq)[None, :]
    ang = 2.0 * np.pi * ((f * s) % n) / n
    fwd = np.concatenate([np.cos(ang), -np.sin(ang)], axis=0)
    fwd[seq, :] = np.cos(np.pi * np.arange(seq))
    t = (np.arange(seq) + seq // 2)[:, None]
    ff = np.arange(seq)[None, :]
    ang2 = 2.0 * np.pi * ((t * ff) % n) / n
    inv_r = 2.0 * np.cos(ang2) / n
    inv_i = -2.0 * np.sin(ang2) / n
    inv_r[:, 0] = 1.0 / n
    inv_i[:, 0] = np.cos(np.pi * t[:, 0]) / n
    inv = np.concatenate([inv_r, inv_i], axis=1)
    return fwd.astype(np.float32), inv.astype(np.float32)


def _filter_feats(seq):
    f32 = jnp.float32
    t = jnp.linspace(0.0, 1.0, seq, dtype=f32)[:, None]
    bands = (HY_EMB - 1) // 2
    ang = (2.0 * math.pi * jnp.arange(seq, dtype=f32) / seq)[:, None] * jnp.linspace(
        1e-4, bands - 1, bands, dtype=f32)[None, :]
    feats = jnp.concatenate([t, jnp.cos(ang), -jnp.sin(ang)], -1)
    deltas = jnp.abs(jnp.linspace(math.log(HY_TARGET) / HY_SLOW_DECAY, math.log(HY_TARGET) / HY_FAST_DECAY,
                                  HY_WIDTH, dtype=f32))
    offset = jnp.abs(jnp.arange(seq) - seq // 2).astype(f32) / (seq // 2)
    window = jnp.exp(-offset[:, None] * deltas[None, :])
    return jnp.pad(feats, ((0, 0), (0, LANES - HY_EMB))), window


def _hfilter_kernel(seq, feats_ref, win_ref, w1_ref, b1_ref, w2_ref, b2_ref, w3_ref, fr_ref, fwd_ref,
                    ha_ref, hb_ref, hd_ref):
    fr = fr_ref[...]
    hid = jnp.sin(fr * (_mm_f32(feats_ref[...], w1_ref[...]) + b1_ref[...]))
    hid = jnp.sin(fr * (_mm_f32(hid, w2_ref[...]) + b2_ref[...]))
    filt = _mm_f32(hid, w3_ref[...]) * win_ref[...]
    filt = filt / (jnp.sum(jnp.abs(filt), axis=0, keepdims=True) + 1e-6)
    spec = _mm_f32(fwd_ref[...], filt)
    h_re = spec[:seq, :]
    h_im = spec[seq:, :]
    first = _iota(h_re.shape, 0) == 0
    ha_ref[...] = h_re
    hb_ref[...] = jnp.where(first, 0.0, h_im)
    hd_ref[...] = jnp.where(first, h_im, h_re)


def _hfilter_call(seq, w1, b1, w2, b2, w3, freq, fwd_f32):
    feats, window = _filter_feats(seq)
    w1p = jnp.pad(w1, ((0, LANES - HY_EMB), (0, 0)))
    full = lambda a: pl.BlockSpec(a.shape, lambda i: (0,) * a.ndim)
    args = [feats, window, w1p, b1.reshape(1, -1), w2, b2.reshape(1, -1), w3, freq.reshape(1, -1), fwd_f32]
    out = jax.ShapeDtypeStruct((seq, HY_WIDTH), F32)
    return pl.pallas_call(
        functools.partial(_hfilter_kernel, seq),
        grid=(1,),
        in_specs=[full(a) for a in args],
        out_specs=[pl.BlockSpec((seq, HY_WIDTH), lambda i: (0, 0))] * 3,
        out_shape=[out, out, out],
        compiler_params=_cparams(1),
        name="hfilter",
    )(*args)


def _hyena_kernel(seq, x0_ref, x1_ref, v_ref, c0_ref, c1_ref, c2_ref, b0_ref, b1_ref, b2_ref,
                  ha_ref, hb_ref, hd_ref, skip_ref, fwd_ref, inv_ref, o_ref):
    row = _iota((seq, HY_CH_BLOCK), 0)

    def conv(x_ref, w_ref, b_ref):
        x = x_ref[...].astype(F32)
        w = w_ref[...]
        xm = jnp.where(row == 0, 0.0, pltpu.roll(x, 1, 0))
        xp = jnp.where(row == seq - 1, 0.0, pltpu.roll(x, seq - 1, 0))
        return w[0:1, :] * xm + w[1:2, :] * x + w[2:3, :] * xp + b_ref[...]

    x0 = conv(x0_ref, c0_ref, b0_ref)
    uu = conv(x1_ref, c1_ref, b1_ref) * conv(v_ref, c2_ref, b2_ref)
    spec = jnp.dot(fwd_ref[...], uu.astype(BF16), preferred_element_type=F32)
    u_re = spec[:seq, :]
    u_im = spec[seq:, :]
    hb = hb_ref[...]
    y_re = u_re * ha_ref[...] - u_im * hb
    y_im = u_re * hb + u_im * hd_ref[...]
    y = jnp.concatenate([y_re, y_im], axis=0).astype(BF16)
    cv = jnp.dot(inv_ref[...], y, preferred_element_type=F32)
    o_ref[...] = (x0 * (cv + uu * skip_ref[...])).astype(o_ref.dtype)


def _hyena_call(hy, conv_w, conv_b, ha, hb, hd, skip, fwd, inv, bsz, seq, act_dt):
    cb = HY_CH_BLOCK
    nblk = HY_WIDTH // cb
    tok = lambda off: pl.BlockSpec((seq, cb), lambda b, j: (b, off + j))
    cw = lambda off: pl.BlockSpec((3, cb), lambda b, j: (0, off + j))
    bias = lambda off: pl.BlockSpec((1, cb), lambda b, j: (0, off + j))
    ch = pl.BlockSpec((seq, cb), lambda b, j: (0, j))
    const = lambda a: pl.BlockSpec(a.shape, lambda b, j: (0, 0), pipeline_mode=pl.Buffered(1))
    conv_b2 = conv_b.reshape(1, -1)
    return pl.pallas_call(
        functools.partial(_hyena_kernel, seq),
        grid=(bsz, nblk),
        in_specs=[tok(0), tok(nblk), tok(2 * nblk), cw(0), cw(nblk), cw(2 * nblk),
                  bias(0), bias(nblk), bias(2 * nblk), ch, ch, ch,
                  pl.BlockSpec((1, cb), lambda b, j: (0, j)), const(fwd), const(inv)],
        out_specs=tok(0),
        out_shape=jax.ShapeDtypeStruct((bsz * seq, HY_WIDTH), act_dt),
        compiler_params=_cparams(2),
        name="hyena",
    )(hy, hy, hy, conv_w, conv_w, conv_w, conv_b2, conv_b2, conv_b2, ha, hb, hd,
      skip.reshape(1, -1), fwd, inv)


def _mix_kernel(has_pos, *refs):
    if has_pos:
        x_ref, pos_ref = refs[:2]
        refs = refs[2:]
        x = x_ref[...] + pos_ref[...]
    else:
        x_ref = refs[0]
        refs = refs[1:]
        x = x_ref[...]
    (oa_ref, ob_ref, g_ref, mod_ref, wua, wub, wout, l1g, l1b, wr_hi, wr_lo, x1_o, u2_o, p_o) = refs
    d = D_MODEL
    gates = g_ref[...].astype(F32)
    ga = _sigmoid(gates[:, :d])
    gb = _sigmoid(gates[:, d:])
    up = (ga * jnp.dot(oa_ref[...].astype(BF16), wua[...], preferred_element_type=F32)
          + gb * jnp.dot(ob_ref[...].astype(BF16), wub[...], preferred_element_type=F32))
    mixed = jnp.dot(up.astype(BF16), wout[...], preferred_element_type=F32)
    g1 = mod_ref[0, 2]
    sh2 = mod_ref[0, 3]
    sc2 = mod_ref[0, 4]
    x1 = _ln(ALPHA * x + g1 * mixed) * l1g[...] + l1b[...]
    x1_o[...] = x1
    u2 = _ln(x1) * (1.0 + sc2) + sh2
    u2_hi = u2.astype(BF16)
    u2_o[...] = u2_hi
    u2_lo = (u2 - u2_hi.astype(F32)).astype(BF16)
    logits = (jnp.dot(u2_hi, wr_hi[...], preferred_element_type=F32)
              + jnp.dot(u2_hi, wr_lo[...], preferred_element_type=F32)
              + jnp.dot(u2_lo, wr_hi[...], preferred_element_type=F32))
    lane = _iota(logits.shape, 1)
    logits = jnp.where(lane < N_EXPERTS, logits, -jnp.inf)
    m = jnp.max(logits, axis=-1, keepdims=True)
    e = jnp.exp(logits - m)
    p_o[...] = e / jnp.sum(e, axis=-1, keepdims=True)


def _mix_call(x2d, pos, o_a, o_b, gates, mod4, mod_row, wua, wub, wout, l1g, l1b, wr_hi, wr_lo):
    n, d = x2d.shape
    tm = TOKEN_TILE
    row = lambda i: (i, 0)
    const = lambda i: (0, 0)
    in_specs = [pl.BlockSpec((tm, d), row)]
    args = [x2d]
    if pos is not None:
        tiles = pos.shape[0] // tm
        in_specs.append(pl.BlockSpec((tm, d), lambda i: (i % tiles, 0)))
        args.append(pos)
    in_specs += [pl.BlockSpec((tm, DN_WIDTH), row), pl.BlockSpec((tm, HY_WIDTH), row),
                 pl.BlockSpec((tm, 2 * d), row), _mod_spec(mod_row)]
    args += [o_a, o_b, gates, mod4]
    for w in (wua, wub, wout, l1g, l1b, wr_hi, wr_lo):
        in_specs.append(pl.BlockSpec(w.shape, const, pipeline_mode=pl.Buffered(1)))
        args.append(w)
    return pl.pallas_call(
        functools.partial(_mix_kernel, pos is not None),
        grid=(n // tm,),
        in_specs=in_specs,
        out_specs=[pl.BlockSpec((tm, d), row), pl.BlockSpec((tm, d), row), pl.BlockSpec((tm, LANES), row)],
        out_shape=[jax.ShapeDtypeStruct((n, d), F32), jax.ShapeDtypeStruct((n, d), BF16),
                   jax.ShapeDtypeStruct((n, LANES), F32)],
        compiler_params=_cparams(1),
        name="mix",
    )(*args)


def _route_kernel(seq, cap, p_ref, u_ref, tri_ref, xs_ref, g_ref, pt_ref):
    e_n = N_EXPERTS
    p = p_ref[...]
    pt = p.T[:e_n, :]
    tri = tri_ref[...]

    if seq <= RANK_COUNT_MAX_SEQ:
        earlier = _iota((seq, seq), 0) < _iota((seq, seq), 1)
        ranks = []
        for e in range(e_n):
            pc = p[:, e:e + 1]
            pr = pt[e:e + 1, :]
            beats = (pc > pr) | (earlier & (pc == pr))
            ranks.append(jnp.sum(jnp.where(beats, 1.0, 0.0), axis=0, keepdims=True))
        sel = jnp.concatenate(ranks, axis=0) < cap
    else:
        def search(i, cur):
            cand = cur | (1 << (30 - i))
            cand_f = pltpu.bitcast(cand, F32)
            cnt = jnp.sum(jnp.where(pt >= cand_f, 1.0, 0.0), axis=1, keepdims=True)
            return jnp.where((cnt >= cap) & (cand >= MIN_NORMAL_F32_BITS), cand, cur)

        thr = pltpu.bitcast(lax.fori_loop(0, 31, search, jnp.zeros((e_n, 1), jnp.int32)), F32)
        gt = pt > thr
        eq = pt == thr
        n_gt = jnp.sum(jnp.where(gt, 1.0, 0.0), axis=1, keepdims=True)
        eq_rank = jnp.dot(jnp.where(eq, 1.0, 0.0).astype(BF16), tri, preferred_element_type=F32)
        sel = gt | (eq & (eq_rank < cap - n_gt))
    pos = jnp.dot(jnp.where(sel, 1.0, 0.0).astype(BF16), tri, preferred_element_type=F32)
    slot = jnp.where(sel, pos, -1.0)
    pt_ref[0] = pt
    u = u_ref[...]
    jrow = _iota((cap, seq), 0).astype(F32)
    for e in range(e_n):
        onehot = jnp.where(slot[e:e + 1, :] == jrow, 1.0, 0.0).astype(BF16)
        g_ref[0, e * cap:(e + 1) * cap, :] = onehot
        xs_ref[e] = jnp.dot(onehot, u, preferred_element_type=F32).astype(xs_ref.dtype)


def _route_call(probs, u2, bsz, seq):
    cap = EC_FACTOR * seq // N_EXPERTS
    tri = jnp.asarray(np.triu(np.ones((seq, seq), np.float32), 1), dtype=BF16)
    return pl.pallas_call(
        functools.partial(_route_kernel, seq, cap),
        grid=(bsz,),
        in_specs=[pl.BlockSpec((seq, LANES), lambda b: (b, 0)),
                  pl.BlockSpec((seq, D_MODEL), lambda b: (b, 0)),
                  pl.BlockSpec((seq, seq), lambda b: (0, 0), pipeline_mode=pl.Buffered(1))],
        out_specs=[pl.BlockSpec((N_EXPERTS, cap, D_MODEL), lambda b: (0, b, 0)),
                   pl.BlockSpec((1, N_EXPERTS * cap, seq), lambda b: (b, 0, 0)),
                   pl.BlockSpec((1, N_EXPERTS, seq), lambda b: (b, 0, 0))],
        out_shape=[jax.ShapeDtypeStruct((N_EXPERTS, bsz * cap, D_MODEL), BF16),
                   jax.ShapeDtypeStruct((bsz, N_EXPERTS * cap, seq), BF16),
                   jax.ShapeDtypeStruct((bsz, N_EXPERTS, seq), F32)],
        compiler_params=_cparams(1),
        name="route",
    )(probs, u2, tri)


def _expert_kernel(xc_ref, xd_ref, wg_ref, wu_ref, wd_ref, yc_ref, yd_ref, accc, accd):
    j = pl.program_id(1)
    wg = wg_ref[0].astype(BF16)
    wu = wu_ref[0].astype(BF16)
    wd = wd_ref[0].astype(BF16)
    for x_ref, y_ref, acc in ((xc_ref, yc_ref, accc), (xd_ref, yd_ref, accd)):
        x = x_ref[0]
        hg = jnp.dot(x, wg, preferred_element_type=F32)
        hu = jnp.dot(x, wu, preferred_element_type=F32)
        part = jnp.dot((_silu(hg) * hu).astype(BF16), wd, preferred_element_type=F32)

        @pl.when(j == 0)
        def _():
            acc[...] = part

        @pl.when(j > 0)
        def _():
            acc[...] += part

        @pl.when(j == pl.num_programs(1) - 1)
        def _():
            y_ref[0] = acc[...].astype(y_ref.dtype)


def _expert_call(xs_c, xs_d, w_gate, w_up, w_down):
    e_n, rc, d = xs_c.shape
    rd = xs_d.shape[1]
    ff = w_gate.shape[2]
    fb = FF_BLOCK
    return pl.pallas_call(
        _expert_kernel,
        grid=(e_n, ff // fb),
        in_specs=[pl.BlockSpec((1, rc, d), lambda e, j: (e, 0, 0)),
                  pl.BlockSpec((1, rd, d), lambda e, j: (e, 0, 0)),
                  pl.BlockSpec((1, d, fb), lambda e, j: (e, 0, j)),
                  pl.BlockSpec((1, d, fb), lambda e, j: (e, 0, j)),
                  pl.BlockSpec((1, fb, d), lambda e, j: (e, j, 0))],
        out_specs=[pl.BlockSpec((1, rc, d), lambda e, j: (e, 0, 0)),
                   pl.BlockSpec((1, rd, d), lambda e, j: (e, 0, 0))],
        out_shape=[jax.ShapeDtypeStruct((e_n, rc, d), BF16), jax.ShapeDtypeStruct((e_n, rd, d), BF16)],
        scratch_shapes=[pltpu.VMEM((rc, d), F32), pltpu.VMEM((rd, d), F32)],
        compiler_params=_cparams(2),
        name="experts",
    )(xs_c, xs_d, w_gate, w_up, w_down)


def _final_kernel(seq, cap, x1_ref, y_ref, g_ref, pt_ref, mod_ref, l2g, l2b, o_ref, yw_s):
    pt = pt_ref[0]
    for e in range(N_EXPERTS):
        onehot = g_ref[0, e * cap:(e + 1) * cap, :].astype(F32)
        aff = jnp.sum(onehot * pt[e:e + 1, :], axis=1, keepdims=True)
        yw_s[e * cap:(e + 1) * cap, :] = (y_ref[e].astype(F32) * aff).astype(BF16)
    ffn = lax.dot_general(g_ref[0], yw_s[...], (((0,), (0,)), ((), ())), preferred_element_type=F32)
    g2 = mod_ref[0, 5]
    o_ref[...] = _ln(ALPHA * x1_ref[...] + g2 * ffn) * l2g[...] + l2b[...]


def _final_call(x1, ys, gmat, pt, mod4, mod_row, l2g, l2b, bsz, seq):
    cap = EC_FACTOR * seq // N_EXPERTS
    d = D_MODEL
    return pl.pallas_call(
        functools.partial(_final_kernel, seq, cap),
        grid=(bsz,),
        in_specs=[pl.BlockSpec((seq, d), lambda b: (b, 0)),
                  pl.BlockSpec((N_EXPERTS, cap, d), lambda b: (0, b, 0)),
                  pl.BlockSpec((1, N_EXPERTS * cap, seq), lambda b: (b, 0, 0)),
                  pl.BlockSpec((1, N_EXPERTS, seq), lambda b: (b, 0, 0)),
                  _mod_spec(mod_row),
                  pl.BlockSpec((1, d), lambda b: (0, 0)),
                  pl.BlockSpec((1, d), lambda b: (0, 0))],
        out_specs=pl.BlockSpec((seq, d), lambda b: (b, 0)),
        out_shape=jax.ShapeDtypeStruct((bsz * seq, d), F32),
        scratch_shapes=[pltpu.VMEM((N_EXPERTS * cap, d), BF16)],
        compiler_params=_cparams(1),
        name="final",
    )(x1, ys, gmat, pt, mod4, l2g, l2b)


def _grid_pos_embed(rows, dim):
    r = jnp.repeat(jnp.arange(rows), GRID_W)
    col = jnp.tile(jnp.arange(GRID_W), rows)
    quarter = dim // 4
    omega = 1.0 / (10000.0 ** (jnp.arange(quarter, dtype=jnp.float32) / quarter))

    def enc(p):
        ang = p.astype(jnp.float32)[:, None] * omega[None, :]
        return jnp.concatenate([jnp.sin(ang), jnp.cos(ang)], -1)

    return jnp.concatenate([enc(r), enc(col)], -1)


ACT_DT = BF16


def kernel(x_prompt, x_sample, state_delta_fwd, state_delta_bwd, c, c_ctx, w_mod, b_mod, w_in, conv_a_w, a_log, dt_bias, gnorm_w, conv_b_w, conv_b_b, hy_w1, hy_b1, hy_w2, hy_b2, hy_w3, hy_freq, hy_bias, w_up_a, w_up_b, w_out, ln1_g, ln1_b, w_router, w_e_gate, w_e_up, w_e_down, ln2_g, ln2_b):
    d = D_MODEL
    n_ctx, l_ctx, _ = x_prompt.shape
    n_dec, l_dec, _ = x_sample.shape
    lyr = 0

    cond8 = jnp.zeros((8, d), F32).at[0].set(c_ctx).at[1:1 + n_dec].set(c)
    mod4 = _mod_call(cond8, w_mod[lyr], b_mod[lyr]).reshape(8, N_MOD, 1, d)

    w = w_in[lyr].astype(BF16)
    o_z = QKV_WIDTH
    o_b = o_z + DN_WIDTH
    o_hy = o_b + 4 * DN_HEADS
    o_g = o_hy + 3 * HY_WIDTH
    w_ba = jnp.pad(w[:, o_b:o_hy], ((0, 0), (0, LANES - 4 * DN_HEADS)))
    in_w = (w[:, :o_z], w[:, o_z:o_b], w_ba, w[:, o_hy:o_g], w[:, o_g:])

    prm = jnp.zeros((8, LANES), F32)
    prm = prm.at[0, 2 * DN_HEADS:4 * DN_HEADS].set(a_log[lyr].reshape(-1))
    prm = prm.at[1, 2 * DN_HEADS:4 * DN_HEADS].set(dt_bias[lyr].reshape(-1))
    gn = gnorm_w[lyr].reshape(1, -1)

    wua = w_up_a[lyr].astype(BF16)
    wub = w_up_b[lyr].astype(BF16)
    wout = w_out[lyr].astype(BF16)
    wr = jnp.pad(w_router[lyr], ((0, 0), (0, LANES - N_EXPERTS)))
    wr_hi = wr.astype(BF16)
    wr_lo = (wr - wr_hi.astype(F32)).astype(BF16)
    l1g, l1b = ln1_g[lyr].reshape(1, -1), ln1_b[lyr].reshape(1, -1)
    l2g, l2b = ln2_g[lyr].reshape(1, -1), ln2_b[lyr].reshape(1, -1)

    def front(x2d, pos, bsz, seq, mod_row, s0f, s0b):
        qkv, z, ba, hy, gates = _inproj_call(x2d, pos, mod4, mod_row, in_w, ACT_DT)
        o_a, s_f, s_b = _deltanet_call(qkv, z, ba, conv_a_w[lyr], prm, gn, s0f, s0b, bsz, seq, ACT_DT)
        fwd, inv = (jnp.asarray(m) for m in _dft_mats(seq))
        ha, hb, hd = _hfilter_call(seq, hy_w1[lyr], hy_b1[lyr], hy_w2[lyr], hy_b2[lyr], hy_w3[lyr],
                                   hy_freq[lyr], fwd)
        o_h = _hyena_call(hy, conv_b_w[lyr], conv_b_b[lyr], ha, hb, hd, hy_bias[lyr],
                          fwd.astype(BF16), inv.astype(BF16), bsz, seq, ACT_DT)
        x1, u2, probs = _mix_call(x2d, pos, o_a, o_h, gates, mod4, mod_row, wua, wub, wout, l1g, l1b,
                                   wr_hi, wr_lo)
        xs, gmat, pt = _route_call(probs, u2, bsz, seq)
        return x1, xs, gmat, pt, s_f, s_b

    tiles_dec = l_dec // TOKEN_TILE
    row_ctx = lambda *idx: 0
    row_dec_tile = lambda i: 1 + i // tiles_dec
    row_dec_req = lambda b: 1 + b

    xc = x_prompt.reshape(n_ctx * l_ctx, d)
    xd = x_sample.reshape(n_dec * l_dec, d)
    pos = _grid_pos_embed(l_dec // GRID_W, d).astype(F32)

    x1c, xsc, gc, ptc, s_f, s_b = front(xc, None, n_ctx, l_ctx, row_ctx, None, None)
    x1d, xsd, gd, ptd, _, _ = front(xd, pos, n_dec, l_dec, row_dec_tile,
                                    state_delta_fwd[:, lyr], state_delta_bwd[:, lyr])
    yc, yd = _expert_call(xsc, xsd, w_e_gate[lyr], w_e_up[lyr], w_e_down[lyr])
    y_prompt = _final_call(x1c, yc, gc, ptc, mod4, row_ctx, l2g, l2b, n_ctx, l_ctx)
    y_sample = _final_call(x1d, yd, gd, ptd, mod4, row_dec_req, l2g, l2b, n_dec, l_dec)

    return (y_prompt.reshape(n_ctx, l_ctx, d), y_sample.reshape(n_dec, l_dec, d),
            s_f[:, None], s_b[:, None])
```

```python
import functools
import math

import jax
import jax.numpy as jnp
import numpy as np
from jax import lax
from jax.experimental import pallas as pl
from jax.experimental.pallas import tpu as pltpu

F32 = jnp.float32
BF16 = jnp.bfloat16
HIGHEST = lax.Precision.HIGHEST

D_MODEL = 1024
DEPTH = 1
GRID_W = 64
DN_HEADS = 4
DN_DK = 128
DN_DV = 128
DN_WIDTH = DN_HEADS * DN_DV
HY_WIDTH = D_MODEL // 2
HY_EMB = 33
HY_FFN = 64
HY_TARGET = 1e-2
HY_FAST_DECAY = 0.3
HY_SLOW_DECAY = 1.5
N_EXPERTS = 16
EC_FACTOR = 2
N_MOD = 6
ALPHA = (2 * DEPTH) ** 0.25
LN_EPS = 1e-5
RMS_EPS = 1e-6
QKV_WIDTH = 2 * DN_HEADS * DN_DK + DN_WIDTH

LANES = 128
INPROJ_TILE = 512
MIX_TILE = 256
MOD_COL_BLOCK = 512
DN_CHUNK = 256
INV_BASE = 16
HY_CH_BLOCK = 512
FF_BLOCK = 512
VMEM_LIMIT = 56 * 1024 * 1024
MIN_NORMAL_F32_BITS = 0x00800000
RANK_COUNT_MAX_SEQ = 256


def _cparams(n_axes):
    return pltpu.CompilerParams(dimension_semantics=("arbitrary",) * n_axes,
                                vmem_limit_bytes=VMEM_LIMIT)


def _mm(a, b):
    return jnp.dot(a.astype(BF16), b.astype(BF16), preferred_element_type=F32)


def _mm_nt(a, b):
    return lax.dot_general(a.astype(BF16), b.astype(BF16), (((1,), (1,)), ((), ())),
                           preferred_element_type=F32)


def _bmm(a, b):
    return jnp.einsum("bmk,bkn->bmn", a.astype(BF16), b.astype(BF16), preferred_element_type=F32)


def _bmm_nt(a, b):
    return jnp.einsum("bmk,bnk->bmn", a.astype(BF16), b.astype(BF16), preferred_element_type=F32)


def _bmm_tn(a, b):
    return jnp.einsum("bkm,bkn->bmn", a.astype(BF16), b.astype(BF16), preferred_element_type=F32)


def _mm_f32(a, b):
    return jnp.dot(a, b, precision=HIGHEST, preferred_element_type=F32)


def _sigmoid(x):
    return 1.0 / (1.0 + jnp.exp(-x))


def _silu(x):
    return x * _sigmoid(x)


def _softplus(x):
    return jnp.maximum(x, 0.0) + jnp.log1p(jnp.exp(-jnp.abs(x)))


def _ln(x):
    mu = jnp.mean(x, axis=-1, keepdims=True)
    xc = x - mu
    var = jnp.mean(xc * xc, axis=-1, keepdims=True)
    return xc * lax.rsqrt(var + LN_EPS)


def _iota(shape, dim):
    return lax.broadcasted_iota(jnp.int32, shape, dim)


def _mod_kernel(n_rows, ct_ref, w_ref, b_ref, o_ref):
    s = _silu(ct_ref[...])
    w = w_ref[...]
    b = b_ref[...]
    o_ref[...] = jnp.zeros(o_ref.shape, F32)
    for r in range(n_rows):
        o_ref[r:r + 1, :] = jnp.sum(w * s[:, r:r + 1], axis=0, keepdims=True) + b


def _mod_call(cond_t, n_rows, w_mod, b_mod):
    d = D_MODEL
    tn = MOD_COL_BLOCK
    return pl.pallas_call(
        functools.partial(_mod_kernel, n_rows),
        grid=(N_MOD * d // tn,),
        in_specs=[pl.BlockSpec((d, 8), lambda j: (0, 0)),
                  pl.BlockSpec((d, tn), lambda j: (0, j)),
                  pl.BlockSpec((1, tn), lambda j: (0, j))],
        out_specs=pl.BlockSpec((8, tn), lambda j: (0, j)),
        out_shape=jax.ShapeDtypeStruct((8, N_MOD * d), F32),
        compiler_params=_cparams(1),
        name="mod",
    )(cond_t, w_mod, b_mod.reshape(1, -1))


def _mod_spec(row_of_step):
    return pl.BlockSpec((1, N_MOD, 1, D_MODEL), lambda *idx: (row_of_step(*idx), 0, 0, 0))


def _inproj_kernel(has_pos, *refs):
    if has_pos:
        x_ref, pos_ref, mod_ref, wqkv, wz, wba, why, wg, qkv_o, z_o, ba_o, hy_o, g_o = refs
        x = x_ref[...] + pos_ref[...]
    else:
        x_ref, mod_ref, wqkv, wz, wba, why, wg, qkv_o, z_o, ba_o, hy_o, g_o = refs
        x = x_ref[...]
    sh1 = mod_ref[0, 0]
    sc1 = mod_ref[0, 1]
    u = (_ln(x) * (1.0 + sc1) + sh1).astype(BF16)
    qkv_o[...] = jnp.dot(u, wqkv[...], preferred_element_type=F32).astype(qkv_o.dtype)
    z_o[...] = jnp.dot(u, wz[...], preferred_element_type=F32).astype(z_o.dtype)
    ba_o[...] = jnp.dot(u, wba[...], preferred_element_type=F32)
    hy_o[...] = jnp.dot(u, why[...], preferred_element_type=F32).astype(hy_o.dtype)
    g_o[...] = jnp.dot(u, wg[...], preferred_element_type=F32).astype(g_o.dtype)


def _inproj_call(x2d, pos, mod4, mod_row, weights, act_dt):
    n, d = x2d.shape
    tm = INPROJ_TILE
    wqkv, wz, wba, why, wg = weights
    row = lambda i: (i, 0)
    const = lambda i: (0, 0)
    in_specs = [pl.BlockSpec((tm, d), row)]
    args = [x2d]
    if pos is not None:
        tiles = pos.shape[0] // tm
        in_specs.append(pl.BlockSpec((tm, d), lambda i: (i % tiles, 0)))
        args.append(pos)
    in_specs.append(_mod_spec(lambda i: mod_row(i * tm)))
    args.append(mod4)
    for w in weights:
        in_specs.append(pl.BlockSpec(w.shape, const, pipeline_mode=pl.Buffered(1)))
        args.append(w)
    widths = (QKV_WIDTH, DN_WIDTH, LANES, 3 * HY_WIDTH, 2 * D_MODEL)
    dts = (act_dt, act_dt, F32, act_dt, act_dt)
    return pl.pallas_call(
        functools.partial(_inproj_kernel, pos is not None),
        grid=(n // tm,),
        in_specs=in_specs,
        out_specs=[pl.BlockSpec((tm, w), row) for w in widths],
        out_shape=[jax.ShapeDtypeStruct((n, w), dt) for w, dt in zip(widths, dts)],
        compiler_params=_cparams(1),
        name="inproj",
    )(*args)


def _inv_unit_tri(lm, ri, ci, block):
    c = lm.shape[-1]

    def same_block(h):
        sh = h.bit_length() - 1
        return (ri >> sh) == (ci >> sh)

    def unfold(f, h):
        return jnp.where(same_block(h), jnp.concatenate([f] * (c // h), axis=1), 0.0)

    base = INV_BASE
    n_nat = jnp.where(same_block(base), -lm, 0.0)
    n_fold = n_nat[:, 0:base, :]
    for i in range(1, c // base):
        n_fold = n_fold + n_nat[:, i * base:(i + 1) * base, :]
    eye_fold = jnp.where((_iota((base, c), 1) & (base - 1)) == _iota((base, c), 0), 1.0, 0.0)
    t = eye_fold + n_fold
    p_fold, p_nat = n_fold, n_nat
    span = 2
    while span < base:
        p_fold = _bmm(p_fold, p_nat)
        p_nat = unfold(p_fold, base)
        t = t + _bmm(t, p_nat)
        span *= 2
    s = base
    while s < block:
        sh = s.bit_length() - 1
        t_nat = unfold(t, s)
        even = ((_iota((s, c), 1) >> sh) & 1) == 0
        t2 = jnp.concatenate([jnp.where(even, t, 0.0), jnp.where(even, 0.0, t)], axis=1)
        lo = jnp.where(same_block(2 * s) & ~same_block(s), lm, 0.0)
        t = t2 - _bmm(_bmm(t2, lo), t_nat)
        s *= 2
    return t


def _dn_chunk(q, k, v, gc_col, gc_row, beta, s_prev, same_chunk, ri, ci):
    c = DN_CHUNK
    nh = DN_HEADS
    half = c // 2
    if same_chunk:
        gram = _bmm_nt(k[:nh], k[:nh])
        qk = _bmm_nt(q[:nh], k[:nh])
        gram = jnp.concatenate([gram, gram], 0)
        qk = jnp.concatenate([qk, qk], 0)
    else:
        gram = _bmm_nt(k, k)
        qk = _bmm_nt(q, k)
    diff = gc_col - gc_row
    incl = jnp.concatenate([jnp.broadcast_to(ri >= ci, (nh, c, c)), jnp.broadcast_to(ri <= ci, (nh, c, c))], 0)
    strict = incl & (ri != ci)
    decay = jnp.exp(jnp.where(incl, diff, -jnp.inf))
    lm = jnp.where(strict, beta * gram * decay, 0.0)
    a_intra = qk * decay
    t = _inv_unit_tri(lm, ri, ci, half)
    t1 = t[:, :, :half]
    t2 = t[:, :, half:]
    e_col = jnp.exp(gc_col)
    rhs = jnp.concatenate([v * beta, k * (beta * e_col)], axis=2)
    y1 = _bmm(t1, rhs[:, :half, :])
    y2 = _bmm(t2, rhs[:, half:, :])
    c_f = _bmm(lm[:nh, half:, :half], y1[:nh])
    c_b = _bmm(lm[nh:, :half, half:], y2[nh:])
    c_f = _bmm(t2[:nh], c_f)
    c_b = _bmm(t1[nh:], c_b)
    sol = jnp.concatenate([jnp.concatenate([y1[:nh], y2[:nh] - c_f], 1),
                           jnp.concatenate([y1[nh:] - c_b, y2[nh:]], 1)], 0)
    u = sol[:, :, :DN_DV]
    w = sol[:, :, DN_DV:]
    gl = jnp.concatenate([gc_col[:nh, c - 1:c, :], gc_col[nh:, 0:1, :]], 0)
    ks = k * jnp.exp(gl - gc_col)
    if s_prev is None:
        v_new = u
        o = _bmm(a_intra, v_new)
        s_new = _bmm_tn(ks, v_new)
    else:
        v_new = u - _bmm(w, s_prev)
        o = _bmm(q * e_col, s_prev) + _bmm(a_intra, v_new)
        s_new = s_prev * jnp.exp(gl) + _bmm_tn(ks, v_new)
    return o, s_new


def _deltanet_kernel(seq, zero_init, *refs):
    qkv_ref, cw_ref, z_ref, ba_ref, prm_ref, gn_ref = refs[:6]
    refs = refs[6:]
    if not zero_init:
        s0f_ref, s0b_ref = refs[:2]
        refs = refs[2:]
    o_ref, sf_ref, sb_ref, q_s, k_s, v_s, g_s, b_s, o_s, st_s = refs
    c = DN_CHUNK
    n = seq // c
    nh = DN_HEADS

    row = _iota((seq, LANES), 0)
    first_row = row == 0
    last_row = row == seq - 1

    def conv_silu(j):
        cols = slice(j * LANES, (j + 1) * LANES)
        x = qkv_ref[:, cols].astype(F32)
        w = cw_ref[:, cols]
        xm = jnp.where(first_row, 0.0, pltpu.roll(x, 1, 0))
        xp = jnp.where(last_row, 0.0, pltpu.roll(x, seq - 1, 0))
        return _silu(w[0:1, :] * xm + w[1:2, :] * x + w[2:3, :] * xp)

    for h in range(nh):
        q = conv_silu(h)
        q_s[h] = q * lax.rsqrt(jnp.sum(q * q, axis=-1, keepdims=True) + 1e-6) * (DN_DK ** -0.5)
        k = conv_silu(nh + h)
        k_s[h] = k * lax.rsqrt(jnp.sum(k * k, axis=-1, keepdims=True) + 1e-6)
        v_s[h] = conv_silu(2 * nh + h)

    ba = ba_ref[...]
    b_s[...] = _sigmoid(ba)
    g_s[...] = -jnp.exp(prm_ref[0:1, :]) * _softplus(ba + prm_ref[1:2, :])

    ri = _iota((c, c), 0)
    ci = _iota((c, c), 1)
    tri_l = jnp.where(ri >= ci, 1.0, 0.0).astype(F32)
    tri_u = jnp.where(ri <= ci, 1.0, 0.0).astype(F32)

    def chunk_pair(idx_f, idx_b, s_prev):
        def ds(idx):
            start = idx * c
            return pl.ds(start if isinstance(start, int) else pl.multiple_of(start, c), c)

        sl_f, sl_b = ds(idx_f), ds(idx_b)
        cs_f = _mm_f32(tri_l, g_s[sl_f, :])
        cs_b = _mm_f32(tri_u, g_s[sl_b, :])
        cst_f = cs_f.T
        cst_b = cs_b.T
        b_f = b_s[sl_f, :]
        b_b = b_s[sl_b, :]
        heads = range(nh)
        gc_col = jnp.stack([cs_f[:, 2 * nh + h:2 * nh + h + 1] for h in heads]
                           + [cs_b[:, 3 * nh + h:3 * nh + h + 1] for h in heads], 0)
        gc_row = jnp.stack([cst_f[2 * nh + h:2 * nh + h + 1, :] for h in heads]
                           + [cst_b[3 * nh + h:3 * nh + h + 1, :] for h in heads], 0)
        beta = jnp.stack([b_f[:, h:h + 1] for h in heads] + [b_b[:, nh + h:nh + h + 1] for h in heads], 0)
        both = lambda ref: jnp.concatenate([ref[:, sl_f, :], ref[:, sl_b, :]], 0)
        o, s_new = _dn_chunk(both(q_s), both(k_s), both(v_s), gc_col, gc_row, beta, s_prev, n == 1, ri, ci)
        o_s[:nh, sl_f, :] = o[:nh]
        o_s[nh:, sl_b, :] = o[nh:]
        return s_new

    if n == 1:
        s0 = None if zero_init else jnp.concatenate([s0f_ref[0], s0b_ref[0]], 0)
        s_fin = chunk_pair(0, 0, s0)
    else:
        if zero_init:
            st_s[...] = jnp.zeros(st_s.shape, F32)
        else:
            st_s[:nh] = s0f_ref[0]
            st_s[nh:] = s0b_ref[0]

        def body(i, carry):
            st_s[...] = chunk_pair(i, n - 1 - i, st_s[...])
            return carry

        lax.fori_loop(0, n, body, 0)
        s_fin = st_s[...]
    sf_ref[0] = s_fin[:nh]
    sb_ref[0] = s_fin[nh:]

    gn = gn_ref[...]
    for h in range(nh):
        cols = slice(h * LANES, (h + 1) * LANES)
        o = o_s[h] + o_s[nh + h]
        o = o * lax.rsqrt(jnp.mean(o * o, axis=-1, keepdims=True) + RMS_EPS) * gn
        o_ref[:, cols] = (o * _silu(z_ref[:, cols].astype(F32))).astype(o_ref.dtype)


def _deltanet_call(qkv, z, ba, conv_w, prm, gn, s0f, s0b, bsz, seq, act_dt):
    nh = DN_HEADS
    zero_init = s0f is None
    st = pl.BlockSpec((1, nh, DN_DK, DN_DV), lambda b: (b, 0, 0, 0))
    in_specs = [pl.BlockSpec((seq, QKV_WIDTH), lambda b: (b, 0)),
                pl.BlockSpec((3, QKV_WIDTH), lambda b: (0, 0)),
                pl.BlockSpec((seq, DN_WIDTH), lambda b: (b, 0)),
                pl.BlockSpec((seq, LANES), lambda b: (b, 0)),
                pl.BlockSpec((8, LANES), lambda b: (0, 0)),
                pl.BlockSpec((1, LANES), lambda b: (0, 0))]
    args = [qkv, conv_w, z, ba, prm, gn]
    if not zero_init:
        in_specs += [st, st]
        args += [s0f, s0b]
    vm = lambda shape: pltpu.VMEM(shape, F32)
    per_head = vm((nh, seq, LANES))
    return pl.pallas_call(
        functools.partial(_deltanet_kernel, seq, zero_init),
        grid=(bsz,),
        in_specs=in_specs,
        out_specs=[pl.BlockSpec((seq, DN_WIDTH), lambda b: (b, 0)), st, st],
        out_shape=[jax.ShapeDtypeStruct((bsz * seq, DN_WIDTH), act_dt),
                   jax.ShapeDtypeStruct((bsz, nh, DN_DK, DN_DV), F32),
                   jax.ShapeDtypeStruct((bsz, nh, DN_DK, DN_DV), F32)],
        scratch_shapes=[per_head, per_head, per_head, vm((seq, LANES)), vm((seq, LANES)),
                        vm((2 * nh, seq, LANES)), vm((2 * nh, DN_DK, DN_DV))],
        compiler_params=_cparams(1),
        name="deltanet",
    )(*args)


def _dft_mats(seq):
    n = 2 * seq
    f = np.arange(seq)[:, None]
    s = np.arange(seq)[None, :]
    ang = 2.0 * np.pi * ((f * s) % n) / n
    fwd = np.concatenate([np.cos(ang), -np.sin(ang)], axis=0)
    fwd[seq, :] = np.cos(np.pi * np.arange(seq))
    t = (np.arange(seq) + seq // 2)[:, None]
    ff = np.arange(seq)[None, :]
    ang2 = 2.0 * np.pi * ((t * ff) % n) / n
    inv_r = 2.0 * np.cos(ang2) / n
    inv_i = -2.0 * np.sin(ang2) / n
    inv_r[:, 0] = 1.0 / n
    inv_i[:, 0] = np.cos(np.pi * t[:, 0]) / n
    inv = np.concatenate([inv_r, inv_i], axis=1)
    return fwd.astype(np.float32), inv.astype(np.float32)


def _filter_feats(seq):
    t = np.linspace(0.0, 1.0, seq)[:, None]
    bands = (HY_EMB - 1) // 2
    ang = (2.0 * math.pi * np.arange(seq) / seq)[:, None] * np.linspace(1e-4, bands - 1, bands)[None, :]
    feats = np.concatenate([t, np.cos(ang), -np.sin(ang)], -1)
    deltas = np.abs(np.linspace(math.log(HY_TARGET) / HY_SLOW_DECAY, math.log(HY_TARGET) / HY_FAST_DECAY,
                                HY_WIDTH))
    offset = np.abs(np.arange(seq) - seq // 2) / (seq // 2)
    window = np.exp(-offset[:, None] * deltas[None, :])
    feats = np.pad(feats, ((0, 0), (0, LANES - HY_EMB)))
    return jnp.asarray(feats, dtype=F32), jnp.asarray(window, dtype=F32)


def _hfilter_kernel(seq, feats_ref, win_ref, w1_ref, b1_ref, w2_ref, b2_ref, w3_ref, fr_ref, fwd_hi_ref,
                    fwd_lo_ref, ha_ref, hb_ref, hd_ref):
    fr = fr_ref[...]
    hid = jnp.sin(fr * (_mm_f32(feats_ref[...], w1_ref[...]) + b1_ref[...]))
    hid = jnp.sin(fr * (_mm_f32(hid, w2_ref[...]) + b2_ref[...]))
    filt = _mm_f32(hid, w3_ref[...]) * win_ref[...]
    filt = filt / (jnp.sum(jnp.abs(filt), axis=0, keepdims=True) + 1e-6)
    filt_hi = filt.astype(BF16)
    filt_lo = (filt - filt_hi.astype(F32)).astype(BF16)
    fwd_hi = fwd_hi_ref[...]
    spec = (jnp.dot(fwd_hi, filt_hi, preferred_element_type=F32)
            + jnp.dot(fwd_hi, filt_lo, preferred_element_type=F32)
            + jnp.dot(fwd_lo_ref[...], filt_hi, preferred_element_type=F32))
    h_re = spec[:seq, :]
    h_im = spec[seq:, :]
    first = _iota(h_re.shape, 0) == 0
    ha_ref[...] = h_re
    hb_ref[...] = jnp.where(first, 0.0, h_im)
    hd_ref[...] = jnp.where(first, h_im, h_re)


def _hfilter_call(seq, w1, b1, w2, b2, w3, freq, fwd_hi, fwd_lo):
    feats, window = _filter_feats(seq)
    w1p = jnp.pad(w1, ((0, LANES - HY_EMB), (0, 0)))
    full = lambda a: pl.BlockSpec(a.shape, lambda i: (0,) * a.ndim)
    args = [feats, window, w1p, b1.reshape(1, -1), w2, b2.reshape(1, -1), w3, freq.reshape(1, -1), fwd_hi,
            fwd_lo]
    out = jax.ShapeDtypeStruct((seq, HY_WIDTH), F32)
    return pl.pallas_call(
        functools.partial(_hfilter_kernel, seq),
        grid=(1,),
        in_specs=[full(a) for a in args],
        out_specs=[pl.BlockSpec((seq, HY_WIDTH), lambda i: (0, 0))] * 3,
        out_shape=[out, out, out],
        compiler_params=_cparams(1),
        name="hfilter",
    )(*args)


def _hyena_kernel(seq, x0_ref, x1_ref, v_ref, c0_ref, c1_ref, c2_ref, b0_ref, b1_ref, b2_ref,
                  ha_ref, hb_ref, hd_ref, skip_ref, fwd_ref, inv_ref, o_ref):
    row = _iota((seq, HY_CH_BLOCK), 0)

    def conv(x_ref, w_ref, b_ref):
        x = x_ref[...].astype(F32)
        w = w_ref[...]
        xm = jnp.where(row == 0, 0.0, pltpu.roll(x, 1, 0))
        xp = jnp.where(row == seq - 1, 0.0, pltpu.roll(x, seq - 1, 0))
        return w[0:1, :] * xm + w[1:2, :] * x + w[2:3, :] * xp + b_ref[...]

    x0 = conv(x0_ref, c0_ref, b0_ref)
    uu = conv(x1_ref, c1_ref, b1_ref) * conv(v_ref, c2_ref, b2_ref)
    spec = jnp.dot(fwd_ref[...], uu.astype(BF16), preferred_element_type=F32)
    u_re = spec[:seq, :]
    u_im = spec[seq:, :]
    hb = hb_ref[...]
    y_re = u_re * ha_ref[...] - u_im * hb
    y_im = u_re * hb + u_im * hd_ref[...]
    y = jnp.concatenate([y_re, y_im], axis=0).astype(BF16)
    cv = jnp.dot(inv_ref[...], y, preferred_element_type=F32)
    o_ref[...] = (x0 * (cv + uu * skip_ref[...])).astype(o_ref.dtype)


def _hyena_call(hy, conv_w, conv_b, ha, hb, hd, skip, fwd, inv, bsz, seq, act_dt):
    cb = HY_CH_BLOCK
    nblk = HY_WIDTH // cb
    tok = lambda off: pl.BlockSpec((seq, cb), lambda b, j: (b, off + j))
    cw = lambda off: pl.BlockSpec((3, cb), lambda b, j: (0, off + j))
    bias = lambda off: pl.BlockSpec((1, cb), lambda b, j: (0, off + j))
    ch = pl.BlockSpec((seq, cb), lambda b, j: (0, j))
    const = lambda a: pl.BlockSpec(a.shape, lambda b, j: (0, 0), pipeline_mode=pl.Buffered(1))
    conv_b2 = conv_b.reshape(1, -1)
    return pl.pallas_call(
        functools.partial(_hyena_kernel, seq),
        grid=(bsz, nblk),
        in_specs=[tok(0), tok(nblk), tok(2 * nblk), cw(0), cw(nblk), cw(2 * nblk),
                  bias(0), bias(nblk), bias(2 * nblk), ch, ch, ch,
                  pl.BlockSpec((1, cb), lambda b, j: (0, j)), const(fwd), const(inv)],
        out_specs=tok(0),
        out_shape=jax.ShapeDtypeStruct((bsz * seq, HY_WIDTH), act_dt),
        compiler_params=_cparams(2),
        name="hyena",
    )(hy, hy, hy, conv_w, conv_w, conv_w, conv_b2, conv_b2, conv_b2, ha, hb, hd,
      skip.reshape(1, -1), fwd, inv)


def _mix_kernel(has_pos, *refs):
    if has_pos:
        x_ref, pos_ref = refs[:2]
        refs = refs[2:]
        x = x_ref[...] + pos_ref[...]
    else:
        x_ref = refs[0]
        refs = refs[1:]
        x = x_ref[...]
    (oa_ref, ob_ref, g_ref, mod_ref, wua, wub, wout, l1g, l1b, wr_hi, wr_lo, x1_o, u2_o, p_o) = refs
    d = D_MODEL
    gates = g_ref[...].astype(F32)
    ga = _sigmoid(gates[:, :d])
    gb = _sigmoid(gates[:, d:])
    up = (ga * jnp.dot(oa_ref[...].astype(BF16), wua[...], preferred_element_type=F32)
          + gb * jnp.dot(ob_ref[...].astype(BF16), wub[...], preferred_element_type=F32))
    mixed = jnp.dot(up.astype(BF16), wout[...], preferred_element_type=F32)
    g1 = mod_ref[0, 2]
    sh2 = mod_ref[0, 3]
    sc2 = mod_ref[0, 4]
    x1 = _ln(ALPHA * x + g1 * mixed) * l1g[...] + l1b[...]
    x1_o[...] = x1
    u2 = _ln(x1) * (1.0 + sc2) + sh2
    u2_hi = u2.astype(BF16)
    u2_o[...] = u2_hi
    u2_lo = (u2 - u2_hi.astype(F32)).astype(BF16)
    logits = (jnp.dot(u2_hi, wr_hi[...], preferred_element_type=F32)
              + jnp.dot(u2_hi, wr_lo[...], preferred_element_type=F32)
              + jnp.dot(u2_lo, wr_hi[...], preferred_element_type=F32))
    lane = _iota(logits.shape, 1)
    logits = jnp.where(lane < N_EXPERTS, logits, -jnp.inf)
    m = jnp.max(logits, axis=-1, keepdims=True)
    e = jnp.exp(logits - m)
    p_o[...] = e / jnp.sum(e, axis=-1, keepdims=True)


def _mix_call(x2d, pos, o_a, o_b, gates, mod4, mod_row, wua, wub, wout, l1g, l1b, wr_hi, wr_lo):
    n, d = x2d.shape
    tm = MIX_TILE
    row = lambda i: (i, 0)
    const = lambda i: (0, 0)
    in_specs = [pl.BlockSpec((tm, d), row)]
    args = [x2d]
    if pos is not None:
        tiles = pos.shape[0] // tm
        in_specs.append(pl.BlockSpec((tm, d), lambda i: (i % tiles, 0)))
        args.append(pos)
    in_specs += [pl.BlockSpec((tm, DN_WIDTH), row), pl.BlockSpec((tm, HY_WIDTH), row),
                 pl.BlockSpec((tm, 2 * d), row),
                 _mod_spec(lambda i: mod_row(i * tm))]
    args += [o_a, o_b, gates, mod4]
    for w in (wua, wub, wout, l1g, l1b, wr_hi, wr_lo):
        in_specs.append(pl.BlockSpec(w.shape, const, pipeline_mode=pl.Buffered(1)))
        args.append(w)
    return pl.pallas_call(
        functools.partial(_mix_kernel, pos is not None),
        grid=(n // tm,),
        in_specs=in_specs,
        out_specs=[pl.BlockSpec((tm, d), row), pl.BlockSpec((tm, d), row), pl.BlockSpec((tm, LANES), row)],
        out_shape=[jax.ShapeDtypeStruct((n, d), F32), jax.ShapeDtypeStruct((n, d), BF16),
                   jax.ShapeDtypeStruct((n, LANES), F32)],
        compiler_params=_cparams(1),
        name="mix",
    )(*args)


def _route_kernel(seq, cap, p_ref, u_ref, tri_ref, xs_ref, g_ref, pt_ref):
    e_n = N_EXPERTS
    p = p_ref[...]
    pt = p.T[:e_n, :]
    tri = tri_ref[...]

    if seq <= RANK_COUNT_MAX_SEQ:
        earlier = _iota((seq, seq), 0) < _iota((seq, seq), 1)
        ranks = []
        for e in range(e_n):
            pc = p[:, e:e + 1]
            pr = pt[e:e + 1, :]
            beats = (pc > pr) | (earlier & (pc == pr))
            ranks.append(jnp.sum(jnp.where(beats, 1.0, 0.0), axis=0, keepdims=True))
        sel = jnp.concatenate(ranks, axis=0) < cap
    else:
        def search(i, cur):
            cand = cur | (1 << (30 - i))
            cand_f = pltpu.bitcast(cand, F32)
            cnt = jnp.sum(jnp.where(pt >= cand_f, 1.0, 0.0), axis=1, keepdims=True)
            return jnp.where((cnt >= cap) & (cand >= MIN_NORMAL_F32_BITS), cand, cur)

        thr = pltpu.bitcast(lax.fori_loop(0, 31, search, jnp.zeros((e_n, 1), jnp.int32)), F32)
        gt = pt > thr
        eq = pt == thr
        n_gt = jnp.sum(jnp.where(gt, 1.0, 0.0), axis=1, keepdims=True)
        eq_rank = jnp.dot(jnp.where(eq, 1.0, 0.0).astype(BF16), tri, preferred_element_type=F32)
        sel = gt | (eq & (eq_rank < cap - n_gt))
    pos = jnp.dot(jnp.where(sel, 1.0, 0.0).astype(BF16), tri, preferred_element_type=F32)
    slot = jnp.where(sel, pos, -1.0)
    pt_ref[0] = pt
    jrow = _iota((cap, seq), 0).astype(F32)
    for e in range(e_n):
        g_ref[0, e * cap:(e + 1) * cap, :] = jnp.where(slot[e:e + 1, :] == jrow, 1.0, 0.0).astype(BF16)
    xs = jnp.dot(g_ref[0], u_ref[...], preferred_element_type=F32)
    for e in range(e_n):
        xs_ref[e] = xs[e * cap:(e + 1) * cap, :].astype(xs_ref.dtype)


def _route_call(probs, u2, bsz, seq):
    cap = EC_FACTOR * seq // N_EXPERTS
    tri = jnp.asarray(np.triu(np.ones((seq, seq), np.float32), 1), dtype=BF16)
    return pl.pallas_call(
        functools.partial(_route_kernel, seq, cap),
        grid=(bsz,),
        in_specs=[pl.BlockSpec((seq, LANES), lambda b: (b, 0)),
                  pl.BlockSpec((seq, D_MODEL), lambda b: (b, 0)),
                  pl.BlockSpec((seq, seq), lambda b: (0, 0), pipeline_mode=pl.Buffered(1))],
        out_specs=[pl.BlockSpec((N_EXPERTS, cap, D_MODEL), lambda b: (0, b, 0)),
                   pl.BlockSpec((1, N_EXPERTS * cap, seq), lambda b: (b, 0, 0)),
                   pl.BlockSpec((1, N_EXPERTS, seq), lambda b: (b, 0, 0))],
        out_shape=[jax.ShapeDtypeStruct((N_EXPERTS, bsz * cap, D_MODEL), BF16),
                   jax.ShapeDtypeStruct((bsz, N_EXPERTS * cap, seq), BF16),
                   jax.ShapeDtypeStruct((bsz, N_EXPERTS, seq), F32)],
        compiler_params=_cparams(1),
        name="route",
    )(probs, u2, tri)


def _expert_kernel(xc_ref, xd_ref, wg_ref, wu_ref, wd_ref, yc_ref, yd_ref, accc, accd):
    j = pl.program_id(1)
    wg = wg_ref[0].astype(BF16)
    wu = wu_ref[0].astype(BF16)
    wd = wd_ref[0].astype(BF16)
    for x_ref, y_ref, acc in ((xc_ref, yc_ref, accc), (xd_ref, yd_ref, accd)):
        x = x_ref[0]
        hg = jnp.dot(x, wg, preferred_element_type=F32)
        hu = jnp.dot(x, wu, preferred_element_type=F32)
        part = jnp.dot((_silu(hg) * hu).astype(BF16), wd, preferred_element_type=F32)

        @pl.when(j == 0)
        def _():
            acc[...] = part

        @pl.when(j > 0)
        def _():
            acc[...] += part

        @pl.when(j == pl.num_programs(1) - 1)
        def _():
            y_ref[0] = acc[...].astype(y_ref.dtype)


def _expert_call(xs_c, xs_d, w_gate, w_up, w_down):
    e_n, rc, d = xs_c.shape
    rd = xs_d.shape[1]
    ff = w_gate.shape[2]
    fb = FF_BLOCK
    return pl.pallas_call(
        _expert_kernel,
        grid=(e_n, ff // fb),
        in_specs=[pl.BlockSpec((1, rc, d), lambda e, j: (e, 0, 0)),
                  pl.BlockSpec((1, rd, d), lambda e, j: (e, 0, 0)),
                  pl.BlockSpec((1, d, fb), lambda e, j: (e, 0, j)),
                  pl.BlockSpec((1, d, fb), lambda e, j: (e, 0, j)),
                  pl.BlockSpec((1, fb, d), lambda e, j: (e, j, 0))],
        out_specs=[pl.BlockSpec((1, rc, d), lambda e, j: (e, 0, 0)),
                   pl.BlockSpec((1, rd, d), lambda e, j: (e, 0, 0))],
        out_shape=[jax.ShapeDtypeStruct((e_n, rc, d), BF16), jax.ShapeDtypeStruct((e_n, rd, d), BF16)],
        scratch_shapes=[pltpu.VMEM((rc, d), F32), pltpu.VMEM((rd, d), F32)],
        compiler_params=_cparams(2),
        name="experts",
    )(xs_c, xs_d, w_gate, w_up, w_down)


def _final_kernel(seq, cap, x1_ref, y_ref, g_ref, pt_ref, mod_ref, l2g, l2b, o_ref, yw_s):
    pt = pt_ref[0]
    for e in range(N_EXPERTS):
        onehot = g_ref[0, e * cap:(e + 1) * cap, :].astype(F32)
        aff = jnp.sum(onehot * pt[e:e + 1, :], axis=1, keepdims=True)
        yw_s[e * cap:(e + 1) * cap, :] = (y_ref[e].astype(F32) * aff).astype(BF16)
    ffn = lax.dot_general(g_ref[0], yw_s[...], (((0,), (0,)), ((), ())), preferred_element_type=F32)
    g2 = mod_ref[0, 5]
    o_ref[...] = _ln(ALPHA * x1_ref[...] + g2 * ffn) * l2g[...] + l2b[...]


def _final_call(x1, ys, gmat, pt, mod4, mod_row, l2g, l2b, bsz, seq):
    cap = EC_FACTOR * seq // N_EXPERTS
    d = D_MODEL
    return pl.pallas_call(
        functools.partial(_final_kernel, seq, cap),
        grid=(bsz,),
        in_specs=[pl.BlockSpec((seq, d), lambda b: (b, 0)),
                  pl.BlockSpec((N_EXPERTS, cap, d), lambda b: (0, b, 0)),
                  pl.BlockSpec((1, N_EXPERTS * cap, seq), lambda b: (b, 0, 0)),
                  pl.BlockSpec((1, N_EXPERTS, seq), lambda b: (b, 0, 0)),
                  _mod_spec(lambda b: mod_row(b * seq)),
                  pl.BlockSpec((1, d), lambda b: (0, 0)),
                  pl.BlockSpec((1, d), lambda b: (0, 0))],
        out_specs=pl.BlockSpec((seq, d), lambda b: (b, 0)),
        out_shape=jax.ShapeDtypeStruct((bsz * seq, d), F32),
        scratch_shapes=[pltpu.VMEM((N_EXPERTS * cap, d), BF16)],
        compiler_params=_cparams(1),
        name="final",
    )(x1, ys, gmat, pt, mod4, l2g, l2b)


def _grid_pos_embed(rows, dim):
    r = np.repeat(np.arange(rows), GRID_W)
    col = np.tile(np.arange(GRID_W), rows)
    quarter = dim // 4
    omega = 1.0 / (10000.0 ** (np.arange(quarter) / quarter))

    def enc(p):
        ang = p[:, None] * omega[None, :]
        return np.concatenate([np.sin(ang), np.cos(ang)], -1)

    return jnp.asarray(np.concatenate([enc(r), enc(col)], -1), dtype=F32)


ACT_DT = BF16


def kernel(x_prompt, x_sample, state_delta_fwd, state_delta_bwd, c, c_ctx, w_mod, b_mod, w_in, conv_a_w, a_log, dt_bias, gnorm_w, conv_b_w, conv_b_b, hy_w1, hy_b1, hy_w2, hy_b2, hy_w3, hy_freq, hy_bias, w_up_a, w_up_b, w_out, ln1_g, ln1_b, w_router, w_e_gate, w_e_up, w_e_down, ln2_g, ln2_b):
    d = D_MODEL
    n_ctx, l_ctx, _ = x_prompt.shape
    n_dec, l_dec, _ = x_sample.shape
    lyr = 0

    cond_t = jnp.zeros((d, 8), F32).at[:, 0].set(c_ctx).at[:, 1:1 + n_dec].set(c.T)
    mod4 = _mod_call(cond_t, 1 + n_dec, w_mod[lyr], b_mod[lyr]).reshape(8, N_MOD, 1, d)

    w = w_in[lyr].astype(BF16)
    o_z = QKV_WIDTH
    o_b = o_z + DN_WIDTH
    o_hy = o_b + 4 * DN_HEADS
    o_g = o_hy + 3 * HY_WIDTH
    w_ba = jnp.pad(w[:, o_b:o_hy], ((0, 0), (0, LANES - 4 * DN_HEADS)))
    in_w = (w[:, :o_z], w[:, o_z:o_b], w_ba, w[:, o_hy:o_g], w[:, o_g:])

    prm = jnp.zeros((8, LANES), F32)
    prm = prm.at[0, 2 * DN_HEADS:4 * DN_HEADS].set(a_log[lyr].reshape(-1))
    prm = prm.at[1, 2 * DN_HEADS:4 * DN_HEADS].set(dt_bias[lyr].reshape(-1))
    gn = gnorm_w[lyr].reshape(1, -1)

    wua = w_up_a[lyr].astype(BF16)
    wub = w_up_b[lyr].astype(BF16)
    wout = w_out[lyr].astype(BF16)
    wr = jnp.pad(w_router[lyr], ((0, 0), (0, LANES - N_EXPERTS)))
    wr_hi = wr.astype(BF16)
    wr_lo = (wr - wr_hi.astype(F32)).astype(BF16)
    l1g, l1b = ln1_g[lyr].reshape(1, -1), ln1_b[lyr].reshape(1, -1)
    l2g, l2b = ln2_g[lyr].reshape(1, -1), ln2_b[lyr].reshape(1, -1)

    def front(x2d, pos, bsz, seq, mod_row, s0f, s0b):
        qkv, z, ba, hy, gates = _inproj_call(x2d, pos, mod4, mod_row, in_w, ACT_DT)
        o_a, s_f, s_b = _deltanet_call(qkv, z, ba, conv_a_w[lyr], prm, gn, s0f, s0b, bsz, seq, ACT_DT)
        fwd, inv = (jnp.asarray(m) for m in _dft_mats(seq))
        fwd_hi = fwd.astype(BF16)
        fwd_lo = (fwd - fwd_hi.astype(F32)).astype(BF16)
        ha, hb, hd = _hfilter_call(seq, hy_w1[lyr], hy_b1[lyr], hy_w2[lyr], hy_b2[lyr], hy_w3[lyr],
                                   hy_freq[lyr], fwd_hi, fwd_lo)
        o_h = _hyena_call(hy, conv_b_w[lyr], conv_b_b[lyr], ha, hb, hd, hy_bias[lyr],
                          fwd_hi, inv.astype(BF16), bsz, seq, ACT_DT)
        x1, u2, probs = _mix_call(x2d, pos, o_a, o_h, gates, mod4, mod_row, wua, wub, wout, l1g, l1b,
                                   wr_hi, wr_lo)
        xs, gmat, pt = _route_call(probs, u2, bsz, seq)
        return x1, xs, gmat, pt, s_f, s_b

    row_ctx = lambda tok: 0
    row_dec = lambda tok: 1 + tok // l_dec

    xc = x_prompt.reshape(n_ctx * l_ctx, d)
    xd = x_sample.reshape(n_dec * l_dec, d)
    pos = _grid_pos_embed(l_dec // GRID_W, d)

    x1c, xsc, gc, ptc, s_f, s_b = front(xc, None, n_ctx, l_ctx, row_ctx, None, None)
    x1d, xsd, gd, ptd, _, _ = front(xd, pos, n_dec, l_dec, row_dec,
                                    state_delta_fwd[:, lyr], state_delta_bwd[:, lyr])
    yc, yd = _expert_call(xsc, xsd, w_e_gate[lyr], w_e_up[lyr], w_e_down[lyr])
    y_prompt = _final_call(x1c, yc, gc, ptc, mod4, row_ctx, l2g, l2b, n_ctx, l_ctx)
    y_sample = _final_call(x1d, yd, gd, ptd, mod4, row_dec, l2g, l2b, n_dec, l_dec)

    return (y_prompt.reshape(n_ctx, l_ctx, d), y_sample.reshape(n_dec, l_dec, d),
            s_f[:, None], s_b[:, None])
```

```python
import functools
import math

import jax
import jax.numpy as jnp
import numpy as np
from jax import lax
from jax.experimental import pallas as pl
from jax.experimental.pallas import tpu as pltpu

F32 = jnp.float32
BF16 = jnp.bfloat16
HIGHEST = lax.Precision.HIGHEST

D_MODEL = 1024
DEPTH = 1
GRID_W = 64
DN_HEADS = 4
DN_DK = 128
DN_DV = 128
DN_WIDTH = DN_HEADS * DN_DV
HY_WIDTH = D_MODEL // 2
HY_EMB = 33
HY_FFN = 64
HY_TARGET = 1e-2
HY_FAST_DECAY = 0.3
HY_SLOW_DECAY = 1.5
N_EXPERTS = 16
EC_FACTOR = 2
N_MOD = 6
ALPHA = (2 * DEPTH) ** 0.25
LN_EPS = 1e-5
RMS_EPS = 1e-6
QKV_WIDTH = 2 * DN_HEADS * DN_DK + DN_WIDTH

LANES = 128
INPROJ_TILE = 512
MIX_TILE = 256
MOD_COL_BLOCK = 1024
DN_CHUNK = 256
INV_BASE = 16
HY_CH_BLOCK = 512
EXPERT_ROW_BLOCK = 512
VMEM_LIMIT = 56 * 1024 * 1024
MIN_NORMAL_F32_BITS = 0x00800000
RANK_COUNT_MAX_SEQ = 256


def _cparams(n_axes):
    return pltpu.CompilerParams(dimension_semantics=("arbitrary",) * n_axes,
                                vmem_limit_bytes=VMEM_LIMIT)


def _mm(a, b):
    return jnp.dot(a.astype(BF16), b.astype(BF16), preferred_element_type=F32)


def _mm_nt(a, b):
    return lax.dot_general(a.astype(BF16), b.astype(BF16), (((1,), (1,)), ((), ())),
                           preferred_element_type=F32)


def _bmm(a, b):
    return jnp.einsum("bmk,bkn->bmn", a.astype(BF16), b.astype(BF16), preferred_element_type=F32)


def _bmm_nt(a, b):
    return jnp.einsum("bmk,bnk->bmn", a.astype(BF16), b.astype(BF16), preferred_element_type=F32)


def _bmm_tn(a, b):
    return jnp.einsum("bkm,bkn->bmn", a.astype(BF16), b.astype(BF16), preferred_element_type=F32)


def _mm_f32(a, b):
    return jnp.dot(a, b, precision=HIGHEST, preferred_element_type=F32)


def _sigmoid(x):
    return 0.5 * jnp.tanh(0.5 * x) + 0.5


def _silu(x):
    return x * _sigmoid(x)


def _softplus(x):
    return jnp.maximum(x, 0.0) + jnp.log1p(jnp.exp(-jnp.abs(x)))


def _ln(x):
    mu = jnp.mean(x, axis=-1, keepdims=True)
    xc = x - mu
    var = jnp.mean(xc * xc, axis=-1, keepdims=True)
    return xc * lax.rsqrt(var + LN_EPS)


def _iota(shape, dim):
    return lax.broadcasted_iota(jnp.int32, shape, dim)


def _mod_kernel(n_rows, ct_ref, w_ref, b_ref, o_ref):
    s = _silu(ct_ref[...])
    w = w_ref[...]
    b = b_ref[...]
    o_ref[...] = jnp.zeros(o_ref.shape, F32)
    for r in range(n_rows):
        o_ref[r:r + 1, :] = jnp.sum(w * s[:, r:r + 1], axis=0, keepdims=True) + b


def _mod_call(cond_t, n_rows, w_mod, b_mod):
    d = D_MODEL
    tn = MOD_COL_BLOCK
    return pl.pallas_call(
        functools.partial(_mod_kernel, n_rows),
        grid=(N_MOD * d // tn,),
        in_specs=[pl.BlockSpec((d, 8), lambda j: (0, 0)),
                  pl.BlockSpec((d, tn), lambda j: (0, j)),
                  pl.BlockSpec((1, tn), lambda j: (0, j))],
        out_specs=pl.BlockSpec((8, tn), lambda j: (0, j)),
        out_shape=jax.ShapeDtypeStruct((8, N_MOD * d), F32),
        compiler_params=_cparams(1),
        name="mod",
    )(cond_t, w_mod, b_mod.reshape(1, -1))


def _mod_spec(row_of_step):
    return pl.BlockSpec((1, N_MOD, 1, D_MODEL), lambda *idx: (row_of_step(*idx), 0, 0, 0))


def _inproj_kernel(has_pos, *refs):
    if has_pos:
        x_ref, pos_ref, mod_ref, w_ref, qkv_o, z_o, ba_o, hy_o, g_o = refs
        x = x_ref[...] + pos_ref[...]
    else:
        x_ref, mod_ref, w_ref, qkv_o, z_o, ba_o, hy_o, g_o = refs
        x = x_ref[...]
    sh1 = mod_ref[0, 0]
    sc1 = mod_ref[0, 1]
    u = (_ln(x) * (1.0 + sc1) + sh1).astype(BF16)
    o_z = QKV_WIDTH
    o_ba = o_z + DN_WIDTH
    n_ba = 4 * DN_HEADS
    o_g = n_ba + 3 * HY_WIDTH
    head = jnp.dot(u, w_ref[:, :o_ba], preferred_element_type=F32)
    qkv_o[...] = head[:, :o_z].astype(qkv_o.dtype)
    z_o[...] = head[:, o_z:].astype(z_o.dtype)
    tail = jnp.dot(u, w_ref[:, o_ba:], preferred_element_type=F32)
    ba_o[...] = tail[:, :LANES]
    hy_o[...] = tail[:, n_ba:o_g].astype(hy_o.dtype)
    g_o[...] = tail[:, o_g:].astype(g_o.dtype)


def _inproj_call(x2d, pos, mod4, mod_row, w, act_dt):
    n, d = x2d.shape
    tm = INPROJ_TILE
    row = lambda i: (i, 0)
    const = lambda i: (0, 0)
    in_specs = [pl.BlockSpec((tm, d), row)]
    args = [x2d]
    if pos is not None:
        tiles = pos.shape[0] // tm
        in_specs.append(pl.BlockSpec((tm, d), lambda i: (i % tiles, 0)))
        args.append(pos)
    in_specs.append(_mod_spec(lambda i: mod_row(i * tm)))
    args.append(mod4)
    in_specs.append(pl.BlockSpec(w.shape, const, pipeline_mode=pl.Buffered(1)))
    args.append(w)
    widths = (QKV_WIDTH, DN_WIDTH, LANES, 3 * HY_WIDTH, 2 * D_MODEL)
    dts = (act_dt, act_dt, F32, act_dt, act_dt)
    return pl.pallas_call(
        functools.partial(_inproj_kernel, pos is not None),
        grid=(n // tm,),
        in_specs=in_specs,
        out_specs=[pl.BlockSpec((tm, w), row) for w in widths],
        out_shape=[jax.ShapeDtypeStruct((n, w), dt) for w, dt in zip(widths, dts)],
        compiler_params=_cparams(1),
        name="inproj",
    )(*args)


def _inv_unit_tri(lm, ri, ci, block):
    c = lm.shape[-1]

    def same_block(h):
        sh = h.bit_length() - 1
        return (ri >> sh) == (ci >> sh)

    def unfold(f, h):
        return jnp.where(same_block(h), jnp.concatenate([f] * (c // h), axis=1), 0.0)

    base = INV_BASE
    n_nat = jnp.where(same_block(base), -lm, 0.0)
    n_fold = n_nat[:, 0:base, :]
    for i in range(1, c // base):
        n_fold = n_fold + n_nat[:, i * base:(i + 1) * base, :]
    eye_fold = jnp.where((_iota((base, c), 1) & (base - 1)) == _iota((base, c), 0), 1.0, 0.0)
    t = eye_fold + n_fold
    p_fold, p_nat = n_fold, n_nat
    span = 2
    while span < base:
        p_fold = _bmm(p_fold, p_nat)
        p_nat = unfold(p_fold, base)
        t = t + _bmm(t, p_nat)
        span *= 2
    s = base
    while s < block:
        sh = s.bit_length() - 1
        t_nat = unfold(t, s)
        even = ((_iota((s, c), 1) >> sh) & 1) == 0
        t2 = jnp.concatenate([jnp.where(even, t, 0.0), jnp.where(even, 0.0, t)], axis=1)
        lo = jnp.where(same_block(2 * s) & ~same_block(s), lm, 0.0)
        t = t2 - _bmm(_bmm(t2, lo), t_nat)
        s *= 2
    return t


def _dn_chunk(q, k, v, gc_col, gc_row, beta, e_col, e_rest, e_all, s_prev, same_chunk, ri, ci):
    c = DN_CHUNK
    nh = DN_HEADS
    half = c // 2
    if same_chunk:
        gram = _bmm_nt(k[:nh], k[:nh])
        qk = _bmm_nt(q[:nh], k[:nh])
        gram = jnp.concatenate([gram, gram], 0)
        qk = jnp.concatenate([qk, qk], 0)
    else:
        gram = _bmm_nt(k, k)
        qk = _bmm_nt(q, k)
    diff = gc_col - gc_row
    incl = jnp.concatenate([jnp.broadcast_to(ri >= ci, (nh, c, c)), jnp.broadcast_to(ri <= ci, (nh, c, c))], 0)
    strict = incl & (ri != ci)
    decay = jnp.exp(jnp.where(incl, diff, -jnp.inf))
    lm = jnp.where(strict, beta * gram * decay, 0.0)
    a_intra = qk * decay
    t = _inv_unit_tri(lm, ri, ci, half)
    t1 = t[:, :, :half]
    t2 = t[:, :, half:]
    rhs = jnp.concatenate([v * beta, k * (beta * e_col)], axis=2)
    y1 = _bmm(t1, rhs[:, :half, :])
    y2 = _bmm(t2, rhs[:, half:, :])
    c_f = _bmm(lm[:nh, half:, :half], y1[:nh])
    c_b = _bmm(lm[nh:, :half, half:], y2[nh:])
    c_f = _bmm(t2[:nh], c_f)
    c_b = _bmm(t1[nh:], c_b)
    sol = jnp.concatenate([jnp.concatenate([y1[:nh], y2[:nh] - c_f], 1),
                           jnp.concatenate([y1[nh:] - c_b, y2[nh:]], 1)], 0)
    u = sol[:, :, :DN_DV]
    w = sol[:, :, DN_DV:]
    ks = k * e_rest
    if s_prev is None:
        v_new = u
        o = _bmm(a_intra, v_new)
        s_new = _bmm_tn(ks, v_new)
    else:
        v_new = u - _bmm(w, s_prev)
        o = _bmm(q * e_col, s_prev) + _bmm(a_intra, v_new)
        s_new = s_prev * e_all + _bmm_tn(ks, v_new)
    return o, s_new


def _deltanet_kernel(seq, zero_init, *refs):
    qkv_ref, cw_ref, z_ref, ba_ref, prm_ref, gn_ref = refs[:6]
    refs = refs[6:]
    if not zero_init:
        s0f_ref, s0b_ref = refs[:2]
        refs = refs[2:]
    o_ref, sf_ref, sb_ref, q_s, k_s, v_s, g_s, b_s, o_s, st_s = refs
    c = DN_CHUNK
    n = seq // c
    nh = DN_HEADS

    row = _iota((seq, LANES), 0)
    first_row = row == 0
    last_row = row == seq - 1

    def conv_silu(j):
        cols = slice(j * LANES, (j + 1) * LANES)
        x = qkv_ref[:, cols].astype(F32)
        w = cw_ref[:, cols]
        xm = jnp.where(first_row, 0.0, pltpu.roll(x, 1, 0))
        xp = jnp.where(last_row, 0.0, pltpu.roll(x, seq - 1, 0))
        return _silu(w[0:1, :] * xm + w[1:2, :] * x + w[2:3, :] * xp)

    for h in range(nh):
        q = conv_silu(h)
        q_s[h] = q * lax.rsqrt(jnp.sum(q * q, axis=-1, keepdims=True) + 1e-6) * (DN_DK ** -0.5)
        k = conv_silu(nh + h)
        k_s[h] = k * lax.rsqrt(jnp.sum(k * k, axis=-1, keepdims=True) + 1e-6)
        v_s[h] = conv_silu(2 * nh + h)

    ba = ba_ref[...]
    b_s[...] = _sigmoid(ba)
    g_s[...] = -jnp.exp(prm_ref[0:1, :]) * _softplus(ba + prm_ref[1:2, :])

    ri = _iota((c, c), 0)
    ci = _iota((c, c), 1)
    tri_l = jnp.where(ri >= ci, 1.0, 0.0).astype(F32)
    tri_u = jnp.where(ri <= ci, 1.0, 0.0).astype(F32)

    def chunk_pair(idx_f, idx_b, s_prev):
        def ds(idx):
            start = idx * c
            return pl.ds(start if isinstance(start, int) else pl.multiple_of(start, c), c)

        sl_f, sl_b = ds(idx_f), ds(idx_b)
        cs_f = _mm_f32(tri_l, g_s[sl_f, :])
        cs_b = _mm_f32(tri_u, g_s[sl_b, :])
        cst_f = cs_f.T
        cst_b = cs_b.T
        b_f = b_s[sl_f, :]
        b_b = b_s[sl_b, :]
        heads = range(nh)
        gc_col = jnp.stack([cs_f[:, 2 * nh + h:2 * nh + h + 1] for h in heads]
                           + [cs_b[:, 3 * nh + h:3 * nh + h + 1] for h in heads], 0)
        gc_row = jnp.stack([cst_f[2 * nh + h:2 * nh + h + 1, :] for h in heads]
                           + [cst_b[3 * nh + h:3 * nh + h + 1, :] for h in heads], 0)
        beta = jnp.stack([b_f[:, h:h + 1] for h in heads] + [b_b[:, nh + h:nh + h + 1] for h in heads], 0)
        e_col = jnp.exp(gc_col)
        gl = jnp.concatenate([gc_col[:nh, c - 1:c, :], gc_col[nh:, 0:1, :]], 0)
        e_rest = jnp.exp(gl - gc_col)
        e_all = jnp.exp(gl)
        both = lambda ref: jnp.concatenate([ref[:, sl_f, :], ref[:, sl_b, :]], 0)
        o, s_new = _dn_chunk(both(q_s), both(k_s), both(v_s), gc_col, gc_row, beta, e_col, e_rest, e_all,
                             s_prev, n == 1, ri, ci)
        o_s[:nh, sl_f, :] = o[:nh]
        o_s[nh:, sl_b, :] = o[nh:]
        return s_new

    if n == 1:
        s0 = None if zero_init else jnp.concatenate([s0f_ref[0], s0b_ref[0]], 0)
        s_fin = chunk_pair(0, 0, s0)
    else:
        if zero_init:
            st_s[...] = jnp.zeros(st_s.shape, F32)
        else:
            st_s[:nh] = s0f_ref[0]
            st_s[nh:] = s0b_ref[0]

        def body(i, carry):
            st_s[...] = chunk_pair(i, n - 1 - i, st_s[...])
            return carry

        lax.fori_loop(0, n, body, 0)
        s_fin = st_s[...]
    sf_ref[0] = s_fin[:nh]
    sb_ref[0] = s_fin[nh:]

    gn = gn_ref[...]
    for h in range(nh):
        cols = slice(h * LANES, (h + 1) * LANES)
        o = o_s[h] + o_s[nh + h]
        o = o * lax.rsqrt(jnp.mean(o * o, axis=-1, keepdims=True) + RMS_EPS) * gn
        o_ref[:, cols] = (o * _silu(z_ref[:, cols].astype(F32))).astype(o_ref.dtype)


def _deltanet_call(qkv, z, ba, conv_w, prm, gn, s0f, s0b, bsz, seq, act_dt):
    nh = DN_HEADS
    zero_init = s0f is None
    st = pl.BlockSpec((1, nh, DN_DK, DN_DV), lambda b: (b, 0, 0, 0))
    in_specs = [pl.BlockSpec((seq, QKV_WIDTH), lambda b: (b, 0)),
                pl.BlockSpec((3, QKV_WIDTH), lambda b: (0, 0)),
                pl.BlockSpec((seq, DN_WIDTH), lambda b: (b, 0)),
                pl.BlockSpec((seq, LANES), lambda b: (b, 0)),
                pl.BlockSpec((8, LANES), lambda b: (0, 0)),
                pl.BlockSpec((1, LANES), lambda b: (0, 0))]
    args = [qkv, conv_w, z, ba, prm, gn]
    if not zero_init:
        in_specs += [st, st]
        args += [s0f, s0b]
    vm = lambda shape: pltpu.VMEM(shape, F32)
    per_head = vm((nh, seq, LANES))
    return pl.pallas_call(
        functools.partial(_deltanet_kernel, seq, zero_init),
        grid=(bsz,),
        in_specs=in_specs,
        out_specs=[pl.BlockSpec((seq, DN_WIDTH), lambda b: (b, 0)), st, st],
        out_shape=[jax.ShapeDtypeStruct((bsz * seq, DN_WIDTH), act_dt),
                   jax.ShapeDtypeStruct((bsz, nh, DN_DK, DN_DV), F32),
                   jax.ShapeDtypeStruct((bsz, nh, DN_DK, DN_DV), F32)],
        scratch_shapes=[per_head, per_head, per_head, vm((seq, LANES)), vm((seq, LANES)),
                        vm((2 * nh, seq, LANES)), vm((2 * nh, DN_DK, DN_DV))],
        compiler_params=_cparams(1),
        name="deltanet",
    )(*args)


def _dft_mats(seq):
    n = 2 * seq
    f = np.arange(seq)[:, None]
    s = np.arange(seq)[None, :]
    ang = 2.0 * np.pi * ((f * s) % n) / n
    fwd = np.concatenate([np.cos(ang), -np.sin(ang)], axis=0)
    fwd[seq, :] = np.cos(np.pi * np.arange(seq))
    t = (np.arange(seq) + seq // 2)[:, None]
    ff = np.arange(seq)[None, :]
    ang2 = 2.0 * np.pi * ((t * ff) % n) / n
    inv_r = 2.0 * np.cos(ang2) / n
    inv_i = -2.0 * np.sin(ang2) / n
    inv_r[:, 0] = 1.0 / n
    inv_i[:, 0] = np.cos(np.pi * t[:, 0]) / n
    inv = np.concatenate([inv_r, inv_i], axis=1)
    return fwd.astype(np.float32), inv.astype(np.float32)


def _filter_feats(seq):
    t = np.linspace(0.0, 1.0, seq)[:, None]
    bands = (HY_EMB - 1) // 2
    ang = (2.0 * math.pi * np.arange(seq) / seq)[:, None] * np.linspace(1e-4, bands - 1, bands)[None, :]
    feats = np.concatenate([t, np.cos(ang), -np.sin(ang)], -1)
    deltas = np.abs(np.linspace(math.log(HY_TARGET) / HY_SLOW_DECAY, math.log(HY_TARGET) / HY_FAST_DECAY,
                                HY_WIDTH))
    offset = np.abs(np.arange(seq) - seq // 2) / (seq // 2)
    window = np.exp(-offset[:, None] * deltas[None, :])
    feats = np.pad(feats, ((0, 0), (0, LANES - HY_EMB)))
    return jnp.asarray(feats, dtype=F32), jnp.asarray(window, dtype=F32)


def _hfilter_kernel(seq, feats_ref, win_ref, w1_ref, b1_ref, w2_ref, b2_ref, w3_ref, fr_ref, fwd_hi_ref,
                    fwd_lo_ref, ha_ref, hb_ref, hd_ref):
    fr = fr_ref[...]
    hid = jnp.sin(fr * (_mm_f32(feats_ref[...], w1_ref[...]) + b1_ref[...]))
    hid = jnp.sin(fr * (_mm_f32(hid, w2_ref[...]) + b2_ref[...]))
    filt = _mm_f32(hid, w3_ref[...]) * win_ref[...]
    filt = filt / (jnp.sum(jnp.abs(filt), axis=0, keepdims=True) + 1e-6)
    filt_hi = filt.astype(BF16)
    filt_lo = (filt - filt_hi.astype(F32)).astype(BF16)
    fwd_hi = fwd_hi_ref[...]
    spec = (jnp.dot(fwd_hi, filt_hi, preferred_element_type=F32)
            + jnp.dot(fwd_hi, filt_lo, preferred_element_type=F32)
            + jnp.dot(fwd_lo_ref[...], filt_hi, preferred_element_type=F32))
    h_re = spec[:seq, :]
    h_im = spec[seq:, :]
    first = _iota(h_re.shape, 0) == 0
    ha_ref[...] = h_re
    hb_ref[...] = jnp.where(first, 0.0, h_im)
    hd_ref[...] = jnp.where(first, h_im, h_re)


def _hfilter_call(seq, w1, b1, w2, b2, w3, freq, fwd_hi, fwd_lo):
    feats, window = _filter_feats(seq)
    w1p = jnp.pad(w1, ((0, LANES - HY_EMB), (0, 0)))
    full = lambda a: pl.BlockSpec(a.shape, lambda i: (0,) * a.ndim)
    args = [feats, window, w1p, b1.reshape(1, -1), w2, b2.reshape(1, -1), w3, freq.reshape(1, -1), fwd_hi,
            fwd_lo]
    out = jax.ShapeDtypeStruct((seq, HY_WIDTH), F32)
    return pl.pallas_call(
        functools.partial(_hfilter_kernel, seq),
        grid=(1,),
        in_specs=[full(a) for a in args],
        out_specs=[pl.BlockSpec((seq, HY_WIDTH), lambda i: (0, 0))] * 3,
        out_shape=[out, out, out],
        compiler_params=_cparams(1),
        name="hfilter",
    )(*args)


def _hyena_kernel(seq, x0_ref, x1_ref, v_ref, c0_ref, c1_ref, c2_ref, b0_ref, b1_ref, b2_ref,
                  ha_ref, hb_ref, hd_ref, skip_ref, fwd_ref, inv_ref, o_ref):
    row = _iota((seq, HY_CH_BLOCK), 0)

    def conv(x_ref, w_ref, b_ref):
        x = x_ref[...].astype(F32)
        w = w_ref[...]
        xm = jnp.where(row == 0, 0.0, pltpu.roll(x, 1, 0))
        xp = jnp.where(row == seq - 1, 0.0, pltpu.roll(x, seq - 1, 0))
        return w[0:1, :] * xm + w[1:2, :] * x + w[2:3, :] * xp + b_ref[...]

    x0 = conv(x0_ref, c0_ref, b0_ref)
    uu = conv(x1_ref, c1_ref, b1_ref) * conv(v_ref, c2_ref, b2_ref)
    spec = jnp.dot(fwd_ref[...], uu.astype(BF16), preferred_element_type=F32)
    u_re = spec[:seq, :]
    u_im = spec[seq:, :]
    hb = hb_ref[...]
    y_re = u_re * ha_ref[...] - u_im * hb
    y_im = u_re * hb + u_im * hd_ref[...]
    y = jnp.concatenate([y_re, y_im], axis=0).astype(BF16)
    cv = jnp.dot(inv_ref[...], y, preferred_element_type=F32)
    o_ref[...] = (x0 * (cv + uu * skip_ref[...])).astype(o_ref.dtype)


def _hyena_call(hy, conv_w, conv_b, ha, hb, hd, skip, fwd, inv, bsz, seq, act_dt):
    cb = HY_CH_BLOCK
    nblk = HY_WIDTH // cb
    tok = lambda off: pl.BlockSpec((seq, cb), lambda b, j: (b, off + j))
    cw = lambda off: pl.BlockSpec((3, cb), lambda b, j: (0, off + j))
    bias = lambda off: pl.BlockSpec((1, cb), lambda b, j: (0, off + j))
    ch = pl.BlockSpec((seq, cb), lambda b, j: (0, j))
    const = lambda a: pl.BlockSpec(a.shape, lambda b, j: (0, 0), pipeline_mode=pl.Buffered(1))
    conv_b2 = conv_b.reshape(1, -1)
    return pl.pallas_call(
        functools.partial(_hyena_kernel, seq),
        grid=(bsz, nblk),
        in_specs=[tok(0), tok(nblk), tok(2 * nblk), cw(0), cw(nblk), cw(2 * nblk),
                  bias(0), bias(nblk), bias(2 * nblk), ch, ch, ch,
                  pl.BlockSpec((1, cb), lambda b, j: (0, j)), const(fwd), const(inv)],
        out_specs=tok(0),
        out_shape=jax.ShapeDtypeStruct((bsz * seq, HY_WIDTH), act_dt),
        compiler_params=_cparams(2),
        name="hyena",
    )(hy, hy, hy, conv_w, conv_w, conv_w, conv_b2, conv_b2, conv_b2, ha, hb, hd,
      skip.reshape(1, -1), fwd, inv)


def _mix_kernel(has_pos, *refs):
    if has_pos:
        x_ref, pos_ref = refs[:2]
        refs = refs[2:]
        x = x_ref[...] + pos_ref[...]
    else:
        x_ref = refs[0]
        refs = refs[1:]
        x = x_ref[...]
    (oa_ref, ob_ref, g_ref, mod_ref, wua, wub, wout, l1g, l1b, wr_hi, wr_lo, x1_o, u2_o, p_o) = refs
    d = D_MODEL
    gates = g_ref[...].astype(F32)
    ga = _sigmoid(gates[:, :d])
    gb = _sigmoid(gates[:, d:])
    up = (ga * jnp.dot(oa_ref[...].astype(BF16), wua[...], preferred_element_type=F32)
          + gb * jnp.dot(ob_ref[...].astype(BF16), wub[...], preferred_element_type=F32))
    mixed = jnp.dot(up.astype(BF16), wout[...], preferred_element_type=F32)
    g1 = mod_ref[0, 2]
    sh2 = mod_ref[0, 3]
    sc2 = mod_ref[0, 4]
    x1 = _ln(ALPHA * x + g1 * mixed) * l1g[...] + l1b[...]
    x1_o[...] = x1
    u2 = _ln(x1) * (1.0 + sc2) + sh2
    u2_hi = u2.astype(BF16)
    u2_o[...] = u2_hi
    u2_lo = (u2 - u2_hi.astype(F32)).astype(BF16)
    logits = (jnp.dot(u2_hi, wr_hi[...], preferred_element_type=F32)
              + jnp.dot(u2_hi, wr_lo[...], preferred_element_type=F32)
              + jnp.dot(u2_lo, wr_hi[...], preferred_element_type=F32))
    lane = _iota(logits.shape, 1)
    logits = jnp.where(lane < N_EXPERTS, logits, -jnp.inf)
    m = jnp.max(logits, axis=-1, keepdims=True)
    e = jnp.exp(logits - m)
    p_o[...] = e / jnp.sum(e, axis=-1, keepdims=True)


def _mix_call(x2d, pos, o_a, o_b, gates, mod4, mod_row, wua, wub, wout, l1g, l1b, wr_hi, wr_lo):
    n, d = x2d.shape
    tm = MIX_TILE
    row = lambda i: (i, 0)
    const = lambda i: (0, 0)
    in_specs = [pl.BlockSpec((tm, d), row)]
    args = [x2d]
    if pos is not None:
        tiles = pos.shape[0] // tm
        in_specs.append(pl.BlockSpec((tm, d), lambda i: (i % tiles, 0)))
        args.append(pos)
    in_specs += [pl.BlockSpec((tm, DN_WIDTH), row), pl.BlockSpec((tm, HY_WIDTH), row),
                 pl.BlockSpec((tm, 2 * d), row),
                 _mod_spec(lambda i: mod_row(i * tm))]
    args += [o_a, o_b, gates, mod4]
    for w in (wua, wub, wout, l1g, l1b, wr_hi, wr_lo):
        in_specs.append(pl.BlockSpec(w.shape, const, pipeline_mode=pl.Buffered(1)))
        args.append(w)
    return pl.pallas_call(
        functools.partial(_mix_kernel, pos is not None),
        grid=(n // tm,),
        in_specs=in_specs,
        out_specs=[pl.BlockSpec((tm, d), row), pl.BlockSpec((tm, d), row), pl.BlockSpec((tm, LANES), row)],
        out_shape=[jax.ShapeDtypeStruct((n, d), F32), jax.ShapeDtypeStruct((n, d), BF16),
                   jax.ShapeDtypeStruct((n, LANES), F32)],
        compiler_params=_cparams(1),
        name="mix",
    )(*args)


def _route_kernel(seq, cap, p_ref, u_ref, tri_ref, xs_ref, g_ref, pt_ref):
    e_n = N_EXPERTS
    p = p_ref[...]
    pt = p.T[:e_n, :]
    tri = tri_ref[...]

    if seq <= RANK_COUNT_MAX_SEQ:
        earlier = _iota((seq, seq), 0) < _iota((seq, seq), 1)
        ranks = []
        for e in range(e_n):
            pc = p[:, e:e + 1]
            pr = pt[e:e + 1, :]
            beats = (pc > pr) | (earlier & (pc == pr))
            ranks.append(jnp.sum(jnp.where(beats, 1.0, 0.0), axis=0, keepdims=True))
        sel = jnp.concatenate(ranks, axis=0) < cap
    else:
        def search(i, cur):
            cand = cur | (1 << (30 - i))
            cand_f = pltpu.bitcast(cand, F32)
            cnt = jnp.sum(jnp.where(pt >= cand_f, 1.0, 0.0), axis=1, keepdims=True)
            return jnp.where((cnt >= cap) & (cand >= MIN_NORMAL_F32_BITS), cand, cur)

        thr = pltpu.bitcast(lax.fori_loop(0, 31, search, jnp.zeros((e_n, 1), jnp.int32)), F32)
        gt = pt > thr
        eq = pt == thr
        n_gt = jnp.sum(jnp.where(gt, 1.0, 0.0), axis=1, keepdims=True)
        eq_rank = jnp.dot(jnp.where(eq, 1.0, 0.0).astype(BF16), tri, preferred_element_type=F32)
        sel = gt | (eq & (eq_rank < cap - n_gt))
    pos = jnp.dot(jnp.where(sel, 1.0, 0.0).astype(BF16), tri, preferred_element_type=F32)
    slot = jnp.where(sel, pos, -1.0)
    pt_ref[0] = pt
    jrow = _iota((cap, seq), 0).astype(F32)
    for e in range(e_n):
        g_ref[0, e * cap:(e + 1) * cap, :] = jnp.where(slot[e:e + 1, :] == jrow, 1.0, 0.0).astype(BF16)
    xs = jnp.dot(g_ref[0], u_ref[...], preferred_element_type=F32)
    for e in range(e_n):
        xs_ref[e] = xs[e * cap:(e + 1) * cap, :].astype(xs_ref.dtype)


def _route_call(probs, u2, bsz, seq):
    cap = EC_FACTOR * seq // N_EXPERTS
    tri = jnp.asarray(np.triu(np.ones((seq, seq), np.float32), 1), dtype=BF16)
    return pl.pallas_call(
        functools.partial(_route_kernel, seq, cap),
        grid=(bsz,),
        in_specs=[pl.BlockSpec((seq, LANES), lambda b: (b, 0)),
                  pl.BlockSpec((seq, D_MODEL), lambda b: (b, 0)),
                  pl.BlockSpec((seq, seq), lambda b: (0, 0), pipeline_mode=pl.Buffered(1))],
        out_specs=[pl.BlockSpec((N_EXPERTS, cap, D_MODEL), lambda b: (0, b, 0)),
                   pl.BlockSpec((1, N_EXPERTS * cap, seq), lambda b: (b, 0, 0)),
                   pl.BlockSpec((1, N_EXPERTS, seq), lambda b: (b, 0, 0))],
        out_shape=[jax.ShapeDtypeStruct((N_EXPERTS, bsz * cap, D_MODEL), BF16),
                   jax.ShapeDtypeStruct((bsz, N_EXPERTS * cap, seq), BF16),
                   jax.ShapeDtypeStruct((bsz, N_EXPERTS, seq), F32)],
        compiler_params=_cparams(1),
        name="route",
    )(probs, u2, tri)


def _expert_kernel(xc_ref, xd_ref, wg_ref, wu_ref, wd_ref, yc_ref, yd_ref):
    wg = wg_ref[0].astype(BF16)
    wu = wu_ref[0].astype(BF16)
    wd = wd_ref[0].astype(BF16)
    for x_ref, y_ref in ((xc_ref, yc_ref), (xd_ref, yd_ref)):
        rows = x_ref.shape[1]
        rb = min(rows, EXPERT_ROW_BLOCK)
        for r0 in range(0, rows, rb):
            x = x_ref[0, r0:r0 + rb, :]
            hg = jnp.dot(x, wg, preferred_element_type=F32)
            hu = jnp.dot(x, wu, preferred_element_type=F32)
            y = jnp.dot((_silu(hg) * hu).astype(BF16), wd, preferred_element_type=F32)
            y_ref[0, r0:r0 + rb, :] = y.astype(y_ref.dtype)


def _expert_call(xs_c, xs_d, w_gate, w_up, w_down):
    e_n, rc, d = xs_c.shape
    rd = xs_d.shape[1]
    ff = w_gate.shape[2]
    return pl.pallas_call(
        _expert_kernel,
        grid=(e_n,),
        in_specs=[pl.BlockSpec((1, rc, d), lambda e: (e, 0, 0)),
                  pl.BlockSpec((1, rd, d), lambda e: (e, 0, 0)),
                  pl.BlockSpec((1, d, ff), lambda e: (e, 0, 0)),
                  pl.BlockSpec((1, d, ff), lambda e: (e, 0, 0)),
                  pl.BlockSpec((1, ff, d), lambda e: (e, 0, 0))],
        out_specs=[pl.BlockSpec((1, rc, d), lambda e: (e, 0, 0)),
                   pl.BlockSpec((1, rd, d), lambda e: (e, 0, 0))],
        out_shape=[jax.ShapeDtypeStruct((e_n, rc, d), BF16), jax.ShapeDtypeStruct((e_n, rd, d), BF16)],
        compiler_params=_cparams(1),
        name="experts",
    )(xs_c, xs_d, w_gate, w_up, w_down)


def _final_kernel(seq, cap, x1_ref, y_ref, g_ref, pt_ref, mod_ref, l2g, l2b, o_ref, yw_s):
    pt = pt_ref[0]
    for e in range(N_EXPERTS):
        onehot = g_ref[0, e * cap:(e + 1) * cap, :].astype(F32)
        aff = jnp.sum(onehot * pt[e:e + 1, :], axis=1, keepdims=True)
        yw_s[e * cap:(e + 1) * cap, :] = (y_ref[e].astype(F32) * aff).astype(BF16)
    ffn = lax.dot_general(g_ref[0], yw_s[...], (((0,), (0,)), ((), ())), preferred_element_type=F32)
    g2 = mod_ref[0, 5]
    o_ref[...] = _ln(ALPHA * x1_ref[...] + g2 * ffn) * l2g[...] + l2b[...]


def _final_call(x1, ys, gmat, pt, mod4, mod_row, l2g, l2b, bsz, seq):
    cap = EC_FACTOR * seq // N_EXPERTS
    d = D_MODEL
    return pl.pallas_call(
        functools.partial(_final_kernel, seq, cap),
        grid=(bsz,),
        in_specs=[pl.BlockSpec((seq, d), lambda b: (b, 0)),
                  pl.BlockSpec((N_EXPERTS, cap, d), lambda b: (0, b, 0)),
                  pl.BlockSpec((1, N_EXPERTS * cap, seq), lambda b: (b, 0, 0)),
                  pl.BlockSpec((1, N_EXPERTS, seq), lambda b: (b, 0, 0)),
                  _mod_spec(lambda b: mod_row(b * seq)),
                  pl.BlockSpec((1, d), lambda b: (0, 0)),
                  pl.BlockSpec((1, d), lambda b: (0, 0))],
        out_specs=pl.BlockSpec((seq, d), lambda b: (b, 0)),
        out_shape=jax.ShapeDtypeStruct((bsz * seq, d), F32),
        scratch_shapes=[pltpu.VMEM((N_EXPERTS * cap, d), BF16)],
        compiler_params=_cparams(1),
        name="final",
    )(x1, ys, gmat, pt, mod4, l2g, l2b)


def _grid_pos_embed(rows, dim):
    r = np.repeat(np.arange(rows), GRID_W)
    col = np.tile(np.arange(GRID_W), rows)
    quarter = dim // 4
    omega = 1.0 / (10000.0 ** (np.arange(quarter) / quarter))

    def enc(p):
        ang = p[:, None] * omega[None, :]
        return np.concatenate([np.sin(ang), np.cos(ang)], -1)

    return jnp.asarray(np.concatenate([enc(r), enc(col)], -1), dtype=F32)


ACT_DT = BF16


def kernel(x_prompt, x_sample, state_delta_fwd, state_delta_bwd, c, c_ctx, w_mod, b_mod, w_in, conv_a_w, a_log, dt_bias, gnorm_w, conv_b_w, conv_b_b, hy_w1, hy_b1, hy_w2, hy_b2, hy_w3, hy_freq, hy_bias, w_up_a, w_up_b, w_out, ln1_g, ln1_b, w_router, w_e_gate, w_e_up, w_e_down, ln2_g, ln2_b):
    d = D_MODEL
    n_ctx, l_ctx, _ = x_prompt.shape
    n_dec, l_dec, _ = x_sample.shape
    lyr = 0

    cond_t = jnp.zeros((d, 8), F32).at[:, 0].set(c_ctx).at[:, 1:1 + n_dec].set(c.T)
    mod4 = _mod_call(cond_t, 1 + n_dec, w_mod[lyr], b_mod[lyr]).reshape(8, N_MOD, 1, d)

    in_w = w_in[lyr].astype(BF16)

    prm = jnp.zeros((8, LANES), F32)
    prm = prm.at[0, 2 * DN_HEADS:4 * DN_HEADS].set(a_log[lyr].reshape(-1))
    prm = prm.at[1, 2 * DN_HEADS:4 * DN_HEADS].set(dt_bias[lyr].reshape(-1))
    gn = gnorm_w[lyr].reshape(1, -1)

    wua = w_up_a[lyr].astype(BF16)
    wub = w_up_b[lyr].astype(BF16)
    wout = w_out[lyr].astype(BF16)
    wr = jnp.pad(w_router[lyr], ((0, 0), (0, LANES - N_EXPERTS)))
    wr_hi = wr.astype(BF16)
    wr_lo = (wr - wr_hi.astype(F32)).astype(BF16)
    l1g, l1b = ln1_g[lyr].reshape(1, -1), ln1_b[lyr].reshape(1, -1)
    l2g, l2b = ln2_g[lyr].reshape(1, -1), ln2_b[lyr].reshape(1, -1)

    def front(x2d, pos, bsz, seq, mod_row, s0f, s0b):
        qkv, z, ba, hy, gates = _inproj_call(x2d, pos, mod4, mod_row, in_w, ACT_DT)
        o_a, s_f, s_b = _deltanet_call(qkv, z, ba, conv_a_w[lyr], prm, gn, s0f, s0b, bsz, seq, ACT_DT)
        fwd, inv = (jnp.asarray(m) for m in _dft_mats(seq))
        fwd_hi = fwd.astype(BF16)
        fwd_lo = (fwd - fwd_hi.astype(F32)).astype(BF16)
        ha, hb, hd = _hfilter_call(seq, hy_w1[lyr], hy_b1[lyr], hy_w2[lyr], hy_b2[lyr], hy_w3[lyr],
                                   hy_freq[lyr], fwd_hi, fwd_lo)
        o_h = _hyena_call(hy, conv_b_w[lyr], conv_b_b[lyr], ha, hb, hd, hy_bias[lyr],
                          fwd_hi, inv.astype(BF16), bsz, seq, ACT_DT)
        x1, u2, probs = _mix_call(x2d, pos, o_a, o_h, gates, mod4, mod_row, wua, wub, wout, l1g, l1b,
                                   wr_hi, wr_lo)
        xs, gmat, pt = _route_call(probs, u2, bsz, seq)
        return x1, xs, gmat, pt, s_f, s_b

    row_ctx = lambda tok: 0
    row_dec = lambda tok: 1 + tok // l_dec

    xc = x_prompt.reshape(n_ctx * l_ctx, d)
    xd = x_sample.reshape(n_dec * l_dec, d)
    pos = _grid_pos_embed(l_dec // GRID_W, d)

    x1c, xsc, gc, ptc, s_f, s_b = front(xc, None, n_ctx, l_ctx, row_ctx, None, None)
    x1d, xsd, gd, ptd, _, _ = front(xd, pos, n_dec, l_dec, row_dec,
                                    state_delta_fwd[:, lyr], state_delta_bwd[:, lyr])
    yc, yd = _expert_call(xsc, xsd, w_e_gate[lyr], w_e_up[lyr], w_e_down[lyr])
    y_prompt = _final_call(x1c, yc, gc, ptc, mod4, row_ctx, l2g, l2b, n_ctx, l_ctx)
    y_sample = _final_call(x1d, yd, gd, ptd, mod4, row_dec, l2g, l2b, n_dec, l_dec)

    return (y_prompt.reshape(n_ctx, l_ctx, d), y_sample.reshape(n_dec, l_dec, d),
            s_f[:, None], s_b[:, None])
```

```python
import functools
import math

import jax
import jax.numpy as jnp
import numpy as np
from jax import lax
from jax.experimental import pallas as pl
from jax.experimental.pallas import tpu as pltpu

F32 = jnp.float32
BF16 = jnp.bfloat16
HIGHEST = lax.Precision.HIGHEST

D_MODEL = 1024
DEPTH = 1
GRID_W = 64
DN_HEADS = 4
DN_DK = 128
DN_DV = 128
DN_WIDTH = DN_HEADS * DN_DV
HY_WIDTH = D_MODEL // 2
HY_EMB = 33
HY_FFN = 64
HY_TARGET = 1e-2
HY_FAST_DECAY = 0.3
HY_SLOW_DECAY = 1.5
N_EXPERTS = 16
EC_FACTOR = 2
N_MOD = 6
ALPHA = (2 * DEPTH) ** 0.25
LN_EPS = 1e-5
RMS_EPS = 1e-6
QKV_WIDTH = 2 * DN_HEADS * DN_DK + DN_WIDTH

LANES = 128
INPROJ_TILE = 512
MIX_TILE = 256
MOD_COL_BLOCK = 1024
DN_CHUNK = 256
INV_BASE = 16
DN_REQS_PER_STEP = 2
HY_CH_BLOCK = 512
EXPERT_ROW_BLOCK = 512
VMEM_LIMIT = 56 * 1024 * 1024
MIN_NORMAL_F32_BITS = 0x00800000
RANK_COUNT_MAX_SEQ = 256


def _cparams(n_axes):
    return pltpu.CompilerParams(dimension_semantics=("arbitrary",) * n_axes,
                                vmem_limit_bytes=VMEM_LIMIT)


def _bmm(a, b):
    return jnp.einsum("bmk,bkn->bmn", a.astype(BF16), b.astype(BF16), preferred_element_type=F32)


def _bmm_nt(a, b):
    return jnp.einsum("bmk,bnk->bmn", a.astype(BF16), b.astype(BF16), preferred_element_type=F32)


def _bmm_tn(a, b):
    return jnp.einsum("bkm,bkn->bmn", a.astype(BF16), b.astype(BF16), preferred_element_type=F32)


def _mm_f32(a, b):
    return jnp.dot(a, b, precision=HIGHEST, preferred_element_type=F32)


def _sigmoid(x):
    return 0.5 * jnp.tanh(0.5 * x) + 0.5


def _silu(x):
    return x * _sigmoid(x)


def _softplus(x):
    return jnp.maximum(x, 0.0) + jnp.log1p(jnp.exp(-jnp.abs(x)))


def _ln(x):
    mu = jnp.mean(x, axis=-1, keepdims=True)
    xc = x - mu
    var = jnp.mean(xc * xc, axis=-1, keepdims=True)
    return xc * lax.rsqrt(var + LN_EPS)


def _l2norm(x, scale=1.0):
    return x * (lax.rsqrt(jnp.sum(x * x, axis=-1, keepdims=True) + 1e-6) * scale)


def _iota(shape, dim):
    return lax.broadcasted_iota(jnp.int32, shape, dim)


def _short_conv3(x, w, first, last):
    xm = jnp.where(first, 0.0, pltpu.roll(x, 1, 0))
    xp = jnp.where(last, 0.0, pltpu.roll(x, x.shape[0] - 1, 0))
    return w[0:1, :] * xm + w[1:2, :] * x + w[2:3, :] * xp


def _mod_kernel(n_rows, ct_ref, w_ref, b_ref, o_ref):
    s = _silu(ct_ref[...])
    w = w_ref[...]
    b = b_ref[...]
    o_ref[...] = jnp.zeros(o_ref.shape, F32)
    for r in range(n_rows):
        o_ref[r:r + 1, :] = jnp.sum(w * s[:, r:r + 1], axis=0, keepdims=True) + b


def _mod_call(cond_t, n_rows, w_mod, b_mod):
    d = D_MODEL
    tn = MOD_COL_BLOCK
    return pl.pallas_call(
        functools.partial(_mod_kernel, n_rows),
        grid=(N_MOD * d // tn,),
        in_specs=[pl.BlockSpec((d, 8), lambda j: (0, 0)),
                  pl.BlockSpec((d, tn), lambda j: (0, j)),
                  pl.BlockSpec((1, tn), lambda j: (0, j))],
        out_specs=pl.BlockSpec((8, tn), lambda j: (0, j)),
        out_shape=jax.ShapeDtypeStruct((8, N_MOD * d), F32),
        compiler_params=_cparams(1),
        name="mod",
    )(cond_t, w_mod, b_mod.reshape(1, -1))


def _mod_spec(row_of_step):
    return pl.BlockSpec((1, N_MOD, 1, D_MODEL), lambda *idx: (row_of_step(*idx), 0, 0, 0))


def _inproj_kernel(has_pos, conv_seq, *refs):
    refs = list(refs)
    x = refs.pop(0)[...]
    if has_pos:
        x = x + refs.pop(0)[...]
    mod_ref, w_ref = refs.pop(0), refs.pop(0)
    if conv_seq:
        cwa_ref, cwb_ref, cbb_ref = refs.pop(0), refs.pop(0), refs.pop(0)
    qkv_o, z_o, ba_o, hy_o, g_o = refs
    sh1 = mod_ref[0, 0]
    sc1 = mod_ref[0, 1]
    u = (_ln(x) * (1.0 + sc1) + sh1).astype(BF16)
    o_z = QKV_WIDTH
    o_ba = o_z + DN_WIDTH
    n_ba = 4 * DN_HEADS
    o_g = n_ba + 3 * HY_WIDTH
    lanes = lambda j: slice(j * LANES, (j + 1) * LANES)
    if conv_seq:
        row = _iota((x.shape[0], LANES), 0) % conv_seq
        first = row == 0
        last = row == conv_seq - 1
    head = jnp.dot(u, w_ref[:, :o_ba], preferred_element_type=F32)
    z_o[...] = head[:, o_z:].astype(z_o.dtype)
    if conv_seq:
        for j in range(QKV_WIDTH // LANES):
            y = _silu(_short_conv3(head[:, lanes(j)], cwa_ref[:, lanes(j)], first, last))
            if j < 2 * DN_HEADS:
                y = _l2norm(y, DN_DK ** -0.5 if j < DN_HEADS else 1.0)
            qkv_o[:, lanes(j)] = y.astype(qkv_o.dtype)
    else:
        qkv_o[...] = head[:, :o_z].astype(qkv_o.dtype)
    tail = jnp.dot(u, w_ref[:, o_ba:], preferred_element_type=F32)
    ba_o[...] = tail[:, :LANES]
    g_o[...] = tail[:, o_g:].astype(g_o.dtype)
    hy = tail[:, n_ba:o_g]
    if conv_seq:
        nj = HY_WIDTH // LANES
        for j in range(nj):
            parts = [_short_conv3(hy[:, lanes(p * nj + j)], cwb_ref[:, lanes(p * nj + j)], first, last)
                     + cbb_ref[:, lanes(p * nj + j)] for p in range(3)]
            hy_o[:, lanes(j)] = parts[0].astype(hy_o.dtype)
            hy_o[:, lanes(nj + j)] = (parts[1] * parts[2]).astype(hy_o.dtype)
    else:
        hy_o[...] = hy.astype(hy_o.dtype)


def _inproj_call(x2d, pos, mod4, mod_row, w, conv_a_w, conv_b_w, conv_b_b, seq, act_dt):
    n, d = x2d.shape
    tm = INPROJ_TILE
    conv_seq = seq if tm % seq == 0 else None
    row = lambda i: (i, 0)
    const = lambda i: (0, 0)
    in_specs = [pl.BlockSpec((tm, d), row)]
    args = [x2d]
    if pos is not None:
        tiles = pos.shape[0] // tm
        in_specs.append(pl.BlockSpec((tm, d), lambda i: (i % tiles, 0)))
        args.append(pos)
    in_specs.append(_mod_spec(lambda i: mod_row(i * tm)))
    args.append(mod4)
    in_specs.append(pl.BlockSpec(w.shape, const, pipeline_mode=pl.Buffered(1)))
    args.append(w)
    if conv_seq:
        for a in (conv_a_w, conv_b_w, conv_b_b.reshape(1, -1)):
            in_specs.append(pl.BlockSpec(a.shape, const))
            args.append(a)
    hy_width = (2 if conv_seq else 3) * HY_WIDTH
    widths = (QKV_WIDTH, DN_WIDTH, LANES, hy_width, 2 * D_MODEL)
    dts = (act_dt, act_dt, F32, act_dt, act_dt)
    outs = pl.pallas_call(
        functools.partial(_inproj_kernel, pos is not None, conv_seq),
        grid=(n // tm,),
        in_specs=in_specs,
        out_specs=[pl.BlockSpec((tm, w), row) for w in widths],
        out_shape=[jax.ShapeDtypeStruct((n, w), dt) for w, dt in zip(widths, dts)],
        compiler_params=_cparams(1),
        name="inproj",
    )(*args)
    return outs, conv_seq is not None


def _inv_unit_tri(lm, ri, ci, block):
    c = lm.shape[-1]

    def same_block(h):
        sh = h.bit_length() - 1
        return (ri >> sh) == (ci >> sh)

    def unfold(f, h):
        return jnp.where(same_block(h), jnp.concatenate([f] * (c // h), axis=1), 0.0)

    base = INV_BASE
    n_nat = jnp.where(same_block(base), -lm, 0.0)
    n_fold = n_nat[:, 0:base, :]
    for i in range(1, c // base):
        n_fold = n_fold + n_nat[:, i * base:(i + 1) * base, :]
    eye_fold = jnp.where((_iota((base, c), 1) & (base - 1)) == _iota((base, c), 0), 1.0, 0.0)
    t = eye_fold + n_fold
    p_fold, p_nat = n_fold, n_nat
    span = 2
    while span < base:
        p_fold = _bmm(p_fold, p_nat)
        p_nat = unfold(p_fold, base)
        t = t + _bmm(t, p_nat)
        span *= 2
    s = base
    while s < block:
        sh = s.bit_length() - 1
        t_nat = unfold(t, s)
        even = ((_iota((s, c), 1) >> sh) & 1) == 0
        t2 = jnp.concatenate([jnp.where(even, t, 0.0), jnp.where(even, 0.0, t)], axis=1)
        lo = jnp.where(same_block(2 * s) & ~same_block(s), lm, 0.0)
        t = t2 - _bmm(_bmm(t2, lo), t_nat)
        s *= 2
    return t


def _dn_chunk(q, k, v, gc_col, gc_row, beta, e_col, e_rest, e_all, s_prev, same_chunk, ri, ci):
    c = DN_CHUNK
    nh = q.shape[0] // 2
    half = c // 2
    if same_chunk:
        gram = _bmm_nt(k[:nh], k[:nh])
        qk = _bmm_nt(q[:nh], k[:nh])
        gram = jnp.concatenate([gram, gram], 0)
        qk = jnp.concatenate([qk, qk], 0)
    else:
        gram = _bmm_nt(k, k)
        qk = _bmm_nt(q, k)
    diff = gc_col - gc_row
    incl = jnp.concatenate([jnp.broadcast_to(ri >= ci, (nh, c, c)), jnp.broadcast_to(ri <= ci, (nh, c, c))], 0)
    strict = incl & (ri != ci)
    decay = jnp.exp(jnp.where(incl, diff, -jnp.inf))
    lm = jnp.where(strict, beta * gram * decay, 0.0)
    a_intra = qk * decay
    t = _inv_unit_tri(lm, ri, ci, half)
    t1 = t[:, :, :half]
    t2 = t[:, :, half:]
    rhs = jnp.concatenate([v * beta, k * (beta * e_col)], axis=2)
    y1 = _bmm(t1, rhs[:, :half, :])
    y2 = _bmm(t2, rhs[:, half:, :])
    c_f = _bmm(lm[:nh, half:, :half], y1[:nh])
    c_b = _bmm(lm[nh:, :half, half:], y2[nh:])
    c_f = _bmm(t2[:nh], c_f)
    c_b = _bmm(t1[nh:], c_b)
    sol = jnp.concatenate([jnp.concatenate([y1[:nh], y2[:nh] - c_f], 1),
                           jnp.concatenate([y1[nh:] - c_b, y2[nh:]], 1)], 0)
    u = sol[:, :, :DN_DV]
    w = sol[:, :, DN_DV:]
    ks = k * e_rest
    if s_prev is None:
        v_new = u
        o = _bmm(a_intra, v_new)
        s_new = _bmm_tn(ks, v_new)
    else:
        v_new = u - _bmm(w, s_prev)
        o = _bmm(q * e_col, s_prev) + _bmm(a_intra, v_new)
        s_new = s_prev * e_all + _bmm_tn(ks, v_new)
    return o, s_new


def _deltanet_kernel(seq, reqs, zero_init, pre_conv, *refs):
    qkv_ref, cw_ref, z_ref, ba_ref, prm_ref, gn_ref = refs[:6]
    refs = refs[6:]
    if not zero_init:
        s0f_ref, s0b_ref = refs[:2]
        refs = refs[2:]
    o_ref, sf_ref, sb_ref, q_s, k_s, v_s, g_s, b_s, o_s, st_s = refs
    c = DN_CHUNK
    n = seq // c
    nh = DN_HEADS
    nf = reqs * nh
    lanes = lambda j: slice(j * LANES, (j + 1) * LANES)

    if not pre_conv:
        row = _iota((seq, LANES), 0)
        first_row = row == 0
        last_row = row == seq - 1
    for r in range(reqs):
        rows = slice(r * seq, (r + 1) * seq)

        def qkv_part(j):
            x = qkv_ref[rows, lanes(j)].astype(F32)
            return x if pre_conv else _silu(_short_conv3(x, cw_ref[:, lanes(j)], first_row, last_row))

        for h in range(nh):
            q = qkv_part(h)
            k = qkv_part(nh + h)
            q_s[r * nh + h] = q if pre_conv else _l2norm(q, DN_DK ** -0.5)
            k_s[r * nh + h] = k if pre_conv else _l2norm(k)
            v_s[r * nh + h] = qkv_part(2 * nh + h)

    ba = ba_ref[...]
    b_s[...] = _sigmoid(ba)
    g_s[...] = -jnp.exp(prm_ref[0:1, :]) * _softplus(ba + prm_ref[1:2, :])

    ri = _iota((c, c), 0)
    ci = _iota((c, c), 1)
    tri_l = jnp.where(ri >= ci, 1.0, 0.0).astype(F32)
    tri_u = jnp.where(ri <= ci, 1.0, 0.0).astype(F32)

    def chunk_pair(idx_f, idx_b, s_prev):
        def ds(idx, base=0):
            start = idx * c
            start = start if isinstance(start, int) else pl.multiple_of(start, c)
            return pl.ds(base + start, c)

        col_f, col_b, row_f, row_b, beta_f, beta_b = [], [], [], [], [], []
        for r in range(reqs):
            cs_f = _mm_f32(tri_l, g_s[ds(idx_f, r * seq), :])
            cs_b = _mm_f32(tri_u, g_s[ds(idx_b, r * seq), :])
            cst_f = cs_f.T
            cst_b = cs_b.T
            b_f = b_s[ds(idx_f, r * seq), :]
            b_b = b_s[ds(idx_b, r * seq), :]
            for h in range(nh):
                jf, jb = 2 * nh + h, 3 * nh + h
                col_f.append(cs_f[:, jf:jf + 1])
                col_b.append(cs_b[:, jb:jb + 1])
                row_f.append(cst_f[jf:jf + 1, :])
                row_b.append(cst_b[jb:jb + 1, :])
                beta_f.append(b_f[:, h:h + 1])
                beta_b.append(b_b[:, nh + h:nh + h + 1])
        gc_col = jnp.stack(col_f + col_b, 0)
        gc_row = jnp.stack(row_f + row_b, 0)
        beta = jnp.stack(beta_f + beta_b, 0)
        e_col = jnp.exp(gc_col)
        gl = jnp.concatenate([gc_col[:nf, c - 1:c, :], gc_col[nf:, 0:1, :]], 0)
        e_rest = jnp.exp(gl - gc_col)
        e_all = jnp.exp(gl)
        sl_f, sl_b = ds(idx_f), ds(idx_b)
        both = lambda ref: jnp.concatenate([ref[:, sl_f, :], ref[:, sl_b, :]], 0)
        o, s_new = _dn_chunk(both(q_s), both(k_s), both(v_s), gc_col, gc_row, beta, e_col, e_rest, e_all,
                             s_prev, n == 1, ri, ci)
        o_s[:nf, sl_f, :] = o[:nf]
        o_s[nf:, sl_b, :] = o[nf:]
        return s_new

    if not zero_init:
        s0 = jnp.concatenate([s0f_ref[r] for r in range(reqs)] + [s0b_ref[r] for r in range(reqs)], 0)
    if n == 1:
        s_fin = chunk_pair(0, 0, None if zero_init else s0)
    else:
        st_s[...] = jnp.zeros(st_s.shape, F32) if zero_init else s0

        def body(i, carry):
            st_s[...] = chunk_pair(i, n - 1 - i, st_s[...])
            return carry

        lax.fori_loop(0, n, body, 0)
        s_fin = st_s[...]
    for r in range(reqs):
        sf_ref[r] = s_fin[r * nh:(r + 1) * nh]
        sb_ref[r] = s_fin[nf + r * nh:nf + (r + 1) * nh]

    gn = gn_ref[...]
    for r in range(reqs):
        rows = slice(r * seq, (r + 1) * seq)
        for h in range(nh):
            o = o_s[r * nh + h] + o_s[nf + r * nh + h]
            o = o * lax.rsqrt(jnp.mean(o * o, axis=-1, keepdims=True) + RMS_EPS) * gn
            o_ref[rows, lanes(h)] = (o * _silu(z_ref[rows, lanes(h)].astype(F32))).astype(o_ref.dtype)


def _deltanet_call(qkv, z, ba, conv_w, prm, gn, s0f, s0b, bsz, seq, reqs, pre_conv, act_dt):
    nh = DN_HEADS
    zero_init = s0f is None
    rows = reqs * seq
    st = pl.BlockSpec((reqs, nh, DN_DK, DN_DV), lambda b: (b, 0, 0, 0))
    in_specs = [pl.BlockSpec((rows, QKV_WIDTH), lambda b: (b, 0)),
                pl.BlockSpec((3, QKV_WIDTH), lambda b: (0, 0)),
                pl.BlockSpec((rows, DN_WIDTH), lambda b: (b, 0)),
                pl.BlockSpec((rows, LANES), lambda b: (b, 0)),
                pl.BlockSpec((8, LANES), lambda b: (0, 0)),
                pl.BlockSpec((1, LANES), lambda b: (0, 0))]
    args = [qkv, conv_w, z, ba, prm, gn]
    if not zero_init:
        in_specs += [st, st]
        args += [s0f, s0b]
    vm = lambda shape: pltpu.VMEM(shape, F32)
    per_head = vm((reqs * nh, seq, LANES))
    return pl.pallas_call(
        functools.partial(_deltanet_kernel, seq, reqs, zero_init, pre_conv),
        grid=(bsz // reqs,),
        in_specs=in_specs,
        out_specs=[pl.BlockSpec((rows, DN_WIDTH), lambda b: (b, 0)), st, st],
        out_shape=[jax.ShapeDtypeStruct((bsz * seq, DN_WIDTH), act_dt),
                   jax.ShapeDtypeStruct((bsz, nh, DN_DK, DN_DV), F32),
                   jax.ShapeDtypeStruct((bsz, nh, DN_DK, DN_DV), F32)],
        scratch_shapes=[per_head, per_head, per_head, vm((rows, LANES)), vm((rows, LANES)),
                        vm((2 * reqs * nh, seq, LANES)), vm((2 * reqs * nh, DN_DK, DN_DV))],
        compiler_params=_cparams(1),
        name="deltanet",
    )(*args)


def _dft_mats(seq):
    n = 2 * seq
    f = np.arange(seq)[:, None]
    s = np.arange(seq)[None, :]
    ang = 2.0 * np.pi * ((f * s) % n) / n
    fwd = np.concatenate([np.cos(ang), -np.sin(ang)], axis=0)
    fwd[seq, :] = np.cos(np.pi * np.arange(seq))
    t = (np.arange(seq) + seq // 2)[:, None]
    ff = np.arange(seq)[None, :]
    ang2 = 2.0 * np.pi * ((t * ff) % n) / n
    inv_r = 2.0 * np.cos(ang2) / n
    inv_i = -2.0 * np.sin(ang2) / n
    inv_r[:, 0] = 1.0 / n
    inv_i[:, 0] = np.cos(np.pi * t[:, 0]) / n
    inv = np.concatenate([inv_r, inv_i], axis=1)
    return fwd.astype(np.float32), inv.astype(np.float32)


def _filter_feats(seq):
    t = np.linspace(0.0, 1.0, seq)[:, None]
    bands = (HY_EMB - 1) // 2
    ang = (2.0 * math.pi * np.arange(seq) / seq)[:, None] * np.linspace(1e-4, bands - 1, bands)[None, :]
    feats = np.concatenate([t, np.cos(ang), -np.sin(ang)], -1)
    deltas = np.abs(np.linspace(math.log(HY_TARGET) / HY_SLOW_DECAY, math.log(HY_TARGET) / HY_FAST_DECAY,
                                HY_WIDTH))
    offset = np.abs(np.arange(seq) - seq // 2) / (seq // 2)
    window = np.exp(-offset[:, None] * deltas[None, :])
    feats = np.pad(feats, ((0, 0), (0, LANES - HY_EMB)))
    return jnp.asarray(feats, dtype=F32), jnp.asarray(window, dtype=F32)


def _hfilter_kernel(seq, feats_ref, win_ref, w1_ref, b1_ref, w2_ref, b2_ref, w3_ref, fr_ref, fwd_hi_ref,
                    fwd_lo_ref, ha_ref, hb_ref, hd_ref):
    fr = fr_ref[...]
    hid = jnp.sin(fr * (_mm_f32(feats_ref[...], w1_ref[...]) + b1_ref[...]))
    hid = jnp.sin(fr * (_mm_f32(hid, w2_ref[...]) + b2_ref[...]))
    filt = _mm_f32(hid, w3_ref[...]) * win_ref[...]
    filt = filt / (jnp.sum(jnp.abs(filt), axis=0, keepdims=True) + 1e-6)
    filt_hi = filt.astype(BF16)
    filt_lo = (filt - filt_hi.astype(F32)).astype(BF16)
    fwd_hi = fwd_hi_ref[...]
    spec = (jnp.dot(fwd_hi, filt_hi, preferred_element_type=F32)
            + jnp.dot(fwd_hi, filt_lo, preferred_element_type=F32)
            + jnp.dot(fwd_lo_ref[...], filt_hi, preferred_element_type=F32))
    h_re = spec[:seq, :]
    h_im = spec[seq:, :]
    first = _iota(h_re.shape, 0) == 0
    ha_ref[...] = h_re
    hb_ref[...] = jnp.where(first, 0.0, h_im)
    hd_ref[...] = jnp.where(first, h_im, h_re)


def _hfilter_call(seq, w1, b1, w2, b2, w3, freq, fwd_hi, fwd_lo):
    feats, window = _filter_feats(seq)
    w1p = jnp.pad(w1, ((0, LANES - HY_EMB), (0, 0)))
    full = lambda a: pl.BlockSpec(a.shape, lambda i: (0,) * a.ndim)
    args = [feats, window, w1p, b1.reshape(1, -1), w2, b2.reshape(1, -1), w3, freq.reshape(1, -1), fwd_hi,
            fwd_lo]
    out = jax.ShapeDtypeStruct((seq, HY_WIDTH), F32)
    return pl.pallas_call(
        functools.partial(_hfilter_kernel, seq),
        grid=(1,),
        in_specs=[full(a) for a in args],
        out_specs=[pl.BlockSpec((seq, HY_WIDTH), lambda i: (0, 0))] * 3,
        out_shape=[out, out, out],
        compiler_params=_cparams(1),
        name="hfilter",
    )(*args)


def _hyena_kernel(seq, pre_conv, *refs):
    if pre_conv:
        x0_ref, uu_ref, ha_ref, hb_ref, hd_ref, skip_ref, fwd_ref, inv_ref, o_ref = refs
        x0 = x0_ref[...].astype(F32)
        uu = uu_ref[...].astype(F32)
    else:
        (x0_ref, x1_ref, v_ref, c0_ref, c1_ref, c2_ref, b0_ref, b1_ref, b2_ref,
         ha_ref, hb_ref, hd_ref, skip_ref, fwd_ref, inv_ref, o_ref) = refs
        row = _iota((seq, HY_CH_BLOCK), 0)
        first = row == 0
        last = row == seq - 1
        conv = lambda x_ref, w_ref, b_ref: _short_conv3(x_ref[...].astype(F32), w_ref[...], first, last) + b_ref[...]
        x0 = conv(x0_ref, c0_ref, b0_ref)
        uu = conv(x1_ref, c1_ref, b1_ref) * conv(v_ref, c2_ref, b2_ref)
    spec = jnp.dot(fwd_ref[...], uu.astype(BF16), preferred_element_type=F32)
    u_re = spec[:seq, :]
    u_im = spec[seq:, :]
    hb = hb_ref[...]
    y_re = u_re * ha_ref[...] - u_im * hb
    y_im = u_re * hb + u_im * hd_ref[...]
    y = jnp.concatenate([y_re, y_im], axis=0).astype(BF16)
    cv = jnp.dot(inv_ref[...], y, preferred_element_type=F32)
    o_ref[...] = (x0 * (cv + uu * skip_ref[...])).astype(o_ref.dtype)


def _hyena_call(hy, pre_conv, conv_w, conv_b, ha, hb, hd, skip, fwd, inv, bsz, seq, act_dt):
    cb = HY_CH_BLOCK
    nblk = HY_WIDTH // cb
    tok = lambda off: pl.BlockSpec((seq, cb), lambda b, j: (b, off + j))
    cw = lambda off: pl.BlockSpec((3, cb), lambda b, j: (0, off + j))
    bias = lambda off: pl.BlockSpec((1, cb), lambda b, j: (0, off + j))
    ch = pl.BlockSpec((seq, cb), lambda b, j: (0, j))
    const = lambda a: pl.BlockSpec(a.shape, lambda b, j: (0, 0), pipeline_mode=pl.Buffered(1))
    tail_specs = [ch, ch, ch, pl.BlockSpec((1, cb), lambda b, j: (0, j)), const(fwd), const(inv)]
    tail_args = [ha, hb, hd, skip.reshape(1, -1), fwd, inv]
    if pre_conv:
        in_specs = [tok(0), tok(nblk)] + tail_specs
        args = [hy, hy] + tail_args
    else:
        conv_b2 = conv_b.reshape(1, -1)
        in_specs = [tok(0), tok(nblk), tok(2 * nblk), cw(0), cw(nblk), cw(2 * nblk),
                    bias(0), bias(nblk), bias(2 * nblk)] + tail_specs
        args = [hy, hy, hy, conv_w, conv_w, conv_w, conv_b2, conv_b2, conv_b2] + tail_args
    return pl.pallas_call(
        functools.partial(_hyena_kernel, seq, pre_conv),
        grid=(bsz, nblk),
        in_specs=in_specs,
        out_specs=tok(0),
        out_shape=jax.ShapeDtypeStruct((bsz * seq, HY_WIDTH), act_dt),
        compiler_params=_cparams(2),
        name="hyena",
    )(*args)


def _mix_kernel(has_pos, *refs):
    if has_pos:
        x_ref, pos_ref = refs[:2]
        refs = refs[2:]
        x = x_ref[...] + pos_ref[...]
    else:
        x_ref = refs[0]
        refs = refs[1:]
        x = x_ref[...]
    (oa_ref, ob_ref, g_ref, mod_ref, wua, wub, wout, l1g, l1b, wr_hi, wr_lo, x1_o, u2_o, p_o) = refs
    d = D_MODEL
    gates = g_ref[...].astype(F32)
    ga = _sigmoid(gates[:, :d])
    gb = _sigmoid(gates[:, d:])
    up = (ga * jnp.dot(oa_ref[...].astype(BF16), wua[...], preferred_element_type=F32)
          + gb * jnp.dot(ob_ref[...].astype(BF16), wub[...], preferred_element_type=F32))
    mixed = jnp.dot(up.astype(BF16), wout[...], preferred_element_type=F32)
    g1 = mod_ref[0, 2]
    sh2 = mod_ref[0, 3]
    sc2 = mod_ref[0, 4]
    x1 = _ln(ALPHA * x + g1 * mixed) * l1g[...] + l1b[...]
    x1_o[...] = x1
    u2 = _ln(x1) * (1.0 + sc2) + sh2
    u2_hi = u2.astype(BF16)
    u2_o[...] = u2_hi
    u2_lo = (u2 - u2_hi.astype(F32)).astype(BF16)
    logits = (jnp.dot(u2_hi, wr_hi[...], preferred_element_type=F32)
              + jnp.dot(u2_hi, wr_lo[...], preferred_element_type=F32)
              + jnp.dot(u2_lo, wr_hi[...], preferred_element_type=F32))
    lane = _iota(logits.shape, 1)
    logits = jnp.where(lane < N_EXPERTS, logits, -jnp.inf)
    m = jnp.max(logits, axis=-1, keepdims=True)
    e = jnp.exp(logits - m)
    p_o[...] = e / jnp.sum(e, axis=-1, keepdims=True)


def _mix_call(x2d, pos, o_a, o_b, gates, mod4, mod_row, wua, wub, wout, l1g, l1b, wr_hi, wr_lo):
    n, d = x2d.shape
    tm = MIX_TILE
    row = lambda i: (i, 0)
    const = lambda i: (0, 0)
    in_specs = [pl.BlockSpec((tm, d), row)]
    args = [x2d]
    if pos is not None:
        tiles = pos.shape[0] // tm
        in_specs.append(pl.BlockSpec((tm, d), lambda i: (i % tiles, 0)))
        args.append(pos)
    in_specs += [pl.BlockSpec((tm, DN_WIDTH), row), pl.BlockSpec((tm, HY_WIDTH), row),
                 pl.BlockSpec((tm, 2 * d), row),
                 _mod_spec(lambda i: mod_row(i * tm))]
    args += [o_a, o_b, gates, mod4]
    for w in (wua, wub, wout, l1g, l1b, wr_hi, wr_lo):
        in_specs.append(pl.BlockSpec(w.shape, const, pipeline_mode=pl.Buffered(1)))
        args.append(w)
    return pl.pallas_call(
        functools.partial(_mix_kernel, pos is not None),
        grid=(n // tm,),
        in_specs=in_specs,
        out_specs=[pl.BlockSpec((tm, d), row), pl.BlockSpec((tm, d), row), pl.BlockSpec((tm, LANES), row)],
        out_shape=[jax.ShapeDtypeStruct((n, d), F32), jax.ShapeDtypeStruct((n, d), BF16),
                   jax.ShapeDtypeStruct((n, LANES), F32)],
        compiler_params=_cparams(1),
        name="mix",
    )(*args)


def _route_kernel(seq, cap, p_ref, u_ref, tri_ref, xs_ref, g_ref, pt_ref):
    e_n = N_EXPERTS
    p = p_ref[...]
    pt = p.T[:e_n, :]
    tri = tri_ref[...]

    if seq <= RANK_COUNT_MAX_SEQ:
        earlier = _iota((seq, seq), 0) < _iota((seq, seq), 1)
        ranks = []
        for e in range(e_n):
            pc = p[:, e:e + 1]
            pr = pt[e:e + 1, :]
            beats = (pc > pr) | (earlier & (pc == pr))
            ranks.append(jnp.sum(jnp.where(beats, 1.0, 0.0), axis=0, keepdims=True))
        sel = jnp.concatenate(ranks, axis=0) < cap
    else:
        def search(i, cur):
            cand = cur | (1 << (30 - i))
            cand_f = pltpu.bitcast(cand, F32)
            cnt = jnp.sum(jnp.where(pt >= cand_f, 1.0, 0.0), axis=1, keepdims=True)
            return jnp.where((cnt >= cap) & (cand >= MIN_NORMAL_F32_BITS), cand, cur)

        thr = pltpu.bitcast(lax.fori_loop(0, 31, search, jnp.zeros((e_n, 1), jnp.int32)), F32)
        gt = pt > thr
        eq = pt == thr
        n_gt = jnp.sum(jnp.where(gt, 1.0, 0.0), axis=1, keepdims=True)
        eq_rank = jnp.dot(jnp.where(eq, 1.0, 0.0).astype(BF16), tri, preferred_element_type=F32)
        sel = gt | (eq & (eq_rank < cap - n_gt))
    pos = jnp.dot(jnp.where(sel, 1.0, 0.0).astype(BF16), tri, preferred_element_type=F32)
    slot = jnp.where(sel, pos, -1.0)
    pt_ref[0] = pt
    jrow = _iota((cap, seq), 0).astype(F32)
    for e in range(e_n):
        g_ref[0, e * cap:(e + 1) * cap, :] = jnp.where(slot[e:e + 1, :] == jrow, 1.0, 0.0).astype(BF16)
    xs = jnp.dot(g_ref[0], u_ref[...], preferred_element_type=F32)
    for e in range(e_n):
        xs_ref[e] = xs[e * cap:(e + 1) * cap, :].astype(xs_ref.dtype)


def _route_call(probs, u2, bsz, seq):
    cap = EC_FACTOR * seq // N_EXPERTS
    tri = jnp.asarray(np.triu(np.ones((seq, seq), np.float32), 1), dtype=BF16)
    return pl.pallas_call(
        functools.partial(_route_kernel, seq, cap),
        grid=(bsz,),
        in_specs=[pl.BlockSpec((seq, LANES), lambda b: (b, 0)),
                  pl.BlockSpec((seq, D_MODEL), lambda b: (b, 0)),
                  pl.BlockSpec((seq, seq), lambda b: (0, 0), pipeline_mode=pl.Buffered(1))],
        out_specs=[pl.BlockSpec((N_EXPERTS, cap, D_MODEL), lambda b: (0, b, 0)),
                   pl.BlockSpec((1, N_EXPERTS * cap, seq), lambda b: (b, 0, 0)),
                   pl.BlockSpec((1, N_EXPERTS, seq), lambda b: (b, 0, 0))],
        out_shape=[jax.ShapeDtypeStruct((N_EXPERTS, bsz * cap, D_MODEL), BF16),
                   jax.ShapeDtypeStruct((bsz, N_EXPERTS * cap, seq), BF16),
                   jax.ShapeDtypeStruct((bsz, N_EXPERTS, seq), F32)],
        compiler_params=_cparams(1),
        name="route",
    )(probs, u2, tri)


def _expert_kernel(xc_ref, xd_ref, wg_ref, wu_ref, wd_ref, yc_ref, yd_ref):
    wg = wg_ref[0].astype(BF16)
    wu = wu_ref[0].astype(BF16)
    wd = wd_ref[0].astype(BF16)
    for x_ref, y_ref in ((xc_ref, yc_ref), (xd_ref, yd_ref)):
        rows = x_ref.shape[1]
        rb = min(rows, EXPERT_ROW_BLOCK)
        for r0 in range(0, rows, rb):
            x = x_ref[0, r0:r0 + rb, :]
            hg = jnp.dot(x, wg, preferred_element_type=F32)
            hu = jnp.dot(x, wu, preferred_element_type=F32)
            y = jnp.dot((_silu(hg) * hu).astype(BF16), wd, preferred_element_type=F32)
            y_ref[0, r0:r0 + rb, :] = y.astype(y_ref.dtype)


def _expert_call(xs_c, xs_d, w_gate, w_up, w_down):
    e_n, rc, d = xs_c.shape
    rd = xs_d.shape[1]
    ff = w_gate.shape[2]
    return pl.pallas_call(
        _expert_kernel,
        grid=(e_n,),
        in_specs=[pl.BlockSpec((1, rc, d), lambda e: (e, 0, 0)),
                  pl.BlockSpec((1, rd, d), lambda e: (e, 0, 0)),
                  pl.BlockSpec((1, d, ff), lambda e: (e, 0, 0)),
                  pl.BlockSpec((1, d, ff), lambda e: (e, 0, 0)),
                  pl.BlockSpec((1, ff, d), lambda e: (e, 0, 0))],
        out_specs=[pl.BlockSpec((1, rc, d), lambda e: (e, 0, 0)),
                   pl.BlockSpec((1, rd, d), lambda e: (e, 0, 0))],
        out_shape=[jax.ShapeDtypeStruct((e_n, rc, d), BF16), jax.ShapeDtypeStruct((e_n, rd, d), BF16)],
        compiler_params=_cparams(1),
        name="experts",
    )(xs_c, xs_d, w_gate, w_up, w_down)


def _final_kernel(seq, cap, x1_ref, y_ref, g_ref, pt_ref, mod_ref, l2g, l2b, o_ref, yw_s):
    pt = pt_ref[0]
    for e in range(N_EXPERTS):
        onehot = g_ref[0, e * cap:(e + 1) * cap, :].astype(F32)
        aff = jnp.sum(onehot * pt[e:e + 1, :], axis=1, keepdims=True)
        yw_s[e * cap:(e + 1) * cap, :] = (y_ref[e].astype(F32) * aff).astype(BF16)
    ffn = lax.dot_general(g_ref[0], yw_s[...], (((0,), (0,)), ((), ())), preferred_element_type=F32)
    g2 = mod_ref[0, 5]
    o_ref[...] = _ln(ALPHA * x1_ref[...] + g2 * ffn) * l2g[...] + l2b[...]


def _final_call(x1, ys, gmat, pt, mod4, mod_row, l2g, l2b, bsz, seq):
    cap = EC_FACTOR * seq // N_EXPERTS
    d = D_MODEL
    return pl.pallas_call(
        functools.partial(_final_kernel, seq, cap),
        grid=(bsz,),
        in_specs=[pl.BlockSpec((seq, d), lambda b: (b, 0)),
                  pl.BlockSpec((N_EXPERTS, cap, d), lambda b: (0, b, 0)),
                  pl.BlockSpec((1, N_EXPERTS * cap, seq), lambda b: (b, 0, 0)),
                  pl.BlockSpec((1, N_EXPERTS, seq), lambda b: (b, 0, 0)),
                  _mod_spec(lambda b: mod_row(b * seq)),
                  pl.BlockSpec((1, d), lambda b: (0, 0)),
                  pl.BlockSpec((1, d), lambda b: (0, 0))],
        out_specs=pl.BlockSpec((seq, d), lambda b: (b, 0)),
        out_shape=jax.ShapeDtypeStruct((bsz * seq, d), F32),
        scratch_shapes=[pltpu.VMEM((N_EXPERTS * cap, d), BF16)],
        compiler_params=_cparams(1),
        name="final",
    )(x1, ys, gmat, pt, mod4, l2g, l2b)


def _grid_pos_embed(rows, dim):
    r = np.repeat(np.arange(rows), GRID_W)
    col = np.tile(np.arange(GRID_W), rows)
    quarter = dim // 4
    omega = 1.0 / (10000.0 ** (np.arange(quarter) / quarter))

    def enc(p):
        ang = p[:, None] * omega[None, :]
        return np.concatenate([np.sin(ang), np.cos(ang)], -1)

    return jnp.asarray(np.concatenate([enc(r), enc(col)], -1), dtype=F32)


ACT_DT = BF16


def kernel(x_prompt, x_sample, state_delta_fwd, state_delta_bwd, c, c_ctx, w_mod, b_mod, w_in, conv_a_w, a_log, dt_bias, gnorm_w, conv_b_w, conv_b_b, hy_w1, hy_b1, hy_w2, hy_b2, hy_w3, hy_freq, hy_bias, w_up_a, w_up_b, w_out, ln1_g, ln1_b, w_router, w_e_gate, w_e_up, w_e_down, ln2_g, ln2_b):
    d = D_MODEL
    n_ctx, l_ctx, _ = x_prompt.shape
    n_dec, l_dec, _ = x_sample.shape
    lyr = 0

    cond_t = jnp.zeros((d, 8), F32).at[:, 0].set(c_ctx).at[:, 1:1 + n_dec].set(c.T)
    mod4 = _mod_call(cond_t, 1 + n_dec, w_mod[lyr], b_mod[lyr]).reshape(8, N_MOD, 1, d)

    in_w = w_in[lyr].astype(BF16)

    prm = jnp.zeros((8, LANES), F32)
    prm = prm.at[0, 2 * DN_HEADS:4 * DN_HEADS].set(a_log[lyr].reshape(-1))
    prm = prm.at[1, 2 * DN_HEADS:4 * DN_HEADS].set(dt_bias[lyr].reshape(-1))
    gn = gnorm_w[lyr].reshape(1, -1)

    wua = w_up_a[lyr].astype(BF16)
    wub = w_up_b[lyr].astype(BF16)
    wout = w_out[lyr].astype(BF16)
    wr = jnp.pad(w_router[lyr], ((0, 0), (0, LANES - N_EXPERTS)))
    wr_hi = wr.astype(BF16)
    wr_lo = (wr - wr_hi.astype(F32)).astype(BF16)
    l1g, l1b = ln1_g[lyr].reshape(1, -1), ln1_b[lyr].reshape(1, -1)
    l2g, l2b = ln2_g[lyr].reshape(1, -1), ln2_b[lyr].reshape(1, -1)

    def front(x2d, pos, bsz, seq, mod_row, s0f, s0b):
        (qkv, z, ba, hy, gates), pre_conv = _inproj_call(x2d, pos, mod4, mod_row, in_w, conv_a_w[lyr],
                                                          conv_b_w[lyr], conv_b_b[lyr], seq, ACT_DT)
        reqs = DN_REQS_PER_STEP if seq == DN_CHUNK and bsz % DN_REQS_PER_STEP == 0 else 1
        o_a, s_f, s_b = _deltanet_call(qkv, z, ba, conv_a_w[lyr], prm, gn, s0f, s0b, bsz, seq, reqs,
                                       pre_conv, ACT_DT)
        fwd, inv = (jnp.asarray(m) for m in _dft_mats(seq))
        fwd_hi = fwd.astype(BF16)
        fwd_lo = (fwd - fwd_hi.astype(F32)).astype(BF16)
        ha, hb, hd = _hfilter_call(seq, hy_w1[lyr], hy_b1[lyr], hy_w2[lyr], hy_b2[lyr], hy_w3[lyr],
                                   hy_freq[lyr], fwd_hi, fwd_lo)
        o_h = _hyena_call(hy, pre_conv, conv_b_w[lyr], conv_b_b[lyr], ha, hb, hd, hy_bias[lyr],
                          fwd_hi, inv.astype(BF16), bsz, seq, ACT_DT)
        x1, u2, probs = _mix_call(x2d, pos, o_a, o_h, gates, mod4, mod_row, wua, wub, wout, l1g, l1b,
                                   wr_hi, wr_lo)
        xs, gmat, pt = _route_call(probs, u2, bsz, seq)
        return x1, xs, gmat, pt, s_f, s_b

    row_ctx = lambda tok: 0
    row_dec = lambda tok: 1 + tok // l_dec

    xc = x_prompt.reshape(n_ctx * l_ctx, d)
    xd = x_sample.reshape(n_dec * l_dec, d)
    pos = _grid_pos_embed(l_dec // GRID_W, d)

    x1c, xsc, gc, ptc, s_f, s_b = front(xc, None, n_ctx, l_ctx, row_ctx, None, None)
    x1d, xsd, gd, ptd, _, _ = front(xd, pos, n_dec, l_dec, row_dec,
                                    state_delta_fwd[:, lyr], state_delta_bwd[:, lyr])
    yc, yd = _expert_call(xsc, xsd, w_e_gate[lyr], w_e_up[lyr], w_e_down[lyr])
    y_prompt = _final_call(x1c, yc, gc, ptc, mod4, row_ctx, l2g, l2b, n_ctx, l_ctx)
    y_sample = _final_call(x1d, yd, gd, ptd, mod4, row_dec, l2g, l2b, n_dec, l_dec)

    return (y_prompt.reshape(n_ctx, l_ctx, d), y_sample.reshape(n_dec, l_dec, d),
            s_f[:, None], s_b[:, None])
```

```python
import functools
import math

import jax
import jax.numpy as jnp
import numpy as np
from jax import lax
from jax.experimental import pallas as pl
from jax.experimental.pallas import tpu as pltpu

F32 = jnp.float32
BF16 = jnp.bfloat16
HIGHEST = lax.Precision.HIGHEST

D_MODEL = 1024
DEPTH = 1
GRID_W = 64
DN_HEADS = 4
DN_DK = 128
DN_DV = 128
DN_WIDTH = DN_HEADS * DN_DV
HY_WIDTH = D_MODEL // 2
HY_EMB = 33
HY_FFN = 64
HY_TARGET = 1e-2
HY_FAST_DECAY = 0.3
HY_SLOW_DECAY = 1.5
N_EXPERTS = 16
EC_FACTOR = 2
N_MOD = 6
ALPHA = (2 * DEPTH) ** 0.25
LN_EPS = 1e-5
RMS_EPS = 1e-6
QKV_WIDTH = 2 * DN_HEADS * DN_DK + DN_WIDTH

LANES = 128
INPROJ_TILE = 512
MIX_TILE = 1024
MIX_SUB_ROWS = 256
MOD_COL_BLOCK = 1024
DN_CHUNK = 256
INV_BASE = 16
DN_REQS_PER_STEP = 2
HY_CH_BLOCK = 512
EXPERT_ROW_BLOCK = 512
VMEM_LIMIT = 56 * 1024 * 1024
MIN_NORMAL_F32_BITS = 0x00800000
RANK_COUNT_MAX_SEQ = 256


def _cparams(n_axes):
    return pltpu.CompilerParams(dimension_semantics=("arbitrary",) * n_axes,
                                vmem_limit_bytes=VMEM_LIMIT)


def _bmm(a, b):
    return jnp.einsum("bmk,bkn->bmn", a.astype(BF16), b.astype(BF16), preferred_element_type=F32)


def _bmm_nt(a, b):
    return jnp.einsum("bmk,bnk->bmn", a.astype(BF16), b.astype(BF16), preferred_element_type=F32)


def _bmm_tn(a, b):
    return jnp.einsum("bkm,bkn->bmn", a.astype(BF16), b.astype(BF16), preferred_element_type=F32)


def _mm_f32(a, b):
    return jnp.dot(a, b, precision=HIGHEST, preferred_element_type=F32)


def _sigmoid(x):
    return 0.5 * jnp.tanh(0.5 * x) + 0.5


def _silu(x):
    return x * _sigmoid(x)


def _softplus(x):
    return jnp.maximum(x, 0.0) + jnp.log1p(jnp.exp(-jnp.abs(x)))


def _ln(x):
    mu = jnp.mean(x, axis=-1, keepdims=True)
    xc = x - mu
    var = jnp.mean(xc * xc, axis=-1, keepdims=True)
    return xc * lax.rsqrt(var + LN_EPS)


def _l2norm(x, scale=1.0):
    return x * (lax.rsqrt(jnp.sum(x * x, axis=-1, keepdims=True) + 1e-6) * scale)


def _iota(shape, dim):
    return lax.broadcasted_iota(jnp.int32, shape, dim)


def _short_conv3(x, w, first, last):
    xm = jnp.where(first, 0.0, pltpu.roll(x, 1, 0))
    xp = jnp.where(last, 0.0, pltpu.roll(x, x.shape[0] - 1, 0))
    return w[0:1, :] * xm + w[1:2, :] * x + w[2:3, :] * xp


def _mod_kernel(n_rows, ct_ref, w_ref, b_ref, o_ref):
    s = _silu(ct_ref[...])
    w = w_ref[...]
    b = b_ref[...]
    o_ref[...] = jnp.zeros(o_ref.shape, F32)
    for r in range(n_rows):
        o_ref[r:r + 1, :] = jnp.sum(w * s[:, r:r + 1], axis=0, keepdims=True) + b


def _mod_call(cond_t, n_rows, w_mod, b_mod):
    d = D_MODEL
    tn = MOD_COL_BLOCK
    return pl.pallas_call(
        functools.partial(_mod_kernel, n_rows),
        grid=(N_MOD * d // tn,),
        in_specs=[pl.BlockSpec((d, 8), lambda j: (0, 0)),
                  pl.BlockSpec((d, tn), lambda j: (0, j)),
                  pl.BlockSpec((1, tn), lambda j: (0, j))],
        out_specs=pl.BlockSpec((8, tn), lambda j: (0, j)),
        out_shape=jax.ShapeDtypeStruct((8, N_MOD * d), F32),
        compiler_params=_cparams(1),
        name="mod",
    )(cond_t, w_mod, b_mod.reshape(1, -1))


def _mod_spec(row_of_step):
    return pl.BlockSpec((1, N_MOD, 1, D_MODEL), lambda *idx: (row_of_step(*idx), 0, 0, 0))


def _inproj_kernel(has_pos, conv_seq, *refs):
    refs = list(refs)
    x_ref = refs.pop(0)
    pos_ref = refs.pop(0) if has_pos else None
    mod_ref, w_ref = refs.pop(0), refs.pop(0)
    if conv_seq:
        cwa_ref, cwb_ref, cbb_ref = refs.pop(0), refs.pop(0), refs.pop(0)
    qkv_o, z_o, ba_o, hy_o, g_o = refs
    sh1 = mod_ref[0, 0]
    sc1 = mod_ref[0, 1]
    o_z = QKV_WIDTH
    o_ba = o_z + DN_WIDTH
    n_ba = 4 * DN_HEADS
    o_g = n_ba + 3 * HY_WIDTH
    lanes = lambda j: slice(j * LANES, (j + 1) * LANES)
    tm = x_ref.shape[0]
    sub = conv_seq if conv_seq else tm
    blocks = [slice(r0, r0 + sub) for r0 in range(0, tm, sub)]

    proj = []
    for rows in blocks:
        x = x_ref[rows, :]
        if has_pos:
            x = x + pos_ref[rows, :]
        u = (_ln(x) * (1.0 + sc1) + sh1).astype(BF16)
        proj.append((jnp.dot(u, w_ref[:, :o_ba], preferred_element_type=F32),
                     jnp.dot(u, w_ref[:, o_ba:], preferred_element_type=F32)))

    if conv_seq:
        row = _iota((sub, LANES), 0)
        first = row == 0
        last = row == sub - 1
    for rows, (head, tail) in zip(blocks, proj):
        z_o[rows, :] = head[:, o_z:].astype(z_o.dtype)
        ba_o[rows, :] = tail[:, :LANES]
        g_o[rows, :] = tail[:, o_g:].astype(g_o.dtype)
        hy = tail[:, n_ba:o_g]
        if not conv_seq:
            qkv_o[rows, :] = head[:, :o_z].astype(qkv_o.dtype)
            hy_o[rows, :] = hy.astype(hy_o.dtype)
            continue
        for j in range(QKV_WIDTH // LANES):
            y = _silu(_short_conv3(head[:, lanes(j)], cwa_ref[:, lanes(j)], first, last))
            if j < 2 * DN_HEADS:
                y = _l2norm(y, DN_DK ** -0.5 if j < DN_HEADS else 1.0)
            qkv_o[rows, lanes(j)] = y.astype(qkv_o.dtype)
        nj = HY_WIDTH // LANES
        for j in range(nj):
            parts = [_short_conv3(hy[:, lanes(p * nj + j)], cwb_ref[:, lanes(p * nj + j)], first, last)
                     + cbb_ref[:, lanes(p * nj + j)] for p in range(3)]
            hy_o[rows, lanes(j)] = parts[0].astype(hy_o.dtype)
            hy_o[rows, lanes(nj + j)] = (parts[1] * parts[2]).astype(hy_o.dtype)


def _inproj_call(x2d, pos, mod4, mod_row, w, conv_a_w, conv_b_w, conv_b_b, seq, act_dt):
    n, d = x2d.shape
    tm = min(INPROJ_TILE, n)
    assert n % tm == 0
    conv_seq = seq if tm % seq == 0 else None
    row = lambda i: (i, 0)
    const = lambda i: (0, 0)
    in_specs = [pl.BlockSpec((tm, d), row)]
    args = [x2d]
    if pos is not None:
        tiles = pos.shape[0] // tm
        in_specs.append(pl.BlockSpec((tm, d), lambda i: (i % tiles, 0)))
        args.append(pos)
    in_specs.append(_mod_spec(lambda i: mod_row(i * tm)))
    args.append(mod4)
    in_specs.append(pl.BlockSpec(w.shape, const, pipeline_mode=pl.Buffered(1)))
    args.append(w)
    if conv_seq:
        for a in (conv_a_w, conv_b_w, conv_b_b.reshape(1, -1)):
            in_specs.append(pl.BlockSpec(a.shape, const))
            args.append(a)
    hy_width = (2 if conv_seq else 3) * HY_WIDTH
    widths = (QKV_WIDTH, DN_WIDTH, LANES, hy_width, 2 * D_MODEL)
    dts = (act_dt, act_dt, F32, act_dt, act_dt)
    outs = pl.pallas_call(
        functools.partial(_inproj_kernel, pos is not None, conv_seq),
        grid=(n // tm,),
        in_specs=in_specs,
        out_specs=[pl.BlockSpec((tm, w), row) for w in widths],
        out_shape=[jax.ShapeDtypeStruct((n, w), dt) for w, dt in zip(widths, dts)],
        compiler_params=_cparams(1),
        name="inproj",
    )(*args)
    return outs, conv_seq is not None


def _inv_unit_tri(lm, ri, ci, block):
    c = lm.shape[-1]

    def same_block(h):
        sh = h.bit_length() - 1
        return (ri >> sh) == (ci >> sh)

    def unfold(f, h):
        return jnp.where(same_block(h), jnp.concatenate([f] * (c // h), axis=1), 0.0)

    base = INV_BASE
    n_nat = jnp.where(same_block(base), -lm, 0.0)
    n_fold = n_nat[:, 0:base, :]
    for i in range(1, c // base):
        n_fold = n_fold + n_nat[:, i * base:(i + 1) * base, :]
    eye_fold = jnp.where((_iota((base, c), 1) & (base - 1)) == _iota((base, c), 0), 1.0, 0.0)
    t = eye_fold + n_fold
    p_fold, p_nat = n_fold, n_nat
    span = 2
    while span < base:
        p_fold = _bmm(p_fold, p_nat)
        p_nat = unfold(p_fold, base)
        t = t + _bmm(t, p_nat)
        span *= 2
    s = base
    while s < block:
        sh = s.bit_length() - 1
        t_nat = unfold(t, s)
        even = ((_iota((s, c), 1) >> sh) & 1) == 0
        t2 = jnp.concatenate([jnp.where(even, t, 0.0), jnp.where(even, 0.0, t)], axis=1)
        lo = jnp.where(same_block(2 * s) & ~same_block(s), lm, 0.0)
        t = t2 - _bmm(_bmm(t2, lo), t_nat)
        s *= 2
    return t


def _dn_chunk(q, k, v, gc_col, gc_row, beta, e_col, e_rest, e_all, s_prev, same_chunk, ri, ci):
    c = DN_CHUNK
    nh = q.shape[0] // 2
    half = c // 2
    if same_chunk:
        gram = _bmm_nt(k[:nh], k[:nh])
        qk = _bmm_nt(q[:nh], k[:nh])
        gram = jnp.concatenate([gram, gram], 0)
        qk = jnp.concatenate([qk, qk], 0)
    else:
        gram = _bmm_nt(k, k)
        qk = _bmm_nt(q, k)
    diff = gc_col - gc_row
    incl = jnp.concatenate([jnp.broadcast_to(ri >= ci, (nh, c, c)), jnp.broadcast_to(ri <= ci, (nh, c, c))], 0)
    strict = incl & (ri != ci)
    decay = jnp.exp(jnp.where(incl, diff, -jnp.inf))
    lm = jnp.where(strict, beta * gram * decay, 0.0)
    a_intra = qk * decay
    t = _inv_unit_tri(lm, ri, ci, half)
    t1 = t[:, :, :half]
    t2 = t[:, :, half:]
    rhs = jnp.concatenate([v * beta, k * (beta * e_col)], axis=2)
    y1 = _bmm(t1, rhs[:, :half, :])
    y2 = _bmm(t2, rhs[:, half:, :])
    c_f = _bmm(lm[:nh, half:, :half], y1[:nh])
    c_b = _bmm(lm[nh:, :half, half:], y2[nh:])
    c_f = _bmm(t2[:nh], c_f)
    c_b = _bmm(t1[nh:], c_b)
    sol = jnp.concatenate([jnp.concatenate([y1[:nh], y2[:nh] - c_f], 1),
                           jnp.concatenate([y1[nh:] - c_b, y2[nh:]], 1)], 0)
    u = sol[:, :, :DN_DV]
    w = sol[:, :, DN_DV:]
    ks = k * e_rest
    if s_prev is None:
        v_new = u
        o = _bmm(a_intra, v_new)
        s_new = _bmm_tn(ks, v_new)
    else:
        v_new = u - _bmm(w, s_prev)
        o = _bmm(q * e_col, s_prev) + _bmm(a_intra, v_new)
        s_new = s_prev * e_all + _bmm_tn(ks, v_new)
    return o, s_new


def _deltanet_kernel(seq, reqs, zero_init, pre_conv, *refs):
    qkv_ref, cw_ref, z_ref, ba_ref, prm_ref, gn_ref = refs[:6]
    refs = refs[6:]
    if not zero_init:
        s0f_ref, s0b_ref = refs[:2]
        refs = refs[2:]
    o_ref, sf_ref, sb_ref, q_s, k_s, v_s, g_s, b_s, o_s, st_s = refs
    c = DN_CHUNK
    n = seq // c
    nh = DN_HEADS
    nf = reqs * nh
    lanes = lambda j: slice(j * LANES, (j + 1) * LANES)

    if not pre_conv:
        row = _iota((seq, LANES), 0)
        first_row = row == 0
        last_row = row == seq - 1
    for r in range(reqs):
        rows = slice(r * seq, (r + 1) * seq)

        def qkv_part(j):
            x = qkv_ref[rows, lanes(j)].astype(F32)
            return x if pre_conv else _silu(_short_conv3(x, cw_ref[:, lanes(j)], first_row, last_row))

        for h in range(nh):
            q = qkv_part(h)
            k = qkv_part(nh + h)
            q_s[r * nh + h] = q if pre_conv else _l2norm(q, DN_DK ** -0.5)
            k_s[r * nh + h] = k if pre_conv else _l2norm(k)
            v_s[r * nh + h] = qkv_part(2 * nh + h)

    ba = ba_ref[...]
    b_s[...] = _sigmoid(ba)
    g_s[...] = -jnp.exp(prm_ref[0:1, :]) * _softplus(ba + prm_ref[1:2, :])

    ri = _iota((c, c), 0)
    ci = _iota((c, c), 1)
    tri_l = jnp.where(ri >= ci, 1.0, 0.0).astype(F32)
    tri_u = jnp.where(ri <= ci, 1.0, 0.0).astype(F32)

    def chunk_pair(idx_f, idx_b, s_prev):
        def ds(idx, base=0):
            start = idx * c
            start = start if isinstance(start, int) else pl.multiple_of(start, c)
            return pl.ds(base + start, c)

        col_f, col_b, row_f, row_b, beta_f, beta_b = [], [], [], [], [], []
        for r in range(reqs):
            cs_f = _mm_f32(tri_l, g_s[ds(idx_f, r * seq), :])
            cs_b = _mm_f32(tri_u, g_s[ds(idx_b, r * seq), :])
            cst_f = cs_f.T
            cst_b = cs_b.T
            b_f = b_s[ds(idx_f, r * seq), :]
            b_b = b_s[ds(idx_b, r * seq), :]
            for h in range(nh):
                jf, jb = 2 * nh + h, 3 * nh + h
                col_f.append(cs_f[:, jf:jf + 1])
                col_b.append(cs_b[:, jb:jb + 1])
                row_f.append(cst_f[jf:jf + 1, :])
                row_b.append(cst_b[jb:jb + 1, :])
                beta_f.append(b_f[:, h:h + 1])
                beta_b.append(b_b[:, nh + h:nh + h + 1])
        gc_col = jnp.stack(col_f + col_b, 0)
        gc_row = jnp.stack(row_f + row_b, 0)
        beta = jnp.stack(beta_f + beta_b, 0)
        e_col = jnp.exp(gc_col)
        gl = jnp.concatenate([gc_col[:nf, c - 1:c, :], gc_col[nf:, 0:1, :]], 0)
        e_rest = jnp.exp(gl - gc_col)
        e_all = jnp.exp(gl)
        sl_f, sl_b = ds(idx_f), ds(idx_b)
        both = lambda ref: jnp.concatenate([ref[:, sl_f, :], ref[:, sl_b, :]], 0)
        o, s_new = _dn_chunk(both(q_s), both(k_s), both(v_s), gc_col, gc_row, beta, e_col, e_rest, e_all,
                             s_prev, n == 1, ri, ci)
        o_s[:nf, sl_f, :] = o[:nf]
        o_s[nf:, sl_b, :] = o[nf:]
        return s_new

    if not zero_init:
        s0 = jnp.concatenate([s0f_ref[r] for r in range(reqs)] + [s0b_ref[r] for r in range(reqs)], 0)
    if n == 1:
        s_fin = chunk_pair(0, 0, None if zero_init else s0)
    else:
        st_s[...] = jnp.zeros(st_s.shape, F32) if zero_init else s0

        def body(i, carry):
            st_s[...] = chunk_pair(i, n - 1 - i, st_s[...])
            return carry

        lax.fori_loop(0, n, body, 0)
        s_fin = st_s[...]
    for r in range(reqs):
        sf_ref[r] = s_fin[r * nh:(r + 1) * nh]
        sb_ref[r] = s_fin[nf + r * nh:nf + (r + 1) * nh]

    gn = gn_ref[...]
    for r in range(reqs):
        rows = slice(r * seq, (r + 1) * seq)
        for h in range(nh):
            o = o_s[r * nh + h] + o_s[nf + r * nh + h]
            o = o * lax.rsqrt(jnp.mean(o * o, axis=-1, keepdims=True) + RMS_EPS) * gn
            o_ref[rows, lanes(h)] = (o * _silu(z_ref[rows, lanes(h)].astype(F32))).astype(o_ref.dtype)


def _deltanet_call(qkv, z, ba, conv_w, prm, gn, s0f, s0b, bsz, seq, reqs, pre_conv, act_dt):
    nh = DN_HEADS
    zero_init = s0f is None
    rows = reqs * seq
    st = pl.BlockSpec((reqs, nh, DN_DK, DN_DV), lambda b: (b, 0, 0, 0))
    in_specs = [pl.BlockSpec((rows, QKV_WIDTH), lambda b: (b, 0)),
                pl.BlockSpec((3, QKV_WIDTH), lambda b: (0, 0)),
                pl.BlockSpec((rows, DN_WIDTH), lambda b: (b, 0)),
                pl.BlockSpec((rows, LANES), lambda b: (b, 0)),
                pl.BlockSpec((8, LANES), lambda b: (0, 0)),
                pl.BlockSpec((1, LANES), lambda b: (0, 0))]
    args = [qkv, conv_w, z, ba, prm, gn]
    if not zero_init:
        in_specs += [st, st]
        args += [s0f, s0b]
    vm = lambda shape: pltpu.VMEM(shape, F32)
    per_head = vm((reqs * nh, seq, LANES))
    return pl.pallas_call(
        functools.partial(_deltanet_kernel, seq, reqs, zero_init, pre_conv),
        grid=(bsz // reqs,),
        in_specs=in_specs,
        out_specs=[pl.BlockSpec((rows, DN_WIDTH), lambda b: (b, 0)), st, st],
        out_shape=[jax.ShapeDtypeStruct((bsz * seq, DN_WIDTH), act_dt),
                   jax.ShapeDtypeStruct((bsz, nh, DN_DK, DN_DV), F32),
                   jax.ShapeDtypeStruct((bsz, nh, DN_DK, DN_DV), F32)],
        scratch_shapes=[per_head, per_head, per_head, vm((rows, LANES)), vm((rows, LANES)),
                        vm((2 * reqs * nh, seq, LANES)), vm((2 * reqs * nh, DN_DK, DN_DV))],
        compiler_params=_cparams(1),
        name="deltanet",
    )(*args)


def _dft_mats(seq):
    n = 2 * seq
    f = np.arange(seq)[:, None]
    s = np.arange(seq)[None, :]
    ang = 2.0 * np.pi * ((f * s) % n) / n
    fwd = np.concatenate([np.cos(ang), -np.sin(ang)], axis=0)
    fwd[seq, :] = np.cos(np.pi * np.arange(seq))
    t = (np.arange(seq) + seq // 2)[:, None]
    ff = np.arange(seq)[None, :]
    ang2 = 2.0 * np.pi * ((t * ff) % n) / n
    inv_r = 2.0 * np.cos(ang2) / n
    inv_i = -2.0 * np.sin(ang2) / n
    inv_r[:, 0] = 1.0 / n
    inv_i[:, 0] = np.cos(np.pi * t[:, 0]) / n
    inv = np.concatenate([inv_r, inv_i], axis=1)
    return fwd.astype(np.float32), inv.astype(np.float32)


def _filter_feats(seq):
    t = np.linspace(0.0, 1.0, seq)[:, None]
    bands = (HY_EMB - 1) // 2
    ang = (2.0 * math.pi * np.arange(seq) / seq)[:, None] * np.linspace(1e-4, bands - 1, bands)[None, :]
    feats = np.concatenate([t, np.cos(ang), -np.sin(ang)], -1)
    deltas = np.abs(np.linspace(math.log(HY_TARGET) / HY_SLOW_DECAY, math.log(HY_TARGET) / HY_FAST_DECAY,
                                HY_WIDTH))
    offset = np.abs(np.arange(seq) - seq // 2) / (seq // 2)
    window = np.exp(-offset[:, None] * deltas[None, :])
    feats = np.pad(feats, ((0, 0), (0, LANES - HY_EMB)))
    return jnp.asarray(feats, dtype=F32), jnp.asarray(window, dtype=F32)


def _hfilter_kernel(seq, feats_ref, win_ref, w1_ref, b1_ref, w2_ref, b2_ref, w3_ref, fr_ref, fwd_hi_ref,
                    fwd_lo_ref, ha_ref, hb_ref, hd_ref):
    fr = fr_ref[...]
    hid = jnp.sin(fr * (_mm_f32(feats_ref[...], w1_ref[...]) + b1_ref[...]))
    hid = jnp.sin(fr * (_mm_f32(hid, w2_ref[...]) + b2_ref[...]))
    filt = _mm_f32(hid, w3_ref[...]) * win_ref[...]
    filt = filt / (jnp.sum(jnp.abs(filt), axis=0, keepdims=True) + 1e-6)
    filt_hi = filt.astype(BF16)
    filt_lo = (filt - filt_hi.astype(F32)).astype(BF16)
    fwd_hi = fwd_hi_ref[...]
    spec = (jnp.dot(fwd_hi, filt_hi, preferred_element_type=F32)
            + jnp.dot(fwd_hi, filt_lo, preferred_element_type=F32)
            + jnp.dot(fwd_lo_ref[...], filt_hi, preferred_element_type=F32))
    h_re = spec[:seq, :]
    h_im = spec[seq:, :]
    first = _iota(h_re.shape, 0) == 0
    ha_ref[...] = h_re
    hb_ref[...] = jnp.where(first, 0.0, h_im)
    hd_ref[...] = jnp.where(first, h_im, h_re)


def _hfilter_call(seq, w1, b1, w2, b2, w3, freq, fwd_hi, fwd_lo):
    feats, window = _filter_feats(seq)
    w1p = jnp.pad(w1, ((0, LANES - HY_EMB), (0, 0)))
    full = lambda a: pl.BlockSpec(a.shape, lambda i: (0,) * a.ndim)
    args = [feats, window, w1p, b1.reshape(1, -1), w2, b2.reshape(1, -1), w3, freq.reshape(1, -1), fwd_hi,
            fwd_lo]
    out = jax.ShapeDtypeStruct((seq, HY_WIDTH), F32)
    return pl.pallas_call(
        functools.partial(_hfilter_kernel, seq),
        grid=(1,),
        in_specs=[full(a) for a in args],
        out_specs=[pl.BlockSpec((seq, HY_WIDTH), lambda i: (0, 0))] * 3,
        out_shape=[out, out, out],
        compiler_params=_cparams(1),
        name="hfilter",
    )(*args)


def _hyena_kernel(seq, pre_conv, *refs):
    if pre_conv:
        x0_ref, uu_ref, ha_ref, hb_ref, hd_ref, skip_ref, fwd_ref, inv_ref, o_ref = refs
        x0 = x0_ref[...].astype(F32)
        uu = uu_ref[...].astype(F32)
    else:
        (x0_ref, x1_ref, v_ref, c0_ref, c1_ref, c2_ref, b0_ref, b1_ref, b2_ref,
         ha_ref, hb_ref, hd_ref, skip_ref, fwd_ref, inv_ref, o_ref) = refs
        row = _iota((seq, HY_CH_BLOCK), 0)
        first = row == 0
        last = row == seq - 1
        conv = lambda x_ref, w_ref, b_ref: _short_conv3(x_ref[...].astype(F32), w_ref[...], first, last) + b_ref[...]
        x0 = conv(x0_ref, c0_ref, b0_ref)
        uu = conv(x1_ref, c1_ref, b1_ref) * conv(v_ref, c2_ref, b2_ref)
    spec = jnp.dot(fwd_ref[...], uu.astype(BF16), preferred_element_type=F32)
    u_re = spec[:seq, :]
    u_im = spec[seq:, :]
    hb = hb_ref[...]
    y_re = u_re * ha_ref[...] - u_im * hb
    y_im = u_re * hb + u_im * hd_ref[...]
    y = jnp.concatenate([y_re, y_im], axis=0).astype(BF16)
    cv = jnp.dot(inv_ref[...], y, preferred_element_type=F32)
    o_ref[...] = (x0 * (cv + uu * skip_ref[...])).astype(o_ref.dtype)


def _hyena_call(hy, pre_conv, conv_w, conv_b, ha, hb, hd, skip, fwd, inv, bsz, seq, act_dt):
    cb = HY_CH_BLOCK
    nblk = HY_WIDTH // cb
    tok = lambda off: pl.BlockSpec((seq, cb), lambda b, j: (b, off + j))
    cw = lambda off: pl.BlockSpec((3, cb), lambda b, j: (0, off + j))
    bias = lambda off: pl.BlockSpec((1, cb), lambda b, j: (0, off + j))
    ch = pl.BlockSpec((seq, cb), lambda b, j: (0, j))
    const = lambda a: pl.BlockSpec(a.shape, lambda b, j: (0, 0), pipeline_mode=pl.Buffered(1))
    tail_specs = [ch, ch, ch, pl.BlockSpec((1, cb), lambda b, j: (0, j)), const(fwd), const(inv)]
    tail_args = [ha, hb, hd, skip.reshape(1, -1), fwd, inv]
    if pre_conv:
        in_specs = [tok(0), tok(nblk)] + tail_specs
        args = [hy, hy] + tail_args
    else:
        conv_b2 = conv_b.reshape(1, -1)
        in_specs = [tok(0), tok(nblk), tok(2 * nblk), cw(0), cw(nblk), cw(2 * nblk),
                    bias(0), bias(nblk), bias(2 * nblk)] + tail_specs
        args = [hy, hy, hy, conv_w, conv_w, conv_w, conv_b2, conv_b2, conv_b2] + tail_args
    return pl.pallas_call(
        functools.partial(_hyena_kernel, seq, pre_conv),
        grid=(bsz, nblk),
        in_specs=in_specs,
        out_specs=tok(0),
        out_shape=jax.ShapeDtypeStruct((bsz * seq, HY_WIDTH), act_dt),
        compiler_params=_cparams(2),
        name="hyena",
    )(*args)


def _mix_kernel(has_pos, *refs):
    refs = list(refs)
    x_ref = refs.pop(0)
    pos_ref = refs.pop(0) if has_pos else None
    (oa_ref, ob_ref, g_ref, mod_ref, wua, wub, wout, l1g, l1b, wr_hi, wr_lo, x1_o, u2_o, p_o) = refs
    d = D_MODEL
    tm = x_ref.shape[0]
    sub = min(tm, MIX_SUB_ROWS)
    blocks = [slice(r0, r0 + sub) for r0 in range(0, tm, sub)]
    g1 = mod_ref[0, 2]
    sh2 = mod_ref[0, 3]
    sc2 = mod_ref[0, 4]

    mixed = []
    for rows in blocks:
        gates = g_ref[rows, :].astype(F32)
        ga = _sigmoid(gates[:, :d])
        gb = _sigmoid(gates[:, d:])
        up = (ga * jnp.dot(oa_ref[rows, :], wua[...], preferred_element_type=F32)
              + gb * jnp.dot(ob_ref[rows, :], wub[...], preferred_element_type=F32))
        mixed.append(jnp.dot(up.astype(BF16), wout[...], preferred_element_type=F32))

    for rows, mx in zip(blocks, mixed):
        x = x_ref[rows, :]
        if has_pos:
            x = x + pos_ref[rows, :]
        x1 = _ln(ALPHA * x + g1 * mx) * l1g[...] + l1b[...]
        x1_o[rows, :] = x1
        u2 = _ln(x1) * (1.0 + sc2) + sh2
        u2_hi = u2.astype(BF16)
        u2_o[rows, :] = u2_hi
        u2_lo = (u2 - u2_hi.astype(F32)).astype(BF16)
        logits = (jnp.dot(u2_hi, wr_hi[...], preferred_element_type=F32)
                  + jnp.dot(u2_hi, wr_lo[...], preferred_element_type=F32)
                  + jnp.dot(u2_lo, wr_hi[...], preferred_element_type=F32))
        lane = _iota(logits.shape, 1)
        logits = jnp.where(lane < N_EXPERTS, logits, -jnp.inf)
        m = jnp.max(logits, axis=-1, keepdims=True)
        e = jnp.exp(logits - m)
        p_o[rows, :] = e / jnp.sum(e, axis=-1, keepdims=True)


def _mix_call(x2d, pos, o_a, o_b, gates, mod4, mod_row, wua, wub, wout, l1g, l1b, wr_hi, wr_lo):
    n, d = x2d.shape
    tm = min(MIX_TILE, n)
    assert n % tm == 0
    row = lambda i: (i, 0)
    const = lambda i: (0, 0)
    in_specs = [pl.BlockSpec((tm, d), row)]
    args = [x2d]
    if pos is not None:
        tiles = pos.shape[0] // tm
        in_specs.append(pl.BlockSpec((tm, d), lambda i: (i % tiles, 0)))
        args.append(pos)
    in_specs += [pl.BlockSpec((tm, DN_WIDTH), row), pl.BlockSpec((tm, HY_WIDTH), row),
                 pl.BlockSpec((tm, 2 * d), row),
                 _mod_spec(lambda i: mod_row(i * tm))]
    args += [o_a, o_b, gates, mod4]
    for w in (wua, wub, wout, l1g, l1b, wr_hi, wr_lo):
        in_specs.append(pl.BlockSpec(w.shape, const, pipeline_mode=pl.Buffered(1)))
        args.append(w)
    return pl.pallas_call(
        functools.partial(_mix_kernel, pos is not None),
        grid=(n // tm,),
        in_specs=in_specs,
        out_specs=[pl.BlockSpec((tm, d), row), pl.BlockSpec((tm, d), row), pl.BlockSpec((tm, LANES), row)],
        out_shape=[jax.ShapeDtypeStruct((n, d), F32), jax.ShapeDtypeStruct((n, d), BF16),
                   jax.ShapeDtypeStruct((n, LANES), F32)],
        compiler_params=_cparams(1),
        name="mix",
    )(*args)


def _route_kernel(seq, cap, p_ref, u_ref, tri_ref, xs_ref, g_ref, pt_ref):
    e_n = N_EXPERTS
    p = p_ref[...]
    pt = p.T[:e_n, :]
    tri = tri_ref[...]

    if seq <= RANK_COUNT_MAX_SEQ:
        earlier = _iota((seq, seq), 0) < _iota((seq, seq), 1)
        ranks = []
        for e in range(e_n):
            pc = p[:, e:e + 1]
            pr = pt[e:e + 1, :]
            beats = (pc > pr) | (earlier & (pc == pr))
            ranks.append(jnp.sum(jnp.where(beats, 1.0, 0.0), axis=0, keepdims=True))
        sel = jnp.concatenate(ranks, axis=0) < cap
    else:
        def search(i, cur):
            cand = cur | (1 << (30 - i))
            cand_f = pltpu.bitcast(cand, F32)
            cnt = jnp.sum(jnp.where(pt >= cand_f, 1.0, 0.0), axis=1, keepdims=True)
            return jnp.where((cnt >= cap) & (cand >= MIN_NORMAL_F32_BITS), cand, cur)

        thr = pltpu.bitcast(lax.fori_loop(0, 31, search, jnp.zeros((e_n, 1), jnp.int32)), F32)
        gt = pt > thr
        eq = pt == thr
        n_gt = jnp.sum(jnp.where(gt, 1.0, 0.0), axis=1, keepdims=True)
        eq_rank = jnp.dot(jnp.where(eq, 1.0, 0.0).astype(BF16), tri, preferred_element_type=F32)
        sel = gt | (eq & (eq_rank < cap - n_gt))
    pos = jnp.dot(jnp.where(sel, 1.0, 0.0).astype(BF16), tri, preferred_element_type=F32)
    slot = jnp.where(sel, pos, -1.0)
    pt_ref[0] = pt
    jrow = _iota((cap, seq), 0).astype(F32)
    for e in range(e_n):
        g_ref[0, e * cap:(e + 1) * cap, :] = jnp.where(slot[e:e + 1, :] == jrow, 1.0, 0.0).astype(BF16)
    xs = jnp.dot(g_ref[0], u_ref[...], preferred_element_type=F32)
    for e in range(e_n):
        xs_ref[e] = xs[e * cap:(e + 1) * cap, :].astype(xs_ref.dtype)


def _route_call(probs, u2, bsz, seq):
    cap = EC_FACTOR * seq // N_EXPERTS
    tri = jnp.asarray(np.triu(np.ones((seq, seq), np.float32), 1), dtype=BF16)
    return pl.pallas_call(
        functools.partial(_route_kernel, seq, cap),
        grid=(bsz,),
        in_specs=[pl.BlockSpec((seq, LANES), lambda b: (b, 0)),
                  pl.BlockSpec((seq, D_MODEL), lambda b: (b, 0)),
                  pl.BlockSpec((seq, seq), lambda b: (0, 0), pipeline_mode=pl.Buffered(1))],
        out_specs=[pl.BlockSpec((N_EXPERTS, cap, D_MODEL), lambda b: (0, b, 0)),
                   pl.BlockSpec((1, N_EXPERTS * cap, seq), lambda b: (b, 0, 0)),
                   pl.BlockSpec((1, N_EXPERTS, seq), lambda b: (b, 0, 0))],
        out_shape=[jax.ShapeDtypeStruct((N_EXPERTS, bsz * cap, D_MODEL), BF16),
                   jax.ShapeDtypeStruct((bsz, N_EXPERTS * cap, seq), BF16),
                   jax.ShapeDtypeStruct((bsz, N_EXPERTS, seq), F32)],
        compiler_params=_cparams(1),
        name="route",
    )(probs, u2, tri)


def _expert_kernel(xc_ref, xd_ref, wg_ref, wu_ref, wd_ref, yc_ref, yd_ref):
    wg = wg_ref[0].astype(BF16)
    wu = wu_ref[0].astype(BF16)
    wd = wd_ref[0].astype(BF16)
    for x_ref, y_ref in ((xc_ref, yc_ref), (xd_ref, yd_ref)):
        rows = x_ref.shape[1]
        rb = min(rows, EXPERT_ROW_BLOCK)
        for r0 in range(0, rows, rb):
            x = x_ref[0, r0:r0 + rb, :]
            hg = jnp.dot(x, wg, preferred_element_type=F32)
            hu = jnp.dot(x, wu, preferred_element_type=F32)
            y = jnp.dot((_silu(hg) * hu).astype(BF16), wd, preferred_element_type=F32)
            y_ref[0, r0:r0 + rb, :] = y.astype(y_ref.dtype)


def _expert_call(xs_c, xs_d, w_gate, w_up, w_down):
    e_n, rc, d = xs_c.shape
    rd = xs_d.shape[1]
    ff = w_gate.shape[2]
    return pl.pallas_call(
        _expert_kernel,
        grid=(e_n,),
        in_specs=[pl.BlockSpec((1, rc, d), lambda e: (e, 0, 0)),
                  pl.BlockSpec((1, rd, d), lambda e: (e, 0, 0)),
                  pl.BlockSpec((1, d, ff), lambda e: (e, 0, 0)),
                  pl.BlockSpec((1, d, ff), lambda e: (e, 0, 0)),
                  pl.BlockSpec((1, ff, d), lambda e: (e, 0, 0))],
        out_specs=[pl.BlockSpec((1, rc, d), lambda e: (e, 0, 0)),
                   pl.BlockSpec((1, rd, d), lambda e: (e, 0, 0))],
        out_shape=[jax.ShapeDtypeStruct((e_n, rc, d), BF16), jax.ShapeDtypeStruct((e_n, rd, d), BF16)],
        compiler_params=_cparams(1),
        name="experts",
    )(xs_c, xs_d, w_gate, w_up, w_down)


def _final_kernel(seq, cap, x1_ref, y_ref, g_ref, pt_ref, mod_ref, l2g, l2b, o_ref, yw_s):
    pt = pt_ref[0]
    for e in range(N_EXPERTS):
        onehot = g_ref[0, e * cap:(e + 1) * cap, :].astype(F32)
        aff = jnp.sum(onehot * pt[e:e + 1, :], axis=1, keepdims=True)
        yw_s[e * cap:(e + 1) * cap, :] = (y_ref[e].astype(F32) * aff).astype(BF16)
    ffn = lax.dot_general(g_ref[0], yw_s[...], (((0,), (0,)), ((), ())), preferred_element_type=F32)
    g2 = mod_ref[0, 5]
    o_ref[...] = _ln(ALPHA * x1_ref[...] + g2 * ffn) * l2g[...] + l2b[...]


def _final_call(x1, ys, gmat, pt, mod4, mod_row, l2g, l2b, bsz, seq):
    cap = EC_FACTOR * seq // N_EXPERTS
    d = D_MODEL
    return pl.pallas_call(
        functools.partial(_final_kernel, seq, cap),
        grid=(bsz,),
        in_specs=[pl.BlockSpec((seq, d), lambda b: (b, 0)),
                  pl.BlockSpec((N_EXPERTS, cap, d), lambda b: (0, b, 0)),
                  pl.BlockSpec((1, N_EXPERTS * cap, seq), lambda b: (b, 0, 0)),
                  pl.BlockSpec((1, N_EXPERTS, seq), lambda b: (b, 0, 0)),
                  _mod_spec(lambda b: mod_row(b * seq)),
                  pl.BlockSpec((1, d), lambda b: (0, 0)),
                  pl.BlockSpec((1, d), lambda b: (0, 0))],
        out_specs=pl.BlockSpec((seq, d), lambda b: (b, 0)),
        out_shape=jax.ShapeDtypeStruct((bsz * seq, d), F32),
        scratch_shapes=[pltpu.VMEM((N_EXPERTS * cap, d), BF16)],
        compiler_params=_cparams(1),
        name="final",
    )(x1, ys, gmat, pt, mod4, l2g, l2b)


def _grid_pos_embed(rows, dim):
    r = np.repeat(np.arange(rows), GRID_W)
    col = np.tile(np.arange(GRID_W), rows)
    quarter = dim // 4
    omega = 1.0 / (10000.0 ** (np.arange(quarter) / quarter))

    def enc(p):
        ang = p[:, None] * omega[None, :]
        return np.concatenate([np.sin(ang), np.cos(ang)], -1)

    return jnp.asarray(np.concatenate([enc(r), enc(col)], -1), dtype=F32)


ACT_DT = BF16


def kernel(x_prompt, x_sample, state_delta_fwd, state_delta_bwd, c, c_ctx, w_mod, b_mod, w_in, conv_a_w, a_log, dt_bias, gnorm_w, conv_b_w, conv_b_b, hy_w1, hy_b1, hy_w2, hy_b2, hy_w3, hy_freq, hy_bias, w_up_a, w_up_b, w_out, ln1_g, ln1_b, w_router, w_e_gate, w_e_up, w_e_down, ln2_g, ln2_b):
    d = D_MODEL
    n_ctx, l_ctx, _ = x_prompt.shape
    n_dec, l_dec, _ = x_sample.shape
    lyr = 0

    cond_t = jnp.zeros((d, 8), F32).at[:, 0].set(c_ctx).at[:, 1:1 + n_dec].set(c.T)
    mod4 = _mod_call(cond_t, 1 + n_dec, w_mod[lyr], b_mod[lyr]).reshape(8, N_MOD, 1, d)

    in_w = w_in[lyr].astype(BF16)

    prm = jnp.zeros((8, LANES), F32)
    prm = prm.at[0, 2 * DN_HEADS:4 * DN_HEADS].set(a_log[lyr].reshape(-1))
    prm = prm.at[1, 2 * DN_HEADS:4 * DN_HEADS].set(dt_bias[lyr].reshape(-1))
    gn = gnorm_w[lyr].reshape(1, -1)

    wua = w_up_a[lyr].astype(BF16)
    wub = w_up_b[lyr].astype(BF16)
    wout = w_out[lyr].astype(BF16)
    wr = jnp.pad(w_router[lyr], ((0, 0), (0, LANES - N_EXPERTS)))
    wr_hi = wr.astype(BF16)
    wr_lo = (wr - wr_hi.astype(F32)).astype(BF16)
    l1g, l1b = ln1_g[lyr].reshape(1, -1), ln1_b[lyr].reshape(1, -1)
    l2g, l2b = ln2_g[lyr].reshape(1, -1), ln2_b[lyr].reshape(1, -1)

    def front(x2d, pos, bsz, seq, mod_row, s0f, s0b):
        (qkv, z, ba, hy, gates), pre_conv = _inproj_call(x2d, pos, mod4, mod_row, in_w, conv_a_w[lyr],
                                                          conv_b_w[lyr], conv_b_b[lyr], seq, ACT_DT)
        reqs = DN_REQS_PER_STEP if seq == DN_CHUNK and bsz % DN_REQS_PER_STEP == 0 else 1
        o_a, s_f, s_b = _deltanet_call(qkv, z, ba, conv_a_w[lyr], prm, gn, s0f, s0b, bsz, seq, reqs,
                                       pre_conv, ACT_DT)
        fwd, inv = (jnp.asarray(m) for m in _dft_mats(seq))
        fwd_hi = fwd.astype(BF16)
        fwd_lo = (fwd - fwd_hi.astype(F32)).astype(BF16)
        ha, hb, hd = _hfilter_call(seq, hy_w1[lyr], hy_b1[lyr], hy_w2[lyr], hy_b2[lyr], hy_w3[lyr],
                                   hy_freq[lyr], fwd_hi, fwd_lo)
        o_h = _hyena_call(hy, pre_conv, conv_b_w[lyr], conv_b_b[lyr], ha, hb, hd, hy_bias[lyr],
                          fwd_hi, inv.astype(BF16), bsz, seq, ACT_DT)
        x1, u2, probs = _mix_call(x2d, pos, o_a, o_h, gates, mod4, mod_row, wua, wub, wout, l1g, l1b,
                                   wr_hi, wr_lo)
        xs, gmat, pt = _route_call(probs, u2, bsz, seq)
        return x1, xs, gmat, pt, s_f, s_b

    row_ctx = lambda tok: 0
    row_dec = lambda tok: 1 + tok // l_dec

    xc = x_prompt.reshape(n_ctx * l_ctx, d)
    xd = x_sample.reshape(n_dec * l_dec, d)
    pos = _grid_pos_embed(l_dec // GRID_W, d)

    x1c, xsc, gc, ptc, s_f, s_b = front(xc, None, n_ctx, l_ctx, row_ctx, None, None)
    x1d, xsd, gd, ptd, _, _ = front(xd, pos, n_dec, l_dec, row_dec,
                                    state_delta_fwd[:, lyr], state_delta_bwd[:, lyr])
    yc, yd = _expert_call(xsc, xsd, w_e_gate[lyr], w_e_up[lyr], w_e_down[lyr])
    y_prompt = _final_call(x1c, yc, gc, ptc, mod4, row_ctx, l2g, l2b, n_ctx, l_ctx)
    y_sample = _final_call(x1d, yd, gd, ptd, mod4, row_dec, l2g, l2b, n_dec, l_dec)

    return (y_prompt.reshape(n_ctx, l_ctx, d), y_sample.reshape(n_dec, l_dec, d),
            s_f[:, None], s_b[:, None])
```

```python
import functools
import math

import jax
import jax.numpy as jnp
import numpy as np
from jax import lax
from jax.experimental import pallas as pl
from jax.experimental.pallas import tpu as pltpu

F32 = jnp.float32
BF16 = jnp.bfloat16
HIGHEST = lax.Precision.HIGHEST

D_MODEL = 1024
DEPTH = 1
GRID_W = 64
DN_HEADS = 4
DN_DK = 128
DN_DV = 128
DN_WIDTH = DN_HEADS * DN_DV
HY_WIDTH = D_MODEL // 2
HY_EMB = 33
HY_FFN = 64
HY_TARGET = 1e-2
HY_FAST_DECAY = 0.3
HY_SLOW_DECAY = 1.5
N_EXPERTS = 16
EC_FACTOR = 2
N_MOD = 6
ALPHA = (2 * DEPTH) ** 0.25
LN_EPS = 1e-5
RMS_EPS = 1e-6
QKV_WIDTH = 2 * DN_HEADS * DN_DK + DN_WIDTH

LANES = 128
INPROJ_TILE = 512
MIX_TILE = 1024
MIX_SUB_ROWS = 256
MIX_MIN_STEPS = 4
MOD_COL_BLOCK = 1024
DN_CHUNK = 256
INV_BASE = 16
DN_REQS_PER_STEP = 2
HY_CH_BLOCK = 512
EXPERT_ROW_BLOCK = 512
VMEM_LIMIT = 56 * 1024 * 1024
MIN_NORMAL_F32_BITS = 0x00800000
ACT_DT = BF16
RANK_COUNT_MAX_SEQ = 256


def _cparams(n_axes):
    return pltpu.CompilerParams(dimension_semantics=("arbitrary",) * n_axes,
                                vmem_limit_bytes=VMEM_LIMIT)


def _bmm(a, b):
    return jnp.einsum("bmk,bkn->bmn", a.astype(BF16), b.astype(BF16), preferred_element_type=F32)


def _bmm_nt(a, b):
    return jnp.einsum("bmk,bnk->bmn", a.astype(BF16), b.astype(BF16), preferred_element_type=F32)


def _bmm_tn(a, b):
    return jnp.einsum("bkm,bkn->bmn", a.astype(BF16), b.astype(BF16), preferred_element_type=F32)


def _mm_f32(a, b):
    return jnp.dot(a, b, precision=HIGHEST, preferred_element_type=F32)


def _sigmoid(x):
    return 0.5 * jnp.tanh(0.5 * x) + 0.5


def _silu(x):
    return x * _sigmoid(x)


def _softplus(x):
    return jnp.maximum(x, 0.0) + jnp.log1p(jnp.exp(-jnp.abs(x)))


def _ln(x):
    mu = jnp.mean(x, axis=-1, keepdims=True)
    xc = x - mu
    var = jnp.mean(xc * xc, axis=-1, keepdims=True)
    return xc * lax.rsqrt(var + LN_EPS)


def _l2norm(x, scale=1.0):
    return x * (lax.rsqrt(jnp.sum(x * x, axis=-1, keepdims=True) + 1e-6) * scale)


def _iota(shape, dim):
    return lax.broadcasted_iota(jnp.int32, shape, dim)


def _short_conv3(x, w, first, last):
    xm = jnp.where(first, 0.0, pltpu.roll(x, 1, 0))
    xp = jnp.where(last, 0.0, pltpu.roll(x, x.shape[0] - 1, 0))
    return w[0:1, :] * xm + w[1:2, :] * x + w[2:3, :] * xp


def _mod_kernel(n_rows, ct_ref, w_ref, b_ref, o_ref):
    s = _silu(ct_ref[...])
    w = w_ref[...]
    b = b_ref[...]
    o_ref[...] = jnp.zeros(o_ref.shape, F32)
    for r in range(n_rows):
        o_ref[r:r + 1, :] = jnp.sum(w * s[:, r:r + 1], axis=0, keepdims=True) + b


def _mod_call(cond_t, n_rows, w_mod, b_mod):
    d = D_MODEL
    tn = MOD_COL_BLOCK
    return pl.pallas_call(
        functools.partial(_mod_kernel, n_rows),
        grid=(N_MOD * d // tn,),
        in_specs=[pl.BlockSpec((d, 8), lambda j: (0, 0)),
                  pl.BlockSpec((d, tn), lambda j: (0, j)),
                  pl.BlockSpec((1, tn), lambda j: (0, j))],
        out_specs=pl.BlockSpec((8, tn), lambda j: (0, j)),
        out_shape=jax.ShapeDtypeStruct((8, N_MOD * d), F32),
        compiler_params=_cparams(1),
        name="mod",
    )(cond_t, w_mod, b_mod.reshape(1, -1))


def _mod_spec(row_of_step):
    return pl.BlockSpec((1, N_MOD, 1, D_MODEL), lambda *idx: (row_of_step(*idx), 0, 0, 0))


def _inproj_kernel(has_pos, conv_seq, *refs):
    refs = list(refs)
    x_ref = refs.pop(0)
    pos_ref = refs.pop(0) if has_pos else None
    mod_ref, w_ref = refs.pop(0), refs.pop(0)
    if conv_seq:
        cwa_ref, cwb_ref, cbb_ref = refs.pop(0), refs.pop(0), refs.pop(0)
    qkv_o, z_o, ba_o, hy_o, g_o = refs
    sh1 = mod_ref[0, 0]
    sc1 = mod_ref[0, 1]
    o_z = QKV_WIDTH
    o_ba = o_z + DN_WIDTH
    n_ba = 4 * DN_HEADS
    o_g = n_ba + 3 * HY_WIDTH
    lanes = lambda j: slice(j * LANES, (j + 1) * LANES)
    tm = x_ref.shape[0]
    sub = conv_seq if conv_seq else tm
    blocks = [slice(r0, r0 + sub) for r0 in range(0, tm, sub)]

    proj = []
    for rows in blocks:
        x = x_ref[rows, :]
        if has_pos:
            x = x + pos_ref[rows, :]
        u = (_ln(x) * (1.0 + sc1) + sh1).astype(BF16)
        proj.append((jnp.dot(u, w_ref[:, :o_ba], preferred_element_type=F32),
                     jnp.dot(u, w_ref[:, o_ba:], preferred_element_type=F32)))

    if conv_seq:
        row = _iota((sub, LANES), 0)
        first = row == 0
        last = row == sub - 1
    for rows, (head, tail) in zip(blocks, proj):
        z_o[rows, :] = head[:, o_z:].astype(z_o.dtype)
        ba_o[rows, :] = tail[:, :LANES]
        g_o[rows, :] = tail[:, o_g:].astype(g_o.dtype)
        hy = tail[:, n_ba:o_g]
        if not conv_seq:
            qkv_o[rows, :] = head[:, :o_z].astype(qkv_o.dtype)
            hy_o[rows, :] = hy.astype(hy_o.dtype)
            continue
        for j in range(QKV_WIDTH // LANES):
            y = _silu(_short_conv3(head[:, lanes(j)], cwa_ref[:, lanes(j)], first, last))
            if j < 2 * DN_HEADS:
                y = _l2norm(y, DN_DK ** -0.5 if j < DN_HEADS else 1.0)
            qkv_o[rows, lanes(j)] = y.astype(qkv_o.dtype)
        nj = HY_WIDTH // LANES
        for j in range(nj):
            parts = [_short_conv3(hy[:, lanes(p * nj + j)], cwb_ref[:, lanes(p * nj + j)], first, last)
                     + cbb_ref[:, lanes(p * nj + j)] for p in range(3)]
            hy_o[rows, lanes(j)] = parts[0].astype(hy_o.dtype)
            hy_o[rows, lanes(nj + j)] = (parts[1] * parts[2]).astype(hy_o.dtype)


def _inproj_call(x2d, pos, mod4, mod_row, w, conv_a_w, conv_b_w, conv_b_b, seq, act_dt):
    n, d = x2d.shape
    tm = min(INPROJ_TILE, n)
    assert n % tm == 0
    conv_seq = seq if tm % seq == 0 else None
    row = lambda i: (i, 0)
    const = lambda i: (0, 0)
    in_specs = [pl.BlockSpec((tm, d), row)]
    args = [x2d]
    if pos is not None:
        tiles = pos.shape[0] // tm
        in_specs.append(pl.BlockSpec((tm, d), lambda i: (i % tiles, 0)))
        args.append(pos)
    in_specs.append(_mod_spec(lambda i: mod_row(i * tm)))
    args.append(mod4)
    in_specs.append(pl.BlockSpec(w.shape, const, pipeline_mode=pl.Buffered(1)))
    args.append(w)
    if conv_seq:
        for a in (conv_a_w, conv_b_w, conv_b_b.reshape(1, -1)):
            in_specs.append(pl.BlockSpec(a.shape, const))
            args.append(a)
    hy_width = (2 if conv_seq else 3) * HY_WIDTH
    widths = (QKV_WIDTH, DN_WIDTH, LANES, hy_width, 2 * D_MODEL)
    dts = (act_dt, act_dt, F32, act_dt, act_dt)
    outs = pl.pallas_call(
        functools.partial(_inproj_kernel, pos is not None, conv_seq),
        grid=(n // tm,),
        in_specs=in_specs,
        out_specs=[pl.BlockSpec((tm, w), row) for w in widths],
        out_shape=[jax.ShapeDtypeStruct((n, w), dt) for w, dt in zip(widths, dts)],
        compiler_params=_cparams(1),
        name="inproj",
    )(*args)
    return outs, conv_seq is not None


def _inv_unit_tri(lm, ri, ci, block):
    c = lm.shape[-1]

    def same_block(h):
        sh = h.bit_length() - 1
        return (ri >> sh) == (ci >> sh)

    def unfold(f, h):
        return jnp.where(same_block(h), jnp.concatenate([f] * (c // h), axis=1), 0.0)

    base = INV_BASE
    n_nat = jnp.where(same_block(base), -lm, 0.0)
    n_fold = n_nat[:, 0:base, :]
    for i in range(1, c // base):
        n_fold = n_fold + n_nat[:, i * base:(i + 1) * base, :]
    eye_fold = jnp.where((_iota((base, c), 1) & (base - 1)) == _iota((base, c), 0), 1.0, 0.0)
    t = eye_fold + n_fold
    p_fold, p_nat = n_fold, n_nat
    span = 2
    while span < base:
        p_fold = _bmm(p_fold, p_nat)
        p_nat = unfold(p_fold, base)
        t = t + _bmm(t, p_nat)
        span *= 2
    s = base
    while s < block:
        sh = s.bit_length() - 1
        t_nat = unfold(t, s)
        even = ((_iota((s, c), 1) >> sh) & 1) == 0
        t2 = jnp.concatenate([jnp.where(even, t, 0.0), jnp.where(even, 0.0, t)], axis=1)
        lo = jnp.where(same_block(2 * s) & ~same_block(s), lm, 0.0)
        t = t2 - _bmm(_bmm(t2, lo), t_nat)
        s *= 2
    return t


def _dn_chunk(q, k, v, gc_col, gc_row, beta, e_col, e_rest, e_all, s_prev, same_chunk, ri, ci):
    c = DN_CHUNK
    nh = q.shape[0] // 2
    half = c // 2
    if same_chunk:
        gram = _bmm_nt(k[:nh], k[:nh])
        qk = _bmm_nt(q[:nh], k[:nh])
        gram = jnp.concatenate([gram, gram], 0)
        qk = jnp.concatenate([qk, qk], 0)
    else:
        gram = _bmm_nt(k, k)
        qk = _bmm_nt(q, k)
    diff = gc_col - gc_row
    incl = jnp.concatenate([jnp.broadcast_to(ri >= ci, (nh, c, c)), jnp.broadcast_to(ri <= ci, (nh, c, c))], 0)
    strict = incl & (ri != ci)
    decay = jnp.exp(jnp.where(incl, diff, -jnp.inf))
    lm = jnp.where(strict, beta * gram * decay, 0.0)
    a_intra = qk * decay
    t = _inv_unit_tri(lm, ri, ci, half)
    t1 = t[:, :, :half]
    t2 = t[:, :, half:]
    rhs = jnp.concatenate([v * beta, k * (beta * e_col)], axis=2)
    y1 = _bmm(t1, rhs[:, :half, :])
    y2 = _bmm(t2, rhs[:, half:, :])
    c_f = _bmm(lm[:nh, half:, :half], y1[:nh])
    c_b = _bmm(lm[nh:, :half, half:], y2[nh:])
    c_f = _bmm(t2[:nh], c_f)
    c_b = _bmm(t1[nh:], c_b)
    sol = jnp.concatenate([jnp.concatenate([y1[:nh], y2[:nh] - c_f], 1),
                           jnp.concatenate([y1[nh:] - c_b, y2[nh:]], 1)], 0)
    u = sol[:, :, :DN_DV]
    w = sol[:, :, DN_DV:]
    ks = k * e_rest
    if s_prev is None:
        v_new = u
        o = _bmm(a_intra, v_new)
        s_new = _bmm_tn(ks, v_new)
    else:
        v_new = u - _bmm(w, s_prev)
        o = _bmm(q * e_col, s_prev) + _bmm(a_intra, v_new)
        s_new = s_prev * e_all + _bmm_tn(ks, v_new)
    return o, s_new


def _deltanet_kernel(seq, reqs, zero_init, pre_conv, *refs):
    qkv_ref, cw_ref, z_ref, ba_ref, prm_ref, gn_ref = refs[:6]
    refs = refs[6:]
    if not zero_init:
        s0f_ref, s0b_ref = refs[:2]
        refs = refs[2:]
    o_ref, sf_ref, sb_ref, q_s, k_s, v_s, g_s, b_s, o_s, st_s = refs
    c = DN_CHUNK
    n = seq // c
    nh = DN_HEADS
    nf = reqs * nh
    lanes = lambda j: slice(j * LANES, (j + 1) * LANES)

    if not pre_conv:
        row = _iota((seq, LANES), 0)
        first_row = row == 0
        last_row = row == seq - 1
    for r in range(reqs):
        rows = slice(r * seq, (r + 1) * seq)

        def qkv_part(j):
            x = qkv_ref[rows, lanes(j)].astype(F32)
            return x if pre_conv else _silu(_short_conv3(x, cw_ref[:, lanes(j)], first_row, last_row))

        for h in range(nh):
            q = qkv_part(h)
            k = qkv_part(nh + h)
            q_s[r * nh + h] = q if pre_conv else _l2norm(q, DN_DK ** -0.5)
            k_s[r * nh + h] = k if pre_conv else _l2norm(k)
            v_s[r * nh + h] = qkv_part(2 * nh + h)

    ba = ba_ref[...]
    b_s[...] = _sigmoid(ba)
    g_s[...] = -jnp.exp(prm_ref[0:1, :]) * _softplus(ba + prm_ref[1:2, :])

    ri = _iota((c, c), 0)
    ci = _iota((c, c), 1)
    tri_l = jnp.where(ri >= ci, 1.0, 0.0).astype(F32)
    tri_u = jnp.where(ri <= ci, 1.0, 0.0).astype(F32)

    def chunk_pair(idx_f, idx_b, s_prev):
        def ds(idx, base=0):
            start = idx * c
            start = start if isinstance(start, int) else pl.multiple_of(start, c)
            return pl.ds(base + start, c)

        col_f, col_b, row_f, row_b, beta_f, beta_b = [], [], [], [], [], []
        for r in range(reqs):
            cs_f = _mm_f32(tri_l, g_s[ds(idx_f, r * seq), :])
            cs_b = _mm_f32(tri_u, g_s[ds(idx_b, r * seq), :])
            cst_f = cs_f.T
            cst_b = cs_b.T
            b_f = b_s[ds(idx_f, r * seq), :]
            b_b = b_s[ds(idx_b, r * seq), :]
            for h in range(nh):
                jf, jb = 2 * nh + h, 3 * nh + h
                col_f.append(cs_f[:, jf:jf + 1])
                col_b.append(cs_b[:, jb:jb + 1])
                row_f.append(cst_f[jf:jf + 1, :])
                row_b.append(cst_b[jb:jb + 1, :])
                beta_f.append(b_f[:, h:h + 1])
                beta_b.append(b_b[:, nh + h:nh + h + 1])
        gc_col = jnp.stack(col_f + col_b, 0)
        gc_row = jnp.stack(row_f + row_b, 0)
        beta = jnp.stack(beta_f + beta_b, 0)
        e_col = jnp.exp(gc_col)
        gl = jnp.concatenate([gc_col[:nf, c - 1:c, :], gc_col[nf:, 0:1, :]], 0)
        e_rest = jnp.exp(gl - gc_col)
        e_all = jnp.exp(gl)
        sl_f, sl_b = ds(idx_f), ds(idx_b)
        both = lambda ref: jnp.concatenate([ref[:, sl_f, :], ref[:, sl_b, :]], 0)
        o, s_new = _dn_chunk(both(q_s), both(k_s), both(v_s), gc_col, gc_row, beta, e_col, e_rest, e_all,
                             s_prev, n == 1, ri, ci)
        o_s[:nf, sl_f, :] = o[:nf]
        o_s[nf:, sl_b, :] = o[nf:]
        return s_new

    if not zero_init:
        s0 = jnp.concatenate([s0f_ref[r] for r in range(reqs)] + [s0b_ref[r] for r in range(reqs)], 0)
    if n == 1:
        s_fin = chunk_pair(0, 0, None if zero_init else s0)
    else:
        st_s[...] = jnp.zeros(st_s.shape, F32) if zero_init else s0

        def body(i, carry):
            st_s[...] = chunk_pair(i, n - 1 - i, st_s[...])
            return carry

        lax.fori_loop(0, n, body, 0)
        s_fin = st_s[...]
    for r in range(reqs):
        sf_ref[r] = s_fin[r * nh:(r + 1) * nh]
        sb_ref[r] = s_fin[nf + r * nh:nf + (r + 1) * nh]

    gn = gn_ref[...]
    for r in range(reqs):
        rows = slice(r * seq, (r + 1) * seq)
        for h in range(nh):
            o = o_s[r * nh + h] + o_s[nf + r * nh + h]
            o = o * lax.rsqrt(jnp.mean(o * o, axis=-1, keepdims=True) + RMS_EPS) * gn
            o_ref[rows, lanes(h)] = (o * _silu(z_ref[rows, lanes(h)].astype(F32))).astype(o_ref.dtype)


def _deltanet_call(qkv, z, ba, conv_w, prm, gn, s0f, s0b, bsz, seq, reqs, pre_conv, act_dt):
    nh = DN_HEADS
    zero_init = s0f is None
    rows = reqs * seq
    st = pl.BlockSpec((reqs, nh, DN_DK, DN_DV), lambda b: (b, 0, 0, 0))
    in_specs = [pl.BlockSpec((rows, QKV_WIDTH), lambda b: (b, 0)),
                pl.BlockSpec((3, QKV_WIDTH), lambda b: (0, 0)),
                pl.BlockSpec((rows, DN_WIDTH), lambda b: (b, 0)),
                pl.BlockSpec((rows, LANES), lambda b: (b, 0)),
                pl.BlockSpec((8, LANES), lambda b: (0, 0)),
                pl.BlockSpec((1, LANES), lambda b: (0, 0))]
    args = [qkv, conv_w, z, ba, prm, gn]
    if not zero_init:
        in_specs += [st, st]
        args += [s0f, s0b]
    vm = lambda shape: pltpu.VMEM(shape, F32)
    per_head = vm((reqs * nh, seq, LANES))
    return pl.pallas_call(
        functools.partial(_deltanet_kernel, seq, reqs, zero_init, pre_conv),
        grid=(bsz // reqs,),
        in_specs=in_specs,
        out_specs=[pl.BlockSpec((rows, DN_WIDTH), lambda b: (b, 0)), st, st],
        out_shape=[jax.ShapeDtypeStruct((bsz * seq, DN_WIDTH), act_dt),
                   jax.ShapeDtypeStruct((bsz, nh, DN_DK, DN_DV), F32),
                   jax.ShapeDtypeStruct((bsz, nh, DN_DK, DN_DV), F32)],
        scratch_shapes=[per_head, per_head, per_head, vm((rows, LANES)), vm((rows, LANES)),
                        vm((2 * reqs * nh, seq, LANES)), vm((2 * reqs * nh, DN_DK, DN_DV))],
        compiler_params=_cparams(1),
        name="deltanet",
    )(*args)


def _dft_mats(seq):
    n = 2 * seq
    f = np.arange(seq)[:, None]
    s = np.arange(seq)[None, :]
    ang = 2.0 * np.pi * ((f * s) % n) / n
    fwd = np.concatenate([np.cos(ang), -np.sin(ang)], axis=0)
    fwd[seq, :] = np.cos(np.pi * np.arange(seq))
    t = (np.arange(seq) + seq // 2)[:, None]
    ff = np.arange(seq)[None, :]
    ang2 = 2.0 * np.pi * ((t * ff) % n) / n
    inv_r = 2.0 * np.cos(ang2) / n
    inv_i = -2.0 * np.sin(ang2) / n
    inv_r[:, 0] = 1.0 / n
    inv_i[:, 0] = np.cos(np.pi * t[:, 0]) / n
    inv = np.concatenate([inv_r, inv_i], axis=1)
    return fwd.astype(np.float32), inv.astype(np.float32)


def _filter_feats(seq):
    t = np.linspace(0.0, 1.0, seq)[:, None]
    bands = (HY_EMB - 1) // 2
    ang = (2.0 * math.pi * np.arange(seq) / seq)[:, None] * np.linspace(1e-4, bands - 1, bands)[None, :]
    feats = np.concatenate([t, np.cos(ang), -np.sin(ang)], -1)
    deltas = np.abs(np.linspace(math.log(HY_TARGET) / HY_SLOW_DECAY, math.log(HY_TARGET) / HY_FAST_DECAY,
                                HY_WIDTH))
    offset = np.abs(np.arange(seq) - seq // 2) / (seq // 2)
    window = np.exp(-offset[:, None] * deltas[None, :])
    feats = np.pad(feats, ((0, 0), (0, LANES - HY_EMB)))
    return jnp.asarray(feats, dtype=F32), jnp.asarray(window, dtype=F32)


def _hfilter_kernel(seq, feats_ref, win_ref, w1_ref, b1_ref, w2_ref, b2_ref, w3_ref, fr_ref, fwd_hi_ref,
                    fwd_lo_ref, ha_ref, hb_ref, hd_ref):
    fr = fr_ref[...]
    hid = jnp.sin(fr * (_mm_f32(feats_ref[...], w1_ref[...]) + b1_ref[...]))
    hid = jnp.sin(fr * (_mm_f32(hid, w2_ref[...]) + b2_ref[...]))
    filt = _mm_f32(hid, w3_ref[...]) * win_ref[...]
    filt = filt / (jnp.sum(jnp.abs(filt), axis=0, keepdims=True) + 1e-6)
    filt_hi = filt.astype(BF16)
    filt_lo = (filt - filt_hi.astype(F32)).astype(BF16)
    fwd_hi = fwd_hi_ref[...]
    spec = (jnp.dot(fwd_hi, filt_hi, preferred_element_type=F32)
            + jnp.dot(fwd_hi, filt_lo, preferred_element_type=F32)
            + jnp.dot(fwd_lo_ref[...], filt_hi, preferred_element_type=F32))
    h_re = spec[:seq, :]
    h_im = spec[seq:, :]
    first = _iota(h_re.shape, 0) == 0
    ha_ref[...] = h_re
    hb_ref[...] = jnp.where(first, 0.0, h_im)
    hd_ref[...] = jnp.where(first, h_im, h_re)


def _hfilter_call(seq, w1, b1, w2, b2, w3, freq, fwd_hi, fwd_lo):
    feats, window = _filter_feats(seq)
    w1p = jnp.pad(w1, ((0, LANES - HY_EMB), (0, 0)))
    full = lambda a: pl.BlockSpec(a.shape, lambda i: (0,) * a.ndim)
    args = [feats, window, w1p, b1.reshape(1, -1), w2, b2.reshape(1, -1), w3, freq.reshape(1, -1), fwd_hi,
            fwd_lo]
    out = jax.ShapeDtypeStruct((seq, HY_WIDTH), F32)
    return pl.pallas_call(
        functools.partial(_hfilter_kernel, seq),
        grid=(1,),
        in_specs=[full(a) for a in args],
        out_specs=[pl.BlockSpec((seq, HY_WIDTH), lambda i: (0, 0))] * 3,
        out_shape=[out, out, out],
        compiler_params=_cparams(1),
        name="hfilter",
    )(*args)


def _hyena_kernel(seq, pre_conv, *refs):
    if pre_conv:
        x0_ref, uu_ref, ha_ref, hb_ref, hd_ref, skip_ref, fwd_ref, inv_ref, o_ref = refs
        x0 = x0_ref[...].astype(F32)
        uu = uu_ref[...].astype(F32)
    else:
        (x0_ref, x1_ref, v_ref, c0_ref, c1_ref, c2_ref, b0_ref, b1_ref, b2_ref,
         ha_ref, hb_ref, hd_ref, skip_ref, fwd_ref, inv_ref, o_ref) = refs
        row = _iota((seq, HY_CH_BLOCK), 0)
        first = row == 0
        last = row == seq - 1
        conv = lambda x_ref, w_ref, b_ref: _short_conv3(x_ref[...].astype(F32), w_ref[...], first, last) + b_ref[...]
        x0 = conv(x0_ref, c0_ref, b0_ref)
        uu = conv(x1_ref, c1_ref, b1_ref) * conv(v_ref, c2_ref, b2_ref)
    spec = jnp.dot(fwd_ref[...], uu.astype(BF16), preferred_element_type=F32)
    u_re = spec[:seq, :]
    u_im = spec[seq:, :]
    hb = hb_ref[...]
    y_re = u_re * ha_ref[...] - u_im * hb
    y_im = u_re * hb + u_im * hd_ref[...]
    y = jnp.concatenate([y_re, y_im], axis=0).astype(BF16)
    cv = jnp.dot(inv_ref[...], y, preferred_element_type=F32)
    o_ref[...] = (x0 * (cv + uu * skip_ref[...])).astype(o_ref.dtype)


def _hyena_call(hy, pre_conv, conv_w, conv_b, ha, hb, hd, skip, fwd, inv, bsz, seq, act_dt):
    cb = HY_CH_BLOCK
    nblk = HY_WIDTH // cb
    tok = lambda off: pl.BlockSpec((seq, cb), lambda b, j: (b, off + j))
    cw = lambda off: pl.BlockSpec((3, cb), lambda b, j: (0, off + j))
    bias = lambda off: pl.BlockSpec((1, cb), lambda b, j: (0, off + j))
    ch = pl.BlockSpec((seq, cb), lambda b, j: (0, j))
    const = lambda a: pl.BlockSpec(a.shape, lambda b, j: (0, 0), pipeline_mode=pl.Buffered(1))
    tail_specs = [ch, ch, ch, pl.BlockSpec((1, cb), lambda b, j: (0, j)), const(fwd), const(inv)]
    tail_args = [ha, hb, hd, skip.reshape(1, -1), fwd, inv]
    if pre_conv:
        in_specs = [tok(0), tok(nblk)] + tail_specs
        args = [hy, hy] + tail_args
    else:
        conv_b2 = conv_b.reshape(1, -1)
        in_specs = [tok(0), tok(nblk), tok(2 * nblk), cw(0), cw(nblk), cw(2 * nblk),
                    bias(0), bias(nblk), bias(2 * nblk)] + tail_specs
        args = [hy, hy, hy, conv_w, conv_w, conv_w, conv_b2, conv_b2, conv_b2] + tail_args
    return pl.pallas_call(
        functools.partial(_hyena_kernel, seq, pre_conv),
        grid=(bsz, nblk),
        in_specs=in_specs,
        out_specs=tok(0),
        out_shape=jax.ShapeDtypeStruct((bsz * seq, HY_WIDTH), act_dt),
        compiler_params=_cparams(2),
        name="hyena",
    )(*args)


def _mix_kernel(has_pos, *refs):
    refs = list(refs)
    x_ref = refs.pop(0)
    pos_ref = refs.pop(0) if has_pos else None
    (oa_ref, ob_ref, g_ref, mod_ref, wua, wub, wout, l1g, l1b, wr_hi, wr_lo, x1_o, u2_o, p_o) = refs
    d = D_MODEL
    tm = x_ref.shape[0]
    sub = min(tm, MIX_SUB_ROWS)
    blocks = [slice(r0, r0 + sub) for r0 in range(0, tm, sub)]
    g1 = mod_ref[0, 2]
    sh2 = mod_ref[0, 3]
    sc2 = mod_ref[0, 4]

    mixed = []
    for rows in blocks:
        gates = g_ref[rows, :].astype(F32)
        ga = _sigmoid(gates[:, :d])
        gb = _sigmoid(gates[:, d:])
        up = (ga * jnp.dot(oa_ref[rows, :], wua[...], preferred_element_type=F32)
              + gb * jnp.dot(ob_ref[rows, :], wub[...], preferred_element_type=F32))
        mixed.append(jnp.dot(up.astype(BF16), wout[...], preferred_element_type=F32))

    for rows, mx in zip(blocks, mixed):
        x = x_ref[rows, :]
        if has_pos:
            x = x + pos_ref[rows, :]
        x1 = _ln(ALPHA * x + g1 * mx) * l1g[...] + l1b[...]
        x1_o[rows, :] = x1.astype(x1_o.dtype)
        u2 = _ln(x1) * (1.0 + sc2) + sh2
        u2_hi = u2.astype(BF16)
        u2_o[rows, :] = u2_hi
        u2_lo = (u2 - u2_hi.astype(F32)).astype(BF16)
        logits = (jnp.dot(u2_hi, wr_hi[...], preferred_element_type=F32)
                  + jnp.dot(u2_hi, wr_lo[...], preferred_element_type=F32)
                  + jnp.dot(u2_lo, wr_hi[...], preferred_element_type=F32))
        lane = _iota(logits.shape, 1)
        logits = jnp.where(lane < N_EXPERTS, logits, -jnp.inf)
        m = jnp.max(logits, axis=-1, keepdims=True)
        e = jnp.exp(logits - m)
        p_o[rows, :] = e / jnp.sum(e, axis=-1, keepdims=True)


def _mix_call(x2d, pos, o_a, o_b, gates, mod4, mod_row, wua, wub, wout, l1g, l1b, wr_hi, wr_lo):
    n, d = x2d.shape
    tm = min(MIX_TILE, max(n // MIX_MIN_STEPS, MIX_SUB_ROWS))
    assert n % tm == 0
    row = lambda i: (i, 0)
    const = lambda i: (0, 0)
    in_specs = [pl.BlockSpec((tm, d), row)]
    args = [x2d]
    if pos is not None:
        tiles = pos.shape[0] // tm
        in_specs.append(pl.BlockSpec((tm, d), lambda i: (i % tiles, 0)))
        args.append(pos)
    in_specs += [pl.BlockSpec((tm, DN_WIDTH), row), pl.BlockSpec((tm, HY_WIDTH), row),
                 pl.BlockSpec((tm, 2 * d), row),
                 _mod_spec(lambda i: mod_row(i * tm))]
    args += [o_a, o_b, gates, mod4]
    for w in (wua, wub, wout, l1g, l1b, wr_hi, wr_lo):
        in_specs.append(pl.BlockSpec(w.shape, const, pipeline_mode=pl.Buffered(1)))
        args.append(w)
    return pl.pallas_call(
        functools.partial(_mix_kernel, pos is not None),
        grid=(n // tm,),
        in_specs=in_specs,
        out_specs=[pl.BlockSpec((tm, d), row), pl.BlockSpec((tm, d), row), pl.BlockSpec((tm, LANES), row)],
        out_shape=[jax.ShapeDtypeStruct((n, d), ACT_DT), jax.ShapeDtypeStruct((n, d), BF16),
                   jax.ShapeDtypeStruct((n, LANES), F32)],
        compiler_params=_cparams(1),
        name="mix",
    )(*args)


def _route_kernel(seq, cap, p_ref, u_ref, tri_ref, xs_ref, slot_ref, pt_ref, g_s):
    e_n = N_EXPERTS
    p = p_ref[...]
    pt = p.T[:e_n, :]
    tri = tri_ref[...]

    if seq <= RANK_COUNT_MAX_SEQ:
        earlier = _iota((seq, seq), 0) < _iota((seq, seq), 1)
        ranks = []
        for e in range(e_n):
            pc = p[:, e:e + 1]
            pr = pt[e:e + 1, :]
            beats = (pc > pr) | (earlier & (pc == pr))
            ranks.append(jnp.sum(jnp.where(beats, 1.0, 0.0), axis=0, keepdims=True))
        sel = jnp.concatenate(ranks, axis=0) < cap
    else:
        def search(i, cur):
            cand = cur | (1 << (30 - i))
            cand_f = pltpu.bitcast(cand, F32)
            cnt = jnp.sum(jnp.where(pt >= cand_f, 1.0, 0.0), axis=1, keepdims=True)
            return jnp.where((cnt >= cap) & (cand >= MIN_NORMAL_F32_BITS), cand, cur)

        thr = pltpu.bitcast(lax.fori_loop(0, 31, search, jnp.zeros((e_n, 1), jnp.int32)), F32)
        gt = pt > thr
        eq = pt == thr
        n_gt = jnp.sum(jnp.where(gt, 1.0, 0.0), axis=1, keepdims=True)
        eq_rank = jnp.dot(jnp.where(eq, 1.0, 0.0).astype(BF16), tri, preferred_element_type=F32)
        sel = gt | (eq & (eq_rank < cap - n_gt))
    pos = jnp.dot(jnp.where(sel, 1.0, 0.0).astype(BF16), tri, preferred_element_type=F32)
    slot = jnp.where(sel, pos, -1.0)
    slot_ref[0] = slot
    pt_ref[0] = pt
    jrow = _iota((cap, seq), 0).astype(F32)
    for e in range(e_n):
        g_s[e * cap:(e + 1) * cap, :] = jnp.where(slot[e:e + 1, :] == jrow, 1.0, 0.0).astype(BF16)
    xs = jnp.dot(g_s[...], u_ref[...], preferred_element_type=F32)
    for e in range(e_n):
        xs_ref[e] = xs[e * cap:(e + 1) * cap, :].astype(xs_ref.dtype)


def _route_call(probs, u2, bsz, seq):
    cap = EC_FACTOR * seq // N_EXPERTS
    tri = jnp.asarray(np.triu(np.ones((seq, seq), np.float32), 1), dtype=BF16)
    return pl.pallas_call(
        functools.partial(_route_kernel, seq, cap),
        grid=(bsz,),
        in_specs=[pl.BlockSpec((seq, LANES), lambda b: (b, 0)),
                  pl.BlockSpec((seq, D_MODEL), lambda b: (b, 0)),
                  pl.BlockSpec((seq, seq), lambda b: (0, 0), pipeline_mode=pl.Buffered(1))],
        out_specs=[pl.BlockSpec((N_EXPERTS, cap, D_MODEL), lambda b: (0, b, 0)),
                   pl.BlockSpec((1, N_EXPERTS, seq), lambda b: (b, 0, 0)),
                   pl.BlockSpec((1, N_EXPERTS, seq), lambda b: (b, 0, 0))],
        out_shape=[jax.ShapeDtypeStruct((N_EXPERTS, bsz * cap, D_MODEL), BF16),
                   jax.ShapeDtypeStruct((bsz, N_EXPERTS, seq), F32),
                   jax.ShapeDtypeStruct((bsz, N_EXPERTS, seq), F32)],
        scratch_shapes=[pltpu.VMEM((N_EXPERTS * cap, seq), BF16)],
        compiler_params=_cparams(1),
        name="route",
    )(probs, u2, tri)


def _expert_kernel(xc_ref, xd_ref, wg_ref, wu_ref, wd_ref, yc_ref, yd_ref):
    wg = wg_ref[0].astype(BF16)
    wu = wu_ref[0].astype(BF16)
    wd = wd_ref[0].astype(BF16)
    for x_ref, y_ref in ((xc_ref, yc_ref), (xd_ref, yd_ref)):
        rows = x_ref.shape[1]
        rb = min(rows, EXPERT_ROW_BLOCK)
        for r0 in range(0, rows, rb):
            x = x_ref[0, r0:r0 + rb, :]
            hg = jnp.dot(x, wg, preferred_element_type=F32)
            hu = jnp.dot(x, wu, preferred_element_type=F32)
            y = jnp.dot((_silu(hg) * hu).astype(BF16), wd, preferred_element_type=F32)
            y_ref[0, r0:r0 + rb, :] = y.astype(y_ref.dtype)


def _expert_call(xs_c, xs_d, w_gate, w_up, w_down):
    e_n, rc, d = xs_c.shape
    rd = xs_d.shape[1]
    ff = w_gate.shape[2]
    return pl.pallas_call(
        _expert_kernel,
        grid=(e_n,),
        in_specs=[pl.BlockSpec((1, rc, d), lambda e: (e, 0, 0)),
                  pl.BlockSpec((1, rd, d), lambda e: (e, 0, 0)),
                  pl.BlockSpec((1, d, ff), lambda e: (e, 0, 0)),
                  pl.BlockSpec((1, d, ff), lambda e: (e, 0, 0)),
                  pl.BlockSpec((1, ff, d), lambda e: (e, 0, 0))],
        out_specs=[pl.BlockSpec((1, rc, d), lambda e: (e, 0, 0)),
                   pl.BlockSpec((1, rd, d), lambda e: (e, 0, 0))],
        out_shape=[jax.ShapeDtypeStruct((e_n, rc, d), BF16), jax.ShapeDtypeStruct((e_n, rd, d), BF16)],
        compiler_params=_cparams(1),
        name="experts",
    )(xs_c, xs_d, w_gate, w_up, w_down)


def _final_kernel(seq, cap, x1_ref, y_ref, slot_ref, pt_ref, mod_ref, l2g, l2b, o_ref, whi_s, wlo_s):
    slot = slot_ref[0]
    pt = pt_ref[0]
    jrow = _iota((cap, seq), 0).astype(F32)
    for e in range(N_EXPERTS):
        w = jnp.where(slot[e:e + 1, :] == jrow, pt[e:e + 1, :], 0.0)
        w_hi = w.astype(BF16)
        whi_s[e * cap:(e + 1) * cap, :] = w_hi
        wlo_s[e * cap:(e + 1) * cap, :] = (w - w_hi.astype(F32)).astype(BF16)
    y = y_ref[...].reshape(N_EXPERTS * cap, y_ref.shape[-1])
    tn = (((0,), (0,)), ((), ()))
    ffn = (lax.dot_general(whi_s[...], y, tn, preferred_element_type=F32)
           + lax.dot_general(wlo_s[...], y, tn, preferred_element_type=F32))
    g2 = mod_ref[0, 5]
    o_ref[...] = _ln(ALPHA * x1_ref[...].astype(F32) + g2 * ffn) * l2g[...] + l2b[...]


def _final_call(x1, ys, slot, pt, mod4, mod_row, l2g, l2b, bsz, seq):
    cap = EC_FACTOR * seq // N_EXPERTS
    d = D_MODEL
    rows = pl.BlockSpec((1, N_EXPERTS, seq), lambda b: (b, 0, 0))
    wbuf = pltpu.VMEM((N_EXPERTS * cap, seq), BF16)
    return pl.pallas_call(
        functools.partial(_final_kernel, seq, cap),
        grid=(bsz,),
        in_specs=[pl.BlockSpec((seq, d), lambda b: (b, 0)),
                  pl.BlockSpec((N_EXPERTS, cap, d), lambda b: (0, b, 0)),
                  rows, rows,
                  _mod_spec(lambda b: mod_row(b * seq)),
                  pl.BlockSpec((1, d), lambda b: (0, 0)),
                  pl.BlockSpec((1, d), lambda b: (0, 0))],
        out_specs=pl.BlockSpec((seq, d), lambda b: (b, 0)),
        out_shape=jax.ShapeDtypeStruct((bsz * seq, d), F32),
        scratch_shapes=[wbuf, wbuf],
        compiler_params=_cparams(1),
        name="final",
    )(x1, ys, slot, pt, mod4, l2g, l2b)


def _grid_pos_embed(rows, dim):
    r = np.repeat(np.arange(rows), GRID_W)
    col = np.tile(np.arange(GRID_W), rows)
    quarter = dim // 4
    omega = 1.0 / (10000.0 ** (np.arange(quarter) / quarter))

    def enc(p):
        ang = p[:, None] * omega[None, :]
        return np.concatenate([np.sin(ang), np.cos(ang)], -1)

    return jnp.asarray(np.concatenate([enc(r), enc(col)], -1), dtype=F32)


def kernel(x_prompt, x_sample, state_delta_fwd, state_delta_bwd, c, c_ctx, w_mod, b_mod, w_in, conv_a_w, a_log, dt_bias, gnorm_w, conv_b_w, conv_b_b, hy_w1, hy_b1, hy_w2, hy_b2, hy_w3, hy_freq, hy_bias, w_up_a, w_up_b, w_out, ln1_g, ln1_b, w_router, w_e_gate, w_e_up, w_e_down, ln2_g, ln2_b):
    d = D_MODEL
    n_ctx, l_ctx, _ = x_prompt.shape
    n_dec, l_dec, _ = x_sample.shape
    lyr = 0

    cond_t = jnp.pad(jnp.concatenate([c_ctx[None, :], c], 0).T, ((0, 0), (0, 7 - n_dec)))
    mod4 = _mod_call(cond_t, 1 + n_dec, w_mod[lyr], b_mod[lyr]).reshape(8, N_MOD, 1, d)

    in_w = w_in[lyr].astype(BF16)

    prm = jnp.pad(jnp.stack([a_log[lyr].reshape(-1), dt_bias[lyr].reshape(-1)]),
                  ((0, 6), (2 * DN_HEADS, LANES - 4 * DN_HEADS)))
    gn = gnorm_w[lyr].reshape(1, -1)

    wua = w_up_a[lyr].astype(BF16)
    wub = w_up_b[lyr].astype(BF16)
    wout = w_out[lyr].astype(BF16)
    wr = jnp.pad(w_router[lyr], ((0, 0), (0, LANES - N_EXPERTS)))
    wr_hi = wr.astype(BF16)
    wr_lo = (wr - wr_hi.astype(F32)).astype(BF16)
    l1g, l1b = ln1_g[lyr].reshape(1, -1), ln1_b[lyr].reshape(1, -1)
    l2g, l2b = ln2_g[lyr].reshape(1, -1), ln2_b[lyr].reshape(1, -1)

    def front(x2d, pos, bsz, seq, mod_row, s0f, s0b):
        (qkv, z, ba, hy, gates), pre_conv = _inproj_call(x2d, pos, mod4, mod_row, in_w, conv_a_w[lyr],
                                                          conv_b_w[lyr], conv_b_b[lyr], seq, ACT_DT)
        reqs = DN_REQS_PER_STEP if seq == DN_CHUNK and bsz % DN_REQS_PER_STEP == 0 else 1
        o_a, s_f, s_b = _deltanet_call(qkv, z, ba, conv_a_w[lyr], prm, gn, s0f, s0b, bsz, seq, reqs,
                                       pre_conv, ACT_DT)
        fwd, inv = (jnp.asarray(m) for m in _dft_mats(seq))
        fwd_hi = fwd.astype(BF16)
        fwd_lo = (fwd - fwd_hi.astype(F32)).astype(BF16)
        ha, hb, hd = _hfilter_call(seq, hy_w1[lyr], hy_b1[lyr], hy_w2[lyr], hy_b2[lyr], hy_w3[lyr],
                                   hy_freq[lyr], fwd_hi, fwd_lo)
        o_h = _hyena_call(hy, pre_conv, conv_b_w[lyr], conv_b_b[lyr], ha, hb, hd, hy_bias[lyr],
                          fwd_hi, inv.astype(BF16), bsz, seq, ACT_DT)
        x1, u2, probs = _mix_call(x2d, pos, o_a, o_h, gates, mod4, mod_row, wua, wub, wout, l1g, l1b,
                                   wr_hi, wr_lo)
        xs, slot, pt = _route_call(probs, u2, bsz, seq)
        return x1, xs, slot, pt, s_f, s_b

    row_ctx = lambda tok: 0
    row_dec = lambda tok: 1 + tok // l_dec

    xc = x_prompt.reshape(n_ctx * l_ctx, d)
    xd = x_sample.reshape(n_dec * l_dec, d)
    pos = _grid_pos_embed(l_dec // GRID_W, d)

    x1c, xsc, gc, ptc, s_f, s_b = front(xc, None, n_ctx, l_ctx, row_ctx, None, None)
    x1d, xsd, gd, ptd, _, _ = front(xd, pos, n_dec, l_dec, row_dec,
                                    state_delta_fwd[:, lyr], state_delta_bwd[:, lyr])
    yc, yd = _expert_call(xsc, xsd, w_e_gate[lyr], w_e_up[lyr], w_e_down[lyr])
    y_prompt = _final_call(x1c, yc, gc, ptc, mod4, row_ctx, l2g, l2b, n_ctx, l_ctx)
    y_sample = _final_call(x1d, yd, gd, ptd, mod4, row_dec, l2g, l2b, n_dec, l_dec)

    return (y_prompt.reshape(n_ctx, l_ctx, d), y_sample.reshape(n_dec, l_dec, d),
            s_f[:, None], s_b[:, None])
```

```python
import functools
import math

import jax
import jax.numpy as jnp
import numpy as np
from jax import lax
from jax.experimental import pallas as pl
from jax.experimental.pallas import tpu as pltpu

F32 = jnp.float32
BF16 = jnp.bfloat16
HIGHEST = lax.Precision.HIGHEST

D_MODEL = 1024
DEPTH = 1
GRID_W = 64
DN_HEADS = 4
DN_DK = 128
DN_DV = 128
DN_WIDTH = DN_HEADS * DN_DV
HY_WIDTH = D_MODEL // 2
HY_EMB = 33
HY_FFN = 64
HY_TARGET = 1e-2
HY_FAST_DECAY = 0.3
HY_SLOW_DECAY = 1.5
N_EXPERTS = 16
EC_FACTOR = 2
N_MOD = 6
ALPHA = (2 * DEPTH) ** 0.25
LN_EPS = 1e-5
RMS_EPS = 1e-6
QKV_WIDTH = 2 * DN_HEADS * DN_DK + DN_WIDTH

LANES = 128
INPROJ_TILE = 512
MIX_TILE = 1024
MIX_SUB_ROWS = 256
MIX_MIN_STEPS = 4
MOD_COL_BLOCK = 1024
DN_CHUNK = 256
INV_BASE = 16
DN_REQS_PER_STEP = 2
HY_CH_BLOCK = 512
HFILTER_ROW_BLOCK = 256
FINAL_CTX_REQS_PER_STEP = 2
EXPERT_ROW_BLOCK = 512
VMEM_LIMIT = 56 * 1024 * 1024
MIN_NORMAL_F32_BITS = 0x00800000
ACT_DT = BF16
RANK_COUNT_MAX_SEQ = 256


def _cparams(n_axes):
    return pltpu.CompilerParams(dimension_semantics=("arbitrary",) * n_axes,
                                vmem_limit_bytes=VMEM_LIMIT)


def _bmm(a, b):
    return jnp.einsum("bmk,bkn->bmn", a.astype(BF16), b.astype(BF16), preferred_element_type=F32)


def _bmm_nt(a, b):
    return jnp.einsum("bmk,bnk->bmn", a.astype(BF16), b.astype(BF16), preferred_element_type=F32)


def _bmm_tn(a, b):
    return jnp.einsum("bkm,bkn->bmn", a.astype(BF16), b.astype(BF16), preferred_element_type=F32)


def _mm_f32(a, b):
    return jnp.dot(a, b, precision=HIGHEST, preferred_element_type=F32)


def _sigmoid(x):
    return 0.5 * jnp.tanh(0.5 * x) + 0.5


def _silu(x):
    return x * _sigmoid(x)


def _softplus(x):
    return jnp.maximum(x, 0.0) + jnp.log1p(jnp.exp(-jnp.abs(x)))


def _ln(x):
    mu = jnp.mean(x, axis=-1, keepdims=True)
    xc = x - mu
    var = jnp.mean(xc * xc, axis=-1, keepdims=True)
    return xc * lax.rsqrt(var + LN_EPS)


def _l2norm(x, scale=1.0):
    return x * (lax.rsqrt(jnp.sum(x * x, axis=-1, keepdims=True) + 1e-6) * scale)


def _iota(shape, dim):
    return lax.broadcasted_iota(jnp.int32, shape, dim)


def _short_conv3(x, w, first, last):
    xm = jnp.where(first, 0.0, pltpu.roll(x, 1, 0))
    xp = jnp.where(last, 0.0, pltpu.roll(x, x.shape[0] - 1, 0))
    return w[0:1, :] * xm + w[1:2, :] * x + w[2:3, :] * xp


def _mod_kernel(n_rows, ct_ref, w_ref, b_ref, o_ref):
    s = _silu(ct_ref[...])
    w = w_ref[...]
    b = b_ref[...]
    o_ref[...] = jnp.zeros(o_ref.shape, F32)
    for r in range(n_rows):
        o_ref[r:r + 1, :] = jnp.sum(w * s[:, r:r + 1], axis=0, keepdims=True) + b


def _mod_call(cond_t, n_rows, w_mod, b_mod):
    d = D_MODEL
    tn = MOD_COL_BLOCK
    return pl.pallas_call(
        functools.partial(_mod_kernel, n_rows),
        grid=(N_MOD * d // tn,),
        in_specs=[pl.BlockSpec((d, 8), lambda j: (0, 0)),
                  pl.BlockSpec((d, tn), lambda j: (0, j)),
                  pl.BlockSpec((1, tn), lambda j: (0, j))],
        out_specs=pl.BlockSpec((8, tn), lambda j: (0, j)),
        out_shape=jax.ShapeDtypeStruct((8, N_MOD * d), F32),
        compiler_params=_cparams(1),
        name="mod",
    )(cond_t, w_mod, b_mod.reshape(1, -1))


def _mod_spec(row_of_step):
    return pl.BlockSpec((1, N_MOD, 1, D_MODEL), lambda *idx: (row_of_step(*idx), 0, 0, 0))


def _inproj_kernel(has_pos, conv_seq, *refs):
    refs = list(refs)
    x_ref = refs.pop(0)
    pos_ref = refs.pop(0) if has_pos else None
    mod_ref, w_ref = refs.pop(0), refs.pop(0)
    if conv_seq:
        cwa_ref, cwb_ref, cbb_ref = refs.pop(0), refs.pop(0), refs.pop(0)
    qkv_o, z_o, ba_o, hy_o, g_o = refs
    sh1 = mod_ref[0, 0]
    sc1 = mod_ref[0, 1]
    o_z = QKV_WIDTH
    o_ba = o_z + DN_WIDTH
    n_ba = 4 * DN_HEADS
    o_g = n_ba + 3 * HY_WIDTH
    lanes = lambda j: slice(j * LANES, (j + 1) * LANES)
    tm = x_ref.shape[0]
    sub = conv_seq if conv_seq else tm
    blocks = [slice(r0, r0 + sub) for r0 in range(0, tm, sub)]

    proj = []
    for rows in blocks:
        x = x_ref[rows, :]
        if has_pos:
            x = x + pos_ref[rows, :]
        u = (_ln(x) * (1.0 + sc1) + sh1).astype(BF16)
        proj.append((jnp.dot(u, w_ref[:, :o_ba], preferred_element_type=F32),
                     jnp.dot(u, w_ref[:, o_ba:], preferred_element_type=F32)))

    if conv_seq:
        row = _iota((sub, LANES), 0)
        first = row == 0
        last = row == sub - 1
    for rows, (head, tail) in zip(blocks, proj):
        z_o[rows, :] = head[:, o_z:].astype(z_o.dtype)
        ba_o[rows, :] = tail[:, :LANES]
        g_o[rows, :] = tail[:, o_g:].astype(g_o.dtype)
        hy = tail[:, n_ba:o_g]
        if not conv_seq:
            qkv_o[rows, :] = head[:, :o_z].astype(qkv_o.dtype)
            hy_o[rows, :] = hy.astype(hy_o.dtype)
            continue
        for j in range(QKV_WIDTH // LANES):
            y = _silu(_short_conv3(head[:, lanes(j)], cwa_ref[:, lanes(j)], first, last))
            if j < 2 * DN_HEADS:
                y = _l2norm(y, DN_DK ** -0.5 if j < DN_HEADS else 1.0)
            qkv_o[rows, lanes(j)] = y.astype(qkv_o.dtype)
        nj = HY_WIDTH // LANES
        for j in range(nj):
            parts = [_short_conv3(hy[:, lanes(p * nj + j)], cwb_ref[:, lanes(p * nj + j)], first, last)
                     + cbb_ref[:, lanes(p * nj + j)] for p in range(3)]
            hy_o[rows, lanes(j)] = parts[0].astype(hy_o.dtype)
            hy_o[rows, lanes(nj + j)] = (parts[1] * parts[2]).astype(hy_o.dtype)


def _inproj_call(x2d, pos, mod4, mod_row, w, conv_a_w, conv_b_w, conv_b_b, seq, act_dt):
    n, d = x2d.shape
    tm = min(INPROJ_TILE, n)
    assert n % tm == 0
    conv_seq = seq if tm % seq == 0 else None
    row = lambda i: (i, 0)
    const = lambda i: (0, 0)
    in_specs = [pl.BlockSpec((tm, d), row)]
    args = [x2d]
    if pos is not None:
        tiles = pos.shape[0] // tm
        in_specs.append(pl.BlockSpec((tm, d), lambda i: (i % tiles, 0)))
        args.append(pos)
    in_specs.append(_mod_spec(lambda i: mod_row(i * tm)))
    args.append(mod4)
    in_specs.append(pl.BlockSpec(w.shape, const, pipeline_mode=pl.Buffered(1)))
    args.append(w)
    if conv_seq:
        for a in (conv_a_w, conv_b_w, conv_b_b.reshape(1, -1)):
            in_specs.append(pl.BlockSpec(a.shape, const))
            args.append(a)
    hy_width = (2 if conv_seq else 3) * HY_WIDTH
    widths = (QKV_WIDTH, DN_WIDTH, LANES, hy_width, 2 * D_MODEL)
    dts = (act_dt, act_dt, F32, act_dt, act_dt)
    outs = pl.pallas_call(
        functools.partial(_inproj_kernel, pos is not None, conv_seq),
        grid=(n // tm,),
        in_specs=in_specs,
        out_specs=[pl.BlockSpec((tm, w), row) for w in widths],
        out_shape=[jax.ShapeDtypeStruct((n, w), dt) for w, dt in zip(widths, dts)],
        compiler_params=_cparams(1),
        name="inproj",
    )(*args)
    return outs, conv_seq is not None


def _inv_unit_tri(lm, ri, ci, block):
    c = lm.shape[-1]

    def same_block(h):
        sh = h.bit_length() - 1
        return (ri >> sh) == (ci >> sh)

    def unfold(f, h):
        return jnp.where(same_block(h), jnp.concatenate([f] * (c // h), axis=1), 0.0)

    base = INV_BASE
    n_nat = jnp.where(same_block(base), -lm, 0.0)
    n_fold = n_nat[:, 0:base, :]
    for i in range(1, c // base):
        n_fold = n_fold + n_nat[:, i * base:(i + 1) * base, :]
    eye_fold = jnp.where((_iota((base, c), 1) & (base - 1)) == _iota((base, c), 0), 1.0, 0.0)
    t = eye_fold + n_fold
    p_fold, p_nat = n_fold, n_nat
    span = 2
    while span < base:
        p_fold = _bmm(p_fold, p_nat)
        p_nat = unfold(p_fold, base)
        t = t + _bmm(t, p_nat)
        span *= 2
    s = base
    while s < block:
        sh = s.bit_length() - 1
        t_nat = unfold(t, s)
        even = ((_iota((s, c), 1) >> sh) & 1) == 0
        t2 = jnp.concatenate([jnp.where(even, t, 0.0), jnp.where(even, 0.0, t)], axis=1)
        lo = jnp.where(same_block(2 * s) & ~same_block(s), lm, 0.0)
        t = t2 - _bmm(_bmm(t2, lo), t_nat)
        s *= 2
    return t


def _dn_chunk(q, k, v, gc_col, gc_row, beta, e_col, e_rest, e_all, s_prev, same_chunk, ri, ci):
    c = DN_CHUNK
    nh = q.shape[0] // 2
    half = c // 2
    if same_chunk:
        gram = _bmm_nt(k[:nh], k[:nh])
        qk = _bmm_nt(q[:nh], k[:nh])
        gram = jnp.concatenate([gram, gram], 0)
        qk = jnp.concatenate([qk, qk], 0)
    else:
        gram = _bmm_nt(k, k)
        qk = _bmm_nt(q, k)
    diff = gc_col - gc_row
    incl = jnp.concatenate([jnp.broadcast_to(ri >= ci, (nh, c, c)), jnp.broadcast_to(ri <= ci, (nh, c, c))], 0)
    strict = incl & (ri != ci)
    decay = jnp.exp(jnp.where(incl, diff, -jnp.inf))
    lm = jnp.where(strict, beta * gram * decay, 0.0)
    a_intra = qk * decay
    t = _inv_unit_tri(lm, ri, ci, half)
    t1 = t[:, :, :half]
    t2 = t[:, :, half:]
    rhs = jnp.concatenate([v * beta, k * (beta * e_col)], axis=2)
    y1 = _bmm(t1, rhs[:, :half, :])
    y2 = _bmm(t2, rhs[:, half:, :])
    c_f = _bmm(lm[:nh, half:, :half], y1[:nh])
    c_b = _bmm(lm[nh:, :half, half:], y2[nh:])
    c_f = _bmm(t2[:nh], c_f)
    c_b = _bmm(t1[nh:], c_b)
    sol = jnp.concatenate([jnp.concatenate([y1[:nh], y2[:nh] - c_f], 1),
                           jnp.concatenate([y1[nh:] - c_b, y2[nh:]], 1)], 0)
    u = sol[:, :, :DN_DV]
    w = sol[:, :, DN_DV:]
    ks = k * e_rest
    if s_prev is None:
        v_new = u
        o = _bmm(a_intra, v_new)
        s_new = _bmm_tn(ks, v_new)
    else:
        v_new = u - _bmm(w, s_prev)
        o = _bmm(q * e_col, s_prev) + _bmm(a_intra, v_new)
        s_new = s_prev * e_all + _bmm_tn(ks, v_new)
    return o, s_new


def _deltanet_kernel(seq, reqs, zero_init, pre_conv, *refs):
    qkv_ref, cw_ref, z_ref, ba_ref, prm_ref, gn_ref = refs[:6]
    refs = refs[6:]
    if not zero_init:
        s0f_ref, s0b_ref = refs[:2]
        refs = refs[2:]
    o_ref, sf_ref, sb_ref, q_s, k_s, v_s, g_s, b_s, o_s, st_s = refs
    c = DN_CHUNK
    n = seq // c
    nh = DN_HEADS
    nf = reqs * nh
    lanes = lambda j: slice(j * LANES, (j + 1) * LANES)

    if not pre_conv:
        row = _iota((seq, LANES), 0)
        first_row = row == 0
        last_row = row == seq - 1
    for r in range(reqs):
        rows = slice(r * seq, (r + 1) * seq)

        def qkv_part(j):
            x = qkv_ref[rows, lanes(j)].astype(F32)
            return x if pre_conv else _silu(_short_conv3(x, cw_ref[:, lanes(j)], first_row, last_row))

        for h in range(nh):
            q = qkv_part(h)
            k = qkv_part(nh + h)
            q_s[r * nh + h] = q if pre_conv else _l2norm(q, DN_DK ** -0.5)
            k_s[r * nh + h] = k if pre_conv else _l2norm(k)
            v_s[r * nh + h] = qkv_part(2 * nh + h)

    ba = ba_ref[...]
    b_s[...] = _sigmoid(ba)
    g_s[...] = -jnp.exp(prm_ref[0:1, :]) * _softplus(ba + prm_ref[1:2, :])

    ri = _iota((c, c), 0)
    ci = _iota((c, c), 1)
    tri_l = jnp.where(ri >= ci, 1.0, 0.0).astype(F32)
    tri_u = jnp.where(ri <= ci, 1.0, 0.0).astype(F32)

    def chunk_pair(idx_f, idx_b, s_prev):
        def ds(idx, base=0):
            start = idx * c
            start = start if isinstance(start, int) else pl.multiple_of(start, c)
            return pl.ds(base + start, c)

        col_f, col_b, row_f, row_b, beta_f, beta_b = [], [], [], [], [], []
        for r in range(reqs):
            cs_f = _mm_f32(tri_l, g_s[ds(idx_f, r * seq), :])
            cs_b = _mm_f32(tri_u, g_s[ds(idx_b, r * seq), :])
            cst_f = cs_f.T
            cst_b = cs_b.T
            b_f = b_s[ds(idx_f, r * seq), :]
            b_b = b_s[ds(idx_b, r * seq), :]
            for h in range(nh):
                jf, jb = 2 * nh + h, 3 * nh + h
                col_f.append(cs_f[:, jf:jf + 1])
                col_b.append(cs_b[:, jb:jb + 1])
                row_f.append(cst_f[jf:jf + 1, :])
                row_b.append(cst_b[jb:jb + 1, :])
                beta_f.append(b_f[:, h:h + 1])
                beta_b.append(b_b[:, nh + h:nh + h + 1])
        gc_col = jnp.stack(col_f + col_b, 0)
        gc_row = jnp.stack(row_f + row_b, 0)
        beta = jnp.stack(beta_f + beta_b, 0)
        e_col = jnp.exp(gc_col)
        gl = jnp.concatenate([gc_col[:nf, c - 1:c, :], gc_col[nf:, 0:1, :]], 0)
        e_rest = jnp.exp(gl - gc_col)
        e_all = jnp.exp(gl)
        sl_f, sl_b = ds(idx_f), ds(idx_b)
        both = lambda ref: jnp.concatenate([ref[:, sl_f, :], ref[:, sl_b, :]], 0)
        o, s_new = _dn_chunk(both(q_s), both(k_s), both(v_s), gc_col, gc_row, beta, e_col, e_rest, e_all,
                             s_prev, n == 1, ri, ci)
        o_s[:nf, sl_f, :] = o[:nf]
        o_s[nf:, sl_b, :] = o[nf:]
        return s_new

    if not zero_init:
        s0 = jnp.concatenate([s0f_ref[r] for r in range(reqs)] + [s0b_ref[r] for r in range(reqs)], 0)
    if n == 1:
        s_fin = chunk_pair(0, 0, None if zero_init else s0)
    else:
        st_s[...] = jnp.zeros(st_s.shape, F32) if zero_init else s0

        def body(i, carry):
            st_s[...] = chunk_pair(i, n - 1 - i, st_s[...])
            return carry

        lax.fori_loop(0, n, body, 0)
        s_fin = st_s[...]
    for r in range(reqs):
        sf_ref[r] = s_fin[r * nh:(r + 1) * nh]
        sb_ref[r] = s_fin[nf + r * nh:nf + (r + 1) * nh]

    gn = gn_ref[...]
    for r in range(reqs):
        rows = slice(r * seq, (r + 1) * seq)
        for h in range(nh):
            o = o_s[r * nh + h] + o_s[nf + r * nh + h]
            o = o * lax.rsqrt(jnp.mean(o * o, axis=-1, keepdims=True) + RMS_EPS) * gn
            o_ref[rows, lanes(h)] = (o * _silu(z_ref[rows, lanes(h)].astype(F32))).astype(o_ref.dtype)


def _deltanet_call(qkv, z, ba, conv_w, prm, gn, s0f, s0b, bsz, seq, reqs, pre_conv, act_dt):
    nh = DN_HEADS
    zero_init = s0f is None
    rows = reqs * seq
    st = pl.BlockSpec((reqs, nh, DN_DK, DN_DV), lambda b: (b, 0, 0, 0))
    in_specs = [pl.BlockSpec((rows, QKV_WIDTH), lambda b: (b, 0)),
                pl.BlockSpec((3, QKV_WIDTH), lambda b: (0, 0)),
                pl.BlockSpec((rows, DN_WIDTH), lambda b: (b, 0)),
                pl.BlockSpec((rows, LANES), lambda b: (b, 0)),
                pl.BlockSpec((8, LANES), lambda b: (0, 0)),
                pl.BlockSpec((1, LANES), lambda b: (0, 0))]
    args = [qkv, conv_w, z, ba, prm, gn]
    if not zero_init:
        in_specs += [st, st]
        args += [s0f, s0b]
    vm = lambda shape: pltpu.VMEM(shape, F32)
    per_head = vm((reqs * nh, seq, LANES))
    return pl.pallas_call(
        functools.partial(_deltanet_kernel, seq, reqs, zero_init, pre_conv),
        grid=(bsz // reqs,),
        in_specs=in_specs,
        out_specs=[pl.BlockSpec((rows, DN_WIDTH), lambda b: (b, 0)), st, st],
        out_shape=[jax.ShapeDtypeStruct((bsz * seq, DN_WIDTH), act_dt),
                   jax.ShapeDtypeStruct((bsz, nh, DN_DK, DN_DV), F32),
                   jax.ShapeDtypeStruct((bsz, nh, DN_DK, DN_DV), F32)],
        scratch_shapes=[per_head, per_head, per_head, vm((rows, LANES)), vm((rows, LANES)),
                        vm((2 * reqs * nh, seq, LANES)), vm((2 * reqs * nh, DN_DK, DN_DV))],
        compiler_params=_cparams(1),
        name="deltanet",
    )(*args)


def _dft_mats(seq):
    n = 2 * seq
    f = np.arange(seq)[:, None]
    s = np.arange(seq)[None, :]
    ang = 2.0 * np.pi * ((f * s) % n) / n
    fwd = np.concatenate([np.cos(ang), -np.sin(ang)], axis=0)
    fwd[seq, :] = np.cos(np.pi * np.arange(seq))
    t = (np.arange(seq) + seq // 2)[:, None]
    ff = np.arange(seq)[None, :]
    ang2 = 2.0 * np.pi * ((t * ff) % n) / n
    inv_r = 2.0 * np.cos(ang2) / n
    inv_i = -2.0 * np.sin(ang2) / n
    inv_r[:, 0] = 1.0 / n
    inv_i[:, 0] = np.cos(np.pi * t[:, 0]) / n
    inv = np.concatenate([inv_r, inv_i], axis=1)
    return fwd.astype(np.float32), inv.astype(np.float32)


def _filter_feats(seq):
    t = np.linspace(0.0, 1.0, seq)[:, None]
    bands = (HY_EMB - 1) // 2
    ang = (2.0 * math.pi * np.arange(seq) / seq)[:, None] * np.linspace(1e-4, bands - 1, bands)[None, :]
    feats = np.concatenate([t, np.cos(ang), -np.sin(ang)], -1)
    deltas = np.abs(np.linspace(math.log(HY_TARGET) / HY_SLOW_DECAY, math.log(HY_TARGET) / HY_FAST_DECAY,
                                HY_WIDTH))
    offset = np.abs(np.arange(seq) - seq // 2) / (seq // 2)
    window = np.exp(-offset[:, None] * deltas[None, :])
    feats = np.pad(feats, ((0, 0), (0, LANES - HY_EMB)))
    return jnp.asarray(feats, dtype=F32), jnp.asarray(window, dtype=F32)


def _hfilter_kernel(feats_ref, win_ref, w1_ref, b1_ref, w2_ref, b2_ref, w3_ref, fr_ref, fwd_hi_ref,
                    fwd_lo_ref, spec_ref, fhi_s, flo_s):
    @pl.when(pl.program_id(0) == 0)
    def _():
        fr = fr_ref[...]
        hid = jnp.sin(fr * (_mm_f32(feats_ref[...], w1_ref[...]) + b1_ref[...]))
        hid = jnp.sin(fr * (_mm_f32(hid, w2_ref[...]) + b2_ref[...]))
        filt = _mm_f32(hid, w3_ref[...]) * win_ref[...]
        filt = filt / (jnp.sum(jnp.abs(filt), axis=0, keepdims=True) + 1e-6)
        filt_hi = filt.astype(BF16)
        fhi_s[...] = filt_hi
        flo_s[...] = (filt - filt_hi.astype(F32)).astype(BF16)

    fwd_hi = fwd_hi_ref[...]
    spec_ref[...] = (jnp.dot(fwd_hi, fhi_s[...], preferred_element_type=F32)
                     + jnp.dot(fwd_hi, flo_s[...], preferred_element_type=F32)
                     + jnp.dot(fwd_lo_ref[...], fhi_s[...], preferred_element_type=F32))


def _hfilter_call(seq, w1, b1, w2, b2, w3, freq, fwd_hi, fwd_lo):
    feats, window = _filter_feats(seq)
    w1p = jnp.pad(w1, ((0, LANES - HY_EMB), (0, 0)))
    full = lambda a: pl.BlockSpec(a.shape, lambda i: (0,) * a.ndim)
    rb = HFILTER_ROW_BLOCK
    fwd_spec = pl.BlockSpec((rb, seq), lambda i: (i, 0))
    small = [feats, window, w1p, b1.reshape(1, -1), w2, b2.reshape(1, -1), w3, freq.reshape(1, -1)]
    return pl.pallas_call(
        _hfilter_kernel,
        grid=(2 * seq // rb,),
        in_specs=[full(a) for a in small] + [fwd_spec, fwd_spec],
        out_specs=pl.BlockSpec((rb, HY_WIDTH), lambda i: (i, 0)),
        out_shape=jax.ShapeDtypeStruct((2 * seq, HY_WIDTH), F32),
        scratch_shapes=[pltpu.VMEM((seq, HY_WIDTH), BF16), pltpu.VMEM((seq, HY_WIDTH), BF16)],
        compiler_params=_cparams(1),
        name="hfilter",
    )(*small, fwd_hi, fwd_lo)


def _hyena_kernel(seq, pre_conv, *refs):
    row = _iota((seq, HY_CH_BLOCK), 0)
    first = row == 0
    if pre_conv:
        x0_ref, uu_ref, hre_ref, him_ref, skip_ref, fwd_ref, inv_ref, o_ref = refs
        x0 = x0_ref[...].astype(F32)
        uu = uu_ref[...].astype(F32)
    else:
        (x0_ref, x1_ref, v_ref, c0_ref, c1_ref, c2_ref, b0_ref, b1_ref, b2_ref,
         hre_ref, him_ref, skip_ref, fwd_ref, inv_ref, o_ref) = refs
        last = row == seq - 1
        conv = lambda x_ref, w_ref, b_ref: _short_conv3(x_ref[...].astype(F32), w_ref[...], first, last) + b_ref[...]
        x0 = conv(x0_ref, c0_ref, b0_ref)
        uu = conv(x1_ref, c1_ref, b1_ref) * conv(v_ref, c2_ref, b2_ref)
    spec = jnp.dot(fwd_ref[...], uu.astype(BF16), preferred_element_type=F32)
    u_re = spec[:seq, :]
    u_im = spec[seq:, :]
    h_re = hre_ref[...]
    h_im = him_ref[...]
    hb = jnp.where(first, 0.0, h_im)
    hd = jnp.where(first, h_im, h_re)
    y_re = u_re * h_re - u_im * hb
    y_im = u_re * hb + u_im * hd
    y = jnp.concatenate([y_re, y_im], axis=0).astype(BF16)
    cv = jnp.dot(inv_ref[...], y, preferred_element_type=F32)
    o_ref[...] = (x0 * (cv + uu * skip_ref[...])).astype(o_ref.dtype)


def _hyena_call(hy, pre_conv, conv_w, conv_b, hspec, skip, fwd, inv, bsz, seq, act_dt):
    cb = HY_CH_BLOCK
    nblk = HY_WIDTH // cb
    tok = lambda off: pl.BlockSpec((seq, cb), lambda b, j: (b, off + j))
    cw = lambda off: pl.BlockSpec((3, cb), lambda b, j: (0, off + j))
    bias = lambda off: pl.BlockSpec((1, cb), lambda b, j: (0, off + j))
    half = lambda part: pl.BlockSpec((seq, cb), lambda b, j: (part, j))
    const = lambda a: pl.BlockSpec(a.shape, lambda b, j: (0, 0), pipeline_mode=pl.Buffered(1))
    tail_specs = [half(0), half(1), pl.BlockSpec((1, cb), lambda b, j: (0, j)), const(fwd), const(inv)]
    tail_args = [hspec, hspec, skip.reshape(1, -1), fwd, inv]
    if pre_conv:
        in_specs = [tok(0), tok(nblk)] + tail_specs
        args = [hy, hy] + tail_args
    else:
        conv_b2 = conv_b.reshape(1, -1)
        in_specs = [tok(0), tok(nblk), tok(2 * nblk), cw(0), cw(nblk), cw(2 * nblk),
                    bias(0), bias(nblk), bias(2 * nblk)] + tail_specs
        args = [hy, hy, hy, conv_w, conv_w, conv_w, conv_b2, conv_b2, conv_b2] + tail_args
    return pl.pallas_call(
        functools.partial(_hyena_kernel, seq, pre_conv),
        grid=(bsz, nblk),
        in_specs=in_specs,
        out_specs=tok(0),
        out_shape=jax.ShapeDtypeStruct((bsz * seq, HY_WIDTH), act_dt),
        compiler_params=_cparams(2),
        name="hyena",
    )(*args)


def _mix_kernel(has_pos, *refs):
    refs = list(refs)
    x_ref = refs.pop(0)
    pos_ref = refs.pop(0) if has_pos else None
    (oa_ref, ob_ref, g_ref, mod_ref, wua, wub, wout, l1g, l1b, wr_hi, wr_lo, x1_o, u2_o, p_o) = refs
    d = D_MODEL
    tm = x_ref.shape[0]
    sub = min(tm, MIX_SUB_ROWS)
    blocks = [slice(r0, r0 + sub) for r0 in range(0, tm, sub)]
    g1 = mod_ref[0, 2]
    sh2 = mod_ref[0, 3]
    sc2 = mod_ref[0, 4]

    mixed = []
    for rows in blocks:
        gates = g_ref[rows, :].astype(F32)
        ga = _sigmoid(gates[:, :d])
        gb = _sigmoid(gates[:, d:])
        up = (ga * jnp.dot(oa_ref[rows, :], wua[...], preferred_element_type=F32)
              + gb * jnp.dot(ob_ref[rows, :], wub[...], preferred_element_type=F32))
        mixed.append(jnp.dot(up.astype(BF16), wout[...], preferred_element_type=F32))

    for rows, mx in zip(blocks, mixed):
        x = x_ref[rows, :]
        if has_pos:
            x = x + pos_ref[rows, :]
        x1 = _ln(ALPHA * x + g1 * mx) * l1g[...] + l1b[...]
        x1_o[rows, :] = x1.astype(x1_o.dtype)
        u2 = _ln(x1) * (1.0 + sc2) + sh2
        u2_hi = u2.astype(BF16)
        u2_o[rows, :] = u2_hi
        u2_lo = (u2 - u2_hi.astype(F32)).astype(BF16)
        logits = (jnp.dot(u2_hi, wr_hi[...], preferred_element_type=F32)
                  + jnp.dot(u2_hi, wr_lo[...], preferred_element_type=F32)
                  + jnp.dot(u2_lo, wr_hi[...], preferred_element_type=F32))
        lane = _iota(logits.shape, 1)
        logits = jnp.where(lane < N_EXPERTS, logits, -jnp.inf)
        m = jnp.max(logits, axis=-1, keepdims=True)
        e = jnp.exp(logits - m)
        p_o[rows, :] = e / jnp.sum(e, axis=-1, keepdims=True)


def _mix_call(x2d, pos, o_a, o_b, gates, mod4, mod_row, wua, wub, wout, l1g, l1b, wr_hi, wr_lo):
    n, d = x2d.shape
    tm = min(MIX_TILE, max(n // MIX_MIN_STEPS, MIX_SUB_ROWS))
    assert n % tm == 0
    row = lambda i: (i, 0)
    const = lambda i: (0, 0)
    in_specs = [pl.BlockSpec((tm, d), row)]
    args = [x2d]
    if pos is not None:
        tiles = pos.shape[0] // tm
        in_specs.append(pl.BlockSpec((tm, d), lambda i: (i % tiles, 0)))
        args.append(pos)
    in_specs += [pl.BlockSpec((tm, DN_WIDTH), row), pl.BlockSpec((tm, HY_WIDTH), row),
                 pl.BlockSpec((tm, 2 * d), row),
                 _mod_spec(lambda i: mod_row(i * tm))]
    args += [o_a, o_b, gates, mod4]
    for w in (wua, wub, wout, l1g, l1b, wr_hi, wr_lo):
        in_specs.append(pl.BlockSpec(w.shape, const, pipeline_mode=pl.Buffered(1)))
        args.append(w)
    return pl.pallas_call(
        functools.partial(_mix_kernel, pos is not None),
        grid=(n // tm,),
        in_specs=in_specs,
        out_specs=[pl.BlockSpec((tm, d), row), pl.BlockSpec((tm, d), row), pl.BlockSpec((tm, LANES), row)],
        out_shape=[jax.ShapeDtypeStruct((n, d), ACT_DT), jax.ShapeDtypeStruct((n, d), BF16),
                   jax.ShapeDtypeStruct((n, LANES), F32)],
        compiler_params=_cparams(1),
        name="mix",
    )(*args)


def _route_kernel(seq, cap, p_ref, u_ref, tri_ref, xs_ref, slot_ref, pt_ref, g_s):
    e_n = N_EXPERTS
    p = p_ref[...]
    pt = p.T[:e_n, :]
    tri = tri_ref[...]

    if seq <= RANK_COUNT_MAX_SEQ:
        earlier = _iota((seq, seq), 0) < _iota((seq, seq), 1)
        ranks = []
        for e in range(e_n):
            pc = p[:, e:e + 1]
            pr = pt[e:e + 1, :]
            beats = (pc > pr) | (earlier & (pc == pr))
            ranks.append(jnp.sum(jnp.where(beats, 1.0, 0.0), axis=0, keepdims=True))
        sel = jnp.concatenate(ranks, axis=0) < cap
    else:
        def search(i, cur):
            cand = cur | (1 << (30 - i))
            cand_f = pltpu.bitcast(cand, F32)
            cnt = jnp.sum(jnp.where(pt >= cand_f, 1.0, 0.0), axis=1, keepdims=True)
            return jnp.where((cnt >= cap) & (cand >= MIN_NORMAL_F32_BITS), cand, cur)

        thr = pltpu.bitcast(lax.fori_loop(0, 31, search, jnp.zeros((e_n, 1), jnp.int32)), F32)
        gt = pt > thr
        eq = pt == thr
        n_gt = jnp.sum(jnp.where(gt, 1.0, 0.0), axis=1, keepdims=True)
        eq_rank = jnp.dot(jnp.where(eq, 1.0, 0.0).astype(BF16), tri, preferred_element_type=F32)
        sel = gt | (eq & (eq_rank < cap - n_gt))
    pos = jnp.dot(jnp.where(sel, 1.0, 0.0).astype(BF16), tri, preferred_element_type=F32)
    slot = jnp.where(sel, pos, -1.0)
    slot_ref[0] = slot
    pt_ref[0] = pt
    jrow = _iota((cap, seq), 0).astype(F32)
    for e in range(e_n):
        g_s[e * cap:(e + 1) * cap, :] = jnp.where(slot[e:e + 1, :] == jrow, 1.0, 0.0).astype(BF16)
    xs = jnp.dot(g_s[...], u_ref[...], preferred_element_type=F32)
    for e in range(e_n):
        xs_ref[e] = xs[e * cap:(e + 1) * cap, :].astype(xs_ref.dtype)


def _route_call(probs, u2, bsz, seq):
    cap = EC_FACTOR * seq // N_EXPERTS
    tri = jnp.asarray(np.triu(np.ones((seq, seq), np.float32), 1), dtype=BF16)
    return pl.pallas_call(
        functools.partial(_route_kernel, seq, cap),
        grid=(bsz,),
        in_specs=[pl.BlockSpec((seq, LANES), lambda b: (b, 0)),
                  pl.BlockSpec((seq, D_MODEL), lambda b: (b, 0)),
                  pl.BlockSpec((seq, seq), lambda b: (0, 0), pipeline_mode=pl.Buffered(1))],
        out_specs=[pl.BlockSpec((N_EXPERTS, cap, D_MODEL), lambda b: (0, b, 0)),
                   pl.BlockSpec((1, N_EXPERTS, seq), lambda b: (b, 0, 0)),
                   pl.BlockSpec((1, N_EXPERTS, seq), lambda b: (b, 0, 0))],
        out_shape=[jax.ShapeDtypeStruct((N_EXPERTS, bsz * cap, D_MODEL), BF16),
                   jax.ShapeDtypeStruct((bsz, N_EXPERTS, seq), F32),
                   jax.ShapeDtypeStruct((bsz, N_EXPERTS, seq), F32)],
        scratch_shapes=[pltpu.VMEM((N_EXPERTS * cap, seq), BF16)],
        compiler_params=_cparams(1),
        name="route",
    )(probs, u2, tri)


def _expert_kernel(xc_ref, xd_ref, wg_ref, wu_ref, wd_ref, yc_ref, yd_ref):
    wg = wg_ref[0].astype(BF16)
    wu = wu_ref[0].astype(BF16)
    wd = wd_ref[0].astype(BF16)
    for x_ref, y_ref in ((xc_ref, yc_ref), (xd_ref, yd_ref)):
        rows = x_ref.shape[1]
        rb = min(rows, EXPERT_ROW_BLOCK)
        for r0 in range(0, rows, rb):
            x = x_ref[0, r0:r0 + rb, :]
            hg = jnp.dot(x, wg, preferred_element_type=F32)
            hu = jnp.dot(x, wu, preferred_element_type=F32)
            y = jnp.dot((_silu(hg) * hu).astype(BF16), wd, preferred_element_type=F32)
            y_ref[0, r0:r0 + rb, :] = y.astype(y_ref.dtype)


def _expert_call(xs_c, xs_d, w_gate, w_up, w_down):
    e_n, rc, d = xs_c.shape
    rd = xs_d.shape[1]
    ff = w_gate.shape[2]
    return pl.pallas_call(
        _expert_kernel,
        grid=(e_n,),
        in_specs=[pl.BlockSpec((1, rc, d), lambda e: (e, 0, 0)),
                  pl.BlockSpec((1, rd, d), lambda e: (e, 0, 0)),
                  pl.BlockSpec((1, d, ff), lambda e: (e, 0, 0)),
                  pl.BlockSpec((1, d, ff), lambda e: (e, 0, 0)),
                  pl.BlockSpec((1, ff, d), lambda e: (e, 0, 0))],
        out_specs=[pl.BlockSpec((1, rc, d), lambda e: (e, 0, 0)),
                   pl.BlockSpec((1, rd, d), lambda e: (e, 0, 0))],
        out_shape=[jax.ShapeDtypeStruct((e_n, rc, d), BF16), jax.ShapeDtypeStruct((e_n, rd, d), BF16)],
        compiler_params=_cparams(1),
        name="experts",
    )(xs_c, xs_d, w_gate, w_up, w_down)


def _final_kernel(seq, cap, reqs, x1_ref, y_ref, slot_ref, pt_ref, mod_ref, l2g, l2b, o_ref, w_s):
    jrow = _iota((cap, seq), 0).astype(F32)
    g2 = mod_ref[0, 5]
    for r in range(reqs):
        slot = slot_ref[r]
        pt = pt_ref[r]
        for e in range(N_EXPERTS):
            w_s[r, e * cap:(e + 1) * cap, :] = jnp.where(slot[e:e + 1, :] == jrow, pt[e:e + 1, :],
                                                         0.0).astype(BF16)
        y = y_ref[:, r * cap:(r + 1) * cap, :].reshape(N_EXPERTS * cap, y_ref.shape[-1])
        ffn = lax.dot_general(w_s[r], y, (((0,), (0,)), ((), ())), preferred_element_type=F32)
        rows = slice(r * seq, (r + 1) * seq)
        o_ref[rows, :] = _ln(ALPHA * x1_ref[rows, :].astype(F32) + g2 * ffn) * l2g[...] + l2b[...]


def _final_call(x1, ys, slot, pt, mod4, mod_row, l2g, l2b, bsz, seq, reqs):
    cap = EC_FACTOR * seq // N_EXPERTS
    d = D_MODEL
    per_req = pl.BlockSpec((reqs, N_EXPERTS, seq), lambda b: (b, 0, 0))
    return pl.pallas_call(
        functools.partial(_final_kernel, seq, cap, reqs),
        grid=(bsz // reqs,),
        in_specs=[pl.BlockSpec((reqs * seq, d), lambda b: (b, 0)),
                  pl.BlockSpec((N_EXPERTS, reqs * cap, d), lambda b: (0, b, 0)),
                  per_req, per_req,
                  _mod_spec(lambda b: mod_row(b * reqs * seq)),
                  pl.BlockSpec((1, d), lambda b: (0, 0)),
                  pl.BlockSpec((1, d), lambda b: (0, 0))],
        out_specs=pl.BlockSpec((reqs * seq, d), lambda b: (b, 0)),
        out_shape=jax.ShapeDtypeStruct((bsz * seq, d), F32),
        scratch_shapes=[pltpu.VMEM((reqs, N_EXPERTS * cap, seq), BF16)],
        compiler_params=_cparams(1),
        name="final",
    )(x1, ys, slot, pt, mod4, l2g, l2b)


def _grid_pos_embed(rows, dim):
    r = np.repeat(np.arange(rows), GRID_W)
    col = np.tile(np.arange(GRID_W), rows)
    quarter = dim // 4
    omega = 1.0 / (10000.0 ** (np.arange(quarter) / quarter))

    def enc(p):
        ang = p[:, None] * omega[None, :]
        return np.concatenate([np.sin(ang), np.cos(ang)], -1)

    return jnp.asarray(np.concatenate([enc(r), enc(col)], -1), dtype=F32)


def kernel(x_prompt, x_sample, state_delta_fwd, state_delta_bwd, c, c_ctx, w_mod, b_mod, w_in, conv_a_w, a_log, dt_bias, gnorm_w, conv_b_w, conv_b_b, hy_w1, hy_b1, hy_w2, hy_b2, hy_w3, hy_freq, hy_bias, w_up_a, w_up_b, w_out, ln1_g, ln1_b, w_router, w_e_gate, w_e_up, w_e_down, ln2_g, ln2_b):
    d = D_MODEL
    n_ctx, l_ctx, _ = x_prompt.shape
    n_dec, l_dec, _ = x_sample.shape
    lyr = 0

    cond_t = jnp.pad(jnp.concatenate([c_ctx[None, :], c], 0).T, ((0, 0), (0, 7 - n_dec)))
    mod4 = _mod_call(cond_t, 1 + n_dec, w_mod[lyr], b_mod[lyr]).reshape(8, N_MOD, 1, d)

    in_w = w_in[lyr].astype(BF16)

    prm = jnp.pad(jnp.stack([a_log[lyr].reshape(-1), dt_bias[lyr].reshape(-1)]),
                  ((0, 6), (2 * DN_HEADS, LANES - 4 * DN_HEADS)))
    gn = gnorm_w[lyr].reshape(1, -1)

    wua = w_up_a[lyr].astype(BF16)
    wub = w_up_b[lyr].astype(BF16)
    wout = w_out[lyr].astype(BF16)
    wr = jnp.pad(w_router[lyr], ((0, 0), (0, LANES - N_EXPERTS)))
    wr_hi = wr.astype(BF16)
    wr_lo = (wr - wr_hi.astype(F32)).astype(BF16)
    l1g, l1b = ln1_g[lyr].reshape(1, -1), ln1_b[lyr].reshape(1, -1)
    l2g, l2b = ln2_g[lyr].reshape(1, -1), ln2_b[lyr].reshape(1, -1)

    def front(x2d, pos, bsz, seq, mod_row, s0f, s0b):
        (qkv, z, ba, hy, gates), pre_conv = _inproj_call(x2d, pos, mod4, mod_row, in_w, conv_a_w[lyr],
                                                          conv_b_w[lyr], conv_b_b[lyr], seq, ACT_DT)
        reqs = DN_REQS_PER_STEP if seq == DN_CHUNK and bsz % DN_REQS_PER_STEP == 0 else 1
        o_a, s_f, s_b = _deltanet_call(qkv, z, ba, conv_a_w[lyr], prm, gn, s0f, s0b, bsz, seq, reqs,
                                       pre_conv, ACT_DT)
        fwd, inv = (jnp.asarray(m) for m in _dft_mats(seq))
        fwd_hi = fwd.astype(BF16)
        fwd_lo = (fwd - fwd_hi.astype(F32)).astype(BF16)
        hspec = _hfilter_call(seq, hy_w1[lyr], hy_b1[lyr], hy_w2[lyr], hy_b2[lyr], hy_w3[lyr],
                              hy_freq[lyr], fwd_hi, fwd_lo)
        o_h = _hyena_call(hy, pre_conv, conv_b_w[lyr], conv_b_b[lyr], hspec, hy_bias[lyr],
                          fwd_hi, inv.astype(BF16), bsz, seq, ACT_DT)
        x1, u2, probs = _mix_call(x2d, pos, o_a, o_h, gates, mod4, mod_row, wua, wub, wout, l1g, l1b,
                                   wr_hi, wr_lo)
        xs, slot, pt = _route_call(probs, u2, bsz, seq)
        return x1, xs, slot, pt, s_f, s_b

    row_ctx = lambda tok: 0
    row_dec = lambda tok: 1 + tok // l_dec

    xc = x_prompt.reshape(n_ctx * l_ctx, d)
    xd = x_sample.reshape(n_dec * l_dec, d)
    pos = _grid_pos_embed(l_dec // GRID_W, d)

    x1c, xsc, gc, ptc, s_f, s_b = front(xc, None, n_ctx, l_ctx, row_ctx, None, None)
    x1d, xsd, gd, ptd, _, _ = front(xd, pos, n_dec, l_dec, row_dec,
                                    state_delta_fwd[:, lyr], state_delta_bwd[:, lyr])
    yc, yd = _expert_call(xsc, xsd, w_e_gate[lyr], w_e_up[lyr], w_e_down[lyr])
    reqs_c = FINAL_CTX_REQS_PER_STEP if n_ctx % FINAL_CTX_REQS_PER_STEP == 0 else 1
    y_prompt = _final_call(x1c, yc, gc, ptc, mod4, row_ctx, l2g, l2b, n_ctx, l_ctx, reqs_c)
    y_sample = _final_call(x1d, yd, gd, ptd, mod4, row_dec, l2g, l2b, n_dec, l_dec, 1)

    return (y_prompt.reshape(n_ctx, l_ctx, d), y_sample.reshape(n_dec, l_dec, d),
            s_f[:, None], s_b[:, None])
```

```python
import functools
import math

import jax
import jax.numpy as jnp
import numpy as np
from jax import lax
from jax.experimental import pallas as pl
from jax.experimental.pallas import tpu as pltpu

F32 = jnp.float32
BF16 = jnp.bfloat16
HIGHEST = lax.Precision.HIGHEST

D_MODEL = 1024
DEPTH = 1
GRID_W = 64
DN_HEADS = 4
DN_DK = 128
DN_DV = 128
DN_WIDTH = DN_HEADS * DN_DV
HY_WIDTH = D_MODEL // 2
HY_EMB = 33
HY_FFN = 64
HY_TARGET = 1e-2
HY_FAST_DECAY = 0.3
HY_SLOW_DECAY = 1.5
N_EXPERTS = 16
EC_FACTOR = 2
N_MOD = 6
ALPHA = (2 * DEPTH) ** 0.25
LN_EPS = 1e-5
RMS_EPS = 1e-6
QKV_WIDTH = 2 * DN_HEADS * DN_DK + DN_WIDTH

LANES = 128
INPROJ_TILE = 512
MIX_TILE = 1024
MIX_SUB_ROWS = 256
MIX_MIN_STEPS = 4
MOD_COL_BLOCK = 1024
DN_CHUNK = 256
INV_BASE = 16
DN_REQS_PER_STEP = 2
HY_CH_BLOCK = 512
HY_ROWS_PER_STEP = 1024
FINAL_CTX_REQS_PER_STEP = 2
EXPERT_ROW_BLOCK = 512
VMEM_LIMIT = 56 * 1024 * 1024
MIN_NORMAL_F32_BITS = 0x00800000
ACT_DT = BF16
ROUTE_ROWS_PER_STEP = 1024
RANK_COUNT_MAX_SEQ = 256


def _cparams(n_axes):
    return pltpu.CompilerParams(dimension_semantics=("arbitrary",) * n_axes,
                                vmem_limit_bytes=VMEM_LIMIT)


def _bmm(a, b):
    return jnp.einsum("bmk,bkn->bmn", a.astype(BF16), b.astype(BF16), preferred_element_type=F32)


def _bmm_nt(a, b):
    return jnp.einsum("bmk,bnk->bmn", a.astype(BF16), b.astype(BF16), preferred_element_type=F32)


def _bmm_tn(a, b):
    return jnp.einsum("bkm,bkn->bmn", a.astype(BF16), b.astype(BF16), preferred_element_type=F32)


def _mm_f32(a, b):
    return jnp.dot(a, b, precision=HIGHEST, preferred_element_type=F32)


def _sigmoid(x):
    return 0.5 * jnp.tanh(0.5 * x) + 0.5


def _silu(x):
    return x * _sigmoid(x)


def _softplus(x):
    return jnp.maximum(x, 0.0) + jnp.log1p(jnp.exp(-jnp.abs(x)))


def _ln(x):
    mu = jnp.mean(x, axis=-1, keepdims=True)
    xc = x - mu
    var = jnp.mean(xc * xc, axis=-1, keepdims=True)
    return xc * lax.rsqrt(var + LN_EPS)


def _l2norm(x, scale=1.0):
    return x * (lax.rsqrt(jnp.sum(x * x, axis=-1, keepdims=True) + 1e-6) * scale)


def _iota(shape, dim):
    return lax.broadcasted_iota(jnp.int32, shape, dim)


def _short_conv3(x, w, first, last):
    xm = jnp.where(first, 0.0, pltpu.roll(x, 1, 0))
    xp = jnp.where(last, 0.0, pltpu.roll(x, x.shape[0] - 1, 0))
    return w[0:1, :] * xm + w[1:2, :] * x + w[2:3, :] * xp


def _mod_kernel(n_rows, ct_ref, w_ref, b_ref, o_ref):
    s = _silu(ct_ref[...])
    w = w_ref[...]
    b = b_ref[...]
    o_ref[...] = jnp.zeros(o_ref.shape, F32)
    for r in range(n_rows):
        o_ref[r:r + 1, :] = jnp.sum(w * s[:, r:r + 1], axis=0, keepdims=True) + b


def _mod_call(cond_t, n_rows, w_mod, b_mod):
    d = D_MODEL
    tn = MOD_COL_BLOCK
    return pl.pallas_call(
        functools.partial(_mod_kernel, n_rows),
        grid=(N_MOD * d // tn,),
        in_specs=[pl.BlockSpec((d, 8), lambda j: (0, 0)),
                  pl.BlockSpec((d, tn), lambda j: (0, j)),
                  pl.BlockSpec((1, tn), lambda j: (0, j))],
        out_specs=pl.BlockSpec((8, tn), lambda j: (0, j)),
        out_shape=jax.ShapeDtypeStruct((8, N_MOD * d), F32),
        compiler_params=_cparams(1),
        name="mod",
    )(cond_t, w_mod, b_mod.reshape(1, -1))


def _mod_spec(row_of_step):
    return pl.BlockSpec((1, N_MOD, 1, D_MODEL), lambda *idx: (row_of_step(*idx), 0, 0, 0))


def _inproj_kernel(has_pos, conv_seq, *refs):
    refs = list(refs)
    x_ref = refs.pop(0)
    pos_ref = refs.pop(0) if has_pos else None
    mod_ref, w_ref = refs.pop(0), refs.pop(0)
    if conv_seq:
        cwa_ref, cwb_ref, cbb_ref = refs.pop(0), refs.pop(0), refs.pop(0)
    qkv_o, z_o, ba_o, hy_o, g_o = refs
    sh1 = mod_ref[0, 0]
    sc1 = mod_ref[0, 1]
    o_z = QKV_WIDTH
    o_ba = o_z + DN_WIDTH
    n_ba = 4 * DN_HEADS
    o_g = n_ba + 3 * HY_WIDTH
    lanes = lambda j: slice(j * LANES, (j + 1) * LANES)
    tm = x_ref.shape[0]
    sub = conv_seq if conv_seq else tm
    blocks = [slice(r0, r0 + sub) for r0 in range(0, tm, sub)]

    proj = []
    for rows in blocks:
        x = x_ref[rows, :]
        if has_pos:
            x = x + pos_ref[rows, :]
        u = (_ln(x) * (1.0 + sc1) + sh1).astype(BF16)
        proj.append((jnp.dot(u, w_ref[:, :o_ba], preferred_element_type=F32),
                     jnp.dot(u, w_ref[:, o_ba:], preferred_element_type=F32)))

    if conv_seq:
        row = _iota((sub, LANES), 0)
        first = row == 0
        last = row == sub - 1
    for rows, (head, tail) in zip(blocks, proj):
        z_o[rows, :] = head[:, o_z:].astype(z_o.dtype)
        ba_o[rows, :] = tail[:, :LANES]
        g_o[rows, :] = tail[:, o_g:].astype(g_o.dtype)
        hy = tail[:, n_ba:o_g]
        if not conv_seq:
            qkv_o[rows, :] = head[:, :o_z].astype(qkv_o.dtype)
            hy_o[rows, :] = hy.astype(hy_o.dtype)
            continue
        for j in range(QKV_WIDTH // LANES):
            y = _silu(_short_conv3(head[:, lanes(j)], cwa_ref[:, lanes(j)], first, last))
            if j < 2 * DN_HEADS:
                y = _l2norm(y, DN_DK ** -0.5 if j < DN_HEADS else 1.0)
            qkv_o[rows, lanes(j)] = y.astype(qkv_o.dtype)
        nj = HY_WIDTH // LANES
        for j in range(nj):
            parts = [_short_conv3(hy[:, lanes(p * nj + j)], cwb_ref[:, lanes(p * nj + j)], first, last)
                     + cbb_ref[:, lanes(p * nj + j)] for p in range(3)]
            hy_o[rows, lanes(j)] = parts[0].astype(hy_o.dtype)
            hy_o[rows, lanes(nj + j)] = (parts[1] * parts[2]).astype(hy_o.dtype)


def _inproj_call(x2d, pos, mod4, mod_row, w, conv_a_w, conv_b_w, conv_b_b, seq, act_dt):
    n, d = x2d.shape
    tm = min(INPROJ_TILE, n)
    assert n % tm == 0
    conv_seq = seq if tm % seq == 0 else None
    row = lambda i: (i, 0)
    const = lambda i: (0, 0)
    in_specs = [pl.BlockSpec((tm, d), row)]
    args = [x2d]
    if pos is not None:
        tiles = pos.shape[0] // tm
        in_specs.append(pl.BlockSpec((tm, d), lambda i: (i % tiles, 0)))
        args.append(pos)
    in_specs.append(_mod_spec(lambda i: mod_row(i * tm)))
    args.append(mod4)
    in_specs.append(pl.BlockSpec(w.shape, const, pipeline_mode=pl.Buffered(1)))
    args.append(w)
    if conv_seq:
        for a in (conv_a_w, conv_b_w, conv_b_b.reshape(1, -1)):
            in_specs.append(pl.BlockSpec(a.shape, const))
            args.append(a)
    hy_width = (2 if conv_seq else 3) * HY_WIDTH
    widths = (QKV_WIDTH, DN_WIDTH, LANES, hy_width, 2 * D_MODEL)
    dts = (act_dt, act_dt, F32, act_dt, act_dt)
    outs = pl.pallas_call(
        functools.partial(_inproj_kernel, pos is not None, conv_seq),
        grid=(n // tm,),
        in_specs=in_specs,
        out_specs=[pl.BlockSpec((tm, w), row) for w in widths],
        out_shape=[jax.ShapeDtypeStruct((n, w), dt) for w, dt in zip(widths, dts)],
        compiler_params=_cparams(1),
        name="inproj",
    )(*args)
    return outs, conv_seq is not None


def _inv_unit_tri(lm, ri, ci, block):
    c = lm.shape[-1]

    def same_block(h):
        sh = h.bit_length() - 1
        return (ri >> sh) == (ci >> sh)

    def unfold(f, h):
        return jnp.where(same_block(h), jnp.concatenate([f] * (c // h), axis=1), 0.0)

    base = INV_BASE
    n_nat = jnp.where(same_block(base), -lm, 0.0)
    n_fold = n_nat[:, 0:base, :]
    for i in range(1, c // base):
        n_fold = n_fold + n_nat[:, i * base:(i + 1) * base, :]
    eye_fold = jnp.where((_iota((base, c), 1) & (base - 1)) == _iota((base, c), 0), 1.0, 0.0)
    t = eye_fold + n_fold
    p_fold, p_nat = n_fold, n_nat
    span = 2
    while span < base:
        p_fold = _bmm(p_fold, p_nat)
        p_nat = unfold(p_fold, base)
        t = t + _bmm(t, p_nat)
        span *= 2
    s = base
    while s < block:
        sh = s.bit_length() - 1
        t_nat = unfold(t, s)
        even = ((_iota((s, c), 1) >> sh) & 1) == 0
        t2 = jnp.concatenate([jnp.where(even, t, 0.0), jnp.where(even, 0.0, t)], axis=1)
        lo = jnp.where(same_block(2 * s) & ~same_block(s), lm, 0.0)
        t = t2 - _bmm(_bmm(t2, lo), t_nat)
        s *= 2
    return t


def _dn_chunk(q, k, v, gc_col, gc_row, beta, e_col, e_rest, e_all, s_prev, same_chunk, ri, ci):
    c = DN_CHUNK
    nh = q.shape[0] // 2
    half = c // 2
    if same_chunk:
        gram = _bmm_nt(k[:nh], k[:nh])
        qk = _bmm_nt(q[:nh], k[:nh])
        gram = jnp.concatenate([gram, gram], 0)
        qk = jnp.concatenate([qk, qk], 0)
    else:
        gram = _bmm_nt(k, k)
        qk = _bmm_nt(q, k)
    diff = gc_col - gc_row
    incl = jnp.concatenate([jnp.broadcast_to(ri >= ci, (nh, c, c)), jnp.broadcast_to(ri <= ci, (nh, c, c))], 0)
    strict = incl & (ri != ci)
    decay = jnp.exp(jnp.where(incl, diff, -jnp.inf))
    lm = jnp.where(strict, beta * gram * decay, 0.0)
    a_intra = qk * decay
    t = _inv_unit_tri(lm, ri, ci, half)
    t1 = t[:, :, :half]
    t2 = t[:, :, half:]
    rhs = jnp.concatenate([v * beta, k * (beta * e_col)], axis=2)
    y1 = _bmm(t1, rhs[:, :half, :])
    y2 = _bmm(t2, rhs[:, half:, :])
    c_f = _bmm(lm[:nh, half:, :half], y1[:nh])
    c_b = _bmm(lm[nh:, :half, half:], y2[nh:])
    c_f = _bmm(t2[:nh], c_f)
    c_b = _bmm(t1[nh:], c_b)
    sol = jnp.concatenate([jnp.concatenate([y1[:nh], y2[:nh] - c_f], 1),
                           jnp.concatenate([y1[nh:] - c_b, y2[nh:]], 1)], 0)
    u = sol[:, :, :DN_DV]
    w = sol[:, :, DN_DV:]
    ks = k * e_rest
    if s_prev is None:
        v_new = u
        o = _bmm(a_intra, v_new)
        s_new = _bmm_tn(ks, v_new)
    else:
        v_new = u - _bmm(w, s_prev)
        o = _bmm(q * e_col, s_prev) + _bmm(a_intra, v_new)
        s_new = s_prev * e_all + _bmm_tn(ks, v_new)
    return o, s_new


def _deltanet_kernel(seq, reqs, zero_init, pre_conv, *refs):
    qkv_ref, cw_ref, z_ref, ba_ref, prm_ref, gn_ref = refs[:6]
    refs = refs[6:]
    if not zero_init:
        s0f_ref, s0b_ref = refs[:2]
        refs = refs[2:]
    o_ref, sf_ref, sb_ref, q_s, k_s, v_s, g_s, b_s, o_s, st_s = refs
    c = DN_CHUNK
    n = seq // c
    nh = DN_HEADS
    nf = reqs * nh
    lanes = lambda j: slice(j * LANES, (j + 1) * LANES)

    if not pre_conv:
        row = _iota((seq, LANES), 0)
        first_row = row == 0
        last_row = row == seq - 1
    for r in range(reqs):
        rows = slice(r * seq, (r + 1) * seq)

        def qkv_part(j):
            x = qkv_ref[rows, lanes(j)].astype(F32)
            return x if pre_conv else _silu(_short_conv3(x, cw_ref[:, lanes(j)], first_row, last_row))

        for h in range(nh):
            q = qkv_part(h)
            k = qkv_part(nh + h)
            q_s[r * nh + h] = q if pre_conv else _l2norm(q, DN_DK ** -0.5)
            k_s[r * nh + h] = k if pre_conv else _l2norm(k)
            v_s[r * nh + h] = qkv_part(2 * nh + h)

    ba = ba_ref[...]
    b_s[...] = _sigmoid(ba)
    g_s[...] = -jnp.exp(prm_ref[0:1, :]) * _softplus(ba + prm_ref[1:2, :])

    ri = _iota((c, c), 0)
    ci = _iota((c, c), 1)
    tri_l = jnp.where(ri >= ci, 1.0, 0.0).astype(F32)
    tri_u = jnp.where(ri <= ci, 1.0, 0.0).astype(F32)

    def chunk_pair(idx_f, idx_b, s_prev):
        def ds(idx, base=0):
            start = idx * c
            start = start if isinstance(start, int) else pl.multiple_of(start, c)
            return pl.ds(base + start, c)

        col_f, col_b, row_f, row_b, beta_f, beta_b = [], [], [], [], [], []
        for r in range(reqs):
            cs_f = _mm_f32(tri_l, g_s[ds(idx_f, r * seq), :])
            cs_b = _mm_f32(tri_u, g_s[ds(idx_b, r * seq), :])
            cst_f = cs_f.T
            cst_b = cs_b.T
            b_f = b_s[ds(idx_f, r * seq), :]
            b_b = b_s[ds(idx_b, r * seq), :]
            for h in range(nh):
                jf, jb = 2 * nh + h, 3 * nh + h
                col_f.append(cs_f[:, jf:jf + 1])
                col_b.append(cs_b[:, jb:jb + 1])
                row_f.append(cst_f[jf:jf + 1, :])
                row_b.append(cst_b[jb:jb + 1, :])
                beta_f.append(b_f[:, h:h + 1])
                beta_b.append(b_b[:, nh + h:nh + h + 1])
        gc_col = jnp.stack(col_f + col_b, 0)
        gc_row = jnp.stack(row_f + row_b, 0)
        beta = jnp.stack(beta_f + beta_b, 0)
        e_col = jnp.exp(gc_col)
        gl = jnp.concatenate([gc_col[:nf, c - 1:c, :], gc_col[nf:, 0:1, :]], 0)
        e_rest = jnp.exp(gl - gc_col)
        e_all = jnp.exp(gl)
        sl_f, sl_b = ds(idx_f), ds(idx_b)
        both = lambda ref: jnp.concatenate([ref[:, sl_f, :], ref[:, sl_b, :]], 0)
        o, s_new = _dn_chunk(both(q_s), both(k_s), both(v_s), gc_col, gc_row, beta, e_col, e_rest, e_all,
                             s_prev, n == 1, ri, ci)
        o_s[:nf, sl_f, :] = o[:nf]
        o_s[nf:, sl_b, :] = o[nf:]
        return s_new

    if not zero_init:
        s0 = jnp.concatenate([s0f_ref[r] for r in range(reqs)] + [s0b_ref[r] for r in range(reqs)], 0)
    if n == 1:
        s_fin = chunk_pair(0, 0, None if zero_init else s0)
    else:
        st_s[...] = jnp.zeros(st_s.shape, F32) if zero_init else s0

        def body(i, carry):
            st_s[...] = chunk_pair(i, n - 1 - i, st_s[...])
            return carry

        lax.fori_loop(0, n, body, 0)
        s_fin = st_s[...]
    for r in range(reqs):
        sf_ref[r] = s_fin[r * nh:(r + 1) * nh]
        sb_ref[r] = s_fin[nf + r * nh:nf + (r + 1) * nh]

    gn = gn_ref[...]
    for r in range(reqs):
        rows = slice(r * seq, (r + 1) * seq)
        for h in range(nh):
            o = o_s[r * nh + h] + o_s[nf + r * nh + h]
            o = o * lax.rsqrt(jnp.mean(o * o, axis=-1, keepdims=True) + RMS_EPS) * gn
            o_ref[rows, lanes(h)] = (o * _silu(z_ref[rows, lanes(h)].astype(F32))).astype(o_ref.dtype)


def _deltanet_call(qkv, z, ba, conv_w, prm, gn, s0f, s0b, bsz, seq, reqs, pre_conv, act_dt):
    nh = DN_HEADS
    zero_init = s0f is None
    rows = reqs * seq
    st = pl.BlockSpec((reqs, nh, DN_DK, DN_DV), lambda b: (b, 0, 0, 0))
    in_specs = [pl.BlockSpec((rows, QKV_WIDTH), lambda b: (b, 0)),
                pl.BlockSpec((3, QKV_WIDTH), lambda b: (0, 0)),
                pl.BlockSpec((rows, DN_WIDTH), lambda b: (b, 0)),
                pl.BlockSpec((rows, LANES), lambda b: (b, 0)),
                pl.BlockSpec((8, LANES), lambda b: (0, 0)),
                pl.BlockSpec((1, LANES), lambda b: (0, 0))]
    args = [qkv, conv_w, z, ba, prm, gn]
    if not zero_init:
        in_specs += [st, st]
        args += [s0f, s0b]
    vm = lambda shape: pltpu.VMEM(shape, F32)
    per_head = vm((reqs * nh, seq, LANES))
    return pl.pallas_call(
        functools.partial(_deltanet_kernel, seq, reqs, zero_init, pre_conv),
        grid=(bsz // reqs,),
        in_specs=in_specs,
        out_specs=[pl.BlockSpec((rows, DN_WIDTH), lambda b: (b, 0)), st, st],
        out_shape=[jax.ShapeDtypeStruct((bsz * seq, DN_WIDTH), act_dt),
                   jax.ShapeDtypeStruct((bsz, nh, DN_DK, DN_DV), F32),
                   jax.ShapeDtypeStruct((bsz, nh, DN_DK, DN_DV), F32)],
        scratch_shapes=[per_head, per_head, per_head, vm((rows, LANES)), vm((rows, LANES)),
                        vm((2 * reqs * nh, seq, LANES)), vm((2 * reqs * nh, DN_DK, DN_DV))],
        compiler_params=_cparams(1),
        name="deltanet",
    )(*args)


def _dft_mats(seq):
    n = 2 * seq
    f = np.arange(seq)[:, None]
    s = np.arange(seq)[None, :]
    ang = 2.0 * np.pi * ((f * s) % n) / n
    fwd = np.concatenate([np.cos(ang), -np.sin(ang)], axis=0)
    fwd[seq, :] = np.cos(np.pi * np.arange(seq))
    t = (np.arange(seq) + seq // 2)[:, None]
    ff = np.arange(seq)[None, :]
    ang2 = 2.0 * np.pi * ((t * ff) % n) / n
    inv_r = 2.0 * np.cos(ang2) / n
    inv_i = -2.0 * np.sin(ang2) / n
    inv_r[:, 0] = 1.0 / n
    inv_i[:, 0] = np.cos(np.pi * t[:, 0]) / n
    inv = np.concatenate([inv_r, inv_i], axis=1)
    return fwd.astype(np.float32), inv.astype(np.float32)


def _filter_feats(seq):
    t = np.linspace(0.0, 1.0, seq)[:, None]
    bands = (HY_EMB - 1) // 2
    ang = (2.0 * math.pi * np.arange(seq) / seq)[:, None] * np.linspace(1e-4, bands - 1, bands)[None, :]
    feats = np.concatenate([t, np.cos(ang), -np.sin(ang)], -1)
    deltas = np.abs(np.linspace(math.log(HY_TARGET) / HY_SLOW_DECAY, math.log(HY_TARGET) / HY_FAST_DECAY,
                                HY_WIDTH))
    offset = np.abs(np.arange(seq) - seq // 2) / (seq // 2)
    window = np.exp(-offset[:, None] * deltas[None, :])
    feats = np.pad(feats, ((0, 0), (0, LANES - HY_EMB)))
    return jnp.asarray(feats, dtype=F32), jnp.asarray(window, dtype=F32)


def _hfilter_kernel(seq, feats_ref, win_ref, w1_ref, b1_ref, w2_ref, b2_ref, w3_ref, fr_ref, fwd_hi_ref,
                    fwd_lo_ref, ha_ref, hb_ref, hd_ref):
    fr = fr_ref[...]
    hid = jnp.sin(fr * (_mm_f32(feats_ref[...], w1_ref[...]) + b1_ref[...]))
    hid = jnp.sin(fr * (_mm_f32(hid, w2_ref[...]) + b2_ref[...]))
    filt = _mm_f32(hid, w3_ref[...]) * win_ref[...]
    filt = filt / (jnp.sum(jnp.abs(filt), axis=0, keepdims=True) + 1e-6)
    filt_hi = filt.astype(BF16)
    filt_lo = (filt - filt_hi.astype(F32)).astype(BF16)
    fwd_hi = fwd_hi_ref[...]
    spec = (jnp.dot(fwd_hi, filt_hi, preferred_element_type=F32)
            + jnp.dot(fwd_hi, filt_lo, preferred_element_type=F32)
            + jnp.dot(fwd_lo_ref[...], filt_hi, preferred_element_type=F32))
    h_re = spec[:seq, :]
    h_im = spec[seq:, :]
    first = _iota(h_re.shape, 0) == 0
    ha_ref[...] = h_re
    hb_ref[...] = jnp.where(first, 0.0, h_im)
    hd_ref[...] = jnp.where(first, h_im, h_re)


def _hfilter_call(seq, w1, b1, w2, b2, w3, freq, fwd_hi, fwd_lo):
    feats, window = _filter_feats(seq)
    w1p = jnp.pad(w1, ((0, LANES - HY_EMB), (0, 0)))
    full = lambda a: pl.BlockSpec(a.shape, lambda i: (0,) * a.ndim)
    args = [feats, window, w1p, b1.reshape(1, -1), w2, b2.reshape(1, -1), w3, freq.reshape(1, -1), fwd_hi,
            fwd_lo]
    out = jax.ShapeDtypeStruct((seq, HY_WIDTH), F32)
    return pl.pallas_call(
        functools.partial(_hfilter_kernel, seq),
        grid=(1,),
        in_specs=[full(a) for a in args],
        out_specs=[pl.BlockSpec((seq, HY_WIDTH), lambda i: (0, 0))] * 3,
        out_shape=[out, out, out],
        compiler_params=_cparams(1),
        name="hfilter",
    )(*args)


def _hyena_kernel(seq, reqs, pre_conv, *refs):
    if pre_conv:
        x0_ref, uu_ref, ha_ref, hb_ref, hd_ref, skip_ref, fwd_ref, inv_ref, o_ref = refs
    else:
        (x0_ref, x1_ref, v_ref, c0_ref, c1_ref, c2_ref, b0_ref, b1_ref, b2_ref,
         ha_ref, hb_ref, hd_ref, skip_ref, fwd_ref, inv_ref, o_ref) = refs
        row = _iota((seq, HY_CH_BLOCK), 0)
        first = row == 0
        last = row == seq - 1
    hb = hb_ref[...]
    for r in range(reqs):
        rows = slice(r * seq, (r + 1) * seq)
        if pre_conv:
            x0 = x0_ref[rows, :].astype(F32)
            uu = uu_ref[rows, :].astype(F32)
        else:
            conv = lambda x_ref, w_ref, b_ref: (_short_conv3(x_ref[rows, :].astype(F32), w_ref[...], first, last)
                                                + b_ref[...])
            x0 = conv(x0_ref, c0_ref, b0_ref)
            uu = conv(x1_ref, c1_ref, b1_ref) * conv(v_ref, c2_ref, b2_ref)
        spec = jnp.dot(fwd_ref[...], uu.astype(BF16), preferred_element_type=F32)
        u_re = spec[:seq, :]
        u_im = spec[seq:, :]
        y_re = u_re * ha_ref[...] - u_im * hb
        y_im = u_re * hb + u_im * hd_ref[...]
        y = jnp.concatenate([y_re, y_im], axis=0).astype(BF16)
        cv = jnp.dot(inv_ref[...], y, preferred_element_type=F32)
        o_ref[rows, :] = (x0 * (cv + uu * skip_ref[...])).astype(o_ref.dtype)


def _hyena_call(hy, pre_conv, conv_w, conv_b, ha, hb, hd, skip, fwd, inv, bsz, seq, act_dt):
    cb = HY_CH_BLOCK
    nblk = HY_WIDTH // cb
    reqs = max(1, min(bsz, HY_ROWS_PER_STEP // seq))
    assert bsz % reqs == 0
    tok = lambda off: pl.BlockSpec((reqs * seq, cb), lambda b, j: (b, off + j))
    cw = lambda off: pl.BlockSpec((3, cb), lambda b, j: (0, off + j))
    bias = lambda off: pl.BlockSpec((1, cb), lambda b, j: (0, off + j))
    ch = pl.BlockSpec((seq, cb), lambda b, j: (0, j))
    const = lambda a: pl.BlockSpec(a.shape, lambda b, j: (0, 0), pipeline_mode=pl.Buffered(1))
    tail_specs = [ch, ch, ch, pl.BlockSpec((1, cb), lambda b, j: (0, j)), const(fwd), const(inv)]
    tail_args = [ha, hb, hd, skip.reshape(1, -1), fwd, inv]
    if pre_conv:
        in_specs = [tok(0), tok(nblk)] + tail_specs
        args = [hy, hy] + tail_args
    else:
        conv_b2 = conv_b.reshape(1, -1)
        in_specs = [tok(0), tok(nblk), tok(2 * nblk), cw(0), cw(nblk), cw(2 * nblk),
                    bias(0), bias(nblk), bias(2 * nblk)] + tail_specs
        args = [hy, hy, hy, conv_w, conv_w, conv_w, conv_b2, conv_b2, conv_b2] + tail_args
    return pl.pallas_call(
        functools.partial(_hyena_kernel, seq, reqs, pre_conv),
        grid=(bsz // reqs, nblk),
        in_specs=in_specs,
        out_specs=tok(0),
        out_shape=jax.ShapeDtypeStruct((bsz * seq, HY_WIDTH), act_dt),
        compiler_params=_cparams(2),
        name="hyena",
    )(*args)


def _mix_kernel(has_pos, *refs):
    refs = list(refs)
    x_ref = refs.pop(0)
    pos_ref = refs.pop(0) if has_pos else None
    (oa_ref, ob_ref, g_ref, mod_ref, wua, wub, wout, l1g, l1b, wr_hi, wr_lo, x1_o, u2_o, p_o) = refs
    d = D_MODEL
    tm = x_ref.shape[0]
    sub = min(tm, MIX_SUB_ROWS)
    blocks = [slice(r0, r0 + sub) for r0 in range(0, tm, sub)]
    g1 = mod_ref[0, 2]
    sh2 = mod_ref[0, 3]
    sc2 = mod_ref[0, 4]

    mixed = []
    for rows in blocks:
        gates = g_ref[rows, :].astype(F32)
        ga = _sigmoid(gates[:, :d])
        gb = _sigmoid(gates[:, d:])
        up = (ga * jnp.dot(oa_ref[rows, :], wua[...], preferred_element_type=F32)
              + gb * jnp.dot(ob_ref[rows, :], wub[...], preferred_element_type=F32))
        mixed.append(jnp.dot(up.astype(BF16), wout[...], preferred_element_type=F32))

    for rows, mx in zip(blocks, mixed):
        x = x_ref[rows, :]
        if has_pos:
            x = x + pos_ref[rows, :]
        x1 = _ln(ALPHA * x + g1 * mx) * l1g[...] + l1b[...]
        x1_o[rows, :] = x1.astype(x1_o.dtype)
        u2 = _ln(x1) * (1.0 + sc2) + sh2
        u2_hi = u2.astype(BF16)
        u2_o[rows, :] = u2_hi
        u2_lo = (u2 - u2_hi.astype(F32)).astype(BF16)
        logits = (jnp.dot(u2_hi, wr_hi[...], preferred_element_type=F32)
                  + jnp.dot(u2_hi, wr_lo[...], preferred_element_type=F32)
                  + jnp.dot(u2_lo, wr_hi[...], preferred_element_type=F32))
        lane = _iota(logits.shape, 1)
        logits = jnp.where(lane < N_EXPERTS, logits, -jnp.inf)
        m = jnp.max(logits, axis=-1, keepdims=True)
        e = jnp.exp(logits - m)
        p_o[rows, :] = e / jnp.sum(e, axis=-1, keepdims=True)


def _mix_call(x2d, pos, o_a, o_b, gates, mod4, mod_row, wua, wub, wout, l1g, l1b, wr_hi, wr_lo):
    n, d = x2d.shape
    tm = min(MIX_TILE, max(n // MIX_MIN_STEPS, MIX_SUB_ROWS))
    assert n % tm == 0
    row = lambda i: (i, 0)
    const = lambda i: (0, 0)
    in_specs = [pl.BlockSpec((tm, d), row)]
    args = [x2d]
    if pos is not None:
        tiles = pos.shape[0] // tm
        in_specs.append(pl.BlockSpec((tm, d), lambda i: (i % tiles, 0)))
        args.append(pos)
    in_specs += [pl.BlockSpec((tm, DN_WIDTH), row), pl.BlockSpec((tm, HY_WIDTH), row),
                 pl.BlockSpec((tm, 2 * d), row),
                 _mod_spec(lambda i: mod_row(i * tm))]
    args += [o_a, o_b, gates, mod4]
    for w in (wua, wub, wout, l1g, l1b, wr_hi, wr_lo):
        in_specs.append(pl.BlockSpec(w.shape, const, pipeline_mode=pl.Buffered(1)))
        args.append(w)
    return pl.pallas_call(
        functools.partial(_mix_kernel, pos is not None),
        grid=(n // tm,),
        in_specs=in_specs,
        out_specs=[pl.BlockSpec((tm, d), row), pl.BlockSpec((tm, d), row), pl.BlockSpec((tm, LANES), row)],
        out_shape=[jax.ShapeDtypeStruct((n, d), ACT_DT), jax.ShapeDtypeStruct((n, d), BF16),
                   jax.ShapeDtypeStruct((n, LANES), F32)],
        compiler_params=_cparams(1),
        name="mix",
    )(*args)


def _route_kernel(seq, cap, reqs, p_ref, u_ref, tri_ref, xs_ref, slot_ref, pt_ref, g_s):
    e_n = N_EXPERTS
    tri = tri_ref[...]
    jrow = _iota((cap, seq), 0).astype(F32)
    if seq <= RANK_COUNT_MAX_SEQ:
        earlier = _iota((seq, seq), 0) < _iota((seq, seq), 1)
    for r in range(reqs):
        rows = slice(r * seq, (r + 1) * seq)
        p = p_ref[rows, :]
        pt = p.T[:e_n, :]

        if seq <= RANK_COUNT_MAX_SEQ:
            ranks = []
            for e in range(e_n):
                pc = p[:, e:e + 1]
                pr = pt[e:e + 1, :]
                beats = (pc > pr) | (earlier & (pc == pr))
                ranks.append(jnp.sum(jnp.where(beats, 1.0, 0.0), axis=0, keepdims=True))
            sel = jnp.concatenate(ranks, axis=0) < cap
        else:
            def search(i, cur):
                cand = cur | (1 << (30 - i))
                cand_f = pltpu.bitcast(cand, F32)
                cnt = jnp.sum(jnp.where(pt >= cand_f, 1.0, 0.0), axis=1, keepdims=True)
                return jnp.where((cnt >= cap) & (cand >= MIN_NORMAL_F32_BITS), cand, cur)

            thr = pltpu.bitcast(lax.fori_loop(0, 31, search, jnp.zeros((e_n, 1), jnp.int32)), F32)
            gt = pt > thr
            eq = pt == thr
            n_gt = jnp.sum(jnp.where(gt, 1.0, 0.0), axis=1, keepdims=True)
            eq_rank = jnp.dot(jnp.where(eq, 1.0, 0.0).astype(BF16), tri, preferred_element_type=F32)
            sel = gt | (eq & (eq_rank < cap - n_gt))
        pos = jnp.dot(jnp.where(sel, 1.0, 0.0).astype(BF16), tri, preferred_element_type=F32)
        slot = jnp.where(sel, pos, -1.0)
        slot_ref[r] = slot
        pt_ref[r] = pt
        for e in range(e_n):
            g_s[r, e * cap:(e + 1) * cap, :] = jnp.where(slot[e:e + 1, :] == jrow, 1.0, 0.0).astype(BF16)
        xs = jnp.dot(g_s[r], u_ref[rows, :], preferred_element_type=F32)
        for e in range(e_n):
            xs_ref[e, r * cap:(r + 1) * cap, :] = xs[e * cap:(e + 1) * cap, :].astype(xs_ref.dtype)


def _route_call(probs, u2, bsz, seq):
    cap = EC_FACTOR * seq // N_EXPERTS
    reqs = max(1, min(bsz, ROUTE_ROWS_PER_STEP // seq))
    assert bsz % reqs == 0
    tri = jnp.asarray(np.triu(np.ones((seq, seq), np.float32), 1), dtype=BF16)
    per_req = pl.BlockSpec((reqs, N_EXPERTS, seq), lambda b: (b, 0, 0))
    return pl.pallas_call(
        functools.partial(_route_kernel, seq, cap, reqs),
        grid=(bsz // reqs,),
        in_specs=[pl.BlockSpec((reqs * seq, LANES), lambda b: (b, 0)),
                  pl.BlockSpec((reqs * seq, D_MODEL), lambda b: (b, 0)),
                  pl.BlockSpec((seq, seq), lambda b: (0, 0), pipeline_mode=pl.Buffered(1))],
        out_specs=[pl.BlockSpec((N_EXPERTS, reqs * cap, D_MODEL), lambda b: (0, b, 0)), per_req, per_req],
        out_shape=[jax.ShapeDtypeStruct((N_EXPERTS, bsz * cap, D_MODEL), BF16),
                   jax.ShapeDtypeStruct((bsz, N_EXPERTS, seq), F32),
                   jax.ShapeDtypeStruct((bsz, N_EXPERTS, seq), F32)],
        scratch_shapes=[pltpu.VMEM((reqs, N_EXPERTS * cap, seq), BF16)],
        compiler_params=_cparams(1),
        name="route",
    )(probs, u2, tri)


def _expert_kernel(xc_ref, xd_ref, wg_ref, wu_ref, wd_ref, yc_ref, yd_ref):
    wg = wg_ref[0].astype(BF16)
    wu = wu_ref[0].astype(BF16)
    wd = wd_ref[0].astype(BF16)
    for x_ref, y_ref in ((xc_ref, yc_ref), (xd_ref, yd_ref)):
        rows = x_ref.shape[1]
        rb = min(rows, EXPERT_ROW_BLOCK)
        for r0 in range(0, rows, rb):
            x = x_ref[0, r0:r0 + rb, :]
            hg = jnp.dot(x, wg, preferred_element_type=F32)
            hu = jnp.dot(x, wu, preferred_element_type=F32)
            y = jnp.dot((_silu(hg) * hu).astype(BF16), wd, preferred_element_type=F32)
            y_ref[0, r0:r0 + rb, :] = y.astype(y_ref.dtype)


def _expert_call(xs_c, xs_d, w_gate, w_up, w_down):
    e_n, rc, d = xs_c.shape
    rd = xs_d.shape[1]
    ff = w_gate.shape[2]
    return pl.pallas_call(
        _expert_kernel,
        grid=(e_n,),
        in_specs=[pl.BlockSpec((1, rc, d), lambda e: (e, 0, 0)),
                  pl.BlockSpec((1, rd, d), lambda e: (e, 0, 0)),
                  pl.BlockSpec((1, d, ff), lambda e: (e, 0, 0)),
                  pl.BlockSpec((1, d, ff), lambda e: (e, 0, 0)),
                  pl.BlockSpec((1, ff, d), lambda e: (e, 0, 0))],
        out_specs=[pl.BlockSpec((1, rc, d), lambda e: (e, 0, 0)),
                   pl.BlockSpec((1, rd, d), lambda e: (e, 0, 0))],
        out_shape=[jax.ShapeDtypeStruct((e_n, rc, d), BF16), jax.ShapeDtypeStruct((e_n, rd, d), BF16)],
        compiler_params=_cparams(1),
        name="experts",
    )(xs_c, xs_d, w_gate, w_up, w_down)


def _final_kernel(seq, cap, reqs, x1_ref, y_ref, slot_ref, pt_ref, mod_ref, l2g, l2b, o_ref, w_s):
    jrow = _iota((cap, seq), 0).astype(F32)
    g2 = mod_ref[0, 5]
    for r in range(reqs):
        slot = slot_ref[r]
        pt = pt_ref[r]
        for e in range(N_EXPERTS):
            w_s[r, e * cap:(e + 1) * cap, :] = jnp.where(slot[e:e + 1, :] == jrow, pt[e:e + 1, :],
                                                         0.0).astype(BF16)
        y = y_ref[:, r * cap:(r + 1) * cap, :].reshape(N_EXPERTS * cap, y_ref.shape[-1])
        ffn = lax.dot_general(w_s[r], y, (((0,), (0,)), ((), ())), preferred_element_type=F32)
        rows = slice(r * seq, (r + 1) * seq)
        o_ref[rows, :] = _ln(ALPHA * x1_ref[rows, :].astype(F32) + g2 * ffn) * l2g[...] + l2b[...]


def _final_call(x1, ys, slot, pt, mod4, mod_row, l2g, l2b, bsz, seq, reqs):
    cap = EC_FACTOR * seq // N_EXPERTS
    d = D_MODEL
    per_req = pl.BlockSpec((reqs, N_EXPERTS, seq), lambda b: (b, 0, 0))
    return pl.pallas_call(
        functools.partial(_final_kernel, seq, cap, reqs),
        grid=(bsz // reqs,),
        in_specs=[pl.BlockSpec((reqs * seq, d), lambda b: (b, 0)),
                  pl.BlockSpec((N_EXPERTS, reqs * cap, d), lambda b: (0, b, 0)),
                  per_req, per_req,
                  _mod_spec(lambda b: mod_row(b * reqs * seq)),
                  pl.BlockSpec((1, d), lambda b: (0, 0)),
                  pl.BlockSpec((1, d), lambda b: (0, 0))],
        out_specs=pl.BlockSpec((reqs * seq, d), lambda b: (b, 0)),
        out_shape=jax.ShapeDtypeStruct((bsz * seq, d), F32),
        scratch_shapes=[pltpu.VMEM((reqs, N_EXPERTS * cap, seq), BF16)],
        compiler_params=_cparams(1),
        name="final",
    )(x1, ys, slot, pt, mod4, l2g, l2b)


def _grid_pos_embed(rows, dim):
    r = np.repeat(np.arange(rows), GRID_W)
    col = np.tile(np.arange(GRID_W), rows)
    quarter = dim // 4
    omega = 1.0 / (10000.0 ** (np.arange(quarter) / quarter))

    def enc(p):
        ang = p[:, None] * omega[None, :]
        return np.concatenate([np.sin(ang), np.cos(ang)], -1)

    return jnp.asarray(np.concatenate([enc(r), enc(col)], -1), dtype=F32)


def kernel(x_prompt, x_sample, state_delta_fwd, state_delta_bwd, c, c_ctx, w_mod, b_mod, w_in, conv_a_w, a_log, dt_bias, gnorm_w, conv_b_w, conv_b_b, hy_w1, hy_b1, hy_w2, hy_b2, hy_w3, hy_freq, hy_bias, w_up_a, w_up_b, w_out, ln1_g, ln1_b, w_router, w_e_gate, w_e_up, w_e_down, ln2_g, ln2_b):
    d = D_MODEL
    n_ctx, l_ctx, _ = x_prompt.shape
    n_dec, l_dec, _ = x_sample.shape
    lyr = 0

    cond_t = jnp.pad(jnp.concatenate([c_ctx[None, :], c], 0).T, ((0, 0), (0, 7 - n_dec)))
    mod4 = _mod_call(cond_t, 1 + n_dec, w_mod[lyr], b_mod[lyr]).reshape(8, N_MOD, 1, d)

    in_w = w_in[lyr].astype(BF16)

    prm = jnp.pad(jnp.stack([a_log[lyr].reshape(-1), dt_bias[lyr].reshape(-1)]),
                  ((0, 6), (2 * DN_HEADS, LANES - 4 * DN_HEADS)))
    gn = gnorm_w[lyr].reshape(1, -1)

    wua = w_up_a[lyr].astype(BF16)
    wub = w_up_b[lyr].astype(BF16)
    wout = w_out[lyr].astype(BF16)
    wr = jnp.pad(w_router[lyr], ((0, 0), (0, LANES - N_EXPERTS)))
    wr_hi = wr.astype(BF16)
    wr_lo = (wr - wr_hi.astype(F32)).astype(BF16)
    l1g, l1b = ln1_g[lyr].reshape(1, -1), ln1_b[lyr].reshape(1, -1)
    l2g, l2b = ln2_g[lyr].reshape(1, -1), ln2_b[lyr].reshape(1, -1)

    def front(x2d, pos, bsz, seq, mod_row, s0f, s0b):
        (qkv, z, ba, hy, gates), pre_conv = _inproj_call(x2d, pos, mod4, mod_row, in_w, conv_a_w[lyr],
                                                          conv_b_w[lyr], conv_b_b[lyr], seq, ACT_DT)
        reqs = DN_REQS_PER_STEP if seq == DN_CHUNK and bsz % DN_REQS_PER_STEP == 0 else 1
        o_a, s_f, s_b = _deltanet_call(qkv, z, ba, conv_a_w[lyr], prm, gn, s0f, s0b, bsz, seq, reqs,
                                       pre_conv, ACT_DT)
        fwd, inv = (jnp.asarray(m) for m in _dft_mats(seq))
        fwd_hi = fwd.astype(BF16)
        fwd_lo = (fwd - fwd_hi.astype(F32)).astype(BF16)
        ha, hb, hd = _hfilter_call(seq, hy_w1[lyr], hy_b1[lyr], hy_w2[lyr], hy_b2[lyr], hy_w3[lyr],
                                   hy_freq[lyr], fwd_hi, fwd_lo)
        o_h = _hyena_call(hy, pre_conv, conv_b_w[lyr], conv_b_b[lyr], ha, hb, hd, hy_bias[lyr],
                          fwd_hi, inv.astype(BF16), bsz, seq, ACT_DT)
        x1, u2, probs = _mix_call(x2d, pos, o_a, o_h, gates, mod4, mod_row, wua, wub, wout, l1g, l1b,
                                   wr_hi, wr_lo)
        xs, slot, pt = _route_call(probs, u2, bsz, seq)
        return x1, xs, slot, pt, s_f, s_b

    row_ctx = lambda tok: 0
    row_dec = lambda tok: 1 + tok // l_dec

    xc = x_prompt.reshape(n_ctx * l_ctx, d)
    xd = x_sample.reshape(n_dec * l_dec, d)
    pos = _grid_pos_embed(l_dec // GRID_W, d)

    x1c, xsc, gc, ptc, s_f, s_b = front(xc, None, n_ctx, l_ctx, row_ctx, None, None)
    x1d, xsd, gd, ptd, _, _ = front(xd, pos, n_dec, l_dec, row_dec,
                                    state_delta_fwd[:, lyr], state_delta_bwd[:, lyr])
    yc, yd = _expert_call(xsc, xsd, w_e_gate[lyr], w_e_up[lyr], w_e_down[lyr])
    reqs_c = FINAL_CTX_REQS_PER_STEP if n_ctx % FINAL_CTX_REQS_PER_STEP == 0 else 1
    y_prompt = _final_call(x1c, yc, gc, ptc, mod4, row_ctx, l2g, l2b, n_ctx, l_ctx, reqs_c)
    y_sample = _final_call(x1d, yd, gd, ptd, mod4, row_dec, l2g, l2b, n_dec, l_dec, 1)

    return (y_prompt.reshape(n_ctx, l_ctx, d), y_sample.reshape(n_dec, l_dec, d),
            s_f[:, None], s_b[:, None])
```

```python
import functools
import math

import jax
import jax.numpy as jnp
import numpy as np
from jax import lax
from jax.experimental import pallas as pl
from jax.experimental.pallas import tpu as pltpu

F32 = jnp.float32
BF16 = jnp.bfloat16
HIGHEST = lax.Precision.HIGHEST

D_MODEL = 1024
DEPTH = 1
GRID_W = 64
DN_HEADS = 4
DN_DK = 128
DN_DV = 128
DN_WIDTH = DN_HEADS * DN_DV
HY_WIDTH = D_MODEL // 2
HY_EMB = 33
HY_FFN = 64
HY_TARGET = 1e-2
HY_FAST_DECAY = 0.3
HY_SLOW_DECAY = 1.5
N_EXPERTS = 16
EC_FACTOR = 2
N_MOD = 6
ALPHA = (2 * DEPTH) ** 0.25
LN_EPS = 1e-5
RMS_EPS = 1e-6
QKV_WIDTH = 2 * DN_HEADS * DN_DK + DN_WIDTH

LANES = 128
INPROJ_TILE = 1024
INPROJ_BLOCK_ROWS = 512
MIX_TILE = 1024
MIX_SUB_ROWS = 256
MIX_MIN_STEPS = 4
MOD_COL_BLOCK = 1024
DN_CHUNK = 256
INV_BASE = 8
DN_REQS_PER_STEP = 2
HY_CH_BLOCK = 512
HY_ROWS_PER_STEP = 1024
FINAL_CTX_REQS_PER_STEP = 4
EXPERT_ROW_BLOCK = 512
VMEM_LIMIT = 56 * 1024 * 1024
MIN_NORMAL_F32_BITS = 0x00800000
ACT_DT = BF16
ROUTE_ROWS_PER_STEP = 1024
RANK_COUNT_MAX_SEQ = 256


def _cparams(n_axes):
    return pltpu.CompilerParams(dimension_semantics=("arbitrary",) * n_axes,
                                vmem_limit_bytes=VMEM_LIMIT)


def _bmm(a, b):
    return jnp.einsum("bmk,bkn->bmn", a.astype(BF16), b.astype(BF16), preferred_element_type=F32)


def _bmm_nt(a, b):
    return jnp.einsum("bmk,bnk->bmn", a.astype(BF16), b.astype(BF16), preferred_element_type=F32)


def _bmm_tn(a, b):
    return jnp.einsum("bkm,bkn->bmn", a.astype(BF16), b.astype(BF16), preferred_element_type=F32)


def _mm_f32(a, b):
    return jnp.dot(a, b, precision=HIGHEST, preferred_element_type=F32)


def _sigmoid(x):
    return 0.5 * jnp.tanh(0.5 * x) + 0.5


def _silu(x):
    return x * _sigmoid(x)


def _softplus(x):
    return jnp.maximum(x, 0.0) + jnp.log1p(jnp.exp(-jnp.abs(x)))


def _ln(x):
    mu = jnp.mean(x, axis=-1, keepdims=True)
    xc = x - mu
    var = jnp.mean(xc * xc, axis=-1, keepdims=True)
    return xc * lax.rsqrt(var + LN_EPS)


def _l2norm(x, scale=1.0):
    return x * (lax.rsqrt(jnp.sum(x * x, axis=-1, keepdims=True) + 1e-6) * scale)


def _iota(shape, dim):
    return lax.broadcasted_iota(jnp.int32, shape, dim)


def _short_conv3(x, w, first, last):
    xm = jnp.where(first, 0.0, pltpu.roll(x, 1, 0))
    xp = jnp.where(last, 0.0, pltpu.roll(x, x.shape[0] - 1, 0))
    return w[0:1, :] * xm + w[1:2, :] * x + w[2:3, :] * xp


def _mod_kernel(n_rows, ct_ref, w_ref, b_ref, o_ref):
    s = _silu(ct_ref[...])
    w = w_ref[...]
    b = b_ref[...]
    o_ref[...] = jnp.zeros(o_ref.shape, F32)
    for r in range(n_rows):
        o_ref[r:r + 1, :] = jnp.sum(w * s[:, r:r + 1], axis=0, keepdims=True) + b


def _mod_call(cond_t, n_rows, w_mod, b_mod):
    d = D_MODEL
    tn = MOD_COL_BLOCK
    return pl.pallas_call(
        functools.partial(_mod_kernel, n_rows),
        grid=(N_MOD * d // tn,),
        in_specs=[pl.BlockSpec((d, 8), lambda j: (0, 0)),
                  pl.BlockSpec((d, tn), lambda j: (0, j)),
                  pl.BlockSpec((1, tn), lambda j: (0, j))],
        out_specs=pl.BlockSpec((8, tn), lambda j: (0, j)),
        out_shape=jax.ShapeDtypeStruct((8, N_MOD * d), F32),
        compiler_params=_cparams(1),
        name="mod",
    )(cond_t, w_mod, b_mod.reshape(1, -1))


def _mod_spec(row_of_step):
    return pl.BlockSpec((1, N_MOD, 1, D_MODEL), lambda *idx: (row_of_step(*idx), 0, 0, 0))


def _inproj_kernel(has_pos, conv_seq, *refs):
    refs = list(refs)
    x_ref = refs.pop(0)
    pos_ref = refs.pop(0) if has_pos else None
    mod_ref, w_ref = refs.pop(0), refs.pop(0)
    if conv_seq:
        cwa_ref, cwb_ref, cbb_ref = refs.pop(0), refs.pop(0), refs.pop(0)
    qkv_o, z_o, ba_o, hy_o, g_o = refs
    sh1 = mod_ref[0, 0]
    sc1 = mod_ref[0, 1]
    o_z = QKV_WIDTH
    o_ba = o_z + DN_WIDTH
    n_ba = 4 * DN_HEADS
    o_g = n_ba + 3 * HY_WIDTH
    lanes = lambda j: slice(j * LANES, (j + 1) * LANES)
    tm = x_ref.shape[0]
    sub = min(tm, INPROJ_BLOCK_ROWS)
    blocks = [slice(r0, r0 + sub) for r0 in range(0, tm, sub)]

    if conv_seq:
        row = _iota((sub, LANES), 0) % conv_seq
        first = row == 0
        last = row == conv_seq - 1

    def project(rows):
        x = x_ref[rows, :]
        if has_pos:
            x = x + pos_ref[rows, :]
        u = (_ln(x) * (1.0 + sc1) + sh1).astype(BF16)
        return (jnp.dot(u, w_ref[:, :o_ba], preferred_element_type=F32),
                jnp.dot(u, w_ref[:, o_ba:], preferred_element_type=F32))

    def epilogue(rows, head, tail):
        z_o[rows, :] = head[:, o_z:].astype(z_o.dtype)
        ba_o[rows, :] = tail[:, :LANES]
        g_o[rows, :] = tail[:, o_g:].astype(g_o.dtype)
        hy = tail[:, n_ba:o_g]
        if not conv_seq:
            qkv_o[rows, :] = head[:, :o_z].astype(qkv_o.dtype)
            hy_o[rows, :] = hy.astype(hy_o.dtype)
            return
        for j in range(QKV_WIDTH // LANES):
            y = _silu(_short_conv3(head[:, lanes(j)], cwa_ref[:, lanes(j)], first, last))
            if j < 2 * DN_HEADS:
                y = _l2norm(y, DN_DK ** -0.5 if j < DN_HEADS else 1.0)
            qkv_o[rows, lanes(j)] = y.astype(qkv_o.dtype)
        nj = HY_WIDTH // LANES
        for j in range(nj):
            parts = [_short_conv3(hy[:, lanes(p * nj + j)], cwb_ref[:, lanes(p * nj + j)], first, last)
                     + cbb_ref[:, lanes(p * nj + j)] for p in range(3)]
            hy_o[rows, lanes(j)] = parts[0].astype(hy_o.dtype)
            hy_o[rows, lanes(nj + j)] = (parts[1] * parts[2]).astype(hy_o.dtype)

    pending = None
    for rows in blocks:
        cur = (rows, *project(rows))
        if pending is not None:
            epilogue(*pending)
        pending = cur
    epilogue(*pending)


def _inproj_call(x2d, pos, mod4, mod_row, w, conv_a_w, conv_b_w, conv_b_b, seq, act_dt):
    n, d = x2d.shape
    fuse = seq <= INPROJ_BLOCK_ROWS and INPROJ_TILE % seq == 0
    tm = min(INPROJ_TILE if fuse else INPROJ_BLOCK_ROWS, n)
    assert n % tm == 0 and (not fuse or tm % seq == 0)
    conv_seq = seq if fuse else None
    row = lambda i: (i, 0)
    const = lambda i: (0, 0)
    in_specs = [pl.BlockSpec((tm, d), row)]
    args = [x2d]
    if pos is not None:
        tiles = pos.shape[0] // tm
        in_specs.append(pl.BlockSpec((tm, d), lambda i: (i % tiles, 0)))
        args.append(pos)
    in_specs.append(_mod_spec(lambda i: mod_row(i * tm)))
    args.append(mod4)
    in_specs.append(pl.BlockSpec(w.shape, const, pipeline_mode=pl.Buffered(1)))
    args.append(w)
    if conv_seq:
        for a in (conv_a_w, conv_b_w, conv_b_b.reshape(1, -1)):
            in_specs.append(pl.BlockSpec(a.shape, const))
            args.append(a)
    hy_width = (2 if conv_seq else 3) * HY_WIDTH
    widths = (QKV_WIDTH, DN_WIDTH, LANES, hy_width, 2 * D_MODEL)
    dts = (act_dt, act_dt, F32, act_dt, act_dt)
    outs = pl.pallas_call(
        functools.partial(_inproj_kernel, pos is not None, conv_seq),
        grid=(n // tm,),
        in_specs=in_specs,
        out_specs=[pl.BlockSpec((tm, w), row) for w in widths],
        out_shape=[jax.ShapeDtypeStruct((n, w), dt) for w, dt in zip(widths, dts)],
        compiler_params=_cparams(1),
        name="inproj",
    )(*args)
    return outs, conv_seq is not None


def _inv_unit_tri(lm, ri, ci, block):
    c = lm.shape[-1]

    def same_block(h):
        sh = h.bit_length() - 1
        return (ri >> sh) == (ci >> sh)

    def unfold(f, h):
        return jnp.where(same_block(h), jnp.concatenate([f] * (c // h), axis=1), 0.0)

    base = INV_BASE
    n_nat = jnp.where(same_block(base), -lm, 0.0)
    n_fold = n_nat[:, 0:base, :]
    for i in range(1, c // base):
        n_fold = n_fold + n_nat[:, i * base:(i + 1) * base, :]
    eye_fold = jnp.where((_iota((base, c), 1) & (base - 1)) == _iota((base, c), 0), 1.0, 0.0)
    t = eye_fold + n_fold
    p_fold, p_nat = n_fold, n_nat
    span = 2
    while span < base:
        p_fold = _bmm(p_fold, p_nat)
        p_nat = unfold(p_fold, base)
        t = t + _bmm(t, p_nat)
        span *= 2
    s = base
    while s < block:
        sh = s.bit_length() - 1
        t_nat = unfold(t, s)
        even = ((_iota((s, c), 1) >> sh) & 1) == 0
        t2 = jnp.concatenate([jnp.where(even, t, 0.0), jnp.where(even, 0.0, t)], axis=1)
        lo = jnp.where(same_block(2 * s) & ~same_block(s), lm, 0.0)
        t = t2 - _bmm(_bmm(t2, lo), t_nat)
        s *= 2
    return t


def _dn_chunk(q, k, v, gc_col, gc_row, beta, e_col, e_rest, e_all, s_prev, same_chunk, ri, ci):
    c = DN_CHUNK
    nh = q.shape[0] // 2
    half = c // 2
    if same_chunk:
        gram = _bmm_nt(k[:nh], k[:nh])
        qk = _bmm_nt(q[:nh], k[:nh])
        gram = jnp.concatenate([gram, gram], 0)
        qk = jnp.concatenate([qk, qk], 0)
    else:
        gram = _bmm_nt(k, k)
        qk = _bmm_nt(q, k)
    diff = gc_col - gc_row
    incl = jnp.concatenate([jnp.broadcast_to(ri >= ci, (nh, c, c)), jnp.broadcast_to(ri <= ci, (nh, c, c))], 0)
    strict = incl & (ri != ci)
    decay = jnp.exp(jnp.where(incl, diff, -jnp.inf))
    lm = jnp.where(strict, beta * gram * decay, 0.0)
    a_intra = qk * decay
    t = _inv_unit_tri(lm, ri, ci, half)
    t1 = t[:, :, :half]
    t2 = t[:, :, half:]
    rhs = jnp.concatenate([v * beta, k * (beta * e_col)], axis=2)
    y1 = _bmm(t1, rhs[:, :half, :])
    y2 = _bmm(t2, rhs[:, half:, :])
    c_f = _bmm(lm[:nh, half:, :half], y1[:nh])
    c_b = _bmm(lm[nh:, :half, half:], y2[nh:])
    c_f = _bmm(t2[:nh], c_f)
    c_b = _bmm(t1[nh:], c_b)
    sol = jnp.concatenate([jnp.concatenate([y1[:nh], y2[:nh] - c_f], 1),
                           jnp.concatenate([y1[nh:] - c_b, y2[nh:]], 1)], 0)
    u = sol[:, :, :DN_DV]
    w = sol[:, :, DN_DV:]
    ks = k * e_rest
    if s_prev is None:
        v_new = u
        o = _bmm(a_intra, v_new)
        s_new = _bmm_tn(ks, v_new)
    else:
        v_new = u - _bmm(w, s_prev)
        o = _bmm(q * e_col, s_prev) + _bmm(a_intra, v_new)
        s_new = s_prev * e_all + _bmm_tn(ks, v_new)
    return o, s_new


def _deltanet_kernel(seq, reqs, zero_init, pre_conv, *refs):
    qkv_ref, cw_ref, z_ref, ba_ref, prm_ref, gn_ref = refs[:6]
    refs = refs[6:]
    if not zero_init:
        s0f_ref, s0b_ref = refs[:2]
        refs = refs[2:]
    o_ref, sf_ref, sb_ref, q_s, k_s, v_s, g_s, b_s, o_s, st_s = refs
    c = DN_CHUNK
    n = seq // c
    nh = DN_HEADS
    nf = reqs * nh
    lanes = lambda j: slice(j * LANES, (j + 1) * LANES)

    if not pre_conv:
        row = _iota((seq, LANES), 0)
        first_row = row == 0
        last_row = row == seq - 1
    for r in range(reqs):
        rows = slice(r * seq, (r + 1) * seq)

        def qkv_part(j):
            x = qkv_ref[rows, lanes(j)].astype(F32)
            return x if pre_conv else _silu(_short_conv3(x, cw_ref[:, lanes(j)], first_row, last_row))

        for h in range(nh):
            q = qkv_part(h)
            k = qkv_part(nh + h)
            q_s[r * nh + h] = q if pre_conv else _l2norm(q, DN_DK ** -0.5)
            k_s[r * nh + h] = k if pre_conv else _l2norm(k)
            v_s[r * nh + h] = qkv_part(2 * nh + h)

    ba = ba_ref[...]
    b_s[...] = _sigmoid(ba)
    g_s[...] = -jnp.exp(prm_ref[0:1, :]) * _softplus(ba + prm_ref[1:2, :])

    ri = _iota((c, c), 0)
    ci = _iota((c, c), 1)
    tri_l = jnp.where(ri >= ci, 1.0, 0.0).astype(F32)
    tri_u = jnp.where(ri <= ci, 1.0, 0.0).astype(F32)

    def chunk_pair(idx_f, idx_b, s_prev):
        def ds(idx, base=0):
            start = idx * c
            start = start if isinstance(start, int) else pl.multiple_of(start, c)
            return pl.ds(base + start, c)

        col_f, col_b, row_f, row_b, beta_f, beta_b = [], [], [], [], [], []
        for r in range(reqs):
            cs_f = _mm_f32(tri_l, g_s[ds(idx_f, r * seq), :])
            cs_b = _mm_f32(tri_u, g_s[ds(idx_b, r * seq), :])
            cst_f = cs_f.T
            cst_b = cs_b.T
            b_f = b_s[ds(idx_f, r * seq), :]
            b_b = b_s[ds(idx_b, r * seq), :]
            for h in range(nh):
                jf, jb = 2 * nh + h, 3 * nh + h
                col_f.append(cs_f[:, jf:jf + 1])
                col_b.append(cs_b[:, jb:jb + 1])
                row_f.append(cst_f[jf:jf + 1, :])
                row_b.append(cst_b[jb:jb + 1, :])
                beta_f.append(b_f[:, h:h + 1])
                beta_b.append(b_b[:, nh + h:nh + h + 1])
        gc_col = jnp.stack(col_f + col_b, 0)
        gc_row = jnp.stack(row_f + row_b, 0)
        beta = jnp.stack(beta_f + beta_b, 0)
        e_col = jnp.exp(gc_col)
        gl = jnp.concatenate([gc_col[:nf, c - 1:c, :], gc_col[nf:, 0:1, :]], 0)
        e_rest = jnp.exp(gl - gc_col)
        e_all = jnp.exp(gl)
        sl_f, sl_b = ds(idx_f), ds(idx_b)
        both = lambda ref: jnp.concatenate([ref[:, sl_f, :], ref[:, sl_b, :]], 0)
        o, s_new = _dn_chunk(both(q_s), both(k_s), both(v_s), gc_col, gc_row, beta, e_col, e_rest, e_all,
                             s_prev, n == 1, ri, ci)
        o_s[:nf, sl_f, :] = o[:nf]
        o_s[nf:, sl_b, :] = o[nf:]
        return s_new

    if not zero_init:
        s0 = jnp.concatenate([s0f_ref[r] for r in range(reqs)] + [s0b_ref[r] for r in range(reqs)], 0)
    if n == 1:
        s_fin = chunk_pair(0, 0, None if zero_init else s0)
    else:
        st_s[...] = jnp.zeros(st_s.shape, F32) if zero_init else s0

        def body(i, carry):
            st_s[...] = chunk_pair(i, n - 1 - i, st_s[...])
            return carry

        lax.fori_loop(0, n, body, 0)
        s_fin = st_s[...]
    for r in range(reqs):
        sf_ref[r] = s_fin[r * nh:(r + 1) * nh]
        sb_ref[r] = s_fin[nf + r * nh:nf + (r + 1) * nh]

    gn = gn_ref[...]
    for r in range(reqs):
        rows = slice(r * seq, (r + 1) * seq)
        for h in range(nh):
            o = o_s[r * nh + h] + o_s[nf + r * nh + h]
            o = o * lax.rsqrt(jnp.mean(o * o, axis=-1, keepdims=True) + RMS_EPS) * gn
            o_ref[rows, lanes(h)] = (o * _silu(z_ref[rows, lanes(h)].astype(F32))).astype(o_ref.dtype)


def _deltanet_call(qkv, z, ba, conv_w, prm, gn, s0f, s0b, bsz, seq, reqs, pre_conv, act_dt):
    nh = DN_HEADS
    zero_init = s0f is None
    rows = reqs * seq
    st = pl.BlockSpec((reqs, nh, DN_DK, DN_DV), lambda b: (b, 0, 0, 0))
    in_specs = [pl.BlockSpec((rows, QKV_WIDTH), lambda b: (b, 0)),
                pl.BlockSpec((3, QKV_WIDTH), lambda b: (0, 0)),
                pl.BlockSpec((rows, DN_WIDTH), lambda b: (b, 0)),
                pl.BlockSpec((rows, LANES), lambda b: (b, 0)),
                pl.BlockSpec((8, LANES), lambda b: (0, 0)),
                pl.BlockSpec((1, LANES), lambda b: (0, 0))]
    args = [qkv, conv_w, z, ba, prm, gn]
    if not zero_init:
        in_specs += [st, st]
        args += [s0f, s0b]
    vm = lambda shape: pltpu.VMEM(shape, F32)
    per_head = vm((reqs * nh, seq, LANES))
    return pl.pallas_call(
        functools.partial(_deltanet_kernel, seq, reqs, zero_init, pre_conv),
        grid=(bsz // reqs,),
        in_specs=in_specs,
        out_specs=[pl.BlockSpec((rows, DN_WIDTH), lambda b: (b, 0)), st, st],
        out_shape=[jax.ShapeDtypeStruct((bsz * seq, DN_WIDTH), act_dt),
                   jax.ShapeDtypeStruct((bsz, nh, DN_DK, DN_DV), F32),
                   jax.ShapeDtypeStruct((bsz, nh, DN_DK, DN_DV), F32)],
        scratch_shapes=[per_head, per_head, per_head, vm((rows, LANES)), vm((rows, LANES)),
                        vm((2 * reqs * nh, seq, LANES)), vm((2 * reqs * nh, DN_DK, DN_DV))],
        compiler_params=_cparams(1),
        name="deltanet",
    )(*args)


def _dft_mats(seq):
    n = 2 * seq
    f = np.arange(seq)[:, None]
    s = np.arange(seq)[None, :]
    ang = 2.0 * np.pi * ((f * s) % n) / n
    fwd = np.concatenate([np.cos(ang), -np.sin(ang)], axis=0)
    fwd[seq, :] = np.cos(np.pi * np.arange(seq))
    t = (np.arange(seq) + seq // 2)[:, None]
    ff = np.arange(seq)[None, :]
    ang2 = 2.0 * np.pi * ((t * ff) % n) / n
    inv_r = 2.0 * np.cos(ang2) / n
    inv_i = -2.0 * np.sin(ang2) / n
    inv_r[:, 0] = 1.0 / n
    inv_i[:, 0] = np.cos(np.pi * t[:, 0]) / n
    inv = np.concatenate([inv_r, inv_i], axis=1)
    return fwd.astype(np.float32), inv.astype(np.float32)


def _filter_feats(seq):
    t = np.linspace(0.0, 1.0, seq)[:, None]
    bands = (HY_EMB - 1) // 2
    ang = (2.0 * math.pi * np.arange(seq) / seq)[:, None] * np.linspace(1e-4, bands - 1, bands)[None, :]
    feats = np.concatenate([t, np.cos(ang), -np.sin(ang)], -1)
    deltas = np.abs(np.linspace(math.log(HY_TARGET) / HY_SLOW_DECAY, math.log(HY_TARGET) / HY_FAST_DECAY,
                                HY_WIDTH))
    offset = np.abs(np.arange(seq) - seq // 2) / (seq // 2)
    window = np.exp(-offset[:, None] * deltas[None, :])
    feats = np.pad(feats, ((0, 0), (0, LANES - HY_EMB)))
    return jnp.asarray(feats, dtype=F32), jnp.asarray(window, dtype=F32)


def _hfilter_kernel(seq, feats_ref, win_ref, w1_ref, b1_ref, w2_ref, b2_ref, w3_ref, fr_ref, fwd_hi_ref,
                    fwd_lo_ref, ha_ref, hb_ref, hd_ref):
    fr = fr_ref[...]
    hid = jnp.sin(fr * (_mm_f32(feats_ref[...], w1_ref[...]) + b1_ref[...]))
    hid = jnp.sin(fr * (_mm_f32(hid, w2_ref[...]) + b2_ref[...]))
    filt = _mm_f32(hid, w3_ref[...]) * win_ref[...]
    filt = filt / (jnp.sum(jnp.abs(filt), axis=0, keepdims=True) + 1e-6)
    filt_hi = filt.astype(BF16)
    filt_lo = (filt - filt_hi.astype(F32)).astype(BF16)
    fwd_hi = fwd_hi_ref[...]
    spec = (jnp.dot(fwd_hi, filt_hi, preferred_element_type=F32)
            + jnp.dot(fwd_hi, filt_lo, preferred_element_type=F32)
            + jnp.dot(fwd_lo_ref[...], filt_hi, preferred_element_type=F32))
    h_re = spec[:seq, :]
    h_im = spec[seq:, :]
    first = _iota(h_re.shape, 0) == 0
    ha_ref[...] = h_re
    hb_ref[...] = jnp.where(first, 0.0, h_im)
    hd_ref[...] = jnp.where(first, h_im, h_re)


def _hfilter_call(seq, w1, b1, w2, b2, w3, freq, fwd_hi, fwd_lo):
    feats, window = _filter_feats(seq)
    w1p = jnp.pad(w1, ((0, LANES - HY_EMB), (0, 0)))
    full = lambda a: pl.BlockSpec(a.shape, lambda i: (0,) * a.ndim)
    args = [feats, window, w1p, b1.reshape(1, -1), w2, b2.reshape(1, -1), w3, freq.reshape(1, -1), fwd_hi,
            fwd_lo]
    out = jax.ShapeDtypeStruct((seq, HY_WIDTH), F32)
    return pl.pallas_call(
        functools.partial(_hfilter_kernel, seq),
        grid=(1,),
        in_specs=[full(a) for a in args],
        out_specs=[pl.BlockSpec((seq, HY_WIDTH), lambda i: (0, 0))] * 3,
        out_shape=[out, out, out],
        compiler_params=_cparams(1),
        name="hfilter",
    )(*args)


def _hyena_kernel(seq, reqs, pre_conv, *refs):
    if pre_conv:
        x0_ref, uu_ref, ha_ref, hb_ref, hd_ref, skip_ref, fwd_ref, inv_ref, o_ref = refs
    else:
        (x0_ref, x1_ref, v_ref, c0_ref, c1_ref, c2_ref, b0_ref, b1_ref, b2_ref,
         ha_ref, hb_ref, hd_ref, skip_ref, fwd_ref, inv_ref, o_ref) = refs
        row = _iota((seq, HY_CH_BLOCK), 0)
        first = row == 0
        last = row == seq - 1
    hb = hb_ref[...]
    for r in range(reqs):
        rows = slice(r * seq, (r + 1) * seq)
        if pre_conv:
            x0 = x0_ref[rows, :].astype(F32)
            uu = uu_ref[rows, :].astype(F32)
        else:
            conv = lambda x_ref, w_ref, b_ref: (_short_conv3(x_ref[rows, :].astype(F32), w_ref[...], first, last)
                                                + b_ref[...])
            x0 = conv(x0_ref, c0_ref, b0_ref)
            uu = conv(x1_ref, c1_ref, b1_ref) * conv(v_ref, c2_ref, b2_ref)
        spec = jnp.dot(fwd_ref[...], uu.astype(BF16), preferred_element_type=F32)
        u_re = spec[:seq, :]
        u_im = spec[seq:, :]
        y_re = u_re * ha_ref[...] - u_im * hb
        y_im = u_re * hb + u_im * hd_ref[...]
        y = jnp.concatenate([y_re, y_im], axis=0).astype(BF16)
        cv = jnp.dot(inv_ref[...], y, preferred_element_type=F32)
        o_ref[rows, :] = (x0 * (cv + uu * skip_ref[...])).astype(o_ref.dtype)


def _hyena_call(hy, pre_conv, conv_w, conv_b, ha, hb, hd, skip, fwd, inv, bsz, seq, act_dt):
    cb = HY_CH_BLOCK
    nblk = HY_WIDTH // cb
    reqs = max(1, min(bsz, HY_ROWS_PER_STEP // seq))
    assert bsz % reqs == 0
    tok = lambda off: pl.BlockSpec((reqs * seq, cb), lambda b, j: (b, off + j))
    cw = lambda off: pl.BlockSpec((3, cb), lambda b, j: (0, off + j))
    bias = lambda off: pl.BlockSpec((1, cb), lambda b, j: (0, off + j))
    ch = pl.BlockSpec((seq, cb), lambda b, j: (0, j))
    const = lambda a: pl.BlockSpec(a.shape, lambda b, j: (0, 0), pipeline_mode=pl.Buffered(1))
    tail_specs = [ch, ch, ch, pl.BlockSpec((1, cb), lambda b, j: (0, j)), const(fwd), const(inv)]
    tail_args = [ha, hb, hd, skip.reshape(1, -1), fwd, inv]
    if pre_conv:
        in_specs = [tok(0), tok(nblk)] + tail_specs
        args = [hy, hy] + tail_args
    else:
        conv_b2 = conv_b.reshape(1, -1)
        in_specs = [tok(0), tok(nblk), tok(2 * nblk), cw(0), cw(nblk), cw(2 * nblk),
                    bias(0), bias(nblk), bias(2 * nblk)] + tail_specs
        args = [hy, hy, hy, conv_w, conv_w, conv_w, conv_b2, conv_b2, conv_b2] + tail_args
    return pl.pallas_call(
        functools.partial(_hyena_kernel, seq, reqs, pre_conv),
        grid=(bsz // reqs, nblk),
        in_specs=in_specs,
        out_specs=tok(0),
        out_shape=jax.ShapeDtypeStruct((bsz * seq, HY_WIDTH), act_dt),
        compiler_params=_cparams(2),
        name="hyena",
    )(*args)


def _mix_kernel(has_pos, *refs):
    refs = list(refs)
    x_ref = refs.pop(0)
    pos_ref = refs.pop(0) if has_pos else None
    (oa_ref, ob_ref, g_ref, mod_ref, wua, wub, wout, l1g, l1b, wr_hi, wr_lo, x1_o, u2_o, p_o) = refs
    d = D_MODEL
    tm = x_ref.shape[0]
    sub = min(tm, MIX_SUB_ROWS)
    blocks = [slice(r0, r0 + sub) for r0 in range(0, tm, sub)]
    g1 = mod_ref[0, 2]
    sh2 = mod_ref[0, 3]
    sc2 = mod_ref[0, 4]

    mixed = []
    for rows in blocks:
        gates = g_ref[rows, :].astype(F32)
        ga = _sigmoid(gates[:, :d])
        gb = _sigmoid(gates[:, d:])
        up = (ga * jnp.dot(oa_ref[rows, :], wua[...], preferred_element_type=F32)
              + gb * jnp.dot(ob_ref[rows, :], wub[...], preferred_element_type=F32))
        mixed.append(jnp.dot(up.astype(BF16), wout[...], preferred_element_type=F32))

    for rows, mx in zip(blocks, mixed):
        x = x_ref[rows, :]
        if has_pos:
            x = x + pos_ref[rows, :]
        x1 = _ln(ALPHA * x + g1 * mx) * l1g[...] + l1b[...]
        x1_o[rows, :] = x1.astype(x1_o.dtype)
        u2 = _ln(x1) * (1.0 + sc2) + sh2
        u2_hi = u2.astype(BF16)
        u2_o[rows, :] = u2_hi
        u2_lo = (u2 - u2_hi.astype(F32)).astype(BF16)
        logits = (jnp.dot(u2_hi, wr_hi[...], preferred_element_type=F32)
                  + jnp.dot(u2_hi, wr_lo[...], preferred_element_type=F32)
                  + jnp.dot(u2_lo, wr_hi[...], preferred_element_type=F32))
        lane = _iota(logits.shape, 1)
        logits = jnp.where(lane < N_EXPERTS, logits, -jnp.inf)
        m = jnp.max(logits, axis=-1, keepdims=True)
        e = jnp.exp(logits - m)
        p_o[rows, :] = e / jnp.sum(e, axis=-1, keepdims=True)


def _mix_call(x2d, pos, o_a, o_b, gates, mod4, mod_row, wua, wub, wout, l1g, l1b, wr_hi, wr_lo):
    n, d = x2d.shape
    tm = min(MIX_TILE, max(n // MIX_MIN_STEPS, MIX_SUB_ROWS))
    assert n % tm == 0
    row = lambda i: (i, 0)
    const = lambda i: (0, 0)
    in_specs = [pl.BlockSpec((tm, d), row)]
    args = [x2d]
    if pos is not None:
        tiles = pos.shape[0] // tm
        in_specs.append(pl.BlockSpec((tm, d), lambda i: (i % tiles, 0)))
        args.append(pos)
    in_specs += [pl.BlockSpec((tm, DN_WIDTH), row), pl.BlockSpec((tm, HY_WIDTH), row),
                 pl.BlockSpec((tm, 2 * d), row),
                 _mod_spec(lambda i: mod_row(i * tm))]
    args += [o_a, o_b, gates, mod4]
    for w in (wua, wub, wout, l1g, l1b, wr_hi, wr_lo):
        in_specs.append(pl.BlockSpec(w.shape, const, pipeline_mode=pl.Buffered(1)))
        args.append(w)
    return pl.pallas_call(
        functools.partial(_mix_kernel, pos is not None),
        grid=(n // tm,),
        in_specs=in_specs,
        out_specs=[pl.BlockSpec((tm, d), row), pl.BlockSpec((tm, d), row), pl.BlockSpec((tm, LANES), row)],
        out_shape=[jax.ShapeDtypeStruct((n, d), ACT_DT), jax.ShapeDtypeStruct((n, d), BF16),
                   jax.ShapeDtypeStruct((n, LANES), F32)],
        compiler_params=_cparams(1),
        name="mix",
    )(*args)


def _route_kernel(seq, cap, reqs, p_ref, u_ref, tri_ref, xs_ref, slot_ref, pt_ref, g_s):
    e_n = N_EXPERTS
    tri = tri_ref[...]
    jrow = _iota((cap, seq), 0).astype(F32)
    if seq <= RANK_COUNT_MAX_SEQ:
        earlier = _iota((seq, seq), 0) < _iota((seq, seq), 1)
    for r in range(reqs):
        rows = slice(r * seq, (r + 1) * seq)
        p = p_ref[rows, :]
        pt = p.T[:e_n, :]

        if seq <= RANK_COUNT_MAX_SEQ:
            ranks = []
            for e in range(e_n):
                pc = p[:, e:e + 1]
                pr = pt[e:e + 1, :]
                beats = (pc > pr) | (earlier & (pc == pr))
                ranks.append(jnp.sum(jnp.where(beats, 1.0, 0.0), axis=0, keepdims=True))
            sel = jnp.concatenate(ranks, axis=0) < cap
        else:
            def search(i, cur):
                cand = cur | (1 << (30 - i))
                cand_f = pltpu.bitcast(cand, F32)
                cnt = jnp.sum(jnp.where(pt >= cand_f, 1.0, 0.0), axis=1, keepdims=True)
                return jnp.where((cnt >= cap) & (cand >= MIN_NORMAL_F32_BITS), cand, cur)

            thr = pltpu.bitcast(lax.fori_loop(0, 31, search, jnp.zeros((e_n, 1), jnp.int32)), F32)
            gt = pt > thr
            eq = pt == thr
            n_gt = jnp.sum(jnp.where(gt, 1.0, 0.0), axis=1, keepdims=True)
            eq_rank = jnp.dot(jnp.where(eq, 1.0, 0.0).astype(BF16), tri, preferred_element_type=F32)
            sel = gt | (eq & (eq_rank < cap - n_gt))
        pos = jnp.dot(jnp.where(sel, 1.0, 0.0).astype(BF16), tri, preferred_element_type=F32)
        slot = jnp.where(sel, pos, -1.0)
        slot_ref[r] = slot
        pt_ref[r] = pt
        for e in range(e_n):
            g_s[r, e * cap:(e + 1) * cap, :] = jnp.where(slot[e:e + 1, :] == jrow, 1.0, 0.0).astype(BF16)
        xs = jnp.dot(g_s[r], u_ref[rows, :], preferred_element_type=F32)
        for e in range(e_n):
            xs_ref[e, r * cap:(r + 1) * cap, :] = xs[e * cap:(e + 1) * cap, :].astype(xs_ref.dtype)


def _route_call(probs, u2, bsz, seq):
    cap = EC_FACTOR * seq // N_EXPERTS
    reqs = max(1, min(bsz, ROUTE_ROWS_PER_STEP // seq))
    assert bsz % reqs == 0
    tri = jnp.asarray(np.triu(np.ones((seq, seq), np.float32), 1), dtype=BF16)
    per_req = pl.BlockSpec((reqs, N_EXPERTS, seq), lambda b: (b, 0, 0))
    return pl.pallas_call(
        functools.partial(_route_kernel, seq, cap, reqs),
        grid=(bsz // reqs,),
        in_specs=[pl.BlockSpec((reqs * seq, LANES), lambda b: (b, 0)),
                  pl.BlockSpec((reqs * seq, D_MODEL), lambda b: (b, 0)),
                  pl.BlockSpec((seq, seq), lambda b: (0, 0), pipeline_mode=pl.Buffered(1))],
        out_specs=[pl.BlockSpec((N_EXPERTS, reqs * cap, D_MODEL), lambda b: (0, b, 0)), per_req, per_req],
        out_shape=[jax.ShapeDtypeStruct((N_EXPERTS, bsz * cap, D_MODEL), BF16),
                   jax.ShapeDtypeStruct((bsz, N_EXPERTS, seq), F32),
                   jax.ShapeDtypeStruct((bsz, N_EXPERTS, seq), F32)],
        scratch_shapes=[pltpu.VMEM((reqs, N_EXPERTS * cap, seq), BF16)],
        compiler_params=_cparams(1),
        name="route",
    )(probs, u2, tri)


def _expert_kernel(xc_ref, xd_ref, wg_ref, wu_ref, wd_ref, yc_ref, yd_ref):
    wg = wg_ref[0].astype(BF16)
    wu = wu_ref[0].astype(BF16)
    wd = wd_ref[0].astype(BF16)
    for x_ref, y_ref in ((xc_ref, yc_ref), (xd_ref, yd_ref)):
        rows = x_ref.shape[1]
        rb = min(rows, EXPERT_ROW_BLOCK)
        for r0 in range(0, rows, rb):
            x = x_ref[0, r0:r0 + rb, :]
            hg = jnp.dot(x, wg, preferred_element_type=F32)
            hu = jnp.dot(x, wu, preferred_element_type=F32)
            y = jnp.dot((_silu(hg) * hu).astype(BF16), wd, preferred_element_type=F32)
            y_ref[0, r0:r0 + rb, :] = y.astype(y_ref.dtype)


def _expert_call(xs_c, xs_d, w_gate, w_up, w_down):
    e_n, rc, d = xs_c.shape
    rd = xs_d.shape[1]
    ff = w_gate.shape[2]
    return pl.pallas_call(
        _expert_kernel,
        grid=(e_n,),
        in_specs=[pl.BlockSpec((1, rc, d), lambda e: (e, 0, 0)),
                  pl.BlockSpec((1, rd, d), lambda e: (e, 0, 0)),
                  pl.BlockSpec((1, d, ff), lambda e: (e, 0, 0)),
                  pl.BlockSpec((1, d, ff), lambda e: (e, 0, 0)),
                  pl.BlockSpec((1, ff, d), lambda e: (e, 0, 0))],
        out_specs=[pl.BlockSpec((1, rc, d), lambda e: (e, 0, 0)),
                   pl.BlockSpec((1, rd, d), lambda e: (e, 0, 0))],
        out_shape=[jax.ShapeDtypeStruct((e_n, rc, d), BF16), jax.ShapeDtypeStruct((e_n, rd, d), BF16)],
        compiler_params=_cparams(1),
        name="experts",
    )(xs_c, xs_d, w_gate, w_up, w_down)


def _final_kernel(seq, cap, reqs, x1_ref, y_ref, slot_ref, pt_ref, mod_ref, l2g, l2b, o_ref, w_s):
    jrow = _iota((cap, seq), 0).astype(F32)
    g2 = mod_ref[0, 5]
    for r in range(reqs):
        slot = slot_ref[r]
        pt = pt_ref[r]
        for e in range(N_EXPERTS):
            w_s[r, e * cap:(e + 1) * cap, :] = jnp.where(slot[e:e + 1, :] == jrow, pt[e:e + 1, :],
                                                         0.0).astype(BF16)
        y = y_ref[:, r * cap:(r + 1) * cap, :].reshape(N_EXPERTS * cap, y_ref.shape[-1])
        ffn = lax.dot_general(w_s[r], y, (((0,), (0,)), ((), ())), preferred_element_type=F32)
        rows = slice(r * seq, (r + 1) * seq)
        o_ref[rows, :] = _ln(ALPHA * x1_ref[rows, :].astype(F32) + g2 * ffn) * l2g[...] + l2b[...]


def _final_call(x1, ys, slot, pt, mod4, mod_row, l2g, l2b, bsz, seq, reqs):
    cap = EC_FACTOR * seq // N_EXPERTS
    d = D_MODEL
    per_req = pl.BlockSpec((reqs, N_EXPERTS, seq), lambda b: (b, 0, 0))
    return pl.pallas_call(
        functools.partial(_final_kernel, seq, cap, reqs),
        grid=(bsz // reqs,),
        in_specs=[pl.BlockSpec((reqs * seq, d), lambda b: (b, 0)),
                  pl.BlockSpec((N_EXPERTS, reqs * cap, d), lambda b: (0, b, 0)),
                  per_req, per_req,
                  _mod_spec(lambda b: mod_row(b * reqs * seq)),
                  pl.BlockSpec((1, d), lambda b: (0, 0)),
                  pl.BlockSpec((1, d), lambda b: (0, 0))],
        out_specs=pl.BlockSpec((reqs * seq, d), lambda b: (b, 0)),
        out_shape=jax.ShapeDtypeStruct((bsz * seq, d), F32),
        scratch_shapes=[pltpu.VMEM((reqs, N_EXPERTS * cap, seq), BF16)],
        compiler_params=_cparams(1),
        name="final",
    )(x1, ys, slot, pt, mod4, l2g, l2b)


def _grid_pos_embed(rows, dim):
    r = np.repeat(np.arange(rows), GRID_W)
    col = np.tile(np.arange(GRID_W), rows)
    quarter = dim // 4
    omega = 1.0 / (10000.0 ** (np.arange(quarter) / quarter))

    def enc(p):
        ang = p[:, None] * omega[None, :]
        return np.concatenate([np.sin(ang), np.cos(ang)], -1)

    return jnp.asarray(np.concatenate([enc(r), enc(col)], -1), dtype=F32)


def kernel(x_prompt, x_sample, state_delta_fwd, state_delta_bwd, c, c_ctx, w_mod, b_mod, w_in, conv_a_w, a_log, dt_bias, gnorm_w, conv_b_w, conv_b_b, hy_w1, hy_b1, hy_w2, hy_b2, hy_w3, hy_freq, hy_bias, w_up_a, w_up_b, w_out, ln1_g, ln1_b, w_router, w_e_gate, w_e_up, w_e_down, ln2_g, ln2_b):
    d = D_MODEL
    n_ctx, l_ctx, _ = x_prompt.shape
    n_dec, l_dec, _ = x_sample.shape
    lyr = 0

    cond_t = jnp.pad(jnp.concatenate([c_ctx[None, :], c], 0).T, ((0, 0), (0, 7 - n_dec)))
    mod4 = _mod_call(cond_t, 1 + n_dec, w_mod[lyr], b_mod[lyr]).reshape(8, N_MOD, 1, d)

    in_w = w_in[lyr].astype(BF16)

    prm = jnp.pad(jnp.stack([a_log[lyr].reshape(-1), dt_bias[lyr].reshape(-1)]),
                  ((0, 6), (2 * DN_HEADS, LANES - 4 * DN_HEADS)))
    gn = gnorm_w[lyr].reshape(1, -1)

    wua = w_up_a[lyr].astype(BF16)
    wub = w_up_b[lyr].astype(BF16)
    wout = w_out[lyr].astype(BF16)
    wr = jnp.pad(w_router[lyr], ((0, 0), (0, LANES - N_EXPERTS)))
    wr_hi = wr.astype(BF16)
    wr_lo = (wr - wr_hi.astype(F32)).astype(BF16)
    l1g, l1b = ln1_g[lyr].reshape(1, -1), ln1_b[lyr].reshape(1, -1)
    l2g, l2b = ln2_g[lyr].reshape(1, -1), ln2_b[lyr].reshape(1, -1)

    def front(x2d, pos, bsz, seq, mod_row, s0f, s0b):
        (qkv, z, ba, hy, gates), pre_conv = _inproj_call(x2d, pos, mod4, mod_row, in_w, conv_a_w[lyr],
                                                          conv_b_w[lyr], conv_b_b[lyr], seq, ACT_DT)
        reqs = DN_REQS_PER_STEP if seq == DN_CHUNK and bsz % DN_REQS_PER_STEP == 0 else 1
        o_a, s_f, s_b = _deltanet_call(qkv, z, ba, conv_a_w[lyr], prm, gn, s0f, s0b, bsz, seq, reqs,
                                       pre_conv, ACT_DT)
        fwd, inv = (jnp.asarray(m) for m in _dft_mats(seq))
        fwd_hi = fwd.astype(BF16)
        fwd_lo = (fwd - fwd_hi.astype(F32)).astype(BF16)
        ha, hb, hd = _hfilter_call(seq, hy_w1[lyr], hy_b1[lyr], hy_w2[lyr], hy_b2[lyr], hy_w3[lyr],
                                   hy_freq[lyr], fwd_hi, fwd_lo)
        o_h = _hyena_call(hy, pre_conv, conv_b_w[lyr], conv_b_b[lyr], ha, hb, hd, hy_bias[lyr],
                          fwd_hi, inv.astype(BF16), bsz, seq, ACT_DT)
        x1, u2, probs = _mix_call(x2d, pos, o_a, o_h, gates, mod4, mod_row, wua, wub, wout, l1g, l1b,
                                   wr_hi, wr_lo)
        xs, slot, pt = _route_call(probs, u2, bsz, seq)
        return x1, xs, slot, pt, s_f, s_b

    row_ctx = lambda tok: 0
    row_dec = lambda tok: 1 + tok // l_dec

    xc = x_prompt.reshape(n_ctx * l_ctx, d)
    xd = x_sample.reshape(n_dec * l_dec, d)
    pos = _grid_pos_embed(l_dec // GRID_W, d)

    x1c, xsc, gc, ptc, s_f, s_b = front(xc, None, n_ctx, l_ctx, row_ctx, None, None)
    x1d, xsd, gd, ptd, _, _ = front(xd, pos, n_dec, l_dec, row_dec,
                                    state_delta_fwd[:, lyr], state_delta_bwd[:, lyr])
    yc, yd = _expert_call(xsc, xsd, w_e_gate[lyr], w_e_up[lyr], w_e_down[lyr])
    reqs_c = FINAL_CTX_REQS_PER_STEP if n_ctx % FINAL_CTX_REQS_PER_STEP == 0 else 1
    y_prompt = _final_call(x1c, yc, gc, ptc, mod4, row_ctx, l2g, l2b, n_ctx, l_ctx, reqs_c)
    y_sample = _final_call(x1d, yd, gd, ptd, mod4, row_dec, l2g, l2b, n_dec, l_dec, 1)

    return (y_prompt.reshape(n_ctx, l_ctx, d), y_sample.reshape(n_dec, l_dec, d),
            s_f[:, None], s_b[:, None])
```

```python
import functools
import math

import jax
import jax.numpy as jnp
import numpy as np
from jax import lax
from jax.experimental import pallas as pl
from jax.experimental.pallas import tpu as pltpu

F32 = jnp.float32
BF16 = jnp.bfloat16
HIGHEST = lax.Precision.HIGHEST

D_MODEL = 1024
DEPTH = 1
GRID_W = 64
DN_HEADS = 4
DN_DK = 128
DN_DV = 128
DN_WIDTH = DN_HEADS * DN_DV
HY_WIDTH = D_MODEL // 2
HY_EMB = 33
HY_FFN = 64
HY_TARGET = 1e-2
HY_FAST_DECAY = 0.3
HY_SLOW_DECAY = 1.5
N_EXPERTS = 16
EC_FACTOR = 2
N_MOD = 6
ALPHA = (2 * DEPTH) ** 0.25
LN_EPS = 1e-5
RMS_EPS = 1e-6
QKV_WIDTH = 2 * DN_HEADS * DN_DK + DN_WIDTH

LANES = 128
INPROJ_TILE = 1024
INPROJ_BLOCK_ROWS = 512
MIX_TILE = 1024
MIX_SUB_ROWS = 256
MIX_MIN_STEPS = 4
MOD_COL_BLOCK = 1024
DN_CHUNK = 256
INV_BASE = 8
DN_REQS_PER_STEP = 2
HY_CH_BLOCK = 512
HY_ROWS_PER_STEP = 1024
FINAL_CTX_REQS_PER_STEP = 4
EXPERT_ROW_BLOCK = 512
VMEM_LIMIT = 56 * 1024 * 1024
MIN_NORMAL_F32_BITS = 0x00800000
ACT_DT = BF16
ROUTE_ROWS_PER_STEP = 1024
RANK_COUNT_MAX_SEQ = 256


def _cparams(n_axes):
    return pltpu.CompilerParams(dimension_semantics=("arbitrary",) * n_axes,
                                vmem_limit_bytes=VMEM_LIMIT)


def _bmm(a, b):
    return jnp.einsum("bmk,bkn->bmn", a.astype(BF16), b.astype(BF16), preferred_element_type=F32)


def _bmm_nt(a, b):
    return jnp.einsum("bmk,bnk->bmn", a.astype(BF16), b.astype(BF16), preferred_element_type=F32)


def _bmm_tn(a, b):
    return jnp.einsum("bkm,bkn->bmn", a.astype(BF16), b.astype(BF16), preferred_element_type=F32)


def _mm_f32(a, b):
    return jnp.dot(a, b, precision=HIGHEST, preferred_element_type=F32)


def _sigmoid(x):
    return 0.5 * jnp.tanh(0.5 * x) + 0.5


def _silu(x):
    h = 0.5 * x
    return h * (jnp.tanh(h) + 1.0)


def _softplus(x):
    return jnp.maximum(x, 0.0) + jnp.log1p(jnp.exp(-jnp.abs(x)))


def _ln(x):
    mu = jnp.mean(x, axis=-1, keepdims=True)
    xc = x - mu
    var = jnp.mean(xc * xc, axis=-1, keepdims=True)
    return xc * lax.rsqrt(var + LN_EPS)


def _l2norm(x, scale=1.0):
    return x * (lax.rsqrt(jnp.sum(x * x, axis=-1, keepdims=True) + 1e-6) * scale)


def _iota(shape, dim):
    return lax.broadcasted_iota(jnp.int32, shape, dim)


def _short_conv3(x, w, first, last):
    xm = jnp.where(first, 0.0, pltpu.roll(x, 1, 0))
    xp = jnp.where(last, 0.0, pltpu.roll(x, x.shape[0] - 1, 0))
    return w[0:1, :] * xm + w[1:2, :] * x + w[2:3, :] * xp


def _mod_kernel(n_rows, ct_ref, w_ref, b_ref, o_ref):
    s = _silu(ct_ref[...])
    w = w_ref[...]
    b = b_ref[...]
    o_ref[...] = jnp.zeros(o_ref.shape, F32)
    for r in range(n_rows):
        o_ref[r:r + 1, :] = jnp.sum(w * s[:, r:r + 1], axis=0, keepdims=True) + b


def _mod_call(cond_t, n_rows, w_mod, b_mod):
    d = D_MODEL
    tn = MOD_COL_BLOCK
    return pl.pallas_call(
        functools.partial(_mod_kernel, n_rows),
        grid=(N_MOD * d // tn,),
        in_specs=[pl.BlockSpec((d, 8), lambda j: (0, 0)),
                  pl.BlockSpec((d, tn), lambda j: (0, j)),
                  pl.BlockSpec((1, tn), lambda j: (0, j))],
        out_specs=pl.BlockSpec((8, tn), lambda j: (0, j)),
        out_shape=jax.ShapeDtypeStruct((8, N_MOD * d), F32),
        compiler_params=_cparams(1),
        name="mod",
    )(cond_t, w_mod, b_mod.reshape(1, -1))


def _mod_spec(row_of_step):
    return pl.BlockSpec((1, N_MOD, 1, D_MODEL), lambda *idx: (row_of_step(*idx), 0, 0, 0))


def _inproj_kernel(has_pos, conv_seq, *refs):
    refs = list(refs)
    x_ref = refs.pop(0)
    pos_ref = refs.pop(0) if has_pos else None
    mod_ref, w_ref = refs.pop(0), refs.pop(0)
    if conv_seq:
        cwa_ref, cwb_ref, cbb_ref = refs.pop(0), refs.pop(0), refs.pop(0)
    qkv_o, z_o, ba_o, hy_o, g_o = refs
    sh1 = mod_ref[0, 0]
    sc1 = mod_ref[0, 1]
    o_z = QKV_WIDTH
    o_ba = o_z + DN_WIDTH
    n_ba = 4 * DN_HEADS
    o_g = n_ba + 3 * HY_WIDTH
    lanes = lambda j: slice(j * LANES, (j + 1) * LANES)
    tm = x_ref.shape[0]
    sub = min(tm, INPROJ_BLOCK_ROWS)
    blocks = [slice(r0, r0 + sub) for r0 in range(0, tm, sub)]

    if conv_seq:
        row = _iota((sub, LANES), 0) % conv_seq
        first = row == 0
        last = row == conv_seq - 1

    def project(rows):
        x = x_ref[rows, :]
        if has_pos:
            x = x + pos_ref[rows, :]
        u = (_ln(x) * (1.0 + sc1) + sh1).astype(BF16)
        return (jnp.dot(u, w_ref[:, :o_ba], preferred_element_type=F32),
                jnp.dot(u, w_ref[:, o_ba:], preferred_element_type=F32))

    def epilogue(rows, head, tail):
        z_o[rows, :] = head[:, o_z:].astype(z_o.dtype)
        ba_o[rows, :] = tail[:, :LANES]
        g_o[rows, :] = tail[:, o_g:].astype(g_o.dtype)
        hy = tail[:, n_ba:o_g]
        if not conv_seq:
            qkv_o[rows, :] = head[:, :o_z].astype(qkv_o.dtype)
            hy_o[rows, :] = hy.astype(hy_o.dtype)
            return
        for j in range(QKV_WIDTH // LANES):
            y = _silu(_short_conv3(head[:, lanes(j)], cwa_ref[:, lanes(j)], first, last))
            if j < 2 * DN_HEADS:
                y = _l2norm(y, DN_DK ** -0.5 if j < DN_HEADS else 1.0)
            qkv_o[rows, lanes(j)] = y.astype(qkv_o.dtype)
        nj = HY_WIDTH // LANES
        for j in range(nj):
            parts = [_short_conv3(hy[:, lanes(p * nj + j)], cwb_ref[:, lanes(p * nj + j)], first, last)
                     + cbb_ref[:, lanes(p * nj + j)] for p in range(3)]
            hy_o[rows, lanes(j)] = parts[0].astype(hy_o.dtype)
            hy_o[rows, lanes(nj + j)] = (parts[1] * parts[2]).astype(hy_o.dtype)

    pending = None
    for rows in blocks:
        cur = (rows, *project(rows))
        if pending is not None:
            epilogue(*pending)
        pending = cur
    epilogue(*pending)


def _inproj_call(x2d, pos, mod4, mod_row, w, conv_a_w, conv_b_w, conv_b_b, seq, act_dt):
    n, d = x2d.shape
    fuse = seq <= INPROJ_BLOCK_ROWS and INPROJ_TILE % seq == 0
    tm = min(INPROJ_TILE if fuse else INPROJ_BLOCK_ROWS, n)
    assert n % tm == 0 and (not fuse or tm % seq == 0)
    conv_seq = seq if fuse else None
    row = lambda i: (i, 0)
    const = lambda i: (0, 0)
    in_specs = [pl.BlockSpec((tm, d), row)]
    args = [x2d]
    if pos is not None:
        tiles = pos.shape[0] // tm
        in_specs.append(pl.BlockSpec((tm, d), lambda i: (i % tiles, 0)))
        args.append(pos)
    in_specs.append(_mod_spec(lambda i: mod_row(i * tm)))
    args.append(mod4)
    in_specs.append(pl.BlockSpec(w.shape, const, pipeline_mode=pl.Buffered(1)))
    args.append(w)
    if conv_seq:
        for a in (conv_a_w, conv_b_w, conv_b_b.reshape(1, -1)):
            in_specs.append(pl.BlockSpec(a.shape, const))
            args.append(a)
    hy_width = (2 if conv_seq else 3) * HY_WIDTH
    widths = (QKV_WIDTH, DN_WIDTH, LANES, hy_width, 2 * D_MODEL)
    dts = (act_dt, act_dt, F32, act_dt, act_dt)
    outs = pl.pallas_call(
        functools.partial(_inproj_kernel, pos is not None, conv_seq),
        grid=(n // tm,),
        in_specs=in_specs,
        out_specs=[pl.BlockSpec((tm, w), row) for w in widths],
        out_shape=[jax.ShapeDtypeStruct((n, w), dt) for w, dt in zip(widths, dts)],
        compiler_params=_cparams(1),
        name="inproj",
    )(*args)
    return outs, conv_seq is not None


def _inv_unit_tri(lm, ri, ci, block):
    c = lm.shape[-1]

    def same_block(h):
        sh = h.bit_length() - 1
        return (ri >> sh) == (ci >> sh)

    def unfold(f, h):
        return jnp.where(same_block(h), jnp.concatenate([f] * (c // h), axis=1), 0.0)

    base = INV_BASE
    n_nat = jnp.where(same_block(base), -lm, 0.0)
    n_fold = n_nat[:, 0:base, :]
    for i in range(1, c // base):
        n_fold = n_fold + n_nat[:, i * base:(i + 1) * base, :]
    eye_fold = jnp.where((_iota((base, c), 1) & (base - 1)) == _iota((base, c), 0), 1.0, 0.0)
    t = eye_fold + n_fold
    p_fold, p_nat = n_fold, n_nat
    span = 2
    while span < base:
        p_fold = _bmm(p_fold, p_nat)
        p_nat = unfold(p_fold, base)
        t = t + _bmm(t, p_nat)
        span *= 2
    s = base
    while s < block:
        sh = s.bit_length() - 1
        t_nat = unfold(t, s)
        even = ((_iota((s, c), 1) >> sh) & 1) == 0
        t2 = jnp.concatenate([jnp.where(even, t, 0.0), jnp.where(even, 0.0, t)], axis=1)
        lo = jnp.where(same_block(2 * s) & ~same_block(s), lm, 0.0)
        t = t2 - _bmm(_bmm(t2, lo), t_nat)
        s *= 2
    return t


def _dn_chunk(q, k, v, gc_col, gc_row, beta, e_col, e_rest, e_all, s_prev, same_chunk, ri, ci):
    c = DN_CHUNK
    nh = q.shape[0] // 2
    half = c // 2
    if same_chunk:
        gram = _bmm_nt(k[:nh], k[:nh])
        qk = _bmm_nt(q[:nh], k[:nh])
        gram = jnp.concatenate([gram, gram], 0)
        qk = jnp.concatenate([qk, qk], 0)
    else:
        gram = _bmm_nt(k, k)
        qk = _bmm_nt(q, k)
    diff = gc_col - gc_row
    incl = jnp.concatenate([jnp.broadcast_to(ri >= ci, (nh, c, c)), jnp.broadcast_to(ri <= ci, (nh, c, c))], 0)
    strict = incl & (ri != ci)
    decay = jnp.exp(jnp.where(incl, diff, -jnp.inf))
    lm = jnp.where(strict, beta * gram * decay, 0.0)
    a_intra = qk * decay
    t = _inv_unit_tri(lm, ri, ci, half)
    t1 = t[:, :, :half]
    t2 = t[:, :, half:]
    rhs = jnp.concatenate([v * beta, k * (beta * e_col)], axis=2)
    y1 = _bmm(t1, rhs[:, :half, :])
    y2 = _bmm(t2, rhs[:, half:, :])
    c_f = _bmm(lm[:nh, half:, :half], y1[:nh])
    c_b = _bmm(lm[nh:, :half, half:], y2[nh:])
    c_f = _bmm(t2[:nh], c_f)
    c_b = _bmm(t1[nh:], c_b)
    sol = jnp.concatenate([jnp.concatenate([y1[:nh], y2[:nh] - c_f], 1),
                           jnp.concatenate([y1[nh:] - c_b, y2[nh:]], 1)], 0)
    u = sol[:, :, :DN_DV]
    w = sol[:, :, DN_DV:]
    ks = k * e_rest
    if s_prev is None:
        v_new = u
        o = _bmm(a_intra, v_new)
        s_new = _bmm_tn(ks, v_new)
    else:
        v_new = u - _bmm(w, s_prev)
        o = _bmm(q * e_col, s_prev) + _bmm(a_intra, v_new)
        s_new = s_prev * e_all + _bmm_tn(ks, v_new)
    return o, s_new


def _deltanet_kernel(seq, reqs, zero_init, pre_conv, *refs):
    qkv_ref, cw_ref, z_ref, ba_ref, prm_ref, gn_ref = refs[:6]
    refs = refs[6:]
    if not zero_init:
        s0f_ref, s0b_ref = refs[:2]
        refs = refs[2:]
    o_ref, sf_ref, sb_ref, q_s, k_s, v_s, g_s, b_s, o_s, st_s = refs
    c = DN_CHUNK
    n = seq // c
    nh = DN_HEADS
    nf = reqs * nh
    lanes = lambda j: slice(j * LANES, (j + 1) * LANES)

    if not pre_conv:
        row = _iota((seq, LANES), 0)
        first_row = row == 0
        last_row = row == seq - 1
    for r in range(reqs):
        rows = slice(r * seq, (r + 1) * seq)

        def qkv_part(j):
            x = qkv_ref[rows, lanes(j)].astype(F32)
            return x if pre_conv else _silu(_short_conv3(x, cw_ref[:, lanes(j)], first_row, last_row))

        for h in range(nh):
            q = qkv_part(h)
            k = qkv_part(nh + h)
            q_s[r * nh + h] = q if pre_conv else _l2norm(q, DN_DK ** -0.5)
            k_s[r * nh + h] = k if pre_conv else _l2norm(k)
            v_s[r * nh + h] = qkv_part(2 * nh + h)

    ba = ba_ref[...]
    b_s[...] = _sigmoid(ba)
    g_s[...] = -jnp.exp(prm_ref[0:1, :]) * _softplus(ba + prm_ref[1:2, :])

    ri = _iota((c, c), 0)
    ci = _iota((c, c), 1)
    tri_l = jnp.where(ri >= ci, 1.0, 0.0).astype(F32)
    tri_u = jnp.where(ri <= ci, 1.0, 0.0).astype(F32)

    def chunk_pair(idx_f, idx_b, s_prev):
        def ds(idx, base=0):
            start = idx * c
            start = start if isinstance(start, int) else pl.multiple_of(start, c)
            return pl.ds(base + start, c)

        col_f, col_b, row_f, row_b, beta_f, beta_b = [], [], [], [], [], []
        for r in range(reqs):
            cs_f = _mm_f32(tri_l, g_s[ds(idx_f, r * seq), :])
            cs_b = _mm_f32(tri_u, g_s[ds(idx_b, r * seq), :])
            cst_f = cs_f.T
            cst_b = cs_b.T
            b_f = b_s[ds(idx_f, r * seq), :]
            b_b = b_s[ds(idx_b, r * seq), :]
            for h in range(nh):
                jf, jb = 2 * nh + h, 3 * nh + h
                col_f.append(cs_f[:, jf:jf + 1])
                col_b.append(cs_b[:, jb:jb + 1])
                row_f.append(cst_f[jf:jf + 1, :])
                row_b.append(cst_b[jb:jb + 1, :])
                beta_f.append(b_f[:, h:h + 1])
                beta_b.append(b_b[:, nh + h:nh + h + 1])
        gc_col = jnp.stack(col_f + col_b, 0)
        gc_row = jnp.stack(row_f + row_b, 0)
        beta = jnp.stack(beta_f + beta_b, 0)
        e_col = jnp.exp(gc_col)
        gl = jnp.concatenate([gc_col[:nf, c - 1:c, :], gc_col[nf:, 0:1, :]], 0)
        e_rest = jnp.exp(gl - gc_col)
        e_all = jnp.exp(gl)
        sl_f, sl_b = ds(idx_f), ds(idx_b)
        both = lambda ref: jnp.concatenate([ref[:, sl_f, :], ref[:, sl_b, :]], 0)
        o, s_new = _dn_chunk(both(q_s), both(k_s), both(v_s), gc_col, gc_row, beta, e_col, e_rest, e_all,
                             s_prev, n == 1, ri, ci)
        o_s[:nf, sl_f, :] = o[:nf]
        o_s[nf:, sl_b, :] = o[nf:]
        return s_new

    if not zero_init:
        s0 = jnp.concatenate([s0f_ref[r] for r in range(reqs)] + [s0b_ref[r] for r in range(reqs)], 0)
    if n == 1:
        s_fin = chunk_pair(0, 0, None if zero_init else s0)
    else:
        st_s[...] = jnp.zeros(st_s.shape, F32) if zero_init else s0

        def body(i, carry):
            st_s[...] = chunk_pair(i, n - 1 - i, st_s[...])
            return carry

        lax.fori_loop(0, n, body, 0)
        s_fin = st_s[...]
    for r in range(reqs):
        sf_ref[r] = s_fin[r * nh:(r + 1) * nh]
        sb_ref[r] = s_fin[nf + r * nh:nf + (r + 1) * nh]

    gn = gn_ref[...]
    for r in range(reqs):
        rows = slice(r * seq, (r + 1) * seq)
        for h in range(nh):
            o = o_s[r * nh + h] + o_s[nf + r * nh + h]
            o = o * lax.rsqrt(jnp.mean(o * o, axis=-1, keepdims=True) + RMS_EPS) * gn
            o_ref[rows, lanes(h)] = (o * _silu(z_ref[rows, lanes(h)].astype(F32))).astype(o_ref.dtype)


def _deltanet_call(qkv, z, ba, conv_w, prm, gn, s0f, s0b, bsz, seq, reqs, pre_conv, act_dt):
    nh = DN_HEADS
    zero_init = s0f is None
    rows = reqs * seq
    st = pl.BlockSpec((reqs, nh, DN_DK, DN_DV), lambda b: (b, 0, 0, 0))
    in_specs = [pl.BlockSpec((rows, QKV_WIDTH), lambda b: (b, 0)),
                pl.BlockSpec((3, QKV_WIDTH), lambda b: (0, 0)),
                pl.BlockSpec((rows, DN_WIDTH), lambda b: (b, 0)),
                pl.BlockSpec((rows, LANES), lambda b: (b, 0)),
                pl.BlockSpec((8, LANES), lambda b: (0, 0)),
                pl.BlockSpec((1, LANES), lambda b: (0, 0))]
    args = [qkv, conv_w, z, ba, prm, gn]
    if not zero_init:
        in_specs += [st, st]
        args += [s0f, s0b]
    vm = lambda shape: pltpu.VMEM(shape, F32)
    per_head = vm((reqs * nh, seq, LANES))
    return pl.pallas_call(
        functools.partial(_deltanet_kernel, seq, reqs, zero_init, pre_conv),
        grid=(bsz // reqs,),
        in_specs=in_specs,
        out_specs=[pl.BlockSpec((rows, DN_WIDTH), lambda b: (b, 0)), st, st],
        out_shape=[jax.ShapeDtypeStruct((bsz * seq, DN_WIDTH), act_dt),
                   jax.ShapeDtypeStruct((bsz, nh, DN_DK, DN_DV), F32),
                   jax.ShapeDtypeStruct((bsz, nh, DN_DK, DN_DV), F32)],
        scratch_shapes=[per_head, per_head, per_head, vm((rows, LANES)), vm((rows, LANES)),
                        vm((2 * reqs * nh, seq, LANES)), vm((2 * reqs * nh, DN_DK, DN_DV))],
        compiler_params=_cparams(1),
        name="deltanet",
    )(*args)


def _dft_mats(seq):
    n = 2 * seq
    f = np.arange(seq)[:, None]
    s = np.arange(seq)[None, :]
    ang = 2.0 * np.pi * ((f * s) % n) / n
    fwd = np.concatenate([np.cos(ang), -np.sin(ang)], axis=0)
    fwd[seq, :] = np.cos(np.pi * np.arange(seq))
    t = (np.arange(seq) + seq // 2)[:, None]
    ff = np.arange(seq)[None, :]
    ang2 = 2.0 * np.pi * ((t * ff) % n) / n
    inv_r = 2.0 * np.cos(ang2) / n
    inv_i = -2.0 * np.sin(ang2) / n
    inv_r[:, 0] = 1.0 / n
    inv_i[:, 0] = np.cos(np.pi * t[:, 0]) / n
    inv = np.concatenate([inv_r, inv_i], axis=1)
    return fwd.astype(np.float32), inv.astype(np.float32)


def _filter_feats(seq):
    t = np.linspace(0.0, 1.0, seq)[:, None]
    bands = (HY_EMB - 1) // 2
    ang = (2.0 * math.pi * np.arange(seq) / seq)[:, None] * np.linspace(1e-4, bands - 1, bands)[None, :]
    feats = np.concatenate([t, np.cos(ang), -np.sin(ang)], -1)
    deltas = np.abs(np.linspace(math.log(HY_TARGET) / HY_SLOW_DECAY, math.log(HY_TARGET) / HY_FAST_DECAY,
                                HY_WIDTH))
    offset = np.abs(np.arange(seq) - seq // 2) / (seq // 2)
    window = np.exp(-offset[:, None] * deltas[None, :])
    feats = np.pad(feats, ((0, 0), (0, LANES - HY_EMB)))
    return jnp.asarray(feats, dtype=F32), jnp.asarray(window, dtype=F32)


def _hfilter_kernel(seq, feats_ref, win_ref, w1_ref, b1_ref, w2_ref, b2_ref, w3_ref, fr_ref, fwd_hi_ref,
                    fwd_lo_ref, ha_ref, hb_ref, hd_ref):
    fr = fr_ref[...]
    hid = jnp.sin(fr * (_mm_f32(feats_ref[...], w1_ref[...]) + b1_ref[...]))
    hid = jnp.sin(fr * (_mm_f32(hid, w2_ref[...]) + b2_ref[...]))
    filt = _mm_f32(hid, w3_ref[...]) * win_ref[...]
    filt = filt / (jnp.sum(jnp.abs(filt), axis=0, keepdims=True) + 1e-6)
    filt_hi = filt.astype(BF16)
    filt_lo = (filt - filt_hi.astype(F32)).astype(BF16)
    fwd_hi = fwd_hi_ref[...]
    spec = (jnp.dot(fwd_hi, filt_hi, preferred_element_type=F32)
            + jnp.dot(fwd_hi, filt_lo, preferred_element_type=F32)
            + jnp.dot(fwd_lo_ref[...], filt_hi, preferred_element_type=F32))
    h_re = spec[:seq, :]
    h_im = spec[seq:, :]
    first = _iota(h_re.shape, 0) == 0
    ha_ref[...] = h_re
    hb_ref[...] = jnp.where(first, 0.0, h_im)
    hd_ref[...] = jnp.where(first, h_im, h_re)


def _hfilter_call(seq, w1, b1, w2, b2, w3, freq, fwd_hi, fwd_lo):
    feats, window = _filter_feats(seq)
    w1p = jnp.pad(w1, ((0, LANES - HY_EMB), (0, 0)))
    full = lambda a: pl.BlockSpec(a.shape, lambda i: (0,) * a.ndim)
    args = [feats, window, w1p, b1.reshape(1, -1), w2, b2.reshape(1, -1), w3, freq.reshape(1, -1), fwd_hi,
            fwd_lo]
    out = jax.ShapeDtypeStruct((seq, HY_WIDTH), F32)
    return pl.pallas_call(
        functools.partial(_hfilter_kernel, seq),
        grid=(1,),
        in_specs=[full(a) for a in args],
        out_specs=[pl.BlockSpec((seq, HY_WIDTH), lambda i: (0, 0))] * 3,
        out_shape=[out, out, out],
        compiler_params=_cparams(1),
        name="hfilter",
    )(*args)


def _hyena_kernel(seq, reqs, pre_conv, *refs):
    if pre_conv:
        x0_ref, uu_ref, ha_ref, hb_ref, hd_ref, skip_ref, fwd_ref, inv_ref, o_ref = refs
    else:
        (x0_ref, x1_ref, v_ref, c0_ref, c1_ref, c2_ref, b0_ref, b1_ref, b2_ref,
         ha_ref, hb_ref, hd_ref, skip_ref, fwd_ref, inv_ref, o_ref) = refs
        row = _iota((seq, HY_CH_BLOCK), 0)
        first = row == 0
        last = row == seq - 1
    hb = hb_ref[...]
    for r in range(reqs):
        rows = slice(r * seq, (r + 1) * seq)
        if pre_conv:
            x0 = x0_ref[rows, :].astype(F32)
            uu = uu_ref[rows, :].astype(F32)
        else:
            conv = lambda x_ref, w_ref, b_ref: (_short_conv3(x_ref[rows, :].astype(F32), w_ref[...], first, last)
                                                + b_ref[...])
            x0 = conv(x0_ref, c0_ref, b0_ref)
            uu = conv(x1_ref, c1_ref, b1_ref) * conv(v_ref, c2_ref, b2_ref)
        spec = jnp.dot(fwd_ref[...], uu.astype(BF16), preferred_element_type=F32)
        u_re = spec[:seq, :]
        u_im = spec[seq:, :]
        y_re = u_re * ha_ref[...] - u_im * hb
        y_im = u_re * hb + u_im * hd_ref[...]
        y = jnp.concatenate([y_re, y_im], axis=0).astype(BF16)
        cv = jnp.dot(inv_ref[...], y, preferred_element_type=F32)
        o_ref[rows, :] = (x0 * (cv + uu * skip_ref[...])).astype(o_ref.dtype)


def _hyena_call(hy, pre_conv, conv_w, conv_b, ha, hb, hd, skip, fwd, inv, bsz, seq, act_dt):
    cb = HY_CH_BLOCK
    nblk = HY_WIDTH // cb
    reqs = max(1, min(bsz, HY_ROWS_PER_STEP // seq))
    assert bsz % reqs == 0
    tok = lambda off: pl.BlockSpec((reqs * seq, cb), lambda b, j: (b, off + j))
    cw = lambda off: pl.BlockSpec((3, cb), lambda b, j: (0, off + j))
    bias = lambda off: pl.BlockSpec((1, cb), lambda b, j: (0, off + j))
    ch = pl.BlockSpec((seq, cb), lambda b, j: (0, j))
    const = lambda a: pl.BlockSpec(a.shape, lambda b, j: (0, 0), pipeline_mode=pl.Buffered(1))
    tail_specs = [ch, ch, ch, pl.BlockSpec((1, cb), lambda b, j: (0, j)), const(fwd), const(inv)]
    tail_args = [ha, hb, hd, skip.reshape(1, -1), fwd, inv]
    if pre_conv:
        in_specs = [tok(0), tok(nblk)] + tail_specs
        args = [hy, hy] + tail_args
    else:
        conv_b2 = conv_b.reshape(1, -1)
        in_specs = [tok(0), tok(nblk), tok(2 * nblk), cw(0), cw(nblk), cw(2 * nblk),
                    bias(0), bias(nblk), bias(2 * nblk)] + tail_specs
        args = [hy, hy, hy, conv_w, conv_w, conv_w, conv_b2, conv_b2, conv_b2] + tail_args
    return pl.pallas_call(
        functools.partial(_hyena_kernel, seq, reqs, pre_conv),
        grid=(bsz // reqs, nblk),
        in_specs=in_specs,
        out_specs=tok(0),
        out_shape=jax.ShapeDtypeStruct((bsz * seq, HY_WIDTH), act_dt),
        compiler_params=_cparams(2),
        name="hyena",
    )(*args)


def _mix_kernel(has_pos, *refs):
    refs = list(refs)
    x_ref = refs.pop(0)
    pos_ref = refs.pop(0) if has_pos else None
    (oa_ref, ob_ref, g_ref, mod_ref, wua, wub, wout_half, l1g, l1b, wr_hi, wr_lo, x1_o, u2_o, p_o) = refs
    d = D_MODEL
    tm = x_ref.shape[0]
    sub = min(tm, MIX_SUB_ROWS)
    blocks = [slice(r0, r0 + sub) for r0 in range(0, tm, sub)]
    g1 = mod_ref[0, 2]
    sh2 = mod_ref[0, 3]
    sc2 = mod_ref[0, 4]

    mixed = []
    for rows in blocks:
        th = jnp.tanh(0.5 * g_ref[rows, :].astype(F32))
        a = jnp.dot(oa_ref[rows, :], wua[...], preferred_element_type=F32)
        b = jnp.dot(ob_ref[rows, :], wub[...], preferred_element_type=F32)
        up2 = (th[:, :d] * a + a) + (th[:, d:] * b + b)
        mixed.append(jnp.dot(up2.astype(BF16), wout_half[...], preferred_element_type=F32))

    for rows, mx in zip(blocks, mixed):
        x = x_ref[rows, :]
        if has_pos:
            x = x + pos_ref[rows, :]
        x1 = _ln(ALPHA * x + g1 * mx) * l1g[...] + l1b[...]
        x1_o[rows, :] = x1.astype(x1_o.dtype)
        u2 = _ln(x1) * (1.0 + sc2) + sh2
        u2_hi = u2.astype(BF16)
        u2_o[rows, :] = u2_hi
        u2_lo = (u2 - u2_hi.astype(F32)).astype(BF16)
        logits = (jnp.dot(u2_hi, wr_hi[...], preferred_element_type=F32)
                  + jnp.dot(u2_hi, wr_lo[...], preferred_element_type=F32)
                  + jnp.dot(u2_lo, wr_hi[...], preferred_element_type=F32))
        lane = _iota(logits.shape, 1)
        logits = jnp.where(lane < N_EXPERTS, logits, -jnp.inf)
        m = jnp.max(logits, axis=-1, keepdims=True)
        e = jnp.exp(logits - m)
        p_o[rows, :] = e / jnp.sum(e, axis=-1, keepdims=True)


def _mix_call(x2d, pos, o_a, o_b, gates, mod4, mod_row, wua, wub, wout_half, l1g, l1b, wr_hi, wr_lo):
    n, d = x2d.shape
    tm = min(MIX_TILE, max(n // MIX_MIN_STEPS, MIX_SUB_ROWS))
    assert n % tm == 0
    row = lambda i: (i, 0)
    const = lambda i: (0, 0)
    in_specs = [pl.BlockSpec((tm, d), row)]
    args = [x2d]
    if pos is not None:
        tiles = pos.shape[0] // tm
        in_specs.append(pl.BlockSpec((tm, d), lambda i: (i % tiles, 0)))
        args.append(pos)
    in_specs += [pl.BlockSpec((tm, DN_WIDTH), row), pl.BlockSpec((tm, HY_WIDTH), row),
                 pl.BlockSpec((tm, 2 * d), row),
                 _mod_spec(lambda i: mod_row(i * tm))]
    args += [o_a, o_b, gates, mod4]
    for w in (wua, wub, wout_half, l1g, l1b, wr_hi, wr_lo):
        in_specs.append(pl.BlockSpec(w.shape, const, pipeline_mode=pl.Buffered(1)))
        args.append(w)
    return pl.pallas_call(
        functools.partial(_mix_kernel, pos is not None),
        grid=(n // tm,),
        in_specs=in_specs,
        out_specs=[pl.BlockSpec((tm, d), row), pl.BlockSpec((tm, d), row), pl.BlockSpec((tm, LANES), row)],
        out_shape=[jax.ShapeDtypeStruct((n, d), ACT_DT), jax.ShapeDtypeStruct((n, d), BF16),
                   jax.ShapeDtypeStruct((n, LANES), F32)],
        compiler_params=_cparams(1),
        name="mix",
    )(*args)


def _route_kernel(seq, cap, reqs, p_ref, u_ref, tri_ref, xs_ref, slot_ref, pt_ref, g_s):
    e_n = N_EXPERTS
    tri = tri_ref[...]
    jrow = _iota((cap, seq), 0).astype(F32)
    if seq <= RANK_COUNT_MAX_SEQ:
        earlier = _iota((seq, seq), 0) < _iota((seq, seq), 1)
    for r in range(reqs):
        rows = slice(r * seq, (r + 1) * seq)
        p = p_ref[rows, :]
        pt = p.T[:e_n, :]

        if seq <= RANK_COUNT_MAX_SEQ:
            ranks = []
            for e in range(e_n):
                pc = p[:, e:e + 1]
                pr = pt[e:e + 1, :]
                beats = (pc > pr) | (earlier & (pc == pr))
                ranks.append(jnp.sum(jnp.where(beats, 1.0, 0.0), axis=0, keepdims=True))
            sel = jnp.concatenate(ranks, axis=0) < cap
        else:
            def search(i, cur):
                cand = cur | (1 << (30 - i))
                cand_f = pltpu.bitcast(cand, F32)
                cnt = jnp.sum(jnp.where(pt >= cand_f, 1.0, 0.0), axis=1, keepdims=True)
                return jnp.where((cnt >= cap) & (cand >= MIN_NORMAL_F32_BITS), cand, cur)

            thr = pltpu.bitcast(lax.fori_loop(0, 31, search, jnp.zeros((e_n, 1), jnp.int32)), F32)
            gt = pt > thr
            eq = pt == thr
            n_gt = jnp.sum(jnp.where(gt, 1.0, 0.0), axis=1, keepdims=True)
            eq_rank = jnp.dot(jnp.where(eq, 1.0, 0.0).astype(BF16), tri, preferred_element_type=F32)
            sel = gt | (eq & (eq_rank < cap - n_gt))
        pos = jnp.dot(jnp.where(sel, 1.0, 0.0).astype(BF16), tri, preferred_element_type=F32)
        slot = jnp.where(sel, pos, -1.0)
        slot_ref[r] = slot
        pt_ref[r] = pt
        for e in range(e_n):
            g_s[r, e * cap:(e + 1) * cap, :] = jnp.where(slot[e:e + 1, :] == jrow, 1.0, 0.0).astype(BF16)
        xs = jnp.dot(g_s[r], u_ref[rows, :], preferred_element_type=F32)
        for e in range(e_n):
            xs_ref[e, r * cap:(r + 1) * cap, :] = xs[e * cap:(e + 1) * cap, :].astype(xs_ref.dtype)


def _route_call(probs, u2, bsz, seq):
    cap = EC_FACTOR * seq // N_EXPERTS
    reqs = max(1, min(bsz, ROUTE_ROWS_PER_STEP // seq))
    assert bsz % reqs == 0
    tri = jnp.asarray(np.triu(np.ones((seq, seq), np.float32), 1), dtype=BF16)
    per_req = pl.BlockSpec((reqs, N_EXPERTS, seq), lambda b: (b, 0, 0))
    return pl.pallas_call(
        functools.partial(_route_kernel, seq, cap, reqs),
        grid=(bsz // reqs,),
        in_specs=[pl.BlockSpec((reqs * seq, LANES), lambda b: (b, 0)),
                  pl.BlockSpec((reqs * seq, D_MODEL), lambda b: (b, 0)),
                  pl.BlockSpec((seq, seq), lambda b: (0, 0), pipeline_mode=pl.Buffered(1))],
        out_specs=[pl.BlockSpec((N_EXPERTS, reqs * cap, D_MODEL), lambda b: (0, b, 0)), per_req, per_req],
        out_shape=[jax.ShapeDtypeStruct((N_EXPERTS, bsz * cap, D_MODEL), BF16),
                   jax.ShapeDtypeStruct((bsz, N_EXPERTS, seq), F32),
                   jax.ShapeDtypeStruct((bsz, N_EXPERTS, seq), F32)],
        scratch_shapes=[pltpu.VMEM((reqs, N_EXPERTS * cap, seq), BF16)],
        compiler_params=_cparams(1),
        name="route",
    )(probs, u2, tri)


def _expert_kernel(xc_ref, xd_ref, wg_ref, wu_ref, wd_ref, yc_ref, yd_ref):
    wg = wg_ref[0].astype(BF16)
    wu = wu_ref[0].astype(BF16)
    wd = wd_ref[0].astype(BF16)
    for x_ref, y_ref in ((xc_ref, yc_ref), (xd_ref, yd_ref)):
        rows = x_ref.shape[1]
        rb = min(rows, EXPERT_ROW_BLOCK)
        for r0 in range(0, rows, rb):
            x = x_ref[0, r0:r0 + rb, :]
            hg = jnp.dot(x, wg, preferred_element_type=F32)
            hu = jnp.dot(x, wu, preferred_element_type=F32)
            y = jnp.dot((_silu(hg) * hu).astype(BF16), wd, preferred_element_type=F32)
            y_ref[0, r0:r0 + rb, :] = y.astype(y_ref.dtype)


def _expert_call(xs_c, xs_d, w_gate, w_up, w_down):
    e_n, rc, d = xs_c.shape
    rd = xs_d.shape[1]
    ff = w_gate.shape[2]
    return pl.pallas_call(
        _expert_kernel,
        grid=(e_n,),
        in_specs=[pl.BlockSpec((1, rc, d), lambda e: (e, 0, 0)),
                  pl.BlockSpec((1, rd, d), lambda e: (e, 0, 0)),
                  pl.BlockSpec((1, d, ff), lambda e: (e, 0, 0)),
                  pl.BlockSpec((1, d, ff), lambda e: (e, 0, 0)),
                  pl.BlockSpec((1, ff, d), lambda e: (e, 0, 0))],
        out_specs=[pl.BlockSpec((1, rc, d), lambda e: (e, 0, 0)),
                   pl.BlockSpec((1, rd, d), lambda e: (e, 0, 0))],
        out_shape=[jax.ShapeDtypeStruct((e_n, rc, d), BF16), jax.ShapeDtypeStruct((e_n, rd, d), BF16)],
        compiler_params=_cparams(1),
        name="experts",
    )(xs_c, xs_d, w_gate, w_up, w_down)


def _final_kernel(seq, cap, reqs, x1_ref, y_ref, slot_ref, pt_ref, mod_ref, l2g, l2b, o_ref, w_s):
    jrow = _iota((cap, seq), 0).astype(F32)
    g2 = mod_ref[0, 5]
    for r in range(reqs):
        slot = slot_ref[r]
        pt = pt_ref[r]
        for e in range(N_EXPERTS):
            w_s[r, e * cap:(e + 1) * cap, :] = jnp.where(slot[e:e + 1, :] == jrow, pt[e:e + 1, :],
                                                         0.0).astype(BF16)
        y = y_ref[:, r * cap:(r + 1) * cap, :].reshape(N_EXPERTS * cap, y_ref.shape[-1])
        ffn = lax.dot_general(w_s[r], y, (((0,), (0,)), ((), ())), preferred_element_type=F32)
        rows = slice(r * seq, (r + 1) * seq)
        o_ref[rows, :] = _ln(ALPHA * x1_ref[rows, :].astype(F32) + g2 * ffn) * l2g[...] + l2b[...]


def _final_call(x1, ys, slot, pt, mod4, mod_row, l2g, l2b, bsz, seq, reqs):
    cap = EC_FACTOR * seq // N_EXPERTS
    d = D_MODEL
    per_req = pl.BlockSpec((reqs, N_EXPERTS, seq), lambda b: (b, 0, 0))
    return pl.pallas_call(
        functools.partial(_final_kernel, seq, cap, reqs),
        grid=(bsz // reqs,),
        in_specs=[pl.BlockSpec((reqs * seq, d), lambda b: (b, 0)),
                  pl.BlockSpec((N_EXPERTS, reqs * cap, d), lambda b: (0, b, 0)),
                  per_req, per_req,
                  _mod_spec(lambda b: mod_row(b * reqs * seq)),
                  pl.BlockSpec((1, d), lambda b: (0, 0)),
                  pl.BlockSpec((1, d), lambda b: (0, 0))],
        out_specs=pl.BlockSpec((reqs * seq, d), lambda b: (b, 0)),
        out_shape=jax.ShapeDtypeStruct((bsz * seq, d), F32),
        scratch_shapes=[pltpu.VMEM((reqs, N_EXPERTS * cap, seq), BF16)],
        compiler_params=_cparams(1),
        name="final",
    )(x1, ys, slot, pt, mod4, l2g, l2b)


def _grid_pos_embed(rows, dim):
    r = np.repeat(np.arange(rows), GRID_W)
    col = np.tile(np.arange(GRID_W), rows)
    quarter = dim // 4
    omega = 1.0 / (10000.0 ** (np.arange(quarter) / quarter))

    def enc(p):
        ang = p[:, None] * omega[None, :]
        return np.concatenate([np.sin(ang), np.cos(ang)], -1)

    return jnp.asarray(np.concatenate([enc(r), enc(col)], -1), dtype=F32)


def kernel(x_prompt, x_sample, state_delta_fwd, state_delta_bwd, c, c_ctx, w_mod, b_mod, w_in, conv_a_w, a_log, dt_bias, gnorm_w, conv_b_w, conv_b_b, hy_w1, hy_b1, hy_w2, hy_b2, hy_w3, hy_freq, hy_bias, w_up_a, w_up_b, w_out, ln1_g, ln1_b, w_router, w_e_gate, w_e_up, w_e_down, ln2_g, ln2_b):
    d = D_MODEL
    n_ctx, l_ctx, _ = x_prompt.shape
    n_dec, l_dec, _ = x_sample.shape
    lyr = 0

    cond_t = jnp.pad(jnp.concatenate([c_ctx[None, :], c], 0).T, ((0, 0), (0, 7 - n_dec)))
    mod4 = _mod_call(cond_t, 1 + n_dec, w_mod[lyr], b_mod[lyr]).reshape(8, N_MOD, 1, d)

    in_w = w_in[lyr].astype(BF16)

    prm = jnp.pad(jnp.stack([a_log[lyr].reshape(-1), dt_bias[lyr].reshape(-1)]),
                  ((0, 6), (2 * DN_HEADS, LANES - 4 * DN_HEADS)))
    gn = gnorm_w[lyr].reshape(1, -1)

    wua = w_up_a[lyr].astype(BF16)
    wub = w_up_b[lyr].astype(BF16)
    wout_half = (0.5 * w_out[lyr]).astype(BF16)
    wr = jnp.pad(w_router[lyr], ((0, 0), (0, LANES - N_EXPERTS)))
    wr_hi = wr.astype(BF16)
    wr_lo = (wr - wr_hi.astype(F32)).astype(BF16)
    l1g, l1b = ln1_g[lyr].reshape(1, -1), ln1_b[lyr].reshape(1, -1)
    l2g, l2b = ln2_g[lyr].reshape(1, -1), ln2_b[lyr].reshape(1, -1)

    def front(x2d, pos, bsz, seq, mod_row, s0f, s0b):
        (qkv, z, ba, hy, gates), pre_conv = _inproj_call(x2d, pos, mod4, mod_row, in_w, conv_a_w[lyr],
                                                          conv_b_w[lyr], conv_b_b[lyr], seq, ACT_DT)
        reqs = DN_REQS_PER_STEP if seq == DN_CHUNK and bsz % DN_REQS_PER_STEP == 0 else 1
        o_a, s_f, s_b = _deltanet_call(qkv, z, ba, conv_a_w[lyr], prm, gn, s0f, s0b, bsz, seq, reqs,
                                       pre_conv, ACT_DT)
        fwd, inv = (jnp.asarray(m) for m in _dft_mats(seq))
        fwd_hi = fwd.astype(BF16)
        fwd_lo = (fwd - fwd_hi.astype(F32)).astype(BF16)
        ha, hb, hd = _hfilter_call(seq, hy_w1[lyr], hy_b1[lyr], hy_w2[lyr], hy_b2[lyr], hy_w3[lyr],
                                   hy_freq[lyr], fwd_hi, fwd_lo)
        o_h = _hyena_call(hy, pre_conv, conv_b_w[lyr], conv_b_b[lyr], ha, hb, hd, hy_bias[lyr],
                          fwd_hi, inv.astype(BF16), bsz, seq, ACT_DT)
        x1, u2, probs = _mix_call(x2d, pos, o_a, o_h, gates, mod4, mod_row, wua, wub, wout_half, l1g, l1b,
                                   wr_hi, wr_lo)
        xs, slot, pt = _route_call(probs, u2, bsz, seq)
        return x1, xs, slot, pt, s_f, s_b

    row_ctx = lambda tok: 0
    row_dec = lambda tok: 1 + tok // l_dec

    xc = x_prompt.reshape(n_ctx * l_ctx, d)
    xd = x_sample.reshape(n_dec * l_dec, d)
    pos = _grid_pos_embed(l_dec // GRID_W, d)

    x1c, xsc, gc, ptc, s_f, s_b = front(xc, None, n_ctx, l_ctx, row_ctx, None, None)
    x1d, xsd, gd, ptd, _, _ = front(xd, pos, n_dec, l_dec, row_dec,
                                    state_delta_fwd[:, lyr], state_delta_bwd[:, lyr])
    yc, yd = _expert_call(xsc, xsd, w_e_gate[lyr], w_e_up[lyr], w_e_down[lyr])
    reqs_c = FINAL_CTX_REQS_PER_STEP if n_ctx % FINAL_CTX_REQS_PER_STEP == 0 else 1
    y_prompt = _final_call(x1c, yc, gc, ptc, mod4, row_ctx, l2g, l2b, n_ctx, l_ctx, reqs_c)
    y_sample = _final_call(x1d, yd, gd, ptd, mod4, row_dec, l2g, l2b, n_dec, l_dec, 1)

    return (y_prompt.reshape(n_ctx, l_ctx, d), y_sample.reshape(n_dec, l_dec, d),
            s_f[:, None], s_b[:, None])
```

```python
import functools
import math

import jax
import jax.numpy as jnp
import numpy as np
from jax import lax
from jax.experimental import pallas as pl
from jax.experimental.pallas import tpu as pltpu

F32 = jnp.float32
BF16 = jnp.bfloat16
HIGHEST = lax.Precision.HIGHEST

D_MODEL = 1024
DEPTH = 1
GRID_W = 64
DN_HEADS = 4
DN_DK = 128
DN_DV = 128
DN_WIDTH = DN_HEADS * DN_DV
HY_WIDTH = D_MODEL // 2
HY_EMB = 33
HY_FFN = 64
HY_TARGET = 1e-2
HY_FAST_DECAY = 0.3
HY_SLOW_DECAY = 1.5
N_EXPERTS = 16
EC_FACTOR = 2
N_MOD = 6
ALPHA = (2 * DEPTH) ** 0.25
LN_EPS = 1e-5
RMS_EPS = 1e-6
QKV_WIDTH = 2 * DN_HEADS * DN_DK + DN_WIDTH

LANES = 128
INPROJ_TILE = 1024
INPROJ_BLOCK_ROWS = 512
MIX_TILE = 1024
MIX_SUB_ROWS = 256
MIX_MIN_STEPS = 4
MOD_ROW_BLOCK = 128
DN_CHUNK = 256
INV_BASE = 8
DN_REQS_PER_STEP = 2
HY_CH_BLOCK = 512
HY_ROWS_PER_STEP = 1024
FINAL_CTX_REQS_PER_STEP = 4
EXPERT_ROW_BLOCK = 512
VMEM_LIMIT = 56 * 1024 * 1024
MIN_NORMAL_F32_BITS = 0x00800000
ACT_DT = BF16
ROUTE_ROWS_PER_STEP = 1024
RANK_COUNT_MAX_SEQ = 256


def _cparams(n_axes):
    return pltpu.CompilerParams(dimension_semantics=("arbitrary",) * n_axes,
                                vmem_limit_bytes=VMEM_LIMIT)


def _bmm(a, b):
    return jnp.einsum("bmk,bkn->bmn", a.astype(BF16), b.astype(BF16), preferred_element_type=F32)


def _bmm_nt(a, b):
    return jnp.einsum("bmk,bnk->bmn", a.astype(BF16), b.astype(BF16), preferred_element_type=F32)


def _bmm_tn(a, b):
    return jnp.einsum("bkm,bkn->bmn", a.astype(BF16), b.astype(BF16), preferred_element_type=F32)


def _mm_f32(a, b):
    return jnp.dot(a, b, precision=HIGHEST, preferred_element_type=F32)


def _sigmoid(x):
    return 0.5 * jnp.tanh(0.5 * x) + 0.5


def _silu(x):
    h = 0.5 * x
    return h * (jnp.tanh(h) + 1.0)


def _softplus(x):
    return jnp.maximum(x, 0.0) + jnp.log1p(jnp.exp(-jnp.abs(x)))


def _ln(x):
    mu = jnp.mean(x, axis=-1, keepdims=True)
    xc = x - mu
    var = jnp.mean(xc * xc, axis=-1, keepdims=True)
    return xc * lax.rsqrt(var + LN_EPS)


def _l2norm(x, scale=1.0):
    return x * (lax.rsqrt(jnp.sum(x * x, axis=-1, keepdims=True) + 1e-6) * scale)


def _iota(shape, dim):
    return lax.broadcasted_iota(jnp.int32, shape, dim)


def _short_conv3(x, w, first, last):
    xm = jnp.where(first, 0.0, pltpu.roll(x, 1, 0))
    xp = jnp.where(last, 0.0, pltpu.roll(x, x.shape[0] - 1, 0))
    return w[0:1, :] * xm + w[1:2, :] * x + w[2:3, :] * xp


def _mod_kernel(n_rows, ct_ref, w_ref, b_ref, o_ref, acc_s):
    k = pl.program_id(0)

    @pl.when(k == 0)
    def _():
        acc_s[...] = jnp.zeros(acc_s.shape, F32)

    s = _silu(ct_ref[...])
    rb, n_out = w_ref.shape
    w = w_ref[...].reshape(rb // 8, 8, n_out)
    for r in range(n_rows):
        acc_s[r] += jnp.sum(w * s[:, r:r + 1].reshape(rb // 8, 8, 1), axis=0)

    @pl.when(k == pl.num_programs(0) - 1)
    def _():
        o_ref[...] = jnp.zeros(o_ref.shape, F32)
        for r in range(n_rows):
            o_ref[r:r + 1, :] = jnp.sum(acc_s[r], axis=0, keepdims=True) + b_ref[...]


def _mod_call(cond_t, n_rows, w_mod, b_mod):
    d = D_MODEL
    rb = MOD_ROW_BLOCK
    n_out = N_MOD * d
    return pl.pallas_call(
        functools.partial(_mod_kernel, n_rows),
        grid=(d // rb,),
        in_specs=[pl.BlockSpec((rb, 8), lambda k: (k, 0)),
                  pl.BlockSpec((rb, n_out), lambda k: (k, 0)),
                  pl.BlockSpec((1, n_out), lambda k: (0, 0))],
        out_specs=pl.BlockSpec((8, n_out), lambda k: (0, 0)),
        out_shape=jax.ShapeDtypeStruct((8, n_out), F32),
        scratch_shapes=[pltpu.VMEM((n_rows, 8, n_out), F32)],
        compiler_params=_cparams(1),
        name="mod",
    )(cond_t, w_mod, b_mod.reshape(1, -1))


def _mod_spec(row_of_step):
    return pl.BlockSpec((1, N_MOD, 1, D_MODEL), lambda *idx: (row_of_step(*idx), 0, 0, 0))


def _inproj_kernel(has_pos, conv_seq, *refs):
    refs = list(refs)
    x_ref = refs.pop(0)
    pos_ref = refs.pop(0) if has_pos else None
    mod_ref, w_ref = refs.pop(0), refs.pop(0)
    if conv_seq:
        cwa_ref, cwb_ref, cbb_ref = refs.pop(0), refs.pop(0), refs.pop(0)
    qkv_o, z_o, ba_o, hy_o, g_o = refs
    sh1 = mod_ref[0, 0]
    sc1 = mod_ref[0, 1]
    o_z = QKV_WIDTH
    o_ba = o_z + DN_WIDTH
    n_ba = 4 * DN_HEADS
    o_g = n_ba + 3 * HY_WIDTH
    lanes = lambda j: slice(j * LANES, (j + 1) * LANES)
    tm = x_ref.shape[0]
    sub = min(tm, INPROJ_BLOCK_ROWS)
    blocks = [slice(r0, r0 + sub) for r0 in range(0, tm, sub)]

    if conv_seq:
        row = _iota((sub, LANES), 0) % conv_seq
        first = row == 0
        last = row == conv_seq - 1

    def project(rows):
        x = x_ref[rows, :]
        if has_pos:
            x = x + pos_ref[rows, :]
        u = (_ln(x) * (1.0 + sc1) + sh1).astype(BF16)
        return (jnp.dot(u, w_ref[:, :o_ba], preferred_element_type=F32),
                jnp.dot(u, w_ref[:, o_ba:], preferred_element_type=F32))

    def epilogue(rows, head, tail):
        z_o[rows, :] = head[:, o_z:].astype(z_o.dtype)
        ba_o[rows, :] = tail[:, :LANES]
        g_o[rows, :] = tail[:, o_g:].astype(g_o.dtype)
        hy = tail[:, n_ba:o_g]
        if not conv_seq:
            qkv_o[rows, :] = head[:, :o_z].astype(qkv_o.dtype)
            hy_o[rows, :] = hy.astype(hy_o.dtype)
            return
        for j in range(QKV_WIDTH // LANES):
            y = _silu(_short_conv3(head[:, lanes(j)], cwa_ref[:, lanes(j)], first, last))
            if j < 2 * DN_HEADS:
                y = _l2norm(y, DN_DK ** -0.5 if j < DN_HEADS else 1.0)
            qkv_o[rows, lanes(j)] = y.astype(qkv_o.dtype)
        nj = HY_WIDTH // LANES
        for j in range(nj):
            parts = [_short_conv3(hy[:, lanes(p * nj + j)], cwb_ref[:, lanes(p * nj + j)], first, last)
                     + cbb_ref[:, lanes(p * nj + j)] for p in range(3)]
            hy_o[rows, lanes(j)] = parts[0].astype(hy_o.dtype)
            hy_o[rows, lanes(nj + j)] = (parts[1] * parts[2]).astype(hy_o.dtype)

    pending = None
    for rows in blocks:
        cur = (rows, *project(rows))
        if pending is not None:
            epilogue(*pending)
        pending = cur
    epilogue(*pending)


def _inproj_call(x2d, pos, mod4, mod_row, w, conv_a_w, conv_b_w, conv_b_b, seq, act_dt):
    n, d = x2d.shape
    fuse = seq <= INPROJ_BLOCK_ROWS and INPROJ_TILE % seq == 0
    tm = min(INPROJ_TILE if fuse else INPROJ_BLOCK_ROWS, n)
    assert n % tm == 0 and (not fuse or tm % seq == 0)
    conv_seq = seq if fuse else None
    row = lambda i: (i, 0)
    const = lambda i: (0, 0)
    in_specs = [pl.BlockSpec((tm, d), row)]
    args = [x2d]
    if pos is not None:
        tiles = pos.shape[0] // tm
        in_specs.append(pl.BlockSpec((tm, d), lambda i: (i % tiles, 0)))
        args.append(pos)
    in_specs.append(_mod_spec(lambda i: mod_row(i * tm)))
    args.append(mod4)
    in_specs.append(pl.BlockSpec(w.shape, const, pipeline_mode=pl.Buffered(1)))
    args.append(w)
    if conv_seq:
        for a in (conv_a_w, conv_b_w, conv_b_b.reshape(1, -1)):
            in_specs.append(pl.BlockSpec(a.shape, const))
            args.append(a)
    hy_width = (2 if conv_seq else 3) * HY_WIDTH
    widths = (QKV_WIDTH, DN_WIDTH, LANES, hy_width, 2 * D_MODEL)
    dts = (act_dt, act_dt, F32, act_dt, act_dt)
    outs = pl.pallas_call(
        functools.partial(_inproj_kernel, pos is not None, conv_seq),
        grid=(n // tm,),
        in_specs=in_specs,
        out_specs=[pl.BlockSpec((tm, w), row) for w in widths],
        out_shape=[jax.ShapeDtypeStruct((n, w), dt) for w, dt in zip(widths, dts)],
        compiler_params=_cparams(1),
        name="inproj",
    )(*args)
    return outs, conv_seq is not None


def _inv_unit_tri(lm, ri, ci, block):
    c = lm.shape[-1]

    def same_block(h):
        sh = h.bit_length() - 1
        return (ri >> sh) == (ci >> sh)

    def unfold(f, h):
        return jnp.where(same_block(h), jnp.concatenate([f] * (c // h), axis=1), 0.0)

    base = INV_BASE
    n_nat = jnp.where(same_block(base), -lm, 0.0)
    n_fold = n_nat[:, 0:base, :]
    for i in range(1, c // base):
        n_fold = n_fold + n_nat[:, i * base:(i + 1) * base, :]
    eye_fold = jnp.where((_iota((base, c), 1) & (base - 1)) == _iota((base, c), 0), 1.0, 0.0)
    t = eye_fold + n_fold
    p_fold, p_nat = n_fold, n_nat
    span = 2
    while span < base:
        p_fold = _bmm(p_fold, p_nat)
        p_nat = unfold(p_fold, base)
        t = t + _bmm(t, p_nat)
        span *= 2
    s = base
    while s < block:
        sh = s.bit_length() - 1
        t_nat = unfold(t, s)
        even = ((_iota((s, c), 1) >> sh) & 1) == 0
        t2 = jnp.concatenate([jnp.where(even, t, 0.0), jnp.where(even, 0.0, t)], axis=1)
        lo = jnp.where(same_block(2 * s) & ~same_block(s), lm, 0.0)
        t = t2 - _bmm(_bmm(t2, lo), t_nat)
        s *= 2
    return t


def _dn_chunk(q, k, v, gc_col, gc_row, beta, e_col, e_rest, e_all, s_prev, same_chunk, ri, ci):
    c = DN_CHUNK
    nh = q.shape[0] // 2
    half = c // 2
    if same_chunk:
        gram = _bmm_nt(k[:nh], k[:nh])
        qk = _bmm_nt(q[:nh], k[:nh])
        gram = jnp.concatenate([gram, gram], 0)
        qk = jnp.concatenate([qk, qk], 0)
    else:
        gram = _bmm_nt(k, k)
        qk = _bmm_nt(q, k)
    diff = gc_col - gc_row
    incl = jnp.concatenate([jnp.broadcast_to(ri >= ci, (nh, c, c)), jnp.broadcast_to(ri <= ci, (nh, c, c))], 0)
    strict = incl & (ri != ci)
    decay = jnp.exp(jnp.where(incl, diff, -jnp.inf))
    lm = jnp.where(strict, beta * gram * decay, 0.0)
    a_intra = qk * decay
    t = _inv_unit_tri(lm, ri, ci, half)
    t1 = t[:, :, :half]
    t2 = t[:, :, half:]
    rhs = jnp.concatenate([v * beta, k * (beta * e_col)], axis=2)
    y1 = _bmm(t1, rhs[:, :half, :])
    y2 = _bmm(t2, rhs[:, half:, :])
    c_f = _bmm(lm[:nh, half:, :half], y1[:nh])
    c_b = _bmm(lm[nh:, :half, half:], y2[nh:])
    c_f = _bmm(t2[:nh], c_f)
    c_b = _bmm(t1[nh:], c_b)
    sol = jnp.concatenate([jnp.concatenate([y1[:nh], y2[:nh] - c_f], 1),
                           jnp.concatenate([y1[nh:] - c_b, y2[nh:]], 1)], 0)
    u = sol[:, :, :DN_DV]
    w = sol[:, :, DN_DV:]
    ks = k * e_rest
    if s_prev is None:
        v_new = u
        o = _bmm(a_intra, v_new)
        s_new = _bmm_tn(ks, v_new)
    else:
        v_new = u - _bmm(w, s_prev)
        o = _bmm(q * e_col, s_prev) + _bmm(a_intra, v_new)
        s_new = s_prev * e_all + _bmm_tn(ks, v_new)
    return o, s_new


def _deltanet_kernel(seq, reqs, zero_init, pre_conv, *refs):
    qkv_ref, cw_ref, z_ref, ba_ref, prm_ref, gn_ref = refs[:6]
    refs = refs[6:]
    if not zero_init:
        s0f_ref, s0b_ref = refs[:2]
        refs = refs[2:]
    o_ref, sf_ref, sb_ref, q_s, k_s, v_s, g_s, b_s, o_s, st_s = refs
    c = DN_CHUNK
    n = seq // c
    nh = DN_HEADS
    nf = reqs * nh
    lanes = lambda j: slice(j * LANES, (j + 1) * LANES)

    if not pre_conv:
        row = _iota((seq, LANES), 0)
        first_row = row == 0
        last_row = row == seq - 1
    for r in range(reqs):
        rows = slice(r * seq, (r + 1) * seq)

        def qkv_part(j):
            x = qkv_ref[rows, lanes(j)].astype(F32)
            return x if pre_conv else _silu(_short_conv3(x, cw_ref[:, lanes(j)], first_row, last_row))

        for h in range(nh):
            q = qkv_part(h)
            k = qkv_part(nh + h)
            q_s[r * nh + h] = q if pre_conv else _l2norm(q, DN_DK ** -0.5)
            k_s[r * nh + h] = k if pre_conv else _l2norm(k)
            v_s[r * nh + h] = qkv_part(2 * nh + h)

    ba = ba_ref[...]
    b_s[...] = _sigmoid(ba)
    g_s[...] = -jnp.exp(prm_ref[0:1, :]) * _softplus(ba + prm_ref[1:2, :])

    ri = _iota((c, c), 0)
    ci = _iota((c, c), 1)
    tri_l = jnp.where(ri >= ci, 1.0, 0.0).astype(F32)
    tri_u = jnp.where(ri <= ci, 1.0, 0.0).astype(F32)

    def chunk_pair(idx_f, idx_b, s_prev):
        def ds(idx, base=0):
            start = idx * c
            start = start if isinstance(start, int) else pl.multiple_of(start, c)
            return pl.ds(base + start, c)

        col_f, col_b, row_f, row_b, beta_f, beta_b = [], [], [], [], [], []
        for r in range(reqs):
            cs_f = _mm_f32(tri_l, g_s[ds(idx_f, r * seq), :])
            cs_b = _mm_f32(tri_u, g_s[ds(idx_b, r * seq), :])
            cst_f = cs_f.T
            cst_b = cs_b.T
            b_f = b_s[ds(idx_f, r * seq), :]
            b_b = b_s[ds(idx_b, r * seq), :]
            for h in range(nh):
                jf, jb = 2 * nh + h, 3 * nh + h
                col_f.append(cs_f[:, jf:jf + 1])
                col_b.append(cs_b[:, jb:jb + 1])
                row_f.append(cst_f[jf:jf + 1, :])
                row_b.append(cst_b[jb:jb + 1, :])
                beta_f.append(b_f[:, h:h + 1])
                beta_b.append(b_b[:, nh + h:nh + h + 1])
        gc_col = jnp.stack(col_f + col_b, 0)
        gc_row = jnp.stack(row_f + row_b, 0)
        beta = jnp.stack(beta_f + beta_b, 0)
        e_col = jnp.exp(gc_col)
        gl = jnp.concatenate([gc_col[:nf, c - 1:c, :], gc_col[nf:, 0:1, :]], 0)
        e_rest = jnp.exp(gl - gc_col)
        e_all = jnp.exp(gl)
        sl_f, sl_b = ds(idx_f), ds(idx_b)
        both = lambda ref: jnp.concatenate([ref[:, sl_f, :], ref[:, sl_b, :]], 0)
        o, s_new = _dn_chunk(both(q_s), both(k_s), both(v_s), gc_col, gc_row, beta, e_col, e_rest, e_all,
                             s_prev, n == 1, ri, ci)
        o_s[:nf, sl_f, :] = o[:nf]
        o_s[nf:, sl_b, :] = o[nf:]
        return s_new

    if not zero_init:
        s0 = jnp.concatenate([s0f_ref[r] for r in range(reqs)] + [s0b_ref[r] for r in range(reqs)], 0)
    if n == 1:
        s_fin = chunk_pair(0, 0, None if zero_init else s0)
    else:
        st_s[...] = jnp.zeros(st_s.shape, F32) if zero_init else s0

        def body(i, carry):
            st_s[...] = chunk_pair(i, n - 1 - i, st_s[...])
            return carry

        lax.fori_loop(0, n, body, 0)
        s_fin = st_s[...]
    for r in range(reqs):
        sf_ref[r] = s_fin[r * nh:(r + 1) * nh]
        sb_ref[r] = s_fin[nf + r * nh:nf + (r + 1) * nh]

    gn = gn_ref[...]
    for r in range(reqs):
        rows = slice(r * seq, (r + 1) * seq)
        for h in range(nh):
            o = o_s[r * nh + h] + o_s[nf + r * nh + h]
            o = o * lax.rsqrt(jnp.mean(o * o, axis=-1, keepdims=True) + RMS_EPS) * gn
            o_ref[rows, lanes(h)] = (o * _silu(z_ref[rows, lanes(h)].astype(F32))).astype(o_ref.dtype)


def _deltanet_call(qkv, z, ba, conv_w, prm, gn, s0f, s0b, bsz, seq, reqs, pre_conv, act_dt):
    nh = DN_HEADS
    zero_init = s0f is None
    rows = reqs * seq
    st = pl.BlockSpec((reqs, nh, DN_DK, DN_DV), lambda b: (b, 0, 0, 0))
    in_specs = [pl.BlockSpec((rows, QKV_WIDTH), lambda b: (b, 0)),
                pl.BlockSpec((3, QKV_WIDTH), lambda b: (0, 0)),
                pl.BlockSpec((rows, DN_WIDTH), lambda b: (b, 0)),
                pl.BlockSpec((rows, LANES), lambda b: (b, 0)),
                pl.BlockSpec((8, LANES), lambda b: (0, 0)),
                pl.BlockSpec((1, LANES), lambda b: (0, 0))]
    args = [qkv, conv_w, z, ba, prm, gn]
    if not zero_init:
        in_specs += [st, st]
        args += [s0f, s0b]
    vm = lambda shape: pltpu.VMEM(shape, F32)
    per_head = vm((reqs * nh, seq, LANES))
    return pl.pallas_call(
        functools.partial(_deltanet_kernel, seq, reqs, zero_init, pre_conv),
        grid=(bsz // reqs,),
        in_specs=in_specs,
        out_specs=[pl.BlockSpec((rows, DN_WIDTH), lambda b: (b, 0)), st, st],
        out_shape=[jax.ShapeDtypeStruct((bsz * seq, DN_WIDTH), act_dt),
                   jax.ShapeDtypeStruct((bsz, nh, DN_DK, DN_DV), F32),
                   jax.ShapeDtypeStruct((bsz, nh, DN_DK, DN_DV), F32)],
        scratch_shapes=[per_head, per_head, per_head, vm((rows, LANES)), vm((rows, LANES)),
                        vm((2 * reqs * nh, seq, LANES)), vm((2 * reqs * nh, DN_DK, DN_DV))],
        compiler_params=_cparams(1),
        name="deltanet",
    )(*args)


def _dft_mats(seq):
    n = 2 * seq
    f = np.arange(seq)[:, None]
    s = np.arange(seq)[None, :]
    ang = 2.0 * np.pi * ((f * s) % n) / n
    fwd = np.concatenate([np.cos(ang), -np.sin(ang)], axis=0)
    fwd[seq, :] = np.cos(np.pi * np.arange(seq))
    t = (np.arange(seq) + seq // 2)[:, None]
    ff = np.arange(seq)[None, :]
    ang2 = 2.0 * np.pi * ((t * ff) % n) / n
    inv_r = 2.0 * np.cos(ang2) / n
    inv_i = -2.0 * np.sin(ang2) / n
    inv_r[:, 0] = 1.0 / n
    inv_i[:, 0] = np.cos(np.pi * t[:, 0]) / n
    inv = np.concatenate([inv_r, inv_i], axis=1)
    return fwd.astype(np.float32), inv.astype(np.float32)


def _filter_feats(seq):
    t = np.linspace(0.0, 1.0, seq)[:, None]
    bands = (HY_EMB - 1) // 2
    ang = (2.0 * math.pi * np.arange(seq) / seq)[:, None] * np.linspace(1e-4, bands - 1, bands)[None, :]
    feats = np.concatenate([t, np.cos(ang), -np.sin(ang)], -1)
    deltas = np.abs(np.linspace(math.log(HY_TARGET) / HY_SLOW_DECAY, math.log(HY_TARGET) / HY_FAST_DECAY,
                                HY_WIDTH))
    offset = np.abs(np.arange(seq) - seq // 2) / (seq // 2)
    window = np.exp(-offset[:, None] * deltas[None, :])
    feats = np.pad(feats, ((0, 0), (0, LANES - HY_EMB)))
    return jnp.asarray(feats, dtype=F32), jnp.asarray(window, dtype=F32)


def _hfilter_kernel(seq, feats_ref, win_ref, w1_ref, b1_ref, w2_ref, b2_ref, w3_ref, fr_ref, fwd_hi_ref,
                    fwd_lo_ref, ha_ref, hb_ref, hd_ref):
    fr = fr_ref[...]
    hid = jnp.sin(fr * (_mm_f32(feats_ref[...], w1_ref[...]) + b1_ref[...]))
    hid = jnp.sin(fr * (_mm_f32(hid, w2_ref[...]) + b2_ref[...]))
    filt = _mm_f32(hid, w3_ref[...]) * win_ref[...]
    filt = filt / (jnp.sum(jnp.abs(filt), axis=0, keepdims=True) + 1e-6)
    filt_hi = filt.astype(BF16)
    filt_lo = (filt - filt_hi.astype(F32)).astype(BF16)
    fwd_hi = fwd_hi_ref[...]
    spec = (jnp.dot(fwd_hi, filt_hi, preferred_element_type=F32)
            + jnp.dot(fwd_hi, filt_lo, preferred_element_type=F32)
            + jnp.dot(fwd_lo_ref[...], filt_hi, preferred_element_type=F32))
    h_re = spec[:seq, :]
    h_im = spec[seq:, :]
    first = _iota(h_re.shape, 0) == 0
    ha_ref[...] = h_re
    hb_ref[...] = jnp.where(first, 0.0, h_im)
    hd_ref[...] = jnp.where(first, h_im, h_re)


def _hfilter_call(seq, w1, b1, w2, b2, w3, freq, fwd_hi, fwd_lo):
    feats, window = _filter_feats(seq)
    w1p = jnp.pad(w1, ((0, LANES - HY_EMB), (0, 0)))
    full = lambda a: pl.BlockSpec(a.shape, lambda i: (0,) * a.ndim)
    args = [feats, window, w1p, b1.reshape(1, -1), w2, b2.reshape(1, -1), w3, freq.reshape(1, -1), fwd_hi,
            fwd_lo]
    out = jax.ShapeDtypeStruct((seq, HY_WIDTH), F32)
    return pl.pallas_call(
        functools.partial(_hfilter_kernel, seq),
        grid=(1,),
        in_specs=[full(a) for a in args],
        out_specs=[pl.BlockSpec((seq, HY_WIDTH), lambda i: (0, 0))] * 3,
        out_shape=[out, out, out],
        compiler_params=_cparams(1),
        name="hfilter",
    )(*args)


def _hyena_kernel(seq, reqs, pre_conv, *refs):
    if pre_conv:
        x0_ref, uu_ref, ha_ref, hb_ref, hd_ref, skip_ref, fwd_ref, inv_ref, o_ref = refs
    else:
        (x0_ref, x1_ref, v_ref, c0_ref, c1_ref, c2_ref, b0_ref, b1_ref, b2_ref,
         ha_ref, hb_ref, hd_ref, skip_ref, fwd_ref, inv_ref, o_ref) = refs
        row = _iota((seq, HY_CH_BLOCK), 0)
        first = row == 0
        last = row == seq - 1
    hb = hb_ref[...]
    for r in range(reqs):
        rows = slice(r * seq, (r + 1) * seq)
        if pre_conv:
            x0 = x0_ref[rows, :].astype(F32)
            uu = uu_ref[rows, :].astype(F32)
        else:
            conv = lambda x_ref, w_ref, b_ref: (_short_conv3(x_ref[rows, :].astype(F32), w_ref[...], first, last)
                                                + b_ref[...])
            x0 = conv(x0_ref, c0_ref, b0_ref)
            uu = conv(x1_ref, c1_ref, b1_ref) * conv(v_ref, c2_ref, b2_ref)
        spec = jnp.dot(fwd_ref[...], uu.astype(BF16), preferred_element_type=F32)
        u_re = spec[:seq, :]
        u_im = spec[seq:, :]
        y_re = u_re * ha_ref[...] - u_im * hb
        y_im = u_re * hb + u_im * hd_ref[...]
        y = jnp.concatenate([y_re, y_im], axis=0).astype(BF16)
        cv = jnp.dot(inv_ref[...], y, preferred_element_type=F32)
        o_ref[rows, :] = (x0 * (cv + uu * skip_ref[...])).astype(o_ref.dtype)


def _hyena_call(hy, pre_conv, conv_w, conv_b, ha, hb, hd, skip, fwd, inv, bsz, seq, act_dt):
    cb = HY_CH_BLOCK
    nblk = HY_WIDTH // cb
    reqs = max(1, min(bsz, HY_ROWS_PER_STEP // seq))
    assert bsz % reqs == 0
    tok = lambda off: pl.BlockSpec((reqs * seq, cb), lambda b, j: (b, off + j))
    cw = lambda off: pl.BlockSpec((3, cb), lambda b, j: (0, off + j))
    bias = lambda off: pl.BlockSpec((1, cb), lambda b, j: (0, off + j))
    ch = pl.BlockSpec((seq, cb), lambda b, j: (0, j))
    const = lambda a: pl.BlockSpec(a.shape, lambda b, j: (0, 0), pipeline_mode=pl.Buffered(1))
    tail_specs = [ch, ch, ch, pl.BlockSpec((1, cb), lambda b, j: (0, j)), const(fwd), const(inv)]
    tail_args = [ha, hb, hd, skip.reshape(1, -1), fwd, inv]
    if pre_conv:
        in_specs = [tok(0), tok(nblk)] + tail_specs
        args = [hy, hy] + tail_args
    else:
        conv_b2 = conv_b.reshape(1, -1)
        in_specs = [tok(0), tok(nblk), tok(2 * nblk), cw(0), cw(nblk), cw(2 * nblk),
                    bias(0), bias(nblk), bias(2 * nblk)] + tail_specs
        args = [hy, hy, hy, conv_w, conv_w, conv_w, conv_b2, conv_b2, conv_b2] + tail_args
    return pl.pallas_call(
        functools.partial(_hyena_kernel, seq, reqs, pre_conv),
        grid=(bsz // reqs, nblk),
        in_specs=in_specs,
        out_specs=tok(0),
        out_shape=jax.ShapeDtypeStruct((bsz * seq, HY_WIDTH), act_dt),
        compiler_params=_cparams(2),
        name="hyena",
    )(*args)


def _mix_kernel(has_pos, *refs):
    refs = list(refs)
    x_ref = refs.pop(0)
    pos_ref = refs.pop(0) if has_pos else None
    (oa_ref, ob_ref, g_ref, mod_ref, wua, wub, wout_half, l1g, l1b, wr_hi, wr_lo, x1_o, u2_o, p_o) = refs
    d = D_MODEL
    tm = x_ref.shape[0]
    sub = min(tm, MIX_SUB_ROWS)
    blocks = [slice(r0, r0 + sub) for r0 in range(0, tm, sub)]
    g1 = mod_ref[0, 2]
    sh2 = mod_ref[0, 3]
    sc2 = mod_ref[0, 4]

    mixed = []
    for rows in blocks:
        th = jnp.tanh(0.5 * g_ref[rows, :].astype(F32))
        a = jnp.dot(oa_ref[rows, :], wua[...], preferred_element_type=F32)
        b = jnp.dot(ob_ref[rows, :], wub[...], preferred_element_type=F32)
        up2 = (th[:, :d] * a + a) + (th[:, d:] * b + b)
        mixed.append(jnp.dot(up2.astype(BF16), wout_half[...], preferred_element_type=F32))

    for rows, mx in zip(blocks, mixed):
        x = x_ref[rows, :]
        if has_pos:
            x = x + pos_ref[rows, :]
        x1 = _ln(ALPHA * x + g1 * mx) * l1g[...] + l1b[...]
        x1_o[rows, :] = x1.astype(x1_o.dtype)
        u2 = _ln(x1) * (1.0 + sc2) + sh2
        u2_hi = u2.astype(BF16)
        u2_o[rows, :] = u2_hi
        u2_lo = (u2 - u2_hi.astype(F32)).astype(BF16)
        logits = (jnp.dot(u2_hi, wr_hi[...], preferred_element_type=F32)
                  + jnp.dot(u2_hi, wr_lo[...], preferred_element_type=F32)
                  + jnp.dot(u2_lo, wr_hi[...], preferred_element_type=F32))
        lane = _iota(logits.shape, 1)
        logits = jnp.where(lane < N_EXPERTS, logits, -jnp.inf)
        m = jnp.max(logits, axis=-1, keepdims=True)
        e = jnp.exp(logits - m)
        p_o[rows, :] = e / jnp.sum(e, axis=-1, keepdims=True)


def _mix_call(x2d, pos, o_a, o_b, gates, mod4, mod_row, wua, wub, wout_half, l1g, l1b, wr_hi, wr_lo):
    n, d = x2d.shape
    tm = min(MIX_TILE, max(n // MIX_MIN_STEPS, MIX_SUB_ROWS))
    assert n % tm == 0
    row = lambda i: (i, 0)
    const = lambda i: (0, 0)
    in_specs = [pl.BlockSpec((tm, d), row)]
    args = [x2d]
    if pos is not None:
        tiles = pos.shape[0] // tm
        in_specs.append(pl.BlockSpec((tm, d), lambda i: (i % tiles, 0)))
        args.append(pos)
    in_specs += [pl.BlockSpec((tm, DN_WIDTH), row), pl.BlockSpec((tm, HY_WIDTH), row),
                 pl.BlockSpec((tm, 2 * d), row),
                 _mod_spec(lambda i: mod_row(i * tm))]
    args += [o_a, o_b, gates, mod4]
    for w in (wua, wub, wout_half, l1g, l1b, wr_hi, wr_lo):
        in_specs.append(pl.BlockSpec(w.shape, const, pipeline_mode=pl.Buffered(1)))
        args.append(w)
    return pl.pallas_call(
        functools.partial(_mix_kernel, pos is not None),
        grid=(n // tm,),
        in_specs=in_specs,
        out_specs=[pl.BlockSpec((tm, d), row), pl.BlockSpec((tm, d), row), pl.BlockSpec((tm, LANES), row)],
        out_shape=[jax.ShapeDtypeStruct((n, d), ACT_DT), jax.ShapeDtypeStruct((n, d), BF16),
                   jax.ShapeDtypeStruct((n, LANES), F32)],
        compiler_params=_cparams(1),
        name="mix",
    )(*args)


def _route_kernel(seq, cap, reqs, p_ref, u_ref, tri_ref, xs_ref, slot_ref, pt_ref, g_s):
    e_n = N_EXPERTS
    tri = tri_ref[...]
    jrow = _iota((cap, seq), 0).astype(F32)
    if seq <= RANK_COUNT_MAX_SEQ:
        earlier = _iota((seq, seq), 0) < _iota((seq, seq), 1)
    for r in range(reqs):
        rows = slice(r * seq, (r + 1) * seq)
        p = p_ref[rows, :]
        pt = p.T[:e_n, :]

        if seq <= RANK_COUNT_MAX_SEQ:
            ranks = []
            for e in range(e_n):
                pc = p[:, e:e + 1]
                pr = pt[e:e + 1, :]
                beats = (pc > pr) | (earlier & (pc == pr))
                ranks.append(jnp.sum(jnp.where(beats, 1.0, 0.0), axis=0, keepdims=True))
            sel = jnp.concatenate(ranks, axis=0) < cap
        else:
            def ok(cand):
                cnt = jnp.sum(jnp.where(pt >= pltpu.bitcast(cand, F32), 1.0, 0.0), axis=1, keepdims=True)
                return (cnt >= cap) & (cand >= MIN_NORMAL_F32_BITS)

            def search(i, cur):
                lo = 28 - 2 * i
                c1, c2, c3 = cur | (1 << lo), cur | (2 << lo), cur | (3 << lo)
                return jnp.where(ok(c3), c3, jnp.where(ok(c2), c2, jnp.where(ok(c1), c1, cur)))

            top = jnp.full((e_n, 1), 1 << 30, jnp.int32)
            cur = jnp.where(ok(top), top, 0)
            thr = pltpu.bitcast(lax.fori_loop(0, 15, search, cur), F32)
            gt = pt > thr
            eq = pt == thr
            n_gt = jnp.sum(jnp.where(gt, 1.0, 0.0), axis=1, keepdims=True)
            eq_rank = jnp.dot(jnp.where(eq, 1.0, 0.0).astype(BF16), tri, preferred_element_type=F32)
            sel = gt | (eq & (eq_rank < cap - n_gt))
        pos = jnp.dot(jnp.where(sel, 1.0, 0.0).astype(BF16), tri, preferred_element_type=F32)
        slot = jnp.where(sel, pos, -1.0)
        slot_ref[r] = slot
        pt_ref[r] = pt
        for e in range(e_n):
            g_s[r, e * cap:(e + 1) * cap, :] = jnp.where(slot[e:e + 1, :] == jrow, 1.0, 0.0).astype(BF16)
        xs = jnp.dot(g_s[r], u_ref[rows, :], preferred_element_type=F32)
        for e in range(e_n):
            xs_ref[e, r * cap:(r + 1) * cap, :] = xs[e * cap:(e + 1) * cap, :].astype(xs_ref.dtype)


def _route_call(probs, u2, bsz, seq):
    cap = EC_FACTOR * seq // N_EXPERTS
    reqs = max(1, min(bsz, ROUTE_ROWS_PER_STEP // seq))
    assert bsz % reqs == 0
    tri = jnp.asarray(np.triu(np.ones((seq, seq), np.float32), 1), dtype=BF16)
    per_req = pl.BlockSpec((reqs, N_EXPERTS, seq), lambda b: (b, 0, 0))
    return pl.pallas_call(
        functools.partial(_route_kernel, seq, cap, reqs),
        grid=(bsz // reqs,),
        in_specs=[pl.BlockSpec((reqs * seq, LANES), lambda b: (b, 0)),
                  pl.BlockSpec((reqs * seq, D_MODEL), lambda b: (b, 0)),
                  pl.BlockSpec((seq, seq), lambda b: (0, 0), pipeline_mode=pl.Buffered(1))],
        out_specs=[pl.BlockSpec((N_EXPERTS, reqs * cap, D_MODEL), lambda b: (0, b, 0)), per_req, per_req],
        out_shape=[jax.ShapeDtypeStruct((N_EXPERTS, bsz * cap, D_MODEL), BF16),
                   jax.ShapeDtypeStruct((bsz, N_EXPERTS, seq), F32),
                   jax.ShapeDtypeStruct((bsz, N_EXPERTS, seq), F32)],
        scratch_shapes=[pltpu.VMEM((reqs, N_EXPERTS * cap, seq), BF16)],
        compiler_params=_cparams(1),
        name="route",
    )(probs, u2, tri)


def _expert_kernel(xc_ref, xd_ref, wg_ref, wu_ref, wd_ref, yc_ref, yd_ref):
    wg = wg_ref[0].astype(BF16)
    wu = wu_ref[0].astype(BF16)
    wd = wd_ref[0].astype(BF16)
    for x_ref, y_ref in ((xc_ref, yc_ref), (xd_ref, yd_ref)):
        rows = x_ref.shape[1]
        rb = min(rows, EXPERT_ROW_BLOCK)
        for r0 in range(0, rows, rb):
            x = x_ref[0, r0:r0 + rb, :]
            hg = jnp.dot(x, wg, preferred_element_type=F32)
            hu = jnp.dot(x, wu, preferred_element_type=F32)
            y = jnp.dot((_silu(hg) * hu).astype(BF16), wd, preferred_element_type=F32)
            y_ref[0, r0:r0 + rb, :] = y.astype(y_ref.dtype)


def _expert_call(xs_c, xs_d, w_gate, w_up, w_down):
    e_n, rc, d = xs_c.shape
    rd = xs_d.shape[1]
    ff = w_gate.shape[2]
    return pl.pallas_call(
        _expert_kernel,
        grid=(e_n,),
        in_specs=[pl.BlockSpec((1, rc, d), lambda e: (e, 0, 0)),
                  pl.BlockSpec((1, rd, d), lambda e: (e, 0, 0)),
                  pl.BlockSpec((1, d, ff), lambda e: (e, 0, 0)),
                  pl.BlockSpec((1, d, ff), lambda e: (e, 0, 0)),
                  pl.BlockSpec((1, ff, d), lambda e: (e, 0, 0))],
        out_specs=[pl.BlockSpec((1, rc, d), lambda e: (e, 0, 0)),
                   pl.BlockSpec((1, rd, d), lambda e: (e, 0, 0))],
        out_shape=[jax.ShapeDtypeStruct((e_n, rc, d), BF16), jax.ShapeDtypeStruct((e_n, rd, d), BF16)],
        compiler_params=_cparams(1),
        name="experts",
    )(xs_c, xs_d, w_gate, w_up, w_down)


def _final_kernel(seq, cap, reqs, x1_ref, y_ref, slot_ref, pt_ref, mod_ref, l2g, l2b, o_ref, w_s):
    jrow = _iota((cap, seq), 0).astype(F32)
    g2 = mod_ref[0, 5]
    for r in range(reqs):
        slot = slot_ref[r]
        pt = pt_ref[r]
        for e in range(N_EXPERTS):
            w_s[r, e * cap:(e + 1) * cap, :] = jnp.where(slot[e:e + 1, :] == jrow, pt[e:e + 1, :],
                                                         0.0).astype(BF16)
        y = y_ref[:, r * cap:(r + 1) * cap, :].reshape(N_EXPERTS * cap, y_ref.shape[-1])
        ffn = lax.dot_general(w_s[r], y, (((0,), (0,)), ((), ())), preferred_element_type=F32)
        rows = slice(r * seq, (r + 1) * seq)
        o_ref[rows, :] = _ln(ALPHA * x1_ref[rows, :].astype(F32) + g2 * ffn) * l2g[...] + l2b[...]


def _final_call(x1, ys, slot, pt, mod4, mod_row, l2g, l2b, bsz, seq, reqs):
    cap = EC_FACTOR * seq // N_EXPERTS
    d = D_MODEL
    per_req = pl.BlockSpec((reqs, N_EXPERTS, seq), lambda b: (b, 0, 0))
    return pl.pallas_call(
        functools.partial(_final_kernel, seq, cap, reqs),
        grid=(bsz // reqs,),
        in_specs=[pl.BlockSpec((reqs * seq, d), lambda b: (b, 0)),
                  pl.BlockSpec((N_EXPERTS, reqs * cap, d), lambda b: (0, b, 0)),
                  per_req, per_req,
                  _mod_spec(lambda b: mod_row(b * reqs * seq)),
                  pl.BlockSpec((1, d), lambda b: (0, 0)),
                  pl.BlockSpec((1, d), lambda b: (0, 0))],
        out_specs=pl.BlockSpec((reqs * seq, d), lambda b: (b, 0)),
        out_shape=jax.ShapeDtypeStruct((bsz * seq, d), F32),
        scratch_shapes=[pltpu.VMEM((reqs, N_EXPERTS * cap, seq), BF16)],
        compiler_params=_cparams(1),
        name="final",
    )(x1, ys, slot, pt, mod4, l2g, l2b)


def _grid_pos_embed(rows, dim):
    r = np.repeat(np.arange(rows), GRID_W)
    col = np.tile(np.arange(GRID_W), rows)
    quarter = dim // 4
    omega = 1.0 / (10000.0 ** (np.arange(quarter) / quarter))

    def enc(p):
        ang = p[:, None] * omega[None, :]
        return np.concatenate([np.sin(ang), np.cos(ang)], -1)

    return jnp.asarray(np.concatenate([enc(r), enc(col)], -1), dtype=F32)


def kernel(x_prompt, x_sample, state_delta_fwd, state_delta_bwd, c, c_ctx, w_mod, b_mod, w_in, conv_a_w, a_log, dt_bias, gnorm_w, conv_b_w, conv_b_b, hy_w1, hy_b1, hy_w2, hy_b2, hy_w3, hy_freq, hy_bias, w_up_a, w_up_b, w_out, ln1_g, ln1_b, w_router, w_e_gate, w_e_up, w_e_down, ln2_g, ln2_b):
    d = D_MODEL
    n_ctx, l_ctx, _ = x_prompt.shape
    n_dec, l_dec, _ = x_sample.shape
    lyr = 0

    cond_t = jnp.pad(jnp.concatenate([c_ctx[None, :], c], 0).T, ((0, 0), (0, 7 - n_dec)))
    mod4 = _mod_call(cond_t, 1 + n_dec, w_mod[lyr], b_mod[lyr]).reshape(8, N_MOD, 1, d)

    in_w = w_in[lyr].astype(BF16)

    prm = jnp.pad(jnp.stack([a_log[lyr].reshape(-1), dt_bias[lyr].reshape(-1)]),
                  ((0, 6), (2 * DN_HEADS, LANES - 4 * DN_HEADS)))
    gn = gnorm_w[lyr].reshape(1, -1)

    wua = w_up_a[lyr].astype(BF16)
    wub = w_up_b[lyr].astype(BF16)
    wout_half = (0.5 * w_out[lyr]).astype(BF16)
    wr = jnp.pad(w_router[lyr], ((0, 0), (0, LANES - N_EXPERTS)))
    wr_hi = wr.astype(BF16)
    wr_lo = (wr - wr_hi.astype(F32)).astype(BF16)
    l1g, l1b = ln1_g[lyr].reshape(1, -1), ln1_b[lyr].reshape(1, -1)
    l2g, l2b = ln2_g[lyr].reshape(1, -1), ln2_b[lyr].reshape(1, -1)

    def front(x2d, pos, bsz, seq, mod_row, s0f, s0b):
        (qkv, z, ba, hy, gates), pre_conv = _inproj_call(x2d, pos, mod4, mod_row, in_w, conv_a_w[lyr],
                                                          conv_b_w[lyr], conv_b_b[lyr], seq, ACT_DT)
        reqs = DN_REQS_PER_STEP if seq == DN_CHUNK and bsz % DN_REQS_PER_STEP == 0 else 1
        o_a, s_f, s_b = _deltanet_call(qkv, z, ba, conv_a_w[lyr], prm, gn, s0f, s0b, bsz, seq, reqs,
                                       pre_conv, ACT_DT)
        fwd, inv = (jnp.asarray(m) for m in _dft_mats(seq))
        fwd_hi = fwd.astype(BF16)
        fwd_lo = (fwd - fwd_hi.astype(F32)).astype(BF16)
        ha, hb, hd = _hfilter_call(seq, hy_w1[lyr], hy_b1[lyr], hy_w2[lyr], hy_b2[lyr], hy_w3[lyr],
                                   hy_freq[lyr], fwd_hi, fwd_lo)
        o_h = _hyena_call(hy, pre_conv, conv_b_w[lyr], conv_b_b[lyr], ha, hb, hd, hy_bias[lyr],
                          fwd_hi, inv.astype(BF16), bsz, seq, ACT_DT)
        x1, u2, probs = _mix_call(x2d, pos, o_a, o_h, gates, mod4, mod_row, wua, wub, wout_half, l1g, l1b,
                                   wr_hi, wr_lo)
        xs, slot, pt = _route_call(probs, u2, bsz, seq)
        return x1, xs, slot, pt, s_f, s_b

    row_ctx = lambda tok: 0
    row_dec = lambda tok: 1 + tok // l_dec

    xc = x_prompt.reshape(n_ctx * l_ctx, d)
    xd = x_sample.reshape(n_dec * l_dec, d)
    pos = _grid_pos_embed(l_dec // GRID_W, d)

    x1c, xsc, gc, ptc, s_f, s_b = front(xc, None, n_ctx, l_ctx, row_ctx, None, None)
    x1d, xsd, gd, ptd, _, _ = front(xd, pos, n_dec, l_dec, row_dec,
                                    state_delta_fwd[:, lyr], state_delta_bwd[:, lyr])
    yc, yd = _expert_call(xsc, xsd, w_e_gate[lyr], w_e_up[lyr], w_e_down[lyr])
    reqs_c = FINAL_CTX_REQS_PER_STEP if n_ctx % FINAL_CTX_REQS_PER_STEP == 0 else 1
    y_prompt = _final_call(x1c, yc, gc, ptc, mod4, row_ctx, l2g, l2b, n_ctx, l_ctx, reqs_c)
    y_sample = _final_call(x1d, yd, gd, ptd, mod4, row_dec, l2g, l2b, n_dec, l_dec, 1)

    return (y_prompt.reshape(n_ctx, l_ctx, d), y_sample.reshape(n_dec, l_dec, d),
            s_f[:, None], s_b[:, None])
```

```python
import functools
import math

import jax
import jax.numpy as jnp
import numpy as np
from jax import lax
from jax.experimental import pallas as pl
from jax.experimental.pallas import tpu as pltpu

F32 = jnp.float32
BF16 = jnp.bfloat16
HIGHEST = lax.Precision.HIGHEST

D_MODEL = 1024
DEPTH = 1
GRID_W = 64
DN_HEADS = 4
DN_DK = 128
DN_DV = 128
DN_WIDTH = DN_HEADS * DN_DV
HY_WIDTH = D_MODEL // 2
HY_EMB = 33
HY_FFN = 64
HY_TARGET = 1e-2
HY_FAST_DECAY = 0.3
HY_SLOW_DECAY = 1.5
N_EXPERTS = 16
EC_FACTOR = 2
N_MOD = 6
ALPHA = (2 * DEPTH) ** 0.25
LN_EPS = 1e-5
RMS_EPS = 1e-6
QKV_WIDTH = 2 * DN_HEADS * DN_DK + DN_WIDTH

LANES = 128
INPROJ_TILE = 1024
INPROJ_BLOCK_ROWS = 512
MIX_TILE = 1024
MIX_SUB_ROWS = 256
MIX_MIN_STEPS = 4
MOD_COL_BLOCK = 1024
DN_CHUNK = 256
INV_BASE = 8
DN_REQS_PER_STEP = 2
HY_CH_BLOCK = 512
HY_ROWS_PER_STEP = 1024
FINAL_CTX_REQS_PER_STEP = 4
EXPERT_ROW_BLOCK = 512
VMEM_LIMIT = 56 * 1024 * 1024
MIN_NORMAL_F32_BITS = 0x00800000
ACT_DT = BF16
ROUTE_ROWS_PER_STEP = 1024
RANK_COUNT_MAX_SEQ = 256


def _cparams(n_axes):
    return pltpu.CompilerParams(dimension_semantics=("arbitrary",) * n_axes,
                                vmem_limit_bytes=VMEM_LIMIT)


def _bmm(a, b):
    return jnp.einsum("bmk,bkn->bmn", a.astype(BF16), b.astype(BF16), preferred_element_type=F32)


def _bmm_nt(a, b):
    return jnp.einsum("bmk,bnk->bmn", a.astype(BF16), b.astype(BF16), preferred_element_type=F32)


def _bmm_tn(a, b):
    return jnp.einsum("bkm,bkn->bmn", a.astype(BF16), b.astype(BF16), preferred_element_type=F32)


def _mm_f32(a, b):
    return jnp.dot(a, b, precision=HIGHEST, preferred_element_type=F32)


def _sigmoid(x):
    return 0.5 * jnp.tanh(0.5 * x) + 0.5


def _silu(x):
    h = 0.5 * x
    return h * (jnp.tanh(h) + 1.0)


def _softplus(x):
    return jnp.maximum(x, 0.0) + jnp.log1p(jnp.exp(-jnp.abs(x)))


def _ln(x):
    mu = jnp.mean(x, axis=-1, keepdims=True)
    xc = x - mu
    var = jnp.mean(xc * xc, axis=-1, keepdims=True)
    return xc * lax.rsqrt(var + LN_EPS)


def _l2norm(x, scale=1.0):
    return x * (lax.rsqrt(jnp.sum(x * x, axis=-1, keepdims=True) + 1e-6) * scale)


def _iota(shape, dim):
    return lax.broadcasted_iota(jnp.int32, shape, dim)


def _short_conv3(x, w, first, last):
    xm = jnp.where(first, 0.0, pltpu.roll(x, 1, 0))
    xp = jnp.where(last, 0.0, pltpu.roll(x, x.shape[0] - 1, 0))
    return w[0:1, :] * xm + w[1:2, :] * x + w[2:3, :] * xp


def _mod_kernel(n_rows, ct_ref, w_ref, b_ref, o_ref):
    s = _silu(ct_ref[...])
    w = w_ref[...]
    b = b_ref[...]
    o_ref[...] = jnp.zeros(o_ref.shape, F32)
    for r in range(n_rows):
        o_ref[r:r + 1, :] = jnp.sum(w * s[:, r:r + 1], axis=0, keepdims=True) + b


def _mod_call(cond_t, n_rows, w_mod, b_mod):
    d = D_MODEL
    tn = MOD_COL_BLOCK
    return pl.pallas_call(
        functools.partial(_mod_kernel, n_rows),
        grid=(N_MOD * d // tn,),
        in_specs=[pl.BlockSpec((d, 8), lambda j: (0, 0)),
                  pl.BlockSpec((d, tn), lambda j: (0, j)),
                  pl.BlockSpec((1, tn), lambda j: (0, j))],
        out_specs=pl.BlockSpec((8, tn), lambda j: (0, j)),
        out_shape=jax.ShapeDtypeStruct((8, N_MOD * d), F32),
        compiler_params=_cparams(1),
        name="mod",
    )(cond_t, w_mod, b_mod.reshape(1, -1))


def _mod_spec(row_of_step):
    return pl.BlockSpec((1, N_MOD, 1, D_MODEL), lambda *idx: (row_of_step(*idx), 0, 0, 0))


def _inproj_kernel(has_pos, conv_seq, *refs):
    refs = list(refs)
    x_ref = refs.pop(0)
    pos_ref = refs.pop(0) if has_pos else None
    mod_ref, w_ref = refs.pop(0), refs.pop(0)
    if conv_seq:
        cwa_ref, cwb_ref, cbb_ref = refs.pop(0), refs.pop(0), refs.pop(0)
    qkv_o, z_o, ba_o, hy_o, g_o = refs
    sh1 = mod_ref[0, 0]
    sc1 = mod_ref[0, 1]
    o_z = QKV_WIDTH
    o_ba = o_z + DN_WIDTH
    n_ba = 4 * DN_HEADS
    o_g = n_ba + 3 * HY_WIDTH
    lanes = lambda j: slice(j * LANES, (j + 1) * LANES)
    tm = x_ref.shape[0]
    sub = min(tm, INPROJ_BLOCK_ROWS)
    blocks = [slice(r0, r0 + sub) for r0 in range(0, tm, sub)]

    if conv_seq:
        row = _iota((sub, LANES), 0) % conv_seq
        first = row == 0
        last = row == conv_seq - 1

    def project(rows):
        x = x_ref[rows, :]
        if has_pos:
            x = x + pos_ref[rows, :]
        u = (_ln(x) * (1.0 + sc1) + sh1).astype(BF16)
        nt = (((1,), (1,)), ((), ()))
        return (lax.dot_general(u, w_ref[:o_ba, :], nt, preferred_element_type=F32),
                lax.dot_general(u, w_ref[o_ba:, :], nt, preferred_element_type=F32))

    def epilogue(rows, head, tail):
        z_o[rows, :] = head[:, o_z:].astype(z_o.dtype)
        ba_o[rows, :] = tail[:, :LANES]
        g_o[rows, :] = tail[:, o_g:].astype(g_o.dtype)
        hy = tail[:, n_ba:o_g]
        if not conv_seq:
            qkv_o[rows, :] = head[:, :o_z].astype(qkv_o.dtype)
            hy_o[rows, :] = hy.astype(hy_o.dtype)
            return
        for j in range(QKV_WIDTH // LANES):
            y = _silu(_short_conv3(head[:, lanes(j)], cwa_ref[:, lanes(j)], first, last))
            if j < 2 * DN_HEADS:
                y = _l2norm(y, DN_DK ** -0.5 if j < DN_HEADS else 1.0)
            qkv_o[rows, lanes(j)] = y.astype(qkv_o.dtype)
        nj = HY_WIDTH // LANES
        for j in range(nj):
            parts = [_short_conv3(hy[:, lanes(p * nj + j)], cwb_ref[:, lanes(p * nj + j)], first, last)
                     + cbb_ref[:, lanes(p * nj + j)] for p in range(3)]
            hy_o[rows, lanes(j)] = parts[0].astype(hy_o.dtype)
            hy_o[rows, lanes(nj + j)] = (parts[1] * parts[2]).astype(hy_o.dtype)

    pending = None
    for rows in blocks:
        cur = (rows, *project(rows))
        if pending is not None:
            epilogue(*pending)
        pending = cur
    epilogue(*pending)


def _inproj_call(x2d, pos, mod4, mod_row, w, conv_a_w, conv_b_w, conv_b_b, seq, act_dt):
    n, d = x2d.shape
    fuse = seq <= INPROJ_BLOCK_ROWS and INPROJ_TILE % seq == 0
    tm = min(INPROJ_TILE if fuse else INPROJ_BLOCK_ROWS, n)
    assert n % tm == 0 and (not fuse or tm % seq == 0)
    conv_seq = seq if fuse else None
    row = lambda i: (i, 0)
    const = lambda i: (0, 0)
    in_specs = [pl.BlockSpec((tm, d), row)]
    args = [x2d]
    if pos is not None:
        tiles = pos.shape[0] // tm
        in_specs.append(pl.BlockSpec((tm, d), lambda i: (i % tiles, 0)))
        args.append(pos)
    in_specs.append(_mod_spec(lambda i: mod_row(i * tm)))
    args.append(mod4)
    in_specs.append(pl.BlockSpec(w.shape, const, pipeline_mode=pl.Buffered(1)))
    args.append(w)
    if conv_seq:
        for a in (conv_a_w, conv_b_w, conv_b_b.reshape(1, -1)):
            in_specs.append(pl.BlockSpec(a.shape, const))
            args.append(a)
    hy_width = (2 if conv_seq else 3) * HY_WIDTH
    widths = (QKV_WIDTH, DN_WIDTH, LANES, hy_width, 2 * D_MODEL)
    dts = (act_dt, act_dt, F32, act_dt, act_dt)
    outs = pl.pallas_call(
        functools.partial(_inproj_kernel, pos is not None, conv_seq),
        grid=(n // tm,),
        in_specs=in_specs,
        out_specs=[pl.BlockSpec((tm, w), row) for w in widths],
        out_shape=[jax.ShapeDtypeStruct((n, w), dt) for w, dt in zip(widths, dts)],
        compiler_params=_cparams(1),
        name="inproj",
    )(*args)
    return outs, conv_seq is not None


def _inv_unit_tri(lm, ri, ci, block):
    c = lm.shape[-1]

    def same_block(h):
        sh = h.bit_length() - 1
        return (ri >> sh) == (ci >> sh)

    def unfold(f, h):
        return jnp.where(same_block(h), jnp.concatenate([f] * (c // h), axis=1), 0.0)

    base = INV_BASE
    n_nat = jnp.where(same_block(base), -lm, 0.0)
    n_fold = n_nat[:, 0:base, :]
    for i in range(1, c // base):
        n_fold = n_fold + n_nat[:, i * base:(i + 1) * base, :]
    eye_fold = jnp.where((_iota((base, c), 1) & (base - 1)) == _iota((base, c), 0), 1.0, 0.0)
    t = eye_fold + n_fold
    p_fold, p_nat = n_fold, n_nat
    span = 2
    while span < base:
        p_fold = _bmm(p_fold, p_nat)
        p_nat = unfold(p_fold, base)
        t = t + _bmm(t, p_nat)
        span *= 2
    s = base
    while s < block:
        sh = s.bit_length() - 1
        t_nat = unfold(t, s)
        even = ((_iota((s, c), 1) >> sh) & 1) == 0
        t2 = jnp.concatenate([jnp.where(even, t, 0.0), jnp.where(even, 0.0, t)], axis=1)
        lo = jnp.where(same_block(2 * s) & ~same_block(s), lm, 0.0)
        t = t2 - _bmm(_bmm(t2, lo), t_nat)
        s *= 2
    return t


def _dn_chunk(q, k, v, gc_col, gc_row, beta, e_col, e_rest, e_all, s_prev, same_chunk, ri, ci):
    c = DN_CHUNK
    nh = q.shape[0] // 2
    half = c // 2
    if same_chunk:
        gram = _bmm_nt(k[:nh], k[:nh])
        qk = _bmm_nt(q[:nh], k[:nh])
        gram = jnp.concatenate([gram, gram], 0)
        qk = jnp.concatenate([qk, qk], 0)
    else:
        gram = _bmm_nt(k, k)
        qk = _bmm_nt(q, k)
    diff = gc_col - gc_row
    incl = jnp.concatenate([jnp.broadcast_to(ri >= ci, (nh, c, c)), jnp.broadcast_to(ri <= ci, (nh, c, c))], 0)
    strict = incl & (ri != ci)
    decay = jnp.exp(jnp.where(incl, diff, -jnp.inf))
    lm = jnp.where(strict, beta * gram * decay, 0.0)
    a_intra = qk * decay
    t = _inv_unit_tri(lm, ri, ci, half)
    t1 = t[:, :, :half]
    t2 = t[:, :, half:]
    rhs = jnp.concatenate([v * beta, k * (beta * e_col)], axis=2)
    y1 = _bmm(t1, rhs[:, :half, :])
    y2 = _bmm(t2, rhs[:, half:, :])
    c_f = _bmm(lm[:nh, half:, :half], y1[:nh])
    c_b = _bmm(lm[nh:, :half, half:], y2[nh:])
    c_f = _bmm(t2[:nh], c_f)
    c_b = _bmm(t1[nh:], c_b)
    sol = jnp.concatenate([jnp.concatenate([y1[:nh], y2[:nh] - c_f], 1),
                           jnp.concatenate([y1[nh:] - c_b, y2[nh:]], 1)], 0)
    u = sol[:, :, :DN_DV]
    w = sol[:, :, DN_DV:]
    ks = k * e_rest
    if s_prev is None:
        v_new = u
        o = _bmm(a_intra, v_new)
        s_new = _bmm_tn(ks, v_new)
    else:
        v_new = u - _bmm(w, s_prev)
        o = _bmm(q * e_col, s_prev) + _bmm(a_intra, v_new)
        s_new = s_prev * e_all + _bmm_tn(ks, v_new)
    return o, s_new


def _deltanet_kernel(seq, reqs, zero_init, pre_conv, *refs):
    qkv_ref, cw_ref, z_ref, ba_ref, prm_ref, gn_ref = refs[:6]
    refs = refs[6:]
    if not zero_init:
        s0f_ref, s0b_ref = refs[:2]
        refs = refs[2:]
    o_ref, sf_ref, sb_ref, q_s, k_s, v_s, g_s, b_s, o_s, st_s = refs
    c = DN_CHUNK
    n = seq // c
    nh = DN_HEADS
    nf = reqs * nh
    lanes = lambda j: slice(j * LANES, (j + 1) * LANES)

    if not pre_conv:
        row = _iota((seq, LANES), 0)
        first_row = row == 0
        last_row = row == seq - 1
    for r in range(reqs):
        rows = slice(r * seq, (r + 1) * seq)

        def qkv_part(j):
            x = qkv_ref[rows, lanes(j)].astype(F32)
            return x if pre_conv else _silu(_short_conv3(x, cw_ref[:, lanes(j)], first_row, last_row))

        for h in range(nh):
            q = qkv_part(h)
            k = qkv_part(nh + h)
            q_s[r * nh + h] = q if pre_conv else _l2norm(q, DN_DK ** -0.5)
            k_s[r * nh + h] = k if pre_conv else _l2norm(k)
            v_s[r * nh + h] = qkv_part(2 * nh + h)

    ba = ba_ref[...]
    b_s[...] = _sigmoid(ba)
    g_s[...] = -jnp.exp(prm_ref[0:1, :]) * _softplus(ba + prm_ref[1:2, :])

    ri = _iota((c, c), 0)
    ci = _iota((c, c), 1)
    tri_l = jnp.where(ri >= ci, 1.0, 0.0).astype(F32)
    tri_u = jnp.where(ri <= ci, 1.0, 0.0).astype(F32)

    def chunk_pair(idx_f, idx_b, s_prev):
        def ds(idx, base=0):
            start = idx * c
            start = start if isinstance(start, int) else pl.multiple_of(start, c)
            return pl.ds(base + start, c)

        col_f, col_b, row_f, row_b, beta_f, beta_b = [], [], [], [], [], []
        for r in range(reqs):
            cs_f = _mm_f32(tri_l, g_s[ds(idx_f, r * seq), :])
            cs_b = _mm_f32(tri_u, g_s[ds(idx_b, r * seq), :])
            cst_f = cs_f.T
            cst_b = cs_b.T
            b_f = b_s[ds(idx_f, r * seq), :]
            b_b = b_s[ds(idx_b, r * seq), :]
            for h in range(nh):
                jf, jb = 2 * nh + h, 3 * nh + h
                col_f.append(cs_f[:, jf:jf + 1])
                col_b.append(cs_b[:, jb:jb + 1])
                row_f.append(cst_f[jf:jf + 1, :])
                row_b.append(cst_b[jb:jb + 1, :])
                beta_f.append(b_f[:, h:h + 1])
                beta_b.append(b_b[:, nh + h:nh + h + 1])
        gc_col = jnp.stack(col_f + col_b, 0)
        gc_row = jnp.stack(row_f + row_b, 0)
        beta = jnp.stack(beta_f + beta_b, 0)
        e_col = jnp.exp(gc_col)
        gl = jnp.concatenate([gc_col[:nf, c - 1:c, :], gc_col[nf:, 0:1, :]], 0)
        e_rest = jnp.exp(gl - gc_col)
        e_all = jnp.exp(gl)
        sl_f, sl_b = ds(idx_f), ds(idx_b)
        both = lambda ref: jnp.concatenate([ref[:, sl_f, :], ref[:, sl_b, :]], 0)
        o, s_new = _dn_chunk(both(q_s), both(k_s), both(v_s), gc_col, gc_row, beta, e_col, e_rest, e_all,
                             s_prev, n == 1, ri, ci)
        o_s[:nf, sl_f, :] = o[:nf]
        o_s[nf:, sl_b, :] = o[nf:]
        return s_new

    if not zero_init:
        s0 = jnp.concatenate([s0f_ref[r] for r in range(reqs)] + [s0b_ref[r] for r in range(reqs)], 0)
    if n == 1:
        s_fin = chunk_pair(0, 0, None if zero_init else s0)
    else:
        st_s[...] = jnp.zeros(st_s.shape, F32) if zero_init else s0

        def body(i, carry):
            st_s[...] = chunk_pair(i, n - 1 - i, st_s[...])
            return carry

        lax.fori_loop(0, n, body, 0)
        s_fin = st_s[...]
    for r in range(reqs):
        sf_ref[r] = s_fin[r * nh:(r + 1) * nh]
        sb_ref[r] = s_fin[nf + r * nh:nf + (r + 1) * nh]

    gn = gn_ref[...]
    for r in range(reqs):
        rows = slice(r * seq, (r + 1) * seq)
        for h in range(nh):
            o = o_s[r * nh + h] + o_s[nf + r * nh + h]
            o = o * lax.rsqrt(jnp.mean(o * o, axis=-1, keepdims=True) + RMS_EPS) * gn
            o_ref[rows, lanes(h)] = (o * _silu(z_ref[rows, lanes(h)].astype(F32))).astype(o_ref.dtype)


def _deltanet_call(qkv, z, ba, conv_w, prm, gn, s0f, s0b, bsz, seq, reqs, pre_conv, act_dt):
    nh = DN_HEADS
    zero_init = s0f is None
    rows = reqs * seq
    st = pl.BlockSpec((reqs, nh, DN_DK, DN_DV), lambda b: (b, 0, 0, 0))
    in_specs = [pl.BlockSpec((rows, QKV_WIDTH), lambda b: (b, 0)),
                pl.BlockSpec((3, QKV_WIDTH), lambda b: (0, 0)),
                pl.BlockSpec((rows, DN_WIDTH), lambda b: (b, 0)),
                pl.BlockSpec((rows, LANES), lambda b: (b, 0)),
                pl.BlockSpec((8, LANES), lambda b: (0, 0)),
                pl.BlockSpec((1, LANES), lambda b: (0, 0))]
    args = [qkv, conv_w, z, ba, prm, gn]
    if not zero_init:
        in_specs += [st, st]
        args += [s0f, s0b]
    vm = lambda shape: pltpu.VMEM(shape, F32)
    per_head = vm((reqs * nh, seq, LANES))
    return pl.pallas_call(
        functools.partial(_deltanet_kernel, seq, reqs, zero_init, pre_conv),
        grid=(bsz // reqs,),
        in_specs=in_specs,
        out_specs=[pl.BlockSpec((rows, DN_WIDTH), lambda b: (b, 0)), st, st],
        out_shape=[jax.ShapeDtypeStruct((bsz * seq, DN_WIDTH), act_dt),
                   jax.ShapeDtypeStruct((bsz, nh, DN_DK, DN_DV), F32),
                   jax.ShapeDtypeStruct((bsz, nh, DN_DK, DN_DV), F32)],
        scratch_shapes=[per_head, per_head, per_head, vm((rows, LANES)), vm((rows, LANES)),
                        vm((2 * reqs * nh, seq, LANES)), vm((2 * reqs * nh, DN_DK, DN_DV))],
        compiler_params=_cparams(1),
        name="deltanet",
    )(*args)


def _dft_mats(seq):
    n = 2 * seq
    f = np.arange(seq)[:, None]
    s = np.arange(seq)[None, :]
    ang = 2.0 * np.pi * ((f * s) % n) / n
    fwd = np.concatenate([np.cos(ang), -np.sin(ang)], axis=0)
    fwd[seq, :] = np.cos(np.pi * np.arange(seq))
    t = (np.arange(seq) + seq // 2)[:, None]
    ff = np.arange(seq)[None, :]
    ang2 = 2.0 * np.pi * ((t * ff) % n) / n
    inv_r = 2.0 * np.cos(ang2) / n
    inv_i = -2.0 * np.sin(ang2) / n
    inv_r[:, 0] = 1.0 / n
    inv_i[:, 0] = np.cos(np.pi * t[:, 0]) / n
    inv = np.concatenate([inv_r, inv_i], axis=1)
    return fwd.astype(np.float32), inv.astype(np.float32)


def _filter_feats(seq):
    t = np.linspace(0.0, 1.0, seq)[:, None]
    bands = (HY_EMB - 1) // 2
    ang = (2.0 * math.pi * np.arange(seq) / seq)[:, None] * np.linspace(1e-4, bands - 1, bands)[None, :]
    feats = np.concatenate([t, np.cos(ang), -np.sin(ang)], -1)
    deltas = np.abs(np.linspace(math.log(HY_TARGET) / HY_SLOW_DECAY, math.log(HY_TARGET) / HY_FAST_DECAY,
                                HY_WIDTH))
    offset = np.abs(np.arange(seq) - seq // 2) / (seq // 2)
    window = np.exp(-offset[:, None] * deltas[None, :])
    feats = np.pad(feats, ((0, 0), (0, LANES - HY_EMB)))
    return jnp.asarray(feats, dtype=F32), jnp.asarray(window, dtype=F32)


def _hfilter_kernel(seq, feats_ref, win_ref, w1_ref, b1_ref, w2_ref, b2_ref, w3_ref, fr_ref, fwd_hi_ref,
                    fwd_lo_ref, ha_ref, hb_ref, hd_ref):
    fr = fr_ref[...]
    hid = jnp.sin(fr * (_mm_f32(feats_ref[...], w1_ref[...]) + b1_ref[...]))
    hid = jnp.sin(fr * (_mm_f32(hid, w2_ref[...]) + b2_ref[...]))
    filt = _mm_f32(hid, w3_ref[...]) * win_ref[...]
    filt = filt / (jnp.sum(jnp.abs(filt), axis=0, keepdims=True) + 1e-6)
    filt_hi = filt.astype(BF16)
    filt_lo = (filt - filt_hi.astype(F32)).astype(BF16)
    fwd_hi = fwd_hi_ref[...]
    spec = (jnp.dot(fwd_hi, filt_hi, preferred_element_type=F32)
            + jnp.dot(fwd_hi, filt_lo, preferred_element_type=F32)
            + jnp.dot(fwd_lo_ref[...], filt_hi, preferred_element_type=F32))
    h_re = spec[:seq, :]
    h_im = spec[seq:, :]
    first = _iota(h_re.shape, 0) == 0
    ha_ref[...] = h_re
    hb_ref[...] = jnp.where(first, 0.0, h_im)
    hd_ref[...] = jnp.where(first, h_im, h_re)


def _hfilter_call(seq, w1, b1, w2, b2, w3, freq, fwd_hi, fwd_lo):
    feats, window = _filter_feats(seq)
    w1p = jnp.pad(w1, ((0, LANES - HY_EMB), (0, 0)))
    full = lambda a: pl.BlockSpec(a.shape, lambda i: (0,) * a.ndim)
    args = [feats, window, w1p, b1.reshape(1, -1), w2, b2.reshape(1, -1), w3, freq.reshape(1, -1), fwd_hi,
            fwd_lo]
    out = jax.ShapeDtypeStruct((seq, HY_WIDTH), F32)
    return pl.pallas_call(
        functools.partial(_hfilter_kernel, seq),
        grid=(1,),
        in_specs=[full(a) for a in args],
        out_specs=[pl.BlockSpec((seq, HY_WIDTH), lambda i: (0, 0))] * 3,
        out_shape=[out, out, out],
        compiler_params=_cparams(1),
        name="hfilter",
    )(*args)


def _hyena_kernel(seq, reqs, pre_conv, *refs):
    if pre_conv:
        x0_ref, uu_ref, ha_ref, hb_ref, hd_ref, skip_ref, fwd_ref, inv_ref, o_ref = refs
    else:
        (x0_ref, x1_ref, v_ref, c0_ref, c1_ref, c2_ref, b0_ref, b1_ref, b2_ref,
         ha_ref, hb_ref, hd_ref, skip_ref, fwd_ref, inv_ref, o_ref) = refs
        row = _iota((seq, HY_CH_BLOCK), 0)
        first = row == 0
        last = row == seq - 1
    hb = hb_ref[...]
    for r in range(reqs):
        rows = slice(r * seq, (r + 1) * seq)
        if pre_conv:
            x0 = x0_ref[rows, :].astype(F32)
            uu = uu_ref[rows, :].astype(F32)
        else:
            conv = lambda x_ref, w_ref, b_ref: (_short_conv3(x_ref[rows, :].astype(F32), w_ref[...], first, last)
                                                + b_ref[...])
            x0 = conv(x0_ref, c0_ref, b0_ref)
            uu = conv(x1_ref, c1_ref, b1_ref) * conv(v_ref, c2_ref, b2_ref)
        spec = jnp.dot(fwd_ref[...], uu.astype(BF16), preferred_element_type=F32)
        u_re = spec[:seq, :]
        u_im = spec[seq:, :]
        y_re = u_re * ha_ref[...] - u_im * hb
        y_im = u_re * hb + u_im * hd_ref[...]
        y = jnp.concatenate([y_re, y_im], axis=0).astype(BF16)
        cv = jnp.dot(inv_ref[...], y, preferred_element_type=F32)
        o_ref[rows, :] = (x0 * (cv + uu * skip_ref[...])).astype(o_ref.dtype)


def _hyena_call(hy, pre_conv, conv_w, conv_b, ha, hb, hd, skip, fwd, inv, bsz, seq, act_dt):
    cb = HY_CH_BLOCK
    nblk = HY_WIDTH // cb
    reqs = max(1, min(bsz, HY_ROWS_PER_STEP // seq))
    assert bsz % reqs == 0
    tok = lambda off: pl.BlockSpec((reqs * seq, cb), lambda b, j: (b, off + j))
    cw = lambda off: pl.BlockSpec((3, cb), lambda b, j: (0, off + j))
    bias = lambda off: pl.BlockSpec((1, cb), lambda b, j: (0, off + j))
    ch = pl.BlockSpec((seq, cb), lambda b, j: (0, j))
    const = lambda a: pl.BlockSpec(a.shape, lambda b, j: (0, 0), pipeline_mode=pl.Buffered(1))
    tail_specs = [ch, ch, ch, pl.BlockSpec((1, cb), lambda b, j: (0, j)), const(fwd), const(inv)]
    tail_args = [ha, hb, hd, skip.reshape(1, -1), fwd, inv]
    if pre_conv:
        in_specs = [tok(0), tok(nblk)] + tail_specs
        args = [hy, hy] + tail_args
    else:
        conv_b2 = conv_b.reshape(1, -1)
        in_specs = [tok(0), tok(nblk), tok(2 * nblk), cw(0), cw(nblk), cw(2 * nblk),
                    bias(0), bias(nblk), bias(2 * nblk)] + tail_specs
        args = [hy, hy, hy, conv_w, conv_w, conv_w, conv_b2, conv_b2, conv_b2] + tail_args
    return pl.pallas_call(
        functools.partial(_hyena_kernel, seq, reqs, pre_conv),
        grid=(bsz // reqs, nblk),
        in_specs=in_specs,
        out_specs=tok(0),
        out_shape=jax.ShapeDtypeStruct((bsz * seq, HY_WIDTH), act_dt),
        compiler_params=_cparams(2),
        name="hyena",
    )(*args)


def _mix_kernel(has_pos, *refs):
    refs = list(refs)
    x_ref = refs.pop(0)
    pos_ref = refs.pop(0) if has_pos else None
    (oa_ref, ob_ref, g_ref, mod_ref, wua, wub, wout_half, l1g, l1b, wr_hi, wr_lo, x1_o, u2_o, p_o) = refs
    d = D_MODEL
    tm = x_ref.shape[0]
    sub = min(tm, MIX_SUB_ROWS)
    blocks = [slice(r0, r0 + sub) for r0 in range(0, tm, sub)]
    g1 = mod_ref[0, 2]
    sh2 = mod_ref[0, 3]
    sc2 = mod_ref[0, 4]

    mixed = []
    for rows in blocks:
        th = jnp.tanh(0.5 * g_ref[rows, :].astype(F32))
        a = jnp.dot(oa_ref[rows, :], wua[...], preferred_element_type=F32)
        b = jnp.dot(ob_ref[rows, :], wub[...], preferred_element_type=F32)
        up2 = (th[:, :d] * a + a) + (th[:, d:] * b + b)
        mixed.append(jnp.dot(up2.astype(BF16), wout_half[...], preferred_element_type=F32))

    for rows, mx in zip(blocks, mixed):
        x = x_ref[rows, :]
        if has_pos:
            x = x + pos_ref[rows, :]
        x1 = _ln(ALPHA * x + g1 * mx) * l1g[...] + l1b[...]
        x1_o[rows, :] = x1.astype(x1_o.dtype)
        u2 = _ln(x1) * (1.0 + sc2) + sh2
        u2_hi = u2.astype(BF16)
        u2_o[rows, :] = u2_hi
        u2_lo = (u2 - u2_hi.astype(F32)).astype(BF16)
        logits = (jnp.dot(u2_hi, wr_hi[...], preferred_element_type=F32)
                  + jnp.dot(u2_hi, wr_lo[...], preferred_element_type=F32)
                  + jnp.dot(u2_lo, wr_hi[...], preferred_element_type=F32))
        lane = _iota(logits.shape, 1)
        logits = jnp.where(lane < N_EXPERTS, logits, -jnp.inf)
        m = jnp.max(logits, axis=-1, keepdims=True)
        e = jnp.exp(logits - m)
        p_o[rows, :] = e / jnp.sum(e, axis=-1, keepdims=True)


def _mix_call(x2d, pos, o_a, o_b, gates, mod4, mod_row, wua, wub, wout_half, l1g, l1b, wr_hi, wr_lo):
    n, d = x2d.shape
    tm = min(MIX_TILE, max(n // MIX_MIN_STEPS, MIX_SUB_ROWS))
    assert n % tm == 0
    row = lambda i: (i, 0)
    const = lambda i: (0, 0)
    in_specs = [pl.BlockSpec((tm, d), row)]
    args = [x2d]
    if pos is not None:
        tiles = pos.shape[0] // tm
        in_specs.append(pl.BlockSpec((tm, d), lambda i: (i % tiles, 0)))
        args.append(pos)
    in_specs += [pl.BlockSpec((tm, DN_WIDTH), row), pl.BlockSpec((tm, HY_WIDTH), row),
                 pl.BlockSpec((tm, 2 * d), row),
                 _mod_spec(lambda i: mod_row(i * tm))]
    args += [o_a, o_b, gates, mod4]
    for w in (wua, wub, wout_half, l1g, l1b, wr_hi, wr_lo):
        in_specs.append(pl.BlockSpec(w.shape, const, pipeline_mode=pl.Buffered(1)))
        args.append(w)
    return pl.pallas_call(
        functools.partial(_mix_kernel, pos is not None),
        grid=(n // tm,),
        in_specs=in_specs,
        out_specs=[pl.BlockSpec((tm, d), row), pl.BlockSpec((tm, d), row), pl.BlockSpec((tm, LANES), row)],
        out_shape=[jax.ShapeDtypeStruct((n, d), ACT_DT), jax.ShapeDtypeStruct((n, d), BF16),
                   jax.ShapeDtypeStruct((n, LANES), F32)],
        compiler_params=_cparams(1),
        name="mix",
    )(*args)


def _route_kernel(seq, cap, reqs, p_ref, u_ref, tri_ref, xs_ref, slot_ref, pt_ref, g_s):
    e_n = N_EXPERTS
    tri = tri_ref[...]
    jrow = _iota((cap, seq), 0).astype(F32)
    if seq <= RANK_COUNT_MAX_SEQ:
        earlier = _iota((seq, seq), 0) < _iota((seq, seq), 1)
    for r in range(reqs):
        rows = slice(r * seq, (r + 1) * seq)
        p = p_ref[rows, :]
        pt = p.T[:e_n, :]

        if seq <= RANK_COUNT_MAX_SEQ:
            ranks = []
            for e in range(e_n):
                pc = p[:, e:e + 1]
                pr = pt[e:e + 1, :]
                beats = (pc > pr) | (earlier & (pc == pr))
                ranks.append(jnp.sum(jnp.where(beats, 1.0, 0.0), axis=0, keepdims=True))
            sel = jnp.concatenate(ranks, axis=0) < cap
        else:
            def ok(cand):
                cnt = jnp.sum(jnp.where(pt >= pltpu.bitcast(cand, F32), 1.0, 0.0), axis=1, keepdims=True)
                return (cnt >= cap) & (cand >= MIN_NORMAL_F32_BITS)

            def search(i, cur):
                lo = 28 - 2 * i
                c1, c2, c3 = cur | (1 << lo), cur | (2 << lo), cur | (3 << lo)
                return jnp.where(ok(c3), c3, jnp.where(ok(c2), c2, jnp.where(ok(c1), c1, cur)))

            top = jnp.full((e_n, 1), 1 << 30, jnp.int32)
            cur = jnp.where(ok(top), top, 0)
            thr = pltpu.bitcast(lax.fori_loop(0, 15, search, cur), F32)
            gt = pt > thr
            eq = pt == thr
            n_gt = jnp.sum(jnp.where(gt, 1.0, 0.0), axis=1, keepdims=True)
            eq_rank = jnp.dot(jnp.where(eq, 1.0, 0.0).astype(BF16), tri, preferred_element_type=F32)
            sel = gt | (eq & (eq_rank < cap - n_gt))
        pos = jnp.dot(jnp.where(sel, 1.0, 0.0).astype(BF16), tri, preferred_element_type=F32)
        slot = jnp.where(sel, pos, -1.0)
        slot_ref[r] = slot
        pt_ref[r] = pt
        for e in range(e_n):
            g_s[r, e * cap:(e + 1) * cap, :] = jnp.where(slot[e:e + 1, :] == jrow, 1.0, 0.0).astype(BF16)
        xs = jnp.dot(g_s[r], u_ref[rows, :], preferred_element_type=F32)
        for e in range(e_n):
            xs_ref[e, r * cap:(r + 1) * cap, :] = xs[e * cap:(e + 1) * cap, :].astype(xs_ref.dtype)


def _route_call(probs, u2, bsz, seq):
    cap = EC_FACTOR * seq // N_EXPERTS
    reqs = max(1, min(bsz, ROUTE_ROWS_PER_STEP // seq))
    assert bsz % reqs == 0
    tri = jnp.asarray(np.triu(np.ones((seq, seq), np.float32), 1), dtype=BF16)
    per_req = pl.BlockSpec((reqs, N_EXPERTS, seq), lambda b: (b, 0, 0))
    return pl.pallas_call(
        functools.partial(_route_kernel, seq, cap, reqs),
        grid=(bsz // reqs,),
        in_specs=[pl.BlockSpec((reqs * seq, LANES), lambda b: (b, 0)),
                  pl.BlockSpec((reqs * seq, D_MODEL), lambda b: (b, 0)),
                  pl.BlockSpec((seq, seq), lambda b: (0, 0), pipeline_mode=pl.Buffered(1))],
        out_specs=[pl.BlockSpec((N_EXPERTS, reqs * cap, D_MODEL), lambda b: (0, b, 0)), per_req, per_req],
        out_shape=[jax.ShapeDtypeStruct((N_EXPERTS, bsz * cap, D_MODEL), BF16),
                   jax.ShapeDtypeStruct((bsz, N_EXPERTS, seq), F32),
                   jax.ShapeDtypeStruct((bsz, N_EXPERTS, seq), F32)],
        scratch_shapes=[pltpu.VMEM((reqs, N_EXPERTS * cap, seq), BF16)],
        compiler_params=_cparams(1),
        name="route",
    )(probs, u2, tri)


def _expert_kernel(xc_ref, xd_ref, wg_ref, wu_ref, wd_ref, yc_ref, yd_ref):
    wg = wg_ref[0].astype(BF16)
    wu = wu_ref[0].astype(BF16)
    wd = wd_ref[0].astype(BF16)
    for x_ref, y_ref in ((xc_ref, yc_ref), (xd_ref, yd_ref)):
        rows = x_ref.shape[1]
        rb = min(rows, EXPERT_ROW_BLOCK)
        for r0 in range(0, rows, rb):
            x = x_ref[0, r0:r0 + rb, :]
            hg = jnp.dot(x, wg, preferred_element_type=F32)
            hu = jnp.dot(x, wu, preferred_element_type=F32)
            y = jnp.dot((_silu(hg) * hu).astype(BF16), wd, preferred_element_type=F32)
            y_ref[0, r0:r0 + rb, :] = y.astype(y_ref.dtype)


def _expert_call(xs_c, xs_d, w_gate, w_up, w_down):
    e_n, rc, d = xs_c.shape
    rd = xs_d.shape[1]
    ff = w_gate.shape[2]
    return pl.pallas_call(
        _expert_kernel,
        grid=(e_n,),
        in_specs=[pl.BlockSpec((1, rc, d), lambda e: (e, 0, 0)),
                  pl.BlockSpec((1, rd, d), lambda e: (e, 0, 0)),
                  pl.BlockSpec((1, d, ff), lambda e: (e, 0, 0)),
                  pl.BlockSpec((1, d, ff), lambda e: (e, 0, 0)),
                  pl.BlockSpec((1, ff, d), lambda e: (e, 0, 0))],
        out_specs=[pl.BlockSpec((1, rc, d), lambda e: (e, 0, 0)),
                   pl.BlockSpec((1, rd, d), lambda e: (e, 0, 0))],
        out_shape=[jax.ShapeDtypeStruct((e_n, rc, d), BF16), jax.ShapeDtypeStruct((e_n, rd, d), BF16)],
        compiler_params=_cparams(1),
        name="experts",
    )(xs_c, xs_d, w_gate, w_up, w_down)


def _final_kernel(seq, cap, reqs, x1_ref, y_ref, slot_ref, pt_ref, mod_ref, l2g, l2b, o_ref, w_s):
    jrow = _iota((cap, seq), 0).astype(F32)
    g2 = mod_ref[0, 5]
    for r in range(reqs):
        slot = slot_ref[r]
        pt = pt_ref[r]
        for e in range(N_EXPERTS):
            w_s[r, e * cap:(e + 1) * cap, :] = jnp.where(slot[e:e + 1, :] == jrow, pt[e:e + 1, :],
                                                         0.0).astype(BF16)
        y = y_ref[:, r * cap:(r + 1) * cap, :].reshape(N_EXPERTS * cap, y_ref.shape[-1])
        ffn = lax.dot_general(w_s[r], y, (((0,), (0,)), ((), ())), preferred_element_type=F32)
        rows = slice(r * seq, (r + 1) * seq)
        o_ref[rows, :] = _ln(ALPHA * x1_ref[rows, :].astype(F32) + g2 * ffn) * l2g[...] + l2b[...]


def _final_call(x1, ys, slot, pt, mod4, mod_row, l2g, l2b, bsz, seq, reqs):
    cap = EC_FACTOR * seq // N_EXPERTS
    d = D_MODEL
    per_req = pl.BlockSpec((reqs, N_EXPERTS, seq), lambda b: (b, 0, 0))
    return pl.pallas_call(
        functools.partial(_final_kernel, seq, cap, reqs),
        grid=(bsz // reqs,),
        in_specs=[pl.BlockSpec((reqs * seq, d), lambda b: (b, 0)),
                  pl.BlockSpec((N_EXPERTS, reqs * cap, d), lambda b: (0, b, 0)),
                  per_req, per_req,
                  _mod_spec(lambda b: mod_row(b * reqs * seq)),
                  pl.BlockSpec((1, d), lambda b: (0, 0)),
                  pl.BlockSpec((1, d), lambda b: (0, 0))],
        out_specs=pl.BlockSpec((reqs * seq, d), lambda b: (b, 0)),
        out_shape=jax.ShapeDtypeStruct((bsz * seq, d), F32),
        scratch_shapes=[pltpu.VMEM((reqs, N_EXPERTS * cap, seq), BF16)],
        compiler_params=_cparams(1),
        name="final",
    )(x1, ys, slot, pt, mod4, l2g, l2b)


def _grid_pos_embed(rows, dim):
    r = np.repeat(np.arange(rows), GRID_W)
    col = np.tile(np.arange(GRID_W), rows)
    quarter = dim // 4
    omega = 1.0 / (10000.0 ** (np.arange(quarter) / quarter))

    def enc(p):
        ang = p[:, None] * omega[None, :]
        return np.concatenate([np.sin(ang), np.cos(ang)], -1)

    return jnp.asarray(np.concatenate([enc(r), enc(col)], -1), dtype=F32)


def kernel(x_prompt, x_sample, state_delta_fwd, state_delta_bwd, c, c_ctx, w_mod, b_mod, w_in, conv_a_w, a_log, dt_bias, gnorm_w, conv_b_w, conv_b_b, hy_w1, hy_b1, hy_w2, hy_b2, hy_w3, hy_freq, hy_bias, w_up_a, w_up_b, w_out, ln1_g, ln1_b, w_router, w_e_gate, w_e_up, w_e_down, ln2_g, ln2_b):
    d = D_MODEL
    n_ctx, l_ctx, _ = x_prompt.shape
    n_dec, l_dec, _ = x_sample.shape
    lyr = 0

    cond_t = jnp.pad(jnp.concatenate([c_ctx[None, :], c], 0).T, ((0, 0), (0, 7 - n_dec)))
    mod4 = _mod_call(cond_t, 1 + n_dec, w_mod[lyr], b_mod[lyr]).reshape(8, N_MOD, 1, d)

    in_w = jnp.swapaxes(w_in[lyr], 0, 1).astype(BF16)

    prm = jnp.pad(jnp.stack([a_log[lyr].reshape(-1), dt_bias[lyr].reshape(-1)]),
                  ((0, 6), (2 * DN_HEADS, LANES - 4 * DN_HEADS)))
    gn = gnorm_w[lyr].reshape(1, -1)

    wua = w_up_a[lyr].astype(BF16)
    wub = w_up_b[lyr].astype(BF16)
    wout_half = (0.5 * w_out[lyr]).astype(BF16)
    wr = jnp.pad(w_router[lyr], ((0, 0), (0, LANES - N_EXPERTS)))
    wr_hi = wr.astype(BF16)
    wr_lo = (wr - wr_hi.astype(F32)).astype(BF16)
    l1g, l1b = ln1_g[lyr].reshape(1, -1), ln1_b[lyr].reshape(1, -1)
    l2g, l2b = ln2_g[lyr].reshape(1, -1), ln2_b[lyr].reshape(1, -1)

    def front(x2d, pos, bsz, seq, mod_row, s0f, s0b):
        (qkv, z, ba, hy, gates), pre_conv = _inproj_call(x2d, pos, mod4, mod_row, in_w, conv_a_w[lyr],
                                                          conv_b_w[lyr], conv_b_b[lyr], seq, ACT_DT)
        reqs = DN_REQS_PER_STEP if seq == DN_CHUNK and bsz % DN_REQS_PER_STEP == 0 else 1
        o_a, s_f, s_b = _deltanet_call(qkv, z, ba, conv_a_w[lyr], prm, gn, s0f, s0b, bsz, seq, reqs,
                                       pre_conv, ACT_DT)
        fwd, inv = (jnp.asarray(m) for m in _dft_mats(seq))
        fwd_hi = fwd.astype(BF16)
        fwd_lo = (fwd - fwd_hi.astype(F32)).astype(BF16)
        ha, hb, hd = _hfilter_call(seq, hy_w1[lyr], hy_b1[lyr], hy_w2[lyr], hy_b2[lyr], hy_w3[lyr],
                                   hy_freq[lyr], fwd_hi, fwd_lo)
        o_h = _hyena_call(hy, pre_conv, conv_b_w[lyr], conv_b_b[lyr], ha, hb, hd, hy_bias[lyr],
                          fwd_hi, inv.astype(BF16), bsz, seq, ACT_DT)
        x1, u2, probs = _mix_call(x2d, pos, o_a, o_h, gates, mod4, mod_row, wua, wub, wout_half, l1g, l1b,
                                   wr_hi, wr_lo)
        xs, slot, pt = _route_call(probs, u2, bsz, seq)
        return x1, xs, slot, pt, s_f, s_b

    row_ctx = lambda tok: 0
    row_dec = lambda tok: 1 + tok // l_dec

    xc = x_prompt.reshape(n_ctx * l_ctx, d)
    xd = x_sample.reshape(n_dec * l_dec, d)
    pos = _grid_pos_embed(l_dec // GRID_W, d)

    x1c, xsc, gc, ptc, s_f, s_b = front(xc, None, n_ctx, l_ctx, row_ctx, None, None)
    x1d, xsd, gd, ptd, _, _ = front(xd, pos, n_dec, l_dec, row_dec,
                                    state_delta_fwd[:, lyr], state_delta_bwd[:, lyr])
    yc, yd = _expert_call(xsc, xsd, w_e_gate[lyr], w_e_up[lyr], w_e_down[lyr])
    reqs_c = FINAL_CTX_REQS_PER_STEP if n_ctx % FINAL_CTX_REQS_PER_STEP == 0 else 1
    y_prompt = _final_call(x1c, yc, gc, ptc, mod4, row_ctx, l2g, l2b, n_ctx, l_ctx, reqs_c)
    y_sample = _final_call(x1d, yd, gd, ptd, mod4, row_dec, l2g, l2b, n_dec, l_dec, 1)

    return (y_prompt.reshape(n_ctx, l_ctx, d), y_sample.reshape(n_dec, l_dec, d),
            s_f[:, None], s_b[:, None])
```

```python
import functools
import math

import jax
import jax.numpy as jnp
import numpy as np
from jax import lax
from jax.experimental import pallas as pl
from jax.experimental.pallas import tpu as pltpu

F32 = jnp.float32
BF16 = jnp.bfloat16
HIGHEST = lax.Precision.HIGHEST

D_MODEL = 1024
DEPTH = 1
GRID_W = 64
DN_HEADS = 4
DN_DK = 128
DN_DV = 128
DN_WIDTH = DN_HEADS * DN_DV
HY_WIDTH = D_MODEL // 2
HY_EMB = 33
HY_FFN = 64
HY_TARGET = 1e-2
HY_FAST_DECAY = 0.3
HY_SLOW_DECAY = 1.5
N_EXPERTS = 16
EC_FACTOR = 2
N_MOD = 6
ALPHA = (2 * DEPTH) ** 0.25
LN_EPS = 1e-5
RMS_EPS = 1e-6
QKV_WIDTH = 2 * DN_HEADS * DN_DK + DN_WIDTH

LANES = 128
INPROJ_TILE = 1024
INPROJ_BLOCK_ROWS = 512
MIX_TILE = 1024
MIX_SUB_ROWS = 256
MIX_MIN_STEPS = 4
MOD_COL_BLOCK = 1024
DN_CHUNK = 256
INV_BASE = 8
DN_REQS_PER_STEP = 4
HY_CH_BLOCK = 512
HY_ROWS_PER_STEP = 2048
FINAL_CTX_REQS_PER_STEP = 4
EXPERT_ROW_BLOCK = 512
VMEM_LIMIT = 56 * 1024 * 1024
MIN_NORMAL_F32_BITS = 0x00800000
ACT_DT = BF16
ROUTE_ROWS_PER_STEP = 2048
RANK_COUNT_MAX_SEQ = 256


def _cparams(n_axes):
    return pltpu.CompilerParams(dimension_semantics=("arbitrary",) * n_axes,
                                vmem_limit_bytes=VMEM_LIMIT)


def _bmm(a, b):
    return jnp.einsum("bmk,bkn->bmn", a.astype(BF16), b.astype(BF16), preferred_element_type=F32)


def _bmm_nt(a, b):
    return jnp.einsum("bmk,bnk->bmn", a.astype(BF16), b.astype(BF16), preferred_element_type=F32)


def _bmm_tn(a, b):
    return jnp.einsum("bkm,bkn->bmn", a.astype(BF16), b.astype(BF16), preferred_element_type=F32)


def _mm_f32(a, b):
    return jnp.dot(a, b, precision=HIGHEST, preferred_element_type=F32)


def _sigmoid(x):
    return 0.5 * jnp.tanh(0.5 * x) + 0.5


def _silu(x):
    h = 0.5 * x
    return h * (jnp.tanh(h) + 1.0)


def _softplus(x):
    return jnp.maximum(x, 0.0) + jnp.log1p(jnp.exp(-jnp.abs(x)))


def _ln(x):
    mu = jnp.mean(x, axis=-1, keepdims=True)
    xc = x - mu
    var = jnp.mean(xc * xc, axis=-1, keepdims=True)
    return xc * lax.rsqrt(var + LN_EPS)


def _l2norm(x, scale=1.0):
    return x * (lax.rsqrt(jnp.sum(x * x, axis=-1, keepdims=True) + 1e-6) * scale)


def _iota(shape, dim):
    return lax.broadcasted_iota(jnp.int32, shape, dim)


def _short_conv3(x, w, first, last):
    xm = jnp.where(first, 0.0, pltpu.roll(x, 1, 0))
    xp = jnp.where(last, 0.0, pltpu.roll(x, x.shape[0] - 1, 0))
    return w[0:1, :] * xm + w[1:2, :] * x + w[2:3, :] * xp


def _mod_kernel(n_rows, ct_ref, w_ref, b_ref, o_ref):
    s = _silu(ct_ref[...])
    w = w_ref[...]
    b = b_ref[...]
    o_ref[...] = jnp.zeros(o_ref.shape, F32)
    for r in range(n_rows):
        o_ref[r:r + 1, :] = jnp.sum(w * s[:, r:r + 1], axis=0, keepdims=True) + b


def _mod_call(cond_t, n_rows, w_mod, b_mod):
    d = D_MODEL
    tn = MOD_COL_BLOCK
    return pl.pallas_call(
        functools.partial(_mod_kernel, n_rows),
        grid=(N_MOD * d // tn,),
        in_specs=[pl.BlockSpec((d, 8), lambda j: (0, 0)),
                  pl.BlockSpec((d, tn), lambda j: (0, j)),
                  pl.BlockSpec((1, tn), lambda j: (0, j))],
        out_specs=pl.BlockSpec((8, tn), lambda j: (0, j)),
        out_shape=jax.ShapeDtypeStruct((8, N_MOD * d), F32),
        compiler_params=_cparams(1),
        name="mod",
    )(cond_t, w_mod, b_mod.reshape(1, -1))


def _mod_spec(row_of_step):
    return pl.BlockSpec((1, N_MOD, 1, D_MODEL), lambda *idx: (row_of_step(*idx), 0, 0, 0))


def _inproj_kernel(has_pos, conv_seq, *refs):
    refs = list(refs)
    x_ref = refs.pop(0)
    pos_ref = refs.pop(0) if has_pos else None
    mod_ref, w_ref = refs.pop(0), refs.pop(0)
    if conv_seq:
        cwa_ref, cwb_ref, cbb_ref = refs.pop(0), refs.pop(0), refs.pop(0)
    qkv_o, z_o, ba_o, hy_o, g_o = refs
    sh1 = mod_ref[0, 0]
    sc1 = mod_ref[0, 1]
    o_z = QKV_WIDTH
    o_ba = o_z + DN_WIDTH
    n_ba = 4 * DN_HEADS
    o_g = n_ba + 3 * HY_WIDTH
    lanes = lambda j: slice(j * LANES, (j + 1) * LANES)
    tm = x_ref.shape[0]
    sub = min(tm, INPROJ_BLOCK_ROWS)
    blocks = [slice(r0, r0 + sub) for r0 in range(0, tm, sub)]

    if conv_seq:
        row = _iota((sub, LANES), 0) % conv_seq
        first = row == 0
        last = row == conv_seq - 1

    def project(rows):
        x = x_ref[rows, :]
        if has_pos:
            x = x + pos_ref[rows, :]
        u = (_ln(x) * (1.0 + sc1) + sh1).astype(BF16)
        nt = (((1,), (1,)), ((), ()))
        return (lax.dot_general(u, w_ref[:o_ba, :], nt, preferred_element_type=F32),
                lax.dot_general(u, w_ref[o_ba:, :], nt, preferred_element_type=F32))

    def epilogue(rows, head, tail):
        z_o[rows, :] = head[:, o_z:].astype(z_o.dtype)
        ba_o[rows, :] = tail[:, :LANES]
        g_o[rows, :] = tail[:, o_g:].astype(g_o.dtype)
        hy = tail[:, n_ba:o_g]
        if not conv_seq:
            qkv_o[rows, :] = head[:, :o_z].astype(qkv_o.dtype)
            hy_o[rows, :] = hy.astype(hy_o.dtype)
            return
        for j in range(QKV_WIDTH // LANES):
            y = _silu(_short_conv3(head[:, lanes(j)], cwa_ref[:, lanes(j)], first, last))
            if j < 2 * DN_HEADS:
                y = _l2norm(y, DN_DK ** -0.5 if j < DN_HEADS else 1.0)
            qkv_o[rows, lanes(j)] = y.astype(qkv_o.dtype)
        nj = HY_WIDTH // LANES
        for j in range(nj):
            parts = [_short_conv3(hy[:, lanes(p * nj + j)], cwb_ref[:, lanes(p * nj + j)], first, last)
                     + cbb_ref[:, lanes(p * nj + j)] for p in range(3)]
            hy_o[rows, lanes(j)] = parts[0].astype(hy_o.dtype)
            hy_o[rows, lanes(nj + j)] = (parts[1] * parts[2]).astype(hy_o.dtype)

    pending = None
    for rows in blocks:
        cur = (rows, *project(rows))
        if pending is not None:
            epilogue(*pending)
        pending = cur
    epilogue(*pending)


def _inproj_call(x2d, pos, mod4, mod_row, w, conv_a_w, conv_b_w, conv_b_b, seq, act_dt):
    n, d = x2d.shape
    fuse = seq <= INPROJ_BLOCK_ROWS and INPROJ_TILE % seq == 0
    tm = min(INPROJ_TILE if fuse else INPROJ_BLOCK_ROWS, n)
    assert n % tm == 0 and (not fuse or tm % seq == 0)
    conv_seq = seq if fuse else None
    row = lambda i: (i, 0)
    const = lambda i: (0, 0)
    in_specs = [pl.BlockSpec((tm, d), row)]
    args = [x2d]
    if pos is not None:
        tiles = pos.shape[0] // tm
        in_specs.append(pl.BlockSpec((tm, d), lambda i: (i % tiles, 0)))
        args.append(pos)
    in_specs.append(_mod_spec(lambda i: mod_row(i * tm)))
    args.append(mod4)
    in_specs.append(pl.BlockSpec(w.shape, const, pipeline_mode=pl.Buffered(1)))
    args.append(w)
    if conv_seq:
        for a in (conv_a_w, conv_b_w, conv_b_b.reshape(1, -1)):
            in_specs.append(pl.BlockSpec(a.shape, const))
            args.append(a)
    hy_width = (2 if conv_seq else 3) * HY_WIDTH
    widths = (QKV_WIDTH, DN_WIDTH, LANES, hy_width, 2 * D_MODEL)
    dts = (act_dt, act_dt, F32, act_dt, act_dt)
    outs = pl.pallas_call(
        functools.partial(_inproj_kernel, pos is not None, conv_seq),
        grid=(n // tm,),
        in_specs=in_specs,
        out_specs=[pl.BlockSpec((tm, w), row) for w in widths],
        out_shape=[jax.ShapeDtypeStruct((n, w), dt) for w, dt in zip(widths, dts)],
        compiler_params=_cparams(1),
        name="inproj",
    )(*args)
    return outs, conv_seq is not None


def _inv_unit_tri(lm, ri, ci, block):
    c = lm.shape[-1]

    def same_block(h):
        sh = h.bit_length() - 1
        return (ri >> sh) == (ci >> sh)

    def unfold(f, h):
        return jnp.where(same_block(h), jnp.concatenate([f] * (c // h), axis=1), 0.0)

    base = INV_BASE
    n_nat = jnp.where(same_block(base), -lm, 0.0)
    n_fold = n_nat[:, 0:base, :]
    for i in range(1, c // base):
        n_fold = n_fold + n_nat[:, i * base:(i + 1) * base, :]
    eye_fold = jnp.where((_iota((base, c), 1) & (base - 1)) == _iota((base, c), 0), 1.0, 0.0)
    t = eye_fold + n_fold
    p_fold, p_nat = n_fold, n_nat
    span = 2
    while span < base:
        p_fold = _bmm(p_fold, p_nat)
        p_nat = unfold(p_fold, base)
        t = t + _bmm(t, p_nat)
        span *= 2
    s = base
    while s < block:
        sh = s.bit_length() - 1
        t_nat = unfold(t, s)
        even = ((_iota((s, c), 1) >> sh) & 1) == 0
        t2 = jnp.concatenate([jnp.where(even, t, 0.0), jnp.where(even, 0.0, t)], axis=1)
        lo = jnp.where(same_block(2 * s) & ~same_block(s), lm, 0.0)
        t = t2 - _bmm(_bmm(t2, lo), t_nat)
        s *= 2
    return t


def _dn_chunk(q, k, v, gc_col, gc_row, beta, e_col, e_rest, e_all, s_prev, same_chunk, ri, ci):
    c = DN_CHUNK
    nh = q.shape[0] // 2
    half = c // 2
    if same_chunk:
        gram = _bmm_nt(k[:nh], k[:nh])
        qk = _bmm_nt(q[:nh], k[:nh])
        gram = jnp.concatenate([gram, gram], 0)
        qk = jnp.concatenate([qk, qk], 0)
    else:
        gram = _bmm_nt(k, k)
        qk = _bmm_nt(q, k)
    diff = gc_col - gc_row
    incl = jnp.concatenate([jnp.broadcast_to(ri >= ci, (nh, c, c)), jnp.broadcast_to(ri <= ci, (nh, c, c))], 0)
    strict = incl & (ri != ci)
    decay = jnp.exp(jnp.where(incl, diff, -jnp.inf))
    lm = jnp.where(strict, beta * gram * decay, 0.0)
    a_intra = qk * decay
    t = _inv_unit_tri(lm, ri, ci, half)
    t1 = t[:, :, :half]
    t2 = t[:, :, half:]
    rhs = jnp.concatenate([v * beta, k * (beta * e_col)], axis=2)
    y1 = _bmm(t1, rhs[:, :half, :])
    y2 = _bmm(t2, rhs[:, half:, :])
    c_f = _bmm(lm[:nh, half:, :half], y1[:nh])
    c_b = _bmm(lm[nh:, :half, half:], y2[nh:])
    c_f = _bmm(t2[:nh], c_f)
    c_b = _bmm(t1[nh:], c_b)
    sol = jnp.concatenate([jnp.concatenate([y1[:nh], y2[:nh] - c_f], 1),
                           jnp.concatenate([y1[nh:] - c_b, y2[nh:]], 1)], 0)
    u = sol[:, :, :DN_DV]
    w = sol[:, :, DN_DV:]
    ks = k * e_rest
    if s_prev is None:
        v_new = u
        o = _bmm(a_intra, v_new)
        s_new = _bmm_tn(ks, v_new)
    else:
        v_new = u - _bmm(w, s_prev)
        o = _bmm(q * e_col, s_prev) + _bmm(a_intra, v_new)
        s_new = s_prev * e_all + _bmm_tn(ks, v_new)
    return o, s_new


def _deltanet_kernel(seq, reqs, zero_init, pre_conv, *refs):
    qkv_ref, cw_ref, z_ref, ba_ref, prm_ref, gn_ref = refs[:6]
    refs = refs[6:]
    if not zero_init:
        s0f_ref, s0b_ref = refs[:2]
        refs = refs[2:]
    o_ref, sf_ref, sb_ref, q_s, k_s, v_s, g_s, b_s, o_s, st_s = refs
    c = DN_CHUNK
    n = seq // c
    nh = DN_HEADS
    nf = reqs * nh
    lanes = lambda j: slice(j * LANES, (j + 1) * LANES)

    if not pre_conv:
        row = _iota((seq, LANES), 0)
        first_row = row == 0
        last_row = row == seq - 1
    for r in range(reqs):
        rows = slice(r * seq, (r + 1) * seq)

        def qkv_part(j):
            x = qkv_ref[rows, lanes(j)].astype(F32)
            return x if pre_conv else _silu(_short_conv3(x, cw_ref[:, lanes(j)], first_row, last_row))

        for h in range(nh):
            q = qkv_part(h)
            k = qkv_part(nh + h)
            q_s[r * nh + h] = q if pre_conv else _l2norm(q, DN_DK ** -0.5)
            k_s[r * nh + h] = k if pre_conv else _l2norm(k)
            v_s[r * nh + h] = qkv_part(2 * nh + h)

    ba = ba_ref[...]
    b_s[...] = _sigmoid(ba)
    g_s[...] = -jnp.exp(prm_ref[0:1, :]) * _softplus(ba + prm_ref[1:2, :])

    ri = _iota((c, c), 0)
    ci = _iota((c, c), 1)
    tri_l = jnp.where(ri >= ci, 1.0, 0.0).astype(F32)
    tri_u = jnp.where(ri <= ci, 1.0, 0.0).astype(F32)

    def chunk_pair(idx_f, idx_b, s_prev):
        def ds(idx, base=0):
            start = idx * c
            start = start if isinstance(start, int) else pl.multiple_of(start, c)
            return pl.ds(base + start, c)

        col_f, col_b, row_f, row_b, beta_f, beta_b = [], [], [], [], [], []
        for r in range(reqs):
            cs_f = _mm_f32(tri_l, g_s[ds(idx_f, r * seq), :])
            cs_b = _mm_f32(tri_u, g_s[ds(idx_b, r * seq), :])
            cst_f = cs_f.T
            cst_b = cs_b.T
            b_f = b_s[ds(idx_f, r * seq), :]
            b_b = b_s[ds(idx_b, r * seq), :]
            for h in range(nh):
                jf, jb = 2 * nh + h, 3 * nh + h
                col_f.append(cs_f[:, jf:jf + 1])
                col_b.append(cs_b[:, jb:jb + 1])
                row_f.append(cst_f[jf:jf + 1, :])
                row_b.append(cst_b[jb:jb + 1, :])
                beta_f.append(b_f[:, h:h + 1])
                beta_b.append(b_b[:, nh + h:nh + h + 1])
        gc_col = jnp.stack(col_f + col_b, 0)
        gc_row = jnp.stack(row_f + row_b, 0)
        beta = jnp.stack(beta_f + beta_b, 0)
        e_col = jnp.exp(gc_col)
        gl = jnp.concatenate([gc_col[:nf, c - 1:c, :], gc_col[nf:, 0:1, :]], 0)
        e_rest = jnp.exp(gl - gc_col)
        e_all = jnp.exp(gl)
        sl_f, sl_b = ds(idx_f), ds(idx_b)
        both = lambda ref: jnp.concatenate([ref[:, sl_f, :], ref[:, sl_b, :]], 0)
        o, s_new = _dn_chunk(both(q_s), both(k_s), both(v_s), gc_col, gc_row, beta, e_col, e_rest, e_all,
                             s_prev, n == 1, ri, ci)
        o_s[:nf, sl_f, :] = o[:nf]
        o_s[nf:, sl_b, :] = o[nf:]
        return s_new

    if not zero_init:
        s0 = jnp.concatenate([s0f_ref[r] for r in range(reqs)] + [s0b_ref[r] for r in range(reqs)], 0)
    if n == 1:
        s_fin = chunk_pair(0, 0, None if zero_init else s0)
    else:
        st_s[...] = jnp.zeros(st_s.shape, F32) if zero_init else s0

        def body(i, carry):
            st_s[...] = chunk_pair(i, n - 1 - i, st_s[...])
            return carry

        lax.fori_loop(0, n, body, 0)
        s_fin = st_s[...]
    for r in range(reqs):
        sf_ref[r] = s_fin[r * nh:(r + 1) * nh]
        sb_ref[r] = s_fin[nf + r * nh:nf + (r + 1) * nh]

    gn = gn_ref[...]
    for r in range(reqs):
        rows = slice(r * seq, (r + 1) * seq)
        for h in range(nh):
            o = o_s[r * nh + h] + o_s[nf + r * nh + h]
            o = o * lax.rsqrt(jnp.mean(o * o, axis=-1, keepdims=True) + RMS_EPS) * gn
            o_ref[rows, lanes(h)] = (o * _silu(z_ref[rows, lanes(h)].astype(F32))).astype(o_ref.dtype)


def _deltanet_call(qkv, z, ba, conv_w, prm, gn, s0f, s0b, bsz, seq, reqs, pre_conv, act_dt):
    nh = DN_HEADS
    zero_init = s0f is None
    rows = reqs * seq
    st = pl.BlockSpec((reqs, nh, DN_DK, DN_DV), lambda b: (b, 0, 0, 0))
    in_specs = [pl.BlockSpec((rows, QKV_WIDTH), lambda b: (b, 0)),
                pl.BlockSpec((3, QKV_WIDTH), lambda b: (0, 0)),
                pl.BlockSpec((rows, DN_WIDTH), lambda b: (b, 0)),
                pl.BlockSpec((rows, LANES), lambda b: (b, 0)),
                pl.BlockSpec((8, LANES), lambda b: (0, 0)),
                pl.BlockSpec((1, LANES), lambda b: (0, 0))]
    args = [qkv, conv_w, z, ba, prm, gn]
    if not zero_init:
        in_specs += [st, st]
        args += [s0f, s0b]
    vm = lambda shape: pltpu.VMEM(shape, F32)
    per_head = vm((reqs * nh, seq, LANES))
    return pl.pallas_call(
        functools.partial(_deltanet_kernel, seq, reqs, zero_init, pre_conv),
        grid=(bsz // reqs,),
        in_specs=in_specs,
        out_specs=[pl.BlockSpec((rows, DN_WIDTH), lambda b: (b, 0)), st, st],
        out_shape=[jax.ShapeDtypeStruct((bsz * seq, DN_WIDTH), act_dt),
                   jax.ShapeDtypeStruct((bsz, nh, DN_DK, DN_DV), F32),
                   jax.ShapeDtypeStruct((bsz, nh, DN_DK, DN_DV), F32)],
        scratch_shapes=[per_head, per_head, per_head, vm((rows, LANES)), vm((rows, LANES)),
                        vm((2 * reqs * nh, seq, LANES)), vm((2 * reqs * nh, DN_DK, DN_DV))],
        compiler_params=_cparams(1),
        name="deltanet",
    )(*args)


def _dft_mats(seq):
    n = 2 * seq
    f = np.arange(seq)[:, None]
    s = np.arange(seq)[None, :]
    ang = 2.0 * np.pi * ((f * s) % n) / n
    fwd = np.concatenate([np.cos(ang), -np.sin(ang)], axis=0)
    fwd[seq, :] = np.cos(np.pi * np.arange(seq))
    t = (np.arange(seq) + seq // 2)[:, None]
    ff = np.arange(seq)[None, :]
    ang2 = 2.0 * np.pi * ((t * ff) % n) / n
    inv_r = 2.0 * np.cos(ang2) / n
    inv_i = -2.0 * np.sin(ang2) / n
    inv_r[:, 0] = 1.0 / n
    inv_i[:, 0] = np.cos(np.pi * t[:, 0]) / n
    inv = np.concatenate([inv_r, inv_i], axis=1)
    return fwd.astype(np.float32), inv.astype(np.float32)


def _filter_feats(seq):
    t = np.linspace(0.0, 1.0, seq)[:, None]
    bands = (HY_EMB - 1) // 2
    ang = (2.0 * math.pi * np.arange(seq) / seq)[:, None] * np.linspace(1e-4, bands - 1, bands)[None, :]
    feats = np.concatenate([t, np.cos(ang), -np.sin(ang)], -1)
    deltas = np.abs(np.linspace(math.log(HY_TARGET) / HY_SLOW_DECAY, math.log(HY_TARGET) / HY_FAST_DECAY,
                                HY_WIDTH))
    offset = np.abs(np.arange(seq) - seq // 2) / (seq // 2)
    window = np.exp(-offset[:, None] * deltas[None, :])
    feats = np.pad(feats, ((0, 0), (0, LANES - HY_EMB)))
    return jnp.asarray(feats, dtype=F32), jnp.asarray(window, dtype=F32)


def _hfilter_kernel(seq, feats_ref, win_ref, w1_ref, b1_ref, w2_ref, b2_ref, w3_ref, fr_ref, fwd_hi_ref,
                    fwd_lo_ref, ha_ref, hb_ref, hd_ref):
    fr = fr_ref[...]
    hid = jnp.sin(fr * (_mm_f32(feats_ref[...], w1_ref[...]) + b1_ref[...]))
    hid = jnp.sin(fr * (_mm_f32(hid, w2_ref[...]) + b2_ref[...]))
    filt = _mm_f32(hid, w3_ref[...]) * win_ref[...]
    filt = filt / (jnp.sum(jnp.abs(filt), axis=0, keepdims=True) + 1e-6)
    filt_hi = filt.astype(BF16)
    filt_lo = (filt - filt_hi.astype(F32)).astype(BF16)
    fwd_hi = fwd_hi_ref[...]
    spec = (jnp.dot(fwd_hi, filt_hi, preferred_element_type=F32)
            + jnp.dot(fwd_hi, filt_lo, preferred_element_type=F32)
            + jnp.dot(fwd_lo_ref[...], filt_hi, preferred_element_type=F32))
    h_re = spec[:seq, :]
    h_im = spec[seq:, :]
    first = _iota(h_re.shape, 0) == 0
    ha_ref[...] = h_re
    hb_ref[...] = jnp.where(first, 0.0, h_im)
    hd_ref[...] = jnp.where(first, h_im, h_re)


def _hfilter_call(seq, w1, b1, w2, b2, w3, freq, fwd_hi, fwd_lo):
    feats, window = _filter_feats(seq)
    w1p = jnp.pad(w1, ((0, LANES - HY_EMB), (0, 0)))
    full = lambda a: pl.BlockSpec(a.shape, lambda i: (0,) * a.ndim)
    args = [feats, window, w1p, b1.reshape(1, -1), w2, b2.reshape(1, -1), w3, freq.reshape(1, -1), fwd_hi,
            fwd_lo]
    out = jax.ShapeDtypeStruct((seq, HY_WIDTH), F32)
    return pl.pallas_call(
        functools.partial(_hfilter_kernel, seq),
        grid=(1,),
        in_specs=[full(a) for a in args],
        out_specs=[pl.BlockSpec((seq, HY_WIDTH), lambda i: (0, 0))] * 3,
        out_shape=[out, out, out],
        compiler_params=_cparams(1),
        name="hfilter",
    )(*args)


def _hyena_kernel(seq, reqs, pre_conv, *refs):
    if pre_conv:
        x0_ref, uu_ref, ha_ref, hb_ref, hd_ref, skip_ref, fwd_ref, inv_ref, o_ref = refs
    else:
        (x0_ref, x1_ref, v_ref, c0_ref, c1_ref, c2_ref, b0_ref, b1_ref, b2_ref,
         ha_ref, hb_ref, hd_ref, skip_ref, fwd_ref, inv_ref, o_ref) = refs
        row = _iota((seq, HY_CH_BLOCK), 0)
        first = row == 0
        last = row == seq - 1
    hb = hb_ref[...]
    for r in range(reqs):
        rows = slice(r * seq, (r + 1) * seq)
        if pre_conv:
            x0 = x0_ref[rows, :].astype(F32)
            uu = uu_ref[rows, :].astype(F32)
        else:
            conv = lambda x_ref, w_ref, b_ref: (_short_conv3(x_ref[rows, :].astype(F32), w_ref[...], first, last)
                                                + b_ref[...])
            x0 = conv(x0_ref, c0_ref, b0_ref)
            uu = conv(x1_ref, c1_ref, b1_ref) * conv(v_ref, c2_ref, b2_ref)
        spec = jnp.dot(fwd_ref[...], uu.astype(BF16), preferred_element_type=F32)
        u_re = spec[:seq, :]
        u_im = spec[seq:, :]
        y_re = u_re * ha_ref[...] - u_im * hb
        y_im = u_re * hb + u_im * hd_ref[...]
        y = jnp.concatenate([y_re, y_im], axis=0).astype(BF16)
        cv = jnp.dot(inv_ref[...], y, preferred_element_type=F32)
        o_ref[rows, :] = (x0 * (cv + uu * skip_ref[...])).astype(o_ref.dtype)


def _hyena_call(hy, pre_conv, conv_w, conv_b, ha, hb, hd, skip, fwd, inv, bsz, seq, act_dt):
    cb = HY_CH_BLOCK
    nblk = HY_WIDTH // cb
    reqs = max(1, min(bsz // 2, HY_ROWS_PER_STEP // seq))
    assert bsz % reqs == 0
    tok = lambda off: pl.BlockSpec((reqs * seq, cb), lambda b, j: (b, off + j))
    cw = lambda off: pl.BlockSpec((3, cb), lambda b, j: (0, off + j))
    bias = lambda off: pl.BlockSpec((1, cb), lambda b, j: (0, off + j))
    ch = pl.BlockSpec((seq, cb), lambda b, j: (0, j))
    const = lambda a: pl.BlockSpec(a.shape, lambda b, j: (0, 0), pipeline_mode=pl.Buffered(1))
    tail_specs = [ch, ch, ch, pl.BlockSpec((1, cb), lambda b, j: (0, j)), const(fwd), const(inv)]
    tail_args = [ha, hb, hd, skip.reshape(1, -1), fwd, inv]
    if pre_conv:
        in_specs = [tok(0), tok(nblk)] + tail_specs
        args = [hy, hy] + tail_args
    else:
        conv_b2 = conv_b.reshape(1, -1)
        in_specs = [tok(0), tok(nblk), tok(2 * nblk), cw(0), cw(nblk), cw(2 * nblk),
                    bias(0), bias(nblk), bias(2 * nblk)] + tail_specs
        args = [hy, hy, hy, conv_w, conv_w, conv_w, conv_b2, conv_b2, conv_b2] + tail_args
    return pl.pallas_call(
        functools.partial(_hyena_kernel, seq, reqs, pre_conv),
        grid=(bsz // reqs, nblk),
        in_specs=in_specs,
        out_specs=tok(0),
        out_shape=jax.ShapeDtypeStruct((bsz * seq, HY_WIDTH), act_dt),
        compiler_params=_cparams(2),
        name="hyena",
    )(*args)


def _mix_kernel(has_pos, *refs):
    refs = list(refs)
    x_ref = refs.pop(0)
    pos_ref = refs.pop(0) if has_pos else None
    (oa_ref, ob_ref, g_ref, mod_ref, wua, wub, wout_half, l1g, l1b, wr_hi, wr_lo, x1_o, u2_o, p_o) = refs
    d = D_MODEL
    tm = x_ref.shape[0]
    sub = min(tm, MIX_SUB_ROWS)
    blocks = [slice(r0, r0 + sub) for r0 in range(0, tm, sub)]
    g1 = mod_ref[0, 2]
    sh2 = mod_ref[0, 3]
    sc2 = mod_ref[0, 4]

    mixed = []
    for rows in blocks:
        th = jnp.tanh(0.5 * g_ref[rows, :].astype(F32))
        a = jnp.dot(oa_ref[rows, :], wua[...], preferred_element_type=F32)
        b = jnp.dot(ob_ref[rows, :], wub[...], preferred_element_type=F32)
        up2 = (th[:, :d] * a + a) + (th[:, d:] * b + b)
        mixed.append(jnp.dot(up2.astype(BF16), wout_half[...], preferred_element_type=F32))

    for rows, mx in zip(blocks, mixed):
        x = x_ref[rows, :]
        if has_pos:
            x = x + pos_ref[rows, :]
        x1 = _ln(ALPHA * x + g1 * mx) * l1g[...] + l1b[...]
        x1_o[rows, :] = x1.astype(x1_o.dtype)
        u2 = _ln(x1) * (1.0 + sc2) + sh2
        u2_hi = u2.astype(BF16)
        u2_o[rows, :] = u2_hi
        u2_lo = (u2 - u2_hi.astype(F32)).astype(BF16)
        logits = (jnp.dot(u2_hi, wr_hi[...], preferred_element_type=F32)
                  + jnp.dot(u2_hi, wr_lo[...], preferred_element_type=F32)
                  + jnp.dot(u2_lo, wr_hi[...], preferred_element_type=F32))
        lane = _iota(logits.shape, 1)
        logits = jnp.where(lane < N_EXPERTS, logits, -jnp.inf)
        m = jnp.max(logits, axis=-1, keepdims=True)
        e = jnp.exp(logits - m)
        p_o[rows, :] = e / jnp.sum(e, axis=-1, keepdims=True)


def _mix_call(x2d, pos, o_a, o_b, gates, mod4, mod_row, wua, wub, wout_half, l1g, l1b, wr_hi, wr_lo):
    n, d = x2d.shape
    tm = min(MIX_TILE, max(n // MIX_MIN_STEPS, MIX_SUB_ROWS))
    assert n % tm == 0
    row = lambda i: (i, 0)
    const = lambda i: (0, 0)
    in_specs = [pl.BlockSpec((tm, d), row)]
    args = [x2d]
    if pos is not None:
        tiles = pos.shape[0] // tm
        in_specs.append(pl.BlockSpec((tm, d), lambda i: (i % tiles, 0)))
        args.append(pos)
    in_specs += [pl.BlockSpec((tm, DN_WIDTH), row), pl.BlockSpec((tm, HY_WIDTH), row),
                 pl.BlockSpec((tm, 2 * d), row),
                 _mod_spec(lambda i: mod_row(i * tm))]
    args += [o_a, o_b, gates, mod4]
    for w in (wua, wub, wout_half, l1g, l1b, wr_hi, wr_lo):
        in_specs.append(pl.BlockSpec(w.shape, const, pipeline_mode=pl.Buffered(1)))
        args.append(w)
    return pl.pallas_call(
        functools.partial(_mix_kernel, pos is not None),
        grid=(n // tm,),
        in_specs=in_specs,
        out_specs=[pl.BlockSpec((tm, d), row), pl.BlockSpec((tm, d), row), pl.BlockSpec((tm, LANES), row)],
        out_shape=[jax.ShapeDtypeStruct((n, d), ACT_DT), jax.ShapeDtypeStruct((n, d), BF16),
                   jax.ShapeDtypeStruct((n, LANES), F32)],
        compiler_params=_cparams(1),
        name="mix",
    )(*args)


def _route_kernel(seq, cap, reqs, p_ref, u_ref, tri_ref, xs_ref, slot_ref, pt_ref, g_s):
    e_n = N_EXPERTS
    tri = tri_ref[...]
    jrow = _iota((cap, seq), 0).astype(F32)
    if seq <= RANK_COUNT_MAX_SEQ:
        earlier = _iota((seq, seq), 0) < _iota((seq, seq), 1)
    for r in range(reqs):
        rows = slice(r * seq, (r + 1) * seq)
        p = p_ref[rows, :]
        pt = p.T[:e_n, :]

        if seq <= RANK_COUNT_MAX_SEQ:
            ranks = []
            for e in range(e_n):
                pc = p[:, e:e + 1]
                pr = pt[e:e + 1, :]
                beats = (pc > pr) | (earlier & (pc == pr))
                ranks.append(jnp.sum(jnp.where(beats, 1.0, 0.0), axis=0, keepdims=True))
            sel = jnp.concatenate(ranks, axis=0) < cap
        else:
            def ok(cand):
                cnt = jnp.sum(jnp.where(pt >= pltpu.bitcast(cand, F32), 1.0, 0.0), axis=1, keepdims=True)
                return (cnt >= cap) & (cand >= MIN_NORMAL_F32_BITS)

            def search(i, cur):
                lo = 28 - 2 * i
                c1, c2, c3 = cur | (1 << lo), cur | (2 << lo), cur | (3 << lo)
                return jnp.where(ok(c3), c3, jnp.where(ok(c2), c2, jnp.where(ok(c1), c1, cur)))

            top = jnp.full((e_n, 1), 1 << 30, jnp.int32)
            cur = jnp.where(ok(top), top, 0)
            thr = pltpu.bitcast(lax.fori_loop(0, 15, search, cur), F32)
            gt = pt > thr
            eq = pt == thr
            n_gt = jnp.sum(jnp.where(gt, 1.0, 0.0), axis=1, keepdims=True)
            eq_rank = jnp.dot(jnp.where(eq, 1.0, 0.0).astype(BF16), tri, preferred_element_type=F32)
            sel = gt | (eq & (eq_rank < cap - n_gt))
        pos = jnp.dot(jnp.where(sel, 1.0, 0.0).astype(BF16), tri, preferred_element_type=F32)
        slot = jnp.where(sel, pos, -1.0)
        slot_ref[r] = slot
        pt_ref[r] = pt
        for e in range(e_n):
            g_s[r, e * cap:(e + 1) * cap, :] = jnp.where(slot[e:e + 1, :] == jrow, 1.0, 0.0).astype(BF16)
        xs = jnp.dot(g_s[r], u_ref[rows, :], preferred_element_type=F32)
        for e in range(e_n):
            xs_ref[e, r * cap:(r + 1) * cap, :] = xs[e * cap:(e + 1) * cap, :].astype(xs_ref.dtype)


def _route_call(probs, u2, bsz, seq):
    cap = EC_FACTOR * seq // N_EXPERTS
    reqs = max(1, min(bsz // 2, ROUTE_ROWS_PER_STEP // seq))
    assert bsz % reqs == 0
    tri = jnp.asarray(np.triu(np.ones((seq, seq), np.float32), 1), dtype=BF16)
    per_req = pl.BlockSpec((reqs, N_EXPERTS, seq), lambda b: (b, 0, 0))
    return pl.pallas_call(
        functools.partial(_route_kernel, seq, cap, reqs),
        grid=(bsz // reqs,),
        in_specs=[pl.BlockSpec((reqs * seq, LANES), lambda b: (b, 0)),
                  pl.BlockSpec((reqs * seq, D_MODEL), lambda b: (b, 0)),
                  pl.BlockSpec((seq, seq), lambda b: (0, 0), pipeline_mode=pl.Buffered(1))],
        out_specs=[pl.BlockSpec((N_EXPERTS, reqs * cap, D_MODEL), lambda b: (0, b, 0)), per_req, per_req],
        out_shape=[jax.ShapeDtypeStruct((N_EXPERTS, bsz * cap, D_MODEL), BF16),
                   jax.ShapeDtypeStruct((bsz, N_EXPERTS, seq), F32),
                   jax.ShapeDtypeStruct((bsz, N_EXPERTS, seq), F32)],
        scratch_shapes=[pltpu.VMEM((reqs, N_EXPERTS * cap, seq), BF16)],
        compiler_params=_cparams(1),
        name="route",
    )(probs, u2, tri)


def _expert_kernel(xc_ref, xd_ref, wg_ref, wu_ref, wd_ref, yc_ref, yd_ref):
    wg = wg_ref[0].astype(BF16)
    wu = wu_ref[0].astype(BF16)
    wd = wd_ref[0].astype(BF16)
    for x_ref, y_ref in ((xc_ref, yc_ref), (xd_ref, yd_ref)):
        rows = x_ref.shape[1]
        rb = min(rows, EXPERT_ROW_BLOCK)
        for r0 in range(0, rows, rb):
            x = x_ref[0, r0:r0 + rb, :]
            hg = jnp.dot(x, wg, preferred_element_type=F32)
            hu = jnp.dot(x, wu, preferred_element_type=F32)
            y = jnp.dot((_silu(hg) * hu).astype(BF16), wd, preferred_element_type=F32)
            y_ref[0, r0:r0 + rb, :] = y.astype(y_ref.dtype)


def _expert_call(xs_c, xs_d, w_gate, w_up, w_down):
    e_n, rc, d = xs_c.shape
    rd = xs_d.shape[1]
    ff = w_gate.shape[2]
    return pl.pallas_call(
        _expert_kernel,
        grid=(e_n,),
        in_specs=[pl.BlockSpec((1, rc, d), lambda e: (e, 0, 0)),
                  pl.BlockSpec((1, rd, d), lambda e: (e, 0, 0)),
                  pl.BlockSpec((1, d, ff), lambda e: (e, 0, 0)),
                  pl.BlockSpec((1, d, ff), lambda e: (e, 0, 0)),
                  pl.BlockSpec((1, ff, d), lambda e: (e, 0, 0))],
        out_specs=[pl.BlockSpec((1, rc, d), lambda e: (e, 0, 0)),
                   pl.BlockSpec((1, rd, d), lambda e: (e, 0, 0))],
        out_shape=[jax.ShapeDtypeStruct((e_n, rc, d), BF16), jax.ShapeDtypeStruct((e_n, rd, d), BF16)],
        compiler_params=_cparams(1),
        name="experts",
    )(xs_c, xs_d, w_gate, w_up, w_down)


def _final_kernel(seq, cap, reqs, x1_ref, y_ref, slot_ref, pt_ref, mod_ref, l2g, l2b, o_ref, w_s):
    jrow = _iota((cap, seq), 0).astype(F32)
    g2 = mod_ref[0, 5]
    for r in range(reqs):
        slot = slot_ref[r]
        pt = pt_ref[r]
        for e in range(N_EXPERTS):
            w_s[r, e * cap:(e + 1) * cap, :] = jnp.where(slot[e:e + 1, :] == jrow, pt[e:e + 1, :],
                                                         0.0).astype(BF16)
        y = y_ref[:, r * cap:(r + 1) * cap, :].reshape(N_EXPERTS * cap, y_ref.shape[-1])
        ffn = lax.dot_general(w_s[r], y, (((0,), (0,)), ((), ())), preferred_element_type=F32)
        rows = slice(r * seq, (r + 1) * seq)
        o_ref[rows, :] = _ln(ALPHA * x1_ref[rows, :].astype(F32) + g2 * ffn) * l2g[...] + l2b[...]


def _final_call(x1, ys, slot, pt, mod4, mod_row, l2g, l2b, bsz, seq, reqs):
    cap = EC_FACTOR * seq // N_EXPERTS
    d = D_MODEL
    per_req = pl.BlockSpec((reqs, N_EXPERTS, seq), lambda b: (b, 0, 0))
    return pl.pallas_call(
        functools.partial(_final_kernel, seq, cap, reqs),
        grid=(bsz // reqs,),
        in_specs=[pl.BlockSpec((reqs * seq, d), lambda b: (b, 0)),
                  pl.BlockSpec((N_EXPERTS, reqs * cap, d), lambda b: (0, b, 0)),
                  per_req, per_req,
                  _mod_spec(lambda b: mod_row(b * reqs * seq)),
                  pl.BlockSpec((1, d), lambda b: (0, 0)),
                  pl.BlockSpec((1, d), lambda b: (0, 0))],
        out_specs=pl.BlockSpec((reqs * seq, d), lambda b: (b, 0)),
        out_shape=jax.ShapeDtypeStruct((bsz * seq, d), F32),
        scratch_shapes=[pltpu.VMEM((reqs, N_EXPERTS * cap, seq), BF16)],
        compiler_params=_cparams(1),
        name="final",
    )(x1, ys, slot, pt, mod4, l2g, l2b)


def _grid_pos_embed(rows, dim):
    r = np.repeat(np.arange(rows), GRID_W)
    col = np.tile(np.arange(GRID_W), rows)
    quarter = dim // 4
    omega = 1.0 / (10000.0 ** (np.arange(quarter) / quarter))

    def enc(p):
        ang = p[:, None] * omega[None, :]
        return np.concatenate([np.sin(ang), np.cos(ang)], -1)

    return jnp.asarray(np.concatenate([enc(r), enc(col)], -1), dtype=F32)


def kernel(x_prompt, x_sample, state_delta_fwd, state_delta_bwd, c, c_ctx, w_mod, b_mod, w_in, conv_a_w, a_log, dt_bias, gnorm_w, conv_b_w, conv_b_b, hy_w1, hy_b1, hy_w2, hy_b2, hy_w3, hy_freq, hy_bias, w_up_a, w_up_b, w_out, ln1_g, ln1_b, w_router, w_e_gate, w_e_up, w_e_down, ln2_g, ln2_b):
    d = D_MODEL
    n_ctx, l_ctx, _ = x_prompt.shape
    n_dec, l_dec, _ = x_sample.shape
    lyr = 0

    cond_t = jnp.pad(jnp.concatenate([c_ctx[None, :], c], 0).T, ((0, 0), (0, 7 - n_dec)))
    mod4 = _mod_call(cond_t, 1 + n_dec, w_mod[lyr], b_mod[lyr]).reshape(8, N_MOD, 1, d)

    in_w = jnp.swapaxes(w_in[lyr], 0, 1).astype(BF16)

    prm = jnp.pad(jnp.stack([a_log[lyr].reshape(-1), dt_bias[lyr].reshape(-1)]),
                  ((0, 6), (2 * DN_HEADS, LANES - 4 * DN_HEADS)))
    gn = gnorm_w[lyr].reshape(1, -1)

    wua = w_up_a[lyr].astype(BF16)
    wub = w_up_b[lyr].astype(BF16)
    wout_half = (0.5 * w_out[lyr]).astype(BF16)
    wr = jnp.pad(w_router[lyr], ((0, 0), (0, LANES - N_EXPERTS)))
    wr_hi = wr.astype(BF16)
    wr_lo = (wr - wr_hi.astype(F32)).astype(BF16)
    l1g, l1b = ln1_g[lyr].reshape(1, -1), ln1_b[lyr].reshape(1, -1)
    l2g, l2b = ln2_g[lyr].reshape(1, -1), ln2_b[lyr].reshape(1, -1)

    def front(x2d, pos, bsz, seq, mod_row, s0f, s0b):
        (qkv, z, ba, hy, gates), pre_conv = _inproj_call(x2d, pos, mod4, mod_row, in_w, conv_a_w[lyr],
                                                          conv_b_w[lyr], conv_b_b[lyr], seq, ACT_DT)
        reqs = DN_REQS_PER_STEP if seq == DN_CHUNK and bsz % DN_REQS_PER_STEP == 0 else 1
        o_a, s_f, s_b = _deltanet_call(qkv, z, ba, conv_a_w[lyr], prm, gn, s0f, s0b, bsz, seq, reqs,
                                       pre_conv, ACT_DT)
        fwd, inv = (jnp.asarray(m) for m in _dft_mats(seq))
        fwd_hi = fwd.astype(BF16)
        fwd_lo = (fwd - fwd_hi.astype(F32)).astype(BF16)
        ha, hb, hd = _hfilter_call(seq, hy_w1[lyr], hy_b1[lyr], hy_w2[lyr], hy_b2[lyr], hy_w3[lyr],
                                   hy_freq[lyr], fwd_hi, fwd_lo)
        o_h = _hyena_call(hy, pre_conv, conv_b_w[lyr], conv_b_b[lyr], ha, hb, hd, hy_bias[lyr],
                          fwd_hi, inv.astype(BF16), bsz, seq, ACT_DT)
        x1, u2, probs = _mix_call(x2d, pos, o_a, o_h, gates, mod4, mod_row, wua, wub, wout_half, l1g, l1b,
                                   wr_hi, wr_lo)
        xs, slot, pt = _route_call(probs, u2, bsz, seq)
        return x1, xs, slot, pt, s_f, s_b

    row_ctx = lambda tok: 0
    row_dec = lambda tok: 1 + tok // l_dec

    xc = x_prompt.reshape(n_ctx * l_ctx, d)
    xd = x_sample.reshape(n_dec * l_dec, d)
    pos = _grid_pos_embed(l_dec // GRID_W, d)

    x1c, xsc, gc, ptc, s_f, s_b = front(xc, None, n_ctx, l_ctx, row_ctx, None, None)
    x1d, xsd, gd, ptd, _, _ = front(xd, pos, n_dec, l_dec, row_dec,
                                    state_delta_fwd[:, lyr], state_delta_bwd[:, lyr])
    yc, yd = _expert_call(xsc, xsd, w_e_gate[lyr], w_e_up[lyr], w_e_down[lyr])
    reqs_c = FINAL_CTX_REQS_PER_STEP if n_ctx % FINAL_CTX_REQS_PER_STEP == 0 else 1
    y_prompt = _final_call(x1c, yc, gc, ptc, mod4, row_ctx, l2g, l2b, n_ctx, l_ctx, reqs_c)
    y_sample = _final_call(x1d, yd, gd, ptd, mod4, row_dec, l2g, l2b, n_dec, l_dec, 1)

    return (y_prompt.reshape(n_ctx, l_ctx, d), y_sample.reshape(n_dec, l_dec, d),
            s_f[:, None], s_b[:, None])
```

```python
import functools
import math

import jax
import jax.numpy as jnp
import numpy as np
from jax import lax
from jax.experimental import pallas as pl
from jax.experimental.pallas import tpu as pltpu

F32 = jnp.float32
BF16 = jnp.bfloat16
HIGHEST = lax.Precision.HIGHEST

D_MODEL = 1024
DEPTH = 1
GRID_W = 64
DN_HEADS = 4
DN_DK = 128
DN_DV = 128
DN_WIDTH = DN_HEADS * DN_DV
HY_WIDTH = D_MODEL // 2
HY_EMB = 33
HY_FFN = 64
HY_TARGET = 1e-2
HY_FAST_DECAY = 0.3
HY_SLOW_DECAY = 1.5
N_EXPERTS = 16
EC_FACTOR = 2
N_MOD = 6
ALPHA = (2 * DEPTH) ** 0.25
LN_EPS = 1e-5
RMS_EPS = 1e-6
QKV_WIDTH = 2 * DN_HEADS * DN_DK + DN_WIDTH

LANES = 128
INPROJ_TILE = 1024
INPROJ_BLOCK_ROWS = 512
MIX_TILE = 1024
MIX_SUB_ROWS = 256
MIX_MIN_STEPS = 4
MOD_COL_BLOCK = 1024
DN_CHUNK = 256
INV_BASE = 8
DN_REQS_PER_STEP = 2
HY_ROWS_PER_STEP = 1024
FINAL_CTX_REQS_PER_STEP = 4
EXPERT_ROW_BLOCK = 512
VMEM_LIMIT = 56 * 1024 * 1024
MIN_NORMAL_F32_BITS = 0x00800000
ACT_DT = BF16
ROUTE_ROWS_PER_STEP = 1024
RANK_COUNT_MAX_SEQ = 256


def _cparams(n_axes):
    return pltpu.CompilerParams(dimension_semantics=("arbitrary",) * n_axes,
                                vmem_limit_bytes=VMEM_LIMIT)


def _bmm(a, b):
    return jnp.einsum("bmk,bkn->bmn", a.astype(BF16), b.astype(BF16), preferred_element_type=F32)


def _bmm_nt(a, b):
    return jnp.einsum("bmk,bnk->bmn", a.astype(BF16), b.astype(BF16), preferred_element_type=F32)


def _bmm_tn(a, b):
    return jnp.einsum("bkm,bkn->bmn", a.astype(BF16), b.astype(BF16), preferred_element_type=F32)


def _mm_f32(a, b):
    return jnp.dot(a, b, precision=HIGHEST, preferred_element_type=F32)


def _sigmoid(x):
    return 0.5 * jnp.tanh(0.5 * x) + 0.5


def _silu(x):
    h = 0.5 * x
    return h * (jnp.tanh(h) + 1.0)


def _softplus(x):
    return jnp.maximum(x, 0.0) + jnp.log1p(jnp.exp(-jnp.abs(x)))


def _ln(x):
    mu = jnp.mean(x, axis=-1, keepdims=True)
    xc = x - mu
    var = jnp.mean(xc * xc, axis=-1, keepdims=True)
    return xc * lax.rsqrt(var + LN_EPS)


def _l2norm(x, scale=1.0):
    return x * (lax.rsqrt(jnp.sum(x * x, axis=-1, keepdims=True) + 1e-6) * scale)


def _iota(shape, dim):
    return lax.broadcasted_iota(jnp.int32, shape, dim)


def _short_conv3(x, w, first, last):
    xm = jnp.where(first, 0.0, pltpu.roll(x, 1, 0))
    xp = jnp.where(last, 0.0, pltpu.roll(x, x.shape[0] - 1, 0))
    return w[0:1, :] * xm + w[1:2, :] * x + w[2:3, :] * xp


def _mod_kernel(n_rows, ct_ref, w_ref, b_ref, o_ref):
    s = _silu(ct_ref[...])
    w = w_ref[...]
    b = b_ref[...]
    o_ref[...] = jnp.zeros(o_ref.shape, F32)
    for r in range(n_rows):
        o_ref[r:r + 1, :] = jnp.sum(w * s[:, r:r + 1], axis=0, keepdims=True) + b


def _mod_call(cond_t, n_rows, w_mod, b_mod):
    d = D_MODEL
    tn = MOD_COL_BLOCK
    return pl.pallas_call(
        functools.partial(_mod_kernel, n_rows),
        grid=(N_MOD * d // tn,),
        in_specs=[pl.BlockSpec((d, 8), lambda j: (0, 0)),
                  pl.BlockSpec((d, tn), lambda j: (0, j)),
                  pl.BlockSpec((1, tn), lambda j: (0, j))],
        out_specs=pl.BlockSpec((8, tn), lambda j: (0, j)),
        out_shape=jax.ShapeDtypeStruct((8, N_MOD * d), F32),
        compiler_params=_cparams(1),
        name="mod",
    )(cond_t, w_mod, b_mod.reshape(1, -1))


def _mod_spec(row_of_step):
    return pl.BlockSpec((1, N_MOD, 1, D_MODEL), lambda *idx: (row_of_step(*idx), 0, 0, 0))


def _inproj_kernel(has_pos, conv_seq, *refs):
    refs = list(refs)
    x_ref = refs.pop(0)
    pos_ref = refs.pop(0) if has_pos else None
    mod_ref, w_ref = refs.pop(0), refs.pop(0)
    if conv_seq:
        cwa_ref, cwb_ref, cbb_ref = refs.pop(0), refs.pop(0), refs.pop(0)
    qkv_o, z_o, ba_o, hy_o, g_o = refs
    sh1 = mod_ref[0, 0]
    sc1 = mod_ref[0, 1]
    o_z = QKV_WIDTH
    o_ba = o_z + DN_WIDTH
    n_ba = 4 * DN_HEADS
    o_g = n_ba + 3 * HY_WIDTH
    lanes = lambda j: slice(j * LANES, (j + 1) * LANES)
    tm = x_ref.shape[0]
    sub = min(tm, INPROJ_BLOCK_ROWS)
    blocks = [slice(r0, r0 + sub) for r0 in range(0, tm, sub)]

    if conv_seq:
        row = _iota((sub, LANES), 0) % conv_seq
        first = row == 0
        last = row == conv_seq - 1

    def project(rows):
        x = x_ref[rows, :]
        if has_pos:
            x = x + pos_ref[rows, :]
        u = (_ln(x) * (1.0 + sc1) + sh1).astype(BF16)
        nt = (((1,), (1,)), ((), ()))
        return (lax.dot_general(u, w_ref[:o_ba, :], nt, preferred_element_type=F32),
                lax.dot_general(u, w_ref[o_ba:, :], nt, preferred_element_type=F32))

    def epilogue(rows, head, tail):
        z_o[rows, :] = head[:, o_z:].astype(z_o.dtype)
        ba_o[rows, :] = tail[:, :LANES]
        g_o[rows, :] = tail[:, o_g:].astype(g_o.dtype)
        hy = tail[:, n_ba:o_g]
        if not conv_seq:
            qkv_o[rows, :] = head[:, :o_z].astype(qkv_o.dtype)
            hy_o[rows, :] = hy.astype(hy_o.dtype)
            return
        for j in range(QKV_WIDTH // LANES):
            y = _silu(_short_conv3(head[:, lanes(j)], cwa_ref[:, lanes(j)], first, last))
            if j < 2 * DN_HEADS:
                y = _l2norm(y, DN_DK ** -0.5 if j < DN_HEADS else 1.0)
            qkv_o[rows, lanes(j)] = y.astype(qkv_o.dtype)
        nj = HY_WIDTH // LANES
        for j in range(nj):
            parts = [_short_conv3(hy[:, lanes(p * nj + j)], cwb_ref[:, lanes(p * nj + j)], first, last)
                     + cbb_ref[:, lanes(p * nj + j)] for p in range(3)]
            hy_o[rows, lanes(j)] = parts[0].astype(hy_o.dtype)
            hy_o[rows, lanes(nj + j)] = (parts[1] * parts[2]).astype(hy_o.dtype)

    pending = None
    for rows in blocks:
        cur = (rows, *project(rows))
        if pending is not None:
            epilogue(*pending)
        pending = cur
    epilogue(*pending)


def _inproj_call(x2d, pos, mod4, mod_row, w, conv_a_w, conv_b_w, conv_b_b, seq, act_dt):
    n, d = x2d.shape
    fuse = seq <= INPROJ_BLOCK_ROWS and INPROJ_TILE % seq == 0
    tm = min(INPROJ_TILE if fuse else INPROJ_BLOCK_ROWS, n)
    assert n % tm == 0 and (not fuse or tm % seq == 0)
    conv_seq = seq if fuse else None
    row = lambda i: (i, 0)
    const = lambda i: (0, 0)
    in_specs = [pl.BlockSpec((tm, d), row)]
    args = [x2d]
    if pos is not None:
        tiles = pos.shape[0] // tm
        in_specs.append(pl.BlockSpec((tm, d), lambda i: (i % tiles, 0)))
        args.append(pos)
    in_specs.append(_mod_spec(lambda i: mod_row(i * tm)))
    args.append(mod4)
    in_specs.append(pl.BlockSpec(w.shape, const, pipeline_mode=pl.Buffered(1)))
    args.append(w)
    if conv_seq:
        for a in (conv_a_w, conv_b_w, conv_b_b.reshape(1, -1)):
            in_specs.append(pl.BlockSpec(a.shape, const))
            args.append(a)
    hy_width = (2 if conv_seq else 3) * HY_WIDTH
    widths = (QKV_WIDTH, DN_WIDTH, LANES, hy_width, 2 * D_MODEL)
    dts = (act_dt, act_dt, F32, act_dt, act_dt)
    outs = pl.pallas_call(
        functools.partial(_inproj_kernel, pos is not None, conv_seq),
        grid=(n // tm,),
        in_specs=in_specs,
        out_specs=[pl.BlockSpec((tm, w), row) for w in widths],
        out_shape=[jax.ShapeDtypeStruct((n, w), dt) for w, dt in zip(widths, dts)],
        compiler_params=_cparams(1),
        name="inproj",
    )(*args)
    return outs, conv_seq is not None


def _inv_unit_tri(lm, ri, ci, block):
    c = lm.shape[-1]

    def same_block(h):
        sh = h.bit_length() - 1
        return (ri >> sh) == (ci >> sh)

    def unfold(f, h):
        return jnp.where(same_block(h), jnp.concatenate([f] * (c // h), axis=1), 0.0)

    base = INV_BASE
    n_nat = jnp.where(same_block(base), -lm, 0.0)
    n_fold = n_nat[:, 0:base, :]
    for i in range(1, c // base):
        n_fold = n_fold + n_nat[:, i * base:(i + 1) * base, :]
    eye_fold = jnp.where((_iota((base, c), 1) & (base - 1)) == _iota((base, c), 0), 1.0, 0.0)
    t = eye_fold + n_fold
    p_fold, p_nat = n_fold, n_nat
    span = 2
    while span < base:
        p_fold = _bmm(p_fold, p_nat)
        p_nat = unfold(p_fold, base)
        t = t + _bmm(t, p_nat)
        span *= 2
    s = base
    while s < block:
        sh = s.bit_length() - 1
        t_nat = unfold(t, s)
        even = ((_iota((s, c), 1) >> sh) & 1) == 0
        t2 = jnp.concatenate([jnp.where(even, t, 0.0), jnp.where(even, 0.0, t)], axis=1)
        lo = jnp.where(same_block(2 * s) & ~same_block(s), lm, 0.0)
        t = t2 - _bmm(_bmm(t2, lo), t_nat)
        s *= 2
    return t


def _dn_chunk(q, k, v, gc_col, gc_row, beta, e_col, e_rest, e_all, s_prev, same_chunk, ri, ci):
    c = DN_CHUNK
    nh = q.shape[0] // 2
    half = c // 2
    if same_chunk:
        gram = _bmm_nt(k[:nh], k[:nh])
        qk = _bmm_nt(q[:nh], k[:nh])
        gram = jnp.concatenate([gram, gram], 0)
        qk = jnp.concatenate([qk, qk], 0)
    else:
        gram = _bmm_nt(k, k)
        qk = _bmm_nt(q, k)
    diff = gc_col - gc_row
    incl = jnp.concatenate([jnp.broadcast_to(ri >= ci, (nh, c, c)), jnp.broadcast_to(ri <= ci, (nh, c, c))], 0)
    strict = incl & (ri != ci)
    decay = jnp.exp(jnp.where(incl, diff, -jnp.inf))
    lm = jnp.where(strict, beta * gram * decay, 0.0)
    a_intra = qk * decay
    t = _inv_unit_tri(lm, ri, ci, half)
    t1 = t[:, :, :half]
    t2 = t[:, :, half:]
    rhs = jnp.concatenate([v * beta, k * (beta * e_col)], axis=2)
    y1 = _bmm(t1, rhs[:, :half, :])
    y2 = _bmm(t2, rhs[:, half:, :])
    c_f = _bmm(lm[:nh, half:, :half], y1[:nh])
    c_b = _bmm(lm[nh:, :half, half:], y2[nh:])
    c_f = _bmm(t2[:nh], c_f)
    c_b = _bmm(t1[nh:], c_b)
    sol = jnp.concatenate([jnp.concatenate([y1[:nh], y2[:nh] - c_f], 1),
                           jnp.concatenate([y1[nh:] - c_b, y2[nh:]], 1)], 0)
    u = sol[:, :, :DN_DV]
    w = sol[:, :, DN_DV:]
    ks = k * e_rest
    if s_prev is None:
        v_new = u
        o = _bmm(a_intra, v_new)
        s_new = _bmm_tn(ks, v_new)
    else:
        v_new = u - _bmm(w, s_prev)
        o = _bmm(q * e_col, s_prev) + _bmm(a_intra, v_new)
        s_new = s_prev * e_all + _bmm_tn(ks, v_new)
    return o, s_new


def _deltanet_kernel(seq, reqs, zero_init, pre_conv, *refs):
    qkv_ref, cw_ref, z_ref, ba_ref, prm_ref, gn_ref = refs[:6]
    refs = refs[6:]
    if not zero_init:
        s0f_ref, s0b_ref = refs[:2]
        refs = refs[2:]
    o_ref, sf_ref, sb_ref, q_s, k_s, v_s, g_s, b_s, o_s, st_s = refs
    c = DN_CHUNK
    n = seq // c
    nh = DN_HEADS
    nf = reqs * nh
    lanes = lambda j: slice(j * LANES, (j + 1) * LANES)

    if not pre_conv:
        row = _iota((seq, LANES), 0)
        first_row = row == 0
        last_row = row == seq - 1
    for r in range(reqs):
        rows = slice(r * seq, (r + 1) * seq)

        def qkv_part(j):
            x = qkv_ref[rows, lanes(j)].astype(F32)
            return x if pre_conv else _silu(_short_conv3(x, cw_ref[:, lanes(j)], first_row, last_row))

        for h in range(nh):
            q = qkv_part(h)
            k = qkv_part(nh + h)
            q_s[r * nh + h] = q if pre_conv else _l2norm(q, DN_DK ** -0.5)
            k_s[r * nh + h] = k if pre_conv else _l2norm(k)
            v_s[r * nh + h] = qkv_part(2 * nh + h)

    ba = ba_ref[...]
    b_s[...] = _sigmoid(ba)
    g_s[...] = -jnp.exp(prm_ref[0:1, :]) * _softplus(ba + prm_ref[1:2, :])

    ri = _iota((c, c), 0)
    ci = _iota((c, c), 1)
    tri_l = jnp.where(ri >= ci, 1.0, 0.0).astype(F32)
    tri_u = jnp.where(ri <= ci, 1.0, 0.0).astype(F32)

    def chunk_pair(idx_f, idx_b, s_prev):
        def ds(idx, base=0):
            start = idx * c
            start = start if isinstance(start, int) else pl.multiple_of(start, c)
            return pl.ds(base + start, c)

        col_f, col_b, row_f, row_b, beta_f, beta_b = [], [], [], [], [], []
        for r in range(reqs):
            cs_f = _mm_f32(tri_l, g_s[ds(idx_f, r * seq), :])
            cs_b = _mm_f32(tri_u, g_s[ds(idx_b, r * seq), :])
            cst_f = cs_f.T
            cst_b = cs_b.T
            b_f = b_s[ds(idx_f, r * seq), :]
            b_b = b_s[ds(idx_b, r * seq), :]
            for h in range(nh):
                jf, jb = 2 * nh + h, 3 * nh + h
                col_f.append(cs_f[:, jf:jf + 1])
                col_b.append(cs_b[:, jb:jb + 1])
                row_f.append(cst_f[jf:jf + 1, :])
                row_b.append(cst_b[jb:jb + 1, :])
                beta_f.append(b_f[:, h:h + 1])
                beta_b.append(b_b[:, nh + h:nh + h + 1])
        gc_col = jnp.stack(col_f + col_b, 0)
        gc_row = jnp.stack(row_f + row_b, 0)
        beta = jnp.stack(beta_f + beta_b, 0)
        e_col = jnp.exp(gc_col)
        gl = jnp.concatenate([gc_col[:nf, c - 1:c, :], gc_col[nf:, 0:1, :]], 0)
        e_rest = jnp.exp(gl - gc_col)
        e_all = jnp.exp(gl)
        sl_f, sl_b = ds(idx_f), ds(idx_b)
        both = lambda ref: jnp.concatenate([ref[:, sl_f, :], ref[:, sl_b, :]], 0)
        o, s_new = _dn_chunk(both(q_s), both(k_s), both(v_s), gc_col, gc_row, beta, e_col, e_rest, e_all,
                             s_prev, n == 1, ri, ci)
        o_s[:nf, sl_f, :] = o[:nf]
        o_s[nf:, sl_b, :] = o[nf:]
        return s_new

    if not zero_init:
        s0 = jnp.concatenate([s0f_ref[r] for r in range(reqs)] + [s0b_ref[r] for r in range(reqs)], 0)
    if n == 1:
        s_fin = chunk_pair(0, 0, None if zero_init else s0)
    else:
        st_s[...] = jnp.zeros(st_s.shape, F32) if zero_init else s0

        def body(i, carry):
            st_s[...] = chunk_pair(i, n - 1 - i, st_s[...])
            return carry

        lax.fori_loop(0, n, body, 0)
        s_fin = st_s[...]
    for r in range(reqs):
        sf_ref[r] = s_fin[r * nh:(r + 1) * nh]
        sb_ref[r] = s_fin[nf + r * nh:nf + (r + 1) * nh]

    gn = gn_ref[...]
    for r in range(reqs):
        rows = slice(r * seq, (r + 1) * seq)
        for h in range(nh):
            o = o_s[r * nh + h] + o_s[nf + r * nh + h]
            o = o * lax.rsqrt(jnp.mean(o * o, axis=-1, keepdims=True) + RMS_EPS) * gn
            o_ref[rows, lanes(h)] = (o * _silu(z_ref[rows, lanes(h)].astype(F32))).astype(o_ref.dtype)


def _deltanet_call(qkv, z, ba, conv_w, prm, gn, s0f, s0b, bsz, seq, reqs, pre_conv, act_dt):
    nh = DN_HEADS
    zero_init = s0f is None
    rows = reqs * seq
    st = pl.BlockSpec((reqs, nh, DN_DK, DN_DV), lambda b: (b, 0, 0, 0))
    in_specs = [pl.BlockSpec((rows, QKV_WIDTH), lambda b: (b, 0)),
                pl.BlockSpec((3, QKV_WIDTH), lambda b: (0, 0)),
                pl.BlockSpec((rows, DN_WIDTH), lambda b: (b, 0)),
                pl.BlockSpec((rows, LANES), lambda b: (b, 0)),
                pl.BlockSpec((8, LANES), lambda b: (0, 0)),
                pl.BlockSpec((1, LANES), lambda b: (0, 0))]
    args = [qkv, conv_w, z, ba, prm, gn]
    if not zero_init:
        in_specs += [st, st]
        args += [s0f, s0b]
    vm = lambda shape: pltpu.VMEM(shape, F32)
    per_head = vm((reqs * nh, seq, LANES))
    return pl.pallas_call(
        functools.partial(_deltanet_kernel, seq, reqs, zero_init, pre_conv),
        grid=(bsz // reqs,),
        in_specs=in_specs,
        out_specs=[pl.BlockSpec((rows, DN_WIDTH), lambda b: (b, 0)), st, st],
        out_shape=[jax.ShapeDtypeStruct((bsz * seq, DN_WIDTH), act_dt),
                   jax.ShapeDtypeStruct((bsz, nh, DN_DK, DN_DV), F32),
                   jax.ShapeDtypeStruct((bsz, nh, DN_DK, DN_DV), F32)],
        scratch_shapes=[per_head, per_head, per_head, vm((rows, LANES)), vm((rows, LANES)),
                        vm((2 * reqs * nh, seq, LANES)), vm((2 * reqs * nh, DN_DK, DN_DV))],
        compiler_params=_cparams(1),
        name="deltanet",
    )(*args)


def _dft_mats(seq):
    n = 2 * seq
    f = np.arange(seq)[:, None]
    s = np.arange(seq)[None, :]
    ang = 2.0 * np.pi * ((f * s) % n) / n
    fwd = np.concatenate([np.cos(ang), -np.sin(ang)], axis=0)
    fwd[seq, :] = np.cos(np.pi * np.arange(seq))
    t = (np.arange(seq) + seq // 2)[:, None]
    ff = np.arange(seq)[None, :]
    ang2 = 2.0 * np.pi * ((t * ff) % n) / n
    inv_r = 2.0 * np.cos(ang2) / n
    inv_i = -2.0 * np.sin(ang2) / n
    inv_r[:, 0] = 1.0 / n
    inv_i[:, 0] = np.cos(np.pi * t[:, 0]) / n
    inv = np.concatenate([inv_r, inv_i], axis=1)
    return fwd.astype(np.float32), inv.astype(np.float32)


def _filter_feats(seq):
    t = np.linspace(0.0, 1.0, seq)[:, None]
    bands = (HY_EMB - 1) // 2
    ang = (2.0 * math.pi * np.arange(seq) / seq)[:, None] * np.linspace(1e-4, bands - 1, bands)[None, :]
    feats = np.concatenate([t, np.cos(ang), -np.sin(ang)], -1)
    deltas = np.abs(np.linspace(math.log(HY_TARGET) / HY_SLOW_DECAY, math.log(HY_TARGET) / HY_FAST_DECAY,
                                HY_WIDTH))
    offset = np.abs(np.arange(seq) - seq // 2) / (seq // 2)
    window = np.exp(-offset[:, None] * deltas[None, :])
    feats = np.pad(feats, ((0, 0), (0, LANES - HY_EMB)))
    return jnp.asarray(feats, dtype=F32), jnp.asarray(window, dtype=F32)


def _filter_spectrum(seq, feats_ref, win_ref, w1_ref, b1_ref, w2_ref, b2_ref, w3_ref, fr_ref, fwd_hi_ref,
                     fwd_lo_ref, ha_s, hb_s, hd_s):
    fr = fr_ref[...]
    hid = jnp.sin(fr * (_mm_f32(feats_ref[...], w1_ref[...]) + b1_ref[...]))
    hid = jnp.sin(fr * (_mm_f32(hid, w2_ref[...]) + b2_ref[...]))
    filt = _mm_f32(hid, w3_ref[...]) * win_ref[...]
    filt = filt / (jnp.sum(jnp.abs(filt), axis=0, keepdims=True) + 1e-6)
    filt_hi = filt.astype(BF16)
    filt_lo = (filt - filt_hi.astype(F32)).astype(BF16)
    fwd_hi = fwd_hi_ref[...]
    spec = (jnp.dot(fwd_hi, filt_hi, preferred_element_type=F32)
            + jnp.dot(fwd_hi, filt_lo, preferred_element_type=F32)
            + jnp.dot(fwd_lo_ref[...], filt_hi, preferred_element_type=F32))
    h_re = spec[:seq, :]
    h_im = spec[seq:, :]
    first = _iota(h_re.shape, 0) == 0
    ha_s[...] = h_re
    hb_s[...] = jnp.where(first, 0.0, h_im)
    hd_s[...] = jnp.where(first, h_im, h_re)


def _hyena_kernel(seq, reqs, pre_conv, *refs):
    refs = list(refs)
    n_tok = 2 if pre_conv else 9
    tok_refs, refs = refs[:n_tok], refs[n_tok:]
    filt_refs, refs = refs[:8], refs[8:]
    skip_ref, fwd_ref, fwd_lo_ref, inv_ref, o_ref, ha_s, hb_s, hd_s = refs

    @pl.when(pl.program_id(0) == 0)
    def _():
        _filter_spectrum(seq, *filt_refs, fwd_ref, fwd_lo_ref, ha_s, hb_s, hd_s)

    if pre_conv:
        x0_ref, uu_ref = tok_refs
    else:
        x0_ref, x1_ref, v_ref, c0_ref, c1_ref, c2_ref, b0_ref, b1_ref, b2_ref = tok_refs
        row = _iota((seq, HY_WIDTH), 0)
        first = row == 0
        last = row == seq - 1
    hb = hb_s[...]
    for r in range(reqs):
        rows = slice(r * seq, (r + 1) * seq)
        if pre_conv:
            x0 = x0_ref[rows, :].astype(F32)
            uu = uu_ref[rows, :].astype(F32)
        else:
            conv = lambda x_ref, w_ref, b_ref: (_short_conv3(x_ref[rows, :].astype(F32), w_ref[...], first, last)
                                                + b_ref[...])
            x0 = conv(x0_ref, c0_ref, b0_ref)
            uu = conv(x1_ref, c1_ref, b1_ref) * conv(v_ref, c2_ref, b2_ref)
        spec = jnp.dot(fwd_ref[...], uu.astype(BF16), preferred_element_type=F32)
        u_re = spec[:seq, :]
        u_im = spec[seq:, :]
        y_re = u_re * ha_s[...] - u_im * hb
        y_im = u_re * hb + u_im * hd_s[...]
        y = jnp.concatenate([y_re, y_im], axis=0).astype(BF16)
        cv = jnp.dot(inv_ref[...], y, preferred_element_type=F32)
        o_ref[rows, :] = (x0 * (cv + uu * skip_ref[...])).astype(o_ref.dtype)


def _hyena_call(hy, pre_conv, conv_w, conv_b, filt_params, skip, fwd_hi, fwd_lo, inv, bsz, seq, act_dt):
    w1, b1, w2, b2, w3, freq = filt_params
    cb = HY_WIDTH
    reqs = max(1, min(bsz, HY_ROWS_PER_STEP // seq))
    assert bsz % reqs == 0
    tok = lambda part: pl.BlockSpec((reqs * seq, cb), lambda b: (b, part))
    const = lambda a: pl.BlockSpec(a.shape, lambda b: (0,) * a.ndim, pipeline_mode=pl.Buffered(1))
    feats, window = _filter_feats(seq)
    w1p = jnp.pad(w1, ((0, LANES - HY_EMB), (0, 0)))
    filt_args = [feats, window, w1p, b1.reshape(1, -1), w2, b2.reshape(1, -1), w3, freq.reshape(1, -1)]
    tail_args = filt_args + [skip.reshape(1, -1), fwd_hi, fwd_lo, inv]
    tail_specs = [const(a) for a in tail_args]
    if pre_conv:
        in_specs = [tok(0), tok(1)] + tail_specs
        args = [hy, hy] + tail_args
    else:
        conv_b2 = conv_b.reshape(1, -1)
        part3 = lambda a, part: pl.BlockSpec((a.shape[0], cb), lambda b: (0, part))
        in_specs = ([tok(0), tok(1), tok(2)] + [part3(conv_w, p) for p in range(3)]
                    + [part3(conv_b2, p) for p in range(3)] + tail_specs)
        args = [hy, hy, hy, conv_w, conv_w, conv_w, conv_b2, conv_b2, conv_b2] + tail_args
    spec_buf = pltpu.VMEM((seq, HY_WIDTH), F32)
    return pl.pallas_call(
        functools.partial(_hyena_kernel, seq, reqs, pre_conv),
        grid=(bsz // reqs,),
        in_specs=in_specs,
        out_specs=tok(0),
        out_shape=jax.ShapeDtypeStruct((bsz * seq, HY_WIDTH), act_dt),
        scratch_shapes=[spec_buf, spec_buf, spec_buf],
        compiler_params=_cparams(1),
        name="hyena",
    )(*args)


def _mix_kernel(has_pos, *refs):
    refs = list(refs)
    x_ref = refs.pop(0)
    pos_ref = refs.pop(0) if has_pos else None
    (oa_ref, ob_ref, g_ref, mod_ref, wua, wub, wout_half, l1g, l1b, wr_hi, wr_lo, x1_o, u2_o, p_o) = refs
    d = D_MODEL
    tm = x_ref.shape[0]
    sub = min(tm, MIX_SUB_ROWS)
    blocks = [slice(r0, r0 + sub) for r0 in range(0, tm, sub)]
    g1 = mod_ref[0, 2]
    sh2 = mod_ref[0, 3]
    sc2 = mod_ref[0, 4]

    mixed = []
    for rows in blocks:
        th = jnp.tanh(0.5 * g_ref[rows, :].astype(F32))
        a = jnp.dot(oa_ref[rows, :], wua[...], preferred_element_type=F32)
        b = jnp.dot(ob_ref[rows, :], wub[...], preferred_element_type=F32)
        up2 = (th[:, :d] * a + a) + (th[:, d:] * b + b)
        mixed.append(jnp.dot(up2.astype(BF16), wout_half[...], preferred_element_type=F32))

    for rows, mx in zip(blocks, mixed):
        x = x_ref[rows, :]
        if has_pos:
            x = x + pos_ref[rows, :]
        x1 = _ln(ALPHA * x + g1 * mx) * l1g[...] + l1b[...]
        x1_o[rows, :] = x1.astype(x1_o.dtype)
        u2 = _ln(x1) * (1.0 + sc2) + sh2
        u2_hi = u2.astype(BF16)
        u2_o[rows, :] = u2_hi
        u2_lo = (u2 - u2_hi.astype(F32)).astype(BF16)
        logits = (jnp.dot(u2_hi, wr_hi[...], preferred_element_type=F32)
                  + jnp.dot(u2_hi, wr_lo[...], preferred_element_type=F32)
                  + jnp.dot(u2_lo, wr_hi[...], preferred_element_type=F32))
        lane = _iota(logits.shape, 1)
        logits = jnp.where(lane < N_EXPERTS, logits, -jnp.inf)
        m = jnp.max(logits, axis=-1, keepdims=True)
        e = jnp.exp(logits - m)
        p_o[rows, :] = e / jnp.sum(e, axis=-1, keepdims=True)


def _mix_call(x2d, pos, o_a, o_b, gates, mod4, mod_row, wua, wub, wout_half, l1g, l1b, wr_hi, wr_lo):
    n, d = x2d.shape
    tm = min(MIX_TILE, max(n // MIX_MIN_STEPS, MIX_SUB_ROWS))
    assert n % tm == 0
    row = lambda i: (i, 0)
    const = lambda i: (0, 0)
    in_specs = [pl.BlockSpec((tm, d), row)]
    args = [x2d]
    if pos is not None:
        tiles = pos.shape[0] // tm
        in_specs.append(pl.BlockSpec((tm, d), lambda i: (i % tiles, 0)))
        args.append(pos)
    in_specs += [pl.BlockSpec((tm, DN_WIDTH), row), pl.BlockSpec((tm, HY_WIDTH), row),
                 pl.BlockSpec((tm, 2 * d), row),
                 _mod_spec(lambda i: mod_row(i * tm))]
    args += [o_a, o_b, gates, mod4]
    for w in (wua, wub, wout_half, l1g, l1b, wr_hi, wr_lo):
        in_specs.append(pl.BlockSpec(w.shape, const, pipeline_mode=pl.Buffered(1)))
        args.append(w)
    return pl.pallas_call(
        functools.partial(_mix_kernel, pos is not None),
        grid=(n // tm,),
        in_specs=in_specs,
        out_specs=[pl.BlockSpec((tm, d), row), pl.BlockSpec((tm, d), row), pl.BlockSpec((tm, LANES), row)],
        out_shape=[jax.ShapeDtypeStruct((n, d), ACT_DT), jax.ShapeDtypeStruct((n, d), BF16),
                   jax.ShapeDtypeStruct((n, LANES), F32)],
        compiler_params=_cparams(1),
        name="mix",
    )(*args)


def _route_kernel(seq, cap, reqs, p_ref, u_ref, tri_ref, xs_ref, slot_ref, pt_ref, g_s):
    e_n = N_EXPERTS
    tri = tri_ref[...]
    jrow = _iota((cap, seq), 0).astype(F32)
    if seq <= RANK_COUNT_MAX_SEQ:
        earlier = _iota((seq, seq), 0) < _iota((seq, seq), 1)
    for r in range(reqs):
        rows = slice(r * seq, (r + 1) * seq)
        p = p_ref[rows, :]
        pt = p.T[:e_n, :]

        if seq <= RANK_COUNT_MAX_SEQ:
            ranks = []
            for e in range(e_n):
                pc = p[:, e:e + 1]
                pr = pt[e:e + 1, :]
                beats = (pc > pr) | (earlier & (pc == pr))
                ranks.append(jnp.sum(jnp.where(beats, 1.0, 0.0), axis=0, keepdims=True))
            sel = jnp.concatenate(ranks, axis=0) < cap
        else:
            def ok(cand):
                cnt = jnp.sum(jnp.where(pt >= pltpu.bitcast(cand, F32), 1.0, 0.0), axis=1, keepdims=True)
                return (cnt >= cap) & (cand >= MIN_NORMAL_F32_BITS)

            def search(i, cur):
                lo = 28 - 2 * i
                c1, c2, c3 = cur | (1 << lo), cur | (2 << lo), cur | (3 << lo)
                return jnp.where(ok(c3), c3, jnp.where(ok(c2), c2, jnp.where(ok(c1), c1, cur)))

            top = jnp.full((e_n, 1), 1 << 30, jnp.int32)
            cur = jnp.where(ok(top), top, 0)
            thr = pltpu.bitcast(lax.fori_loop(0, 15, search, cur), F32)
            gt = pt > thr
            eq = pt == thr
            n_gt = jnp.sum(jnp.where(gt, 1.0, 0.0), axis=1, keepdims=True)
            eq_rank = jnp.dot(jnp.where(eq, 1.0, 0.0).astype(BF16), tri, preferred_element_type=F32)
            sel = gt | (eq & (eq_rank < cap - n_gt))
        pos = jnp.dot(jnp.where(sel, 1.0, 0.0).astype(BF16), tri, preferred_element_type=F32)
        slot = jnp.where(sel, pos, -1.0)
        slot_ref[r] = slot
        pt_ref[r] = pt
        for e in range(e_n):
            g_s[r, e * cap:(e + 1) * cap, :] = jnp.where(slot[e:e + 1, :] == jrow, 1.0, 0.0).astype(BF16)
        xs = jnp.dot(g_s[r], u_ref[rows, :], preferred_element_type=F32)
        for e in range(e_n):
            xs_ref[e, r * cap:(r + 1) * cap, :] = xs[e * cap:(e + 1) * cap, :].astype(xs_ref.dtype)


def _route_call(probs, u2, bsz, seq):
    cap = EC_FACTOR * seq // N_EXPERTS
    reqs = max(1, min(bsz, ROUTE_ROWS_PER_STEP // seq))
    assert bsz % reqs == 0
    tri = jnp.asarray(np.triu(np.ones((seq, seq), np.float32), 1), dtype=BF16)
    per_req = pl.BlockSpec((reqs, N_EXPERTS, seq), lambda b: (b, 0, 0))
    return pl.pallas_call(
        functools.partial(_route_kernel, seq, cap, reqs),
        grid=(bsz // reqs,),
        in_specs=[pl.BlockSpec((reqs * seq, LANES), lambda b: (b, 0)),
                  pl.BlockSpec((reqs * seq, D_MODEL), lambda b: (b, 0)),
                  pl.BlockSpec((seq, seq), lambda b: (0, 0), pipeline_mode=pl.Buffered(1))],
        out_specs=[pl.BlockSpec((N_EXPERTS, reqs * cap, D_MODEL), lambda b: (0, b, 0)), per_req, per_req],
        out_shape=[jax.ShapeDtypeStruct((N_EXPERTS, bsz * cap, D_MODEL), BF16),
                   jax.ShapeDtypeStruct((bsz, N_EXPERTS, seq), F32),
                   jax.ShapeDtypeStruct((bsz, N_EXPERTS, seq), F32)],
        scratch_shapes=[pltpu.VMEM((reqs, N_EXPERTS * cap, seq), BF16)],
        compiler_params=_cparams(1),
        name="route",
    )(probs, u2, tri)


def _expert_kernel(xc_ref, xd_ref, wg_ref, wu_ref, wd_ref, yc_ref, yd_ref):
    wg = wg_ref[0].astype(BF16)
    wu = wu_ref[0].astype(BF16)
    wd = wd_ref[0].astype(BF16)
    for x_ref, y_ref in ((xc_ref, yc_ref), (xd_ref, yd_ref)):
        rows = x_ref.shape[1]
        rb = min(rows, EXPERT_ROW_BLOCK)
        for r0 in range(0, rows, rb):
            x = x_ref[0, r0:r0 + rb, :]
            hg = jnp.dot(x, wg, preferred_element_type=F32)
            hu = jnp.dot(x, wu, preferred_element_type=F32)
            y = jnp.dot((_silu(hg) * hu).astype(BF16), wd, preferred_element_type=F32)
            y_ref[0, r0:r0 + rb, :] = y.astype(y_ref.dtype)


def _expert_call(xs_c, xs_d, w_gate, w_up, w_down):
    e_n, rc, d = xs_c.shape
    rd = xs_d.shape[1]
    ff = w_gate.shape[2]
    return pl.pallas_call(
        _expert_kernel,
        grid=(e_n,),
        in_specs=[pl.BlockSpec((1, rc, d), lambda e: (e, 0, 0)),
                  pl.BlockSpec((1, rd, d), lambda e: (e, 0, 0)),
                  pl.BlockSpec((1, d, ff), lambda e: (e, 0, 0)),
                  pl.BlockSpec((1, d, ff), lambda e: (e, 0, 0)),
                  pl.BlockSpec((1, ff, d), lambda e: (e, 0, 0))],
        out_specs=[pl.BlockSpec((1, rc, d), lambda e: (e, 0, 0)),
                   pl.BlockSpec((1, rd, d), lambda e: (e, 0, 0))],
        out_shape=[jax.ShapeDtypeStruct((e_n, rc, d), BF16), jax.ShapeDtypeStruct((e_n, rd, d), BF16)],
        compiler_params=_cparams(1),
        name="experts",
    )(xs_c, xs_d, w_gate, w_up, w_down)


def _final_kernel(seq, cap, reqs, x1_ref, y_ref, slot_ref, pt_ref, mod_ref, l2g, l2b, o_ref, w_s):
    jrow = _iota((cap, seq), 0).astype(F32)
    g2 = mod_ref[0, 5]
    for r in range(reqs):
        slot = slot_ref[r]
        pt = pt_ref[r]
        for e in range(N_EXPERTS):
            w_s[r, e * cap:(e + 1) * cap, :] = jnp.where(slot[e:e + 1, :] == jrow, pt[e:e + 1, :],
                                                         0.0).astype(BF16)
        y = y_ref[:, r * cap:(r + 1) * cap, :].reshape(N_EXPERTS * cap, y_ref.shape[-1])
        ffn = lax.dot_general(w_s[r], y, (((0,), (0,)), ((), ())), preferred_element_type=F32)
        rows = slice(r * seq, (r + 1) * seq)
        o_ref[rows, :] = _ln(ALPHA * x1_ref[rows, :].astype(F32) + g2 * ffn) * l2g[...] + l2b[...]


def _final_call(x1, ys, slot, pt, mod4, mod_row, l2g, l2b, bsz, seq, reqs):
    cap = EC_FACTOR * seq // N_EXPERTS
    d = D_MODEL
    per_req = pl.BlockSpec((reqs, N_EXPERTS, seq), lambda b: (b, 0, 0))
    return pl.pallas_call(
        functools.partial(_final_kernel, seq, cap, reqs),
        grid=(bsz // reqs,),
        in_specs=[pl.BlockSpec((reqs * seq, d), lambda b: (b, 0)),
                  pl.BlockSpec((N_EXPERTS, reqs * cap, d), lambda b: (0, b, 0)),
                  per_req, per_req,
                  _mod_spec(lambda b: mod_row(b * reqs * seq)),
                  pl.BlockSpec((1, d), lambda b: (0, 0)),
                  pl.BlockSpec((1, d), lambda b: (0, 0))],
        out_specs=pl.BlockSpec((reqs * seq, d), lambda b: (b, 0)),
        out_shape=jax.ShapeDtypeStruct((bsz * seq, d), F32),
        scratch_shapes=[pltpu.VMEM((reqs, N_EXPERTS * cap, seq), BF16)],
        compiler_params=_cparams(1),
        name="final",
    )(x1, ys, slot, pt, mod4, l2g, l2b)


def _grid_pos_embed(rows, dim):
    r = np.repeat(np.arange(rows), GRID_W)
    col = np.tile(np.arange(GRID_W), rows)
    quarter = dim // 4
    omega = 1.0 / (10000.0 ** (np.arange(quarter) / quarter))

    def enc(p):
        ang = p[:, None] * omega[None, :]
        return np.concatenate([np.sin(ang), np.cos(ang)], -1)

    return jnp.asarray(np.concatenate([enc(r), enc(col)], -1), dtype=F32)


def kernel(x_prompt, x_sample, state_delta_fwd, state_delta_bwd, c, c_ctx, w_mod, b_mod, w_in, conv_a_w, a_log, dt_bias, gnorm_w, conv_b_w, conv_b_b, hy_w1, hy_b1, hy_w2, hy_b2, hy_w3, hy_freq, hy_bias, w_up_a, w_up_b, w_out, ln1_g, ln1_b, w_router, w_e_gate, w_e_up, w_e_down, ln2_g, ln2_b):
    d = D_MODEL
    n_ctx, l_ctx, _ = x_prompt.shape
    n_dec, l_dec, _ = x_sample.shape
    lyr = 0

    cond_t = jnp.pad(jnp.concatenate([c_ctx[None, :], c], 0).T, ((0, 0), (0, 7 - n_dec)))
    mod4 = _mod_call(cond_t, 1 + n_dec, w_mod[lyr], b_mod[lyr]).reshape(8, N_MOD, 1, d)

    in_w = jnp.swapaxes(w_in[lyr], 0, 1).astype(BF16)

    prm = jnp.pad(jnp.stack([a_log[lyr].reshape(-1), dt_bias[lyr].reshape(-1)]),
                  ((0, 6), (2 * DN_HEADS, LANES - 4 * DN_HEADS)))
    gn = gnorm_w[lyr].reshape(1, -1)

    wua = w_up_a[lyr].astype(BF16)
    wub = w_up_b[lyr].astype(BF16)
    wout_half = (0.5 * w_out[lyr]).astype(BF16)
    wr = jnp.pad(w_router[lyr], ((0, 0), (0, LANES - N_EXPERTS)))
    wr_hi = wr.astype(BF16)
    wr_lo = (wr - wr_hi.astype(F32)).astype(BF16)
    l1g, l1b = ln1_g[lyr].reshape(1, -1), ln1_b[lyr].reshape(1, -1)
    l2g, l2b = ln2_g[lyr].reshape(1, -1), ln2_b[lyr].reshape(1, -1)

    def front(x2d, pos, bsz, seq, mod_row, s0f, s0b):
        (qkv, z, ba, hy, gates), pre_conv = _inproj_call(x2d, pos, mod4, mod_row, in_w, conv_a_w[lyr],
                                                          conv_b_w[lyr], conv_b_b[lyr], seq, ACT_DT)
        reqs = DN_REQS_PER_STEP if seq == DN_CHUNK and bsz % DN_REQS_PER_STEP == 0 else 1
        o_a, s_f, s_b = _deltanet_call(qkv, z, ba, conv_a_w[lyr], prm, gn, s0f, s0b, bsz, seq, reqs,
                                       pre_conv, ACT_DT)
        fwd, inv = (jnp.asarray(m) for m in _dft_mats(seq))
        fwd_hi = fwd.astype(BF16)
        fwd_lo = (fwd - fwd_hi.astype(F32)).astype(BF16)
        filt_params = (hy_w1[lyr], hy_b1[lyr], hy_w2[lyr], hy_b2[lyr], hy_w3[lyr], hy_freq[lyr])
        o_h = _hyena_call(hy, pre_conv, conv_b_w[lyr], conv_b_b[lyr], filt_params, hy_bias[lyr],
                          fwd_hi, fwd_lo, inv.astype(BF16), bsz, seq, ACT_DT)
        x1, u2, probs = _mix_call(x2d, pos, o_a, o_h, gates, mod4, mod_row, wua, wub, wout_half, l1g, l1b,
                                   wr_hi, wr_lo)
        xs, slot, pt = _route_call(probs, u2, bsz, seq)
        return x1, xs, slot, pt, s_f, s_b

    row_ctx = lambda tok: 0
    row_dec = lambda tok: 1 + tok // l_dec

    xc = x_prompt.reshape(n_ctx * l_ctx, d)
    xd = x_sample.reshape(n_dec * l_dec, d)
    pos = _grid_pos_embed(l_dec // GRID_W, d)

    x1c, xsc, gc, ptc, s_f, s_b = front(xc, None, n_ctx, l_ctx, row_ctx, None, None)
    x1d, xsd, gd, ptd, _, _ = front(xd, pos, n_dec, l_dec, row_dec,
                                    state_delta_fwd[:, lyr], state_delta_bwd[:, lyr])
    yc, yd = _expert_call(xsc, xsd, w_e_gate[lyr], w_e_up[lyr], w_e_down[lyr])
    reqs_c = FINAL_CTX_REQS_PER_STEP if n_ctx % FINAL_CTX_REQS_PER_STEP == 0 else 1
    y_prompt = _final_call(x1c, yc, gc, ptc, mod4, row_ctx, l2g, l2b, n_ctx, l_ctx, reqs_c)
    y_sample = _final_call(x1d, yd, gd, ptd, mod4, row_dec, l2g, l2b, n_dec, l_dec, 1)

    return (y_prompt.reshape(n_ctx, l_ctx, d), y_sample.reshape(n_dec, l_dec, d),
            s_f[:, None], s_b[:, None])
```

```python
import functools
import math

import jax
import jax.numpy as jnp
import numpy as np
from jax import lax
from jax.experimental import pallas as pl
from jax.experimental.pallas import tpu as pltpu

F32 = jnp.float32
BF16 = jnp.bfloat16
HIGHEST = lax.Precision.HIGHEST

D_MODEL = 1024
DEPTH = 1
GRID_W = 64
DN_HEADS = 4
DN_DK = 128
DN_DV = 128
DN_WIDTH = DN_HEADS * DN_DV
HY_WIDTH = D_MODEL // 2
HY_EMB = 33
HY_FFN = 64
HY_TARGET = 1e-2
HY_FAST_DECAY = 0.3
HY_SLOW_DECAY = 1.5
N_EXPERTS = 16
EC_FACTOR = 2
N_MOD = 6
ALPHA = (2 * DEPTH) ** 0.25
LN_EPS = 1e-5
RMS_EPS = 1e-6
QKV_WIDTH = 2 * DN_HEADS * DN_DK + DN_WIDTH

LANES = 128
INPROJ_TILE = 1024
INPROJ_BLOCK_ROWS = 512
MIX_TILE = 1024
MIX_SUB_ROWS = 256
MIX_MIN_STEPS = 4
MOD_COL_BLOCK = 1024
DN_CHUNK = 256
INV_BASE = 8
DN_REQS_PER_STEP = 2
HY_ROWS_PER_STEP = 1024
FINAL_CTX_REQS_PER_STEP = 4
EXPERT_ROW_BLOCK = 512
VMEM_LIMIT = 56 * 1024 * 1024
MIN_NORMAL_F32_BITS = 0x00800000
ACT_DT = BF16
ROUTE_ROWS_PER_STEP = 1024
RANK_COUNT_MAX_SEQ = 256


def _cparams(n_axes):
    return pltpu.CompilerParams(dimension_semantics=("arbitrary",) * n_axes,
                                vmem_limit_bytes=VMEM_LIMIT)


def _bmm(a, b):
    return jnp.einsum("bmk,bkn->bmn", a.astype(BF16), b.astype(BF16), preferred_element_type=F32)


def _bmm_nt(a, b):
    return jnp.einsum("bmk,bnk->bmn", a.astype(BF16), b.astype(BF16), preferred_element_type=F32)


def _bmm_tn(a, b):
    return jnp.einsum("bkm,bkn->bmn", a.astype(BF16), b.astype(BF16), preferred_element_type=F32)


def _mm_f32(a, b):
    return jnp.dot(a, b, precision=HIGHEST, preferred_element_type=F32)


def _sigmoid(x):
    return 0.5 * jnp.tanh(0.5 * x) + 0.5


def _silu(x):
    h = 0.5 * x
    return h * (jnp.tanh(h) + 1.0)


def _softplus(x):
    return jnp.maximum(x, 0.0) + jnp.log1p(jnp.exp(-jnp.abs(x)))


def _ln(x):
    mu = jnp.mean(x, axis=-1, keepdims=True)
    xc = x - mu
    var = jnp.mean(xc * xc, axis=-1, keepdims=True)
    return xc * lax.rsqrt(var + LN_EPS)


def _l2norm(x, scale=1.0):
    return x * (lax.rsqrt(jnp.sum(x * x, axis=-1, keepdims=True) + 1e-6) * scale)


def _iota(shape, dim):
    return lax.broadcasted_iota(jnp.int32, shape, dim)


def _short_conv3(x, w, first, last):
    xm = jnp.where(first, 0.0, pltpu.roll(x, 1, 0))
    xp = jnp.where(last, 0.0, pltpu.roll(x, x.shape[0] - 1, 0))
    return w[0:1, :] * xm + w[1:2, :] * x + w[2:3, :] * xp


def _mod_kernel(n_rows, ct_ref, w_ref, b_ref, o_ref):
    s = _silu(ct_ref[...])
    w = w_ref[...]
    b = b_ref[...]
    o_ref[...] = jnp.zeros(o_ref.shape, F32)
    for r in range(n_rows):
        o_ref[r:r + 1, :] = jnp.sum(w * s[:, r:r + 1], axis=0, keepdims=True) + b


def _mod_call(cond_t, n_rows, w_mod, b_mod):
    d = D_MODEL
    tn = MOD_COL_BLOCK
    return pl.pallas_call(
        functools.partial(_mod_kernel, n_rows),
        grid=(N_MOD * d // tn,),
        in_specs=[pl.BlockSpec((d, 8), lambda j: (0, 0)),
                  pl.BlockSpec((d, tn), lambda j: (0, j)),
                  pl.BlockSpec((1, tn), lambda j: (0, j))],
        out_specs=pl.BlockSpec((8, tn), lambda j: (0, j)),
        out_shape=jax.ShapeDtypeStruct((8, N_MOD * d), F32),
        compiler_params=_cparams(1),
        name="mod",
    )(cond_t, w_mod, b_mod.reshape(1, -1))


def _mod_spec(row_of_step):
    return pl.BlockSpec((1, N_MOD, 1, D_MODEL), lambda *idx: (row_of_step(*idx), 0, 0, 0))


def _inproj_kernel(has_pos, conv_seq, *refs):
    refs = list(refs)
    x_ref = refs.pop(0)
    pos_ref = refs.pop(0) if has_pos else None
    mod_ref, w_ref = refs.pop(0), refs.pop(0)
    if conv_seq:
        cwa_ref, cwb_ref, cbb_ref = refs.pop(0), refs.pop(0), refs.pop(0)
    qkv_o, z_o, ba_o, hy_o, g_o = refs
    sh1 = mod_ref[0, 0]
    sc1 = mod_ref[0, 1]
    o_z = QKV_WIDTH
    o_ba = o_z + DN_WIDTH
    n_ba = 4 * DN_HEADS
    o_g = n_ba + 3 * HY_WIDTH
    lanes = lambda j: slice(j * LANES, (j + 1) * LANES)
    tm = x_ref.shape[0]
    sub = min(tm, INPROJ_BLOCK_ROWS)
    blocks = [slice(r0, r0 + sub) for r0 in range(0, tm, sub)]

    if conv_seq:
        row = _iota((sub, LANES), 0) % conv_seq
        first = row == 0
        last = row == conv_seq - 1

    def project(rows):
        x = x_ref[rows, :]
        if has_pos:
            x = x + pos_ref[rows, :]
        u = (_ln(x) * (1.0 + sc1) + sh1).astype(BF16)
        return (jnp.dot(u, w_ref[:, :o_ba], preferred_element_type=F32),
                jnp.dot(u, w_ref[:, o_ba:], preferred_element_type=F32))

    def epilogue(rows, head, tail):
        z_o[rows, :] = head[:, o_z:].astype(z_o.dtype)
        ba_o[rows, :] = tail[:, :LANES]
        g_o[rows, :] = tail[:, o_g:].astype(g_o.dtype)
        hy = tail[:, n_ba:o_g]
        if not conv_seq:
            qkv_o[rows, :] = head[:, :o_z].astype(qkv_o.dtype)
            hy_o[rows, :] = hy.astype(hy_o.dtype)
            return
        for j in range(QKV_WIDTH // LANES):
            y = _silu(_short_conv3(head[:, lanes(j)], cwa_ref[:, lanes(j)], first, last))
            if j < 2 * DN_HEADS:
                y = _l2norm(y, DN_DK ** -0.5 if j < DN_HEADS else 1.0)
            qkv_o[rows, lanes(j)] = y.astype(qkv_o.dtype)
        nj = HY_WIDTH // LANES
        for j in range(nj):
            parts = [_short_conv3(hy[:, lanes(p * nj + j)], cwb_ref[:, lanes(p * nj + j)], first, last)
                     + cbb_ref[:, lanes(p * nj + j)] for p in range(3)]
            hy_o[rows, lanes(j)] = parts[0].astype(hy_o.dtype)
            hy_o[rows, lanes(nj + j)] = (parts[1] * parts[2]).astype(hy_o.dtype)

    pending = None
    for rows in blocks:
        cur = (rows, *project(rows))
        if pending is not None:
            epilogue(*pending)
        pending = cur
    epilogue(*pending)


def _inproj_call(x2d, pos, mod4, mod_row, w, conv_a_w, conv_b_w, conv_b_b, seq, act_dt):
    n, d = x2d.shape
    fuse = seq <= INPROJ_BLOCK_ROWS and INPROJ_TILE % seq == 0
    tm = min(INPROJ_TILE if fuse else INPROJ_BLOCK_ROWS, n)
    assert n % tm == 0 and (not fuse or tm % seq == 0)
    conv_seq = seq if fuse else None
    row = lambda i: (i, 0)
    const = lambda i: (0, 0)
    in_specs = [pl.BlockSpec((tm, d), row)]
    args = [x2d]
    if pos is not None:
        tiles = pos.shape[0] // tm
        in_specs.append(pl.BlockSpec((tm, d), lambda i: (i % tiles, 0)))
        args.append(pos)
    in_specs.append(_mod_spec(lambda i: mod_row(i * tm)))
    args.append(mod4)
    in_specs.append(pl.BlockSpec(w.shape, const, pipeline_mode=pl.Buffered(1)))
    args.append(w)
    if conv_seq:
        for a in (conv_a_w, conv_b_w, conv_b_b.reshape(1, -1)):
            in_specs.append(pl.BlockSpec(a.shape, const))
            args.append(a)
    hy_width = (2 if conv_seq else 3) * HY_WIDTH
    widths = (QKV_WIDTH, DN_WIDTH, LANES, hy_width, 2 * D_MODEL)
    dts = (act_dt, act_dt, F32, act_dt, act_dt)
    outs = pl.pallas_call(
        functools.partial(_inproj_kernel, pos is not None, conv_seq),
        grid=(n // tm,),
        in_specs=in_specs,
        out_specs=[pl.BlockSpec((tm, w), row) for w in widths],
        out_shape=[jax.ShapeDtypeStruct((n, w), dt) for w, dt in zip(widths, dts)],
        compiler_params=_cparams(1),
        name="inproj",
    )(*args)
    return outs, conv_seq is not None


def _inv_unit_tri(lm, ri, ci, block):
    c = lm.shape[-1]

    def same_block(h):
        sh = h.bit_length() - 1
        return (ri >> sh) == (ci >> sh)

    def unfold(f, h):
        return jnp.where(same_block(h), jnp.concatenate([f] * (c // h), axis=1), 0.0)

    base = INV_BASE
    n_nat = jnp.where(same_block(base), -lm, 0.0)
    n_fold = n_nat[:, 0:base, :]
    for i in range(1, c // base):
        n_fold = n_fold + n_nat[:, i * base:(i + 1) * base, :]
    eye_fold = jnp.where((_iota((base, c), 1) & (base - 1)) == _iota((base, c), 0), 1.0, 0.0)
    t = eye_fold + n_fold
    p_fold, p_nat = n_fold, n_nat
    span = 2
    while span < base:
        p_fold = _bmm(p_fold, p_nat)
        p_nat = unfold(p_fold, base)
        t = t + _bmm(t, p_nat)
        span *= 2
    s = base
    while s < block:
        sh = s.bit_length() - 1
        t_nat = unfold(t, s)
        even = ((_iota((s, c), 1) >> sh) & 1) == 0
        t2 = jnp.concatenate([jnp.where(even, t, 0.0), jnp.where(even, 0.0, t)], axis=1)
        lo = jnp.where(same_block(2 * s) & ~same_block(s), lm, 0.0)
        t = t2 - _bmm(_bmm(t2, lo), t_nat)
        s *= 2
    return t


def _dn_chunk(q, k, v, gc_col, gc_row, beta, e_col, e_rest, e_all, s_prev, same_chunk, ri, ci):
    c = DN_CHUNK
    nh = q.shape[0] // 2
    half = c // 2
    if same_chunk:
        gram = _bmm_nt(k[:nh], k[:nh])
        qk = _bmm_nt(q[:nh], k[:nh])
        gram = jnp.concatenate([gram, gram], 0)
        qk = jnp.concatenate([qk, qk], 0)
    else:
        gram = _bmm_nt(k, k)
        qk = _bmm_nt(q, k)
    diff = gc_col - gc_row
    incl = jnp.concatenate([jnp.broadcast_to(ri >= ci, (nh, c, c)), jnp.broadcast_to(ri <= ci, (nh, c, c))], 0)
    strict = incl & (ri != ci)
    decay = jnp.exp(jnp.where(incl, diff, -jnp.inf))
    lm = jnp.where(strict, beta * gram * decay, 0.0)
    a_intra = qk * decay
    t = _inv_unit_tri(lm, ri, ci, half)
    t1 = t[:, :, :half]
    t2 = t[:, :, half:]
    rhs = jnp.concatenate([v * beta, k * (beta * e_col)], axis=2)
    y1 = _bmm(t1, rhs[:, :half, :])
    y2 = _bmm(t2, rhs[:, half:, :])
    c_f = _bmm(lm[:nh, half:, :half], y1[:nh])
    c_b = _bmm(lm[nh:, :half, half:], y2[nh:])
    c_f = _bmm(t2[:nh], c_f)
    c_b = _bmm(t1[nh:], c_b)
    sol = jnp.concatenate([jnp.concatenate([y1[:nh], y2[:nh] - c_f], 1),
                           jnp.concatenate([y1[nh:] - c_b, y2[nh:]], 1)], 0)
    u = sol[:, :, :DN_DV]
    w = sol[:, :, DN_DV:]
    ks = k * e_rest
    if s_prev is None:
        v_new = u
        o = _bmm(a_intra, v_new)
        s_new = _bmm_tn(ks, v_new)
    else:
        v_new = u - _bmm(w, s_prev)
        o = _bmm(q * e_col, s_prev) + _bmm(a_intra, v_new)
        s_new = s_prev * e_all + _bmm_tn(ks, v_new)
    return o, s_new


def _deltanet_kernel(seq, reqs, zero_init, pre_conv, *refs):
    qkv_ref, cw_ref, z_ref, ba_ref, prm_ref, gn_ref = refs[:6]
    refs = refs[6:]
    if not zero_init:
        s0f_ref, s0b_ref = refs[:2]
        refs = refs[2:]
    o_ref, sf_ref, sb_ref, q_s, k_s, v_s, g_s, b_s, o_s, st_s = refs
    c = DN_CHUNK
    n = seq // c
    nh = DN_HEADS
    nf = reqs * nh
    lanes = lambda j: slice(j * LANES, (j + 1) * LANES)

    if not pre_conv:
        row = _iota((seq, LANES), 0)
        first_row = row == 0
        last_row = row == seq - 1
    for r in range(reqs):
        rows = slice(r * seq, (r + 1) * seq)

        def qkv_part(j):
            x = qkv_ref[rows, lanes(j)].astype(F32)
            return x if pre_conv else _silu(_short_conv3(x, cw_ref[:, lanes(j)], first_row, last_row))

        for h in range(nh):
            q = qkv_part(h)
            k = qkv_part(nh + h)
            q_s[r * nh + h] = q if pre_conv else _l2norm(q, DN_DK ** -0.5)
            k_s[r * nh + h] = k if pre_conv else _l2norm(k)
            v_s[r * nh + h] = qkv_part(2 * nh + h)

    ba = ba_ref[...]
    b_s[...] = _sigmoid(ba)
    g_s[...] = -jnp.exp(prm_ref[0:1, :]) * _softplus(ba + prm_ref[1:2, :])

    ri = _iota((c, c), 0)
    ci = _iota((c, c), 1)
    tri_l = jnp.where(ri >= ci, 1.0, 0.0).astype(F32)
    tri_u = jnp.where(ri <= ci, 1.0, 0.0).astype(F32)

    def chunk_pair(idx_f, idx_b, s_prev):
        def ds(idx, base=0):
            start = idx * c
            start = start if isinstance(start, int) else pl.multiple_of(start, c)
            return pl.ds(base + start, c)

        col_f, col_b, row_f, row_b, beta_f, beta_b = [], [], [], [], [], []
        for r in range(reqs):
            cs_f = _mm_f32(tri_l, g_s[ds(idx_f, r * seq), :])
            cs_b = _mm_f32(tri_u, g_s[ds(idx_b, r * seq), :])
            cst_f = cs_f.T
            cst_b = cs_b.T
            b_f = b_s[ds(idx_f, r * seq), :]
            b_b = b_s[ds(idx_b, r * seq), :]
            for h in range(nh):
                jf, jb = 2 * nh + h, 3 * nh + h
                col_f.append(cs_f[:, jf:jf + 1])
                col_b.append(cs_b[:, jb:jb + 1])
                row_f.append(cst_f[jf:jf + 1, :])
                row_b.append(cst_b[jb:jb + 1, :])
                beta_f.append(b_f[:, h:h + 1])
                beta_b.append(b_b[:, nh + h:nh + h + 1])
        gc_col = jnp.stack(col_f + col_b, 0)
        gc_row = jnp.stack(row_f + row_b, 0)
        beta = jnp.stack(beta_f + beta_b, 0)
        e_col = jnp.exp(gc_col)
        gl = jnp.concatenate([gc_col[:nf, c - 1:c, :], gc_col[nf:, 0:1, :]], 0)
        e_rest = jnp.exp(gl - gc_col)
        e_all = jnp.exp(gl)
        sl_f, sl_b = ds(idx_f), ds(idx_b)
        both = lambda ref: jnp.concatenate([ref[:, sl_f, :], ref[:, sl_b, :]], 0)
        o, s_new = _dn_chunk(both(q_s), both(k_s), both(v_s), gc_col, gc_row, beta, e_col, e_rest, e_all,
                             s_prev, n == 1, ri, ci)
        o_s[:nf, sl_f, :] = o[:nf]
        o_s[nf:, sl_b, :] = o[nf:]
        return s_new

    if not zero_init:
        s0 = jnp.concatenate([s0f_ref[r] for r in range(reqs)] + [s0b_ref[r] for r in range(reqs)], 0)
    if n == 1:
        s_fin = chunk_pair(0, 0, None if zero_init else s0)
    else:
        st_s[...] = jnp.zeros(st_s.shape, F32) if zero_init else s0

        def body(i, carry):
            st_s[...] = chunk_pair(i, n - 1 - i, st_s[...])
            return carry

        lax.fori_loop(0, n, body, 0)
        s_fin = st_s[...]
    for r in range(reqs):
        sf_ref[r] = s_fin[r * nh:(r + 1) * nh]
        sb_ref[r] = s_fin[nf + r * nh:nf + (r + 1) * nh]

    gn = gn_ref[...]
    for r in range(reqs):
        rows = slice(r * seq, (r + 1) * seq)
        for h in range(nh):
            o = o_s[r * nh + h] + o_s[nf + r * nh + h]
            o = o * lax.rsqrt(jnp.mean(o * o, axis=-1, keepdims=True) + RMS_EPS) * gn
            o_ref[rows, lanes(h)] = (o * _silu(z_ref[rows, lanes(h)].astype(F32))).astype(o_ref.dtype)


def _deltanet_call(qkv, z, ba, conv_w, prm, gn, s0f, s0b, bsz, seq, reqs, pre_conv, act_dt):
    nh = DN_HEADS
    zero_init = s0f is None
    rows = reqs * seq
    st = pl.BlockSpec((reqs, nh, DN_DK, DN_DV), lambda b: (b, 0, 0, 0))
    in_specs = [pl.BlockSpec((rows, QKV_WIDTH), lambda b: (b, 0)),
                pl.BlockSpec((3, QKV_WIDTH), lambda b: (0, 0)),
                pl.BlockSpec((rows, DN_WIDTH), lambda b: (b, 0)),
                pl.BlockSpec((rows, LANES), lambda b: (b, 0)),
                pl.BlockSpec((8, LANES), lambda b: (0, 0)),
                pl.BlockSpec((1, LANES), lambda b: (0, 0))]
    args = [qkv, conv_w, z, ba, prm, gn]
    if not zero_init:
        in_specs += [st, st]
        args += [s0f, s0b]
    vm = lambda shape: pltpu.VMEM(shape, F32)
    per_head = vm((reqs * nh, seq, LANES))
    return pl.pallas_call(
        functools.partial(_deltanet_kernel, seq, reqs, zero_init, pre_conv),
        grid=(bsz // reqs,),
        in_specs=in_specs,
        out_specs=[pl.BlockSpec((rows, DN_WIDTH), lambda b: (b, 0)), st, st],
        out_shape=[jax.ShapeDtypeStruct((bsz * seq, DN_WIDTH), act_dt),
                   jax.ShapeDtypeStruct((bsz, nh, DN_DK, DN_DV), F32),
                   jax.ShapeDtypeStruct((bsz, nh, DN_DK, DN_DV), F32)],
        scratch_shapes=[per_head, per_head, per_head, vm((rows, LANES)), vm((rows, LANES)),
                        vm((2 * reqs * nh, seq, LANES)), vm((2 * reqs * nh, DN_DK, DN_DV))],
        compiler_params=_cparams(1),
        name="deltanet",
    )(*args)


def _dft_mats(seq):
    n = 2 * seq
    f = np.arange(seq)[:, None]
    s = np.arange(seq)[None, :]
    ang = 2.0 * np.pi * ((f * s) % n) / n
    fwd = np.concatenate([np.cos(ang), -np.sin(ang)], axis=0)
    fwd[seq, :] = np.cos(np.pi * np.arange(seq))
    t = (np.arange(seq) + seq // 2)[:, None]
    ff = np.arange(seq)[None, :]
    ang2 = 2.0 * np.pi * ((t * ff) % n) / n
    inv_r = 2.0 * np.cos(ang2) / n
    inv_i = -2.0 * np.sin(ang2) / n
    inv_r[:, 0] = 1.0 / n
    inv_i[:, 0] = np.cos(np.pi * t[:, 0]) / n
    inv = np.concatenate([inv_r, inv_i], axis=1)
    return fwd.astype(np.float32), inv.astype(np.float32)


def _filter_feats(seq):
    t = np.linspace(0.0, 1.0, seq)[:, None]
    bands = (HY_EMB - 1) // 2
    ang = (2.0 * math.pi * np.arange(seq) / seq)[:, None] * np.linspace(1e-4, bands - 1, bands)[None, :]
    feats = np.concatenate([t, np.cos(ang), -np.sin(ang)], -1)
    deltas = np.abs(np.linspace(math.log(HY_TARGET) / HY_SLOW_DECAY, math.log(HY_TARGET) / HY_FAST_DECAY,
                                HY_WIDTH))
    offset = np.abs(np.arange(seq) - seq // 2) / (seq // 2)
    window = np.exp(-offset[:, None] * deltas[None, :])
    feats = np.pad(feats, ((0, 0), (0, LANES - HY_EMB)))
    return jnp.asarray(feats, dtype=F32), jnp.asarray(window, dtype=F32)


def _filter_spectrum(seq, feats_ref, win_ref, w1_ref, b1_ref, w2_ref, b2_ref, w3_ref, fr_ref, fwd_hi_ref,
                     fwd_lo_ref, ha_s, hb_s, hd_s):
    fr = fr_ref[...]
    hid = jnp.sin(fr * (_mm_f32(feats_ref[...], w1_ref[...]) + b1_ref[...]))
    hid = jnp.sin(fr * (_mm_f32(hid, w2_ref[...]) + b2_ref[...]))
    filt = _mm_f32(hid, w3_ref[...]) * win_ref[...]
    filt = filt / (jnp.sum(jnp.abs(filt), axis=0, keepdims=True) + 1e-6)
    filt_hi = filt.astype(BF16)
    filt_lo = (filt - filt_hi.astype(F32)).astype(BF16)
    fwd_hi = fwd_hi_ref[...]
    spec = (jnp.dot(fwd_hi, filt_hi, preferred_element_type=F32)
            + jnp.dot(fwd_hi, filt_lo, preferred_element_type=F32)
            + jnp.dot(fwd_lo_ref[...], filt_hi, preferred_element_type=F32))
    h_re = spec[:seq, :]
    h_im = spec[seq:, :]
    first = _iota(h_re.shape, 0) == 0
    ha_s[...] = h_re
    hb_s[...] = jnp.where(first, 0.0, h_im)
    hd_s[...] = jnp.where(first, h_im, h_re)


def _hyena_kernel(seq, reqs, pre_conv, *refs):
    refs = list(refs)
    n_tok = 2 if pre_conv else 9
    tok_refs, refs = refs[:n_tok], refs[n_tok:]
    filt_refs, refs = refs[:8], refs[8:]
    skip_ref, fwd_ref, fwd_lo_ref, inv_ref, o_ref, ha_s, hb_s, hd_s = refs

    @pl.when(pl.program_id(0) == 0)
    def _():
        _filter_spectrum(seq, *filt_refs, fwd_ref, fwd_lo_ref, ha_s, hb_s, hd_s)

    if pre_conv:
        x0_ref, uu_ref = tok_refs
    else:
        x0_ref, x1_ref, v_ref, c0_ref, c1_ref, c2_ref, b0_ref, b1_ref, b2_ref = tok_refs
        row = _iota((seq, HY_WIDTH), 0)
        first = row == 0
        last = row == seq - 1
    hb = hb_s[...]
    for r in range(reqs):
        rows = slice(r * seq, (r + 1) * seq)
        if pre_conv:
            x0 = x0_ref[rows, :].astype(F32)
            uu = uu_ref[rows, :].astype(F32)
        else:
            conv = lambda x_ref, w_ref, b_ref: (_short_conv3(x_ref[rows, :].astype(F32), w_ref[...], first, last)
                                                + b_ref[...])
            x0 = conv(x0_ref, c0_ref, b0_ref)
            uu = conv(x1_ref, c1_ref, b1_ref) * conv(v_ref, c2_ref, b2_ref)
        spec = jnp.dot(fwd_ref[...], uu.astype(BF16), preferred_element_type=F32)
        u_re = spec[:seq, :]
        u_im = spec[seq:, :]
        y_re = u_re * ha_s[...] - u_im * hb
        y_im = u_re * hb + u_im * hd_s[...]
        y = jnp.concatenate([y_re, y_im], axis=0).astype(BF16)
        cv = jnp.dot(inv_ref[...], y, preferred_element_type=F32)
        o_ref[rows, :] = (x0 * (cv + uu * skip_ref[...])).astype(o_ref.dtype)


def _hyena_call(hy, pre_conv, conv_w, conv_b, filt_params, skip, fwd_hi, fwd_lo, inv, bsz, seq, act_dt):
    w1, b1, w2, b2, w3, freq = filt_params
    cb = HY_WIDTH
    reqs = max(1, min(bsz, HY_ROWS_PER_STEP // seq))
    assert bsz % reqs == 0
    tok = lambda part: pl.BlockSpec((reqs * seq, cb), lambda b: (b, part))
    const = lambda a: pl.BlockSpec(a.shape, lambda b: (0,) * a.ndim, pipeline_mode=pl.Buffered(1))
    feats, window = _filter_feats(seq)
    w1p = jnp.pad(w1, ((0, LANES - HY_EMB), (0, 0)))
    filt_args = [feats, window, w1p, b1.reshape(1, -1), w2, b2.reshape(1, -1), w3, freq.reshape(1, -1)]
    tail_args = filt_args + [skip.reshape(1, -1), fwd_hi, fwd_lo, inv]
    tail_specs = [const(a) for a in tail_args]
    if pre_conv:
        in_specs = [tok(0), tok(1)] + tail_specs
        args = [hy, hy] + tail_args
    else:
        conv_b2 = conv_b.reshape(1, -1)
        part3 = lambda a, part: pl.BlockSpec((a.shape[0], cb), lambda b: (0, part))
        in_specs = ([tok(0), tok(1), tok(2)] + [part3(conv_w, p) for p in range(3)]
                    + [part3(conv_b2, p) for p in range(3)] + tail_specs)
        args = [hy, hy, hy, conv_w, conv_w, conv_w, conv_b2, conv_b2, conv_b2] + tail_args
    spec_buf = pltpu.VMEM((seq, HY_WIDTH), F32)
    return pl.pallas_call(
        functools.partial(_hyena_kernel, seq, reqs, pre_conv),
        grid=(bsz // reqs,),
        in_specs=in_specs,
        out_specs=tok(0),
        out_shape=jax.ShapeDtypeStruct((bsz * seq, HY_WIDTH), act_dt),
        scratch_shapes=[spec_buf, spec_buf, spec_buf],
        compiler_params=_cparams(1),
        name="hyena",
    )(*args)


def _mix_kernel(has_pos, *refs):
    refs = list(refs)
    x_ref = refs.pop(0)
    pos_ref = refs.pop(0) if has_pos else None
    (oa_ref, ob_ref, g_ref, mod_ref, wua, wub, wout_half, l1g, l1b, wr_hi, wr_lo, x1_o, u2_o, p_o) = refs
    d = D_MODEL
    tm = x_ref.shape[0]
    sub = min(tm, MIX_SUB_ROWS)
    blocks = [slice(r0, r0 + sub) for r0 in range(0, tm, sub)]
    g1 = mod_ref[0, 2]
    sh2 = mod_ref[0, 3]
    sc2 = mod_ref[0, 4]

    mixed = []
    for rows in blocks:
        th = jnp.tanh(0.5 * g_ref[rows, :].astype(F32))
        a = jnp.dot(oa_ref[rows, :], wua[...], preferred_element_type=F32)
        b = jnp.dot(ob_ref[rows, :], wub[...], preferred_element_type=F32)
        up2 = (th[:, :d] * a + a) + (th[:, d:] * b + b)
        mixed.append(jnp.dot(up2.astype(BF16), wout_half[...], preferred_element_type=F32))

    for rows, mx in zip(blocks, mixed):
        x = x_ref[rows, :]
        if has_pos:
            x = x + pos_ref[rows, :]
        x1 = _ln(ALPHA * x + g1 * mx) * l1g[...] + l1b[...]
        x1_o[rows, :] = x1.astype(x1_o.dtype)
        u2 = _ln(x1) * (1.0 + sc2) + sh2
        u2_hi = u2.astype(BF16)
        u2_o[rows, :] = u2_hi
        u2_lo = (u2 - u2_hi.astype(F32)).astype(BF16)
        logits = (jnp.dot(u2_hi, wr_hi[...], preferred_element_type=F32)
                  + jnp.dot(u2_hi, wr_lo[...], preferred_element_type=F32)
                  + jnp.dot(u2_lo, wr_hi[...], preferred_element_type=F32))
        lane = _iota(logits.shape, 1)
        logits = jnp.where(lane < N_EXPERTS, logits, -jnp.inf)
        m = jnp.max(logits, axis=-1, keepdims=True)
        e = jnp.exp(logits - m)
        p_o[rows, :] = e / jnp.sum(e, axis=-1, keepdims=True)


def _mix_call(x2d, pos, o_a, o_b, gates, mod4, mod_row, wua, wub, wout_half, l1g, l1b, wr_hi, wr_lo):
    n, d = x2d.shape
    tm = min(MIX_TILE, max(n // MIX_MIN_STEPS, MIX_SUB_ROWS))
    assert n % tm == 0
    row = lambda i: (i, 0)
    const = lambda i: (0, 0)
    in_specs = [pl.BlockSpec((tm, d), row)]
    args = [x2d]
    if pos is not None:
        tiles = pos.shape[0] // tm
        in_specs.append(pl.BlockSpec((tm, d), lambda i: (i % tiles, 0)))
        args.append(pos)
    in_specs += [pl.BlockSpec((tm, DN_WIDTH), row), pl.BlockSpec((tm, HY_WIDTH), row),
                 pl.BlockSpec((tm, 2 * d), row),
                 _mod_spec(lambda i: mod_row(i * tm))]
    args += [o_a, o_b, gates, mod4]
    for w in (wua, wub, wout_half, l1g, l1b, wr_hi, wr_lo):
        in_specs.append(pl.BlockSpec(w.shape, const, pipeline_mode=pl.Buffered(1)))
        args.append(w)
    return pl.pallas_call(
        functools.partial(_mix_kernel, pos is not None),
        grid=(n // tm,),
        in_specs=in_specs,
        out_specs=[pl.BlockSpec((tm, d), row), pl.BlockSpec((tm, d), row), pl.BlockSpec((tm, LANES), row)],
        out_shape=[jax.ShapeDtypeStruct((n, d), ACT_DT), jax.ShapeDtypeStruct((n, d), BF16),
                   jax.ShapeDtypeStruct((n, LANES), F32)],
        compiler_params=_cparams(1),
        name="mix",
    )(*args)


def _route_kernel(seq, cap, reqs, p_ref, u_ref, tri_ref, xs_ref, slot_ref, pt_ref, g_s):
    e_n = N_EXPERTS
    tri = tri_ref[...]
    jrow = _iota((cap, seq), 0).astype(F32)
    if seq <= RANK_COUNT_MAX_SEQ:
        earlier = _iota((seq, seq), 0) < _iota((seq, seq), 1)
    for r in range(reqs):
        rows = slice(r * seq, (r + 1) * seq)
        p = p_ref[rows, :]
        pt = p.T[:e_n, :]

        if seq <= RANK_COUNT_MAX_SEQ:
            ranks = []
            for e in range(e_n):
                pc = p[:, e:e + 1]
                pr = pt[e:e + 1, :]
                beats = (pc > pr) | (earlier & (pc == pr))
                ranks.append(jnp.sum(jnp.where(beats, 1.0, 0.0), axis=0, keepdims=True))
            sel = jnp.concatenate(ranks, axis=0) < cap
        else:
            def ok(cand):
                cnt = jnp.sum(jnp.where(pt >= pltpu.bitcast(cand, F32), 1.0, 0.0), axis=1, keepdims=True)
                return (cnt >= cap) & (cand >= MIN_NORMAL_F32_BITS)

            def search(i, cur):
                lo = 28 - 2 * i
                c1, c2, c3 = cur | (1 << lo), cur | (2 << lo), cur | (3 << lo)
                return jnp.where(ok(c3), c3, jnp.where(ok(c2), c2, jnp.where(ok(c1), c1, cur)))

            top = jnp.full((e_n, 1), 1 << 30, jnp.int32)
            cur = jnp.where(ok(top), top, 0)
            thr = pltpu.bitcast(lax.fori_loop(0, 15, search, cur), F32)
            gt = pt > thr
            eq = pt == thr
            n_gt = jnp.sum(jnp.where(gt, 1.0, 0.0), axis=1, keepdims=True)
            eq_rank = jnp.dot(jnp.where(eq, 1.0, 0.0).astype(BF16), tri, preferred_element_type=F32)
            sel = gt | (eq & (eq_rank < cap - n_gt))
        pos = jnp.dot(jnp.where(sel, 1.0, 0.0).astype(BF16), tri, preferred_element_type=F32)
        slot = jnp.where(sel, pos, -1.0)
        slot_ref[r] = slot
        pt_ref[r] = pt
        for e in range(e_n):
            g_s[r, e * cap:(e + 1) * cap, :] = jnp.where(slot[e:e + 1, :] == jrow, 1.0, 0.0).astype(BF16)
        xs = jnp.dot(g_s[r], u_ref[rows, :], preferred_element_type=F32)
        for e in range(e_n):
            xs_ref[e, r * cap:(r + 1) * cap, :] = xs[e * cap:(e + 1) * cap, :].astype(xs_ref.dtype)


def _route_call(probs, u2, bsz, seq):
    cap = EC_FACTOR * seq // N_EXPERTS
    reqs = max(1, min(bsz, ROUTE_ROWS_PER_STEP // seq))
    assert bsz % reqs == 0
    tri = jnp.asarray(np.triu(np.ones((seq, seq), np.float32), 1), dtype=BF16)
    per_req = pl.BlockSpec((reqs, N_EXPERTS, seq), lambda b: (b, 0, 0))
    return pl.pallas_call(
        functools.partial(_route_kernel, seq, cap, reqs),
        grid=(bsz // reqs,),
        in_specs=[pl.BlockSpec((reqs * seq, LANES), lambda b: (b, 0)),
                  pl.BlockSpec((reqs * seq, D_MODEL), lambda b: (b, 0)),
                  pl.BlockSpec((seq, seq), lambda b: (0, 0), pipeline_mode=pl.Buffered(1))],
        out_specs=[pl.BlockSpec((N_EXPERTS, reqs * cap, D_MODEL), lambda b: (0, b, 0)), per_req, per_req],
        out_shape=[jax.ShapeDtypeStruct((N_EXPERTS, bsz * cap, D_MODEL), BF16),
                   jax.ShapeDtypeStruct((bsz, N_EXPERTS, seq), F32),
                   jax.ShapeDtypeStruct((bsz, N_EXPERTS, seq), F32)],
        scratch_shapes=[pltpu.VMEM((reqs, N_EXPERTS * cap, seq), BF16)],
        compiler_params=_cparams(1),
        name="route",
    )(probs, u2, tri)


def _expert_kernel(xc_ref, xd_ref, wg_ref, wu_ref, wd_ref, yc_ref, yd_ref):
    wg = wg_ref[0].astype(BF16)
    wu = wu_ref[0].astype(BF16)
    wd = wd_ref[0].astype(BF16)
    for x_ref, y_ref in ((xc_ref, yc_ref), (xd_ref, yd_ref)):
        rows = x_ref.shape[1]
        rb = min(rows, EXPERT_ROW_BLOCK)
        for r0 in range(0, rows, rb):
            x = x_ref[0, r0:r0 + rb, :]
            hg = jnp.dot(x, wg, preferred_element_type=F32)
            hu = jnp.dot(x, wu, preferred_element_type=F32)
            y = jnp.dot((_silu(hg) * hu).astype(BF16), wd, preferred_element_type=F32)
            y_ref[0, r0:r0 + rb, :] = y.astype(y_ref.dtype)


def _expert_call(xs_c, xs_d, w_gate, w_up, w_down):
    e_n, rc, d = xs_c.shape
    rd = xs_d.shape[1]
    ff = w_gate.shape[2]
    return pl.pallas_call(
        _expert_kernel,
        grid=(e_n,),
        in_specs=[pl.BlockSpec((1, rc, d), lambda e: (e, 0, 0)),
                  pl.BlockSpec((1, rd, d), lambda e: (e, 0, 0)),
                  pl.BlockSpec((1, d, ff), lambda e: (e, 0, 0)),
                  pl.BlockSpec((1, d, ff), lambda e: (e, 0, 0)),
                  pl.BlockSpec((1, ff, d), lambda e: (e, 0, 0))],
        out_specs=[pl.BlockSpec((1, rc, d), lambda e: (e, 0, 0)),
                   pl.BlockSpec((1, rd, d), lambda e: (e, 0, 0))],
        out_shape=[jax.ShapeDtypeStruct((e_n, rc, d), BF16), jax.ShapeDtypeStruct((e_n, rd, d), BF16)],
        compiler_params=_cparams(1),
        name="experts",
    )(xs_c, xs_d, w_gate, w_up, w_down)


def _final_kernel(seq, cap, reqs, x1_ref, y_ref, slot_ref, pt_ref, mod_ref, l2g, l2b, o_ref, w_s):
    jrow = _iota((cap, seq), 0).astype(F32)
    g2 = mod_ref[0, 5]
    for r in range(reqs):
        slot = slot_ref[r]
        pt = pt_ref[r]
        for e in range(N_EXPERTS):
            w_s[r, e * cap:(e + 1) * cap, :] = jnp.where(slot[e:e + 1, :] == jrow, pt[e:e + 1, :],
                                                         0.0).astype(BF16)
        y = y_ref[:, r * cap:(r + 1) * cap, :].reshape(N_EXPERTS * cap, y_ref.shape[-1])
        ffn = lax.dot_general(w_s[r], y, (((0,), (0,)), ((), ())), preferred_element_type=F32)
        rows = slice(r * seq, (r + 1) * seq)
        o_ref[rows, :] = _ln(ALPHA * x1_ref[rows, :].astype(F32) + g2 * ffn) * l2g[...] + l2b[...]


def _final_call(x1, ys, slot, pt, mod4, mod_row, l2g, l2b, bsz, seq, reqs):
    cap = EC_FACTOR * seq // N_EXPERTS
    d = D_MODEL
    per_req = pl.BlockSpec((reqs, N_EXPERTS, seq), lambda b: (b, 0, 0))
    return pl.pallas_call(
        functools.partial(_final_kernel, seq, cap, reqs),
        grid=(bsz // reqs,),
        in_specs=[pl.BlockSpec((reqs * seq, d), lambda b: (b, 0)),
                  pl.BlockSpec((N_EXPERTS, reqs * cap, d), lambda b: (0, b, 0)),
                  per_req, per_req,
                  _mod_spec(lambda b: mod_row(b * reqs * seq)),
                  pl.BlockSpec((1, d), lambda b: (0, 0)),
                  pl.BlockSpec((1, d), lambda b: (0, 0))],
        out_specs=pl.BlockSpec((reqs * seq, d), lambda b: (b, 0)),
        out_shape=jax.ShapeDtypeStruct((bsz * seq, d), F32),
        scratch_shapes=[pltpu.VMEM((reqs, N_EXPERTS * cap, seq), BF16)],
        compiler_params=_cparams(1),
        name="final",
    )(x1, ys, slot, pt, mod4, l2g, l2b)


def _grid_pos_embed(rows, dim):
    r = np.repeat(np.arange(rows), GRID_W)
    col = np.tile(np.arange(GRID_W), rows)
    quarter = dim // 4
    omega = 1.0 / (10000.0 ** (np.arange(quarter) / quarter))

    def enc(p):
        ang = p[:, None] * omega[None, :]
        return np.concatenate([np.sin(ang), np.cos(ang)], -1)

    return jnp.asarray(np.concatenate([enc(r), enc(col)], -1), dtype=F32)


def kernel(x_prompt, x_sample, state_delta_fwd, state_delta_bwd, c, c_ctx, w_mod, b_mod, w_in, conv_a_w, a_log, dt_bias, gnorm_w, conv_b_w, conv_b_b, hy_w1, hy_b1, hy_w2, hy_b2, hy_w3, hy_freq, hy_bias, w_up_a, w_up_b, w_out, ln1_g, ln1_b, w_router, w_e_gate, w_e_up, w_e_down, ln2_g, ln2_b):
    d = D_MODEL
    n_ctx, l_ctx, _ = x_prompt.shape
    n_dec, l_dec, _ = x_sample.shape
    lyr = 0

    cond_t = jnp.pad(jnp.concatenate([c_ctx[None, :], c], 0).T, ((0, 0), (0, 7 - n_dec)))
    mod4 = _mod_call(cond_t, 1 + n_dec, w_mod[lyr], b_mod[lyr]).reshape(8, N_MOD, 1, d)

    in_w = w_in[lyr].astype(BF16)

    prm = jnp.pad(jnp.stack([a_log[lyr].reshape(-1), dt_bias[lyr].reshape(-1)]),
                  ((0, 6), (2 * DN_HEADS, LANES - 4 * DN_HEADS)))
    gn = gnorm_w[lyr].reshape(1, -1)

    wua = w_up_a[lyr].astype(BF16)
    wub = w_up_b[lyr].astype(BF16)
    wout_half = (0.5 * w_out[lyr]).astype(BF16)
    wr = jnp.pad(w_router[lyr], ((0, 0), (0, LANES - N_EXPERTS)))
    wr_hi = wr.astype(BF16)
    wr_lo = (wr - wr_hi.astype(F32)).astype(BF16)
    l1g, l1b = ln1_g[lyr].reshape(1, -1), ln1_b[lyr].reshape(1, -1)
    l2g, l2b = ln2_g[lyr].reshape(1, -1), ln2_b[lyr].reshape(1, -1)

    def front(x2d, pos, bsz, seq, mod_row, s0f, s0b):
        (qkv, z, ba, hy, gates), pre_conv = _inproj_call(x2d, pos, mod4, mod_row, in_w, conv_a_w[lyr],
                                                          conv_b_w[lyr], conv_b_b[lyr], seq, ACT_DT)
        reqs = DN_REQS_PER_STEP if seq == DN_CHUNK and bsz % DN_REQS_PER_STEP == 0 else 1
        o_a, s_f, s_b = _deltanet_call(qkv, z, ba, conv_a_w[lyr], prm, gn, s0f, s0b, bsz, seq, reqs,
                                       pre_conv, ACT_DT)
        fwd, inv = (jnp.asarray(m) for m in _dft_mats(seq))
        fwd_hi = fwd.astype(BF16)
        fwd_lo = (fwd - fwd_hi.astype(F32)).astype(BF16)
        filt_params = (hy_w1[lyr], hy_b1[lyr], hy_w2[lyr], hy_b2[lyr], hy_w3[lyr], hy_freq[lyr])
        o_h = _hyena_call(hy, pre_conv, conv_b_w[lyr], conv_b_b[lyr], filt_params, hy_bias[lyr],
                          fwd_hi, fwd_lo, inv.astype(BF16), bsz, seq, ACT_DT)
        x1, u2, probs = _mix_call(x2d, pos, o_a, o_h, gates, mod4, mod_row, wua, wub, wout_half, l1g, l1b,
                                   wr_hi, wr_lo)
        xs, slot, pt = _route_call(probs, u2, bsz, seq)
        return x1, xs, slot, pt, s_f, s_b

    row_ctx = lambda tok: 0
    row_dec = lambda tok: 1 + tok // l_dec

    xc = x_prompt.reshape(n_ctx * l_ctx, d)
    xd = x_sample.reshape(n_dec * l_dec, d)
    pos = _grid_pos_embed(l_dec // GRID_W, d)

    x1c, xsc, gc, ptc, s_f, s_b = front(xc, None, n_ctx, l_ctx, row_ctx, None, None)
    x1d, xsd, gd, ptd, _, _ = front(xd, pos, n_dec, l_dec, row_dec,
                                    state_delta_fwd[:, lyr], state_delta_bwd[:, lyr])
    yc, yd = _expert_call(xsc, xsd, w_e_gate[lyr], w_e_up[lyr], w_e_down[lyr])
    reqs_c = FINAL_CTX_REQS_PER_STEP if n_ctx % FINAL_CTX_REQS_PER_STEP == 0 else 1
    y_prompt = _final_call(x1c, yc, gc, ptc, mod4, row_ctx, l2g, l2b, n_ctx, l_ctx, reqs_c)
    y_sample = _final_call(x1d, yd, gd, ptd, mod4, row_dec, l2g, l2b, n_dec, l_dec, 1)

    return (y_prompt.reshape(n_ctx, l_ctx, d), y_sample.reshape(n_dec, l_dec, d),
            s_f[:, None], s_b[:, None])
```

```python
import functools
import math

import jax
import jax.numpy as jnp
import numpy as np
from jax import lax
from jax.experimental import pallas as pl
from jax.experimental.pallas import tpu as pltpu

F32 = jnp.float32
BF16 = jnp.bfloat16
HIGHEST = lax.Precision.HIGHEST

D_MODEL = 1024
DEPTH = 1
GRID_W = 64
DN_HEADS = 4
DN_DK = 128
DN_DV = 128
DN_WIDTH = DN_HEADS * DN_DV
HY_WIDTH = D_MODEL // 2
HY_EMB = 33
HY_FFN = 64
HY_TARGET = 1e-2
HY_FAST_DECAY = 0.3
HY_SLOW_DECAY = 1.5
N_EXPERTS = 16
EC_FACTOR = 2
N_MOD = 6
ALPHA = (2 * DEPTH) ** 0.25
LN_EPS = 1e-5
RMS_EPS = 1e-6
QKV_WIDTH = 2 * DN_HEADS * DN_DK + DN_WIDTH

LANES = 128
INPROJ_TILE = 1024
INPROJ_BLOCK_ROWS = 512
MIX_TILE = 1024
MIX_SUB_ROWS = 256
MIX_MIN_STEPS = 4
MOD_COL_BLOCK = 2048
DN_CHUNK = 256
INV_BASE = 8
DN_REQS_PER_STEP = 2
HY_ROWS_PER_STEP = 1024
FINAL_CTX_REQS_PER_STEP = 4
EXPERT_ROW_BLOCK = 512
VMEM_LIMIT = 56 * 1024 * 1024
MIN_NORMAL_F32_BITS = 0x00800000
ACT_DT = BF16
ROUTE_ROWS_PER_STEP = 1024
RANK_COUNT_MAX_SEQ = 256


def _cparams(n_axes):
    return pltpu.CompilerParams(dimension_semantics=("arbitrary",) * n_axes,
                                vmem_limit_bytes=VMEM_LIMIT)


def _bmm(a, b):
    return jnp.einsum("bmk,bkn->bmn", a.astype(BF16), b.astype(BF16), preferred_element_type=F32)


def _bmm_nt(a, b):
    return jnp.einsum("bmk,bnk->bmn", a.astype(BF16), b.astype(BF16), preferred_element_type=F32)


def _bmm_tn(a, b):
    return jnp.einsum("bkm,bkn->bmn", a.astype(BF16), b.astype(BF16), preferred_element_type=F32)


def _mm_f32(a, b):
    return jnp.dot(a, b, precision=HIGHEST, preferred_element_type=F32)


def _sigmoid(x):
    return 0.5 * jnp.tanh(0.5 * x) + 0.5


def _silu(x):
    h = 0.5 * x
    return h * (jnp.tanh(h) + 1.0)


def _softplus(x):
    return jnp.maximum(x, 0.0) + jnp.log1p(jnp.exp(-jnp.abs(x)))


def _ln(x):
    mu = jnp.mean(x, axis=-1, keepdims=True)
    xc = x - mu
    var = jnp.mean(xc * xc, axis=-1, keepdims=True)
    return xc * lax.rsqrt(var + LN_EPS)


def _l2norm(x, scale=1.0):
    return x * (lax.rsqrt(jnp.sum(x * x, axis=-1, keepdims=True) + 1e-6) * scale)


def _iota(shape, dim):
    return lax.broadcasted_iota(jnp.int32, shape, dim)


def _short_conv3(x, w, first, last):
    xm = jnp.where(first, 0.0, pltpu.roll(x, 1, 0))
    xp = jnp.where(last, 0.0, pltpu.roll(x, x.shape[0] - 1, 0))
    return w[0:1, :] * xm + w[1:2, :] * x + w[2:3, :] * xp


def _mod_kernel(n_rows, ct_ref, w_ref, b_ref, o_ref):
    s = _silu(ct_ref[...])
    w = w_ref[...]
    b = b_ref[...]
    o_ref[...] = jnp.zeros(o_ref.shape, F32)
    for r in range(n_rows):
        o_ref[r:r + 1, :] = jnp.sum(w * s[:, r:r + 1], axis=0, keepdims=True) + b


def _mod_call(cond_t, n_rows, w_mod, b_mod):
    d = D_MODEL
    tn = MOD_COL_BLOCK
    return pl.pallas_call(
        functools.partial(_mod_kernel, n_rows),
        grid=(N_MOD * d // tn,),
        in_specs=[pl.BlockSpec((d, 8), lambda j: (0, 0)),
                  pl.BlockSpec((d, tn), lambda j: (0, j)),
                  pl.BlockSpec((1, tn), lambda j: (0, j))],
        out_specs=pl.BlockSpec((8, tn), lambda j: (0, j)),
        out_shape=jax.ShapeDtypeStruct((8, N_MOD * d), F32),
        compiler_params=_cparams(1),
        name="mod",
    )(cond_t, w_mod, b_mod.reshape(1, -1))


def _mod_spec(row_of_step):
    return pl.BlockSpec((1, N_MOD, 1, D_MODEL), lambda *idx: (row_of_step(*idx), 0, 0, 0))


def _inproj_kernel(has_pos, conv_seq, *refs):
    refs = list(refs)
    x_ref = refs.pop(0)
    pos_ref = refs.pop(0) if has_pos else None
    mod_ref, w_ref = refs.pop(0), refs.pop(0)
    if conv_seq:
        cwa_ref, cwb_ref, cbb_ref = refs.pop(0), refs.pop(0), refs.pop(0)
    qkv_o, z_o, ba_o, hy_o, g_o = refs
    sh1 = mod_ref[0, 0]
    sc1 = mod_ref[0, 1]
    o_z = QKV_WIDTH
    o_ba = o_z + DN_WIDTH
    n_ba = 4 * DN_HEADS
    o_g = n_ba + 3 * HY_WIDTH
    lanes = lambda j: slice(j * LANES, (j + 1) * LANES)
    tm = x_ref.shape[0]
    sub = min(tm, INPROJ_BLOCK_ROWS)
    blocks = [slice(r0, r0 + sub) for r0 in range(0, tm, sub)]

    if conv_seq:
        row = _iota((sub, LANES), 0) % conv_seq
        first = row == 0
        last = row == conv_seq - 1

    def project(rows):
        x = x_ref[rows, :]
        if has_pos:
            x = x + pos_ref[rows, :]
        u = (_ln(x) * (1.0 + sc1) + sh1).astype(BF16)
        return (jnp.dot(u, w_ref[:, :o_ba], preferred_element_type=F32),
                jnp.dot(u, w_ref[:, o_ba:], preferred_element_type=F32))

    def epilogue(rows, head, tail):
        z_o[rows, :] = head[:, o_z:].astype(z_o.dtype)
        ba_o[rows, :] = tail[:, :LANES]
        g_o[rows, :] = tail[:, o_g:].astype(g_o.dtype)
        hy = tail[:, n_ba:o_g]
        if not conv_seq:
            qkv_o[rows, :] = head[:, :o_z].astype(qkv_o.dtype)
            hy_o[rows, :] = hy.astype(hy_o.dtype)
            return
        for j in range(QKV_WIDTH // LANES):
            y = _silu(_short_conv3(head[:, lanes(j)], cwa_ref[:, lanes(j)], first, last))
            if j < 2 * DN_HEADS:
                y = _l2norm(y, DN_DK ** -0.5 if j < DN_HEADS else 1.0)
            qkv_o[rows, lanes(j)] = y.astype(qkv_o.dtype)
        nj = HY_WIDTH // LANES
        for j in range(nj):
            parts = [_short_conv3(hy[:, lanes(p * nj + j)], cwb_ref[:, lanes(p * nj + j)], first, last)
                     + cbb_ref[:, lanes(p * nj + j)] for p in range(3)]
            hy_o[rows, lanes(j)] = parts[0].astype(hy_o.dtype)
            hy_o[rows, lanes(nj + j)] = (parts[1] * parts[2]).astype(hy_o.dtype)

    pending = None
    for rows in blocks:
        cur = (rows, *project(rows))
        if pending is not None:
            epilogue(*pending)
        pending = cur
    epilogue(*pending)


def _inproj_call(x2d, pos, mod4, mod_row, w, conv_a_w, conv_b_w, conv_b_b, seq, act_dt):
    n, d = x2d.shape
    fuse = seq <= INPROJ_BLOCK_ROWS and INPROJ_TILE % seq == 0
    tm = min(INPROJ_TILE if fuse else INPROJ_BLOCK_ROWS, n)
    assert n % tm == 0 and (not fuse or tm % seq == 0)
    conv_seq = seq if fuse else None
    row = lambda i: (i, 0)
    const = lambda i: (0, 0)
    in_specs = [pl.BlockSpec((tm, d), row)]
    args = [x2d]
    if pos is not None:
        tiles = pos.shape[0] // tm
        in_specs.append(pl.BlockSpec((tm, d), lambda i: (i % tiles, 0)))
        args.append(pos)
    in_specs.append(_mod_spec(lambda i: mod_row(i * tm)))
    args.append(mod4)
    in_specs.append(pl.BlockSpec(w.shape, const, pipeline_mode=pl.Buffered(1)))
    args.append(w)
    if conv_seq:
        for a in (conv_a_w, conv_b_w, conv_b_b.reshape(1, -1)):
            in_specs.append(pl.BlockSpec(a.shape, const))
            args.append(a)
    hy_width = (2 if conv_seq else 3) * HY_WIDTH
    widths = (QKV_WIDTH, DN_WIDTH, LANES, hy_width, 2 * D_MODEL)
    dts = (act_dt, act_dt, F32, act_dt, act_dt)
    outs = pl.pallas_call(
        functools.partial(_inproj_kernel, pos is not None, conv_seq),
        grid=(n // tm,),
        in_specs=in_specs,
        out_specs=[pl.BlockSpec((tm, w), row) for w in widths],
        out_shape=[jax.ShapeDtypeStruct((n, w), dt) for w, dt in zip(widths, dts)],
        compiler_params=_cparams(1),
        name="inproj",
    )(*args)
    return outs, conv_seq is not None


def _inv_unit_tri(lm, ri, ci, block):
    c = lm.shape[-1]

    def same_block(h):
        sh = h.bit_length() - 1
        return (ri >> sh) == (ci >> sh)

    def unfold(f, h):
        return jnp.where(same_block(h), jnp.concatenate([f] * (c // h), axis=1), 0.0)

    base = INV_BASE
    n_nat = jnp.where(same_block(base), -lm, 0.0)
    n_fold = n_nat[:, 0:base, :]
    for i in range(1, c // base):
        n_fold = n_fold + n_nat[:, i * base:(i + 1) * base, :]
    eye_fold = jnp.where((_iota((base, c), 1) & (base - 1)) == _iota((base, c), 0), 1.0, 0.0)
    t = eye_fold + n_fold
    p_fold, p_nat = n_fold, n_nat
    span = 2
    while span < base:
        p_fold = _bmm(p_fold, p_nat)
        p_nat = unfold(p_fold, base)
        t = t + _bmm(t, p_nat)
        span *= 2
    s = base
    while s < block:
        sh = s.bit_length() - 1
        t_nat = unfold(t, s)
        even = ((_iota((s, c), 1) >> sh) & 1) == 0
        t2 = jnp.concatenate([jnp.where(even, t, 0.0), jnp.where(even, 0.0, t)], axis=1)
        lo = jnp.where(same_block(2 * s) & ~same_block(s), lm, 0.0)
        t = t2 - _bmm(_bmm(t2, lo), t_nat)
        s *= 2
    return t


def _dn_chunk(q, k, v, gc_col, gc_row, beta, e_col, e_rest, e_all, s_prev, same_chunk, ri, ci):
    c = DN_CHUNK
    nh = q.shape[0] // 2
    half = c // 2
    if same_chunk:
        gram = _bmm_nt(k[:nh], k[:nh])
        qk = _bmm_nt(q[:nh], k[:nh])
        gram = jnp.concatenate([gram, gram], 0)
        qk = jnp.concatenate([qk, qk], 0)
    else:
        gram = _bmm_nt(k, k)
        qk = _bmm_nt(q, k)
    diff = gc_col - gc_row
    incl = jnp.concatenate([jnp.broadcast_to(ri >= ci, (nh, c, c)), jnp.broadcast_to(ri <= ci, (nh, c, c))], 0)
    strict = incl & (ri != ci)
    decay = jnp.exp(jnp.where(incl, diff, -jnp.inf))
    lm = jnp.where(strict, beta * gram * decay, 0.0)
    a_intra = qk * decay
    t = _inv_unit_tri(lm, ri, ci, half)
    t1 = t[:, :, :half]
    t2 = t[:, :, half:]
    rhs = jnp.concatenate([v * beta, k * (beta * e_col)], axis=2)
    y1 = _bmm(t1, rhs[:, :half, :])
    y2 = _bmm(t2, rhs[:, half:, :])
    c_f = _bmm(lm[:nh, half:, :half], y1[:nh])
    c_b = _bmm(lm[nh:, :half, half:], y2[nh:])
    c_f = _bmm(t2[:nh], c_f)
    c_b = _bmm(t1[nh:], c_b)
    sol = jnp.concatenate([jnp.concatenate([y1[:nh], y2[:nh] - c_f], 1),
                           jnp.concatenate([y1[nh:] - c_b, y2[nh:]], 1)], 0)
    u = sol[:, :, :DN_DV]
    w = sol[:, :, DN_DV:]
    ks = k * e_rest
    if s_prev is None:
        v_new = u
        o = _bmm(a_intra, v_new)
        s_new = _bmm_tn(ks, v_new)
    else:
        v_new = u - _bmm(w, s_prev)
        o = _bmm(q * e_col, s_prev) + _bmm(a_intra, v_new)
        s_new = s_prev * e_all + _bmm_tn(ks, v_new)
    return o, s_new


def _deltanet_kernel(seq, reqs, zero_init, pre_conv, *refs):
    qkv_ref, cw_ref, z_ref, ba_ref, prm_ref, gn_ref = refs[:6]
    refs = refs[6:]
    if not zero_init:
        s0f_ref, s0b_ref = refs[:2]
        refs = refs[2:]
    o_ref, sf_ref, sb_ref, q_s, k_s, v_s, g_s, b_s, o_s, st_s = refs
    c = DN_CHUNK
    n = seq // c
    nh = DN_HEADS
    nf = reqs * nh
    lanes = lambda j: slice(j * LANES, (j + 1) * LANES)

    if not pre_conv:
        row = _iota((seq, LANES), 0)
        first_row = row == 0
        last_row = row == seq - 1
    for r in range(reqs):
        rows = slice(r * seq, (r + 1) * seq)

        def qkv_part(j):
            x = qkv_ref[rows, lanes(j)].astype(F32)
            return x if pre_conv else _silu(_short_conv3(x, cw_ref[:, lanes(j)], first_row, last_row))

        for h in range(nh):
            q = qkv_part(h)
            k = qkv_part(nh + h)
            q_s[r * nh + h] = q if pre_conv else _l2norm(q, DN_DK ** -0.5)
            k_s[r * nh + h] = k if pre_conv else _l2norm(k)
            v_s[r * nh + h] = qkv_part(2 * nh + h)

    ba = ba_ref[...]
    b_s[...] = _sigmoid(ba)
    g_s[...] = -jnp.exp(prm_ref[0:1, :]) * _softplus(ba + prm_ref[1:2, :])

    ri = _iota((c, c), 0)
    ci = _iota((c, c), 1)
    tri_l = jnp.where(ri >= ci, 1.0, 0.0).astype(F32)
    tri_u = jnp.where(ri <= ci, 1.0, 0.0).astype(F32)

    def chunk_pair(idx_f, idx_b, s_prev):
        def ds(idx, base=0):
            start = idx * c
            start = start if isinstance(start, int) else pl.multiple_of(start, c)
            return pl.ds(base + start, c)

        col_f, col_b, row_f, row_b, beta_f, beta_b = [], [], [], [], [], []
        for r in range(reqs):
            cs_f = _mm_f32(tri_l, g_s[ds(idx_f, r * seq), :])
            cs_b = _mm_f32(tri_u, g_s[ds(idx_b, r * seq), :])
            cst_f = cs_f.T
            cst_b = cs_b.T
            b_f = b_s[ds(idx_f, r * seq), :]
            b_b = b_s[ds(idx_b, r * seq), :]
            for h in range(nh):
                jf, jb = 2 * nh + h, 3 * nh + h
                col_f.append(cs_f[:, jf:jf + 1])
                col_b.append(cs_b[:, jb:jb + 1])
                row_f.append(cst_f[jf:jf + 1, :])
                row_b.append(cst_b[jb:jb + 1, :])
                beta_f.append(b_f[:, h:h + 1])
                beta_b.append(b_b[:, nh + h:nh + h + 1])
        gc_col = jnp.stack(col_f + col_b, 0)
        gc_row = jnp.stack(row_f + row_b, 0)
        beta = jnp.stack(beta_f + beta_b, 0)
        e_col = jnp.exp(gc_col)
        gl = jnp.concatenate([gc_col[:nf, c - 1:c, :], gc_col[nf:, 0:1, :]], 0)
        e_rest = jnp.exp(gl - gc_col)
        e_all = jnp.exp(gl)
        sl_f, sl_b = ds(idx_f), ds(idx_b)
        both = lambda ref: jnp.concatenate([ref[:, sl_f, :], ref[:, sl_b, :]], 0)
        o, s_new = _dn_chunk(both(q_s), both(k_s), both(v_s), gc_col, gc_row, beta, e_col, e_rest, e_all,
                             s_prev, n == 1, ri, ci)
        o_s[:nf, sl_f, :] = o[:nf]
        o_s[nf:, sl_b, :] = o[nf:]
        return s_new

    if not zero_init:
        s0 = jnp.concatenate([s0f_ref[r] for r in range(reqs)] + [s0b_ref[r] for r in range(reqs)], 0)
    if n == 1:
        s_fin = chunk_pair(0, 0, None if zero_init else s0)
    else:
        st_s[...] = jnp.zeros(st_s.shape, F32) if zero_init else s0

        def body(i, carry):
            st_s[...] = chunk_pair(i, n - 1 - i, st_s[...])
            return carry

        lax.fori_loop(0, n, body, 0)
        s_fin = st_s[...]
    for r in range(reqs):
        sf_ref[r] = s_fin[r * nh:(r + 1) * nh]
        sb_ref[r] = s_fin[nf + r * nh:nf + (r + 1) * nh]

    gn = gn_ref[...]
    for r in range(reqs):
        rows = slice(r * seq, (r + 1) * seq)
        for h in range(nh):
            o = o_s[r * nh + h] + o_s[nf + r * nh + h]
            o = o * lax.rsqrt(jnp.mean(o * o, axis=-1, keepdims=True) + RMS_EPS) * gn
            o_ref[rows, lanes(h)] = (o * _silu(z_ref[rows, lanes(h)].astype(F32))).astype(o_ref.dtype)


def _deltanet_call(qkv, z, ba, conv_w, prm, gn, s0f, s0b, bsz, seq, reqs, pre_conv, act_dt):
    nh = DN_HEADS
    zero_init = s0f is None
    rows = reqs * seq
    st = pl.BlockSpec((reqs, nh, DN_DK, DN_DV), lambda b: (b, 0, 0, 0))
    in_specs = [pl.BlockSpec((rows, QKV_WIDTH), lambda b: (b, 0)),
                pl.BlockSpec((3, QKV_WIDTH), lambda b: (0, 0)),
                pl.BlockSpec((rows, DN_WIDTH), lambda b: (b, 0)),
                pl.BlockSpec((rows, LANES), lambda b: (b, 0)),
                pl.BlockSpec((8, LANES), lambda b: (0, 0)),
                pl.BlockSpec((1, LANES), lambda b: (0, 0))]
    args = [qkv, conv_w, z, ba, prm, gn]
    if not zero_init:
        in_specs += [st, st]
        args += [s0f, s0b]
    vm = lambda shape: pltpu.VMEM(shape, F32)
    per_head = vm((reqs * nh, seq, LANES))
    return pl.pallas_call(
        functools.partial(_deltanet_kernel, seq, reqs, zero_init, pre_conv),
        grid=(bsz // reqs,),
        in_specs=in_specs,
        out_specs=[pl.BlockSpec((rows, DN_WIDTH), lambda b: (b, 0)), st, st],
        out_shape=[jax.ShapeDtypeStruct((bsz * seq, DN_WIDTH), act_dt),
                   jax.ShapeDtypeStruct((bsz, nh, DN_DK, DN_DV), F32),
                   jax.ShapeDtypeStruct((bsz, nh, DN_DK, DN_DV), F32)],
        scratch_shapes=[per_head, per_head, per_head, vm((rows, LANES)), vm((rows, LANES)),
                        vm((2 * reqs * nh, seq, LANES)), vm((2 * reqs * nh, DN_DK, DN_DV))],
        compiler_params=_cparams(1),
        name="deltanet",
    )(*args)


def _dft_mats(seq):
    n = 2 * seq
    f = np.arange(seq)[:, None]
    s = np.arange(seq)[None, :]
    ang = 2.0 * np.pi * ((f * s) % n) / n
    fwd = np.concatenate([np.cos(ang), -np.sin(ang)], axis=0)
    fwd[seq, :] = np.cos(np.pi * np.arange(seq))
    t = (np.arange(seq) + seq // 2)[:, None]
    ff = np.arange(seq)[None, :]
    ang2 = 2.0 * np.pi * ((t * ff) % n) / n
    inv_r = 2.0 * np.cos(ang2) / n
    inv_i = -2.0 * np.sin(ang2) / n
    inv_r[:, 0] = 1.0 / n
    inv_i[:, 0] = np.cos(np.pi * t[:, 0]) / n
    inv = np.concatenate([inv_r, inv_i], axis=1)
    return fwd.astype(np.float32), inv.astype(np.float32)


def _filter_feats(seq):
    t = np.linspace(0.0, 1.0, seq)[:, None]
    bands = (HY_EMB - 1) // 2
    ang = (2.0 * math.pi * np.arange(seq) / seq)[:, None] * np.linspace(1e-4, bands - 1, bands)[None, :]
    feats = np.concatenate([t, np.cos(ang), -np.sin(ang)], -1)
    deltas = np.abs(np.linspace(math.log(HY_TARGET) / HY_SLOW_DECAY, math.log(HY_TARGET) / HY_FAST_DECAY,
                                HY_WIDTH))
    offset = np.abs(np.arange(seq) - seq // 2) / (seq // 2)
    window = np.exp(-offset[:, None] * deltas[None, :])
    feats = np.pad(feats, ((0, 0), (0, LANES - HY_EMB)))
    return jnp.asarray(feats, dtype=F32), jnp.asarray(window, dtype=F32)


def _filter_spectrum(seq, feats_ref, win_ref, w1_ref, b1_ref, w2_ref, b2_ref, w3_ref, fr_ref, fwd_hi_ref,
                     fwd_lo_ref, ha_s, hb_s, hd_s):
    fr = fr_ref[...]
    hid = jnp.sin(fr * (_mm_f32(feats_ref[...], w1_ref[...]) + b1_ref[...]))
    hid = jnp.sin(fr * (_mm_f32(hid, w2_ref[...]) + b2_ref[...]))
    filt = _mm_f32(hid, w3_ref[...]) * win_ref[...]
    filt = filt / (jnp.sum(jnp.abs(filt), axis=0, keepdims=True) + 1e-6)
    filt_hi = filt.astype(BF16)
    filt_lo = (filt - filt_hi.astype(F32)).astype(BF16)
    fwd_hi = fwd_hi_ref[...]
    spec = (jnp.dot(fwd_hi, filt_hi, preferred_element_type=F32)
            + jnp.dot(fwd_hi, filt_lo, preferred_element_type=F32)
            + jnp.dot(fwd_lo_ref[...], filt_hi, preferred_element_type=F32))
    h_re = spec[:seq, :]
    h_im = spec[seq:, :]
    first = _iota(h_re.shape, 0) == 0
    ha_s[...] = h_re
    hb_s[...] = jnp.where(first, 0.0, h_im)
    hd_s[...] = jnp.where(first, h_im, h_re)


def _hyena_kernel(seq, reqs, pre_conv, *refs):
    refs = list(refs)
    n_tok = 2 if pre_conv else 9
    tok_refs, refs = refs[:n_tok], refs[n_tok:]
    filt_refs, refs = refs[:8], refs[8:]
    skip_ref, fwd_ref, fwd_lo_ref, inv_ref, o_ref, ha_s, hb_s, hd_s = refs

    @pl.when(pl.program_id(0) == 0)
    def _():
        _filter_spectrum(seq, *filt_refs, fwd_ref, fwd_lo_ref, ha_s, hb_s, hd_s)

    if pre_conv:
        x0_ref, uu_ref = tok_refs
    else:
        x0_ref, x1_ref, v_ref, c0_ref, c1_ref, c2_ref, b0_ref, b1_ref, b2_ref = tok_refs
        row = _iota((seq, HY_WIDTH), 0)
        first = row == 0
        last = row == seq - 1
    hb = hb_s[...]
    for r in range(reqs):
        rows = slice(r * seq, (r + 1) * seq)
        if pre_conv:
            x0 = x0_ref[rows, :].astype(F32)
            uu = uu_ref[rows, :].astype(F32)
        else:
            conv = lambda x_ref, w_ref, b_ref: (_short_conv3(x_ref[rows, :].astype(F32), w_ref[...], first, last)
                                                + b_ref[...])
            x0 = conv(x0_ref, c0_ref, b0_ref)
            uu = conv(x1_ref, c1_ref, b1_ref) * conv(v_ref, c2_ref, b2_ref)
        spec = jnp.dot(fwd_ref[...], uu.astype(BF16), preferred_element_type=F32)
        u_re = spec[:seq, :]
        u_im = spec[seq:, :]
        y_re = u_re * ha_s[...] - u_im * hb
        y_im = u_re * hb + u_im * hd_s[...]
        y = jnp.concatenate([y_re, y_im], axis=0).astype(BF16)
        cv = jnp.dot(inv_ref[...], y, preferred_element_type=F32)
        o_ref[rows, :] = (x0 * (cv + uu * skip_ref[...])).astype(o_ref.dtype)


def _hyena_call(hy, pre_conv, conv_w, conv_b, filt_params, skip, fwd_hi, fwd_lo, inv, bsz, seq, act_dt):
    w1, b1, w2, b2, w3, freq = filt_params
    cb = HY_WIDTH
    reqs = max(1, min(bsz, HY_ROWS_PER_STEP // seq))
    assert bsz % reqs == 0
    tok = lambda part: pl.BlockSpec((reqs * seq, cb), lambda b: (b, part))
    const = lambda a: pl.BlockSpec(a.shape, lambda b: (0,) * a.ndim, pipeline_mode=pl.Buffered(1))
    feats, window = _filter_feats(seq)
    w1p = jnp.pad(w1, ((0, LANES - HY_EMB), (0, 0)))
    filt_args = [feats, window, w1p, b1.reshape(1, -1), w2, b2.reshape(1, -1), w3, freq.reshape(1, -1)]
    tail_args = filt_args + [skip.reshape(1, -1), fwd_hi, fwd_lo, inv]
    tail_specs = [const(a) for a in tail_args]
    if pre_conv:
        in_specs = [tok(0), tok(1)] + tail_specs
        args = [hy, hy] + tail_args
    else:
        conv_b2 = conv_b.reshape(1, -1)
        part3 = lambda a, part: pl.BlockSpec((a.shape[0], cb), lambda b: (0, part))
        in_specs = ([tok(0), tok(1), tok(2)] + [part3(conv_w, p) for p in range(3)]
                    + [part3(conv_b2, p) for p in range(3)] + tail_specs)
        args = [hy, hy, hy, conv_w, conv_w, conv_w, conv_b2, conv_b2, conv_b2] + tail_args
    spec_buf = pltpu.VMEM((seq, HY_WIDTH), F32)
    return pl.pallas_call(
        functools.partial(_hyena_kernel, seq, reqs, pre_conv),
        grid=(bsz // reqs,),
        in_specs=in_specs,
        out_specs=tok(0),
        out_shape=jax.ShapeDtypeStruct((bsz * seq, HY_WIDTH), act_dt),
        scratch_shapes=[spec_buf, spec_buf, spec_buf],
        compiler_params=_cparams(1),
        name="hyena",
    )(*args)


def _mix_kernel(has_pos, *refs):
    refs = list(refs)
    x_ref = refs.pop(0)
    pos_ref = refs.pop(0) if has_pos else None
    (oa_ref, ob_ref, g_ref, mod_ref, wua, wub, wout_half, l1g, l1b, wr_hi, wr_lo, x1_o, u2_o, p_o) = refs
    d = D_MODEL
    tm = x_ref.shape[0]
    sub = min(tm, MIX_SUB_ROWS)
    blocks = [slice(r0, r0 + sub) for r0 in range(0, tm, sub)]
    g1 = mod_ref[0, 2]
    sh2 = mod_ref[0, 3]
    sc2 = mod_ref[0, 4]

    mixed = []
    for rows in blocks:
        th = jnp.tanh(0.5 * g_ref[rows, :].astype(F32))
        a = jnp.dot(oa_ref[rows, :], wua[...], preferred_element_type=F32)
        b = jnp.dot(ob_ref[rows, :], wub[...], preferred_element_type=F32)
        up2 = (th[:, :d] * a + a) + (th[:, d:] * b + b)
        mixed.append(jnp.dot(up2.astype(BF16), wout_half[...], preferred_element_type=F32))

    for rows, mx in zip(blocks, mixed):
        x = x_ref[rows, :]
        if has_pos:
            x = x + pos_ref[rows, :]
        x1 = _ln(ALPHA * x + g1 * mx) * l1g[...] + l1b[...]
        x1_o[rows, :] = x1.astype(x1_o.dtype)
        u2 = _ln(x1) * (1.0 + sc2) + sh2
        u2_hi = u2.astype(BF16)
        u2_o[rows, :] = u2_hi
        u2_lo = (u2 - u2_hi.astype(F32)).astype(BF16)
        logits = (jnp.dot(u2_hi, wr_hi[...], preferred_element_type=F32)
                  + jnp.dot(u2_hi, wr_lo[...], preferred_element_type=F32)
                  + jnp.dot(u2_lo, wr_hi[...], preferred_element_type=F32))
        lane = _iota(logits.shape, 1)
        logits = jnp.where(lane < N_EXPERTS, logits, -jnp.inf)
        m = jnp.max(logits, axis=-1, keepdims=True)
        e = jnp.exp(logits - m)
        p_o[rows, :] = e / jnp.sum(e, axis=-1, keepdims=True)


def _mix_call(x2d, pos, o_a, o_b, gates, mod4, mod_row, wua, wub, wout_half, l1g, l1b, wr_hi, wr_lo):
    n, d = x2d.shape
    tm = min(MIX_TILE, max(n // MIX_MIN_STEPS, MIX_SUB_ROWS))
    assert n % tm == 0
    row = lambda i: (i, 0)
    const = lambda i: (0, 0)
    in_specs = [pl.BlockSpec((tm, d), row)]
    args = [x2d]
    if pos is not None:
        tiles = pos.shape[0] // tm
        in_specs.append(pl.BlockSpec((tm, d), lambda i: (i % tiles, 0)))
        args.append(pos)
    in_specs += [pl.BlockSpec((tm, DN_WIDTH), row), pl.BlockSpec((tm, HY_WIDTH), row),
                 pl.BlockSpec((tm, 2 * d), row),
                 _mod_spec(lambda i: mod_row(i * tm))]
    args += [o_a, o_b, gates, mod4]
    for w in (wua, wub, wout_half, l1g, l1b, wr_hi, wr_lo):
        in_specs.append(pl.BlockSpec(w.shape, const, pipeline_mode=pl.Buffered(1)))
        args.append(w)
    return pl.pallas_call(
        functools.partial(_mix_kernel, pos is not None),
        grid=(n // tm,),
        in_specs=in_specs,
        out_specs=[pl.BlockSpec((tm, d), row), pl.BlockSpec((tm, d), row), pl.BlockSpec((tm, LANES), row)],
        out_shape=[jax.ShapeDtypeStruct((n, d), ACT_DT), jax.ShapeDtypeStruct((n, d), BF16),
                   jax.ShapeDtypeStruct((n, LANES), F32)],
        compiler_params=_cparams(1),
        name="mix",
    )(*args)


def _route_kernel(seq, cap, reqs, p_ref, u_ref, tri_ref, xs_ref, slot_ref, pt_ref, g_s):
    e_n = N_EXPERTS
    tri = tri_ref[...]
    jrow = _iota((cap, seq), 0).astype(F32)
    if seq <= RANK_COUNT_MAX_SEQ:
        earlier = _iota((seq, seq), 0) < _iota((seq, seq), 1)
    for r in range(reqs):
        rows = slice(r * seq, (r + 1) * seq)
        p = p_ref[rows, :]
        pt = p.T[:e_n, :]

        if seq <= RANK_COUNT_MAX_SEQ:
            ranks = []
            for e in range(e_n):
                pc = p[:, e:e + 1]
                pr = pt[e:e + 1, :]
                beats = (pc > pr) | (earlier & (pc == pr))
                ranks.append(jnp.sum(jnp.where(beats, 1.0, 0.0), axis=0, keepdims=True))
            sel = jnp.concatenate(ranks, axis=0) < cap
        else:
            def ok(cand):
                cnt = jnp.sum(jnp.where(pt >= pltpu.bitcast(cand, F32), 1.0, 0.0), axis=1, keepdims=True)
                return (cnt >= cap) & (cand >= MIN_NORMAL_F32_BITS)

            def search(i, cur):
                lo = 28 - 2 * i
                c1, c2, c3 = cur | (1 << lo), cur | (2 << lo), cur | (3 << lo)
                return jnp.where(ok(c3), c3, jnp.where(ok(c2), c2, jnp.where(ok(c1), c1, cur)))

            top = jnp.full((e_n, 1), 1 << 30, jnp.int32)
            cur = jnp.where(ok(top), top, 0)
            thr = pltpu.bitcast(lax.fori_loop(0, 15, search, cur), F32)
            gt = pt > thr
            eq = pt == thr
            n_gt = jnp.sum(jnp.where(gt, 1.0, 0.0), axis=1, keepdims=True)
            eq_rank = jnp.dot(jnp.where(eq, 1.0, 0.0).astype(BF16), tri, preferred_element_type=F32)
            sel = gt | (eq & (eq_rank < cap - n_gt))
        pos = jnp.dot(jnp.where(sel, 1.0, 0.0).astype(BF16), tri, preferred_element_type=F32)
        slot = jnp.where(sel, pos, -1.0)
        slot_ref[r] = slot
        pt_ref[r] = pt
        for e in range(e_n):
            g_s[r, e * cap:(e + 1) * cap, :] = jnp.where(slot[e:e + 1, :] == jrow, 1.0, 0.0).astype(BF16)
        xs = jnp.dot(g_s[r], u_ref[rows, :], preferred_element_type=F32)
        for e in range(e_n):
            xs_ref[e, r * cap:(r + 1) * cap, :] = xs[e * cap:(e + 1) * cap, :].astype(xs_ref.dtype)


def _route_call(probs, u2, bsz, seq):
    cap = EC_FACTOR * seq // N_EXPERTS
    reqs = max(1, min(bsz, ROUTE_ROWS_PER_STEP // seq))
    assert bsz % reqs == 0
    tri = jnp.asarray(np.triu(np.ones((seq, seq), np.float32), 1), dtype=BF16)
    per_req = pl.BlockSpec((reqs, N_EXPERTS, seq), lambda b: (b, 0, 0))
    return pl.pallas_call(
        functools.partial(_route_kernel, seq, cap, reqs),
        grid=(bsz // reqs,),
        in_specs=[pl.BlockSpec((reqs * seq, LANES), lambda b: (b, 0)),
                  pl.BlockSpec((reqs * seq, D_MODEL), lambda b: (b, 0)),
                  pl.BlockSpec((seq, seq), lambda b: (0, 0), pipeline_mode=pl.Buffered(1))],
        out_specs=[pl.BlockSpec((N_EXPERTS, reqs * cap, D_MODEL), lambda b: (0, b, 0)), per_req, per_req],
        out_shape=[jax.ShapeDtypeStruct((N_EXPERTS, bsz * cap, D_MODEL), BF16),
                   jax.ShapeDtypeStruct((bsz, N_EXPERTS, seq), F32),
                   jax.ShapeDtypeStruct((bsz, N_EXPERTS, seq), F32)],
        scratch_shapes=[pltpu.VMEM((reqs, N_EXPERTS * cap, seq), BF16)],
        compiler_params=_cparams(1),
        name="route",
    )(probs, u2, tri)


def _expert_kernel(xc_ref, xd_ref, wg_ref, wu_ref, wd_ref, yc_ref, yd_ref):
    wg = wg_ref[0].astype(BF16)
    wu = wu_ref[0].astype(BF16)
    wd = wd_ref[0].astype(BF16)
    for x_ref, y_ref in ((xc_ref, yc_ref), (xd_ref, yd_ref)):
        rows = x_ref.shape[1]
        rb = min(rows, EXPERT_ROW_BLOCK)
        for r0 in range(0, rows, rb):
            x = x_ref[0, r0:r0 + rb, :]
            hg = jnp.dot(x, wg, preferred_element_type=F32)
            hu = jnp.dot(x, wu, preferred_element_type=F32)
            y = jnp.dot((_silu(hg) * hu).astype(BF16), wd, preferred_element_type=F32)
            y_ref[0, r0:r0 + rb, :] = y.astype(y_ref.dtype)


def _expert_call(xs_c, xs_d, w_gate, w_up, w_down):
    e_n, rc, d = xs_c.shape
    rd = xs_d.shape[1]
    ff = w_gate.shape[2]
    return pl.pallas_call(
        _expert_kernel,
        grid=(e_n,),
        in_specs=[pl.BlockSpec((1, rc, d), lambda e: (e, 0, 0)),
                  pl.BlockSpec((1, rd, d), lambda e: (e, 0, 0)),
                  pl.BlockSpec((1, d, ff), lambda e: (e, 0, 0)),
                  pl.BlockSpec((1, d, ff), lambda e: (e, 0, 0)),
                  pl.BlockSpec((1, ff, d), lambda e: (e, 0, 0))],
        out_specs=[pl.BlockSpec((1, rc, d), lambda e: (e, 0, 0)),
                   pl.BlockSpec((1, rd, d), lambda e: (e, 0, 0))],
        out_shape=[jax.ShapeDtypeStruct((e_n, rc, d), BF16), jax.ShapeDtypeStruct((e_n, rd, d), BF16)],
        compiler_params=_cparams(1),
        name="experts",
    )(xs_c, xs_d, w_gate, w_up, w_down)


def _final_kernel(seq, cap, reqs, x1_ref, y_ref, slot_ref, pt_ref, mod_ref, l2g, l2b, o_ref, w_s):
    jrow = _iota((cap, seq), 0).astype(F32)
    g2 = mod_ref[0, 5]
    for r in range(reqs):
        slot = slot_ref[r]
        pt = pt_ref[r]
        for e in range(N_EXPERTS):
            w_s[r, e * cap:(e + 1) * cap, :] = jnp.where(slot[e:e + 1, :] == jrow, pt[e:e + 1, :],
                                                         0.0).astype(BF16)
        y = y_ref[:, r * cap:(r + 1) * cap, :].reshape(N_EXPERTS * cap, y_ref.shape[-1])
        ffn = lax.dot_general(w_s[r], y, (((0,), (0,)), ((), ())), preferred_element_type=F32)
        rows = slice(r * seq, (r + 1) * seq)
        o_ref[rows, :] = _ln(ALPHA * x1_ref[rows, :].astype(F32) + g2 * ffn) * l2g[...] + l2b[...]


def _final_call(x1, ys, slot, pt, mod4, mod_row, l2g, l2b, bsz, seq, reqs):
    cap = EC_FACTOR * seq // N_EXPERTS
    d = D_MODEL
    per_req = pl.BlockSpec((reqs, N_EXPERTS, seq), lambda b: (b, 0, 0))
    return pl.pallas_call(
        functools.partial(_final_kernel, seq, cap, reqs),
        grid=(bsz // reqs,),
        in_specs=[pl.BlockSpec((reqs * seq, d), lambda b: (b, 0)),
                  pl.BlockSpec((N_EXPERTS, reqs * cap, d), lambda b: (0, b, 0)),
                  per_req, per_req,
                  _mod_spec(lambda b: mod_row(b * reqs * seq)),
                  pl.BlockSpec((1, d), lambda b: (0, 0)),
                  pl.BlockSpec((1, d), lambda b: (0, 0))],
        out_specs=pl.BlockSpec((reqs * seq, d), lambda b: (b, 0)),
        out_shape=jax.ShapeDtypeStruct((bsz * seq, d), F32),
        scratch_shapes=[pltpu.VMEM((reqs, N_EXPERTS * cap, seq), BF16)],
        compiler_params=_cparams(1),
        name="final",
    )(x1, ys, slot, pt, mod4, l2g, l2b)


def _grid_pos_embed(rows, dim):
    r = np.repeat(np.arange(rows), GRID_W)
    col = np.tile(np.arange(GRID_W), rows)
    quarter = dim // 4
    omega = 1.0 / (10000.0 ** (np.arange(quarter) / quarter))

    def enc(p):
        ang = p[:, None] * omega[None, :]
        return np.concatenate([np.sin(ang), np.cos(ang)], -1)

    return jnp.asarray(np.concatenate([enc(r), enc(col)], -1), dtype=F32)


def kernel(x_prompt, x_sample, state_delta_fwd, state_delta_bwd, c, c_ctx, w_mod, b_mod, w_in, conv_a_w, a_log, dt_bias, gnorm_w, conv_b_w, conv_b_b, hy_w1, hy_b1, hy_w2, hy_b2, hy_w3, hy_freq, hy_bias, w_up_a, w_up_b, w_out, ln1_g, ln1_b, w_router, w_e_gate, w_e_up, w_e_down, ln2_g, ln2_b):
    d = D_MODEL
    n_ctx, l_ctx, _ = x_prompt.shape
    n_dec, l_dec, _ = x_sample.shape
    assert w_mod.shape[0] == DEPTH == 1, "single-layer trunk"
    lyr = 0

    cond_t = jnp.pad(jnp.concatenate([c_ctx[None, :], c], 0).T, ((0, 0), (0, 7 - n_dec)))
    mod4 = _mod_call(cond_t, 1 + n_dec, w_mod[lyr], b_mod[lyr]).reshape(8, N_MOD, 1, d)

    in_w = w_in[lyr].astype(BF16)

    prm = jnp.pad(jnp.stack([a_log[lyr].reshape(-1), dt_bias[lyr].reshape(-1)]),
                  ((0, 6), (2 * DN_HEADS, LANES - 4 * DN_HEADS)))
    gn = gnorm_w[lyr].reshape(1, -1)

    wua = w_up_a[lyr].astype(BF16)
    wub = w_up_b[lyr].astype(BF16)
    wout_half = (0.5 * w_out[lyr]).astype(BF16)
    wr = jnp.pad(w_router[lyr], ((0, 0), (0, LANES - N_EXPERTS)))
    wr_hi = wr.astype(BF16)
    wr_lo = (wr - wr_hi.astype(F32)).astype(BF16)
    l1g, l1b = ln1_g[lyr].reshape(1, -1), ln1_b[lyr].reshape(1, -1)
    l2g, l2b = ln2_g[lyr].reshape(1, -1), ln2_b[lyr].reshape(1, -1)

    def front(x2d, pos, bsz, seq, mod_row, s0f, s0b):
        (qkv, z, ba, hy, gates), pre_conv = _inproj_call(x2d, pos, mod4, mod_row, in_w, conv_a_w[lyr],
                                                          conv_b_w[lyr], conv_b_b[lyr], seq, ACT_DT)
        reqs = DN_REQS_PER_STEP if seq == DN_CHUNK and bsz % DN_REQS_PER_STEP == 0 else 1
        o_a, s_f, s_b = _deltanet_call(qkv, z, ba, conv_a_w[lyr], prm, gn, s0f, s0b, bsz, seq, reqs,
                                       pre_conv, ACT_DT)
        fwd, inv = (jnp.asarray(m) for m in _dft_mats(seq))
        fwd_hi = fwd.astype(BF16)
        fwd_lo = (fwd - fwd_hi.astype(F32)).astype(BF16)
        filt_params = (hy_w1[lyr], hy_b1[lyr], hy_w2[lyr], hy_b2[lyr], hy_w3[lyr], hy_freq[lyr])
        o_h = _hyena_call(hy, pre_conv, conv_b_w[lyr], conv_b_b[lyr], filt_params, hy_bias[lyr],
                          fwd_hi, fwd_lo, inv.astype(BF16), bsz, seq, ACT_DT)
        x1, u2, probs = _mix_call(x2d, pos, o_a, o_h, gates, mod4, mod_row, wua, wub, wout_half, l1g, l1b,
                                   wr_hi, wr_lo)
        xs, slot, pt = _route_call(probs, u2, bsz, seq)
        return x1, xs, slot, pt, s_f, s_b

    row_ctx = lambda tok: 0
    row_dec = lambda tok: 1 + tok // l_dec

    xc = x_prompt.reshape(n_ctx * l_ctx, d)
    xd = x_sample.reshape(n_dec * l_dec, d)
    pos = _grid_pos_embed(l_dec // GRID_W, d)

    x1c, xsc, gc, ptc, s_f, s_b = front(xc, None, n_ctx, l_ctx, row_ctx, None, None)
    x1d, xsd, gd, ptd, _, _ = front(xd, pos, n_dec, l_dec, row_dec,
                                    state_delta_fwd[:, lyr], state_delta_bwd[:, lyr])
    yc, yd = _expert_call(xsc, xsd, w_e_gate[lyr], w_e_up[lyr], w_e_down[lyr])
    reqs_c = FINAL_CTX_REQS_PER_STEP if n_ctx % FINAL_CTX_REQS_PER_STEP == 0 else 1
    y_prompt = _final_call(x1c, yc, gc, ptc, mod4, row_ctx, l2g, l2b, n_ctx, l_ctx, reqs_c)
    y_sample = _final_call(x1d, yd, gd, ptd, mod4, row_dec, l2g, l2b, n_dec, l_dec, 1)

    return (y_prompt.reshape(n_ctx, l_ctx, d), y_sample.reshape(n_dec, l_dec, d),
            s_f[:, None], s_b[:, None])
```

```python
import functools
import math

import jax
import jax.numpy as jnp
import numpy as np
from jax import lax
from jax.experimental import pallas as pl
from jax.experimental.pallas import tpu as pltpu

F32 = jnp.float32
BF16 = jnp.bfloat16
HIGHEST = lax.Precision.HIGHEST

D_MODEL = 1024
DEPTH = 1
GRID_W = 64
DN_HEADS = 4
DN_DK = 128
DN_DV = 128
DN_WIDTH = DN_HEADS * DN_DV
HY_WIDTH = D_MODEL // 2
HY_EMB = 33
HY_FFN = 64
HY_TARGET = 1e-2
HY_FAST_DECAY = 0.3
HY_SLOW_DECAY = 1.5
N_EXPERTS = 16
EC_FACTOR = 2
N_MOD = 6
ALPHA = (2 * DEPTH) ** 0.25
LN_EPS = 1e-5
RMS_EPS = 1e-6
QKV_WIDTH = 2 * DN_HEADS * DN_DK + DN_WIDTH

LANES = 128
INPROJ_TILE = 1024
INPROJ_BLOCK_ROWS = 512
MIX_TILE = 1024
MIX_SUB_ROWS = 256
MIX_MIN_STEPS = 4
MOD_COL_BLOCK = 1024
DN_CHUNK = 256
INV_BASE = 8
DN_REQS_PER_STEP = 2
HY_ROWS_PER_STEP = 1024
FINAL_CTX_REQS_PER_STEP = 4
EXPERT_ROW_BLOCK = 512
VMEM_LIMIT = 56 * 1024 * 1024
MIN_NORMAL_F32_BITS = 0x00800000
ACT_DT = BF16
ROUTE_ROWS_PER_STEP = 1024
RANK_COUNT_MAX_SEQ = 256


def _cparams(n_axes):
    return pltpu.CompilerParams(dimension_semantics=("arbitrary",) * n_axes,
                                vmem_limit_bytes=VMEM_LIMIT)


def _bmm(a, b):
    return jnp.einsum("bmk,bkn->bmn", a.astype(BF16), b.astype(BF16), preferred_element_type=F32)


def _bmm_nt(a, b):
    return jnp.einsum("bmk,bnk->bmn", a.astype(BF16), b.astype(BF16), preferred_element_type=F32)


def _bmm_tn(a, b):
    return jnp.einsum("bkm,bkn->bmn", a.astype(BF16), b.astype(BF16), preferred_element_type=F32)


def _mm_f32(a, b):
    return jnp.dot(a, b, precision=HIGHEST, preferred_element_type=F32)


def _sigmoid(x):
    return 0.5 * jnp.tanh(0.5 * x) + 0.5


def _silu(x):
    h = 0.5 * x
    return h * (jnp.tanh(h) + 1.0)


def _softplus(x):
    return jnp.maximum(x, 0.0) + jnp.log1p(jnp.exp(-jnp.abs(x)))


def _ln(x):
    mu = jnp.mean(x, axis=-1, keepdims=True)
    xc = x - mu
    var = jnp.mean(xc * xc, axis=-1, keepdims=True)
    return xc * lax.rsqrt(var + LN_EPS)


def _l2norm(x, scale=1.0):
    return x * (lax.rsqrt(jnp.sum(x * x, axis=-1, keepdims=True) + 1e-6) * scale)


def _iota(shape, dim):
    return lax.broadcasted_iota(jnp.int32, shape, dim)


def _short_conv3(x, w, first, last):
    xm = jnp.where(first, 0.0, pltpu.roll(x, 1, 0))
    xp = jnp.where(last, 0.0, pltpu.roll(x, x.shape[0] - 1, 0))
    return w[0:1, :] * xm + w[1:2, :] * x + w[2:3, :] * xp


def _mod_kernel(n_rows, ct_ref, w_ref, b_ref, o_ref):
    s = _silu(ct_ref[...])
    w = w_ref[...]
    b = b_ref[...]
    o_ref[...] = jnp.zeros(o_ref.shape, F32)
    for r in range(n_rows):
        o_ref[r:r + 1, :] = jnp.sum(w * s[:, r:r + 1], axis=0, keepdims=True) + b


def _mod_call(cond_t, n_rows, w_mod, b_mod):
    d = D_MODEL
    tn = MOD_COL_BLOCK
    return pl.pallas_call(
        functools.partial(_mod_kernel, n_rows),
        grid=(N_MOD * d // tn,),
        in_specs=[pl.BlockSpec((d, 8), lambda j: (0, 0)),
                  pl.BlockSpec((d, tn), lambda j: (0, j)),
                  pl.BlockSpec((1, tn), lambda j: (0, j))],
        out_specs=pl.BlockSpec((8, tn), lambda j: (0, j)),
        out_shape=jax.ShapeDtypeStruct((8, N_MOD * d), F32),
        compiler_params=_cparams(1),
        name="mod",
    )(cond_t, w_mod, b_mod.reshape(1, -1))


def _mod_spec(row_of_step):
    return pl.BlockSpec((1, N_MOD, 1, D_MODEL), lambda *idx: (row_of_step(*idx), 0, 0, 0))


def _inproj_kernel(has_pos, conv_seq, *refs):
    refs = list(refs)
    x_ref = refs.pop(0)
    pos_ref = refs.pop(0) if has_pos else None
    mod_ref, w_ref = refs.pop(0), refs.pop(0)
    if conv_seq:
        cwa_ref, cwb_ref, cbb_ref = refs.pop(0), refs.pop(0), refs.pop(0)
    qkv_o, z_o, ba_o, hy_o, g_o = refs
    sh1 = mod_ref[0, 0]
    sc1 = mod_ref[0, 1]
    o_z = QKV_WIDTH
    o_ba = o_z + DN_WIDTH
    n_ba = 4 * DN_HEADS
    o_g = n_ba + 3 * HY_WIDTH
    lanes = lambda j: slice(j * LANES, (j + 1) * LANES)
    tm = x_ref.shape[0]
    sub = min(tm, INPROJ_BLOCK_ROWS)
    blocks = [slice(r0, r0 + sub) for r0 in range(0, tm, sub)]

    if conv_seq:
        row = _iota((sub, LANES), 0) % conv_seq
        first = row == 0
        last = row == conv_seq - 1

    def project(rows):
        x = x_ref[rows, :]
        if has_pos:
            x = x + pos_ref[rows, :]
        u = (_ln(x) * (1.0 + sc1) + sh1).astype(BF16)
        return (jnp.dot(u, w_ref[:, :o_ba], preferred_element_type=F32),
                jnp.dot(u, w_ref[:, o_ba:], preferred_element_type=F32))

    def epilogue(rows, head, tail):
        z_o[rows, :] = head[:, o_z:].astype(z_o.dtype)
        ba_o[rows, :] = tail[:, :LANES]
        g_o[rows, :] = tail[:, o_g:].astype(g_o.dtype)
        hy = tail[:, n_ba:o_g]
        if not conv_seq:
            qkv_o[rows, :] = head[:, :o_z].astype(qkv_o.dtype)
            hy_o[rows, :] = hy.astype(hy_o.dtype)
            return
        for j in range(QKV_WIDTH // LANES):
            y = _silu(_short_conv3(head[:, lanes(j)], cwa_ref[:, lanes(j)], first, last))
            if j < 2 * DN_HEADS:
                y = _l2norm(y, DN_DK ** -0.5 if j < DN_HEADS else 1.0)
            qkv_o[rows, lanes(j)] = y.astype(qkv_o.dtype)
        nj = HY_WIDTH // LANES
        for j in range(nj):
            parts = [_short_conv3(hy[:, lanes(p * nj + j)], cwb_ref[:, lanes(p * nj + j)], first, last)
                     + cbb_ref[:, lanes(p * nj + j)] for p in range(3)]
            hy_o[rows, lanes(j)] = parts[0].astype(hy_o.dtype)
            hy_o[rows, lanes(nj + j)] = (parts[1] * parts[2]).astype(hy_o.dtype)

    pending = None
    for rows in blocks:
        cur = (rows, *project(rows))
        if pending is not None:
            epilogue(*pending)
        pending = cur
    epilogue(*pending)


def _inproj_call(x2d, pos, mod4, mod_row, w, conv_a_w, conv_b_w, conv_b_b, seq, act_dt):
    n, d = x2d.shape
    fuse = seq <= INPROJ_BLOCK_ROWS and INPROJ_TILE % seq == 0
    tm = min(INPROJ_TILE if fuse else INPROJ_BLOCK_ROWS, n)
    assert n % tm == 0 and (not fuse or tm % seq == 0)
    conv_seq = seq if fuse else None
    row = lambda i: (i, 0)
    const = lambda i: (0, 0)
    in_specs = [pl.BlockSpec((tm, d), row)]
    args = [x2d]
    if pos is not None:
        tiles = pos.shape[0] // tm
        in_specs.append(pl.BlockSpec((tm, d), lambda i: (i % tiles, 0)))
        args.append(pos)
    in_specs.append(_mod_spec(lambda i: mod_row(i * tm)))
    args.append(mod4)
    in_specs.append(pl.BlockSpec(w.shape, const, pipeline_mode=pl.Buffered(1)))
    args.append(w)
    if conv_seq:
        for a in (conv_a_w, conv_b_w, conv_b_b.reshape(1, -1)):
            in_specs.append(pl.BlockSpec(a.shape, const))
            args.append(a)
    hy_width = (2 if conv_seq else 3) * HY_WIDTH
    widths = (QKV_WIDTH, DN_WIDTH, LANES, hy_width, 2 * D_MODEL)
    dts = (act_dt, act_dt, F32, act_dt, act_dt)
    outs = pl.pallas_call(
        functools.partial(_inproj_kernel, pos is not None, conv_seq),
        grid=(n // tm,),
        in_specs=in_specs,
        out_specs=[pl.BlockSpec((tm, w), row) for w in widths],
        out_shape=[jax.ShapeDtypeStruct((n, w), dt) for w, dt in zip(widths, dts)],
        compiler_params=_cparams(1),
        name="inproj",
    )(*args)
    return outs, conv_seq is not None


def _inv_unit_tri(lm, ri, ci, block):
    c = lm.shape[-1]

    def same_block(h):
        sh = h.bit_length() - 1
        return (ri >> sh) == (ci >> sh)

    def unfold(f, h):
        return jnp.where(same_block(h), jnp.concatenate([f] * (c // h), axis=1), 0.0)

    base = INV_BASE
    n_nat = jnp.where(same_block(base), -lm, 0.0)
    n_fold = n_nat[:, 0:base, :]
    for i in range(1, c // base):
        n_fold = n_fold + n_nat[:, i * base:(i + 1) * base, :]
    eye_fold = jnp.where((_iota((base, c), 1) & (base - 1)) == _iota((base, c), 0), 1.0, 0.0)
    t = eye_fold + n_fold
    p_fold, p_nat = n_fold, n_nat
    span = 2
    while span < base:
        p_fold = _bmm(p_fold, p_nat)
        p_nat = unfold(p_fold, base)
        t = t + _bmm(t, p_nat)
        span *= 2
    s = base
    while s < block:
        sh = s.bit_length() - 1
        t_nat = unfold(t, s)
        even = ((_iota((s, c), 1) >> sh) & 1) == 0
        t2 = jnp.concatenate([jnp.where(even, t, 0.0), jnp.where(even, 0.0, t)], axis=1)
        lo = jnp.where(same_block(2 * s) & ~same_block(s), lm, 0.0)
        t = t2 - _bmm(_bmm(t2, lo), t_nat)
        s *= 2
    return t


def _dn_chunk(q, k, v, gc_col, gc_row, beta, e_col, e_rest, e_all, s_prev, same_chunk, ri, ci):
    c = DN_CHUNK
    nh = q.shape[0] // 2
    half = c // 2
    if same_chunk:
        gram = _bmm_nt(k[:nh], k[:nh])
        qk = _bmm_nt(q[:nh], k[:nh])
        gram = jnp.concatenate([gram, gram], 0)
        qk = jnp.concatenate([qk, qk], 0)
    else:
        gram = _bmm_nt(k, k)
        qk = _bmm_nt(q, k)
    diff = gc_col - gc_row
    incl = jnp.concatenate([jnp.broadcast_to(ri >= ci, (nh, c, c)), jnp.broadcast_to(ri <= ci, (nh, c, c))], 0)
    strict = incl & (ri != ci)
    decay = jnp.exp(jnp.where(incl, diff, -jnp.inf))
    lm = jnp.where(strict, beta * gram * decay, 0.0)
    a_intra = qk * decay
    t = _inv_unit_tri(lm, ri, ci, half)
    t1 = t[:, :, :half]
    t2 = t[:, :, half:]
    rhs = jnp.concatenate([v * beta, k * (beta * e_col)], axis=2)
    y1 = _bmm(t1, rhs[:, :half, :])
    y2 = _bmm(t2, rhs[:, half:, :])
    c_f = _bmm(lm[:nh, half:, :half], y1[:nh])
    c_b = _bmm(lm[nh:, :half, half:], y2[nh:])
    c_f = _bmm(t2[:nh], c_f)
    c_b = _bmm(t1[nh:], c_b)
    sol = jnp.concatenate([jnp.concatenate([y1[:nh], y2[:nh] - c_f], 1),
                           jnp.concatenate([y1[nh:] - c_b, y2[nh:]], 1)], 0)
    u = sol[:, :, :DN_DV]
    w = sol[:, :, DN_DV:]
    ks = k * e_rest
    if s_prev is None:
        v_new = u
        o = _bmm(a_intra, v_new)
        s_new = _bmm_tn(ks, v_new)
    else:
        v_new = u - _bmm(w, s_prev)
        o = _bmm(q * e_col, s_prev) + _bmm(a_intra, v_new)
        s_new = s_prev * e_all + _bmm_tn(ks, v_new)
    return o, s_new


def _deltanet_kernel(seq, reqs, zero_init, pre_conv, *refs):
    qkv_ref, cw_ref, z_ref, ba_ref, prm_ref, gn_ref = refs[:6]
    refs = refs[6:]
    if not zero_init:
        s0f_ref, s0b_ref = refs[:2]
        refs = refs[2:]
    o_ref, sf_ref, sb_ref, q_s, k_s, v_s, g_s, b_s, o_s, st_s = refs
    c = DN_CHUNK
    n = seq // c
    nh = DN_HEADS
    nf = reqs * nh
    lanes = lambda j: slice(j * LANES, (j + 1) * LANES)

    if not pre_conv:
        row = _iota((seq, LANES), 0)
        first_row = row == 0
        last_row = row == seq - 1
    for r in range(reqs):
        rows = slice(r * seq, (r + 1) * seq)

        def qkv_part(j):
            x = qkv_ref[rows, lanes(j)].astype(F32)
            return x if pre_conv else _silu(_short_conv3(x, cw_ref[:, lanes(j)], first_row, last_row))

        for h in range(nh):
            q = qkv_part(h)
            k = qkv_part(nh + h)
            q_s[r * nh + h] = q if pre_conv else _l2norm(q, DN_DK ** -0.5)
            k_s[r * nh + h] = k if pre_conv else _l2norm(k)
            v_s[r * nh + h] = qkv_part(2 * nh + h)

    ba = ba_ref[...]
    b_s[...] = _sigmoid(ba)
    g_s[...] = -jnp.exp(prm_ref[0:1, :]) * _softplus(ba + prm_ref[1:2, :])

    ri = _iota((c, c), 0)
    ci = _iota((c, c), 1)
    tri_l = jnp.where(ri >= ci, 1.0, 0.0).astype(F32)
    tri_u = jnp.where(ri <= ci, 1.0, 0.0).astype(F32)

    def chunk_pair(idx_f, idx_b, s_prev):
        def ds(idx, base=0):
            start = idx * c
            start = start if isinstance(start, int) else pl.multiple_of(start, c)
            return pl.ds(base + start, c)

        col_f, col_b, row_f, row_b, beta_f, beta_b = [], [], [], [], [], []
        for r in range(reqs):
            cs_f = _mm_f32(tri_l, g_s[ds(idx_f, r * seq), :])
            cs_b = _mm_f32(tri_u, g_s[ds(idx_b, r * seq), :])
            cst_f = cs_f.T
            cst_b = cs_b.T
            b_f = b_s[ds(idx_f, r * seq), :]
            b_b = b_s[ds(idx_b, r * seq), :]
            for h in range(nh):
                jf, jb = 2 * nh + h, 3 * nh + h
                col_f.append(cs_f[:, jf:jf + 1])
                col_b.append(cs_b[:, jb:jb + 1])
                row_f.append(cst_f[jf:jf + 1, :])
                row_b.append(cst_b[jb:jb + 1, :])
                beta_f.append(b_f[:, h:h + 1])
                beta_b.append(b_b[:, nh + h:nh + h + 1])
        gc_col = jnp.stack(col_f + col_b, 0)
        gc_row = jnp.stack(row_f + row_b, 0)
        beta = jnp.stack(beta_f + beta_b, 0)
        e_col = jnp.exp(gc_col)
        gl = jnp.concatenate([gc_col[:nf, c - 1:c, :], gc_col[nf:, 0:1, :]], 0)
        e_rest = jnp.exp(gl - gc_col)
        e_all = jnp.exp(gl)
        sl_f, sl_b = ds(idx_f), ds(idx_b)
        both = lambda ref: jnp.concatenate([ref[:, sl_f, :], ref[:, sl_b, :]], 0)
        o, s_new = _dn_chunk(both(q_s), both(k_s), both(v_s), gc_col, gc_row, beta, e_col, e_rest, e_all,
                             s_prev, n == 1, ri, ci)
        o_s[:nf, sl_f, :] = o[:nf]
        o_s[nf:, sl_b, :] = o[nf:]
        return s_new

    if not zero_init:
        s0 = jnp.concatenate([s0f_ref[r] for r in range(reqs)] + [s0b_ref[r] for r in range(reqs)], 0)
    if n == 1:
        s_fin = chunk_pair(0, 0, None if zero_init else s0)
    else:
        st_s[...] = jnp.zeros(st_s.shape, F32) if zero_init else s0

        def body(i, carry):
            st_s[...] = chunk_pair(i, n - 1 - i, st_s[...])
            return carry

        lax.fori_loop(0, n, body, 0)
        s_fin = st_s[...]
    for r in range(reqs):
        sf_ref[r] = s_fin[r * nh:(r + 1) * nh]
        sb_ref[r] = s_fin[nf + r * nh:nf + (r + 1) * nh]

    gn = gn_ref[...]
    for r in range(reqs):
        rows = slice(r * seq, (r + 1) * seq)
        for h in range(nh):
            o = o_s[r * nh + h] + o_s[nf + r * nh + h]
            o = o * lax.rsqrt(jnp.mean(o * o, axis=-1, keepdims=True) + RMS_EPS) * gn
            o_ref[rows, lanes(h)] = (o * _silu(z_ref[rows, lanes(h)].astype(F32))).astype(o_ref.dtype)


def _deltanet_call(qkv, z, ba, conv_w, prm, gn, s0f, s0b, bsz, seq, reqs, pre_conv, act_dt):
    nh = DN_HEADS
    zero_init = s0f is None
    rows = reqs * seq
    st = pl.BlockSpec((reqs, nh, DN_DK, DN_DV), lambda b: (b, 0, 0, 0))
    in_specs = [pl.BlockSpec((rows, QKV_WIDTH), lambda b: (b, 0)),
                pl.BlockSpec((3, QKV_WIDTH), lambda b: (0, 0)),
                pl.BlockSpec((rows, DN_WIDTH), lambda b: (b, 0)),
                pl.BlockSpec((rows, LANES), lambda b: (b, 0)),
                pl.BlockSpec((8, LANES), lambda b: (0, 0)),
                pl.BlockSpec((1, LANES), lambda b: (0, 0))]
    args = [qkv, conv_w, z, ba, prm, gn]
    if not zero_init:
        in_specs += [st, st]
        args += [s0f, s0b]
    vm = lambda shape: pltpu.VMEM(shape, F32)
    per_head = vm((reqs * nh, seq, LANES))
    return pl.pallas_call(
        functools.partial(_deltanet_kernel, seq, reqs, zero_init, pre_conv),
        grid=(bsz // reqs,),
        in_specs=in_specs,
        out_specs=[pl.BlockSpec((rows, DN_WIDTH), lambda b: (b, 0)), st, st],
        out_shape=[jax.ShapeDtypeStruct((bsz * seq, DN_WIDTH), act_dt),
                   jax.ShapeDtypeStruct((bsz, nh, DN_DK, DN_DV), F32),
                   jax.ShapeDtypeStruct((bsz, nh, DN_DK, DN_DV), F32)],
        scratch_shapes=[per_head, per_head, per_head, vm((rows, LANES)), vm((rows, LANES)),
                        vm((2 * reqs * nh, seq, LANES)), vm((2 * reqs * nh, DN_DK, DN_DV))],
        compiler_params=_cparams(1),
        name="deltanet",
    )(*args)


def _dft_mats(seq):
    n = 2 * seq
    f = np.arange(seq)[:, None]
    s = np.arange(seq)[None, :]
    ang = 2.0 * np.pi * ((f * s) % n) / n
    fwd = np.concatenate([np.cos(ang), -np.sin(ang)], axis=0)
    fwd[seq, :] = np.cos(np.pi * np.arange(seq))
    t = (np.arange(seq) + seq // 2)[:, None]
    ff = np.arange(seq)[None, :]
    ang2 = 2.0 * np.pi * ((t * ff) % n) / n
    inv_r = 2.0 * np.cos(ang2) / n
    inv_i = -2.0 * np.sin(ang2) / n
    inv_r[:, 0] = 1.0 / n
    inv_i[:, 0] = np.cos(np.pi * t[:, 0]) / n
    inv = np.concatenate([inv_r, inv_i], axis=1)
    return fwd.astype(np.float32), inv.astype(np.float32)


def _filter_feats(seq):
    t = np.linspace(0.0, 1.0, seq)[:, None]
    bands = (HY_EMB - 1) // 2
    ang = (2.0 * math.pi * np.arange(seq) / seq)[:, None] * np.linspace(1e-4, bands - 1, bands)[None, :]
    feats = np.concatenate([t, np.cos(ang), -np.sin(ang)], -1)
    deltas = np.abs(np.linspace(math.log(HY_TARGET) / HY_SLOW_DECAY, math.log(HY_TARGET) / HY_FAST_DECAY,
                                HY_WIDTH))
    offset = np.abs(np.arange(seq) - seq // 2) / (seq // 2)
    window = np.exp(-offset[:, None] * deltas[None, :])
    feats = np.pad(feats, ((0, 0), (0, LANES - HY_EMB)))
    return jnp.asarray(feats, dtype=F32), jnp.asarray(window, dtype=F32)


def _filter_spectrum(seq, feats_ref, win_ref, w1_ref, b1_ref, w2_ref, b2_ref, w3_ref, fr_ref, fwd_hi_ref,
                     fwd_lo_ref, ha_s, hb_s, hd_s):
    fr = fr_ref[...]
    hid = jnp.sin(fr * (_mm_f32(feats_ref[...], w1_ref[...]) + b1_ref[...]))
    hid = jnp.sin(fr * (_mm_f32(hid, w2_ref[...]) + b2_ref[...]))
    filt = _mm_f32(hid, w3_ref[...]) * win_ref[...]
    filt = filt / (jnp.sum(jnp.abs(filt), axis=0, keepdims=True) + 1e-6)
    filt_hi = filt.astype(BF16)
    filt_lo = (filt - filt_hi.astype(F32)).astype(BF16)
    fwd_hi = fwd_hi_ref[...]
    spec = (jnp.dot(fwd_hi, filt_hi, preferred_element_type=F32)
            + jnp.dot(fwd_hi, filt_lo, preferred_element_type=F32)
            + jnp.dot(fwd_lo_ref[...], filt_hi, preferred_element_type=F32))
    h_re = spec[:seq, :]
    h_im = spec[seq:, :]
    first = _iota(h_re.shape, 0) == 0
    ha_s[...] = h_re
    hb_s[...] = jnp.where(first, 0.0, h_im)
    hd_s[...] = jnp.where(first, h_im, h_re)


def _hyena_kernel(seq, reqs, pre_conv, *refs):
    refs = list(refs)
    n_tok = 2 if pre_conv else 9
    tok_refs, refs = refs[:n_tok], refs[n_tok:]
    filt_refs, refs = refs[:8], refs[8:]
    skip_ref, fwd_ref, fwd_lo_ref, inv_ref, o_ref, ha_s, hb_s, hd_s = refs

    @pl.when(pl.program_id(0) == 0)
    def _():
        _filter_spectrum(seq, *filt_refs, fwd_ref, fwd_lo_ref, ha_s, hb_s, hd_s)

    if pre_conv:
        x0_ref, uu_ref = tok_refs
    else:
        x0_ref, x1_ref, v_ref, c0_ref, c1_ref, c2_ref, b0_ref, b1_ref, b2_ref = tok_refs
        row = _iota((seq, HY_WIDTH), 0)
        first = row == 0
        last = row == seq - 1
    hb = hb_s[...]
    for r in range(reqs):
        rows = slice(r * seq, (r + 1) * seq)
        if pre_conv:
            x0 = x0_ref[rows, :].astype(F32)
            uu = uu_ref[rows, :].astype(F32)
        else:
            conv = lambda x_ref, w_ref, b_ref: (_short_conv3(x_ref[rows, :].astype(F32), w_ref[...], first, last)
                                                + b_ref[...])
            x0 = conv(x0_ref, c0_ref, b0_ref)
            uu = conv(x1_ref, c1_ref, b1_ref) * conv(v_ref, c2_ref, b2_ref)
        spec = jnp.dot(fwd_ref[...], uu.astype(BF16), preferred_element_type=F32)
        u_re = spec[:seq, :]
        u_im = spec[seq:, :]
        y_re = u_re * ha_s[...] - u_im * hb
        y_im = u_re * hb + u_im * hd_s[...]
        y = jnp.concatenate([y_re, y_im], axis=0).astype(BF16)
        cv = jnp.dot(inv_ref[...], y, preferred_element_type=F32)
        o_ref[rows, :] = (x0 * (cv + uu * skip_ref[...])).astype(o_ref.dtype)


def _hyena_call(hy, pre_conv, conv_w, conv_b, filt_params, skip, fwd_hi, fwd_lo, inv, bsz, seq, act_dt):
    w1, b1, w2, b2, w3, freq = filt_params
    cb = HY_WIDTH
    reqs = max(1, min(bsz, HY_ROWS_PER_STEP // seq))
    assert bsz % reqs == 0
    tok = lambda part: pl.BlockSpec((reqs * seq, cb), lambda b: (b, part))
    const = lambda a: pl.BlockSpec(a.shape, lambda b: (0,) * a.ndim, pipeline_mode=pl.Buffered(1))
    feats, window = _filter_feats(seq)
    w1p = jnp.pad(w1, ((0, LANES - HY_EMB), (0, 0)))
    filt_args = [feats, window, w1p, b1.reshape(1, -1), w2, b2.reshape(1, -1), w3, freq.reshape(1, -1)]
    tail_args = filt_args + [skip.reshape(1, -1), fwd_hi, fwd_lo, inv]
    tail_specs = [const(a) for a in tail_args]
    if pre_conv:
        in_specs = [tok(0), tok(1)] + tail_specs
        args = [hy, hy] + tail_args
    else:
        conv_b2 = conv_b.reshape(1, -1)
        part3 = lambda a, part: pl.BlockSpec((a.shape[0], cb), lambda b: (0, part))
        in_specs = ([tok(0), tok(1), tok(2)] + [part3(conv_w, p) for p in range(3)]
                    + [part3(conv_b2, p) for p in range(3)] + tail_specs)
        args = [hy, hy, hy, conv_w, conv_w, conv_w, conv_b2, conv_b2, conv_b2] + tail_args
    spec_buf = pltpu.VMEM((seq, HY_WIDTH), F32)
    return pl.pallas_call(
        functools.partial(_hyena_kernel, seq, reqs, pre_conv),
        grid=(bsz // reqs,),
        in_specs=in_specs,
        out_specs=tok(0),
        out_shape=jax.ShapeDtypeStruct((bsz * seq, HY_WIDTH), act_dt),
        scratch_shapes=[spec_buf, spec_buf, spec_buf],
        compiler_params=_cparams(1),
        name="hyena",
    )(*args)


def _mix_kernel(has_pos, *refs):
    refs = list(refs)
    x_ref = refs.pop(0)
    pos_ref = refs.pop(0) if has_pos else None
    (oa_ref, ob_ref, g_ref, mod_ref, wua, wub, wout_half, l1g, l1b, wr_hi, wr_lo, x1_o, u2_o, p_o) = refs
    d = D_MODEL
    tm = x_ref.shape[0]
    sub = min(tm, MIX_SUB_ROWS)
    blocks = [slice(r0, r0 + sub) for r0 in range(0, tm, sub)]
    g1 = mod_ref[0, 2]
    sh2 = mod_ref[0, 3]
    sc2 = mod_ref[0, 4]

    mixed = []
    for rows in blocks:
        th = jnp.tanh(0.5 * g_ref[rows, :].astype(F32))
        a = jnp.dot(oa_ref[rows, :], wua[...], preferred_element_type=F32)
        b = jnp.dot(ob_ref[rows, :], wub[...], preferred_element_type=F32)
        up2 = (th[:, :d] * a + a) + (th[:, d:] * b + b)
        mixed.append(jnp.dot(up2.astype(BF16), wout_half[...], preferred_element_type=F32))

    for rows, mx in zip(blocks, mixed):
        x = x_ref[rows, :]
        if has_pos:
            x = x + pos_ref[rows, :]
        x1 = _ln(ALPHA * x + g1 * mx) * l1g[...] + l1b[...]
        x1_o[rows, :] = x1.astype(x1_o.dtype)
        u2 = _ln(x1) * (1.0 + sc2) + sh2
        u2_hi = u2.astype(BF16)
        u2_o[rows, :] = u2_hi
        u2_lo = (u2 - u2_hi.astype(F32)).astype(BF16)
        logits = (jnp.dot(u2_hi, wr_hi[...], preferred_element_type=F32)
                  + jnp.dot(u2_hi, wr_lo[...], preferred_element_type=F32)
                  + jnp.dot(u2_lo, wr_hi[...], preferred_element_type=F32))
        lane = _iota(logits.shape, 1)
        logits = jnp.where(lane < N_EXPERTS, logits, -jnp.inf)
        m = jnp.max(logits, axis=-1, keepdims=True)
        e = jnp.exp(logits - m)
        p_o[rows, :] = e / jnp.sum(e, axis=-1, keepdims=True)


def _mix_call(x2d, pos, o_a, o_b, gates, mod4, mod_row, wua, wub, wout_half, l1g, l1b, wr_hi, wr_lo):
    n, d = x2d.shape
    tm = min(MIX_TILE, max(n // MIX_MIN_STEPS, MIX_SUB_ROWS))
    assert n % tm == 0
    row = lambda i: (i, 0)
    const = lambda i: (0, 0)
    in_specs = [pl.BlockSpec((tm, d), row)]
    args = [x2d]
    if pos is not None:
        tiles = pos.shape[0] // tm
        in_specs.append(pl.BlockSpec((tm, d), lambda i: (i % tiles, 0)))
        args.append(pos)
    in_specs += [pl.BlockSpec((tm, DN_WIDTH), row), pl.BlockSpec((tm, HY_WIDTH), row),
                 pl.BlockSpec((tm, 2 * d), row),
                 _mod_spec(lambda i: mod_row(i * tm))]
    args += [o_a, o_b, gates, mod4]
    for w in (wua, wub, wout_half, l1g, l1b, wr_hi, wr_lo):
        in_specs.append(pl.BlockSpec(w.shape, const, pipeline_mode=pl.Buffered(1)))
        args.append(w)
    return pl.pallas_call(
        functools.partial(_mix_kernel, pos is not None),
        grid=(n // tm,),
        in_specs=in_specs,
        out_specs=[pl.BlockSpec((tm, d), row), pl.BlockSpec((tm, d), row), pl.BlockSpec((tm, LANES), row)],
        out_shape=[jax.ShapeDtypeStruct((n, d), ACT_DT), jax.ShapeDtypeStruct((n, d), BF16),
                   jax.ShapeDtypeStruct((n, LANES), F32)],
        compiler_params=_cparams(1),
        name="mix",
    )(*args)


def _route_kernel(seq, cap, reqs, p_ref, u_ref, tri_ref, xs_ref, slot_ref, pt_ref, g_s):
    e_n = N_EXPERTS
    tri = tri_ref[...]
    jrow = _iota((cap, seq), 0).astype(F32)
    if seq <= RANK_COUNT_MAX_SEQ:
        earlier = _iota((seq, seq), 0) < _iota((seq, seq), 1)
    for r in range(reqs):
        rows = slice(r * seq, (r + 1) * seq)
        p = p_ref[rows, :]
        pt = p.T[:e_n, :]

        if seq <= RANK_COUNT_MAX_SEQ:
            ranks = []
            for e in range(e_n):
                pc = p[:, e:e + 1]
                pr = pt[e:e + 1, :]
                beats = (pc > pr) | (earlier & (pc == pr))
                ranks.append(jnp.sum(jnp.where(beats, 1.0, 0.0), axis=0, keepdims=True))
            sel = jnp.concatenate(ranks, axis=0) < cap
        else:
            def ok(cand):
                cnt = jnp.sum(jnp.where(pt >= pltpu.bitcast(cand, F32), 1.0, 0.0), axis=1, keepdims=True)
                return (cnt >= cap) & (cand >= MIN_NORMAL_F32_BITS)

            def search(i, cur):
                lo = 28 - 2 * i
                c1, c2, c3 = cur | (1 << lo), cur | (2 << lo), cur | (3 << lo)
                return jnp.where(ok(c3), c3, jnp.where(ok(c2), c2, jnp.where(ok(c1), c1, cur)))

            top = jnp.full((e_n, 1), 1 << 30, jnp.int32)
            cur = jnp.where(ok(top), top, 0)
            thr = pltpu.bitcast(lax.fori_loop(0, 15, search, cur), F32)
            gt = pt > thr
            eq = pt == thr
            n_gt = jnp.sum(jnp.where(gt, 1.0, 0.0), axis=1, keepdims=True)
            eq_rank = jnp.dot(jnp.where(eq, 1.0, 0.0).astype(BF16), tri, preferred_element_type=F32)
            sel = gt | (eq & (eq_rank < cap - n_gt))
        pos = jnp.dot(jnp.where(sel, 1.0, 0.0).astype(BF16), tri, preferred_element_type=F32)
        slot = jnp.where(sel, pos, -1.0)
        slot_ref[r] = slot
        pt_ref[r] = pt
        for e in range(e_n):
            g_s[r, e * cap:(e + 1) * cap, :] = jnp.where(slot[e:e + 1, :] == jrow, 1.0, 0.0).astype(BF16)
        xs = jnp.dot(g_s[r], u_ref[rows, :], preferred_element_type=F32)
        for e in range(e_n):
            xs_ref[e, r * cap:(r + 1) * cap, :] = xs[e * cap:(e + 1) * cap, :].astype(xs_ref.dtype)


def _route_call(probs, u2, bsz, seq):
    cap = EC_FACTOR * seq // N_EXPERTS
    reqs = max(1, min(bsz, ROUTE_ROWS_PER_STEP // seq))
    assert bsz % reqs == 0
    tri = jnp.asarray(np.triu(np.ones((seq, seq), np.float32), 1), dtype=BF16)
    per_req = pl.BlockSpec((reqs, N_EXPERTS, seq), lambda b: (b, 0, 0))
    return pl.pallas_call(
        functools.partial(_route_kernel, seq, cap, reqs),
        grid=(bsz // reqs,),
        in_specs=[pl.BlockSpec((reqs * seq, LANES), lambda b: (b, 0)),
                  pl.BlockSpec((reqs * seq, D_MODEL), lambda b: (b, 0)),
                  pl.BlockSpec((seq, seq), lambda b: (0, 0), pipeline_mode=pl.Buffered(1))],
        out_specs=[pl.BlockSpec((N_EXPERTS, reqs * cap, D_MODEL), lambda b: (0, b, 0)), per_req, per_req],
        out_shape=[jax.ShapeDtypeStruct((N_EXPERTS, bsz * cap, D_MODEL), BF16),
                   jax.ShapeDtypeStruct((bsz, N_EXPERTS, seq), F32),
                   jax.ShapeDtypeStruct((bsz, N_EXPERTS, seq), F32)],
        scratch_shapes=[pltpu.VMEM((reqs, N_EXPERTS * cap, seq), BF16)],
        compiler_params=_cparams(1),
        name="route",
    )(probs, u2, tri)


def _expert_kernel(xc_ref, xd_ref, wg_ref, wu_ref, wd_ref, yc_ref, yd_ref):
    wg = wg_ref[0].astype(BF16)
    wu = wu_ref[0].astype(BF16)
    wd = wd_ref[0].astype(BF16)
    for x_ref, y_ref in ((xc_ref, yc_ref), (xd_ref, yd_ref)):
        rows = x_ref.shape[1]
        rb = min(rows, EXPERT_ROW_BLOCK)
        for r0 in range(0, rows, rb):
            x = x_ref[0, r0:r0 + rb, :]
            hg = jnp.dot(x, wg, preferred_element_type=F32)
            hu = jnp.dot(x, wu, preferred_element_type=F32)
            y = jnp.dot((_silu(hg) * hu).astype(BF16), wd, preferred_element_type=F32)
            y_ref[0, r0:r0 + rb, :] = y.astype(y_ref.dtype)


def _expert_call(xs_c, xs_d, w_gate, w_up, w_down):
    e_n, rc, d = xs_c.shape
    rd = xs_d.shape[1]
    ff = w_gate.shape[2]
    return pl.pallas_call(
        _expert_kernel,
        grid=(e_n,),
        in_specs=[pl.BlockSpec((1, rc, d), lambda e: (e, 0, 0)),
                  pl.BlockSpec((1, rd, d), lambda e: (e, 0, 0)),
                  pl.BlockSpec((1, d, ff), lambda e: (e, 0, 0)),
                  pl.BlockSpec((1, d, ff), lambda e: (e, 0, 0)),
                  pl.BlockSpec((1, ff, d), lambda e: (e, 0, 0))],
        out_specs=[pl.BlockSpec((1, rc, d), lambda e: (e, 0, 0)),
                   pl.BlockSpec((1, rd, d), lambda e: (e, 0, 0))],
        out_shape=[jax.ShapeDtypeStruct((e_n, rc, d), BF16), jax.ShapeDtypeStruct((e_n, rd, d), BF16)],
        compiler_params=_cparams(1),
        name="experts",
    )(xs_c, xs_d, w_gate, w_up, w_down)


def _final_kernel(seq, cap, reqs, x1_ref, y_ref, slot_ref, pt_ref, mod_ref, l2g, l2b, o_ref, w_s):
    jrow = _iota((cap, seq), 0).astype(F32)
    g2 = mod_ref[0, 5]
    for r in range(reqs):
        slot = slot_ref[r]
        pt = pt_ref[r]
        for e in range(N_EXPERTS):
            w_s[r, e * cap:(e + 1) * cap, :] = jnp.where(slot[e:e + 1, :] == jrow, pt[e:e + 1, :],
                                                         0.0).astype(BF16)
        y = y_ref[:, r * cap:(r + 1) * cap, :].reshape(N_EXPERTS * cap, y_ref.shape[-1])
        ffn = lax.dot_general(w_s[r], y, (((0,), (0,)), ((), ())), preferred_element_type=F32)
        rows = slice(r * seq, (r + 1) * seq)
        o_ref[rows, :] = _ln(ALPHA * x1_ref[rows, :].astype(F32) + g2 * ffn) * l2g[...] + l2b[...]


def _final_call(x1, ys, slot, pt, mod4, mod_row, l2g, l2b, bsz, seq, reqs):
    cap = EC_FACTOR * seq // N_EXPERTS
    d = D_MODEL
    per_req = pl.BlockSpec((reqs, N_EXPERTS, seq), lambda b: (b, 0, 0))
    return pl.pallas_call(
        functools.partial(_final_kernel, seq, cap, reqs),
        grid=(bsz // reqs,),
        in_specs=[pl.BlockSpec((reqs * seq, d), lambda b: (b, 0)),
                  pl.BlockSpec((N_EXPERTS, reqs * cap, d), lambda b: (0, b, 0)),
                  per_req, per_req,
                  _mod_spec(lambda b: mod_row(b * reqs * seq)),
                  pl.BlockSpec((1, d), lambda b: (0, 0)),
                  pl.BlockSpec((1, d), lambda b: (0, 0))],
        out_specs=pl.BlockSpec((reqs * seq, d), lambda b: (b, 0)),
        out_shape=jax.ShapeDtypeStruct((bsz * seq, d), F32),
        scratch_shapes=[pltpu.VMEM((reqs, N_EXPERTS * cap, seq), BF16)],
        compiler_params=_cparams(1),
        name="final",
    )(x1, ys, slot, pt, mod4, l2g, l2b)


def _grid_pos_embed(rows, dim):
    r = np.repeat(np.arange(rows), GRID_W)
    col = np.tile(np.arange(GRID_W), rows)
    quarter = dim // 4
    omega = 1.0 / (10000.0 ** (np.arange(quarter) / quarter))

    def enc(p):
        ang = p[:, None] * omega[None, :]
        return np.concatenate([np.sin(ang), np.cos(ang)], -1)

    return jnp.asarray(np.concatenate([enc(r), enc(col)], -1), dtype=F32)


def kernel(x_prompt, x_sample, state_delta_fwd, state_delta_bwd, c, c_ctx, w_mod, b_mod, w_in, conv_a_w, a_log, dt_bias, gnorm_w, conv_b_w, conv_b_b, hy_w1, hy_b1, hy_w2, hy_b2, hy_w3, hy_freq, hy_bias, w_up_a, w_up_b, w_out, ln1_g, ln1_b, w_router, w_e_gate, w_e_up, w_e_down, ln2_g, ln2_b):
    d = D_MODEL
    n_ctx, l_ctx, _ = x_prompt.shape
    n_dec, l_dec, _ = x_sample.shape
    assert w_mod.shape[0] == DEPTH == 1, "single-layer trunk"
    lyr = 0

    cond_t = jnp.pad(jnp.concatenate([c_ctx[None, :], c], 0).T, ((0, 0), (0, 7 - n_dec)))
    mod4 = _mod_call(cond_t, 1 + n_dec, w_mod[lyr], b_mod[lyr]).reshape(8, N_MOD, 1, d)

    in_w = w_in[lyr].astype(BF16)

    prm = jnp.pad(jnp.stack([a_log[lyr].reshape(-1), dt_bias[lyr].reshape(-1)]),
                  ((0, 6), (2 * DN_HEADS, LANES - 4 * DN_HEADS)))
    gn = gnorm_w[lyr].reshape(1, -1)

    wua = w_up_a[lyr].astype(BF16)
    wub = w_up_b[lyr].astype(BF16)
    wout_half = (0.5 * w_out[lyr]).astype(BF16)
    wr = jnp.pad(w_router[lyr], ((0, 0), (0, LANES - N_EXPERTS)))
    wr_hi = wr.astype(BF16)
    wr_lo = (wr - wr_hi.astype(F32)).astype(BF16)
    l1g, l1b = ln1_g[lyr].reshape(1, -1), ln1_b[lyr].reshape(1, -1)
    l2g, l2b = ln2_g[lyr].reshape(1, -1), ln2_b[lyr].reshape(1, -1)

    def front(x2d, pos, bsz, seq, mod_row, s0f, s0b):
        (qkv, z, ba, hy, gates), pre_conv = _inproj_call(x2d, pos, mod4, mod_row, in_w, conv_a_w[lyr],
                                                          conv_b_w[lyr], conv_b_b[lyr], seq, ACT_DT)
        reqs = DN_REQS_PER_STEP if seq == DN_CHUNK and bsz % DN_REQS_PER_STEP == 0 else 1
        o_a, s_f, s_b = _deltanet_call(qkv, z, ba, conv_a_w[lyr], prm, gn, s0f, s0b, bsz, seq, reqs,
                                       pre_conv, ACT_DT)
        fwd, inv = (jnp.asarray(m) for m in _dft_mats(seq))
        fwd_hi = fwd.astype(BF16)
        fwd_lo = (fwd - fwd_hi.astype(F32)).astype(BF16)
        filt_params = (hy_w1[lyr], hy_b1[lyr], hy_w2[lyr], hy_b2[lyr], hy_w3[lyr], hy_freq[lyr])
        o_h = _hyena_call(hy, pre_conv, conv_b_w[lyr], conv_b_b[lyr], filt_params, hy_bias[lyr],
                          fwd_hi, fwd_lo, inv.astype(BF16), bsz, seq, ACT_DT)
        x1, u2, probs = _mix_call(x2d, pos, o_a, o_h, gates, mod4, mod_row, wua, wub, wout_half, l1g, l1b,
                                   wr_hi, wr_lo)
        xs, slot, pt = _route_call(probs, u2, bsz, seq)
        return x1, xs, slot, pt, s_f, s_b

    row_ctx = lambda tok: 0
    row_dec = lambda tok: 1 + tok // l_dec

    xc = x_prompt.reshape(n_ctx * l_ctx, d)
    xd = x_sample.reshape(n_dec * l_dec, d)
    pos = _grid_pos_embed(l_dec // GRID_W, d)

    x1c, xsc, gc, ptc, s_f, s_b = front(xc, None, n_ctx, l_ctx, row_ctx, None, None)
    x1d, xsd, gd, ptd, _, _ = front(xd, pos, n_dec, l_dec, row_dec,
                                    state_delta_fwd[:, lyr], state_delta_bwd[:, lyr])
    yc, yd = _expert_call(xsc, xsd, w_e_gate[lyr], w_e_up[lyr], w_e_down[lyr])
    reqs_c = FINAL_CTX_REQS_PER_STEP if n_ctx % FINAL_CTX_REQS_PER_STEP == 0 else 1
    y_prompt = _final_call(x1c, yc, gc, ptc, mod4, row_ctx, l2g, l2b, n_ctx, l_ctx, reqs_c)
    y_sample = _final_call(x1d, yd, gd, ptd, mod4, row_dec, l2g, l2b, n_dec, l_dec, 1)

    return (y_prompt.reshape(n_ctx, l_ctx, d), y_sample.reshape(n_dec, l_dec, d),
            s_f[:, None], s_b[:, None])
```

```python
import functools
import math

import jax
import jax.numpy as jnp
import numpy as np
from jax import lax
from jax.experimental import pallas as pl
from jax.experimental.pallas import tpu as pltpu

F32 = jnp.float32
BF16 = jnp.bfloat16
HIGHEST = lax.Precision.HIGHEST

D_MODEL = 1024
DEPTH = 1
GRID_W = 64
DN_HEADS = 4
DN_DK = 128
DN_DV = 128
DN_WIDTH = DN_HEADS * DN_DV
HY_WIDTH = D_MODEL // 2
HY_EMB = 33
HY_FFN = 64
HY_TARGET = 1e-2
HY_FAST_DECAY = 0.3
HY_SLOW_DECAY = 1.5
N_EXPERTS = 16
EC_FACTOR = 2
N_MOD = 6
ALPHA = (2 * DEPTH) ** 0.25
LN_EPS = 1e-5
RMS_EPS = 1e-6
QKV_WIDTH = 2 * DN_HEADS * DN_DK + DN_WIDTH

LANES = 128
INPROJ_TILE = 1024
INPROJ_BLOCK_ROWS = 512
MIX_TILE = 1024
MIX_SUB_ROWS = 256
MIX_MIN_STEPS = 4
MOD_COL_BLOCK = 2048
DN_CHUNK = 256
INV_BASE = 8
DN_REQS_PER_STEP = 2
HY_ROWS_PER_STEP = 1024
FINAL_CTX_REQS_PER_STEP = 4
EXPERT_ROW_BLOCK = 512
VMEM_LIMIT = 56 * 1024 * 1024
MIN_NORMAL_F32_BITS = 0x00800000
ACT_DT = BF16
ROUTE_ROWS_PER_STEP = 1024
RANK_COUNT_MAX_SEQ = 256


def _cparams(n_axes):
    return pltpu.CompilerParams(dimension_semantics=("arbitrary",) * n_axes,
                                vmem_limit_bytes=VMEM_LIMIT)


def _bmm(a, b):
    return jnp.einsum("bmk,bkn->bmn", a.astype(BF16), b.astype(BF16), preferred_element_type=F32)


def _bmm_nt(a, b):
    return jnp.einsum("bmk,bnk->bmn", a.astype(BF16), b.astype(BF16), preferred_element_type=F32)


def _bmm_tn(a, b):
    return jnp.einsum("bkm,bkn->bmn", a.astype(BF16), b.astype(BF16), preferred_element_type=F32)


def _mm_f32(a, b):
    return jnp.dot(a, b, precision=HIGHEST, preferred_element_type=F32)


def _sigmoid(x):
    return 0.5 * jnp.tanh(0.5 * x) + 0.5


def _silu(x):
    h = 0.5 * x
    return h * (jnp.tanh(h) + 1.0)


def _softplus(x):
    return jnp.maximum(x, 0.0) + jnp.log1p(jnp.exp(-jnp.abs(x)))


def _ln(x):
    mu = jnp.mean(x, axis=-1, keepdims=True)
    xc = x - mu
    var = jnp.mean(xc * xc, axis=-1, keepdims=True)
    return xc * lax.rsqrt(var + LN_EPS)


def _l2norm(x, scale=1.0):
    return x * (lax.rsqrt(jnp.sum(x * x, axis=-1, keepdims=True) + 1e-6) * scale)


def _iota(shape, dim):
    return lax.broadcasted_iota(jnp.int32, shape, dim)


def _short_conv3(x, w, first, last):
    xm = jnp.where(first, 0.0, pltpu.roll(x, 1, 0))
    xp = jnp.where(last, 0.0, pltpu.roll(x, x.shape[0] - 1, 0))
    return w[0:1, :] * xm + w[1:2, :] * x + w[2:3, :] * xp


def _mod_kernel(n_rows, ct_ref, w_ref, b_ref, o_ref):
    s = _silu(ct_ref[...])
    w = w_ref[...]
    b = b_ref[...]
    o_ref[...] = jnp.zeros(o_ref.shape, F32)
    for r in range(n_rows):
        o_ref[r:r + 1, :] = jnp.sum(w * s[:, r:r + 1], axis=0, keepdims=True) + b


def _mod_call(cond_t, n_rows, w_mod, b_mod):
    d = D_MODEL
    tn = MOD_COL_BLOCK
    return pl.pallas_call(
        functools.partial(_mod_kernel, n_rows),
        grid=(N_MOD * d // tn,),
        in_specs=[pl.BlockSpec((d, 8), lambda j: (0, 0)),
                  pl.BlockSpec((d, tn), lambda j: (0, j)),
                  pl.BlockSpec((1, tn), lambda j: (0, j))],
        out_specs=pl.BlockSpec((8, tn), lambda j: (0, j)),
        out_shape=jax.ShapeDtypeStruct((8, N_MOD * d), F32),
        compiler_params=_cparams(1),
        name="mod",
    )(cond_t, w_mod, b_mod.reshape(1, -1))


def _mod_spec(row_of_step):
    return pl.BlockSpec((1, N_MOD, 1, D_MODEL), lambda *idx: (row_of_step(*idx), 0, 0, 0))


def _inproj_kernel(has_pos, conv_seq, *refs):
    refs = list(refs)
    x_ref = refs.pop(0)
    pos_ref = refs.pop(0) if has_pos else None
    mod_ref, w_ref = refs.pop(0), refs.pop(0)
    if conv_seq:
        cwa_ref, cwb_ref, cbb_ref = refs.pop(0), refs.pop(0), refs.pop(0)
    qkv_o, z_o, ba_o, hy_o, g_o = refs
    sh1 = mod_ref[0, 0]
    sc1 = mod_ref[0, 1]
    o_z = QKV_WIDTH
    o_ba = o_z + DN_WIDTH
    n_ba = 4 * DN_HEADS
    o_g = n_ba + 3 * HY_WIDTH
    lanes = lambda j: slice(j * LANES, (j + 1) * LANES)
    tm = x_ref.shape[0]
    sub = min(tm, INPROJ_BLOCK_ROWS)
    blocks = [slice(r0, r0 + sub) for r0 in range(0, tm, sub)]

    if conv_seq:
        row = _iota((sub, LANES), 0) % conv_seq
        first = row == 0
        last = row == conv_seq - 1

    def project(rows):
        x = x_ref[rows, :]
        if has_pos:
            x = x + pos_ref[rows, :]
        u = (_ln(x) * (1.0 + sc1) + sh1).astype(BF16)
        return (jnp.dot(u, w_ref[:, :o_ba], preferred_element_type=F32),
                jnp.dot(u, w_ref[:, o_ba:], preferred_element_type=F32))

    def epilogue(rows, head, tail):
        z_o[rows, :] = head[:, o_z:].astype(z_o.dtype)
        ba_o[rows, :] = tail[:, :LANES]
        g_o[rows, :] = tail[:, o_g:].astype(g_o.dtype)
        hy = tail[:, n_ba:o_g]
        if not conv_seq:
            qkv_o[rows, :] = head[:, :o_z].astype(qkv_o.dtype)
            hy_o[rows, :] = hy.astype(hy_o.dtype)
            return
        for j in range(QKV_WIDTH // LANES):
            y = _silu(_short_conv3(head[:, lanes(j)], cwa_ref[:, lanes(j)], first, last))
            if j < 2 * DN_HEADS:
                y = _l2norm(y, DN_DK ** -0.5 if j < DN_HEADS else 1.0)
            qkv_o[rows, lanes(j)] = y.astype(qkv_o.dtype)
        nj = HY_WIDTH // LANES
        for j in range(nj):
            parts = [_short_conv3(hy[:, lanes(p * nj + j)], cwb_ref[:, lanes(p * nj + j)], first, last)
                     + cbb_ref[:, lanes(p * nj + j)] for p in range(3)]
            hy_o[rows, lanes(j)] = parts[0].astype(hy_o.dtype)
            hy_o[rows, lanes(nj + j)] = (parts[1] * parts[2]).astype(hy_o.dtype)

    pending = None
    for rows in blocks:
        cur = (rows, *project(rows))
        if pending is not None:
            epilogue(*pending)
        pending = cur
    epilogue(*pending)


def _inproj_call(x2d, pos, mod4, mod_row, w, conv_a_w, conv_b_w, conv_b_b, seq, act_dt):
    n, d = x2d.shape
    fuse = seq <= INPROJ_BLOCK_ROWS and INPROJ_TILE % seq == 0
    tm = min(INPROJ_TILE if fuse else INPROJ_BLOCK_ROWS, n)
    assert n % tm == 0 and (not fuse or tm % seq == 0)
    conv_seq = seq if fuse else None
    row = lambda i: (i, 0)
    const = lambda i: (0, 0)
    in_specs = [pl.BlockSpec((tm, d), row)]
    args = [x2d]
    if pos is not None:
        tiles = pos.shape[0] // tm
        in_specs.append(pl.BlockSpec((tm, d), lambda i: (i % tiles, 0)))
        args.append(pos)
    in_specs.append(_mod_spec(lambda i: mod_row(i * tm)))
    args.append(mod4)
    in_specs.append(pl.BlockSpec(w.shape, const, pipeline_mode=pl.Buffered(1)))
    args.append(w)
    if conv_seq:
        for a in (conv_a_w, conv_b_w, conv_b_b.reshape(1, -1)):
            in_specs.append(pl.BlockSpec(a.shape, const))
            args.append(a)
    hy_width = (2 if conv_seq else 3) * HY_WIDTH
    widths = (QKV_WIDTH, DN_WIDTH, LANES, hy_width, 2 * D_MODEL)
    dts = (act_dt, act_dt, F32, act_dt, act_dt)
    outs = pl.pallas_call(
        functools.partial(_inproj_kernel, pos is not None, conv_seq),
        grid=(n // tm,),
        in_specs=in_specs,
        out_specs=[pl.BlockSpec((tm, w), row) for w in widths],
        out_shape=[jax.ShapeDtypeStruct((n, w), dt) for w, dt in zip(widths, dts)],
        compiler_params=_cparams(1),
        name="inproj",
    )(*args)
    return outs, conv_seq is not None


def _inv_unit_tri(lm, ri, ci, block):
    c = lm.shape[-1]

    def same_block(h):
        sh = h.bit_length() - 1
        return (ri >> sh) == (ci >> sh)

    def unfold(f, h):
        return jnp.where(same_block(h), jnp.concatenate([f] * (c // h), axis=1), 0.0)

    base = INV_BASE
    n_nat = jnp.where(same_block(base), -lm, 0.0)
    n_fold = n_nat[:, 0:base, :]
    for i in range(1, c // base):
        n_fold = n_fold + n_nat[:, i * base:(i + 1) * base, :]
    eye_fold = jnp.where((_iota((base, c), 1) & (base - 1)) == _iota((base, c), 0), 1.0, 0.0)
    t = eye_fold + n_fold
    p_fold, p_nat = n_fold, n_nat
    span = 2
    while span < base:
        p_fold = _bmm(p_fold, p_nat)
        p_nat = unfold(p_fold, base)
        t = t + _bmm(t, p_nat)
        span *= 2
    s = base
    while s < block:
        sh = s.bit_length() - 1
        t_nat = unfold(t, s)
        even = ((_iota((s, c), 1) >> sh) & 1) == 0
        t2 = jnp.concatenate([jnp.where(even, t, 0.0), jnp.where(even, 0.0, t)], axis=1)
        lo = jnp.where(same_block(2 * s) & ~same_block(s), lm, 0.0)
        t = t2 - _bmm(_bmm(t2, lo), t_nat)
        s *= 2
    return t


def _dn_chunk(q, k, v, gc_col, gc_row, beta, e_col, e_rest, e_all, s_prev, same_chunk, ri, ci):
    c = DN_CHUNK
    nh = q.shape[0] // 2
    half = c // 2
    if same_chunk:
        gram = _bmm_nt(k[:nh], k[:nh])
        qk = _bmm_nt(q[:nh], k[:nh])
        gram = jnp.concatenate([gram, gram], 0)
        qk = jnp.concatenate([qk, qk], 0)
    else:
        gram = _bmm_nt(k, k)
        qk = _bmm_nt(q, k)
    diff = gc_col - gc_row
    incl = jnp.concatenate([jnp.broadcast_to(ri >= ci, (nh, c, c)), jnp.broadcast_to(ri <= ci, (nh, c, c))], 0)
    strict = incl & (ri != ci)
    decay = jnp.exp(jnp.where(incl, diff, -jnp.inf))
    lm = jnp.where(strict, beta * gram * decay, 0.0)
    a_intra = qk * decay
    t = _inv_unit_tri(lm, ri, ci, half)
    t1 = t[:, :, :half]
    t2 = t[:, :, half:]
    rhs = jnp.concatenate([v * beta, k * (beta * e_col)], axis=2)
    y1 = _bmm(t1, rhs[:, :half, :])
    y2 = _bmm(t2, rhs[:, half:, :])
    c_f = _bmm(lm[:nh, half:, :half], y1[:nh])
    c_b = _bmm(lm[nh:, :half, half:], y2[nh:])
    c_f = _bmm(t2[:nh], c_f)
    c_b = _bmm(t1[nh:], c_b)
    sol = jnp.concatenate([jnp.concatenate([y1[:nh], y2[:nh] - c_f], 1),
                           jnp.concatenate([y1[nh:] - c_b, y2[nh:]], 1)], 0)
    u = sol[:, :, :DN_DV]
    w = sol[:, :, DN_DV:]
    ks = k * e_rest
    if s_prev is None:
        v_new = u
        o = _bmm(a_intra, v_new)
        s_new = _bmm_tn(ks, v_new)
    else:
        v_new = u - _bmm(w, s_prev)
        o = _bmm(q * e_col, s_prev) + _bmm(a_intra, v_new)
        s_new = s_prev * e_all + _bmm_tn(ks, v_new)
    return o, s_new


def _deltanet_kernel(seq, reqs, zero_init, pre_conv, *refs):
    qkv_ref, cw_ref, z_ref, ba_ref, prm_ref, gn_ref = refs[:6]
    refs = refs[6:]
    if not zero_init:
        s0f_ref, s0b_ref = refs[:2]
        refs = refs[2:]
    o_ref, sf_ref, sb_ref, q_s, k_s, v_s, g_s, b_s, o_s, st_s = refs
    c = DN_CHUNK
    n = seq // c
    nh = DN_HEADS
    nf = reqs * nh
    lanes = lambda j: slice(j * LANES, (j + 1) * LANES)

    if not pre_conv:
        row = _iota((seq, LANES), 0)
        first_row = row == 0
        last_row = row == seq - 1
    for r in range(reqs):
        rows = slice(r * seq, (r + 1) * seq)

        def qkv_part(j):
            x = qkv_ref[rows, lanes(j)].astype(F32)
            return x if pre_conv else _silu(_short_conv3(x, cw_ref[:, lanes(j)], first_row, last_row))

        for h in range(nh):
            q = qkv_part(h)
            k = qkv_part(nh + h)
            q_s[r * nh + h] = q if pre_conv else _l2norm(q, DN_DK ** -0.5)
            k_s[r * nh + h] = k if pre_conv else _l2norm(k)
            v_s[r * nh + h] = qkv_part(2 * nh + h)

    ba = ba_ref[...]
    b_s[...] = _sigmoid(ba)
    g_s[...] = -jnp.exp(prm_ref[0:1, :]) * _softplus(ba + prm_ref[1:2, :])

    ri = _iota((c, c), 0)
    ci = _iota((c, c), 1)
    tri_l = jnp.where(ri >= ci, 1.0, 0.0).astype(F32)
    tri_u = jnp.where(ri <= ci, 1.0, 0.0).astype(F32)

    def chunk_pair(idx_f, idx_b, s_prev):
        def ds(idx, base=0):
            start = idx * c
            start = start if isinstance(start, int) else pl.multiple_of(start, c)
            return pl.ds(base + start, c)

        col_f, col_b, row_f, row_b, beta_f, beta_b = [], [], [], [], [], []
        for r in range(reqs):
            cs_f = _mm_f32(tri_l, g_s[ds(idx_f, r * seq), :])
            cs_b = _mm_f32(tri_u, g_s[ds(idx_b, r * seq), :])
            cst_f = cs_f.T
            cst_b = cs_b.T
            b_f = b_s[ds(idx_f, r * seq), :]
            b_b = b_s[ds(idx_b, r * seq), :]
            for h in range(nh):
                jf, jb = 2 * nh + h, 3 * nh + h
                col_f.append(cs_f[:, jf:jf + 1])
                col_b.append(cs_b[:, jb:jb + 1])
                row_f.append(cst_f[jf:jf + 1, :])
                row_b.append(cst_b[jb:jb + 1, :])
                beta_f.append(b_f[:, h:h + 1])
                beta_b.append(b_b[:, nh + h:nh + h + 1])
        gc_col = jnp.stack(col_f + col_b, 0)
        gc_row = jnp.stack(row_f + row_b, 0)
        beta = jnp.stack(beta_f + beta_b, 0)
        e_col = jnp.exp(gc_col)
        gl = jnp.concatenate([gc_col[:nf, c - 1:c, :], gc_col[nf:, 0:1, :]], 0)
        e_rest = jnp.exp(gl - gc_col)
        e_all = jnp.exp(gl)
        sl_f, sl_b = ds(idx_f), ds(idx_b)
        both = lambda ref: jnp.concatenate([ref[:, sl_f, :], ref[:, sl_b, :]], 0)
        o, s_new = _dn_chunk(both(q_s), both(k_s), both(v_s), gc_col, gc_row, beta, e_col, e_rest, e_all,
                             s_prev, n == 1, ri, ci)
        o_s[:nf, sl_f, :] = o[:nf]
        o_s[nf:, sl_b, :] = o[nf:]
        return s_new

    if not zero_init:
        s0 = jnp.concatenate([s0f_ref[r] for r in range(reqs)] + [s0b_ref[r] for r in range(reqs)], 0)
    if n == 1:
        s_fin = chunk_pair(0, 0, None if zero_init else s0)
    else:
        st_s[...] = jnp.zeros(st_s.shape, F32) if zero_init else s0

        def body(i, carry):
            st_s[...] = chunk_pair(i, n - 1 - i, st_s[...])
            return carry

        lax.fori_loop(0, n, body, 0, unroll=True)
        s_fin = st_s[...]
    for r in range(reqs):
        sf_ref[r] = s_fin[r * nh:(r + 1) * nh]
        sb_ref[r] = s_fin[nf + r * nh:nf + (r + 1) * nh]

    gn = gn_ref[...]
    for r in range(reqs):
        rows = slice(r * seq, (r + 1) * seq)
        for h in range(nh):
            o = o_s[r * nh + h] + o_s[nf + r * nh + h]
            o = o * lax.rsqrt(jnp.mean(o * o, axis=-1, keepdims=True) + RMS_EPS) * gn
            o_ref[rows, lanes(h)] = (o * _silu(z_ref[rows, lanes(h)].astype(F32))).astype(o_ref.dtype)


def _deltanet_call(qkv, z, ba, conv_w, prm, gn, s0f, s0b, bsz, seq, reqs, pre_conv, act_dt):
    nh = DN_HEADS
    zero_init = s0f is None
    rows = reqs * seq
    st = pl.BlockSpec((reqs, nh, DN_DK, DN_DV), lambda b: (b, 0, 0, 0))
    in_specs = [pl.BlockSpec((rows, QKV_WIDTH), lambda b: (b, 0)),
                pl.BlockSpec((3, QKV_WIDTH), lambda b: (0, 0)),
                pl.BlockSpec((rows, DN_WIDTH), lambda b: (b, 0)),
                pl.BlockSpec((rows, LANES), lambda b: (b, 0)),
                pl.BlockSpec((8, LANES), lambda b: (0, 0)),
                pl.BlockSpec((1, LANES), lambda b: (0, 0))]
    args = [qkv, conv_w, z, ba, prm, gn]
    if not zero_init:
        in_specs += [st, st]
        args += [s0f, s0b]
    vm = lambda shape: pltpu.VMEM(shape, F32)
    per_head = vm((reqs * nh, seq, LANES))
    return pl.pallas_call(
        functools.partial(_deltanet_kernel, seq, reqs, zero_init, pre_conv),
        grid=(bsz // reqs,),
        in_specs=in_specs,
        out_specs=[pl.BlockSpec((rows, DN_WIDTH), lambda b: (b, 0)), st, st],
        out_shape=[jax.ShapeDtypeStruct((bsz * seq, DN_WIDTH), act_dt),
                   jax.ShapeDtypeStruct((bsz, nh, DN_DK, DN_DV), F32),
                   jax.ShapeDtypeStruct((bsz, nh, DN_DK, DN_DV), F32)],
        scratch_shapes=[per_head, per_head, per_head, vm((rows, LANES)), vm((rows, LANES)),
                        vm((2 * reqs * nh, seq, LANES)), vm((2 * reqs * nh, DN_DK, DN_DV))],
        compiler_params=_cparams(1),
        name="deltanet",
    )(*args)


def _dft_mats(seq):
    n = 2 * seq
    f = np.arange(seq)[:, None]
    s = np.arange(seq)[None, :]
    ang = 2.0 * np.pi * ((f * s) % n) / n
    fwd = np.concatenate([np.cos(ang), -np.sin(ang)], axis=0)
    fwd[seq, :] = np.cos(np.pi * np.arange(seq))
    t = (np.arange(seq) + seq // 2)[:, None]
    ff = np.arange(seq)[None, :]
    ang2 = 2.0 * np.pi * ((t * ff) % n) / n
    inv_r = 2.0 * np.cos(ang2) / n
    inv_i = -2.0 * np.sin(ang2) / n
    inv_r[:, 0] = 1.0 / n
    inv_i[:, 0] = np.cos(np.pi * t[:, 0]) / n
    inv = np.concatenate([inv_r, inv_i], axis=1)
    return fwd.astype(np.float32), inv.astype(np.float32)


def _filter_feats(seq):
    t = np.linspace(0.0, 1.0, seq)[:, None]
    bands = (HY_EMB - 1) // 2
    ang = (2.0 * math.pi * np.arange(seq) / seq)[:, None] * np.linspace(1e-4, bands - 1, bands)[None, :]
    feats = np.concatenate([t, np.cos(ang), -np.sin(ang)], -1)
    deltas = np.abs(np.linspace(math.log(HY_TARGET) / HY_SLOW_DECAY, math.log(HY_TARGET) / HY_FAST_DECAY,
                                HY_WIDTH))
    offset = np.abs(np.arange(seq) - seq // 2) / (seq // 2)
    window = np.exp(-offset[:, None] * deltas[None, :])
    feats = np.pad(feats, ((0, 0), (0, LANES - HY_EMB)))
    return jnp.asarray(feats, dtype=F32), jnp.asarray(window, dtype=F32)


def _filter_spectrum(seq, feats_ref, win_ref, w1_ref, b1_ref, w2_ref, b2_ref, w3_ref, fr_ref, fwd_hi_ref,
                     fwd_lo_ref, ha_s, hb_s, hd_s):
    fr = fr_ref[...]
    hid = jnp.sin(fr * (_mm_f32(feats_ref[...], w1_ref[...]) + b1_ref[...]))
    hid = jnp.sin(fr * (_mm_f32(hid, w2_ref[...]) + b2_ref[...]))
    filt = _mm_f32(hid, w3_ref[...]) * win_ref[...]
    filt = filt / (jnp.sum(jnp.abs(filt), axis=0, keepdims=True) + 1e-6)
    filt_hi = filt.astype(BF16)
    filt_lo = (filt - filt_hi.astype(F32)).astype(BF16)
    fwd_hi = fwd_hi_ref[...]
    spec = (jnp.dot(fwd_hi, filt_hi, preferred_element_type=F32)
            + jnp.dot(fwd_hi, filt_lo, preferred_element_type=F32)
            + jnp.dot(fwd_lo_ref[...], filt_hi, preferred_element_type=F32))
    h_re = spec[:seq, :]
    h_im = spec[seq:, :]
    first = _iota(h_re.shape, 0) == 0
    ha_s[...] = h_re
    hb_s[...] = jnp.where(first, 0.0, h_im)
    hd_s[...] = jnp.where(first, h_im, h_re)


def _hyena_kernel(seq, reqs, pre_conv, *refs):
    refs = list(refs)
    n_tok = 2 if pre_conv else 9
    tok_refs, refs = refs[:n_tok], refs[n_tok:]
    filt_refs, refs = refs[:8], refs[8:]
    skip_ref, fwd_ref, fwd_lo_ref, inv_ref, o_ref, ha_s, hb_s, hd_s = refs

    @pl.when(pl.program_id(0) == 0)
    def _():
        _filter_spectrum(seq, *filt_refs, fwd_ref, fwd_lo_ref, ha_s, hb_s, hd_s)

    if pre_conv:
        x0_ref, uu_ref = tok_refs
    else:
        x0_ref, x1_ref, v_ref, c0_ref, c1_ref, c2_ref, b0_ref, b1_ref, b2_ref = tok_refs
        row = _iota((seq, HY_WIDTH), 0)
        first = row == 0
        last = row == seq - 1
    hb = hb_s[...]
    for r in range(reqs):
        rows = slice(r * seq, (r + 1) * seq)
        if pre_conv:
            x0 = x0_ref[rows, :].astype(F32)
            uu = uu_ref[rows, :].astype(F32)
        else:
            conv = lambda x_ref, w_ref, b_ref: (_short_conv3(x_ref[rows, :].astype(F32), w_ref[...], first, last)
                                                + b_ref[...])
            x0 = conv(x0_ref, c0_ref, b0_ref)
            uu = conv(x1_ref, c1_ref, b1_ref) * conv(v_ref, c2_ref, b2_ref)
        spec = jnp.dot(fwd_ref[...], uu.astype(BF16), preferred_element_type=F32)
        u_re = spec[:seq, :]
        u_im = spec[seq:, :]
        y_re = u_re * ha_s[...] - u_im * hb
        y_im = u_re * hb + u_im * hd_s[...]
        y = jnp.concatenate([y_re, y_im], axis=0).astype(BF16)
        cv = jnp.dot(inv_ref[...], y, preferred_element_type=F32)
        o_ref[rows, :] = (x0 * (cv + uu * skip_ref[...])).astype(o_ref.dtype)


def _hyena_call(hy, pre_conv, conv_w, conv_b, filt_params, skip, fwd_hi, fwd_lo, inv, bsz, seq, act_dt):
    w1, b1, w2, b2, w3, freq = filt_params
    cb = HY_WIDTH
    reqs = max(1, min(bsz, HY_ROWS_PER_STEP // seq))
    assert bsz % reqs == 0
    tok = lambda part: pl.BlockSpec((reqs * seq, cb), lambda b: (b, part))
    const = lambda a: pl.BlockSpec(a.shape, lambda b: (0,) * a.ndim, pipeline_mode=pl.Buffered(1))
    feats, window = _filter_feats(seq)
    w1p = jnp.pad(w1, ((0, LANES - HY_EMB), (0, 0)))
    filt_args = [feats, window, w1p, b1.reshape(1, -1), w2, b2.reshape(1, -1), w3, freq.reshape(1, -1)]
    tail_args = filt_args + [skip.reshape(1, -1), fwd_hi, fwd_lo, inv]
    tail_specs = [const(a) for a in tail_args]
    if pre_conv:
        in_specs = [tok(0), tok(1)] + tail_specs
        args = [hy, hy] + tail_args
    else:
        conv_b2 = conv_b.reshape(1, -1)
        part3 = lambda a, part: pl.BlockSpec((a.shape[0], cb), lambda b: (0, part))
        in_specs = ([tok(0), tok(1), tok(2)] + [part3(conv_w, p) for p in range(3)]
                    + [part3(conv_b2, p) for p in range(3)] + tail_specs)
        args = [hy, hy, hy, conv_w, conv_w, conv_w, conv_b2, conv_b2, conv_b2] + tail_args
    spec_buf = pltpu.VMEM((seq, HY_WIDTH), F32)
    return pl.pallas_call(
        functools.partial(_hyena_kernel, seq, reqs, pre_conv),
        grid=(bsz // reqs,),
        in_specs=in_specs,
        out_specs=tok(0),
        out_shape=jax.ShapeDtypeStruct((bsz * seq, HY_WIDTH), act_dt),
        scratch_shapes=[spec_buf, spec_buf, spec_buf],
        compiler_params=_cparams(1),
        name="hyena",
    )(*args)


def _mix_kernel(has_pos, *refs):
    refs = list(refs)
    x_ref = refs.pop(0)
    pos_ref = refs.pop(0) if has_pos else None
    (oa_ref, ob_ref, g_ref, mod_ref, wua, wub, wout_half, l1g, l1b, wr_hi, wr_lo, x1_o, u2_o, p_o) = refs
    d = D_MODEL
    tm = x_ref.shape[0]
    sub = min(tm, MIX_SUB_ROWS)
    blocks = [slice(r0, r0 + sub) for r0 in range(0, tm, sub)]
    g1 = mod_ref[0, 2]
    sh2 = mod_ref[0, 3]
    sc2 = mod_ref[0, 4]

    mixed = []
    for rows in blocks:
        th = jnp.tanh(0.5 * g_ref[rows, :].astype(F32))
        a = jnp.dot(oa_ref[rows, :], wua[...], preferred_element_type=F32)
        b = jnp.dot(ob_ref[rows, :], wub[...], preferred_element_type=F32)
        up2 = (th[:, :d] * a + a) + (th[:, d:] * b + b)
        mixed.append(jnp.dot(up2.astype(BF16), wout_half[...], preferred_element_type=F32))

    for rows, mx in zip(blocks, mixed):
        x = x_ref[rows, :]
        if has_pos:
            x = x + pos_ref[rows, :]
        x1 = _ln(ALPHA * x + g1 * mx) * l1g[...] + l1b[...]
        x1_o[rows, :] = x1.astype(x1_o.dtype)
        u2 = _ln(x1) * (1.0 + sc2) + sh2
        u2_hi = u2.astype(BF16)
        u2_o[rows, :] = u2_hi
        u2_lo = (u2 - u2_hi.astype(F32)).astype(BF16)
        logits = (jnp.dot(u2_hi, wr_hi[...], preferred_element_type=F32)
                  + jnp.dot(u2_hi, wr_lo[...], preferred_element_type=F32)
                  + jnp.dot(u2_lo, wr_hi[...], preferred_element_type=F32))
        lane = _iota(logits.shape, 1)
        logits = jnp.where(lane < N_EXPERTS, logits, -jnp.inf)
        m = jnp.max(logits, axis=-1, keepdims=True)
        e = jnp.exp(logits - m)
        p_o[rows, :] = e / jnp.sum(e, axis=-1, keepdims=True)


def _mix_call(x2d, pos, o_a, o_b, gates, mod4, mod_row, wua, wub, wout_half, l1g, l1b, wr_hi, wr_lo):
    n, d = x2d.shape
    tm = min(MIX_TILE, max(n // MIX_MIN_STEPS, MIX_SUB_ROWS))
    assert n % tm == 0
    row = lambda i: (i, 0)
    const = lambda i: (0, 0)
    in_specs = [pl.BlockSpec((tm, d), row)]
    args = [x2d]
    if pos is not None:
        tiles = pos.shape[0] // tm
        in_specs.append(pl.BlockSpec((tm, d), lambda i: (i % tiles, 0)))
        args.append(pos)
    in_specs += [pl.BlockSpec((tm, DN_WIDTH), row), pl.BlockSpec((tm, HY_WIDTH), row),
                 pl.BlockSpec((tm, 2 * d), row),
                 _mod_spec(lambda i: mod_row(i * tm))]
    args += [o_a, o_b, gates, mod4]
    for w in (wua, wub, wout_half, l1g, l1b, wr_hi, wr_lo):
        in_specs.append(pl.BlockSpec(w.shape, const, pipeline_mode=pl.Buffered(1)))
        args.append(w)
    return pl.pallas_call(
        functools.partial(_mix_kernel, pos is not None),
        grid=(n // tm,),
        in_specs=in_specs,
        out_specs=[pl.BlockSpec((tm, d), row), pl.BlockSpec((tm, d), row), pl.BlockSpec((tm, LANES), row)],
        out_shape=[jax.ShapeDtypeStruct((n, d), ACT_DT), jax.ShapeDtypeStruct((n, d), BF16),
                   jax.ShapeDtypeStruct((n, LANES), F32)],
        compiler_params=_cparams(1),
        name="mix",
    )(*args)


def _route_kernel(seq, cap, reqs, p_ref, u_ref, tri_ref, xs_ref, slot_ref, pt_ref, g_s):
    e_n = N_EXPERTS
    tri = tri_ref[...]
    jrow = _iota((cap, seq), 0).astype(F32)
    if seq <= RANK_COUNT_MAX_SEQ:
        earlier = _iota((seq, seq), 0) < _iota((seq, seq), 1)
    for r in range(reqs):
        rows = slice(r * seq, (r + 1) * seq)
        p = p_ref[rows, :]
        pt = p.T[:e_n, :]

        if seq <= RANK_COUNT_MAX_SEQ:
            ranks = []
            for e in range(e_n):
                pc = p[:, e:e + 1]
                pr = pt[e:e + 1, :]
                beats = (pc > pr) | (earlier & (pc == pr))
                ranks.append(jnp.sum(jnp.where(beats, 1.0, 0.0), axis=0, keepdims=True))
            sel = jnp.concatenate(ranks, axis=0) < cap
        else:
            def ok(cand):
                cnt = jnp.sum(jnp.where(pt >= pltpu.bitcast(cand, F32), 1.0, 0.0), axis=1, keepdims=True)
                return (cnt >= cap) & (cand >= MIN_NORMAL_F32_BITS)

            def search(i, cur):
                lo = 28 - 2 * i
                c1, c2, c3 = cur | (1 << lo), cur | (2 << lo), cur | (3 << lo)
                return jnp.where(ok(c3), c3, jnp.where(ok(c2), c2, jnp.where(ok(c1), c1, cur)))

            top = jnp.full((e_n, 1), 1 << 30, jnp.int32)
            cur = jnp.where(ok(top), top, 0)
            thr = pltpu.bitcast(lax.fori_loop(0, 15, search, cur), F32)
            gt = pt > thr
            eq = pt == thr
            n_gt = jnp.sum(jnp.where(gt, 1.0, 0.0), axis=1, keepdims=True)
            eq_rank = jnp.dot(jnp.where(eq, 1.0, 0.0).astype(BF16), tri, preferred_element_type=F32)
            sel = gt | (eq & (eq_rank < cap - n_gt))
        pos = jnp.dot(jnp.where(sel, 1.0, 0.0).astype(BF16), tri, preferred_element_type=F32)
        slot = jnp.where(sel, pos, -1.0)
        slot_ref[r] = slot
        pt_ref[r] = pt
        for e in range(e_n):
            g_s[r, e * cap:(e + 1) * cap, :] = jnp.where(slot[e:e + 1, :] == jrow, 1.0, 0.0).astype(BF16)
        xs = jnp.dot(g_s[r], u_ref[rows, :], preferred_element_type=F32)
        for e in range(e_n):
            xs_ref[e, r * cap:(r + 1) * cap, :] = xs[e * cap:(e + 1) * cap, :].astype(xs_ref.dtype)


def _route_call(probs, u2, bsz, seq):
    cap = EC_FACTOR * seq // N_EXPERTS
    reqs = max(1, min(bsz, ROUTE_ROWS_PER_STEP // seq))
    assert bsz % reqs == 0
    tri = jnp.asarray(np.triu(np.ones((seq, seq), np.float32), 1), dtype=BF16)
    per_req = pl.BlockSpec((reqs, N_EXPERTS, seq), lambda b: (b, 0, 0))
    return pl.pallas_call(
        functools.partial(_route_kernel, seq, cap, reqs),
        grid=(bsz // reqs,),
        in_specs=[pl.BlockSpec((reqs * seq, LANES), lambda b: (b, 0)),
                  pl.BlockSpec((reqs * seq, D_MODEL), lambda b: (b, 0)),
                  pl.BlockSpec((seq, seq), lambda b: (0, 0), pipeline_mode=pl.Buffered(1))],
        out_specs=[pl.BlockSpec((N_EXPERTS, reqs * cap, D_MODEL), lambda b: (0, b, 0)), per_req, per_req],
        out_shape=[jax.ShapeDtypeStruct((N_EXPERTS, bsz * cap, D_MODEL), BF16),
                   jax.ShapeDtypeStruct((bsz, N_EXPERTS, seq), F32),
                   jax.ShapeDtypeStruct((bsz, N_EXPERTS, seq), F32)],
        scratch_shapes=[pltpu.VMEM((reqs, N_EXPERTS * cap, seq), BF16)],
        compiler_params=_cparams(1),
        name="route",
    )(probs, u2, tri)


def _expert_kernel(xc_ref, xd_ref, wg_ref, wu_ref, wd_ref, yc_ref, yd_ref):
    wg = wg_ref[0].astype(BF16)
    wu = wu_ref[0].astype(BF16)
    wd = wd_ref[0].astype(BF16)
    for x_ref, y_ref in ((xc_ref, yc_ref), (xd_ref, yd_ref)):
        rows = x_ref.shape[1]
        rb = min(rows, EXPERT_ROW_BLOCK)
        for r0 in range(0, rows, rb):
            x = x_ref[0, r0:r0 + rb, :]
            hg = jnp.dot(x, wg, preferred_element_type=F32)
            hu = jnp.dot(x, wu, preferred_element_type=F32)
            y = jnp.dot((_silu(hg) * hu).astype(BF16), wd, preferred_element_type=F32)
            y_ref[0, r0:r0 + rb, :] = y.astype(y_ref.dtype)


def _expert_call(xs_c, xs_d, w_gate, w_up, w_down):
    e_n, rc, d = xs_c.shape
    rd = xs_d.shape[1]
    ff = w_gate.shape[2]
    return pl.pallas_call(
        _expert_kernel,
        grid=(e_n,),
        in_specs=[pl.BlockSpec((1, rc, d), lambda e: (e, 0, 0)),
                  pl.BlockSpec((1, rd, d), lambda e: (e, 0, 0)),
                  pl.BlockSpec((1, d, ff), lambda e: (e, 0, 0)),
                  pl.BlockSpec((1, d, ff), lambda e: (e, 0, 0)),
                  pl.BlockSpec((1, ff, d), lambda e: (e, 0, 0))],
        out_specs=[pl.BlockSpec((1, rc, d), lambda e: (e, 0, 0)),
                   pl.BlockSpec((1, rd, d), lambda e: (e, 0, 0))],
        out_shape=[jax.ShapeDtypeStruct((e_n, rc, d), BF16), jax.ShapeDtypeStruct((e_n, rd, d), BF16)],
        compiler_params=_cparams(1),
        name="experts",
    )(xs_c, xs_d, w_gate, w_up, w_down)


def _final_kernel(seq, cap, reqs, x1_ref, y_ref, slot_ref, pt_ref, mod_ref, l2g, l2b, o_ref, w_s):
    jrow = _iota((cap, seq), 0).astype(F32)
    g2 = mod_ref[0, 5]
    for r in range(reqs):
        slot = slot_ref[r]
        pt = pt_ref[r]
        for e in range(N_EXPERTS):
            w_s[r, e * cap:(e + 1) * cap, :] = jnp.where(slot[e:e + 1, :] == jrow, pt[e:e + 1, :],
                                                         0.0).astype(BF16)
        y = y_ref[:, r * cap:(r + 1) * cap, :].reshape(N_EXPERTS * cap, y_ref.shape[-1])
        ffn = lax.dot_general(w_s[r], y, (((0,), (0,)), ((), ())), preferred_element_type=F32)
        rows = slice(r * seq, (r + 1) * seq)
        o_ref[rows, :] = _ln(ALPHA * x1_ref[rows, :].astype(F32) + g2 * ffn) * l2g[...] + l2b[...]


def _final_call(x1, ys, slot, pt, mod4, mod_row, l2g, l2b, bsz, seq, reqs):
    cap = EC_FACTOR * seq // N_EXPERTS
    d = D_MODEL
    per_req = pl.BlockSpec((reqs, N_EXPERTS, seq), lambda b: (b, 0, 0))
    return pl.pallas_call(
        functools.partial(_final_kernel, seq, cap, reqs),
        grid=(bsz // reqs,),
        in_specs=[pl.BlockSpec((reqs * seq, d), lambda b: (b, 0)),
                  pl.BlockSpec((N_EXPERTS, reqs * cap, d), lambda b: (0, b, 0)),
                  per_req, per_req,
                  _mod_spec(lambda b: mod_row(b * reqs * seq)),
                  pl.BlockSpec((1, d), lambda b: (0, 0)),
                  pl.BlockSpec((1, d), lambda b: (0, 0))],
        out_specs=pl.BlockSpec((reqs * seq, d), lambda b: (b, 0)),
        out_shape=jax.ShapeDtypeStruct((bsz * seq, d), F32),
        scratch_shapes=[pltpu.VMEM((reqs, N_EXPERTS * cap, seq), BF16)],
        compiler_params=_cparams(1),
        name="final",
    )(x1, ys, slot, pt, mod4, l2g, l2b)


def _grid_pos_embed(rows, dim):
    r = np.repeat(np.arange(rows), GRID_W)
    col = np.tile(np.arange(GRID_W), rows)
    quarter = dim // 4
    omega = 1.0 / (10000.0 ** (np.arange(quarter) / quarter))

    def enc(p):
        ang = p[:, None] * omega[None, :]
        return np.concatenate([np.sin(ang), np.cos(ang)], -1)

    return jnp.asarray(np.concatenate([enc(r), enc(col)], -1), dtype=F32)


def kernel(x_prompt, x_sample, state_delta_fwd, state_delta_bwd, c, c_ctx, w_mod, b_mod, w_in, conv_a_w, a_log, dt_bias, gnorm_w, conv_b_w, conv_b_b, hy_w1, hy_b1, hy_w2, hy_b2, hy_w3, hy_freq, hy_bias, w_up_a, w_up_b, w_out, ln1_g, ln1_b, w_router, w_e_gate, w_e_up, w_e_down, ln2_g, ln2_b):
    d = D_MODEL
    n_ctx, l_ctx, _ = x_prompt.shape
    n_dec, l_dec, _ = x_sample.shape
    assert w_mod.shape[0] == DEPTH == 1, "single-layer trunk"
    lyr = 0

    cond_t = jnp.pad(jnp.concatenate([c_ctx[None, :], c], 0).T, ((0, 0), (0, 7 - n_dec)))
    mod4 = _mod_call(cond_t, 1 + n_dec, w_mod[lyr], b_mod[lyr]).reshape(8, N_MOD, 1, d)

    in_w = w_in[lyr].astype(BF16)

    prm = jnp.pad(jnp.stack([a_log[lyr].reshape(-1), dt_bias[lyr].reshape(-1)]),
                  ((0, 6), (2 * DN_HEADS, LANES - 4 * DN_HEADS)))
    gn = gnorm_w[lyr].reshape(1, -1)

    wua = w_up_a[lyr].astype(BF16)
    wub = w_up_b[lyr].astype(BF16)
    wout_half = (0.5 * w_out[lyr]).astype(BF16)
    wr = jnp.pad(w_router[lyr], ((0, 0), (0, LANES - N_EXPERTS)))
    wr_hi = wr.astype(BF16)
    wr_lo = (wr - wr_hi.astype(F32)).astype(BF16)
    l1g, l1b = ln1_g[lyr].reshape(1, -1), ln1_b[lyr].reshape(1, -1)
    l2g, l2b = ln2_g[lyr].reshape(1, -1), ln2_b[lyr].reshape(1, -1)

    def front(x2d, pos, bsz, seq, mod_row, s0f, s0b):
        (qkv, z, ba, hy, gates), pre_conv = _inproj_call(x2d, pos, mod4, mod_row, in_w, conv_a_w[lyr],
                                                          conv_b_w[lyr], conv_b_b[lyr], seq, ACT_DT)
        reqs = DN_REQS_PER_STEP if seq == DN_CHUNK and bsz % DN_REQS_PER_STEP == 0 else 1
        o_a, s_f, s_b = _deltanet_call(qkv, z, ba, conv_a_w[lyr], prm, gn, s0f, s0b, bsz, seq, reqs,
                                       pre_conv, ACT_DT)
        fwd, inv = (jnp.asarray(m) for m in _dft_mats(seq))
        fwd_hi = fwd.astype(BF16)
        fwd_lo = (fwd - fwd_hi.astype(F32)).astype(BF16)
        filt_params = (hy_w1[lyr], hy_b1[lyr], hy_w2[lyr], hy_b2[lyr], hy_w3[lyr], hy_freq[lyr])
        o_h = _hyena_call(hy, pre_conv, conv_b_w[lyr], conv_b_b[lyr], filt_params, hy_bias[lyr],
                          fwd_hi, fwd_lo, inv.astype(BF16), bsz, seq, ACT_DT)
        x1, u2, probs = _mix_call(x2d, pos, o_a, o_h, gates, mod4, mod_row, wua, wub, wout_half, l1g, l1b,
                                   wr_hi, wr_lo)
        xs, slot, pt = _route_call(probs, u2, bsz, seq)
        return x1, xs, slot, pt, s_f, s_b

    row_ctx = lambda tok: 0
    row_dec = lambda tok: 1 + tok // l_dec

    xc = x_prompt.reshape(n_ctx * l_ctx, d)
    xd = x_sample.reshape(n_dec * l_dec, d)
    pos = _grid_pos_embed(l_dec // GRID_W, d)

    x1c, xsc, gc, ptc, s_f, s_b = front(xc, None, n_ctx, l_ctx, row_ctx, None, None)
    x1d, xsd, gd, ptd, _, _ = front(xd, pos, n_dec, l_dec, row_dec,
                                    state_delta_fwd[:, lyr], state_delta_bwd[:, lyr])
    yc, yd = _expert_call(xsc, xsd, w_e_gate[lyr], w_e_up[lyr], w_e_down[lyr])
    reqs_c = FINAL_CTX_REQS_PER_STEP if n_ctx % FINAL_CTX_REQS_PER_STEP == 0 else 1
    y_prompt = _final_call(x1c, yc, gc, ptc, mod4, row_ctx, l2g, l2b, n_ctx, l_ctx, reqs_c)
    y_sample = _final_call(x1d, yd, gd, ptd, mod4, row_dec, l2g, l2b, n_dec, l_dec, 1)

    return (y_prompt.reshape(n_ctx, l_ctx, d), y_sample.reshape(n_dec, l_dec, d),
            s_f[:, None], s_b[:, None])
```

```python
import functools
import math

import jax
import jax.numpy as jnp
import numpy as np
from jax import lax
from jax.experimental import pallas as pl
from jax.experimental.pallas import tpu as pltpu

F32 = jnp.float32
BF16 = jnp.bfloat16
HIGHEST = lax.Precision.HIGHEST

D_MODEL = 1024
DEPTH = 1
GRID_W = 64
DN_HEADS = 4
DN_DK = 128
DN_DV = 128
DN_WIDTH = DN_HEADS * DN_DV
HY_WIDTH = D_MODEL // 2
HY_EMB = 33
HY_FFN = 64
HY_TARGET = 1e-2
HY_FAST_DECAY = 0.3
HY_SLOW_DECAY = 1.5
N_EXPERTS = 16
EC_FACTOR = 2
N_MOD = 6
ALPHA = (2 * DEPTH) ** 0.25
LN_EPS = 1e-5
RMS_EPS = 1e-6
QKV_WIDTH = 2 * DN_HEADS * DN_DK + DN_WIDTH

LANES = 128
INPROJ_TILE = 1024
INPROJ_BLOCK_ROWS = 512
MIX_TILE = 1024
MIX_SUB_ROWS = 256
MIX_MIN_STEPS = 4
MOD_COL_BLOCK = 1024
DN_CHUNK = 256
INV_BASE = 8
DN_REQS_PER_STEP = 2
HY_ROWS_PER_STEP = 1024
FINAL_CTX_REQS_PER_STEP = 8
EXPERT_ROW_BLOCK = 512
VMEM_LIMIT = 56 * 1024 * 1024
MIN_NORMAL_F32_BITS = 0x00800000
ACT_DT = BF16
ROUTE_ROWS_PER_STEP = 1024
RANK_COUNT_MAX_SEQ = 256


def _cparams(n_axes):
    return pltpu.CompilerParams(dimension_semantics=("arbitrary",) * n_axes,
                                vmem_limit_bytes=VMEM_LIMIT)


def _bmm(a, b):
    return jnp.einsum("bmk,bkn->bmn", a.astype(BF16), b.astype(BF16), preferred_element_type=F32)


def _bmm_nt(a, b):
    return jnp.einsum("bmk,bnk->bmn", a.astype(BF16), b.astype(BF16), preferred_element_type=F32)


def _bmm_tn(a, b):
    return jnp.einsum("bkm,bkn->bmn", a.astype(BF16), b.astype(BF16), preferred_element_type=F32)


def _mm_f32(a, b):
    return jnp.dot(a, b, precision=HIGHEST, preferred_element_type=F32)


def _sigmoid(x):
    return 0.5 * jnp.tanh(0.5 * x) + 0.5


def _silu(x):
    h = 0.5 * x
    return h * (jnp.tanh(h) + 1.0)


def _softplus(x):
    return jnp.maximum(x, 0.0) + jnp.log1p(jnp.exp(-jnp.abs(x)))


def _ln(x):
    mu = jnp.mean(x, axis=-1, keepdims=True)
    xc = x - mu
    var = jnp.mean(xc * xc, axis=-1, keepdims=True)
    return xc * lax.rsqrt(var + LN_EPS)


def _l2norm(x, scale=1.0):
    return x * (lax.rsqrt(jnp.sum(x * x, axis=-1, keepdims=True) + 1e-6) * scale)


def _iota(shape, dim):
    return lax.broadcasted_iota(jnp.int32, shape, dim)


def _short_conv3(x, w, first, last):
    xm = jnp.where(first, 0.0, pltpu.roll(x, 1, 0))
    xp = jnp.where(last, 0.0, pltpu.roll(x, x.shape[0] - 1, 0))
    return w[0:1, :] * xm + w[1:2, :] * x + w[2:3, :] * xp


def _mod_kernel(n_rows, ct_ref, w_ref, b_ref, o_ref):
    s = _silu(ct_ref[...])
    w = w_ref[...]
    b = b_ref[...]
    o_ref[...] = jnp.zeros(o_ref.shape, F32)
    for r in range(n_rows):
        o_ref[r:r + 1, :] = jnp.sum(w * s[:, r:r + 1], axis=0, keepdims=True) + b


def _mod_call(cond_t, n_rows, w_mod, b_mod):
    d = D_MODEL
    tn = MOD_COL_BLOCK
    return pl.pallas_call(
        functools.partial(_mod_kernel, n_rows),
        grid=(N_MOD * d // tn,),
        in_specs=[pl.BlockSpec((d, 8), lambda j: (0, 0)),
                  pl.BlockSpec((d, tn), lambda j: (0, j)),
                  pl.BlockSpec((1, tn), lambda j: (0, j))],
        out_specs=pl.BlockSpec((8, tn), lambda j: (0, j)),
        out_shape=jax.ShapeDtypeStruct((8, N_MOD * d), F32),
        compiler_params=_cparams(1),
        name="mod",
    )(cond_t, w_mod, b_mod.reshape(1, -1))


def _mod_spec(row_of_step):
    return pl.BlockSpec((1, N_MOD, 1, D_MODEL), lambda *idx: (row_of_step(*idx), 0, 0, 0))


def _inproj_kernel(has_pos, conv_seq, *refs):
    refs = list(refs)
    x_ref = refs.pop(0)
    pos_ref = refs.pop(0) if has_pos else None
    mod_ref, w_ref = refs.pop(0), refs.pop(0)
    if conv_seq:
        cwa_ref, cwb_ref, cbb_ref = refs.pop(0), refs.pop(0), refs.pop(0)
    qkv_o, z_o, ba_o, hy_o, g_o = refs
    sh1 = mod_ref[0, 0]
    sc1 = mod_ref[0, 1]
    o_z = QKV_WIDTH
    o_ba = o_z + DN_WIDTH
    n_ba = 4 * DN_HEADS
    o_g = n_ba + 3 * HY_WIDTH
    lanes = lambda j: slice(j * LANES, (j + 1) * LANES)
    tm = x_ref.shape[0]
    sub = min(tm, INPROJ_BLOCK_ROWS)
    blocks = [slice(r0, r0 + sub) for r0 in range(0, tm, sub)]

    if conv_seq:
        row = _iota((sub, LANES), 0) % conv_seq
        first = row == 0
        last = row == conv_seq - 1

    def project(rows):
        x = x_ref[rows, :]
        if has_pos:
            x = x + pos_ref[rows, :]
        u = (_ln(x) * (1.0 + sc1) + sh1).astype(BF16)
        return (jnp.dot(u, w_ref[:, :o_ba], preferred_element_type=F32),
                jnp.dot(u, w_ref[:, o_ba:], preferred_element_type=F32))

    def epilogue(rows, head, tail):
        z_o[rows, :] = head[:, o_z:].astype(z_o.dtype)
        ba_o[rows, :] = tail[:, :LANES]
        g_o[rows, :] = tail[:, o_g:].astype(g_o.dtype)
        hy = tail[:, n_ba:o_g]
        if not conv_seq:
            qkv_o[rows, :] = head[:, :o_z].astype(qkv_o.dtype)
            hy_o[rows, :] = hy.astype(hy_o.dtype)
            return
        for j in range(QKV_WIDTH // LANES):
            y = _silu(_short_conv3(head[:, lanes(j)], cwa_ref[:, lanes(j)], first, last))
            if j < 2 * DN_HEADS:
                y = _l2norm(y, DN_DK ** -0.5 if j < DN_HEADS else 1.0)
            qkv_o[rows, lanes(j)] = y.astype(qkv_o.dtype)
        nj = HY_WIDTH // LANES
        for j in range(nj):
            parts = [_short_conv3(hy[:, lanes(p * nj + j)], cwb_ref[:, lanes(p * nj + j)], first, last)
                     + cbb_ref[:, lanes(p * nj + j)] for p in range(3)]
            hy_o[rows, lanes(j)] = parts[0].astype(hy_o.dtype)
            hy_o[rows, lanes(nj + j)] = (parts[1] * parts[2]).astype(hy_o.dtype)

    pending = None
    for rows in blocks:
        cur = (rows, *project(rows))
        if pending is not None:
            epilogue(*pending)
        pending = cur
    epilogue(*pending)


def _inproj_call(x2d, pos, mod4, mod_row, w, conv_a_w, conv_b_w, conv_b_b, seq, act_dt):
    n, d = x2d.shape
    fuse = seq <= INPROJ_BLOCK_ROWS and INPROJ_TILE % seq == 0
    tm = min(INPROJ_TILE if fuse else INPROJ_BLOCK_ROWS, n)
    assert n % tm == 0 and (not fuse or tm % seq == 0)
    conv_seq = seq if fuse else None
    row = lambda i: (i, 0)
    const = lambda i: (0, 0)
    in_specs = [pl.BlockSpec((tm, d), row)]
    args = [x2d]
    if pos is not None:
        tiles = pos.shape[0] // tm
        in_specs.append(pl.BlockSpec((tm, d), lambda i: (i % tiles, 0)))
        args.append(pos)
    in_specs.append(_mod_spec(lambda i: mod_row(i * tm)))
    args.append(mod4)
    in_specs.append(pl.BlockSpec(w.shape, const, pipeline_mode=pl.Buffered(1)))
    args.append(w)
    if conv_seq:
        for a in (conv_a_w, conv_b_w, conv_b_b.reshape(1, -1)):
            in_specs.append(pl.BlockSpec(a.shape, const))
            args.append(a)
    hy_width = (2 if conv_seq else 3) * HY_WIDTH
    widths = (QKV_WIDTH, DN_WIDTH, LANES, hy_width, 2 * D_MODEL)
    dts = (act_dt, act_dt, F32, act_dt, act_dt)
    outs = pl.pallas_call(
        functools.partial(_inproj_kernel, pos is not None, conv_seq),
        grid=(n // tm,),
        in_specs=in_specs,
        out_specs=[pl.BlockSpec((tm, w), row) for w in widths],
        out_shape=[jax.ShapeDtypeStruct((n, w), dt) for w, dt in zip(widths, dts)],
        compiler_params=_cparams(1),
        name="inproj",
    )(*args)
    return outs, conv_seq is not None


def _inv_unit_tri(lm, ri, ci, block):
    c = lm.shape[-1]

    def same_block(h):
        sh = h.bit_length() - 1
        return (ri >> sh) == (ci >> sh)

    def unfold(f, h):
        return jnp.where(same_block(h), jnp.concatenate([f] * (c // h), axis=1), 0.0)

    base = INV_BASE
    n_nat = jnp.where(same_block(base), -lm, 0.0)
    n_fold = n_nat[:, 0:base, :]
    for i in range(1, c // base):
        n_fold = n_fold + n_nat[:, i * base:(i + 1) * base, :]
    eye_fold = jnp.where((_iota((base, c), 1) & (base - 1)) == _iota((base, c), 0), 1.0, 0.0)
    t = eye_fold + n_fold
    p_fold, p_nat = n_fold, n_nat
    span = 2
    while span < base:
        p_fold = _bmm(p_fold, p_nat)
        p_nat = unfold(p_fold, base)
        t = t + _bmm(t, p_nat)
        span *= 2
    s = base
    while s < block:
        sh = s.bit_length() - 1
        t_nat = unfold(t, s)
        even = ((_iota((s, c), 1) >> sh) & 1) == 0
        t2 = jnp.concatenate([jnp.where(even, t, 0.0), jnp.where(even, 0.0, t)], axis=1)
        lo = jnp.where(same_block(2 * s) & ~same_block(s), lm, 0.0)
        t = t2 - _bmm(_bmm(t2, lo), t_nat)
        s *= 2
    return t


def _dn_chunk(q, k, v, gc_col, gc_row, beta, e_col, e_rest, e_all, s_prev, same_chunk, ri, ci):
    c = DN_CHUNK
    nh = q.shape[0] // 2
    half = c // 2
    if same_chunk:
        gram = _bmm_nt(k[:nh], k[:nh])
        qk = _bmm_nt(q[:nh], k[:nh])
        gram = jnp.concatenate([gram, gram], 0)
        qk = jnp.concatenate([qk, qk], 0)
    else:
        gram = _bmm_nt(k, k)
        qk = _bmm_nt(q, k)
    diff = gc_col - gc_row
    incl = jnp.concatenate([jnp.broadcast_to(ri >= ci, (nh, c, c)), jnp.broadcast_to(ri <= ci, (nh, c, c))], 0)
    strict = incl & (ri != ci)
    decay = jnp.exp(jnp.where(incl, diff, -jnp.inf))
    lm = jnp.where(strict, beta * gram * decay, 0.0)
    a_intra = qk * decay
    t = _inv_unit_tri(lm, ri, ci, half)
    t1 = t[:, :, :half]
    t2 = t[:, :, half:]
    rhs = jnp.concatenate([v * beta, k * (beta * e_col)], axis=2)
    y1 = _bmm(t1, rhs[:, :half, :])
    y2 = _bmm(t2, rhs[:, half:, :])
    c_f = _bmm(lm[:nh, half:, :half], y1[:nh])
    c_b = _bmm(lm[nh:, :half, half:], y2[nh:])
    c_f = _bmm(t2[:nh], c_f)
    c_b = _bmm(t1[nh:], c_b)
    sol = jnp.concatenate([jnp.concatenate([y1[:nh], y2[:nh] - c_f], 1),
                           jnp.concatenate([y1[nh:] - c_b, y2[nh:]], 1)], 0)
    u = sol[:, :, :DN_DV]
    w = sol[:, :, DN_DV:]
    ks = k * e_rest
    if s_prev is None:
        v_new = u
        o = _bmm(a_intra, v_new)
        s_new = _bmm_tn(ks, v_new)
    else:
        v_new = u - _bmm(w, s_prev)
        o = _bmm(q * e_col, s_prev) + _bmm(a_intra, v_new)
        s_new = s_prev * e_all + _bmm_tn(ks, v_new)
    return o, s_new


def _deltanet_kernel(seq, reqs, zero_init, pre_conv, *refs):
    qkv_ref, cw_ref, z_ref, ba_ref, prm_ref, gn_ref = refs[:6]
    refs = refs[6:]
    if not zero_init:
        s0f_ref, s0b_ref = refs[:2]
        refs = refs[2:]
    o_ref, sf_ref, sb_ref, q_s, k_s, v_s, g_s, b_s, o_s, st_s = refs
    c = DN_CHUNK
    n = seq // c
    nh = DN_HEADS
    nf = reqs * nh
    lanes = lambda j: slice(j * LANES, (j + 1) * LANES)

    if not pre_conv:
        row = _iota((seq, LANES), 0)
        first_row = row == 0
        last_row = row == seq - 1
    for r in range(reqs):
        rows = slice(r * seq, (r + 1) * seq)

        def qkv_part(j):
            x = qkv_ref[rows, lanes(j)].astype(F32)
            return x if pre_conv else _silu(_short_conv3(x, cw_ref[:, lanes(j)], first_row, last_row))

        for h in range(nh):
            q = qkv_part(h)
            k = qkv_part(nh + h)
            q_s[r * nh + h] = q if pre_conv else _l2norm(q, DN_DK ** -0.5)
            k_s[r * nh + h] = k if pre_conv else _l2norm(k)
            v_s[r * nh + h] = qkv_part(2 * nh + h)

    ba = ba_ref[...]
    b_s[...] = _sigmoid(ba)
    g_s[...] = -jnp.exp(prm_ref[0:1, :]) * _softplus(ba + prm_ref[1:2, :])

    ri = _iota((c, c), 0)
    ci = _iota((c, c), 1)
    tri_l = jnp.where(ri >= ci, 1.0, 0.0).astype(F32)
    tri_u = jnp.where(ri <= ci, 1.0, 0.0).astype(F32)

    def chunk_pair(idx_f, idx_b, s_prev):
        def ds(idx, base=0):
            start = idx * c
            start = start if isinstance(start, int) else pl.multiple_of(start, c)
            return pl.ds(base + start, c)

        col_f, col_b, row_f, row_b, beta_f, beta_b = [], [], [], [], [], []
        for r in range(reqs):
            cs_f = _mm_f32(tri_l, g_s[ds(idx_f, r * seq), :])
            cs_b = _mm_f32(tri_u, g_s[ds(idx_b, r * seq), :])
            cst_f = cs_f.T
            cst_b = cs_b.T
            b_f = b_s[ds(idx_f, r * seq), :]
            b_b = b_s[ds(idx_b, r * seq), :]
            for h in range(nh):
                jf, jb = 2 * nh + h, 3 * nh + h
                col_f.append(cs_f[:, jf:jf + 1])
                col_b.append(cs_b[:, jb:jb + 1])
                row_f.append(cst_f[jf:jf + 1, :])
                row_b.append(cst_b[jb:jb + 1, :])
                beta_f.append(b_f[:, h:h + 1])
                beta_b.append(b_b[:, nh + h:nh + h + 1])
        gc_col = jnp.stack(col_f + col_b, 0)
        gc_row = jnp.stack(row_f + row_b, 0)
        beta = jnp.stack(beta_f + beta_b, 0)
        e_col = jnp.exp(gc_col)
        gl = jnp.concatenate([gc_col[:nf, c - 1:c, :], gc_col[nf:, 0:1, :]], 0)
        e_rest = jnp.exp(gl - gc_col)
        e_all = jnp.exp(gl)
        sl_f, sl_b = ds(idx_f), ds(idx_b)
        both = lambda ref: jnp.concatenate([ref[:, sl_f, :], ref[:, sl_b, :]], 0)
        o, s_new = _dn_chunk(both(q_s), both(k_s), both(v_s), gc_col, gc_row, beta, e_col, e_rest, e_all,
                             s_prev, n == 1, ri, ci)
        o_s[:nf, sl_f, :] = o[:nf]
        o_s[nf:, sl_b, :] = o[nf:]
        return s_new

    if not zero_init:
        s0 = jnp.concatenate([s0f_ref[r] for r in range(reqs)] + [s0b_ref[r] for r in range(reqs)], 0)
    if n == 1:
        s_fin = chunk_pair(0, 0, None if zero_init else s0)
    else:
        st_s[...] = jnp.zeros(st_s.shape, F32) if zero_init else s0

        def body(i, carry):
            st_s[...] = chunk_pair(i, n - 1 - i, st_s[...])
            return carry

        lax.fori_loop(0, n, body, 0)
        s_fin = st_s[...]
    for r in range(reqs):
        sf_ref[r] = s_fin[r * nh:(r + 1) * nh]
        sb_ref[r] = s_fin[nf + r * nh:nf + (r + 1) * nh]

    gn = gn_ref[...]
    for r in range(reqs):
        rows = slice(r * seq, (r + 1) * seq)
        for h in range(nh):
            o = o_s[r * nh + h] + o_s[nf + r * nh + h]
            o = o * lax.rsqrt(jnp.mean(o * o, axis=-1, keepdims=True) + RMS_EPS) * gn
            o_ref[rows, lanes(h)] = (o * _silu(z_ref[rows, lanes(h)].astype(F32))).astype(o_ref.dtype)


def _deltanet_call(qkv, z, ba, conv_w, prm, gn, s0f, s0b, bsz, seq, reqs, pre_conv, act_dt):
    nh = DN_HEADS
    zero_init = s0f is None
    rows = reqs * seq
    st = pl.BlockSpec((reqs, nh, DN_DK, DN_DV), lambda b: (b, 0, 0, 0))
    in_specs = [pl.BlockSpec((rows, QKV_WIDTH), lambda b: (b, 0)),
                pl.BlockSpec((3, QKV_WIDTH), lambda b: (0, 0)),
                pl.BlockSpec((rows, DN_WIDTH), lambda b: (b, 0)),
                pl.BlockSpec((rows, LANES), lambda b: (b, 0)),
                pl.BlockSpec((8, LANES), lambda b: (0, 0)),
                pl.BlockSpec((1, LANES), lambda b: (0, 0))]
    args = [qkv, conv_w, z, ba, prm, gn]
    if not zero_init:
        in_specs += [st, st]
        args += [s0f, s0b]
    vm = lambda shape: pltpu.VMEM(shape, F32)
    per_head = vm((reqs * nh, seq, LANES))
    return pl.pallas_call(
        functools.partial(_deltanet_kernel, seq, reqs, zero_init, pre_conv),
        grid=(bsz // reqs,),
        in_specs=in_specs,
        out_specs=[pl.BlockSpec((rows, DN_WIDTH), lambda b: (b, 0)), st, st],
        out_shape=[jax.ShapeDtypeStruct((bsz * seq, DN_WIDTH), act_dt),
                   jax.ShapeDtypeStruct((bsz, nh, DN_DK, DN_DV), F32),
                   jax.ShapeDtypeStruct((bsz, nh, DN_DK, DN_DV), F32)],
        scratch_shapes=[per_head, per_head, per_head, vm((rows, LANES)), vm((rows, LANES)),
                        vm((2 * reqs * nh, seq, LANES)), vm((2 * reqs * nh, DN_DK, DN_DV))],
        compiler_params=_cparams(1),
        name="deltanet",
    )(*args)


def _dft_mats(seq):
    n = 2 * seq
    f = np.arange(seq)[:, None]
    s = np.arange(seq)[None, :]
    ang = 2.0 * np.pi * ((f * s) % n) / n
    fwd = np.concatenate([np.cos(ang), -np.sin(ang)], axis=0)
    fwd[seq, :] = np.cos(np.pi * np.arange(seq))
    t = (np.arange(seq) + seq // 2)[:, None]
    ff = np.arange(seq)[None, :]
    ang2 = 2.0 * np.pi * ((t * ff) % n) / n
    inv_r = 2.0 * np.cos(ang2) / n
    inv_i = -2.0 * np.sin(ang2) / n
    inv_r[:, 0] = 1.0 / n
    inv_i[:, 0] = np.cos(np.pi * t[:, 0]) / n
    inv = np.concatenate([inv_r, inv_i], axis=1)
    return fwd.astype(np.float32), inv.astype(np.float32)


def _filter_feats(seq):
    t = np.linspace(0.0, 1.0, seq)[:, None]
    bands = (HY_EMB - 1) // 2
    ang = (2.0 * math.pi * np.arange(seq) / seq)[:, None] * np.linspace(1e-4, bands - 1, bands)[None, :]
    feats = np.concatenate([t, np.cos(ang), -np.sin(ang)], -1)
    deltas = np.abs(np.linspace(math.log(HY_TARGET) / HY_SLOW_DECAY, math.log(HY_TARGET) / HY_FAST_DECAY,
                                HY_WIDTH))
    offset = np.abs(np.arange(seq) - seq // 2) / (seq // 2)
    window = np.exp(-offset[:, None] * deltas[None, :])
    feats = np.pad(feats, ((0, 0), (0, LANES - HY_EMB)))
    return jnp.asarray(feats, dtype=F32), jnp.asarray(window, dtype=F32)


def _filter_spectrum(seq, feats_ref, win_ref, w1_ref, b1_ref, w2_ref, b2_ref, w3_ref, fr_ref, fwd_hi_ref,
                     fwd_lo_ref, ha_s, hb_s, hd_s):
    fr = fr_ref[...]
    hid = jnp.sin(fr * (_mm_f32(feats_ref[...], w1_ref[...]) + b1_ref[...]))
    hid = jnp.sin(fr * (_mm_f32(hid, w2_ref[...]) + b2_ref[...]))
    filt = _mm_f32(hid, w3_ref[...]) * win_ref[...]
    filt = filt / (jnp.sum(jnp.abs(filt), axis=0, keepdims=True) + 1e-6)
    filt_hi = filt.astype(BF16)
    filt_lo = (filt - filt_hi.astype(F32)).astype(BF16)
    fwd_hi = fwd_hi_ref[...]
    spec = (jnp.dot(fwd_hi, filt_hi, preferred_element_type=F32)
            + jnp.dot(fwd_hi, filt_lo, preferred_element_type=F32)
            + jnp.dot(fwd_lo_ref[...], filt_hi, preferred_element_type=F32))
    h_re = spec[:seq, :]
    h_im = spec[seq:, :]
    first = _iota(h_re.shape, 0) == 0
    ha_s[...] = h_re
    hb_s[...] = jnp.where(first, 0.0, h_im)
    hd_s[...] = jnp.where(first, h_im, h_re)


def _hyena_kernel(seq, reqs, pre_conv, *refs):
    refs = list(refs)
    n_tok = 2 if pre_conv else 9
    tok_refs, refs = refs[:n_tok], refs[n_tok:]
    filt_refs, refs = refs[:8], refs[8:]
    skip_ref, fwd_ref, fwd_lo_ref, inv_ref, o_ref, ha_s, hb_s, hd_s = refs

    @pl.when(pl.program_id(0) == 0)
    def _():
        _filter_spectrum(seq, *filt_refs, fwd_ref, fwd_lo_ref, ha_s, hb_s, hd_s)

    if pre_conv:
        x0_ref, uu_ref = tok_refs
    else:
        x0_ref, x1_ref, v_ref, c0_ref, c1_ref, c2_ref, b0_ref, b1_ref, b2_ref = tok_refs
        row = _iota((seq, HY_WIDTH), 0)
        first = row == 0
        last = row == seq - 1
    hb = hb_s[...]
    for r in range(reqs):
        rows = slice(r * seq, (r + 1) * seq)
        if pre_conv:
            x0 = x0_ref[rows, :].astype(F32)
            uu = uu_ref[rows, :].astype(F32)
        else:
            conv = lambda x_ref, w_ref, b_ref: (_short_conv3(x_ref[rows, :].astype(F32), w_ref[...], first, last)
                                                + b_ref[...])
            x0 = conv(x0_ref, c0_ref, b0_ref)
            uu = conv(x1_ref, c1_ref, b1_ref) * conv(v_ref, c2_ref, b2_ref)
        spec = jnp.dot(fwd_ref[...], uu.astype(BF16), preferred_element_type=F32)
        u_re = spec[:seq, :]
        u_im = spec[seq:, :]
        y_re = u_re * ha_s[...] - u_im * hb
        y_im = u_re * hb + u_im * hd_s[...]
        y = jnp.concatenate([y_re, y_im], axis=0).astype(BF16)
        cv = jnp.dot(inv_ref[...], y, preferred_element_type=F32)
        o_ref[rows, :] = (x0 * (cv + uu * skip_ref[...])).astype(o_ref.dtype)


def _hyena_call(hy, pre_conv, conv_w, conv_b, filt_params, skip, fwd_hi, fwd_lo, inv, bsz, seq, act_dt):
    w1, b1, w2, b2, w3, freq = filt_params
    cb = HY_WIDTH
    reqs = max(1, min(bsz, HY_ROWS_PER_STEP // seq))
    assert bsz % reqs == 0
    tok = lambda part: pl.BlockSpec((reqs * seq, cb), lambda b: (b, part))
    const = lambda a: pl.BlockSpec(a.shape, lambda b: (0,) * a.ndim, pipeline_mode=pl.Buffered(1))
    feats, window = _filter_feats(seq)
    w1p = jnp.pad(w1, ((0, LANES - HY_EMB), (0, 0)))
    filt_args = [feats, window, w1p, b1.reshape(1, -1), w2, b2.reshape(1, -1), w3, freq.reshape(1, -1)]
    tail_args = filt_args + [skip.reshape(1, -1), fwd_hi, fwd_lo, inv]
    tail_specs = [const(a) for a in tail_args]
    if pre_conv:
        in_specs = [tok(0), tok(1)] + tail_specs
        args = [hy, hy] + tail_args
    else:
        conv_b2 = conv_b.reshape(1, -1)
        part3 = lambda a, part: pl.BlockSpec((a.shape[0], cb), lambda b: (0, part))
        in_specs = ([tok(0), tok(1), tok(2)] + [part3(conv_w, p) for p in range(3)]
                    + [part3(conv_b2, p) for p in range(3)] + tail_specs)
        args = [hy, hy, hy, conv_w, conv_w, conv_w, conv_b2, conv_b2, conv_b2] + tail_args
    spec_buf = pltpu.VMEM((seq, HY_WIDTH), F32)
    return pl.pallas_call(
        functools.partial(_hyena_kernel, seq, reqs, pre_conv),
        grid=(bsz // reqs,),
        in_specs=in_specs,
        out_specs=tok(0),
        out_shape=jax.ShapeDtypeStruct((bsz * seq, HY_WIDTH), act_dt),
        scratch_shapes=[spec_buf, spec_buf, spec_buf],
        compiler_params=_cparams(1),
        name="hyena",
    )(*args)


def _mix_kernel(has_pos, *refs):
    refs = list(refs)
    x_ref = refs.pop(0)
    pos_ref = refs.pop(0) if has_pos else None
    (oa_ref, ob_ref, g_ref, mod_ref, wua, wub, wout_half, l1g, l1b, wr_hi, wr_lo, x1_o, u2_o, p_o) = refs
    d = D_MODEL
    tm = x_ref.shape[0]
    sub = min(tm, MIX_SUB_ROWS)
    blocks = [slice(r0, r0 + sub) for r0 in range(0, tm, sub)]
    g1 = mod_ref[0, 2]
    sh2 = mod_ref[0, 3]
    sc2 = mod_ref[0, 4]

    mixed = []
    for rows in blocks:
        th = jnp.tanh(0.5 * g_ref[rows, :].astype(F32))
        a = jnp.dot(oa_ref[rows, :], wua[...], preferred_element_type=F32)
        b = jnp.dot(ob_ref[rows, :], wub[...], preferred_element_type=F32)
        up2 = (th[:, :d] * a + a) + (th[:, d:] * b + b)
        mixed.append(jnp.dot(up2.astype(BF16), wout_half[...], preferred_element_type=F32))

    for rows, mx in zip(blocks, mixed):
        x = x_ref[rows, :]
        if has_pos:
            x = x + pos_ref[rows, :]
        x1 = _ln(ALPHA * x + g1 * mx) * l1g[...] + l1b[...]
        x1_o[rows, :] = x1.astype(x1_o.dtype)
        u2 = _ln(x1) * (1.0 + sc2) + sh2
        u2_hi = u2.astype(BF16)
        u2_o[rows, :] = u2_hi
        u2_lo = (u2 - u2_hi.astype(F32)).astype(BF16)
        logits = (jnp.dot(u2_hi, wr_hi[...], preferred_element_type=F32)
                  + jnp.dot(u2_hi, wr_lo[...], preferred_element_type=F32)
                  + jnp.dot(u2_lo, wr_hi[...], preferred_element_type=F32))
        lane = _iota(logits.shape, 1)
        logits = jnp.where(lane < N_EXPERTS, logits, -jnp.inf)
        m = jnp.max(logits, axis=-1, keepdims=True)
        e = jnp.exp(logits - m)
        p_o[rows, :] = e / jnp.sum(e, axis=-1, keepdims=True)


def _mix_call(x2d, pos, o_a, o_b, gates, mod4, mod_row, wua, wub, wout_half, l1g, l1b, wr_hi, wr_lo):
    n, d = x2d.shape
    tm = min(MIX_TILE, max(n // MIX_MIN_STEPS, MIX_SUB_ROWS))
    assert n % tm == 0
    row = lambda i: (i, 0)
    const = lambda i: (0, 0)
    in_specs = [pl.BlockSpec((tm, d), row)]
    args = [x2d]
    if pos is not None:
        tiles = pos.shape[0] // tm
        in_specs.append(pl.BlockSpec((tm, d), lambda i: (i % tiles, 0)))
        args.append(pos)
    in_specs += [pl.BlockSpec((tm, DN_WIDTH), row), pl.BlockSpec((tm, HY_WIDTH), row),
                 pl.BlockSpec((tm, 2 * d), row),
                 _mod_spec(lambda i: mod_row(i * tm))]
    args += [o_a, o_b, gates, mod4]
    for w in (wua, wub, wout_half, l1g, l1b, wr_hi, wr_lo):
        in_specs.append(pl.BlockSpec(w.shape, const, pipeline_mode=pl.Buffered(1)))
        args.append(w)
    return pl.pallas_call(
        functools.partial(_mix_kernel, pos is not None),
        grid=(n // tm,),
        in_specs=in_specs,
        out_specs=[pl.BlockSpec((tm, d), row), pl.BlockSpec((tm, d), row), pl.BlockSpec((tm, LANES), row)],
        out_shape=[jax.ShapeDtypeStruct((n, d), ACT_DT), jax.ShapeDtypeStruct((n, d), BF16),
                   jax.ShapeDtypeStruct((n, LANES), F32)],
        compiler_params=_cparams(1),
        name="mix",
    )(*args)


def _route_kernel(seq, cap, reqs, p_ref, u_ref, tri_ref, xs_ref, slot_ref, pt_ref, g_s):
    e_n = N_EXPERTS
    tri = tri_ref[...]
    jrow = _iota((cap, seq), 0).astype(F32)
    if seq <= RANK_COUNT_MAX_SEQ:
        earlier = _iota((seq, seq), 0) < _iota((seq, seq), 1)
    for r in range(reqs):
        rows = slice(r * seq, (r + 1) * seq)
        p = p_ref[rows, :]
        pt = p.T[:e_n, :]

        if seq <= RANK_COUNT_MAX_SEQ:
            ranks = []
            for e in range(e_n):
                pc = p[:, e:e + 1]
                pr = pt[e:e + 1, :]
                beats = (pc > pr) | (earlier & (pc == pr))
                ranks.append(jnp.sum(jnp.where(beats, 1.0, 0.0), axis=0, keepdims=True))
            sel = jnp.concatenate(ranks, axis=0) < cap
        else:
            def ok(cand):
                cnt = jnp.sum(jnp.where(pt >= pltpu.bitcast(cand, F32), 1.0, 0.0), axis=1, keepdims=True)
                return (cnt >= cap) & (cand >= MIN_NORMAL_F32_BITS)

            def search(i, cur):
                lo = 28 - 2 * i
                c1, c2, c3 = cur | (1 << lo), cur | (2 << lo), cur | (3 << lo)
                return jnp.where(ok(c3), c3, jnp.where(ok(c2), c2, jnp.where(ok(c1), c1, cur)))

            top = jnp.full((e_n, 1), 1 << 30, jnp.int32)
            cur = jnp.where(ok(top), top, 0)
            thr = pltpu.bitcast(lax.fori_loop(0, 15, search, cur), F32)
            gt = pt > thr
            eq = pt == thr
            n_gt = jnp.sum(jnp.where(gt, 1.0, 0.0), axis=1, keepdims=True)
            eq_rank = jnp.dot(jnp.where(eq, 1.0, 0.0).astype(BF16), tri, preferred_element_type=F32)
            sel = gt | (eq & (eq_rank < cap - n_gt))
        pos = jnp.dot(jnp.where(sel, 1.0, 0.0).astype(BF16), tri, preferred_element_type=F32)
        slot = jnp.where(sel, pos, -1.0)
        slot_ref[r] = slot
        pt_ref[r] = pt
        for e in range(e_n):
            g_s[r, e * cap:(e + 1) * cap, :] = jnp.where(slot[e:e + 1, :] == jrow, 1.0, 0.0).astype(BF16)
        xs = jnp.dot(g_s[r], u_ref[rows, :], preferred_element_type=F32)
        for e in range(e_n):
            xs_ref[e, r * cap:(r + 1) * cap, :] = xs[e * cap:(e + 1) * cap, :].astype(xs_ref.dtype)


def _route_call(probs, u2, bsz, seq):
    cap = EC_FACTOR * seq // N_EXPERTS
    reqs = max(1, min(bsz, ROUTE_ROWS_PER_STEP // seq))
    assert bsz % reqs == 0
    tri = jnp.asarray(np.triu(np.ones((seq, seq), np.float32), 1), dtype=BF16)
    per_req = pl.BlockSpec((reqs, N_EXPERTS, seq), lambda b: (b, 0, 0))
    return pl.pallas_call(
        functools.partial(_route_kernel, seq, cap, reqs),
        grid=(bsz // reqs,),
        in_specs=[pl.BlockSpec((reqs * seq, LANES), lambda b: (b, 0)),
                  pl.BlockSpec((reqs * seq, D_MODEL), lambda b: (b, 0)),
                  pl.BlockSpec((seq, seq), lambda b: (0, 0), pipeline_mode=pl.Buffered(1))],
        out_specs=[pl.BlockSpec((N_EXPERTS, reqs * cap, D_MODEL), lambda b: (0, b, 0)), per_req, per_req],
        out_shape=[jax.ShapeDtypeStruct((N_EXPERTS, bsz * cap, D_MODEL), BF16),
                   jax.ShapeDtypeStruct((bsz, N_EXPERTS, seq), F32),
                   jax.ShapeDtypeStruct((bsz, N_EXPERTS, seq), F32)],
        scratch_shapes=[pltpu.VMEM((reqs, N_EXPERTS * cap, seq), BF16)],
        compiler_params=_cparams(1),
        name="route",
    )(probs, u2, tri)


def _expert_kernel(xc_ref, xd_ref, wg_ref, wu_ref, wd_ref, yc_ref, yd_ref):
    wg = wg_ref[0].astype(BF16)
    wu = wu_ref[0].astype(BF16)
    wd = wd_ref[0].astype(BF16)
    for x_ref, y_ref in ((xc_ref, yc_ref), (xd_ref, yd_ref)):
        rows = x_ref.shape[1]
        rb = min(rows, EXPERT_ROW_BLOCK)
        for r0 in range(0, rows, rb):
            x = x_ref[0, r0:r0 + rb, :]
            hg = jnp.dot(x, wg, preferred_element_type=F32)
            hu = jnp.dot(x, wu, preferred_element_type=F32)
            y = jnp.dot((_silu(hg) * hu).astype(BF16), wd, preferred_element_type=F32)
            y_ref[0, r0:r0 + rb, :] = y.astype(y_ref.dtype)


def _expert_call(xs_c, xs_d, w_gate, w_up, w_down):
    e_n, rc, d = xs_c.shape
    rd = xs_d.shape[1]
    ff = w_gate.shape[2]
    return pl.pallas_call(
        _expert_kernel,
        grid=(e_n,),
        in_specs=[pl.BlockSpec((1, rc, d), lambda e: (e, 0, 0)),
                  pl.BlockSpec((1, rd, d), lambda e: (e, 0, 0)),
                  pl.BlockSpec((1, d, ff), lambda e: (e, 0, 0)),
                  pl.BlockSpec((1, d, ff), lambda e: (e, 0, 0)),
                  pl.BlockSpec((1, ff, d), lambda e: (e, 0, 0))],
        out_specs=[pl.BlockSpec((1, rc, d), lambda e: (e, 0, 0)),
                   pl.BlockSpec((1, rd, d), lambda e: (e, 0, 0))],
        out_shape=[jax.ShapeDtypeStruct((e_n, rc, d), BF16), jax.ShapeDtypeStruct((e_n, rd, d), BF16)],
        compiler_params=_cparams(1),
        name="experts",
    )(xs_c, xs_d, w_gate, w_up, w_down)


def _final_kernel(seq, cap, reqs, x1_ref, y_ref, slot_ref, pt_ref, mod_ref, l2g, l2b, o_ref, w_s):
    jrow = _iota((cap, seq), 0).astype(F32)
    g2 = mod_ref[0, 5]
    for r in range(reqs):
        slot = slot_ref[r]
        pt = pt_ref[r]
        for e in range(N_EXPERTS):
            w_s[r, e * cap:(e + 1) * cap, :] = jnp.where(slot[e:e + 1, :] == jrow, pt[e:e + 1, :],
                                                         0.0).astype(BF16)
        y = y_ref[:, r * cap:(r + 1) * cap, :].reshape(N_EXPERTS * cap, y_ref.shape[-1])
        ffn = lax.dot_general(w_s[r], y, (((0,), (0,)), ((), ())), preferred_element_type=F32)
        rows = slice(r * seq, (r + 1) * seq)
        o_ref[rows, :] = _ln(ALPHA * x1_ref[rows, :].astype(F32) + g2 * ffn) * l2g[...] + l2b[...]


def _final_call(x1, ys, slot, pt, mod4, mod_row, l2g, l2b, bsz, seq, reqs):
    cap = EC_FACTOR * seq // N_EXPERTS
    d = D_MODEL
    per_req = pl.BlockSpec((reqs, N_EXPERTS, seq), lambda b: (b, 0, 0))
    return pl.pallas_call(
        functools.partial(_final_kernel, seq, cap, reqs),
        grid=(bsz // reqs,),
        in_specs=[pl.BlockSpec((reqs * seq, d), lambda b: (b, 0)),
                  pl.BlockSpec((N_EXPERTS, reqs * cap, d), lambda b: (0, b, 0)),
                  per_req, per_req,
                  _mod_spec(lambda b: mod_row(b * reqs * seq)),
                  pl.BlockSpec((1, d), lambda b: (0, 0)),
                  pl.BlockSpec((1, d), lambda b: (0, 0))],
        out_specs=pl.BlockSpec((reqs * seq, d), lambda b: (b, 0)),
        out_shape=jax.ShapeDtypeStruct((bsz * seq, d), F32),
        scratch_shapes=[pltpu.VMEM((reqs, N_EXPERTS * cap, seq), BF16)],
        compiler_params=_cparams(1),
        name="final",
    )(x1, ys, slot, pt, mod4, l2g, l2b)


def _grid_pos_embed(rows, dim):
    r = np.repeat(np.arange(rows), GRID_W)
    col = np.tile(np.arange(GRID_W), rows)
    quarter = dim // 4
    omega = 1.0 / (10000.0 ** (np.arange(quarter) / quarter))

    def enc(p):
        ang = p[:, None] * omega[None, :]
        return np.concatenate([np.sin(ang), np.cos(ang)], -1)

    return jnp.asarray(np.concatenate([enc(r), enc(col)], -1), dtype=F32)


def kernel(x_prompt, x_sample, state_delta_fwd, state_delta_bwd, c, c_ctx, w_mod, b_mod, w_in, conv_a_w, a_log, dt_bias, gnorm_w, conv_b_w, conv_b_b, hy_w1, hy_b1, hy_w2, hy_b2, hy_w3, hy_freq, hy_bias, w_up_a, w_up_b, w_out, ln1_g, ln1_b, w_router, w_e_gate, w_e_up, w_e_down, ln2_g, ln2_b):
    d = D_MODEL
    n_ctx, l_ctx, _ = x_prompt.shape
    n_dec, l_dec, _ = x_sample.shape
    assert w_mod.shape[0] == DEPTH == 1, "single-layer trunk"
    lyr = 0

    cond_t = jnp.pad(jnp.concatenate([c_ctx[None, :], c], 0).T, ((0, 0), (0, 7 - n_dec)))
    mod4 = _mod_call(cond_t, 1 + n_dec, w_mod[lyr], b_mod[lyr]).reshape(8, N_MOD, 1, d)

    in_w = w_in[lyr].astype(BF16)

    prm = jnp.pad(jnp.stack([a_log[lyr].reshape(-1), dt_bias[lyr].reshape(-1)]),
                  ((0, 6), (2 * DN_HEADS, LANES - 4 * DN_HEADS)))
    gn = gnorm_w[lyr].reshape(1, -1)

    wua = w_up_a[lyr].astype(BF16)
    wub = w_up_b[lyr].astype(BF16)
    wout_half = (0.5 * w_out[lyr]).astype(BF16)
    wr = jnp.pad(w_router[lyr], ((0, 0), (0, LANES - N_EXPERTS)))
    wr_hi = wr.astype(BF16)
    wr_lo = (wr - wr_hi.astype(F32)).astype(BF16)
    l1g, l1b = ln1_g[lyr].reshape(1, -1), ln1_b[lyr].reshape(1, -1)
    l2g, l2b = ln2_g[lyr].reshape(1, -1), ln2_b[lyr].reshape(1, -1)

    def front(x2d, pos, bsz, seq, mod_row, s0f, s0b):
        (qkv, z, ba, hy, gates), pre_conv = _inproj_call(x2d, pos, mod4, mod_row, in_w, conv_a_w[lyr],
                                                          conv_b_w[lyr], conv_b_b[lyr], seq, ACT_DT)
        reqs = DN_REQS_PER_STEP if seq == DN_CHUNK and bsz % DN_REQS_PER_STEP == 0 else 1
        o_a, s_f, s_b = _deltanet_call(qkv, z, ba, conv_a_w[lyr], prm, gn, s0f, s0b, bsz, seq, reqs,
                                       pre_conv, ACT_DT)
        fwd, inv = (jnp.asarray(m) for m in _dft_mats(seq))
        fwd_hi = fwd.astype(BF16)
        fwd_lo = (fwd - fwd_hi.astype(F32)).astype(BF16)
        filt_params = (hy_w1[lyr], hy_b1[lyr], hy_w2[lyr], hy_b2[lyr], hy_w3[lyr], hy_freq[lyr])
        o_h = _hyena_call(hy, pre_conv, conv_b_w[lyr], conv_b_b[lyr], filt_params, hy_bias[lyr],
                          fwd_hi, fwd_lo, inv.astype(BF16), bsz, seq, ACT_DT)
        x1, u2, probs = _mix_call(x2d, pos, o_a, o_h, gates, mod4, mod_row, wua, wub, wout_half, l1g, l1b,
                                   wr_hi, wr_lo)
        xs, slot, pt = _route_call(probs, u2, bsz, seq)
        return x1, xs, slot, pt, s_f, s_b

    row_ctx = lambda tok: 0
    row_dec = lambda tok: 1 + tok // l_dec

    xc = x_prompt.reshape(n_ctx * l_ctx, d)
    xd = x_sample.reshape(n_dec * l_dec, d)
    pos = _grid_pos_embed(l_dec // GRID_W, d)

    x1c, xsc, gc, ptc, s_f, s_b = front(xc, None, n_ctx, l_ctx, row_ctx, None, None)
    x1d, xsd, gd, ptd, _, _ = front(xd, pos, n_dec, l_dec, row_dec,
                                    state_delta_fwd[:, lyr], state_delta_bwd[:, lyr])
    yc, yd = _expert_call(xsc, xsd, w_e_gate[lyr], w_e_up[lyr], w_e_down[lyr])
    reqs_c = FINAL_CTX_REQS_PER_STEP if n_ctx % FINAL_CTX_REQS_PER_STEP == 0 else 1
    y_prompt = _final_call(x1c, yc, gc, ptc, mod4, row_ctx, l2g, l2b, n_ctx, l_ctx, reqs_c)
    y_sample = _final_call(x1d, yd, gd, ptd, mod4, row_dec, l2g, l2b, n_dec, l_dec, 1)

    return (y_prompt.reshape(n_ctx, l_ctx, d), y_sample.reshape(n_dec, l_dec, d),
            s_f[:, None], s_b[:, None])
```

```python
import functools
import math

import jax
import jax.numpy as jnp
import numpy as np
from jax import lax
from jax.experimental import pallas as pl
from jax.experimental.pallas import tpu as pltpu

F32 = jnp.float32
BF16 = jnp.bfloat16
HIGHEST = lax.Precision.HIGHEST

D_MODEL = 1024
DEPTH = 1
GRID_W = 64
DN_HEADS = 4
DN_DK = 128
DN_DV = 128
DN_WIDTH = DN_HEADS * DN_DV
HY_WIDTH = D_MODEL // 2
HY_EMB = 33
HY_FFN = 64
HY_TARGET = 1e-2
HY_FAST_DECAY = 0.3
HY_SLOW_DECAY = 1.5
N_EXPERTS = 16
EC_FACTOR = 2
N_MOD = 6
ALPHA = (2 * DEPTH) ** 0.25
LN_EPS = 1e-5
RMS_EPS = 1e-6
QKV_WIDTH = 2 * DN_HEADS * DN_DK + DN_WIDTH

LANES = 128
INPROJ_TILE = 1024
INPROJ_BLOCK_ROWS = 512
MIX_TILE = 1024
MIX_SUB_ROWS = 256
MIX_MIN_STEPS = 4
MOD_COL_BLOCK = 1024
MOD_BUFFERS = 3
DN_CHUNK = 256
INV_BASE = 8
DN_REQS_PER_STEP = 2
HY_ROWS_PER_STEP = 1024
FINAL_CTX_REQS_PER_STEP = 8
EXPERT_ROW_BLOCK = 512
VMEM_LIMIT = 56 * 1024 * 1024
MIN_NORMAL_F32_BITS = 0x00800000
ACT_DT = BF16
ROUTE_ROWS_PER_STEP = 1024
RANK_COUNT_MAX_SEQ = 256


def _cparams(n_axes):
    return pltpu.CompilerParams(dimension_semantics=("arbitrary",) * n_axes,
                                vmem_limit_bytes=VMEM_LIMIT)


def _bmm(a, b):
    return jnp.einsum("bmk,bkn->bmn", a.astype(BF16), b.astype(BF16), preferred_element_type=F32)


def _bmm_nt(a, b):
    return jnp.einsum("bmk,bnk->bmn", a.astype(BF16), b.astype(BF16), preferred_element_type=F32)


def _bmm_tn(a, b):
    return jnp.einsum("bkm,bkn->bmn", a.astype(BF16), b.astype(BF16), preferred_element_type=F32)


def _mm_f32(a, b):
    return jnp.dot(a, b, precision=HIGHEST, preferred_element_type=F32)


def _sigmoid(x):
    return 0.5 * jnp.tanh(0.5 * x) + 0.5


def _silu(x):
    h = 0.5 * x
    return h * (jnp.tanh(h) + 1.0)


def _softplus(x):
    return jnp.maximum(x, 0.0) + jnp.log1p(jnp.exp(-jnp.abs(x)))


def _ln(x):
    mu = jnp.mean(x, axis=-1, keepdims=True)
    xc = x - mu
    var = jnp.mean(xc * xc, axis=-1, keepdims=True)
    return xc * lax.rsqrt(var + LN_EPS)


def _l2norm(x, scale=1.0):
    return x * (lax.rsqrt(jnp.sum(x * x, axis=-1, keepdims=True) + 1e-6) * scale)


def _iota(shape, dim):
    return lax.broadcasted_iota(jnp.int32, shape, dim)


def _short_conv3(x, w, first, last):
    xm = jnp.where(first, 0.0, pltpu.roll(x, 1, 0))
    xp = jnp.where(last, 0.0, pltpu.roll(x, x.shape[0] - 1, 0))
    return w[0:1, :] * xm + w[1:2, :] * x + w[2:3, :] * xp


def _mod_kernel(n_rows, n_blocks, ct_ref, w_hbm, b_ref, o_ref, wbuf, sem):
    tn = MOD_COL_BLOCK

    def copy(j):
        slot = j % MOD_BUFFERS
        return pltpu.make_async_copy(w_hbm.at[:, pl.ds(j * tn, tn)], wbuf.at[slot], sem.at[slot])

    for j in range(min(MOD_BUFFERS - 1, n_blocks)):
        copy(j).start()
    s = _silu(ct_ref[...])
    o_ref[...] = jnp.zeros(o_ref.shape, F32)
    for j in range(n_blocks):
        if j + MOD_BUFFERS - 1 < n_blocks:
            copy(j + MOD_BUFFERS - 1).start()
        copy(j).wait()
        w = wbuf[j % MOD_BUFFERS]
        cols = slice(j * tn, (j + 1) * tn)
        for r in range(n_rows):
            o_ref[r:r + 1, cols] = jnp.sum(w * s[:, r:r + 1], axis=0, keepdims=True) + b_ref[:, cols]


def _mod_call(cond_t, n_rows, w_mod, b_mod):
    d = D_MODEL
    n_out = N_MOD * d
    n_blocks = n_out // MOD_COL_BLOCK
    return pl.pallas_call(
        functools.partial(_mod_kernel, n_rows, n_blocks),
        grid=(1,),
        in_specs=[pl.BlockSpec((d, 8), lambda i: (0, 0)),
                  pl.BlockSpec(memory_space=pl.ANY),
                  pl.BlockSpec((1, n_out), lambda i: (0, 0))],
        out_specs=pl.BlockSpec((8, n_out), lambda i: (0, 0)),
        out_shape=jax.ShapeDtypeStruct((8, n_out), F32),
        scratch_shapes=[pltpu.VMEM((MOD_BUFFERS, d, MOD_COL_BLOCK), F32),
                        pltpu.SemaphoreType.DMA((MOD_BUFFERS,))],
        compiler_params=_cparams(1),
        name="mod",
    )(cond_t, w_mod, b_mod.reshape(1, -1))


def _mod_spec(row_of_step):
    return pl.BlockSpec((1, N_MOD, 1, D_MODEL), lambda *idx: (row_of_step(*idx), 0, 0, 0))


def _inproj_kernel(has_pos, conv_seq, *refs):
    refs = list(refs)
    x_ref = refs.pop(0)
    pos_ref = refs.pop(0) if has_pos else None
    mod_ref, w_ref = refs.pop(0), refs.pop(0)
    if conv_seq:
        cwa_ref, cwb_ref, cbb_ref = refs.pop(0), refs.pop(0), refs.pop(0)
    qkv_o, z_o, ba_o, hy_o, g_o = refs
    sh1 = mod_ref[0, 0]
    sc1 = mod_ref[0, 1]
    o_z = QKV_WIDTH
    o_ba = o_z + DN_WIDTH
    n_ba = 4 * DN_HEADS
    o_g = n_ba + 3 * HY_WIDTH
    lanes = lambda j: slice(j * LANES, (j + 1) * LANES)
    tm = x_ref.shape[0]
    sub = min(tm, INPROJ_BLOCK_ROWS)
    blocks = [slice(r0, r0 + sub) for r0 in range(0, tm, sub)]

    if conv_seq:
        row = _iota((sub, LANES), 0) % conv_seq
        first = row == 0
        last = row == conv_seq - 1

    def project(rows):
        x = x_ref[rows, :]
        if has_pos:
            x = x + pos_ref[rows, :]
        u = (_ln(x) * (1.0 + sc1) + sh1).astype(BF16)
        return (jnp.dot(u, w_ref[:, :o_ba], preferred_element_type=F32),
                jnp.dot(u, w_ref[:, o_ba:], preferred_element_type=F32))

    def epilogue(rows, head, tail):
        z_o[rows, :] = head[:, o_z:].astype(z_o.dtype)
        ba_o[rows, :] = tail[:, :LANES]
        g_o[rows, :] = tail[:, o_g:].astype(g_o.dtype)
        hy = tail[:, n_ba:o_g]
        if not conv_seq:
            qkv_o[rows, :] = head[:, :o_z].astype(qkv_o.dtype)
            hy_o[rows, :] = hy.astype(hy_o.dtype)
            return
        for j in range(QKV_WIDTH // LANES):
            y = _silu(_short_conv3(head[:, lanes(j)], cwa_ref[:, lanes(j)], first, last))
            if j < 2 * DN_HEADS:
                y = _l2norm(y, DN_DK ** -0.5 if j < DN_HEADS else 1.0)
            qkv_o[rows, lanes(j)] = y.astype(qkv_o.dtype)
        nj = HY_WIDTH // LANES
        for j in range(nj):
            parts = [_short_conv3(hy[:, lanes(p * nj + j)], cwb_ref[:, lanes(p * nj + j)], first, last)
                     + cbb_ref[:, lanes(p * nj + j)] for p in range(3)]
            hy_o[rows, lanes(j)] = parts[0].astype(hy_o.dtype)
            hy_o[rows, lanes(nj + j)] = (parts[1] * parts[2]).astype(hy_o.dtype)

    pending = None
    for rows in blocks:
        cur = (rows, *project(rows))
        if pending is not None:
            epilogue(*pending)
        pending = cur
    epilogue(*pending)


def _inproj_call(x2d, pos, mod4, mod_row, w, conv_a_w, conv_b_w, conv_b_b, seq, act_dt):
    n, d = x2d.shape
    fuse = seq <= INPROJ_BLOCK_ROWS and INPROJ_TILE % seq == 0
    tm = min(INPROJ_TILE if fuse else INPROJ_BLOCK_ROWS, n)
    assert n % tm == 0 and (not fuse or tm % seq == 0)
    conv_seq = seq if fuse else None
    row = lambda i: (i, 0)
    const = lambda i: (0, 0)
    in_specs = [pl.BlockSpec((tm, d), row)]
    args = [x2d]
    if pos is not None:
        tiles = pos.shape[0] // tm
        in_specs.append(pl.BlockSpec((tm, d), lambda i: (i % tiles, 0)))
        args.append(pos)
    in_specs.append(_mod_spec(lambda i: mod_row(i * tm)))
    args.append(mod4)
    in_specs.append(pl.BlockSpec(w.shape, const, pipeline_mode=pl.Buffered(1)))
    args.append(w)
    if conv_seq:
        for a in (conv_a_w, conv_b_w, conv_b_b.reshape(1, -1)):
            in_specs.append(pl.BlockSpec(a.shape, const))
            args.append(a)
    hy_width = (2 if conv_seq else 3) * HY_WIDTH
    widths = (QKV_WIDTH, DN_WIDTH, LANES, hy_width, 2 * D_MODEL)
    dts = (act_dt, act_dt, F32, act_dt, act_dt)
    outs = pl.pallas_call(
        functools.partial(_inproj_kernel, pos is not None, conv_seq),
        grid=(n // tm,),
        in_specs=in_specs,
        out_specs=[pl.BlockSpec((tm, w), row) for w in widths],
        out_shape=[jax.ShapeDtypeStruct((n, w), dt) for w, dt in zip(widths, dts)],
        compiler_params=_cparams(1),
        name="inproj",
    )(*args)
    return outs, conv_seq is not None


def _inv_unit_tri(lm, ri, ci, block):
    c = lm.shape[-1]

    def same_block(h):
        sh = h.bit_length() - 1
        return (ri >> sh) == (ci >> sh)

    def unfold(f, h):
        return jnp.where(same_block(h), jnp.concatenate([f] * (c // h), axis=1), 0.0)

    base = INV_BASE
    n_nat = jnp.where(same_block(base), -lm, 0.0)
    n_fold = n_nat[:, 0:base, :]
    for i in range(1, c // base):
        n_fold = n_fold + n_nat[:, i * base:(i + 1) * base, :]
    eye_fold = jnp.where((_iota((base, c), 1) & (base - 1)) == _iota((base, c), 0), 1.0, 0.0)
    t = eye_fold + n_fold
    p_fold, p_nat = n_fold, n_nat
    span = 2
    while span < base:
        p_fold = _bmm(p_fold, p_nat)
        p_nat = unfold(p_fold, base)
        t = t + _bmm(t, p_nat)
        span *= 2
    s = base
    while s < block:
        sh = s.bit_length() - 1
        t_nat = unfold(t, s)
        even = ((_iota((s, c), 1) >> sh) & 1) == 0
        t2 = jnp.concatenate([jnp.where(even, t, 0.0), jnp.where(even, 0.0, t)], axis=1)
        lo = jnp.where(same_block(2 * s) & ~same_block(s), lm, 0.0)
        t = t2 - _bmm(_bmm(t2, lo), t_nat)
        s *= 2
    return t


def _dn_chunk(q, k, v, gc_col, gc_row, beta, e_col, e_rest, e_all, s_prev, same_chunk, ri, ci):
    c = DN_CHUNK
    nh = q.shape[0] // 2
    half = c // 2
    if same_chunk:
        gram = _bmm_nt(k[:nh], k[:nh])
        qk = _bmm_nt(q[:nh], k[:nh])
        gram = jnp.concatenate([gram, gram], 0)
        qk = jnp.concatenate([qk, qk], 0)
    else:
        gram = _bmm_nt(k, k)
        qk = _bmm_nt(q, k)
    diff = gc_col - gc_row
    incl = jnp.concatenate([jnp.broadcast_to(ri >= ci, (nh, c, c)), jnp.broadcast_to(ri <= ci, (nh, c, c))], 0)
    strict = incl & (ri != ci)
    decay = jnp.exp(jnp.where(incl, diff, -jnp.inf))
    lm = jnp.where(strict, beta * gram * decay, 0.0)
    a_intra = qk * decay
    t = _inv_unit_tri(lm, ri, ci, half)
    t1 = t[:, :, :half]
    t2 = t[:, :, half:]
    rhs = jnp.concatenate([v * beta, k * (beta * e_col)], axis=2)
    y1 = _bmm(t1, rhs[:, :half, :])
    y2 = _bmm(t2, rhs[:, half:, :])
    c_f = _bmm(lm[:nh, half:, :half], y1[:nh])
    c_b = _bmm(lm[nh:, :half, half:], y2[nh:])
    c_f = _bmm(t2[:nh], c_f)
    c_b = _bmm(t1[nh:], c_b)
    sol = jnp.concatenate([jnp.concatenate([y1[:nh], y2[:nh] - c_f], 1),
                           jnp.concatenate([y1[nh:] - c_b, y2[nh:]], 1)], 0)
    u = sol[:, :, :DN_DV]
    w = sol[:, :, DN_DV:]
    ks = k * e_rest
    if s_prev is None:
        v_new = u
        o = _bmm(a_intra, v_new)
        s_new = _bmm_tn(ks, v_new)
    else:
        v_new = u - _bmm(w, s_prev)
        o = _bmm(q * e_col, s_prev) + _bmm(a_intra, v_new)
        s_new = s_prev * e_all + _bmm_tn(ks, v_new)
    return o, s_new


def _deltanet_kernel(seq, reqs, zero_init, pre_conv, *refs):
    qkv_ref, cw_ref, z_ref, ba_ref, prm_ref, gn_ref = refs[:6]
    refs = refs[6:]
    if not zero_init:
        s0f_ref, s0b_ref = refs[:2]
        refs = refs[2:]
    o_ref, sf_ref, sb_ref, q_s, k_s, v_s, g_s, b_s, o_s, st_s = refs
    c = DN_CHUNK
    n = seq // c
    nh = DN_HEADS
    nf = reqs * nh
    lanes = lambda j: slice(j * LANES, (j + 1) * LANES)

    if not pre_conv:
        row = _iota((seq, LANES), 0)
        first_row = row == 0
        last_row = row == seq - 1
    for r in range(reqs):
        rows = slice(r * seq, (r + 1) * seq)

        def qkv_part(j):
            x = qkv_ref[rows, lanes(j)].astype(F32)
            return x if pre_conv else _silu(_short_conv3(x, cw_ref[:, lanes(j)], first_row, last_row))

        for h in range(nh):
            q = qkv_part(h)
            k = qkv_part(nh + h)
            q_s[r * nh + h] = q if pre_conv else _l2norm(q, DN_DK ** -0.5)
            k_s[r * nh + h] = k if pre_conv else _l2norm(k)
            v_s[r * nh + h] = qkv_part(2 * nh + h)

    ba = ba_ref[...]
    b_s[...] = _sigmoid(ba)
    g_s[...] = -jnp.exp(prm_ref[0:1, :]) * _softplus(ba + prm_ref[1:2, :])

    ri = _iota((c, c), 0)
    ci = _iota((c, c), 1)
    tri_l = jnp.where(ri >= ci, 1.0, 0.0).astype(F32)
    tri_u = jnp.where(ri <= ci, 1.0, 0.0).astype(F32)

    def chunk_pair(idx_f, idx_b, s_prev):
        def ds(idx, base=0):
            start = idx * c
            start = start if isinstance(start, int) else pl.multiple_of(start, c)
            return pl.ds(base + start, c)

        col_f, col_b, row_f, row_b, beta_f, beta_b = [], [], [], [], [], []
        for r in range(reqs):
            cs_f = _mm_f32(tri_l, g_s[ds(idx_f, r * seq), :])
            cs_b = _mm_f32(tri_u, g_s[ds(idx_b, r * seq), :])
            cst_f = cs_f.T
            cst_b = cs_b.T
            b_f = b_s[ds(idx_f, r * seq), :]
            b_b = b_s[ds(idx_b, r * seq), :]
            for h in range(nh):
                jf, jb = 2 * nh + h, 3 * nh + h
                col_f.append(cs_f[:, jf:jf + 1])
                col_b.append(cs_b[:, jb:jb + 1])
                row_f.append(cst_f[jf:jf + 1, :])
                row_b.append(cst_b[jb:jb + 1, :])
                beta_f.append(b_f[:, h:h + 1])
                beta_b.append(b_b[:, nh + h:nh + h + 1])
        gc_col = jnp.stack(col_f + col_b, 0)
        gc_row = jnp.stack(row_f + row_b, 0)
        beta = jnp.stack(beta_f + beta_b, 0)
        e_col = jnp.exp(gc_col)
        gl = jnp.concatenate([gc_col[:nf, c - 1:c, :], gc_col[nf:, 0:1, :]], 0)
        e_rest = jnp.exp(gl - gc_col)
        e_all = jnp.exp(gl)
        sl_f, sl_b = ds(idx_f), ds(idx_b)
        both = lambda ref: jnp.concatenate([ref[:, sl_f, :], ref[:, sl_b, :]], 0)
        o, s_new = _dn_chunk(both(q_s), both(k_s), both(v_s), gc_col, gc_row, beta, e_col, e_rest, e_all,
                             s_prev, n == 1, ri, ci)
        o_s[:nf, sl_f, :] = o[:nf]
        o_s[nf:, sl_b, :] = o[nf:]
        return s_new

    if not zero_init:
        s0 = jnp.concatenate([s0f_ref[r] for r in range(reqs)] + [s0b_ref[r] for r in range(reqs)], 0)
    if n == 1:
        s_fin = chunk_pair(0, 0, None if zero_init else s0)
    else:
        st_s[...] = jnp.zeros(st_s.shape, F32) if zero_init else s0

        def body(i, carry):
            st_s[...] = chunk_pair(i, n - 1 - i, st_s[...])
            return carry

        lax.fori_loop(0, n, body, 0)
        s_fin = st_s[...]
    for r in range(reqs):
        sf_ref[r] = s_fin[r * nh:(r + 1) * nh]
        sb_ref[r] = s_fin[nf + r * nh:nf + (r + 1) * nh]

    gn = gn_ref[...]
    for r in range(reqs):
        rows = slice(r * seq, (r + 1) * seq)
        for h in range(nh):
            o = o_s[r * nh + h] + o_s[nf + r * nh + h]
            o = o * lax.rsqrt(jnp.mean(o * o, axis=-1, keepdims=True) + RMS_EPS) * gn
            o_ref[rows, lanes(h)] = (o * _silu(z_ref[rows, lanes(h)].astype(F32))).astype(o_ref.dtype)


def _deltanet_call(qkv, z, ba, conv_w, prm, gn, s0f, s0b, bsz, seq, reqs, pre_conv, act_dt):
    nh = DN_HEADS
    zero_init = s0f is None
    rows = reqs * seq
    st = pl.BlockSpec((reqs, nh, DN_DK, DN_DV), lambda b: (b, 0, 0, 0))
    in_specs = [pl.BlockSpec((rows, QKV_WIDTH), lambda b: (b, 0)),
                pl.BlockSpec((3, QKV_WIDTH), lambda b: (0, 0)),
                pl.BlockSpec((rows, DN_WIDTH), lambda b: (b, 0)),
                pl.BlockSpec((rows, LANES), lambda b: (b, 0)),
                pl.BlockSpec((8, LANES), lambda b: (0, 0)),
                pl.BlockSpec((1, LANES), lambda b: (0, 0))]
    args = [qkv, conv_w, z, ba, prm, gn]
    if not zero_init:
        in_specs += [st, st]
        args += [s0f, s0b]
    vm = lambda shape: pltpu.VMEM(shape, F32)
    per_head = vm((reqs * nh, seq, LANES))
    return pl.pallas_call(
        functools.partial(_deltanet_kernel, seq, reqs, zero_init, pre_conv),
        grid=(bsz // reqs,),
        in_specs=in_specs,
        out_specs=[pl.BlockSpec((rows, DN_WIDTH), lambda b: (b, 0)), st, st],
        out_shape=[jax.ShapeDtypeStruct((bsz * seq, DN_WIDTH), act_dt),
                   jax.ShapeDtypeStruct((bsz, nh, DN_DK, DN_DV), F32),
                   jax.ShapeDtypeStruct((bsz, nh, DN_DK, DN_DV), F32)],
        scratch_shapes=[per_head, per_head, per_head, vm((rows, LANES)), vm((rows, LANES)),
                        vm((2 * reqs * nh, seq, LANES)), vm((2 * reqs * nh, DN_DK, DN_DV))],
        compiler_params=_cparams(1),
        name="deltanet",
    )(*args)


def _dft_mats(seq):
    n = 2 * seq
    f = np.arange(seq)[:, None]
    s = np.arange(seq)[None, :]
    ang = 2.0 * np.pi * ((f * s) % n) / n
    fwd = np.concatenate([np.cos(ang), -np.sin(ang)], axis=0)
    fwd[seq, :] = np.cos(np.pi * np.arange(seq))
    t = (np.arange(seq) + seq // 2)[:, None]
    ff = np.arange(seq)[None, :]
    ang2 = 2.0 * np.pi * ((t * ff) % n) / n
    inv_r = 2.0 * np.cos(ang2) / n
    inv_i = -2.0 * np.sin(ang2) / n
    inv_r[:, 0] = 1.0 / n
    inv_i[:, 0] = np.cos(np.pi * t[:, 0]) / n
    inv = np.concatenate([inv_r, inv_i], axis=1)
    return fwd.astype(np.float32), inv.astype(np.float32)


def _filter_feats(seq):
    t = np.linspace(0.0, 1.0, seq)[:, None]
    bands = (HY_EMB - 1) // 2
    ang = (2.0 * math.pi * np.arange(seq) / seq)[:, None] * np.linspace(1e-4, bands - 1, bands)[None, :]
    feats = np.concatenate([t, np.cos(ang), -np.sin(ang)], -1)
    deltas = np.abs(np.linspace(math.log(HY_TARGET) / HY_SLOW_DECAY, math.log(HY_TARGET) / HY_FAST_DECAY,
                                HY_WIDTH))
    offset = np.abs(np.arange(seq) - seq // 2) / (seq // 2)
    window = np.exp(-offset[:, None] * deltas[None, :])
    feats = np.pad(feats, ((0, 0), (0, LANES - HY_EMB)))
    return jnp.asarray(feats, dtype=F32), jnp.asarray(window, dtype=F32)


def _filter_spectrum(seq, feats_ref, win_ref, w1_ref, b1_ref, w2_ref, b2_ref, w3_ref, fr_ref, fwd_hi_ref,
                     fwd_lo_ref, ha_s, hb_s, hd_s):
    fr = fr_ref[...]
    hid = jnp.sin(fr * (_mm_f32(feats_ref[...], w1_ref[...]) + b1_ref[...]))
    hid = jnp.sin(fr * (_mm_f32(hid, w2_ref[...]) + b2_ref[...]))
    filt = _mm_f32(hid, w3_ref[...]) * win_ref[...]
    filt = filt / (jnp.sum(jnp.abs(filt), axis=0, keepdims=True) + 1e-6)
    filt_hi = filt.astype(BF16)
    filt_lo = (filt - filt_hi.astype(F32)).astype(BF16)
    fwd_hi = fwd_hi_ref[...]
    spec = (jnp.dot(fwd_hi, filt_hi, preferred_element_type=F32)
            + jnp.dot(fwd_hi, filt_lo, preferred_element_type=F32)
            + jnp.dot(fwd_lo_ref[...], filt_hi, preferred_element_type=F32))
    h_re = spec[:seq, :]
    h_im = spec[seq:, :]
    first = _iota(h_re.shape, 0) == 0
    ha_s[...] = h_re
    hb_s[...] = jnp.where(first, 0.0, h_im)
    hd_s[...] = jnp.where(first, h_im, h_re)


def _hyena_kernel(seq, reqs, pre_conv, *refs):
    refs = list(refs)
    n_tok = 2 if pre_conv else 9
    tok_refs, refs = refs[:n_tok], refs[n_tok:]
    filt_refs, refs = refs[:8], refs[8:]
    skip_ref, fwd_ref, fwd_lo_ref, inv_ref, o_ref, ha_s, hb_s, hd_s = refs

    @pl.when(pl.program_id(0) == 0)
    def _():
        _filter_spectrum(seq, *filt_refs, fwd_ref, fwd_lo_ref, ha_s, hb_s, hd_s)

    if pre_conv:
        x0_ref, uu_ref = tok_refs
    else:
        x0_ref, x1_ref, v_ref, c0_ref, c1_ref, c2_ref, b0_ref, b1_ref, b2_ref = tok_refs
        row = _iota((seq, HY_WIDTH), 0)
        first = row == 0
        last = row == seq - 1
    hb = hb_s[...]
    for r in range(reqs):
        rows = slice(r * seq, (r + 1) * seq)
        if pre_conv:
            x0 = x0_ref[rows, :].astype(F32)
            uu = uu_ref[rows, :].astype(F32)
        else:
            conv = lambda x_ref, w_ref, b_ref: (_short_conv3(x_ref[rows, :].astype(F32), w_ref[...], first, last)
                                                + b_ref[...])
            x0 = conv(x0_ref, c0_ref, b0_ref)
            uu = conv(x1_ref, c1_ref, b1_ref) * conv(v_ref, c2_ref, b2_ref)
        spec = jnp.dot(fwd_ref[...], uu.astype(BF16), preferred_element_type=F32)
        u_re = spec[:seq, :]
        u_im = spec[seq:, :]
        y_re = u_re * ha_s[...] - u_im * hb
        y_im = u_re * hb + u_im * hd_s[...]
        y = jnp.concatenate([y_re, y_im], axis=0).astype(BF16)
        cv = jnp.dot(inv_ref[...], y, preferred_element_type=F32)
        o_ref[rows, :] = (x0 * (cv + uu * skip_ref[...])).astype(o_ref.dtype)


def _hyena_call(hy, pre_conv, conv_w, conv_b, filt_params, skip, fwd_hi, fwd_lo, inv, bsz, seq, act_dt):
    w1, b1, w2, b2, w3, freq = filt_params
    cb = HY_WIDTH
    reqs = max(1, min(bsz, HY_ROWS_PER_STEP // seq))
    assert bsz % reqs == 0
    tok = lambda part: pl.BlockSpec((reqs * seq, cb), lambda b: (b, part))
    const = lambda a: pl.BlockSpec(a.shape, lambda b: (0,) * a.ndim, pipeline_mode=pl.Buffered(1))
    feats, window = _filter_feats(seq)
    w1p = jnp.pad(w1, ((0, LANES - HY_EMB), (0, 0)))
    filt_args = [feats, window, w1p, b1.reshape(1, -1), w2, b2.reshape(1, -1), w3, freq.reshape(1, -1)]
    tail_args = filt_args + [skip.reshape(1, -1), fwd_hi, fwd_lo, inv]
    tail_specs = [const(a) for a in tail_args]
    if pre_conv:
        in_specs = [tok(0), tok(1)] + tail_specs
        args = [hy, hy] + tail_args
    else:
        conv_b2 = conv_b.reshape(1, -1)
        part3 = lambda a, part: pl.BlockSpec((a.shape[0], cb), lambda b: (0, part))
        in_specs = ([tok(0), tok(1), tok(2)] + [part3(conv_w, p) for p in range(3)]
                    + [part3(conv_b2, p) for p in range(3)] + tail_specs)
        args = [hy, hy, hy, conv_w, conv_w, conv_w, conv_b2, conv_b2, conv_b2] + tail_args
    spec_buf = pltpu.VMEM((seq, HY_WIDTH), F32)
    return pl.pallas_call(
        functools.partial(_hyena_kernel, seq, reqs, pre_conv),
        grid=(bsz // reqs,),
        in_specs=in_specs,
        out_specs=tok(0),
        out_shape=jax.ShapeDtypeStruct((bsz * seq, HY_WIDTH), act_dt),
        scratch_shapes=[spec_buf, spec_buf, spec_buf],
        compiler_params=_cparams(1),
        name="hyena",
    )(*args)


def _mix_kernel(has_pos, *refs):
    refs = list(refs)
    x_ref = refs.pop(0)
    pos_ref = refs.pop(0) if has_pos else None
    (oa_ref, ob_ref, g_ref, mod_ref, wua, wub, wout_half, l1g, l1b, wr_hi, wr_lo, x1_o, u2_o, p_o) = refs
    d = D_MODEL
    tm = x_ref.shape[0]
    sub = min(tm, MIX_SUB_ROWS)
    blocks = [slice(r0, r0 + sub) for r0 in range(0, tm, sub)]
    g1 = mod_ref[0, 2]
    sh2 = mod_ref[0, 3]
    sc2 = mod_ref[0, 4]

    mixed = []
    for rows in blocks:
        th = jnp.tanh(0.5 * g_ref[rows, :].astype(F32))
        a = jnp.dot(oa_ref[rows, :], wua[...], preferred_element_type=F32)
        b = jnp.dot(ob_ref[rows, :], wub[...], preferred_element_type=F32)
        up2 = (th[:, :d] * a + a) + (th[:, d:] * b + b)
        mixed.append(jnp.dot(up2.astype(BF16), wout_half[...], preferred_element_type=F32))

    for rows, mx in zip(blocks, mixed):
        x = x_ref[rows, :]
        if has_pos:
            x = x + pos_ref[rows, :]
        x1 = _ln(ALPHA * x + g1 * mx) * l1g[...] + l1b[...]
        x1_o[rows, :] = x1.astype(x1_o.dtype)
        u2 = _ln(x1) * (1.0 + sc2) + sh2
        u2_hi = u2.astype(BF16)
        u2_o[rows, :] = u2_hi
        u2_lo = (u2 - u2_hi.astype(F32)).astype(BF16)
        logits = (jnp.dot(u2_hi, wr_hi[...], preferred_element_type=F32)
                  + jnp.dot(u2_hi, wr_lo[...], preferred_element_type=F32)
                  + jnp.dot(u2_lo, wr_hi[...], preferred_element_type=F32))
        lane = _iota(logits.shape, 1)
        logits = jnp.where(lane < N_EXPERTS, logits, -jnp.inf)
        m = jnp.max(logits, axis=-1, keepdims=True)
        e = jnp.exp(logits - m)
        p_o[rows, :] = e / jnp.sum(e, axis=-1, keepdims=True)


def _mix_call(x2d, pos, o_a, o_b, gates, mod4, mod_row, wua, wub, wout_half, l1g, l1b, wr_hi, wr_lo):
    n, d = x2d.shape
    tm = min(MIX_TILE, max(n // MIX_MIN_STEPS, MIX_SUB_ROWS))
    assert n % tm == 0
    row = lambda i: (i, 0)
    const = lambda i: (0, 0)
    in_specs = [pl.BlockSpec((tm, d), row)]
    args = [x2d]
    if pos is not None:
        tiles = pos.shape[0] // tm
        in_specs.append(pl.BlockSpec((tm, d), lambda i: (i % tiles, 0)))
        args.append(pos)
    in_specs += [pl.BlockSpec((tm, DN_WIDTH), row), pl.BlockSpec((tm, HY_WIDTH), row),
                 pl.BlockSpec((tm, 2 * d), row),
                 _mod_spec(lambda i: mod_row(i * tm))]
    args += [o_a, o_b, gates, mod4]
    for w in (wua, wub, wout_half, l1g, l1b, wr_hi, wr_lo):
        in_specs.append(pl.BlockSpec(w.shape, const, pipeline_mode=pl.Buffered(1)))
        args.append(w)
    return pl.pallas_call(
        functools.partial(_mix_kernel, pos is not None),
        grid=(n // tm,),
        in_specs=in_specs,
        out_specs=[pl.BlockSpec((tm, d), row), pl.BlockSpec((tm, d), row), pl.BlockSpec((tm, LANES), row)],
        out_shape=[jax.ShapeDtypeStruct((n, d), ACT_DT), jax.ShapeDtypeStruct((n, d), BF16),
                   jax.ShapeDtypeStruct((n, LANES), F32)],
        compiler_params=_cparams(1),
        name="mix",
    )(*args)


def _route_kernel(seq, cap, reqs, p_ref, u_ref, tri_ref, xs_ref, slot_ref, pt_ref, g_s):
    e_n = N_EXPERTS
    tri = tri_ref[...]
    jrow = _iota((cap, seq), 0).astype(F32)
    if seq <= RANK_COUNT_MAX_SEQ:
        earlier = _iota((seq, seq), 0) < _iota((seq, seq), 1)
    for r in range(reqs):
        rows = slice(r * seq, (r + 1) * seq)
        p = p_ref[rows, :]
        pt = p.T[:e_n, :]

        if seq <= RANK_COUNT_MAX_SEQ:
            ranks = []
            for e in range(e_n):
                pc = p[:, e:e + 1]
                pr = pt[e:e + 1, :]
                beats = (pc > pr) | (earlier & (pc == pr))
                ranks.append(jnp.sum(jnp.where(beats, 1.0, 0.0), axis=0, keepdims=True))
            sel = jnp.concatenate(ranks, axis=0) < cap
        else:
            def ok(cand):
                cnt = jnp.sum(jnp.where(pt >= pltpu.bitcast(cand, F32), 1.0, 0.0), axis=1, keepdims=True)
                return (cnt >= cap) & (cand >= MIN_NORMAL_F32_BITS)

            def search(i, cur):
                lo = 28 - 2 * i
                c1, c2, c3 = cur | (1 << lo), cur | (2 << lo), cur | (3 << lo)
                return jnp.where(ok(c3), c3, jnp.where(ok(c2), c2, jnp.where(ok(c1), c1, cur)))

            top = jnp.full((e_n, 1), 1 << 30, jnp.int32)
            cur = jnp.where(ok(top), top, 0)
            thr = pltpu.bitcast(lax.fori_loop(0, 15, search, cur), F32)
            gt = pt > thr
            eq = pt == thr
            n_gt = jnp.sum(jnp.where(gt, 1.0, 0.0), axis=1, keepdims=True)
            eq_rank = jnp.dot(jnp.where(eq, 1.0, 0.0).astype(BF16), tri, preferred_element_type=F32)
            sel = gt | (eq & (eq_rank < cap - n_gt))
        pos = jnp.dot(jnp.where(sel, 1.0, 0.0).astype(BF16), tri, preferred_element_type=F32)
        slot = jnp.where(sel, pos, -1.0)
        slot_ref[r] = slot
        pt_ref[r] = pt
        for e in range(e_n):
            g_s[r, e * cap:(e + 1) * cap, :] = jnp.where(slot[e:e + 1, :] == jrow, 1.0, 0.0).astype(BF16)
        xs = jnp.dot(g_s[r], u_ref[rows, :], preferred_element_type=F32)
        for e in range(e_n):
            xs_ref[e, r * cap:(r + 1) * cap, :] = xs[e * cap:(e + 1) * cap, :].astype(xs_ref.dtype)


def _route_call(probs, u2, bsz, seq):
    cap = EC_FACTOR * seq // N_EXPERTS
    reqs = max(1, min(bsz, ROUTE_ROWS_PER_STEP // seq))
    assert bsz % reqs == 0
    tri = jnp.asarray(np.triu(np.ones((seq, seq), np.float32), 1), dtype=BF16)
    per_req = pl.BlockSpec((reqs, N_EXPERTS, seq), lambda b: (b, 0, 0))
    return pl.pallas_call(
        functools.partial(_route_kernel, seq, cap, reqs),
        grid=(bsz // reqs,),
        in_specs=[pl.BlockSpec((reqs * seq, LANES), lambda b: (b, 0)),
                  pl.BlockSpec((reqs * seq, D_MODEL), lambda b: (b, 0)),
                  pl.BlockSpec((seq, seq), lambda b: (0, 0), pipeline_mode=pl.Buffered(1))],
        out_specs=[pl.BlockSpec((N_EXPERTS, reqs * cap, D_MODEL), lambda b: (0, b, 0)), per_req, per_req],
        out_shape=[jax.ShapeDtypeStruct((N_EXPERTS, bsz * cap, D_MODEL), BF16),
                   jax.ShapeDtypeStruct((bsz, N_EXPERTS, seq), F32),
                   jax.ShapeDtypeStruct((bsz, N_EXPERTS, seq), F32)],
        scratch_shapes=[pltpu.VMEM((reqs, N_EXPERTS * cap, seq), BF16)],
        compiler_params=_cparams(1),
        name="route",
    )(probs, u2, tri)


def _expert_kernel(xc_ref, xd_ref, wg_ref, wu_ref, wd_ref, yc_ref, yd_ref):
    wg = wg_ref[0].astype(BF16)
    wu = wu_ref[0].astype(BF16)
    wd = wd_ref[0].astype(BF16)
    for x_ref, y_ref in ((xc_ref, yc_ref), (xd_ref, yd_ref)):
        rows = x_ref.shape[1]
        rb = min(rows, EXPERT_ROW_BLOCK)
        for r0 in range(0, rows, rb):
            x = x_ref[0, r0:r0 + rb, :]
            hg = jnp.dot(x, wg, preferred_element_type=F32)
            hu = jnp.dot(x, wu, preferred_element_type=F32)
            y = jnp.dot((_silu(hg) * hu).astype(BF16), wd, preferred_element_type=F32)
            y_ref[0, r0:r0 + rb, :] = y.astype(y_ref.dtype)


def _expert_call(xs_c, xs_d, w_gate, w_up, w_down):
    e_n, rc, d = xs_c.shape
    rd = xs_d.shape[1]
    ff = w_gate.shape[2]
    return pl.pallas_call(
        _expert_kernel,
        grid=(e_n,),
        in_specs=[pl.BlockSpec((1, rc, d), lambda e: (e, 0, 0)),
                  pl.BlockSpec((1, rd, d), lambda e: (e, 0, 0)),
                  pl.BlockSpec((1, d, ff), lambda e: (e, 0, 0)),
                  pl.BlockSpec((1, d, ff), lambda e: (e, 0, 0)),
                  pl.BlockSpec((1, ff, d), lambda e: (e, 0, 0))],
        out_specs=[pl.BlockSpec((1, rc, d), lambda e: (e, 0, 0)),
                   pl.BlockSpec((1, rd, d), lambda e: (e, 0, 0))],
        out_shape=[jax.ShapeDtypeStruct((e_n, rc, d), BF16), jax.ShapeDtypeStruct((e_n, rd, d), BF16)],
        compiler_params=_cparams(1),
        name="experts",
    )(xs_c, xs_d, w_gate, w_up, w_down)


def _final_kernel(seq, cap, reqs, x1_ref, y_ref, slot_ref, pt_ref, mod_ref, l2g, l2b, o_ref, w_s):
    jrow = _iota((cap, seq), 0).astype(F32)
    g2 = mod_ref[0, 5]
    for r in range(reqs):
        slot = slot_ref[r]
        pt = pt_ref[r]
        for e in range(N_EXPERTS):
            w_s[r, e * cap:(e + 1) * cap, :] = jnp.where(slot[e:e + 1, :] == jrow, pt[e:e + 1, :],
                                                         0.0).astype(BF16)
        y = y_ref[:, r * cap:(r + 1) * cap, :].reshape(N_EXPERTS * cap, y_ref.shape[-1])
        ffn = lax.dot_general(w_s[r], y, (((0,), (0,)), ((), ())), preferred_element_type=F32)
        rows = slice(r * seq, (r + 1) * seq)
        o_ref[rows, :] = _ln(ALPHA * x1_ref[rows, :].astype(F32) + g2 * ffn) * l2g[...] + l2b[...]


def _final_call(x1, ys, slot, pt, mod4, mod_row, l2g, l2b, bsz, seq, reqs):
    cap = EC_FACTOR * seq // N_EXPERTS
    d = D_MODEL
    per_req = pl.BlockSpec((reqs, N_EXPERTS, seq), lambda b: (b, 0, 0))
    return pl.pallas_call(
        functools.partial(_final_kernel, seq, cap, reqs),
        grid=(bsz // reqs,),
        in_specs=[pl.BlockSpec((reqs * seq, d), lambda b: (b, 0)),
                  pl.BlockSpec((N_EXPERTS, reqs * cap, d), lambda b: (0, b, 0)),
                  per_req, per_req,
                  _mod_spec(lambda b: mod_row(b * reqs * seq)),
                  pl.BlockSpec((1, d), lambda b: (0, 0)),
                  pl.BlockSpec((1, d), lambda b: (0, 0))],
        out_specs=pl.BlockSpec((reqs * seq, d), lambda b: (b, 0)),
        out_shape=jax.ShapeDtypeStruct((bsz * seq, d), F32),
        scratch_shapes=[pltpu.VMEM((reqs, N_EXPERTS * cap, seq), BF16)],
        compiler_params=_cparams(1),
        name="final",
    )(x1, ys, slot, pt, mod4, l2g, l2b)


def _grid_pos_embed(rows, dim):
    r = np.repeat(np.arange(rows), GRID_W)
    col = np.tile(np.arange(GRID_W), rows)
    quarter = dim // 4
    omega = 1.0 / (10000.0 ** (np.arange(quarter) / quarter))

    def enc(p):
        ang = p[:, None] * omega[None, :]
        return np.concatenate([np.sin(ang), np.cos(ang)], -1)

    return jnp.asarray(np.concatenate([enc(r), enc(col)], -1), dtype=F32)


def kernel(x_prompt, x_sample, state_delta_fwd, state_delta_bwd, c, c_ctx, w_mod, b_mod, w_in, conv_a_w, a_log, dt_bias, gnorm_w, conv_b_w, conv_b_b, hy_w1, hy_b1, hy_w2, hy_b2, hy_w3, hy_freq, hy_bias, w_up_a, w_up_b, w_out, ln1_g, ln1_b, w_router, w_e_gate, w_e_up, w_e_down, ln2_g, ln2_b):
    d = D_MODEL
    n_ctx, l_ctx, _ = x_prompt.shape
    n_dec, l_dec, _ = x_sample.shape
    assert w_mod.shape[0] == DEPTH == 1, "single-layer trunk"
    lyr = 0

    cond_t = jnp.pad(jnp.concatenate([c_ctx[None, :], c], 0).T, ((0, 0), (0, 7 - n_dec)))
    mod4 = _mod_call(cond_t, 1 + n_dec, w_mod[lyr], b_mod[lyr]).reshape(8, N_MOD, 1, d)

    in_w = w_in[lyr].astype(BF16)

    prm = jnp.pad(jnp.stack([a_log[lyr].reshape(-1), dt_bias[lyr].reshape(-1)]),
                  ((0, 6), (2 * DN_HEADS, LANES - 4 * DN_HEADS)))
    gn = gnorm_w[lyr].reshape(1, -1)

    wua = w_up_a[lyr].astype(BF16)
    wub = w_up_b[lyr].astype(BF16)
    wout_half = (0.5 * w_out[lyr]).astype(BF16)
    wr = jnp.pad(w_router[lyr], ((0, 0), (0, LANES - N_EXPERTS)))
    wr_hi = wr.astype(BF16)
    wr_lo = (wr - wr_hi.astype(F32)).astype(BF16)
    l1g, l1b = ln1_g[lyr].reshape(1, -1), ln1_b[lyr].reshape(1, -1)
    l2g, l2b = ln2_g[lyr].reshape(1, -1), ln2_b[lyr].reshape(1, -1)

    def front(x2d, pos, bsz, seq, mod_row, s0f, s0b):
        (qkv, z, ba, hy, gates), pre_conv = _inproj_call(x2d, pos, mod4, mod_row, in_w, conv_a_w[lyr],
                                                          conv_b_w[lyr], conv_b_b[lyr], seq, ACT_DT)
        reqs = DN_REQS_PER_STEP if seq == DN_CHUNK and bsz % DN_REQS_PER_STEP == 0 else 1
        o_a, s_f, s_b = _deltanet_call(qkv, z, ba, conv_a_w[lyr], prm, gn, s0f, s0b, bsz, seq, reqs,
                                       pre_conv, ACT_DT)
        fwd, inv = (jnp.asarray(m) for m in _dft_mats(seq))
        fwd_hi = fwd.astype(BF16)
        fwd_lo = (fwd - fwd_hi.astype(F32)).astype(BF16)
        filt_params = (hy_w1[lyr], hy_b1[lyr], hy_w2[lyr], hy_b2[lyr], hy_w3[lyr], hy_freq[lyr])
        o_h = _hyena_call(hy, pre_conv, conv_b_w[lyr], conv_b_b[lyr], filt_params, hy_bias[lyr],
                          fwd_hi, fwd_lo, inv.astype(BF16), bsz, seq, ACT_DT)
        x1, u2, probs = _mix_call(x2d, pos, o_a, o_h, gates, mod4, mod_row, wua, wub, wout_half, l1g, l1b,
                                   wr_hi, wr_lo)
        xs, slot, pt = _route_call(probs, u2, bsz, seq)
        return x1, xs, slot, pt, s_f, s_b

    row_ctx = lambda tok: 0
    row_dec = lambda tok: 1 + tok // l_dec

    xc = x_prompt.reshape(n_ctx * l_ctx, d)
    xd = x_sample.reshape(n_dec * l_dec, d)
    pos = _grid_pos_embed(l_dec // GRID_W, d)

    x1c, xsc, gc, ptc, s_f, s_b = front(xc, None, n_ctx, l_ctx, row_ctx, None, None)
    x1d, xsd, gd, ptd, _, _ = front(xd, pos, n_dec, l_dec, row_dec,
                                    state_delta_fwd[:, lyr], state_delta_bwd[:, lyr])
    yc, yd = _expert_call(xsc, xsd, w_e_gate[lyr], w_e_up[lyr], w_e_down[lyr])
    reqs_c = FINAL_CTX_REQS_PER_STEP if n_ctx % FINAL_CTX_REQS_PER_STEP == 0 else 1
    y_prompt = _final_call(x1c, yc, gc, ptc, mod4, row_ctx, l2g, l2b, n_ctx, l_ctx, reqs_c)
    y_sample = _final_call(x1d, yd, gd, ptd, mod4, row_dec, l2g, l2b, n_dec, l_dec, 1)

    return (y_prompt.reshape(n_ctx, l_ctx, d), y_sample.reshape(n_dec, l_dec, d),
            s_f[:, None], s_b[:, None])
```
